```python
import math
import jax, jax.numpy as jnp
from jax import lax
import numpy as np

D_MODEL = 1024
BATCH = 8
SEQ = 8192
DEPTH = 2

N_MIXERS = 2
N_A_LAYERS = (DEPTH + 1) // 2
N_B_LAYERS = DEPTH // 2

CHUNK = 128
GATE_WIDTH = D_MODEL
GATE_GROUPS = 8
GATE_GROUP_DIM = GATE_WIDTH // GATE_GROUPS

WINDOW_DILATIONS = ((128, 1), (512, 4), (2048, 16))
N_DIL_GROUPS = len(WINDOW_DILATIONS)
ATT_HEADS = 8
HEAD_DIM = 64
ATT_WIDTH = ATT_HEADS * HEAD_DIM

N_BUCKETS = 32
MAX_EXACT = N_BUCKETS // 2
REL_MAX_DISTANCE = max(w for w, _ in WINDOW_DILATIONS)

D_FF = 4 * D_MODEL

EPS = 1e-6
NEG_INF = -1e30

kernel_name = "interleaved_gmlp_dilated_attention_trunk"


def _rms_norm(x, g):
    xf = x.astype(jnp.float32)
    y = xf * lax.rsqrt(jnp.mean(xf * xf, axis=-1, keepdims=True) + EPS)
    return (y * g.astype(jnp.float32)).astype(x.dtype)


def _layer_norm(x, g, b):
    xf = x.astype(jnp.float32)
    mu = jnp.mean(xf, axis=-1, keepdims=True)
    xc = xf - mu
    y = xc * lax.rsqrt(jnp.mean(xc * xc, axis=-1, keepdims=True) + EPS)
    return (y * g.astype(jnp.float32) + b.astype(jnp.float32)).astype(x.dtype)


def _t5_bucket(distance):
    small = distance < MAX_EXACT
    nf = jnp.maximum(distance, 1).astype(jnp.float32)
    large = MAX_EXACT + (jnp.log(nf / MAX_EXACT) / math.log(REL_MAX_DISTANCE / MAX_EXACT)
                         * (N_BUCKETS - MAX_EXACT)).astype(jnp.int32)
    large = jnp.minimum(large, N_BUCKETS - 1)
    return jnp.where(small, distance, large)


def _chunk_gating_mixer(h, w_in, ln_g, ln_b, w_s, b_s, w_out):
    B_, S_, _ = h.shape
    uv = jax.nn.gelu(h @ w_in, approximate=False)
    u, v = jnp.split(uv, 2, axis=-1)
    v = _layer_norm(v, ln_g, ln_b)
    nc = S_ // CHUNK
    v = v.reshape(B_, nc, CHUNK, GATE_GROUPS, GATE_GROUP_DIM)
    causal = jnp.tril(jnp.ones((CHUNK, CHUNK), dtype=bool))
    w = jnp.where(causal[None], w_s, 0.0)
    mixed = jnp.einsum('gts,bnsgc->bntgc', w, v) + b_s.T[None, None, :, :, None]
    gate = mixed.reshape(B_, S_, GATE_WIDTH)
    return (u * gate) @ w_out


def _dilated_group(q, k, v, bias_table, window, dilation):
    B_, S_, H, hd = q.shape
    blk = window // dilation
    span = blk * dilation
    Sp = -(-S_ // span) * span
    nb = Sp // span

    def split(t):
        t = jnp.pad(t, ((0, 0), (0, Sp - S_), (0, 0), (0, 0)))
        return t.reshape(B_, nb, blk, dilation, H, hd)

    def with_prev(t):
        prev = jnp.pad(t, ((0, 0), (1, 0), (0, 0), (0, 0), (0, 0), (0, 0)))[:, :-1]
        return jnp.concatenate([prev, t], axis=2)

    qb = split(q)
    kc = with_prev(split(k))
    vc = with_prev(split(v))

    s = jnp.einsum('bnqrhc,bnkrhc->bnrhqk', qb, kc) * (HEAD_DIM ** -0.5)
    rel = blk + jnp.arange(blk)[:, None] - jnp.arange(2 * blk)[None, :]
    band = (rel >= 0) & (rel <= blk)
    bucket = _t5_bucket(jnp.clip(rel, 0, blk) * dilation)
    bias = jnp.transpose(bias_table[bucket], (2, 0, 1))
    first = (jnp.arange(nb)[:, None, None] == 0) & (jnp.arange(2 * blk)[None, None, :] < blk)
    valid = band[None] & ~first
    logits = jnp.where(valid[None, :, None, None], s + bias[None, None, None], NEG_INF)

    m = jnp.max(logits, axis=-1)
    p = jnp.exp(logits - m[..., None])
    den = jnp.sum(p, axis=-1)
    num = jnp.einsum('bnrhqk,bnkrhc->bnqrhc', p, vc)

    num = num.reshape(B_, Sp, H, hd)[:, :S_]
    den = jnp.transpose(den, (0, 1, 4, 2, 3)).reshape(B_, Sp, H)[:, :S_]
    m = jnp.transpose(m, (0, 1, 4, 2, 3)).reshape(B_, Sp, H)[:, :S_]
    return num, den, m


def _dilated_attention_mixer(h, w_qkv, w_out, rel_bias):
    B_, S_, _ = h.shape
    qkv = (h @ w_qkv).astype(jnp.float32).reshape(B_, S_, 3, N_DIL_GROUPS, ATT_HEADS, HEAD_DIM)
    rb = rel_bias.astype(jnp.float32)
    nums, dens, maxs = [], [], []
    for g, (window, dil) in enumerate(WINDOW_DILATIONS):
        n_, d_, m_ = _dilated_group(qkv[:, :, 0, g], qkv[:, :, 1, g], qkv[:, :, 2, g],
                                    rb[:, g * ATT_HEADS:(g + 1) * ATT_HEADS], window, dil)
        nums.append(n_)
        dens.append(d_)
        maxs.append(m_)
    m_all = jnp.max(jnp.stack(maxs), axis=0)
    scales = [jnp.exp(m_ - m_all) for m_ in maxs]
    num_sum = sum(n_ * c[..., None] for n_, c in zip(nums, scales))
    den_sum = sum(d_ * c for d_, c in zip(dens, scales))
    o = (num_sum / den_sum[..., None]).astype(h.dtype).reshape(B_, S_, ATT_WIDTH)
    return o @ w_out


def _fwd_setup_inputs(seed: int = 0) -> dict:
    key = jax.random.key(seed)
    ks = jax.random.split(key, 16)
    f32 = jnp.float32
    nrm = lambda k, shape, s: jax.random.normal(k, shape, f32) * s
    qkv_cols = 3 * N_DIL_GROUPS * ATT_HEADS * HEAD_DIM
    return {
        "x": jax.random.normal(ks[0], (BATCH, SEQ, D_MODEL), f32),
        "mix_norm_g": 1.0 + nrm(ks[1], (DEPTH, D_MODEL), 0.05),
        "mlp_norm_g": 1.0 + nrm(ks[2], (DEPTH, D_MODEL), 0.05),
        "final_norm_g": 1.0 + nrm(ks[3], (D_MODEL,), 0.05),
        "a_w_in": nrm(ks[4], (N_A_LAYERS, D_MODEL, 2 * GATE_WIDTH), D_MODEL ** -0.5),
        "a_ln_g": 1.0 + nrm(ks[5], (N_A_LAYERS, GATE_WIDTH), 0.05),
        "a_ln_b": nrm(ks[6], (N_A_LAYERS, GATE_WIDTH), 0.02),
        "a_w_s": nrm(ks[7], (N_A_LAYERS, GATE_GROUPS, CHUNK, CHUNK), CHUNK ** -0.5),
        "a_b_s": 1.0 + nrm(ks[8], (N_A_LAYERS, GATE_GROUPS, CHUNK), 0.1),
        "a_w_out": nrm(ks[9], (N_A_LAYERS, GATE_WIDTH, D_MODEL), GATE_WIDTH ** -0.5),
        "b_w_qkv": nrm(ks[10], (N_B_LAYERS, D_MODEL, qkv_cols), D_MODEL ** -0.5),
        "b_w_out": nrm(ks[11], (N_B_LAYERS, ATT_WIDTH, D_MODEL), ATT_WIDTH ** -0.5),
        "rel_bias": nrm(ks[12], (N_BUCKETS, N_DIL_GROUPS * ATT_HEADS), 0.5),
        "w_up": nrm(ks[13], (DEPTH, D_MODEL, D_FF), D_MODEL ** -0.5),
        "w_down": nrm(ks[14], (DEPTH, D_FF, D_MODEL), D_FF ** -0.5),
    }


def _fwd_reference(x, mix_norm_g, mlp_norm_g, final_norm_g, a_w_in, a_ln_g, a_ln_b, a_w_s, a_b_s,
              a_w_out, b_w_qkv, b_w_out, rel_bias, w_up, w_down):
    h = x
    for layer in range(DEPTH):
        y = _rms_norm(h, mix_norm_g[layer])
        j = layer // N_MIXERS
        if layer % N_MIXERS == 0:
            y = _chunk_gating_mixer(y, a_w_in[j], a_ln_g[j], a_ln_b[j], a_w_s[j], a_b_s[j], a_w_out[j])
        else:
            y = _dilated_attention_mixer(y, b_w_qkv[j], b_w_out[j], rel_bias)
        h = h + y
        y = _rms_norm(h, mlp_norm_g[layer])
        h = h + jnp.square(jax.nn.relu(y @ w_up[layer])) @ w_down[layer]
    return _rms_norm(h, final_norm_g)


import jax as _jax
import jax.numpy as _jnp

TWIN_FORMAT = 'train_step'
FWD_PARAMS = ['x', 'mix_norm_g', 'mlp_norm_g', 'final_norm_g', 'a_w_in', 'a_ln_g', 'a_ln_b', 'a_w_s', 'a_b_s', 'a_w_out', 'b_w_qkv', 'b_w_out', 'rel_bias', 'w_up', 'w_down']
TWIN_WEIGHTS = ['mix_norm_g', 'mlp_norm_g', 'final_norm_g', 'a_w_in', 'a_ln_g', 'a_ln_b', 'a_w_s', 'a_b_s', 'a_w_out', 'b_w_qkv', 'b_w_out', 'rel_bias', 'w_up', 'w_down']
TWIN_DIFF_INPUT = 'x'
TWIN_INPUTS = ['x', 'mix_norm_g', 'mlp_norm_g', 'final_norm_g', 'a_w_in', 'a_ln_g', 'a_ln_b', 'a_w_s', 'a_b_s', 'a_w_out', 'b_w_qkv', 'b_w_out', 'rel_bias', 'w_up', 'w_down', 'loss_target', 'm_mix_norm_g', 'm_mlp_norm_g', 'm_final_norm_g', 'm_a_w_in', 'm_a_ln_g', 'm_a_ln_b', 'm_a_w_s', 'm_a_b_s', 'm_a_w_out', 'm_b_w_qkv', 'm_b_w_out', 'm_rel_bias', 'm_w_up', 'm_w_down', 'v_mix_norm_g', 'v_mlp_norm_g', 'v_final_norm_g', 'v_a_w_in', 'v_a_ln_g', 'v_a_ln_b', 'v_a_w_s', 'v_a_b_s', 'v_a_w_out', 'v_b_w_qkv', 'v_b_w_out', 'v_rel_bias', 'v_w_up', 'v_w_down']
TWIN_OUTPUTS = ['loss', 'grad_x', 'grad_mix_norm_g', 'grad_mlp_norm_g', 'grad_final_norm_g', 'grad_a_w_in', 'grad_a_ln_g', 'grad_a_ln_b', 'grad_a_w_s', 'grad_a_b_s', 'grad_a_w_out', 'grad_b_w_qkv', 'grad_b_w_out', 'grad_rel_bias', 'grad_w_up', 'grad_w_down', 'delta_mix_norm_g', 'delta_mlp_norm_g', 'delta_final_norm_g', 'delta_a_w_in', 'delta_a_ln_g', 'delta_a_ln_b', 'delta_a_w_s', 'delta_a_b_s', 'delta_a_w_out', 'delta_b_w_qkv', 'delta_b_w_out', 'delta_rel_bias', 'delta_w_up', 'delta_w_down', 'new_m_mix_norm_g', 'new_m_mlp_norm_g', 'new_m_final_norm_g', 'new_m_a_w_in', 'new_m_a_ln_g', 'new_m_a_ln_b', 'new_m_a_w_s', 'new_m_a_b_s', 'new_m_a_w_out', 'new_m_b_w_qkv', 'new_m_b_w_out', 'new_m_rel_bias', 'new_m_w_up', 'new_m_w_down', 'new_v_mix_norm_g', 'new_v_mlp_norm_g', 'new_v_final_norm_g', 'new_v_a_w_in', 'new_v_a_ln_g', 'new_v_a_ln_b', 'new_v_a_w_s', 'new_v_a_b_s', 'new_v_a_w_out', 'new_v_b_w_qkv', 'new_v_b_w_out', 'new_v_rel_bias', 'new_v_w_up', 'new_v_w_down']
TWIN_LEAF_KINDS = {'loss': 'loss', 'grad_x': 'grad_x', 'grad_mix_norm_g': 'grad_w', 'grad_mlp_norm_g': 'grad_w', 'grad_final_norm_g': 'grad_w', 'grad_a_w_in': 'grad_w', 'grad_a_ln_g': 'grad_w', 'grad_a_ln_b': 'grad_w', 'grad_a_w_s': 'grad_w', 'grad_a_b_s': 'grad_w', 'grad_a_w_out': 'grad_w', 'grad_b_w_qkv': 'grad_w', 'grad_b_w_out': 'grad_w', 'grad_rel_bias': 'grad_w', 'grad_w_up': 'grad_w', 'grad_w_down': 'grad_w', 'delta_mix_norm_g': 'delta_w', 'delta_mlp_norm_g': 'delta_w', 'delta_final_norm_g': 'delta_w', 'delta_a_w_in': 'delta_w', 'delta_a_ln_g': 'delta_w', 'delta_a_ln_b': 'delta_w', 'delta_a_w_s': 'delta_w', 'delta_a_b_s': 'delta_w', 'delta_a_w_out': 'delta_w', 'delta_b_w_qkv': 'delta_w', 'delta_b_w_out': 'delta_w', 'delta_rel_bias': 'delta_w', 'delta_w_up': 'delta_w', 'delta_w_down': 'delta_w', 'new_m_mix_norm_g': 'new_m', 'new_m_mlp_norm_g': 'new_m', 'new_m_final_norm_g': 'new_m', 'new_m_a_w_in': 'new_m', 'new_m_a_ln_g': 'new_m', 'new_m_a_ln_b': 'new_m', 'new_m_a_w_s': 'new_m', 'new_m_a_b_s': 'new_m', 'new_m_a_w_out': 'new_m', 'new_m_b_w_qkv': 'new_m', 'new_m_b_w_out': 'new_m', 'new_m_rel_bias': 'new_m', 'new_m_w_up': 'new_m', 'new_m_w_down': 'new_m', 'new_v_mix_norm_g': 'new_v', 'new_v_mlp_norm_g': 'new_v', 'new_v_final_norm_g': 'new_v', 'new_v_a_w_in': 'new_v', 'new_v_a_ln_g': 'new_v', 'new_v_a_ln_b': 'new_v', 'new_v_a_w_s': 'new_v', 'new_v_a_b_s': 'new_v', 'new_v_a_w_out': 'new_v', 'new_v_b_w_qkv': 'new_v', 'new_v_b_w_out': 'new_v', 'new_v_rel_bias': 'new_v', 'new_v_w_up': 'new_v', 'new_v_w_down': 'new_v'}


def _forward(args):
    return _fwd_reference(*[args[k] for k in FWD_PARAMS])


def _output_shape():
    def fwd():
        inp = _fwd_setup_inputs(0)
        return _fwd_reference(*[inp[k] for k in FWD_PARAMS])
    out = _jax.eval_shape(fwd)
    return out.shape, out.dtype

N_MICROBATCH = 1
ADAM_LR = 0.001
ADAM_B1 = 0.9
ADAM_B2 = 0.999
ADAM_EPS = 1e-08
ADAM_WD = 0.01
ADAM_STEP = 10
PER_EXAMPLE_BATCH_AXIS = {'x': 0, 'loss_target': 0}
SHARED_INPUTS = []
_WEIGHT_DTYPES = {'mix_norm_g': _jnp.float32, 'mlp_norm_g': _jnp.float32, 'final_norm_g': _jnp.float32, 'a_w_in': _jnp.float32, 'a_ln_g': _jnp.float32, 'a_ln_b': _jnp.float32, 'a_w_s': _jnp.float32, 'a_b_s': _jnp.float32, 'a_w_out': _jnp.float32, 'b_w_qkv': _jnp.float32, 'b_w_out': _jnp.float32, 'rel_bias': _jnp.float32, 'w_up': _jnp.float32, 'w_down': _jnp.float32}
MOMENT_SCALE = {'mix_norm_g': 1.714717e-01, 'mlp_norm_g': 2.533815e-01, 'final_norm_g': 6.530859e+01, 'a_w_in': 1.560415e-01, 'a_ln_g': 1.025462e-01, 'a_ln_b': 1.096715e-01, 'a_w_s': 1.015979e-01, 'a_b_s': 1.538364e-01, 'a_w_out': 3.113728e-01, 'b_w_qkv': 7.566886e-02, 'b_w_out': 1.746041e-01, 'rel_bias': 4.060330e-02, 'w_up': 1.292449e-01, 'w_down': 4.898856e-01}


def _to_microbatches(a, axis):
    t = _jnp.moveaxis(a, axis, 0)
    t = t.reshape((N_MICROBATCH, t.shape[0] // N_MICROBATCH) + t.shape[1:])
    return _jnp.moveaxis(t, 1, axis + 1)


def setup_inputs(seed: int = 0) -> dict:
    inp = _fwd_setup_inputs(seed)
    key = _jax.random.fold_in(_jax.random.key(seed), 7919)
    shape, _ = _output_shape()
    out = dict(inp)
    out["loss_target"] = _jax.random.normal(_jax.random.fold_in(key, 0), shape, _jnp.float32)
    for i, name in enumerate(TWIN_WEIGHTS):
        w = inp[name].astype(_jnp.float32)
        if MOMENT_SCALE is None:
            s = _jnp.sqrt(_jnp.mean(_jnp.square(w)) + 1e-30)
        else:
            s = MOMENT_SCALE[name]
        km, kv = _jax.random.split(_jax.random.fold_in(key, i + 1))
        out[name] = w
        out["m_" + name] = s * _jax.random.normal(km, w.shape, _jnp.float32)
        out["v_" + name] = (s * s) * _jax.random.uniform(kv, w.shape, _jnp.float32, 0.5, 1.5)
    if N_MICROBATCH > 1:
        for name, axis in PER_EXAMPLE_BATCH_AXIS.items():
            out[name] = _to_microbatches(out[name], axis)
    return {'x': out['x'], 'mix_norm_g': out['mix_norm_g'], 'mlp_norm_g': out['mlp_norm_g'], 'final_norm_g': out['final_norm_g'], 'a_w_in': out['a_w_in'], 'a_ln_g': out['a_ln_g'], 'a_ln_b': out['a_ln_b'], 'a_w_s': out['a_w_s'], 'a_b_s': out['a_b_s'], 'a_w_out': out['a_w_out'], 'b_w_qkv': out['b_w_qkv'], 'b_w_out': out['b_w_out'], 'rel_bias': out['rel_bias'], 'w_up': out['w_up'], 'w_down': out['w_down'], 'loss_target': out['loss_target'], 'm_mix_norm_g': out['m_mix_norm_g'], 'm_mlp_norm_g': out['m_mlp_norm_g'], 'm_final_norm_g': out['m_final_norm_g'], 'm_a_w_in': out['m_a_w_in'], 'm_a_ln_g': out['m_a_ln_g'], 'm_a_ln_b': out['m_a_ln_b'], 'm_a_w_s': out['m_a_w_s'], 'm_a_b_s': out['m_a_b_s'], 'm_a_w_out': out['m_a_w_out'], 'm_b_w_qkv': out['m_b_w_qkv'], 'm_b_w_out': out['m_b_w_out'], 'm_rel_bias': out['m_rel_bias'], 'm_w_up': out['m_w_up'], 'm_w_down': out['m_w_down'], 'v_mix_norm_g': out['v_mix_norm_g'], 'v_mlp_norm_g': out['v_mlp_norm_g'], 'v_final_norm_g': out['v_final_norm_g'], 'v_a_w_in': out['v_a_w_in'], 'v_a_ln_g': out['v_a_ln_g'], 'v_a_ln_b': out['v_a_ln_b'], 'v_a_w_s': out['v_a_w_s'], 'v_a_b_s': out['v_a_b_s'], 'v_a_w_out': out['v_a_w_out'], 'v_b_w_qkv': out['v_b_w_qkv'], 'v_b_w_out': out['v_b_w_out'], 'v_rel_bias': out['v_rel_bias'], 'v_w_up': out['v_w_up'], 'v_w_down': out['v_w_down']}


def _loss(weights, diff, rest, loss_target):
    with _jax.named_scope("forward"):
        args = {**rest, TWIN_DIFF_INPUT: diff, **{k: w.astype(_WEIGHT_DTYPES[k]) for k, w in weights.items()}}
        y = _forward(args)
    with _jax.named_scope("loss_head"):
        err = _jnp.square(y.astype(_jnp.float32) - loss_target)
        return 0.5 * _jnp.sum(_jnp.mean(err, axis=-1)) if err.ndim else 0.5 * err


def _adamw(w, g, m, v):
    m = ADAM_B1 * m + (1.0 - ADAM_B1) * g
    v = ADAM_B2 * v + (1.0 - ADAM_B2) * _jnp.square(g)
    m_hat = m / (1.0 - ADAM_B1 ** ADAM_STEP)
    v_hat = v / (1.0 - ADAM_B2 ** ADAM_STEP)
    delta = -ADAM_LR * (m_hat / (_jnp.sqrt(v_hat) + ADAM_EPS) + ADAM_WD * w)
    return delta, m, v


def reference(x, mix_norm_g, mlp_norm_g, final_norm_g, a_w_in, a_ln_g, a_ln_b, a_w_s, a_b_s, a_w_out, b_w_qkv, b_w_out, rel_bias, w_up, w_down, loss_target, m_mix_norm_g, m_mlp_norm_g, m_final_norm_g, m_a_w_in, m_a_ln_g, m_a_ln_b, m_a_w_s, m_a_b_s, m_a_w_out, m_b_w_qkv, m_b_w_out, m_rel_bias, m_w_up, m_w_down, v_mix_norm_g, v_mlp_norm_g, v_final_norm_g, v_a_w_in, v_a_ln_g, v_a_ln_b, v_a_w_s, v_a_b_s, v_a_w_out, v_b_w_qkv, v_b_w_out, v_rel_bias, v_w_up, v_w_down):
    given = dict(x=x, mix_norm_g=mix_norm_g, mlp_norm_g=mlp_norm_g, final_norm_g=final_norm_g, a_w_in=a_w_in, a_ln_g=a_ln_g, a_ln_b=a_ln_b, a_w_s=a_w_s, a_b_s=a_b_s, a_w_out=a_w_out, b_w_qkv=b_w_qkv, b_w_out=b_w_out, rel_bias=rel_bias, w_up=w_up, w_down=w_down, loss_target=loss_target, m_mix_norm_g=m_mix_norm_g, m_mlp_norm_g=m_mlp_norm_g, m_final_norm_g=m_final_norm_g, m_a_w_in=m_a_w_in, m_a_ln_g=m_a_ln_g, m_a_ln_b=m_a_ln_b, m_a_w_s=m_a_w_s, m_a_b_s=m_a_b_s, m_a_w_out=m_a_w_out, m_b_w_qkv=m_b_w_qkv, m_b_w_out=m_b_w_out, m_rel_bias=m_rel_bias, m_w_up=m_w_up, m_w_down=m_w_down, v_mix_norm_g=v_mix_norm_g, v_mlp_norm_g=v_mlp_norm_g, v_final_norm_g=v_final_norm_g, v_a_w_in=v_a_w_in, v_a_ln_g=v_a_ln_g, v_a_ln_b=v_a_ln_b, v_a_w_s=v_a_w_s, v_a_b_s=v_a_b_s, v_a_w_out=v_a_w_out, v_b_w_qkv=v_b_w_qkv, v_b_w_out=v_b_w_out, v_rel_bias=v_rel_bias, v_w_up=v_w_up, v_w_down=v_w_down)
    weights = {n: given[n] for n in TWIN_WEIGHTS}
    shared = {n: given[n] for n in SHARED_INPUTS}
    per_example = {n: given[n] for n in ['x']}
    grad_fn = _jax.value_and_grad(_loss, argnums=(0, 1))

    def one_microbatch(ex, loss_target):
        ex = dict(ex)
        diff = ex.pop(TWIN_DIFF_INPUT)
        return grad_fn(weights, diff, {**shared, **ex}, loss_target)

    if N_MICROBATCH == 1:
        loss, (grad_w, grad_x) = one_microbatch(per_example, given["loss_target"])
    else:
        def body(carry, xs):
            loss_sum, grad_sum = carry
            l_k, (gw_k, gx_k) = one_microbatch(xs[0], xs[1])
            with _jax.named_scope("update"):
                return (loss_sum + l_k, _jax.tree.map(_jnp.add, grad_sum, gw_k)), gx_k

        init = (_jnp.zeros((), _jnp.float32), _jax.tree.map(_jnp.zeros_like, weights))
        (loss, grad_w), grad_x = _jax.lax.scan(body, init, (per_example, given["loss_target"]))
    with _jax.named_scope("update"):
        delta_w, new_m, new_v = {}, {}, {}
        for n in TWIN_WEIGHTS:
            delta_w[n], new_m[n], new_v[n] = _adamw(weights[n], grad_w[n], given["m_" + n], given["v_" + n])
    return (loss, grad_x, *[grad_w[n] for n in TWIN_WEIGHTS], *[delta_w[n] for n in TWIN_WEIGHTS],
            *[new_m[n] for n in TWIN_WEIGHTS], *[new_v[n] for n in TWIN_WEIGHTS])
```

```python
import functools
import math

import numpy as np
import jax
import jax.numpy as jnp
from jax import lax
from jax.experimental import pallas as pl
from jax.experimental.pallas import tpu as pltpu

F32 = jnp.float32
BF16 = jnp.bfloat16
MESH = pl.DeviceIdType.MESH
ANY = pl.BlockSpec(memory_space=pl.ANY)

N_CHIPS = 4
N_DEV = 8
VMEM_LIMIT_BYTES = 52 * 1024 * 1024

EPS = 1e-6
NEG_INF = -1e30
CHUNK = 128
GROUP_DIM = 128
HEAD_DIM = 64
ATT_HEADS = 8
ATT_WIDTH = ATT_HEADS * HEAD_DIM
PAIR = 2 * HEAD_DIM
BLK = 128
DILATIONS = (1, 4, 16)
N_BUCKETS = 32
MAX_EXACT = N_BUCKETS // 2
REL_MAX_DISTANCE = 2048

ADAM_LR = 0.001
ADAM_B1 = 0.9
ADAM_B2 = 0.999
ADAM_EPS = 1e-08
ADAM_WD = 0.01
ADAM_STEP = 10

NN = (((1,), (0,)), ((), ()))
NT = (((1,), (1,)), ((), ()))
TN = (((0,), (0,)), ((), ()))


def _params(**kw):
    return pltpu.CompilerParams(vmem_limit_bytes=VMEM_LIMIT_BYTES, **kw)


def _dot(a, b, dims=NN):
    return lax.dot_general(a, b, dims, preferred_element_type=F32)


def _gelu(x):
    return 0.5 * x * (1.0 + lax.erf(x * math.sqrt(0.5)))


def _gelu_grad(x):
    return 0.5 * (1.0 + lax.erf(x * math.sqrt(0.5))) + x * jnp.exp(-0.5 * x * x) * (1.0 / math.sqrt(2.0 * math.pi))


def _mean(x):
    return jnp.mean(x, axis=-1, keepdims=True)


def _mm(name, a, b, mode, *, tm, tn, tk, outs, epi=None, extras=()):
    sharded = b.ndim == 3
    if mode == "tn":
        K, M = a.shape
    else:
        M, K = a.shape
    if mode == "nn":
        N = b.shape[-1] * (N_CHIPS if sharded else 1)
    elif mode == "nt":
        N = b.shape[-2]
    else:
        N = b.shape[-1]
    tm, tn, tk = min(tm, M), min(tn, N), min(tk, K)
    if sharded and mode == "nn" or any(layout == "col" for _, layout in outs):
        tn = min(tn, N // N_CHIPS)
    if sharded and mode == "nt":
        tk = min(tk, b.shape[-1])
    assert M % tm == 0 and N % tn == 0 and K % tk == 0, (name, M, N, K, tm, tn, tk)
    nk = K // tk
    grid = (M // tm, N // tn, nk)

    if mode == "tn":
        a_spec = pl.BlockSpec((tk, tm), lambda i, j, k: (k, i))
    else:
        a_spec = pl.BlockSpec((tm, tk), lambda i, j, k: (i, k))
    if mode == "nn":
        if sharded:
            per = b.shape[-1] // tn
            assert b.shape[-1] % tn == 0, name
            b_spec = pl.BlockSpec((None, tk, tn), lambda i, j, k: (j // per, k, j % per))
        else:
            b_spec = pl.BlockSpec((tk, tn), lambda i, j, k: (k, j))
    elif mode == "nt":
        if sharded:
            per = b.shape[-1] // tk
            assert b.shape[-1] % tk == 0, name
            b_spec = pl.BlockSpec((None, tn, tk), lambda i, j, k: (k // per, j, k % per))
        else:
            b_spec = pl.BlockSpec((tn, tk), lambda i, j, k: (j, k))
    else:
        assert not sharded
        b_spec = pl.BlockSpec((tk, tn), lambda i, j, k: (k, j))

    out_shapes, out_specs = [], []
    for dtype, layout in outs:
        if layout == "col":
            nsh = N // N_CHIPS
            assert nsh % tn == 0, name
            pero = nsh // tn
            out_shapes.append(jax.ShapeDtypeStruct((N_CHIPS, M, nsh), dtype))
            out_specs.append(pl.BlockSpec((None, tm, tn), lambda i, j, k, pero=pero: (j // pero, i, j % pero)))
        else:
            out_shapes.append(jax.ShapeDtypeStruct((M, N), dtype))
            out_specs.append(pl.BlockSpec((tm, tn), lambda i, j, k: (i, j)))
    extra_specs = [pl.BlockSpec((tm, tn), lambda i, j, k: (i, j)) for _ in extras]
    n_extra, n_out = len(extras), len(outs)
    dims = {"nn": NN, "nt": NT, "tn": TN}[mode]

    def body(*refs):
        a_ref, b_ref = refs[0], refs[1]
        extra_refs = refs[2:2 + n_extra]
        out_refs = refs[2 + n_extra:2 + n_extra + n_out]
        part = _dot(a_ref[...].astype(BF16), b_ref[...].astype(BF16), dims)

        def finish(acc):
            res = epi(acc, *[e[...] for e in extra_refs]) if epi is not None else (acc,) * n_out
            for o, r in zip(out_refs, res):
                o[...] = r.astype(o.dtype)

        if nk == 1:
            finish(part)
        else:
            acc_ref = refs[-1]
            k = pl.program_id(2)

            @pl.when(k == 0)
            def _():
                acc_ref[...] = part

            @pl.when(k > 0)
            def _():
                acc_ref[...] += part

            @pl.when(k == nk - 1)
            def _():
                finish(acc_ref[...])

    res = pl.pallas_call(
        body, name=name, grid=grid,
        in_specs=[a_spec, b_spec] + extra_specs,
        out_specs=out_specs, out_shape=out_shapes,
        scratch_shapes=[pltpu.VMEM((tm, tn), F32)] if nk > 1 else [],
        compiler_params=_params(dimension_semantics=("parallel", "parallel", "arbitrary")),
    )(a, b, *extras)
    return res[0] if n_out == 1 else res


def _epi_residual(acc, res):
    return (res + acc,)


def _epi_relu2(acc):
    return (jnp.square(jnp.maximum(acc, 0.0)),)


def _epi_relu2_grad(acc, q):
    return (acc * (2.0 * jnp.sqrt(q.astype(F32))),)


def _row_tile(T):
    return min(T, 512)


def _rms_fwd(name, h, g):
    T, D = h.shape
    tr = _row_tile(T)

    def body(h_ref, g_ref, y_ref):
        hv = h_ref[...]
        y = hv * lax.rsqrt(_mean(hv * hv) + EPS)
        y_ref[...] = (y * g_ref[...]).astype(BF16)

    return pl.pallas_call(
        body, name=name, grid=(T // tr,),
        in_specs=[pl.BlockSpec((tr, D), lambda i: (i, 0)), pl.BlockSpec((1, D), lambda i: (0, 0))],
        out_specs=pl.BlockSpec((tr, D), lambda i: (i, 0)),
        out_shape=jax.ShapeDtypeStruct((T, D), BF16),
        compiler_params=_params(dimension_semantics=("parallel",)),
    )(h, g)


def _rms_bwd(name, dy, h, g, dres):
    T, D = h.shape
    tr = _row_tile(T)

    def body(dy_ref, h_ref, g_ref, dres_ref, dh_ref, dg_ref):
        @pl.when(pl.program_id(0) == 0)
        def _():
            dg_ref[...] = jnp.zeros_like(dg_ref)

        hv = h_ref[...]
        r = lax.rsqrt(_mean(hv * hv) + EPS)
        hn = hv * r
        dyv = dy_ref[...]
        dg_ref[...] += jnp.sum(dyv * hn, axis=0, keepdims=True)
        dyg = dyv * g_ref[...]
        dh_ref[...] = dres_ref[...] + r * (dyg - hn * _mean(dyg * hn))

    row = pl.BlockSpec((tr, D), lambda i: (i, 0))
    vec = pl.BlockSpec((1, D), lambda i: (0, 0))
    return pl.pallas_call(
        body, name=name, grid=(T // tr,),
        in_specs=[row, row, vec, row], out_specs=[row, vec],
        out_shape=[jax.ShapeDtypeStruct((T, D), F32), jax.ShapeDtypeStruct((1, D), F32)],
        compiler_params=_params(dimension_semantics=("arbitrary",)),
    )(dy, h, g, dres)


def _loss_head(h, g, target):
    T, D = h.shape
    tr = _row_tile(T)

    def body(h_ref, g_ref, t_ref, dh_ref, dg_ref, loss_ref):
        @pl.when(pl.program_id(0) == 0)
        def _():
            dg_ref[...] = jnp.zeros_like(dg_ref)
            loss_ref[...] = jnp.zeros_like(loss_ref)

        hv = h_ref[...]
        r = lax.rsqrt(_mean(hv * hv) + EPS)
        hn = hv * r
        gv = g_ref[...]
        diff = hn * gv - t_ref[...]
        loss_ref[...] += 0.5 * jnp.sum(_mean(diff * diff))
        dyv = diff * (1.0 / D)
        dg_ref[...] += jnp.sum(dyv * hn, axis=0, keepdims=True)
        dyg = dyv * gv
        dh_ref[...] = r * (dyg - hn * _mean(dyg * hn))

    row = pl.BlockSpec((tr, D), lambda i: (i, 0))
    vec = pl.BlockSpec((1, D), lambda i: (0, 0))
    return pl.pallas_call(
        body, name="loss_head", grid=(T // tr,),
        in_specs=[row, vec, row], out_specs=[row, vec, pl.BlockSpec((8, 128), lambda i: (0, 0))],
        out_shape=[jax.ShapeDtypeStruct((T, D), F32), jax.ShapeDtypeStruct((1, D), F32),
                   jax.ShapeDtypeStruct((8, 128), F32)],
        compiler_params=_params(dimension_semantics=("arbitrary",)),
    )(h, g, target)


def _gate_tile(T):
    return min(T, 256)


def _gate_fwd(a, ln_g, ln_b, w_tril, b_rows):
    T, W2 = a.shape
    W = W2 // 2
    G = W // GROUP_DIM
    tr = _gate_tile(T)

    def body(a_ref, lng_ref, lnb_ref, w_ref, b_ref, z_ref):
        u = _gelu(a_ref[:, :W].astype(F32))
        vg = _gelu(a_ref[:, W:].astype(F32))
        xc = vg - _mean(vg)
        vn = xc * lax.rsqrt(_mean(xc * xc) + EPS)
        vl = (vn * lng_ref[...] + lnb_ref[...]).astype(BF16)
        for n in range(tr // CHUNK):
            rows = slice(n * CHUNK, (n + 1) * CHUNK)
            for g in range(G):
                cols = slice(g * GROUP_DIM, (g + 1) * GROUP_DIM)
                gate = _dot(w_ref[g], vl[rows, cols]) + b_ref[g]
                z_ref[rows, cols] = (u[rows, cols] * gate).astype(BF16)

    vec = pl.BlockSpec((1, W), lambda i: (0, 0))
    grp = pl.BlockSpec((G, CHUNK, CHUNK), lambda i: (0, 0, 0))
    return pl.pallas_call(
        body, name="gate_fwd", grid=(T // tr,),
        in_specs=[pl.BlockSpec((tr, W2), lambda i: (i, 0)), vec, vec, grp, grp],
        out_specs=pl.BlockSpec((tr, W), lambda i: (i, 0)),
        out_shape=jax.ShapeDtypeStruct((T, W), BF16),
        compiler_params=_params(dimension_semantics=("parallel",)),
    )(a, ln_g, ln_b, w_tril, b_rows)


def _gate_bwd(a, dz, ln_g, ln_b, w_tril, w_tril_t, b_rows):
    T, W2 = a.shape
    W = W2 // 2
    G = W // GROUP_DIM
    tr = _gate_tile(T)
    steps = T // tr

    def body(a_ref, dz_ref, lng_ref, lnb_ref, w_ref, wt_ref, b_ref, da_ref, dlng_ref, dlnb_ref, dw_ref, dbs_ref, dvl_ref):
        step = pl.program_id(0)

        @pl.when(step == 0)
        def _():
            dlng_ref[...] = jnp.zeros_like(dlng_ref)
            dlnb_ref[...] = jnp.zeros_like(dlnb_ref)
            dw_ref[...] = jnp.zeros_like(dw_ref)
            dbs_ref[...] = jnp.zeros_like(dbs_ref)

        au = a_ref[:, :W].astype(F32)
        av = a_ref[:, W:].astype(F32)
        u = _gelu(au)
        vg = _gelu(av)
        xc = vg - _mean(vg)
        rstd = lax.rsqrt(_mean(xc * xc) + EPS)
        vn = xc * rstd
        lng = lng_ref[...]
        vl = (vn * lng + lnb_ref[...]).astype(BF16)
        du_scale = dz_ref[...] * _gelu_grad(au)
        dgate_all = dz_ref[...] * u
        for n in range(tr // CHUNK):
            rows = slice(n * CHUNK, (n + 1) * CHUNK)
            for g in range(G):
                cols = slice(g * GROUP_DIM, (g + 1) * GROUP_DIM)
                vlg = vl[rows, cols]
                gate = _dot(w_ref[g], vlg) + b_ref[g]
                da_ref[rows, cols] = (du_scale[rows, cols] * gate).astype(BF16)
                dgate = dgate_all[rows, cols]
                dbs_ref[g] += dgate
                dgate_b = dgate.astype(BF16)
                dw_ref[g] += _dot(dgate_b, vlg, NT)
                dvl_ref[rows, cols] = _dot(wt_ref[g], dgate_b)
        dvl = dvl_ref[...]
        dlnb_ref[...] += jnp.sum(dvl, axis=0, keepdims=True)
        dlng_ref[...] += jnp.sum(dvl * vn, axis=0, keepdims=True)
        dvn = dvl * lng
        dvg = rstd * (dvn - _mean(dvn) - vn * _mean(dvn * vn))
        da_ref[:, W:] = (dvg * _gelu_grad(av)).astype(BF16)

        @pl.when(step == steps - 1)
        def _():
            t_idx = lax.broadcasted_iota(jnp.int32, (CHUNK, CHUNK), 0)
            s_idx = lax.broadcasted_iota(jnp.int32, (CHUNK, CHUNK), 1)
            for g in range(G):
                dw_ref[g] = jnp.where(s_idx <= t_idx, dw_ref[g], 0.0)
                dbs_ref[g] = jnp.broadcast_to(jnp.sum(dbs_ref[g], axis=-1, keepdims=True), (CHUNK, CHUNK))

    vec = pl.BlockSpec((1, W), lambda i: (0, 0))
    grp = pl.BlockSpec((G, CHUNK, CHUNK), lambda i: (0, 0, 0))
    return pl.pallas_call(
        body, name="gate_bwd", grid=(steps,),
        in_specs=[pl.BlockSpec((tr, W2), lambda i: (i, 0)), pl.BlockSpec((tr, W), lambda i: (i, 0)),
                  vec, vec, grp, grp, grp],
        out_specs=[pl.BlockSpec((tr, W2), lambda i: (i, 0)), vec, vec, grp, grp],
        out_shape=[jax.ShapeDtypeStruct((T, W2), BF16), jax.ShapeDtypeStruct((1, W), F32),
                   jax.ShapeDtypeStruct((1, W), F32), jax.ShapeDtypeStruct((G, CHUNK, CHUNK), F32),
                   jax.ShapeDtypeStruct((G, CHUNK, CHUNK), F32)],
        scratch_shapes=[pltpu.VMEM((tr, W), F32)],
        compiler_params=_params(dimension_semantics=("arbitrary",)),
    )(a, dz, ln_g, ln_b, w_tril, w_tril_t, b_rows)


def _bucket_map(dilation):
    rel = BLK + np.arange(BLK)[:, None] - np.arange(2 * BLK)[None, :]
    dist = np.clip(rel, 0, BLK) * dilation
    nf = np.maximum(dist, 1).astype(np.float32)
    large = MAX_EXACT + (np.log(nf / np.float32(MAX_EXACT)) / np.float32(math.log(REL_MAX_DISTANCE / MAX_EXACT))
                         * np.float32(N_BUCKETS - MAX_EXACT)).astype(np.int32)
    large = np.minimum(large, N_BUCKETS - 1)
    return np.where(dist < MAX_EXACT, dist, large).astype(np.int32)


def _bucket_maps():
    return jnp.asarray(np.stack([_bucket_map(d) for d in DILATIONS]))


def _bias_build(rel_bias, buckets):
    NG = len(DILATIONS)

    def body(table_ref, bucket_ref, out_ref):
        out_ref[...] = jnp.zeros_like(out_ref)
        for g in range(NG):
            bk = bucket_ref[g]
            for b in range(N_BUCKETS):
                hit = bk == b
                for h in range(ATT_HEADS):
                    out_ref[g, h] = jnp.where(hit, table_ref[b, g * ATT_HEADS + h], out_ref[g, h])

    return pl.pallas_call(
        body, name="bias_build",
        in_specs=[pl.BlockSpec(memory_space=pltpu.SMEM), pl.BlockSpec(memory_space=pltpu.VMEM)],
        out_specs=pl.BlockSpec(memory_space=pltpu.VMEM),
        out_shape=jax.ShapeDtypeStruct((NG, ATT_HEADS, BLK, 2 * BLK), F32),
        compiler_params=_params(),
    )(rel_bias, buckets)


def _bias_scatter(dbias, buckets):
    NG = len(DILATIONS)

    def body(dbias_ref, bucket_ref, out_ref):
        for g in range(NG):
            bk = bucket_ref[g]
            for b in range(N_BUCKETS):
                hit = bk == b
                for h in range(ATT_HEADS):
                    out_ref[b, g * ATT_HEADS + h] = jnp.sum(jnp.where(hit, dbias_ref[g, h], 0.0))

    return pl.pallas_call(
        body, name="bias_scatter",
        in_specs=[pl.BlockSpec(memory_space=pltpu.VMEM), pl.BlockSpec(memory_space=pltpu.VMEM)],
        out_specs=pl.BlockSpec(memory_space=pltpu.SMEM),
        out_shape=jax.ShapeDtypeStruct((N_BUCKETS, NG * ATT_HEADS), F32),
        compiler_params=_params(),
    )(dbias, buckets)


def _window_mask(first):
    qi = lax.broadcasted_iota(jnp.int32, (BLK, 2 * BLK), 0)
    kj = lax.broadcasted_iota(jnp.int32, (BLK, 2 * BLK), 1)
    rel = BLK + qi - kj
    return (rel >= 0) & (rel <= BLK) & (kj >= BLK * first)


def _head_lanes(hh):
    lane = lax.broadcasted_iota(jnp.int32, (1, PAIR), 1)
    return (lane >= hh * HEAD_DIM) & (lane < (hh + 1) * HEAD_DIM)


def _attn_fwd(name, g, q, k, v, qc, kc, vc, bias, blocks_per_residue):
    T = q.shape[0]
    nb = T // BLK
    scale = HEAD_DIM ** -0.5

    def body(q_ref, kp_ref, kc_ref, vp_ref, vc_ref, bias_ref, o_ref, lse_ref):
        b = pl.program_id(0)
        valid = _window_mask((b % blocks_per_residue == 0).astype(jnp.int32))
        low = _head_lanes(0)
        for hp in range(ATT_HEADS // 2):
            cols = slice(hp * PAIR, (hp + 1) * PAIR)
            qp = q_ref[:, cols]
            kk = jnp.concatenate([kp_ref[:, cols], kc_ref[:, cols]], axis=0)
            vv = jnp.concatenate([vp_ref[:, cols], vc_ref[:, cols]], axis=0)
            o_h, lse_h = [], []
            for hh in range(2):
                qm = jnp.where(_head_lanes(hh), qp, jnp.zeros_like(qp))
                s = _dot(qm, kk, NT) * scale
                logits = jnp.where(valid, s + bias_ref[g, 2 * hp + hh], NEG_INF)
                m = jnp.max(logits, axis=-1, keepdims=True)
                p = jnp.exp(logits - m)
                den = jnp.sum(p, axis=-1, keepdims=True)
                o_h.append(_dot(p.astype(BF16), vv) / den)
                lse_h.append(m + jnp.log(den))
            o_ref[:, cols] = jnp.where(low, o_h[0], o_h[1])
            lse_ref[:, cols] = jnp.where(low, lse_h[0], lse_h[1])

    def cur(c):
        return pl.BlockSpec((BLK, ATT_WIDTH), lambda b: (b, c))

    def prev(c):
        return pl.BlockSpec((BLK, ATT_WIDTH), lambda b: (jnp.maximum(b - 1, 0), c))

    out = pl.BlockSpec((BLK, ATT_WIDTH), lambda b: (b, 0))
    return pl.pallas_call(
        body, name=name, grid=(nb,),
        in_specs=[cur(qc), prev(kc), cur(kc), prev(vc), cur(vc),
                  pl.BlockSpec(bias.shape, lambda b: (0, 0, 0, 0))],
        out_specs=[out, out],
        out_shape=[jax.ShapeDtypeStruct((T, ATT_WIDTH), F32), jax.ShapeDtypeStruct((T, ATT_WIDTH), F32)],
        compiler_params=_params(dimension_semantics=("parallel",)),
    )(q, k, k, v, v, bias)


def _attn_merge(outs, lses):
    T = outs[0].shape[0]
    tr = _row_tile(T)
    n = len(outs)

    def body(*refs):
        o_refs, l_refs = refs[:n], refs[n:2 * n]
        o_ref, lse_ref = refs[2 * n], refs[2 * n + 1]
        ls = [r[...] for r in l_refs]
        m = functools.reduce(jnp.maximum, ls)
        es = [jnp.exp(l - m) for l in ls]
        tot = functools.reduce(lambda x, y: x + y, es)
        acc = functools.reduce(lambda x, y: x + y, [e * r[...] for e, r in zip(es, o_refs)])
        o_ref[...] = (acc / tot).astype(BF16)
        lse_ref[...] = m + jnp.log(tot)

    row = pl.BlockSpec((tr, ATT_WIDTH), lambda i: (i, 0))
    return pl.pallas_call(
        body, name="attn_merge", grid=(T // tr,),
        in_specs=[row] * (2 * n), out_specs=[row, row],
        out_shape=[jax.ShapeDtypeStruct((T, ATT_WIDTH), BF16), jax.ShapeDtypeStruct((T, ATT_WIDTH), F32)],
        compiler_params=_params(dimension_semantics=("parallel",)),
    )(*outs, *lses)


def _attn_bwd(name, g, q, k, v, qc, kc, vc, do, o, lse, bias, blocks_per_residue):
    T = q.shape[0]
    nb = T // BLK
    scale = HEAD_DIM ** -0.5

    def body(q_ref, kp_ref, kc_ref, vp_ref, vc_ref, do_ref, o_ref, lse_ref, bias_ref,
             dq_ref, dk_ref, dv_ref, db_ref, ck_ref, cv_ref):
        b = pl.program_id(0)

        @pl.when(b == 0)
        def _():
            db_ref[...] = jnp.zeros_like(db_ref)
            ck_ref[...] = jnp.zeros_like(ck_ref)
            cv_ref[...] = jnp.zeros_like(cv_ref)

        @pl.when(b == nb)
        def _():
            dk_ref[...] = ck_ref[...].astype(BF16)
            dv_ref[...] = cv_ref[...].astype(BF16)

        @pl.when(b < nb)
        def _():
            valid = _window_mask((b % blocks_per_residue == 0).astype(jnp.int32))
            for hp in range(ATT_HEADS // 2):
                cols = slice(hp * PAIR, (hp + 1) * PAIR)
                qp = q_ref[:, cols]
                kk = jnp.concatenate([kp_ref[:, cols], kc_ref[:, cols]], axis=0)
                vv = jnp.concatenate([vp_ref[:, cols], vc_ref[:, cols]], axis=0)
                dop = do_ref[:, cols]
                lsep = lse_ref[:, cols]
                prod = dop.astype(F32) * o_ref[:, cols].astype(F32)
                dq = jnp.zeros((BLK, PAIR), F32)
                dk = jnp.zeros((2 * BLK, PAIR), F32)
                dv = jnp.zeros((2 * BLK, PAIR), F32)
                for hh in range(2):
                    lanes = _head_lanes(hh)
                    qm = jnp.where(lanes, qp, jnp.zeros_like(qp))
                    dom = jnp.where(lanes, dop, jnp.zeros_like(dop))
                    km = jnp.where(lanes, kk, jnp.zeros_like(kk))
                    delta = jnp.sum(jnp.where(lanes, prod, 0.0), axis=-1, keepdims=True)
                    lse_h = jnp.max(jnp.where(lanes, lsep, NEG_INF), axis=-1, keepdims=True)
                    s = _dot(qm, kk, NT) * scale
                    logits = jnp.where(valid, s + bias_ref[g, 2 * hp + hh], NEG_INF)
                    p = jnp.exp(logits - lse_h)
                    dv += _dot(p.astype(BF16), dom, TN)
                    ds = p * (_dot(dom, vv, NT) - delta)
                    db_ref[2 * hp + hh] += ds
                    dss = (ds * scale).astype(BF16)
                    dq += _dot(dss, km)
                    dk += _dot(dss, qm, TN)
                dq_ref[:, cols] = dq.astype(BF16)
                dk_ref[:, cols] = (ck_ref[:, cols] + dk[:BLK]).astype(BF16)
                dv_ref[:, cols] = (cv_ref[:, cols] + dv[:BLK]).astype(BF16)
                ck_ref[:, cols] = dk[BLK:]
                cv_ref[:, cols] = dv[BLK:]

    last = nb - 1

    def cur(c):
        return pl.BlockSpec((BLK, ATT_WIDTH), lambda b: (jnp.minimum(b, last), c))

    def prev(c):
        return pl.BlockSpec((BLK, ATT_WIDTH), lambda b: (jnp.clip(b - 1, 0, last), c))

    dbias_shape = (ATT_HEADS, BLK, 2 * BLK)
    return pl.pallas_call(
        body, name=name, grid=(nb + 1,),
        in_specs=[cur(qc), prev(kc), cur(kc), prev(vc), cur(vc), cur(0), cur(0), cur(0),
                  pl.BlockSpec(bias.shape, lambda b: (0, 0, 0, 0))],
        out_specs=[cur(0), prev(0), prev(0), pl.BlockSpec(dbias_shape, lambda b: (0, 0, 0))],
        out_shape=[jax.ShapeDtypeStruct((T, ATT_WIDTH), BF16)] * 3 + [jax.ShapeDtypeStruct(dbias_shape, F32)],
        scratch_shapes=[pltpu.VMEM((BLK, ATT_WIDTH), F32), pltpu.VMEM((BLK, ATT_WIDTH), F32)],
        compiler_params=_params(dimension_semantics=("arbitrary",)),
    )(q, k, k, v, v, do, o, lse, bias)


def _residue_major(a, d):
    T, C = a.shape
    return a.reshape(T // d, d, C).transpose(1, 0, 2).reshape(T, C)


def _position_major(a, d):
    T, C = a.shape
    return a.reshape(d, T // d, C).transpose(1, 0, 2).reshape(T, C)


def _group_operands(qkv, g, d):
    NG = len(DILATIONS)
    if d == 1:
        return [(qkv, part * NG + g) for part in range(3)]
    cols = lambda part: qkv[:, (part * NG + g) * ATT_WIDTH:(part * NG + g + 1) * ATT_WIDTH]
    return [(_residue_major(cols(part), d), 0) for part in range(3)]


def _attention_fwd(qkv, bias):
    T = qkv.shape[0]
    outs, lses = [], []
    for g, d in enumerate(DILATIONS):
        (q, qc), (k, kc), (v, vc) = _group_operands(qkv, g, d)
        o_g, lse_g = _attn_fwd(f"attn_fwd_{g}", g, q, k, v, qc, kc, vc, bias, T // (d * BLK))
        if d > 1:
            o_g, lse_g = _position_major(o_g, d), _position_major(lse_g, d)
        outs.append(o_g)
        lses.append(lse_g)
    return _attn_merge(outs, lses)


def _attention_bwd(qkv, do, o, lse, bias):
    T = qkv.shape[0]
    dqs, dks, dvs, dbs = [], [], [], []
    for g, d in enumerate(DILATIONS):
        (q, qc), (k, kc), (v, vc) = _group_operands(qkv, g, d)
        do_g, o_g, lse_g = (do, o, lse) if d == 1 else tuple(_residue_major(t, d) for t in (do, o, lse))
        dq, dk, dv, db = _attn_bwd(f"attn_bwd_{g}", g, q, k, v, qc, kc, vc, do_g, o_g, lse_g, bias, T // (d * BLK))
        if d > 1:
            dq, dk, dv = (_position_major(t, d) for t in (dq, dk, dv))
        dqs.append(dq)
        dks.append(dk)
        dvs.append(dv)
        dbs.append(db)
    return jnp.concatenate(dqs + dks + dvs, axis=1), jnp.stack(dbs)


def _other_chips(x, y):
    return [(1 - x, y), (x, 1 - y), (1 - x, 1 - y)]


def _allgather_weights(shards):
    n = len(shards)

    def body(*refs):
        srcs, outs = refs[:n], refs[n:2 * n]
        send_sems, recv_sems, local_sems = refs[2 * n:]
        x, y, c = lax.axis_index("x"), lax.axis_index("y"), lax.axis_index("c")
        sibling = (x, y, 1 - c)
        chips = _other_chips(x, y)

        def half(ref, pc):
            rows = ref.shape[0] // 2
            return ref.at[pl.ds(pc * rows, rows), :]

        def landing(i, px, py, pc):
            return half(outs[i].at[2 * px + py], pc)

        def copy(i, k, block, to, src=None):
            dst = landing(i, *block)
            return pltpu.make_async_remote_copy(
                src_ref=dst if src is None else src, dst_ref=dst,
                send_sem=send_sems.at[6 * i + k], recv_sem=recv_sems.at[6 * i + k],
                device_id=to, device_id_type=MESH)

        mine = [pltpu.make_async_copy(srcs[i], outs[i].at[2 * x + y], local_sems.at[i]) for i in range(n)]
        for cp in mine:
            cp.start()
        first = [copy(i, j, (x, y, c), (*chip, c), src=half(srcs[i], c))
                 for j, chip in enumerate(chips) for i in range(n)]
        for cp in first:
            cp.start()
        passed = []
        for j, chip in enumerate(chips):
            for i in range(n):
                copy(i, j, (*chip, c), (x, y, c)).wait_recv()
                cp = copy(i, 3 + j, (*chip, c), sibling)
                cp.start()
                passed.append(cp)
        for j, chip in enumerate(chips):
            for i in range(n):
                copy(i, 3 + j, (*chip, 1 - c), (x, y, c)).wait_recv()
        for cp in first + passed:
            cp.wait_send()
        for cp in mine:
            cp.wait()

    return pl.pallas_call(
        body, name="allgather_weights",
        in_specs=[ANY] * n, out_specs=[ANY] * n,
        out_shape=[jax.ShapeDtypeStruct((N_CHIPS,) + s.shape, s.dtype) for s in shards],
        scratch_shapes=[pltpu.SemaphoreType.DMA((6 * n,)), pltpu.SemaphoreType.DMA((6 * n,)),
                        pltpu.SemaphoreType.DMA((n,))],
        compiler_params=_params(),
    )(*shards)


def _scatter_grads(grads):
    n = len(grads)

    def body(*refs):
        srcs, outs = refs[:n], refs[n:2 * n]
        send_sems, recv_sems, local_sems = refs[2 * n:]
        x, y, c = lax.axis_index("x"), lax.axis_index("y"), lax.axis_index("c")
        me = 2 * x + y
        chips = _other_chips(x, y)

        def copy(i, j):
            px, py = chips[j]
            return pltpu.make_async_remote_copy(
                src_ref=srcs[i].at[2 * px + py], dst_ref=outs[i].at[me],
                send_sem=send_sems.at[3 * i + j], recv_sem=recv_sems.at[3 * i + j],
                device_id=(px, py, c), device_id_type=MESH)

        mine = [pltpu.make_async_copy(srcs[i].at[me], outs[i].at[me], local_sems.at[i]) for i in range(n)]
        sends = [copy(i, j) for j in range(3) for i in range(n)]
        for cp in mine + sends:
            cp.start()
        for cp in sends:
            cp.wait()
        for cp in mine:
            cp.wait()

    return pl.pallas_call(
        body, name="scatter_grads",
        in_specs=[ANY] * n, out_specs=[ANY] * n,
        out_shape=[jax.ShapeDtypeStruct(s.shape, s.dtype) for s in grads],
        scratch_shapes=[pltpu.SemaphoreType.DMA((3 * n,)), pltpu.SemaphoreType.DMA((3 * n,)),
                        pltpu.SemaphoreType.DMA((n,))],
        compiler_params=_params(),
    )(*grads)


def _exchange_sibling(parts):
    n = len(parts)

    def body(*refs):
        srcs, outs = refs[:n], refs[n:2 * n]
        send_sems, recv_sems = refs[2 * n:]
        sibling = (lax.axis_index("x"), lax.axis_index("y"), 1 - lax.axis_index("c"))
        copies = [pltpu.make_async_remote_copy(src_ref=srcs[i], dst_ref=outs[i], send_sem=send_sems.at[i],
                                               recv_sem=recv_sems.at[i], device_id=sibling, device_id_type=MESH)
                  for i in range(n)]
        for cp in copies:
            cp.start()
        for cp in copies:
            cp.wait()

    return pl.pallas_call(
        body, name="exchange_sibling",
        in_specs=[ANY] * n, out_specs=[ANY] * n,
        out_shape=[jax.ShapeDtypeStruct(s.shape, s.dtype) for s in parts],
        scratch_shapes=[pltpu.SemaphoreType.DMA((n,)), pltpu.SemaphoreType.DMA((n,))],
        compiler_params=_params(),
    )(*parts)


def _allgather_small(block):
    m_per, ncol = block.shape

    def body(x_ref, out_ref, send_sems, recv_sems, local_sem):
        x, y, c = lax.axis_index("x"), lax.axis_index("y"), lax.axis_index("c")
        me, sibling = (x, y, c), (x, y, 1 - c)
        chips = _other_chips(x, y)

        def rows(px, py, pc):
            return out_ref.at[4 * px + 2 * py + pc]

        def copy(k, block_of, to, src=None):
            return pltpu.make_async_remote_copy(
                src_ref=rows(*block_of) if src is None else src, dst_ref=rows(*block_of),
                send_sem=send_sems.at[k], recv_sem=recv_sems.at[k], device_id=to, device_id_type=MESH)

        mine = pltpu.make_async_copy(x_ref, rows(*me), local_sem)
        mine.start()
        first = [copy(0, me, sibling, src=x_ref)]
        first += [copy(1 + j, me, (*chip, c), src=x_ref) for j, chip in enumerate(chips)]
        for cp in first:
            cp.start()
        passed = [copy(4 + j, (*chip, c), sibling) for j, chip in enumerate(chips)]
        for j, chip in enumerate(chips):
            copy(1 + j, (*chip, c), me).wait_recv()
            passed[j].start()
        copy(0, sibling, me).wait_recv()
        for j, chip in enumerate(chips):
            copy(4 + j, (*chip, 1 - c), me).wait_recv()
        for cp in first + passed:
            cp.wait_send()
        mine.wait()

    return pl.pallas_call(
        body, name="allgather_small",
        in_specs=[pl.BlockSpec(memory_space=pltpu.VMEM)], out_specs=pl.BlockSpec(memory_space=pltpu.VMEM),
        out_shape=jax.ShapeDtypeStruct((N_DEV, m_per, ncol), block.dtype),
        scratch_shapes=[pltpu.SemaphoreType.DMA((7,)), pltpu.SemaphoreType.DMA((7,)), pltpu.SemaphoreType.DMA],
        compiler_params=_params(),
    )(block)


def _adamw(w, g, m, v):
    m = ADAM_B1 * m + (1.0 - ADAM_B1) * g
    v = ADAM_B2 * v + (1.0 - ADAM_B2) * jnp.square(g)
    m_hat = m / (1.0 - ADAM_B1 ** ADAM_STEP)
    v_hat = v / (1.0 - ADAM_B2 ** ADAM_STEP)
    delta = -ADAM_LR * (m_hat / (jnp.sqrt(v_hat) + ADAM_EPS) + ADAM_WD * w)
    return delta, m, v


def _flat_tile(rows):
    return min(rows, 256)


def _sum_pieces(name, pieces):
    P, R, C = pieces.shape
    tr = _flat_tile(R)

    def body(p_ref, out_ref):
        acc = p_ref[0].astype(F32)
        for j in range(1, P):
            acc = acc + p_ref[j].astype(F32)
        out_ref[...] = acc

    return pl.pallas_call(
        body, name=name, grid=(R // tr,),
        in_specs=[pl.BlockSpec((P, tr, C), lambda i: (0, i, 0))],
        out_specs=pl.BlockSpec((tr, C), lambda i: (i, 0)),
        out_shape=jax.ShapeDtypeStruct((R, C), F32),
        compiler_params=_params(dimension_semantics=("parallel",)),
    )(pieces)


def _adam_pair(name, w, m, v, part_a, part_b):
    R, C = w.shape
    tr = _flat_tile(R)

    def body(w_ref, m_ref, v_ref, a_ref, b_ref, g_ref, d_ref, nm_ref, nv_ref):
        g = a_ref[...] + b_ref[...]
        g_ref[...] = g
        d_ref[...], nm_ref[...], nv_ref[...] = _adamw(w_ref[...], g, m_ref[...], v_ref[...])

    row = pl.BlockSpec((tr, C), lambda i: (i, 0))
    return pl.pallas_call(
        body, name=name, grid=(R // tr,),
        in_specs=[row] * 5, out_specs=[row] * 4,
        out_shape=[jax.ShapeDtypeStruct((R, C), F32)] * 4,
        compiler_params=_params(dimension_semantics=("parallel",)),
    )(w, m, v, part_a, part_b)


def _adam_small(w, m, v, gathered):
    R, C = w.shape

    def body(w_ref, m_ref, v_ref, p_ref, g_ref, d_ref, nm_ref, nv_ref):
        g = p_ref[0]
        for j in range(1, N_DEV):
            g = g + p_ref[j]
        g_ref[...] = g
        d_ref[...], nm_ref[...], nv_ref[...] = _adamw(w_ref[...], g, m_ref[...], v_ref[...])

    return pl.pallas_call(
        body, name="adam_small",
        out_shape=[jax.ShapeDtypeStruct((R, C), F32)] * 4,
        compiler_params=_params(),
    )(w, m, v, gathered)


SMALL = ("mix_norm_g", "mlp_norm_g", "final_norm_g", "a_ln_g", "a_ln_b", "a_w_s", "a_b_s", "rel_bias")


def _pack_small(arrays, width):
    rows = []
    for a in arrays:
        flat = a.reshape(-1)
        pad = (-flat.shape[0]) % width
        rows.append(jnp.pad(flat, (0, pad)).reshape(-1, width))
    block = jnp.concatenate(rows, axis=0)
    return jnp.pad(block, ((0, (-block.shape[0]) % 8), (0, 0)))


def _unpack_small(block, shapes, width):
    out, row = [], 0
    for shape in shapes:
        size = int(np.prod(shape))
        nrows = -(-size // width)
        out.append(block[row:row + nrows].reshape(-1)[:size].reshape(shape))
        row += nrows
    return out


def kernel(x, mix_norm_g, mlp_norm_g, final_norm_g, a_w_in, a_ln_g, a_ln_b, a_w_s, a_b_s, a_w_out, b_w_qkv, b_w_out, rel_bias, w_up, w_down, loss_target, m_mix_norm_g, m_mlp_norm_g, m_final_norm_g, m_a_w_in, m_a_ln_g, m_a_ln_b, m_a_w_s, m_a_b_s, m_a_w_out, m_b_w_qkv, m_b_w_out, m_rel_bias, m_w_up, m_w_down, v_mix_norm_g, v_mlp_norm_g, v_final_norm_g, v_a_w_in, v_a_ln_g, v_a_ln_b, v_a_w_s, v_a_b_s, v_a_w_out, v_b_w_qkv, v_b_w_out, v_rel_bias, v_w_up, v_w_down):
    T, D = x.shape[1], x.shape[2]
    h0 = x.reshape(T, D)
    target = loss_target.reshape(T, D)
    G = a_w_s.shape[1]

    def big(a_in, a_out, qkv, b_out, up, down):
        return [a_in[0], a_out[0], qkv[0], b_out[0], up[0], up[1], down[0], down[1]]

    w_big = big(a_w_in, a_w_out, b_w_qkv, b_w_out, w_up, w_down)
    m_big = big(m_a_w_in, m_a_w_out, m_b_w_qkv, m_b_w_out, m_w_up, m_w_down)
    v_big = big(v_a_w_in, v_a_w_out, v_b_w_qkv, v_b_w_out, v_w_up, v_w_down)
    gathered = _allgather_weights([w.astype(BF16) for w in w_big])
    W_in, W_out, W_qkv, W_bo, W_up0, W_up1, W_dn0, W_dn1 = gathered
    W_out = W_out.reshape(-1, W_out.shape[-1])
    W_dn = [W_dn0.reshape(-1, D), W_dn1.reshape(-1, D)]
    W_up = [W_up0, W_up1]

    tril = jnp.tril(jnp.ones((CHUNK, CHUNK), dtype=bool))
    w_tril = jnp.where(tril[None], a_w_s[0], 0.0).astype(BF16)
    w_tril_t = jnp.swapaxes(w_tril, 1, 2)
    b_rows = jnp.broadcast_to(a_b_s[0][:, :, None], (G, CHUNK, CHUNK))
    buckets = _bucket_maps()
    bias = _bias_build(rel_bias, buckets)

    plain_bf = [(BF16, "plain")]
    plain_f32 = [(F32, "plain")]
    TM = 1024

    y0 = _rms_fwd("rms_fwd_mix0", h0, mix_norm_g[0:1])
    a_pre = _mm("a_in", y0, W_in, "nn", tm=TM, tn=512, tk=D, outs=plain_bf)
    z = _gate_fwd(a_pre, a_ln_g, a_ln_b, w_tril, b_rows)
    h1 = _mm("a_out", z, W_out, "nn", tm=TM, tn=1024, tk=1024, outs=plain_f32, epi=_epi_residual, extras=(h0,))

    def mlp_fwd(layer, h):
        y = _rms_fwd(f"rms_fwd_mlp{layer}", h, mlp_norm_g[layer:layer + 1])
        q = _mm(f"mlp_up{layer}", y, W_up[layer], "nn", tm=TM, tn=1024, tk=D, outs=plain_bf, epi=_epi_relu2)
        h_next = _mm(f"mlp_down{layer}", q, W_dn[layer], "nn", tm=TM, tn=1024, tk=1024, outs=plain_f32,
                     epi=_epi_residual, extras=(h,))
        return y, q, h_next

    y1, q1, h2 = mlp_fwd(0, h1)
    y2 = _rms_fwd("rms_fwd_mix1", h2, mix_norm_g[1:2])
    qkv = _mm("b_qkv", y2, W_qkv, "nn", tm=TM, tn=1152, tk=D, outs=plain_bf)
    o, lse = _attention_fwd(qkv, bias)
    h3 = _mm("b_out", o, W_bo, "nn", tm=TM, tn=256, tk=ATT_WIDTH, outs=plain_f32, epi=_epi_residual, extras=(h2,))
    y3, q3, h4 = mlp_fwd(1, h3)
    dh4, d_final_g, loss_tile = _loss_head(h4, final_norm_g.reshape(1, D), target)
    loss = lax.psum(loss_tile[0, 0], ("x", "y", "c"))

    wgrad_plain = [(BF16, "plain")]
    wgrad_col = [(BF16, "col")]

    def mlp_bwd(layer, h, y, q, dh_out):
        dp = _mm(f"mlp_down_bwd{layer}", dh_out, W_dn[layer], "nt", tm=TM, tn=1024, tk=D, outs=plain_bf,
                 epi=_epi_relu2_grad, extras=(q,))
        g_dn = _mm(f"mlp_down_wgrad{layer}", q, dh_out, "tn", tm=1024, tn=1024, tk=512, outs=wgrad_plain)
        dy = _mm(f"mlp_up_bwd{layer}", dp, W_up[layer], "nt", tm=TM, tn=1024, tk=1024, outs=plain_f32)
        g_up = _mm(f"mlp_up_wgrad{layer}", y, dp, "tn", tm=1024, tn=1024, tk=512, outs=wgrad_col)
        dh, dg = _rms_bwd(f"rms_bwd_mlp{layer}", dy, h, mlp_norm_g[layer:layer + 1], dh_out)
        return dh, dg, g_up, g_dn

    dh3, dg_mlp1, g_up1, g_dn1 = mlp_bwd(1, h3, y3, q3, dh4)
    do = _mm("b_out_bwd", dh3, W_bo, "nt", tm=TM, tn=ATT_WIDTH, tk=256, outs=plain_bf)
    g_bo = _mm("b_out_wgrad", o, dh3, "tn", tm=ATT_WIDTH, tn=256, tk=512, outs=wgrad_col)
    dqkv, dbias = _attention_bwd(qkv, do, o, lse, bias)
    d_rel_bias = _bias_scatter(dbias, buckets)
    dy2 = _mm("b_qkv_bwd", dqkv, W_qkv, "nt", tm=TM, tn=1024, tk=1152, outs=plain_f32)
    g_qkv = _mm("b_qkv_wgrad", y2, dqkv, "tn", tm=1024, tn=1152, tk=512, outs=wgrad_col)
    dh2, dg_mix1 = _rms_bwd("rms_bwd_mix1", dy2, h2, mix_norm_g[1:2], dh3)
    dh1, dg_mlp0, g_up0, g_dn0 = mlp_bwd(0, h1, y1, q1, dh2)
    dz = _mm("a_out_bwd", dh1, W_out, "nt", tm=TM, tn=1024, tk=D, outs=plain_f32)
    g_out = _mm("a_out_wgrad", z, dh1, "tn", tm=1024, tn=1024, tk=512, outs=wgrad_plain)
    da, d_ln_g, d_ln_b, d_w_s, d_b_s = _gate_bwd(a_pre, dz, a_ln_g, a_ln_b, w_tril, w_tril_t, b_rows)
    dy0 = _mm("a_in_bwd", da, W_in, "nt", tm=TM, tn=1024, tk=512, outs=plain_f32)
    g_in = _mm("a_in_wgrad", y0, da, "tn", tm=1024, tn=512, tk=512, outs=wgrad_col)
    grad_x, dg_mix0 = _rms_bwd("rms_bwd_mix0", dy0, h0, mix_norm_g[0:1], dh1)

    def pieces(g):
        return g if g.ndim == 3 else g.reshape(N_CHIPS, g.shape[0] // N_CHIPS, g.shape[1])

    order = [g_in, g_out, g_qkv, g_bo, g_up0, g_up1, g_dn0, g_dn1]
    received = _scatter_grads([pieces(g) for g in order])
    plane = [_sum_pieces(f"sum_pieces{i}", r) for i, r in enumerate(received)]
    other = _exchange_sibling(plane)
    big_out = [_adam_pair(f"adam{i}", w_big[i], m_big[i], v_big[i], plane[i], other[i]) for i in range(len(order))]

    def unbig(kind):
        t = [b[kind] for b in big_out]
        return {"a_w_in": t[0][None], "a_w_out": t[1][None], "b_w_qkv": t[2][None], "b_w_out": t[3][None],
                "w_up": jnp.stack([t[4], t[5]]), "w_down": jnp.stack([t[6], t[7]])}

    small_w = [mix_norm_g, mlp_norm_g, final_norm_g, a_ln_g, a_ln_b, a_w_s, a_b_s, rel_bias]
    small_m = [m_mix_norm_g, m_mlp_norm_g, m_final_norm_g, m_a_ln_g, m_a_ln_b, m_a_w_s, m_a_b_s, m_rel_bias]
    small_v = [v_mix_norm_g, v_mlp_norm_g, v_final_norm_g, v_a_ln_g, v_a_ln_b, v_a_w_s, v_a_b_s, v_rel_bias]
    small_g = [jnp.concatenate([dg_mix0, dg_mix1]), jnp.concatenate([dg_mlp0, dg_mlp1]), d_final_g,
               d_ln_g, d_ln_b, d_w_s[None], d_b_s[None, :, :, 0], d_rel_bias]
    width = max(D, 128)
    gathered_small = _allgather_small(_pack_small(small_g, width))
    small_out = _adam_small(_pack_small(small_w, width), _pack_small(small_m, width), _pack_small(small_v, width),
                            gathered_small)
    shapes = [w.shape for w in small_w]

    names = ["mix_norm_g", "mlp_norm_g", "final_norm_g", "a_w_in", "a_ln_g", "a_ln_b", "a_w_s", "a_b_s", "a_w_out",
             "b_w_qkv", "b_w_out", "rel_bias", "w_up", "w_down"]
    results = [loss, grad_x.reshape(x.shape)]
    for kind in range(4):
        table = dict(zip(SMALL, _unpack_small(small_out[kind], shapes, width)))
        table.update(unbig(kind))
        results += [table[n] for n in names]
    return tuple(results)
```

```python
import functools
import math

import numpy as np
import jax
import jax.numpy as jnp
from jax import lax
from jax.experimental import pallas as pl
from jax.experimental.pallas import tpu as pltpu

F32 = jnp.float32
BF16 = jnp.bfloat16
MESH = pl.DeviceIdType.MESH
ANY = pl.BlockSpec(memory_space=pl.ANY)

N_CHIPS = 4
N_DEV = 8
VMEM_LIMIT_BYTES = 52 * 1024 * 1024

EPS = 1e-6
NEG_INF = -1e30
CHUNK = 128
GROUP_DIM = 128
HEAD_DIM = 64
ATT_HEADS = 8
ATT_WIDTH = ATT_HEADS * HEAD_DIM
PAIR = 2 * HEAD_DIM
BLK = 128
DILATIONS = (1, 4, 16)
N_BUCKETS = 32
MAX_EXACT = N_BUCKETS // 2
REL_MAX_DISTANCE = 2048

ADAM_LR = 0.001
ADAM_B1 = 0.9
ADAM_B2 = 0.999
ADAM_EPS = 1e-08
ADAM_WD = 0.01
ADAM_STEP = 10

NN = (((1,), (0,)), ((), ()))
NT = (((1,), (1,)), ((), ()))
TN = (((0,), (0,)), ((), ()))


def _params(**kw):
    return pltpu.CompilerParams(vmem_limit_bytes=VMEM_LIMIT_BYTES, **kw)


def _dot(a, b, dims=NN):
    return lax.dot_general(a, b, dims, preferred_element_type=F32)


def _gelu(x):
    return 0.5 * x * (1.0 + lax.erf(x * math.sqrt(0.5)))


def _gelu_grad(x):
    return 0.5 * (1.0 + lax.erf(x * math.sqrt(0.5))) + x * jnp.exp(-0.5 * x * x) * (1.0 / math.sqrt(2.0 * math.pi))


def _mean(x):
    return jnp.mean(x, axis=-1, keepdims=True)


def _mm(name, a, b, mode, *, tm, tn, tk, outs, epi=None, extras=()):
    if mode == "tn":
        K, M = a.shape
    else:
        M, K = a.shape
    N = b.shape[0] if mode == "nt" else b.shape[1]
    tm, tn, tk = min(tm, M), min(tn, N), min(tk, K)
    assert M % tm == 0 and N % tn == 0 and K % tk == 0, (name, M, N, K, tm, tn, tk)
    nk = K // tk
    grid = (M // tm, N // tn, nk)

    if mode == "tn":
        a_spec = pl.BlockSpec((tk, tm), lambda i, j, k: (k, i))
    else:
        a_spec = pl.BlockSpec((tm, tk), lambda i, j, k: (i, k))
    if mode == "nt":
        b_spec = pl.BlockSpec((tn, tk), lambda i, j, k: (j, k))
    else:
        b_spec = pl.BlockSpec((tk, tn), lambda i, j, k: (k, j))
    tile = pl.BlockSpec((tm, tn), lambda i, j, k: (i, j))
    out_shapes = [jax.ShapeDtypeStruct((M, N), dtype) for dtype in outs]
    out_specs = [tile for _ in outs]
    extra_specs = [tile for _ in extras]
    n_extra, n_out = len(extras), len(outs)
    dims = {"nn": NN, "nt": NT, "tn": TN}[mode]

    def body(*refs):
        a_ref, b_ref = refs[0], refs[1]
        extra_refs = refs[2:2 + n_extra]
        out_refs = refs[2 + n_extra:2 + n_extra + n_out]
        part = _dot(a_ref[...].astype(BF16), b_ref[...].astype(BF16), dims)

        def finish(acc):
            res = epi(acc, *[e[...] for e in extra_refs]) if epi is not None else (acc,) * n_out
            for o, r in zip(out_refs, res):
                o[...] = r.astype(o.dtype)

        if nk == 1:
            finish(part)
        else:
            acc_ref = refs[-1]
            k = pl.program_id(2)

            @pl.when(k == 0)
            def _():
                acc_ref[...] = part

            @pl.when(k > 0)
            def _():
                acc_ref[...] += part

            @pl.when(k == nk - 1)
            def _():
                finish(acc_ref[...])

    res = pl.pallas_call(
        body, name=name, grid=grid,
        in_specs=[a_spec, b_spec] + extra_specs,
        out_specs=out_specs, out_shape=out_shapes,
        scratch_shapes=[pltpu.VMEM((tm, tn), F32)] if nk > 1 else [],
        compiler_params=_params(dimension_semantics=("parallel", "parallel", "arbitrary")),
    )(a, b, *extras)
    return res[0] if n_out == 1 else res


def _epi_residual(acc, res):
    return (res + acc,)


def _epi_relu2(acc):
    return (jnp.square(jnp.maximum(acc, 0.0)),)


def _epi_relu2_grad(acc, q):
    return (acc * (2.0 * jnp.sqrt(q.astype(F32))),)


def _row_tile(T):
    return min(T, 512)


def _rms_fwd(name, h, g):
    T, D = h.shape
    tr = _row_tile(T)

    def body(h_ref, g_ref, y_ref):
        hv = h_ref[...]
        y = hv * lax.rsqrt(_mean(hv * hv) + EPS)
        y_ref[...] = (y * g_ref[...]).astype(BF16)

    return pl.pallas_call(
        body, name=name, grid=(T // tr,),
        in_specs=[pl.BlockSpec((tr, D), lambda i: (i, 0)), pl.BlockSpec((1, D), lambda i: (0, 0))],
        out_specs=pl.BlockSpec((tr, D), lambda i: (i, 0)),
        out_shape=jax.ShapeDtypeStruct((T, D), BF16),
        compiler_params=_params(dimension_semantics=("parallel",)),
    )(h, g)


def _rms_bwd(name, dy, h, g, dres):
    T, D = h.shape
    tr = _row_tile(T)

    def body(dy_ref, h_ref, g_ref, dres_ref, dh_ref, dhb_ref, dg_ref):
        @pl.when(pl.program_id(0) == 0)
        def _():
            dg_ref[...] = jnp.zeros_like(dg_ref)

        hv = h_ref[...]
        r = lax.rsqrt(_mean(hv * hv) + EPS)
        hn = hv * r
        dyv = dy_ref[...]
        dg_ref[...] += jnp.sum(dyv * hn, axis=0, keepdims=True)
        dyg = dyv * g_ref[...]
        dh = dres_ref[...] + r * (dyg - hn * _mean(dyg * hn))
        dh_ref[...] = dh
        dhb_ref[...] = dh.astype(BF16)

    row = pl.BlockSpec((tr, D), lambda i: (i, 0))
    vec = pl.BlockSpec((1, D), lambda i: (0, 0))
    return pl.pallas_call(
        body, name=name, grid=(T // tr,),
        in_specs=[row, row, vec, row], out_specs=[row, row, vec],
        out_shape=[jax.ShapeDtypeStruct((T, D), F32), jax.ShapeDtypeStruct((T, D), BF16),
                   jax.ShapeDtypeStruct((1, D), F32)],
        compiler_params=_params(dimension_semantics=("arbitrary",)),
    )(dy, h, g, dres)


def _loss_head(h, g, target):
    T, D = h.shape
    tr = _row_tile(T)

    def body(h_ref, g_ref, t_ref, dh_ref, dhb_ref, dg_ref, loss_ref):
        @pl.when(pl.program_id(0) == 0)
        def _():
            dg_ref[...] = jnp.zeros_like(dg_ref)
            loss_ref[...] = jnp.zeros_like(loss_ref)

        hv = h_ref[...]
        r = lax.rsqrt(_mean(hv * hv) + EPS)
        hn = hv * r
        gv = g_ref[...]
        diff = hn * gv - t_ref[...]
        loss_ref[...] += 0.5 * jnp.sum(_mean(diff * diff))
        dyv = diff * (1.0 / D)
        dg_ref[...] += jnp.sum(dyv * hn, axis=0, keepdims=True)
        dyg = dyv * gv
        dh = r * (dyg - hn * _mean(dyg * hn))
        dh_ref[...] = dh
        dhb_ref[...] = dh.astype(BF16)

    row = pl.BlockSpec((tr, D), lambda i: (i, 0))
    vec = pl.BlockSpec((1, D), lambda i: (0, 0))
    return pl.pallas_call(
        body, name="loss_head", grid=(T // tr,),
        in_specs=[row, vec, row], out_specs=[row, row, vec, pl.BlockSpec((8, 128), lambda i: (0, 0))],
        out_shape=[jax.ShapeDtypeStruct((T, D), F32), jax.ShapeDtypeStruct((T, D), BF16),
                   jax.ShapeDtypeStruct((1, D), F32), jax.ShapeDtypeStruct((8, 128), F32)],
        compiler_params=_params(dimension_semantics=("arbitrary",)),
    )(h, g, target)


def _gate_tile(T):
    return min(T, 256)


def _gate_fwd(a, ln_g, ln_b, w_tril, b_rows):
    T, W2 = a.shape
    W = W2 // 2
    G = W // GROUP_DIM
    tr = _gate_tile(T)

    def body(a_ref, lng_ref, lnb_ref, w_ref, b_ref, z_ref):
        u = _gelu(a_ref[:, :W].astype(F32))
        vg = _gelu(a_ref[:, W:].astype(F32))
        xc = vg - _mean(vg)
        vn = xc * lax.rsqrt(_mean(xc * xc) + EPS)
        vl = (vn * lng_ref[...] + lnb_ref[...]).astype(BF16)
        for n in range(tr // CHUNK):
            rows = slice(n * CHUNK, (n + 1) * CHUNK)
            for g in range(G):
                cols = slice(g * GROUP_DIM, (g + 1) * GROUP_DIM)
                gate = _dot(w_ref[g], vl[rows, cols]) + b_ref[g]
                z_ref[rows, cols] = (u[rows, cols] * gate).astype(BF16)

    vec = pl.BlockSpec((1, W), lambda i: (0, 0))
    grp = pl.BlockSpec((G, CHUNK, CHUNK), lambda i: (0, 0, 0))
    return pl.pallas_call(
        body, name="gate_fwd", grid=(T // tr,),
        in_specs=[pl.BlockSpec((tr, W2), lambda i: (i, 0)), vec, vec, grp, grp],
        out_specs=pl.BlockSpec((tr, W), lambda i: (i, 0)),
        out_shape=jax.ShapeDtypeStruct((T, W), BF16),
        compiler_params=_params(dimension_semantics=("parallel",)),
    )(a, ln_g, ln_b, w_tril, b_rows)


def _gate_bwd(a, dz, ln_g, ln_b, w_tril, w_tril_t, b_rows):
    T, W2 = a.shape
    W = W2 // 2
    G = W // GROUP_DIM
    tr = _gate_tile(T)
    steps = T // tr

    def body(a_ref, dz_ref, lng_ref, lnb_ref, w_ref, wt_ref, b_ref, da_ref, dlng_ref, dlnb_ref, dw_ref, dbs_ref, dvl_ref):
        step = pl.program_id(0)

        @pl.when(step == 0)
        def _():
            dlng_ref[...] = jnp.zeros_like(dlng_ref)
            dlnb_ref[...] = jnp.zeros_like(dlnb_ref)
            dw_ref[...] = jnp.zeros_like(dw_ref)
            dbs_ref[...] = jnp.zeros_like(dbs_ref)

        au = a_ref[:, :W].astype(F32)
        av = a_ref[:, W:].astype(F32)
        u = _gelu(au)
        vg = _gelu(av)
        xc = vg - _mean(vg)
        rstd = lax.rsqrt(_mean(xc * xc) + EPS)
        vn = xc * rstd
        lng = lng_ref[...]
        vl = (vn * lng + lnb_ref[...]).astype(BF16)
        du_scale = dz_ref[...] * _gelu_grad(au)
        dgate_all = dz_ref[...] * u
        for n in range(tr // CHUNK):
            rows = slice(n * CHUNK, (n + 1) * CHUNK)
            for g in range(G):
                cols = slice(g * GROUP_DIM, (g + 1) * GROUP_DIM)
                vlg = vl[rows, cols]
                gate = _dot(w_ref[g], vlg) + b_ref[g]
                da_ref[rows, cols] = (du_scale[rows, cols] * gate).astype(BF16)
                dgate = dgate_all[rows, cols]
                dbs_ref[g] += dgate
                dgate_b = dgate.astype(BF16)
                dw_ref[g] += _dot(dgate_b, vlg, NT)
                dvl_ref[rows, cols] = _dot(wt_ref[g], dgate_b)
        dvl = dvl_ref[...]
        dlnb_ref[...] += jnp.sum(dvl, axis=0, keepdims=True)
        dlng_ref[...] += jnp.sum(dvl * vn, axis=0, keepdims=True)
        dvn = dvl * lng
        dvg = rstd * (dvn - _mean(dvn) - vn * _mean(dvn * vn))
        da_ref[:, W:] = (dvg * _gelu_grad(av)).astype(BF16)

        @pl.when(step == steps - 1)
        def _():
            t_idx = lax.broadcasted_iota(jnp.int32, (CHUNK, CHUNK), 0)
            s_idx = lax.broadcasted_iota(jnp.int32, (CHUNK, CHUNK), 1)
            for g in range(G):
                dw_ref[g] = jnp.where(s_idx <= t_idx, dw_ref[g], 0.0)
                dbs_ref[g] = jnp.broadcast_to(jnp.sum(dbs_ref[g], axis=-1, keepdims=True), (CHUNK, CHUNK))

    vec = pl.BlockSpec((1, W), lambda i: (0, 0))
    grp = pl.BlockSpec((G, CHUNK, CHUNK), lambda i: (0, 0, 0))
    return pl.pallas_call(
        body, name="gate_bwd", grid=(steps,),
        in_specs=[pl.BlockSpec((tr, W2), lambda i: (i, 0)), pl.BlockSpec((tr, W), lambda i: (i, 0)),
                  vec, vec, grp, grp, grp],
        out_specs=[pl.BlockSpec((tr, W2), lambda i: (i, 0)), vec, vec, grp, grp],
        out_shape=[jax.ShapeDtypeStruct((T, W2), BF16), jax.ShapeDtypeStruct((1, W), F32),
                   jax.ShapeDtypeStruct((1, W), F32), jax.ShapeDtypeStruct((G, CHUNK, CHUNK), F32),
                   jax.ShapeDtypeStruct((G, CHUNK, CHUNK), F32)],
        scratch_shapes=[pltpu.VMEM((tr, W), F32)],
        compiler_params=_params(dimension_semantics=("arbitrary",)),
    )(a, dz, ln_g, ln_b, w_tril, w_tril_t, b_rows)


def _bucket_map(dilation):
    rel = BLK + np.arange(BLK)[:, None] - np.arange(2 * BLK)[None, :]
    dist = np.clip(rel, 0, BLK) * dilation
    nf = np.maximum(dist, 1).astype(np.float32)
    large = MAX_EXACT + (np.log(nf / np.float32(MAX_EXACT)) / np.float32(math.log(REL_MAX_DISTANCE / MAX_EXACT))
                         * np.float32(N_BUCKETS - MAX_EXACT)).astype(np.int32)
    large = np.minimum(large, N_BUCKETS - 1)
    return np.where(dist < MAX_EXACT, dist, large).astype(np.int32)


def _bucket_maps():
    return jnp.asarray(np.stack([_bucket_map(d) for d in DILATIONS]))


def _bias_build(rel_bias, buckets):
    NG = len(DILATIONS)

    def body(table_ref, bucket_ref, out_ref):
        out_ref[...] = jnp.zeros_like(out_ref)
        for g in range(NG):
            bk = bucket_ref[g]
            for b in range(N_BUCKETS):
                hit = bk == b
                for h in range(ATT_HEADS):
                    out_ref[g, h] = jnp.where(hit, table_ref[b, g * ATT_HEADS + h], out_ref[g, h])

    return pl.pallas_call(
        body, name="bias_build",
        in_specs=[pl.BlockSpec(memory_space=pltpu.SMEM), pl.BlockSpec(memory_space=pltpu.VMEM)],
        out_specs=pl.BlockSpec(memory_space=pltpu.VMEM),
        out_shape=jax.ShapeDtypeStruct((NG, ATT_HEADS, BLK, 2 * BLK), F32),
        compiler_params=_params(),
    )(rel_bias, buckets)


def _bias_scatter(dbias, buckets):
    NG = len(DILATIONS)

    def body(dbias_ref, bucket_ref, out_ref):
        for g in range(NG):
            bk = bucket_ref[g]
            for b in range(N_BUCKETS):
                hit = bk == b
                for h in range(ATT_HEADS):
                    out_ref[b, g * ATT_HEADS + h] = jnp.sum(jnp.where(hit, dbias_ref[g, h], 0.0))

    return pl.pallas_call(
        body, name="bias_scatter",
        in_specs=[pl.BlockSpec(memory_space=pltpu.VMEM), pl.BlockSpec(memory_space=pltpu.VMEM)],
        out_specs=pl.BlockSpec(memory_space=pltpu.SMEM),
        out_shape=jax.ShapeDtypeStruct((N_BUCKETS, NG * ATT_HEADS), F32),
        compiler_params=_params(),
    )(dbias, buckets)


def _window_mask(first):
    qi = lax.broadcasted_iota(jnp.int32, (BLK, 2 * BLK), 0)
    kj = lax.broadcasted_iota(jnp.int32, (BLK, 2 * BLK), 1)
    rel = BLK + qi - kj
    return (rel >= 0) & (rel <= BLK) & (kj >= BLK * first)


def _head_lanes(hh):
    lane = lax.broadcasted_iota(jnp.int32, (1, PAIR), 1)
    return (lane >= hh * HEAD_DIM) & (lane < (hh + 1) * HEAD_DIM)


def _attn_fwd(name, g, q, k, v, qc, kc, vc, bias, blocks_per_residue):
    T = q.shape[0]
    nb = T // BLK
    scale = HEAD_DIM ** -0.5

    def body(q_ref, kp_ref, kc_ref, vp_ref, vc_ref, bias_ref, o_ref, lse_ref):
        b = pl.program_id(0)
        valid = _window_mask((b % blocks_per_residue == 0).astype(jnp.int32))
        low = _head_lanes(0)
        for hp in range(ATT_HEADS // 2):
            cols = slice(hp * PAIR, (hp + 1) * PAIR)
            qp = q_ref[:, cols]
            kk = jnp.concatenate([kp_ref[:, cols], kc_ref[:, cols]], axis=0)
            vv = jnp.concatenate([vp_ref[:, cols], vc_ref[:, cols]], axis=0)
            o_h, lse_h = [], []
            for hh in range(2):
                qm = jnp.where(_head_lanes(hh), qp, jnp.zeros_like(qp))
                s = _dot(qm, kk, NT) * scale
                logits = jnp.where(valid, s + bias_ref[g, 2 * hp + hh], NEG_INF)
                m = jnp.max(logits, axis=-1, keepdims=True)
                p = jnp.exp(logits - m)
                den = jnp.sum(p, axis=-1, keepdims=True)
                o_h.append(_dot(p.astype(BF16), vv) / den)
                lse_h.append(m + jnp.log(den))
            o_ref[:, cols] = jnp.where(low, o_h[0], o_h[1])
            lse_ref[:, cols] = jnp.where(low, lse_h[0], lse_h[1])

    def cur(c):
        return pl.BlockSpec((BLK, ATT_WIDTH), lambda b: (b, c))

    def prev(c):
        return pl.BlockSpec((BLK, ATT_WIDTH), lambda b: (jnp.maximum(b - 1, 0), c))

    out = pl.BlockSpec((BLK, ATT_WIDTH), lambda b: (b, 0))
    return pl.pallas_call(
        body, name=name, grid=(nb,),
        in_specs=[cur(qc), prev(kc), cur(kc), prev(vc), cur(vc),
                  pl.BlockSpec(bias.shape, lambda b: (0, 0, 0, 0))],
        out_specs=[out, out],
        out_shape=[jax.ShapeDtypeStruct((T, ATT_WIDTH), F32), jax.ShapeDtypeStruct((T, ATT_WIDTH), F32)],
        compiler_params=_params(dimension_semantics=("parallel",)),
    )(q, k, k, v, v, bias)


def _attn_merge(outs, lses):
    T = outs[0].shape[0]
    tr = _row_tile(T)
    n = len(outs)

    def body(*refs):
        o_refs, l_refs = refs[:n], refs[n:2 * n]
        o_ref, lse_ref = refs[2 * n], refs[2 * n + 1]
        ls = [r[...] for r in l_refs]
        m = functools.reduce(jnp.maximum, ls)
        es = [jnp.exp(l - m) for l in ls]
        tot = functools.reduce(lambda x, y: x + y, es)
        acc = functools.reduce(lambda x, y: x + y, [e * r[...] for e, r in zip(es, o_refs)])
        o_ref[...] = (acc / tot).astype(BF16)
        lse_ref[...] = m + jnp.log(tot)

    row = pl.BlockSpec((tr, ATT_WIDTH), lambda i: (i, 0))
    return pl.pallas_call(
        body, name="attn_merge", grid=(T // tr,),
        in_specs=[row] * (2 * n), out_specs=[row, row],
        out_shape=[jax.ShapeDtypeStruct((T, ATT_WIDTH), BF16), jax.ShapeDtypeStruct((T, ATT_WIDTH), F32)],
        compiler_params=_params(dimension_semantics=("parallel",)),
    )(*outs, *lses)


def _attn_bwd(name, g, q, k, v, qc, kc, vc, do, o, lse, bias, blocks_per_residue):
    T = q.shape[0]
    nb = T // BLK
    scale = HEAD_DIM ** -0.5

    def body(q_ref, kp_ref, kc_ref, vp_ref, vc_ref, do_ref, o_ref, lse_ref, bias_ref,
             dq_ref, dk_ref, dv_ref, db_ref, ck_ref, cv_ref):
        b = pl.program_id(0)

        @pl.when(b == 0)
        def _():
            db_ref[...] = jnp.zeros_like(db_ref)
            ck_ref[...] = jnp.zeros_like(ck_ref)
            cv_ref[...] = jnp.zeros_like(cv_ref)

        @pl.when(b == nb)
        def _():
            dk_ref[...] = ck_ref[...].astype(BF16)
            dv_ref[...] = cv_ref[...].astype(BF16)

        @pl.when(b < nb)
        def _():
            valid = _window_mask((b % blocks_per_residue == 0).astype(jnp.int32))
            for hp in range(ATT_HEADS // 2):
                cols = slice(hp * PAIR, (hp + 1) * PAIR)
                qp = q_ref[:, cols]
                kk = jnp.concatenate([kp_ref[:, cols], kc_ref[:, cols]], axis=0)
                vv = jnp.concatenate([vp_ref[:, cols], vc_ref[:, cols]], axis=0)
                dop = do_ref[:, cols]
                lsep = lse_ref[:, cols]
                prod = dop.astype(F32) * o_ref[:, cols].astype(F32)
                dq = jnp.zeros((BLK, PAIR), F32)
                dk = jnp.zeros((2 * BLK, PAIR), F32)
                dv = jnp.zeros((2 * BLK, PAIR), F32)
                for hh in range(2):
                    lanes = _head_lanes(hh)
                    qm = jnp.where(lanes, qp, jnp.zeros_like(qp))
                    dom = jnp.where(lanes, dop, jnp.zeros_like(dop))
                    km = jnp.where(lanes, kk, jnp.zeros_like(kk))
                    delta = jnp.sum(jnp.where(lanes, prod, 0.0), axis=-1, keepdims=True)
                    lse_h = jnp.max(jnp.where(lanes, lsep, NEG_INF), axis=-1, keepdims=True)
                    s = _dot(qm, kk, NT) * scale
                    logits = jnp.where(valid, s + bias_ref[g, 2 * hp + hh], NEG_INF)
                    p = jnp.exp(logits - lse_h)
                    dv += _dot(p.astype(BF16), dom, TN)
                    ds = p * (_dot(dom, vv, NT) - delta)
                    db_ref[2 * hp + hh] += ds
                    dss = (ds * scale).astype(BF16)
                    dq += _dot(dss, km)
                    dk += _dot(dss, qm, TN)
                dq_ref[:, cols] = dq.astype(BF16)
                dk_ref[:, cols] = (ck_ref[:, cols] + dk[:BLK]).astype(BF16)
                dv_ref[:, cols] = (cv_ref[:, cols] + dv[:BLK]).astype(BF16)
                ck_ref[:, cols] = dk[BLK:]
                cv_ref[:, cols] = dv[BLK:]

    last = nb - 1

    def cur(c):
        return pl.BlockSpec((BLK, ATT_WIDTH), lambda b: (jnp.minimum(b, last), c))

    def prev(c):
        return pl.BlockSpec((BLK, ATT_WIDTH), lambda b: (jnp.clip(b - 1, 0, last), c))

    dbias_shape = (ATT_HEADS, BLK, 2 * BLK)
    return pl.pallas_call(
        body, name=name, grid=(nb + 1,),
        in_specs=[cur(qc), prev(kc), cur(kc), prev(vc), cur(vc), cur(0), cur(0), cur(0),
                  pl.BlockSpec(bias.shape, lambda b: (0, 0, 0, 0))],
        out_specs=[cur(0), prev(0), prev(0), pl.BlockSpec(dbias_shape, lambda b: (0, 0, 0))],
        out_shape=[jax.ShapeDtypeStruct((T, ATT_WIDTH), BF16)] * 3 + [jax.ShapeDtypeStruct(dbias_shape, F32)],
        scratch_shapes=[pltpu.VMEM((BLK, ATT_WIDTH), F32), pltpu.VMEM((BLK, ATT_WIDTH), F32)],
        compiler_params=_params(dimension_semantics=("arbitrary",)),
    )(q, k, k, v, v, do, o, lse, bias)


def _residue_major(a, d):
    T, C = a.shape
    return a.reshape(T // d, d, C).transpose(1, 0, 2).reshape(T, C)


def _position_major(a, d):
    T, C = a.shape
    return a.reshape(d, T // d, C).transpose(1, 0, 2).reshape(T, C)


def _group_operands(qkv, g, d):
    NG = len(DILATIONS)
    if d == 1:
        return [(qkv, part * NG + g) for part in range(3)]
    cols = lambda part: qkv[:, (part * NG + g) * ATT_WIDTH:(part * NG + g + 1) * ATT_WIDTH]
    return [(_residue_major(cols(part), d), 0) for part in range(3)]


def _attention_fwd(qkv, bias):
    T = qkv.shape[0]
    outs, lses = [], []
    for g, d in enumerate(DILATIONS):
        (q, qc), (k, kc), (v, vc) = _group_operands(qkv, g, d)
        o_g, lse_g = _attn_fwd(f"attn_fwd_{g}", g, q, k, v, qc, kc, vc, bias, T // (d * BLK))
        if d > 1:
            o_g, lse_g = _position_major(o_g, d), _position_major(lse_g, d)
        outs.append(o_g)
        lses.append(lse_g)
    return _attn_merge(outs, lses)


def _attention_bwd(qkv, do, o, lse, bias):
    T = qkv.shape[0]
    dqs, dks, dvs, dbs = [], [], [], []
    for g, d in enumerate(DILATIONS):
        (q, qc), (k, kc), (v, vc) = _group_operands(qkv, g, d)
        do_g, o_g, lse_g = (do, o, lse) if d == 1 else tuple(_residue_major(t, d) for t in (do, o, lse))
        dq, dk, dv, db = _attn_bwd(f"attn_bwd_{g}", g, q, k, v, qc, kc, vc, do_g, o_g, lse_g, bias, T // (d * BLK))
        if d > 1:
            dq, dk, dv = (_position_major(t, d) for t in (dq, dk, dv))
        dqs.append(dq)
        dks.append(dk)
        dvs.append(dv)
        dbs.append(db)
    return jnp.concatenate(dqs + dks + dvs, axis=1), jnp.stack(dbs)


def _other_chips(x, y):
    return [(1 - x, y), (x, 1 - y), (1 - x, 1 - y)]


def _shard_region(ref, shape, by_cols, chip, rows=None):
    R, C = shape
    start, size = (0, R) if rows is None else rows
    if by_cols:
        return ref.at[pl.ds(start, size), pl.ds(chip * C, C)]
    return ref.at[pl.ds(chip * R + start, size), :]


def _allgather_weights(shards, by_cols):
    flat = [(i, l) for i, s in enumerate(shards) for l in range(s.shape[0])]
    n, nw = len(flat), len(shards)

    def body(*refs):
        srcs, outs = refs[:nw], refs[nw:nw + n]
        send_sems, recv_sems, local_sems = refs[nw + n:]
        x, y, c = lax.axis_index("x"), lax.axis_index("y"), lax.axis_index("c")
        sibling = (x, y, 1 - c)
        chips = _other_chips(x, y)

        def landing(f, px, py, pc):
            i, _ = flat[f]
            _, R, C = shards[i].shape
            return _shard_region(outs[f], (R, C), by_cols[i], 2 * px + py, rows=(pc * (R // 2), R // 2))

        def my_half(f):
            i, l = flat[f]
            R = shards[i].shape[1]
            return srcs[i].at[l, pl.ds(c * (R // 2), R // 2), :]

        def copy(f, k, block, to, src=None):
            dst = landing(f, *block)
            return pltpu.make_async_remote_copy(
                src_ref=dst if src is None else src, dst_ref=dst,
                send_sem=send_sems.at[6 * f + k], recv_sem=recv_sems.at[6 * f + k],
                device_id=to, device_id_type=MESH)

        mine = []
        for f, (i, l) in enumerate(flat):
            dst = _shard_region(outs[f], shards[i].shape[1:], by_cols[i], 2 * x + y)
            mine.append(pltpu.make_async_copy(srcs[i].at[l], dst, local_sems.at[f]))
        for cp in mine:
            cp.start()
        first = [copy(f, j, (x, y, c), (*chip, c), src=my_half(f)) for j, chip in enumerate(chips) for f in range(n)]
        for cp in first:
            cp.start()
        passed = []
        for j, chip in enumerate(chips):
            for f in range(n):
                copy(f, j, (*chip, c), (x, y, c)).wait_recv()
                cp = copy(f, 3 + j, (*chip, c), sibling)
                cp.start()
                passed.append(cp)
        for j, chip in enumerate(chips):
            for f in range(n):
                copy(f, 3 + j, (*chip, 1 - c), (x, y, c)).wait_recv()
        for cp in first + passed:
            cp.wait_send()
        for cp in mine:
            cp.wait()

    def whole(i):
        _, R, C = shards[i].shape
        return (R, N_CHIPS * C) if by_cols[i] else (N_CHIPS * R, C)

    return pl.pallas_call(
        body, name="allgather_weights",
        in_specs=[ANY] * nw, out_specs=[ANY] * n,
        out_shape=[jax.ShapeDtypeStruct(whole(i), shards[i].dtype) for i, _ in flat],
        scratch_shapes=[pltpu.SemaphoreType.DMA((6 * n,)), pltpu.SemaphoreType.DMA((6 * n,)),
                        pltpu.SemaphoreType.DMA((n,))],
        compiler_params=_params(),
    )(*shards)


def _scatter_grads(grads, shard_shapes, by_cols):
    flat = [(i, l) for i, g in enumerate(grads) for l in range(len(g))]
    n, nw = len(flat), len(grads)

    def body(*refs):
        srcs, outs = refs[:n], refs[n:n + nw]
        send_sems, recv_sems, local_sems = refs[n + nw:]
        x, y, c = lax.axis_index("x"), lax.axis_index("y"), lax.axis_index("c")
        me = 2 * x + y
        chips = _other_chips(x, y)

        def piece(f, chip):
            i, _ = flat[f]
            return _shard_region(srcs[f], shard_shapes[i][1:], by_cols[i], chip)

        def copy(f, j):
            i, l = flat[f]
            px, py = chips[j]
            return pltpu.make_async_remote_copy(
                src_ref=piece(f, 2 * px + py), dst_ref=outs[i].at[l, me],
                send_sem=send_sems.at[3 * f + j], recv_sem=recv_sems.at[3 * f + j],
                device_id=(px, py, c), device_id_type=MESH)

        mine = [pltpu.make_async_copy(piece(f, me), outs[i].at[l, me], local_sems.at[f])
                for f, (i, l) in enumerate(flat)]
        sends = [copy(f, j) for j in range(3) for f in range(n)]
        for cp in mine + sends:
            cp.start()
        for cp in sends:
            cp.wait()
        for cp in mine:
            cp.wait()

    return pl.pallas_call(
        body, name="scatter_grads",
        in_specs=[ANY] * n, out_specs=[ANY] * nw,
        out_shape=[jax.ShapeDtypeStruct((s[0], N_CHIPS) + tuple(s[1:]), BF16) for s in shard_shapes],
        scratch_shapes=[pltpu.SemaphoreType.DMA((3 * n,)), pltpu.SemaphoreType.DMA((3 * n,)),
                        pltpu.SemaphoreType.DMA((n,))],
        compiler_params=_params(),
    )(*[g for gs in grads for g in gs])


def _exchange_sibling(parts):
    n = len(parts)

    def body(*refs):
        srcs, outs = refs[:n], refs[n:2 * n]
        send_sems, recv_sems = refs[2 * n:]
        sibling = (lax.axis_index("x"), lax.axis_index("y"), 1 - lax.axis_index("c"))
        copies = [pltpu.make_async_remote_copy(src_ref=srcs[i], dst_ref=outs[i], send_sem=send_sems.at[i],
                                               recv_sem=recv_sems.at[i], device_id=sibling, device_id_type=MESH)
                  for i in range(n)]
        for cp in copies:
            cp.start()
        for cp in copies:
            cp.wait()

    return pl.pallas_call(
        body, name="exchange_sibling",
        in_specs=[ANY] * n, out_specs=[ANY] * n,
        out_shape=[jax.ShapeDtypeStruct(s.shape, s.dtype) for s in parts],
        scratch_shapes=[pltpu.SemaphoreType.DMA((n,)), pltpu.SemaphoreType.DMA((n,))],
        compiler_params=_params(),
    )(*parts)


def _allgather_small(block):
    m_per, ncol = block.shape

    def body(x_ref, out_ref, send_sems, recv_sems, local_sem):
        x, y, c = lax.axis_index("x"), lax.axis_index("y"), lax.axis_index("c")
        me, sibling = (x, y, c), (x, y, 1 - c)
        chips = _other_chips(x, y)

        def rows(px, py, pc):
            return out_ref.at[4 * px + 2 * py + pc]

        def copy(k, block_of, to, src=None):
            return pltpu.make_async_remote_copy(
                src_ref=rows(*block_of) if src is None else src, dst_ref=rows(*block_of),
                send_sem=send_sems.at[k], recv_sem=recv_sems.at[k], device_id=to, device_id_type=MESH)

        mine = pltpu.make_async_copy(x_ref, rows(*me), local_sem)
        mine.start()
        first = [copy(0, me, sibling, src=x_ref)]
        first += [copy(1 + j, me, (*chip, c), src=x_ref) for j, chip in enumerate(chips)]
        for cp in first:
            cp.start()
        passed = [copy(4 + j, (*chip, c), sibling) for j, chip in enumerate(chips)]
        for j, chip in enumerate(chips):
            copy(1 + j, (*chip, c), me).wait_recv()
            passed[j].start()
        copy(0, sibling, me).wait_recv()
        for j, chip in enumerate(chips):
            copy(4 + j, (*chip, 1 - c), me).wait_recv()
        for cp in first + passed:
            cp.wait_send()
        mine.wait()

    return pl.pallas_call(
        body, name="allgather_small",
        in_specs=[pl.BlockSpec(memory_space=pltpu.VMEM)], out_specs=pl.BlockSpec(memory_space=pltpu.VMEM),
        out_shape=jax.ShapeDtypeStruct((N_DEV, m_per, ncol), block.dtype),
        scratch_shapes=[pltpu.SemaphoreType.DMA((7,)), pltpu.SemaphoreType.DMA((7,)), pltpu.SemaphoreType.DMA],
        compiler_params=_params(),
    )(block)


def _adamw(w, g, m, v):
    m = ADAM_B1 * m + (1.0 - ADAM_B1) * g
    v = ADAM_B2 * v + (1.0 - ADAM_B2) * jnp.square(g)
    m_hat = m / (1.0 - ADAM_B1 ** ADAM_STEP)
    v_hat = v / (1.0 - ADAM_B2 ** ADAM_STEP)
    delta = -ADAM_LR * (m_hat / (jnp.sqrt(v_hat) + ADAM_EPS) + ADAM_WD * w)
    return delta, m, v


def _flat_tile(rows):
    return min(rows, 256)


def _sum_pieces(name, pieces):
    L, P, R, C = pieces.shape
    tr = _flat_tile(R)

    def body(p_ref, out_ref):
        acc = p_ref[0].astype(F32)
        for j in range(1, P):
            acc = acc + p_ref[j].astype(F32)
        out_ref[...] = acc

    return pl.pallas_call(
        body, name=name, grid=(L, R // tr),
        in_specs=[pl.BlockSpec((None, P, tr, C), lambda l, i: (l, 0, i, 0))],
        out_specs=pl.BlockSpec((None, tr, C), lambda l, i: (l, i, 0)),
        out_shape=jax.ShapeDtypeStruct((L, R, C), F32),
        compiler_params=_params(dimension_semantics=("parallel", "parallel")),
    )(pieces)


def _adam_pair(name, w, m, v, part_a, part_b):
    L, R, C = w.shape
    tr = _flat_tile(R)

    def body(w_ref, m_ref, v_ref, a_ref, b_ref, g_ref, d_ref, nm_ref, nv_ref):
        g = a_ref[...] + b_ref[...]
        g_ref[...] = g
        d_ref[...], nm_ref[...], nv_ref[...] = _adamw(w_ref[...], g, m_ref[...], v_ref[...])

    row = pl.BlockSpec((None, tr, C), lambda l, i: (l, i, 0))
    return pl.pallas_call(
        body, name=name, grid=(L, R // tr),
        in_specs=[row] * 5, out_specs=[row] * 4,
        out_shape=[jax.ShapeDtypeStruct((L, R, C), F32)] * 4,
        compiler_params=_params(dimension_semantics=("parallel", "parallel")),
    )(w, m, v, part_a, part_b)


def _adam_small(w, m, v, gathered):
    R, C = w.shape

    def body(w_ref, m_ref, v_ref, p_ref, g_ref, d_ref, nm_ref, nv_ref):
        g = p_ref[0]
        for j in range(1, N_DEV):
            g = g + p_ref[j]
        g_ref[...] = g
        d_ref[...], nm_ref[...], nv_ref[...] = _adamw(w_ref[...], g, m_ref[...], v_ref[...])

    return pl.pallas_call(
        body, name="adam_small",
        out_shape=[jax.ShapeDtypeStruct((R, C), F32)] * 4,
        compiler_params=_params(),
    )(w, m, v, gathered)


SMALL = ("mix_norm_g", "mlp_norm_g", "final_norm_g", "a_ln_g", "a_ln_b", "a_w_s", "a_b_s", "rel_bias")


def _pack_small(arrays, width):
    rows = []
    for a in arrays:
        flat = a.reshape(-1)
        pad = (-flat.shape[0]) % width
        rows.append(jnp.pad(flat, (0, pad)).reshape(-1, width))
    block = jnp.concatenate(rows, axis=0)
    return jnp.pad(block, ((0, (-block.shape[0]) % 8), (0, 0)))


def _unpack_small(block, shapes, width):
    out, row = [], 0
    for shape in shapes:
        size = int(np.prod(shape))
        nrows = -(-size // width)
        out.append(block[row:row + nrows].reshape(-1)[:size].reshape(shape))
        row += nrows
    return out


def kernel(x, mix_norm_g, mlp_norm_g, final_norm_g, a_w_in, a_ln_g, a_ln_b, a_w_s, a_b_s, a_w_out, b_w_qkv, b_w_out, rel_bias, w_up, w_down, loss_target, m_mix_norm_g, m_mlp_norm_g, m_final_norm_g, m_a_w_in, m_a_ln_g, m_a_ln_b, m_a_w_s, m_a_b_s, m_a_w_out, m_b_w_qkv, m_b_w_out, m_rel_bias, m_w_up, m_w_down, v_mix_norm_g, v_mlp_norm_g, v_final_norm_g, v_a_w_in, v_a_ln_g, v_a_ln_b, v_a_w_s, v_a_b_s, v_a_w_out, v_b_w_qkv, v_b_w_out, v_rel_bias, v_w_up, v_w_down):
    T, D = x.shape[1], x.shape[2]
    h0 = x.reshape(T, D)
    target = loss_target.reshape(T, D)
    G = a_w_s.shape[1]

    w_big = [a_w_in, a_w_out, b_w_qkv, b_w_out, w_up, w_down]
    m_big = [m_a_w_in, m_a_w_out, m_b_w_qkv, m_b_w_out, m_w_up, m_w_down]
    v_big = [v_a_w_in, v_a_w_out, v_b_w_qkv, v_b_w_out, v_w_up, v_w_down]
    by_cols = [True, False, True, True, True, False]
    W_in, W_out, W_qkv, W_bo, W_up0, W_up1, W_dn0, W_dn1 = _allgather_weights([w.astype(BF16) for w in w_big], by_cols)
    W_dn = [W_dn0, W_dn1]
    W_up = [W_up0, W_up1]

    tril = jnp.tril(jnp.ones((CHUNK, CHUNK), dtype=bool))
    w_tril = jnp.where(tril[None], a_w_s[0], 0.0).astype(BF16)
    w_tril_t = jnp.swapaxes(w_tril, 1, 2)
    b_rows = jnp.broadcast_to(a_b_s[0][:, :, None], (G, CHUNK, CHUNK))
    buckets = _bucket_maps()
    bias = _bias_build(rel_bias, buckets)

    FF = W_up0.shape[1]
    QKV = W_qkv.shape[1]
    TM = 1024
    TK_WGRAD = 2048

    def matmul(name, a, b, mode, out, tm=TM, tn=1024, epi=None, extras=()):
        return _mm(name, a, b, mode, tm=tm, tn=tn, tk=a.shape[1], outs=[out], epi=epi, extras=extras)

    def wgrad(name, a, b, tn=1024):
        return _mm(name, a, b, "tn", tm=1024, tn=tn, tk=TK_WGRAD, outs=[BF16])

    y0 = _rms_fwd("rms_fwd_mix0", h0, mix_norm_g[0:1])
    a_pre = matmul("a_in", y0, W_in, "nn", BF16)
    z = _gate_fwd(a_pre, a_ln_g, a_ln_b, w_tril, b_rows)
    h1 = matmul("a_out", z, W_out, "nn", F32, epi=_epi_residual, extras=(h0,))

    def mlp_fwd(layer, h):
        y = _rms_fwd(f"rms_fwd_mlp{layer}", h, mlp_norm_g[layer:layer + 1])
        q = matmul(f"mlp_up{layer}", y, W_up[layer], "nn", BF16, epi=_epi_relu2)
        h_next = matmul(f"mlp_down{layer}", q, W_dn[layer], "nn", F32, tm=TM // 2, epi=_epi_residual, extras=(h,))
        return y, q, h_next

    y1, q1, h2 = mlp_fwd(0, h1)
    y2 = _rms_fwd("rms_fwd_mix1", h2, mix_norm_g[1:2])
    qkv = matmul("b_qkv", y2, W_qkv, "nn", BF16, tn=QKV // 4)
    o, lse = _attention_fwd(qkv, bias)
    h3 = matmul("b_out", o, W_bo, "nn", F32, epi=_epi_residual, extras=(h2,))
    y3, q3, h4 = mlp_fwd(1, h3)
    dh4, dh4_b, d_final_g, loss_tile = _loss_head(h4, final_norm_g.reshape(1, D), target)
    loss = lax.psum(loss_tile[0, 0], ("x", "y", "c"))

    def mlp_bwd(layer, h, y, q, dh_out, dh_out_b):
        dp = matmul(f"mlp_down_bwd{layer}", dh_out_b, W_dn[layer], "nt", BF16, epi=_epi_relu2_grad, extras=(q,))
        g_dn = wgrad(f"mlp_down_wgrad{layer}", q, dh_out_b)
        dy = matmul(f"mlp_up_bwd{layer}", dp, W_up[layer], "nt", F32, tm=TM // 2)
        g_up = wgrad(f"mlp_up_wgrad{layer}", y, dp)
        dh, dh_b, dg = _rms_bwd(f"rms_bwd_mlp{layer}", dy, h, mlp_norm_g[layer:layer + 1], dh_out)
        return dh, dh_b, dg, g_up, g_dn

    dh3, dh3_b, dg_mlp1, g_up1, g_dn1 = mlp_bwd(1, h3, y3, q3, dh4, dh4_b)
    do = matmul("b_out_bwd", dh3_b, W_bo, "nt", BF16)
    g_bo = wgrad("b_out_wgrad", o, dh3_b)
    dqkv, dbias = _attention_bwd(qkv, do, o, lse, bias)
    d_rel_bias = _bias_scatter(dbias, buckets)
    dy2 = matmul("b_qkv_bwd", dqkv, W_qkv, "nt", F32, tm=TM // 2)
    g_qkv = wgrad("b_qkv_wgrad", y2, dqkv, tn=QKV // 4)
    dh2, dh2_b, dg_mix1 = _rms_bwd("rms_bwd_mix1", dy2, h2, mix_norm_g[1:2], dh3)
    dh1, dh1_b, dg_mlp0, g_up0, g_dn0 = mlp_bwd(0, h1, y1, q1, dh2, dh2_b)
    dz = matmul("a_out_bwd", dh1_b, W_out, "nt", F32)
    g_out = wgrad("a_out_wgrad", z, dh1_b)
    da, d_ln_g, d_ln_b, d_w_s, d_b_s = _gate_bwd(a_pre, dz, a_ln_g, a_ln_b, w_tril, w_tril_t, b_rows)
    dy0 = matmul("a_in_bwd", da, W_in, "nt", F32)
    g_in = wgrad("a_in_wgrad", y0, da)
    grad_x, _, dg_mix0 = _rms_bwd("rms_bwd_mix0", dy0, h0, mix_norm_g[0:1], dh1)

    grads = [[g_in], [g_out], [g_qkv], [g_bo], [g_up0, g_up1], [g_dn0, g_dn1]]
    received = _scatter_grads(grads, [w.shape for w in w_big], by_cols)
    plane = [_sum_pieces(f"sum_pieces{i}", r) for i, r in enumerate(received)]
    other = _exchange_sibling(plane)
    big_out = [_adam_pair(f"adam{i}", w_big[i], m_big[i], v_big[i], plane[i], other[i]) for i in range(len(w_big))]

    def unbig(kind):
        return dict(zip(["a_w_in", "a_w_out", "b_w_qkv", "b_w_out", "w_up", "w_down"], [b[kind] for b in big_out]))

    small_w = [mix_norm_g, mlp_norm_g, final_norm_g, a_ln_g, a_ln_b, a_w_s, a_b_s, rel_bias]
    small_m = [m_mix_norm_g, m_mlp_norm_g, m_final_norm_g, m_a_ln_g, m_a_ln_b, m_a_w_s, m_a_b_s, m_rel_bias]
    small_v = [v_mix_norm_g, v_mlp_norm_g, v_final_norm_g, v_a_ln_g, v_a_ln_b, v_a_w_s, v_a_b_s, v_rel_bias]
    small_g = [jnp.concatenate([dg_mix0, dg_mix1]), jnp.concatenate([dg_mlp0, dg_mlp1]), d_final_g,
               d_ln_g, d_ln_b, d_w_s[None], d_b_s[None, :, :, 0], d_rel_bias]
    width = max(D, 128)
    gathered_small = _allgather_small(_pack_small(small_g, width))
    small_out = _adam_small(_pack_small(small_w, width), _pack_small(small_m, width), _pack_small(small_v, width),
                            gathered_small)
    shapes = [w.shape for w in small_w]

    names = ["mix_norm_g", "mlp_norm_g", "final_norm_g", "a_w_in", "a_ln_g", "a_ln_b", "a_w_s", "a_b_s", "a_w_out",
             "b_w_qkv", "b_w_out", "rel_bias", "w_up", "w_down"]
    results = [loss, grad_x.reshape(x.shape)]
    for kind in range(4):
        table = dict(zip(SMALL, _unpack_small(small_out[kind], shapes, width)))
        table.update(unbig(kind))
        results += [table[n] for n in names]
    return tuple(results)
```

```python
import functools
import math

import numpy as np
import jax
import jax.numpy as jnp
from jax import lax
from jax.experimental import pallas as pl
from jax.experimental.pallas import tpu as pltpu

F32 = jnp.float32
BF16 = jnp.bfloat16
MESH = pl.DeviceIdType.MESH
ANY = pl.BlockSpec(memory_space=pl.ANY)

N_CHIPS = 4
N_DEV = 8
VMEM_LIMIT_BYTES = 52 * 1024 * 1024

EPS = 1e-6
NEG_INF = -1e30
CHUNK = 128
GROUP_DIM = 128
HEAD_DIM = 64
ATT_HEADS = 8
ATT_WIDTH = ATT_HEADS * HEAD_DIM
PAIR = 2 * HEAD_DIM
BLK = 128
DILATIONS = (1, 4, 16)
N_BUCKETS = 32
MAX_EXACT = N_BUCKETS // 2
REL_MAX_DISTANCE = 2048

ADAM_LR = 0.001
ADAM_B1 = 0.9
ADAM_B2 = 0.999
ADAM_EPS = 1e-08
ADAM_WD = 0.01
ADAM_STEP = 10

NN = (((1,), (0,)), ((), ()))
NT = (((1,), (1,)), ((), ()))
TN = (((0,), (0,)), ((), ()))


def _params(**kw):
    return pltpu.CompilerParams(vmem_limit_bytes=VMEM_LIMIT_BYTES, **kw)


def _dot(a, b, dims=NN):
    return lax.dot_general(a, b, dims, preferred_element_type=F32)


def _gelu(x):
    return 0.5 * x * (1.0 + lax.erf(x * math.sqrt(0.5)))


def _gelu_grad(x):
    return 0.5 * (1.0 + lax.erf(x * math.sqrt(0.5))) + x * jnp.exp(-0.5 * x * x) * (1.0 / math.sqrt(2.0 * math.pi))


def _mean(x):
    return jnp.mean(x, axis=-1, keepdims=True)


class _Comm:
    def __init__(self, inputs, out_shapes, scratch, start, end, mid=None):
        self.inputs, self.out_shapes, self.scratch = list(inputs), list(out_shapes), list(scratch)
        self.start, self.mid, self.end = start, mid, end


def _run_comm(name, comm):
    n_in, n_out = len(comm.inputs), len(comm.out_shapes)

    def body(*refs):
        parts = refs[:n_in], refs[n_in:n_in + n_out], refs[n_in + n_out:]
        comm.start(*parts)
        if comm.mid is not None:
            comm.mid(*parts)
        comm.end(*parts)

    return pl.pallas_call(
        body, name=name, in_specs=[ANY] * n_in, out_specs=[ANY] * n_out, out_shape=comm.out_shapes,
        scratch_shapes=comm.scratch, compiler_params=_params(),
    )(*comm.inputs)


def _mm(name, a, b, mode, *, tm, tn, tk, outs, epi=None, extras=(), comm=None):
    if mode == "tn":
        K, M = a.shape
    else:
        M, K = a.shape
    N = b.shape[0] if mode == "nt" else b.shape[1]
    tm, tn, tk = min(tm, M), min(tn, N), min(tk, K)
    assert M % tm == 0 and N % tn == 0 and K % tk == 0, (name, M, N, K, tm, tn, tk)
    nk = K // tk
    grid = (M // tm, N // tn, nk)

    if mode == "tn":
        a_spec = pl.BlockSpec((tk, tm), lambda i, j, k: (k, i))
    else:
        a_spec = pl.BlockSpec((tm, tk), lambda i, j, k: (i, k))
    if mode == "nt":
        b_spec = pl.BlockSpec((tn, tk), lambda i, j, k: (j, k))
    else:
        b_spec = pl.BlockSpec((tk, tn), lambda i, j, k: (k, j))
    tile = pl.BlockSpec((tm, tn), lambda i, j, k: (i, j))
    out_shapes = [jax.ShapeDtypeStruct((M, N), dtype) for dtype in outs]
    out_specs = [tile for _ in outs]
    extra_specs = [tile for _ in extras]
    n_extra, n_out = len(extras), len(outs)
    n_cin = len(comm.inputs) if comm else 0
    n_cout = len(comm.out_shapes) if comm else 0
    dims = {"nn": NN, "nt": NT, "tn": TN}[mode]
    steps = grid[0] * grid[1] * grid[2]

    def body(*refs):
        a_ref, b_ref = refs[0], refs[1]
        pos = 2
        extra_refs = refs[pos:pos + n_extra]
        pos += n_extra
        comm_in = refs[pos:pos + n_cin]
        pos += n_cin
        out_refs = refs[pos:pos + n_out]
        pos += n_out
        comm_out = refs[pos:pos + n_cout]
        pos += n_cout
        acc_ref = refs[pos] if nk > 1 else None
        comm_sems = refs[pos + (nk > 1):]
        k = pl.program_id(2)
        step = (pl.program_id(0) * grid[1] + pl.program_id(1)) * nk + k

        if comm is not None:
            @pl.when(step == 0)
            def _():
                comm.start(comm_in, comm_out, comm_sems)

        part = _dot(a_ref[...].astype(BF16), b_ref[...].astype(BF16), dims)

        def finish(acc):
            res = epi(acc, *[e[...] for e in extra_refs]) if epi is not None else (acc,) * n_out
            for o, r in zip(out_refs, res):
                o[...] = r.astype(o.dtype)

        if nk == 1:
            finish(part)
        else:
            @pl.when(k == 0)
            def _():
                acc_ref[...] = part

            @pl.when(k > 0)
            def _():
                acc_ref[...] += part

            @pl.when(k == nk - 1)
            def _():
                finish(acc_ref[...])

        if comm is not None:
            if comm.mid is not None:
                @pl.when(step == steps // 2)
                def _():
                    comm.mid(comm_in, comm_out, comm_sems)

            @pl.when(step == steps - 1)
            def _():
                comm.end(comm_in, comm_out, comm_sems)

    order = ("arbitrary",) * 3 if comm else ("parallel", "parallel", "arbitrary")
    res = pl.pallas_call(
        body, name=name, grid=grid,
        in_specs=[a_spec, b_spec] + extra_specs + [ANY] * n_cin,
        out_specs=out_specs + [ANY] * n_cout,
        out_shape=out_shapes + (comm.out_shapes if comm else []),
        scratch_shapes=([pltpu.VMEM((tm, tn), F32)] if nk > 1 else []) + (comm.scratch if comm else []),
        compiler_params=_params(dimension_semantics=order),
    )(a, b, *extras, *(comm.inputs if comm else []))
    mm_out = res[0] if n_out == 1 else list(res[:n_out])
    return (mm_out, list(res[n_out:])) if comm else mm_out


def _epi_residual(acc, res):
    return (res + acc,)


def _epi_relu2(acc):
    return (jnp.square(jnp.maximum(acc, 0.0)),)


def _epi_relu2_grad(acc, q):
    return (acc * (2.0 * jnp.sqrt(q.astype(F32))),)


def _row_tile(T):
    return min(T, 512)


def _rms_fwd(name, h, g):
    T, D = h.shape
    tr = _row_tile(T)

    def body(h_ref, g_ref, y_ref):
        hv = h_ref[...]
        y = hv * lax.rsqrt(_mean(hv * hv) + EPS)
        y_ref[...] = (y * g_ref[...]).astype(BF16)

    return pl.pallas_call(
        body, name=name, grid=(T // tr,),
        in_specs=[pl.BlockSpec((tr, D), lambda i: (i, 0)), pl.BlockSpec((1, D), lambda i: (0, 0))],
        out_specs=pl.BlockSpec((tr, D), lambda i: (i, 0)),
        out_shape=jax.ShapeDtypeStruct((T, D), BF16),
        compiler_params=_params(dimension_semantics=("parallel",)),
    )(h, g)


def _rms_bwd(name, dy, h, g, dres):
    T, D = h.shape
    tr = _row_tile(T)

    def body(dy_ref, h_ref, g_ref, dres_ref, dh_ref, dhb_ref, dg_ref):
        @pl.when(pl.program_id(0) == 0)
        def _():
            dg_ref[...] = jnp.zeros_like(dg_ref)

        hv = h_ref[...]
        r = lax.rsqrt(_mean(hv * hv) + EPS)
        hn = hv * r
        dyv = dy_ref[...]
        dg_ref[...] += jnp.sum(dyv * hn, axis=0, keepdims=True)
        dyg = dyv * g_ref[...]
        dh = dres_ref[...] + r * (dyg - hn * _mean(dyg * hn))
        dh_ref[...] = dh
        dhb_ref[...] = dh.astype(BF16)

    row = pl.BlockSpec((tr, D), lambda i: (i, 0))
    vec = pl.BlockSpec((1, D), lambda i: (0, 0))
    return pl.pallas_call(
        body, name=name, grid=(T // tr,),
        in_specs=[row, row, vec, row], out_specs=[row, row, vec],
        out_shape=[jax.ShapeDtypeStruct((T, D), F32), jax.ShapeDtypeStruct((T, D), BF16),
                   jax.ShapeDtypeStruct((1, D), F32)],
        compiler_params=_params(dimension_semantics=("arbitrary",)),
    )(dy, h, g, dres)


def _loss_head(h, g, target):
    T, D = h.shape
    tr = _row_tile(T)

    def body(h_ref, g_ref, t_ref, dh_ref, dhb_ref, dg_ref, loss_ref):
        @pl.when(pl.program_id(0) == 0)
        def _():
            dg_ref[...] = jnp.zeros_like(dg_ref)
            loss_ref[...] = jnp.zeros_like(loss_ref)

        hv = h_ref[...]
        r = lax.rsqrt(_mean(hv * hv) + EPS)
        hn = hv * r
        gv = g_ref[...]
        diff = hn * gv - t_ref[...]
        loss_ref[...] += 0.5 * jnp.sum(_mean(diff * diff))
        dyv = diff * (1.0 / D)
        dg_ref[...] += jnp.sum(dyv * hn, axis=0, keepdims=True)
        dyg = dyv * gv
        dh = r * (dyg - hn * _mean(dyg * hn))
        dh_ref[...] = dh
        dhb_ref[...] = dh.astype(BF16)

    row = pl.BlockSpec((tr, D), lambda i: (i, 0))
    vec = pl.BlockSpec((1, D), lambda i: (0, 0))
    return pl.pallas_call(
        body, name="loss_head", grid=(T // tr,),
        in_specs=[row, vec, row], out_specs=[row, row, vec, pl.BlockSpec((8, 128), lambda i: (0, 0))],
        out_shape=[jax.ShapeDtypeStruct((T, D), F32), jax.ShapeDtypeStruct((T, D), BF16),
                   jax.ShapeDtypeStruct((1, D), F32), jax.ShapeDtypeStruct((8, 128), F32)],
        compiler_params=_params(dimension_semantics=("arbitrary",)),
    )(h, g, target)


def _gate_tile(T):
    return min(T, 256)


def _gate_fwd(a, ln_g, ln_b, w_tril, b_rows):
    T, W2 = a.shape
    W = W2 // 2
    G = W // GROUP_DIM
    tr = _gate_tile(T)

    def body(a_ref, lng_ref, lnb_ref, w_ref, b_ref, z_ref):
        u = _gelu(a_ref[:, :W].astype(F32))
        vg = _gelu(a_ref[:, W:].astype(F32))
        xc = vg - _mean(vg)
        vn = xc * lax.rsqrt(_mean(xc * xc) + EPS)
        vl = (vn * lng_ref[...] + lnb_ref[...]).astype(BF16)
        for n in range(tr // CHUNK):
            rows = slice(n * CHUNK, (n + 1) * CHUNK)
            for g in range(G):
                cols = slice(g * GROUP_DIM, (g + 1) * GROUP_DIM)
                gate = _dot(w_ref[g], vl[rows, cols]) + b_ref[g]
                z_ref[rows, cols] = (u[rows, cols] * gate).astype(BF16)

    vec = pl.BlockSpec((1, W), lambda i: (0, 0))
    grp = pl.BlockSpec((G, CHUNK, CHUNK), lambda i: (0, 0, 0))
    return pl.pallas_call(
        body, name="gate_fwd", grid=(T // tr,),
        in_specs=[pl.BlockSpec((tr, W2), lambda i: (i, 0)), vec, vec, grp, grp],
        out_specs=pl.BlockSpec((tr, W), lambda i: (i, 0)),
        out_shape=jax.ShapeDtypeStruct((T, W), BF16),
        compiler_params=_params(dimension_semantics=("parallel",)),
    )(a, ln_g, ln_b, w_tril, b_rows)


def _gate_bwd(a, dz, ln_g, ln_b, w_tril, w_tril_t, b_rows):
    T, W2 = a.shape
    W = W2 // 2
    G = W // GROUP_DIM
    tr = _gate_tile(T)
    steps = T // tr

    def body(a_ref, dz_ref, lng_ref, lnb_ref, w_ref, wt_ref, b_ref, da_ref, dlng_ref, dlnb_ref, dw_ref, dbs_ref, dvl_ref):
        step = pl.program_id(0)

        @pl.when(step == 0)
        def _():
            dlng_ref[...] = jnp.zeros_like(dlng_ref)
            dlnb_ref[...] = jnp.zeros_like(dlnb_ref)
            dw_ref[...] = jnp.zeros_like(dw_ref)
            dbs_ref[...] = jnp.zeros_like(dbs_ref)

        au = a_ref[:, :W].astype(F32)
        av = a_ref[:, W:].astype(F32)
        u = _gelu(au)
        vg = _gelu(av)
        xc = vg - _mean(vg)
        rstd = lax.rsqrt(_mean(xc * xc) + EPS)
        vn = xc * rstd
        lng = lng_ref[...]
        vl = (vn * lng + lnb_ref[...]).astype(BF16)
        du_scale = dz_ref[...] * _gelu_grad(au)
        dgate_all = dz_ref[...] * u
        for n in range(tr // CHUNK):
            rows = slice(n * CHUNK, (n + 1) * CHUNK)
            for g in range(G):
                cols = slice(g * GROUP_DIM, (g + 1) * GROUP_DIM)
                vlg = vl[rows, cols]
                gate = _dot(w_ref[g], vlg) + b_ref[g]
                da_ref[rows, cols] = (du_scale[rows, cols] * gate).astype(BF16)
                dgate = dgate_all[rows, cols]
                dbs_ref[g] += dgate
                dgate_b = dgate.astype(BF16)
                dw_ref[g] += _dot(dgate_b, vlg, NT)
                dvl_ref[rows, cols] = _dot(wt_ref[g], dgate_b)
        dvl = dvl_ref[...]
        dlnb_ref[...] += jnp.sum(dvl, axis=0, keepdims=True)
        dlng_ref[...] += jnp.sum(dvl * vn, axis=0, keepdims=True)
        dvn = dvl * lng
        dvg = rstd * (dvn - _mean(dvn) - vn * _mean(dvn * vn))
        da_ref[:, W:] = (dvg * _gelu_grad(av)).astype(BF16)

        @pl.when(step == steps - 1)
        def _():
            t_idx = lax.broadcasted_iota(jnp.int32, (CHUNK, CHUNK), 0)
            s_idx = lax.broadcasted_iota(jnp.int32, (CHUNK, CHUNK), 1)
            for g in range(G):
                dw_ref[g] = jnp.where(s_idx <= t_idx, dw_ref[g], 0.0)
                dbs_ref[g] = jnp.broadcast_to(jnp.sum(dbs_ref[g], axis=-1, keepdims=True), (CHUNK, CHUNK))

    vec = pl.BlockSpec((1, W), lambda i: (0, 0))
    grp = pl.BlockSpec((G, CHUNK, CHUNK), lambda i: (0, 0, 0))
    return pl.pallas_call(
        body, name="gate_bwd", grid=(steps,),
        in_specs=[pl.BlockSpec((tr, W2), lambda i: (i, 0)), pl.BlockSpec((tr, W), lambda i: (i, 0)),
                  vec, vec, grp, grp, grp],
        out_specs=[pl.BlockSpec((tr, W2), lambda i: (i, 0)), vec, vec, grp, grp],
        out_shape=[jax.ShapeDtypeStruct((T, W2), BF16), jax.ShapeDtypeStruct((1, W), F32),
                   jax.ShapeDtypeStruct((1, W), F32), jax.ShapeDtypeStruct((G, CHUNK, CHUNK), F32),
                   jax.ShapeDtypeStruct((G, CHUNK, CHUNK), F32)],
        scratch_shapes=[pltpu.VMEM((tr, W), F32)],
        compiler_params=_params(dimension_semantics=("arbitrary",)),
    )(a, dz, ln_g, ln_b, w_tril, w_tril_t, b_rows)


def _bucket_map(dilation):
    rel = BLK + np.arange(BLK)[:, None] - np.arange(2 * BLK)[None, :]
    dist = np.clip(rel, 0, BLK) * dilation
    nf = np.maximum(dist, 1).astype(np.float32)
    large = MAX_EXACT + (np.log(nf / np.float32(MAX_EXACT)) / np.float32(math.log(REL_MAX_DISTANCE / MAX_EXACT))
                         * np.float32(N_BUCKETS - MAX_EXACT)).astype(np.int32)
    large = np.minimum(large, N_BUCKETS - 1)
    return np.where(dist < MAX_EXACT, dist, large).astype(np.int32)


def _bucket_maps():
    return jnp.asarray(np.stack([_bucket_map(d) for d in DILATIONS]))


def _bias_build(rel_bias, buckets):
    NG = len(DILATIONS)

    def body(table_ref, bucket_ref, out_ref):
        out_ref[...] = jnp.zeros_like(out_ref)
        for g in range(NG):
            bk = bucket_ref[g]
            for b in range(N_BUCKETS):
                hit = bk == b
                for h in range(ATT_HEADS):
                    out_ref[g, h] = jnp.where(hit, table_ref[b, g * ATT_HEADS + h], out_ref[g, h])

    return pl.pallas_call(
        body, name="bias_build",
        in_specs=[pl.BlockSpec(memory_space=pltpu.SMEM), pl.BlockSpec(memory_space=pltpu.VMEM)],
        out_specs=pl.BlockSpec(memory_space=pltpu.VMEM),
        out_shape=jax.ShapeDtypeStruct((NG, ATT_HEADS, BLK, 2 * BLK), F32),
        compiler_params=_params(),
    )(rel_bias, buckets)


def _bias_scatter(dbias, buckets):
    NG = len(DILATIONS)

    def body(dbias_ref, bucket_ref, out_ref):
        for g in range(NG):
            bk = bucket_ref[g]
            for b in range(N_BUCKETS):
                hit = bk == b
                for h in range(ATT_HEADS):
                    out_ref[b, g * ATT_HEADS + h] = jnp.sum(jnp.where(hit, dbias_ref[g, h], 0.0))

    return pl.pallas_call(
        body, name="bias_scatter",
        in_specs=[pl.BlockSpec(memory_space=pltpu.VMEM), pl.BlockSpec(memory_space=pltpu.VMEM)],
        out_specs=pl.BlockSpec(memory_space=pltpu.SMEM),
        out_shape=jax.ShapeDtypeStruct((N_BUCKETS, NG * ATT_HEADS), F32),
        compiler_params=_params(),
    )(dbias, buckets)


def _window_mask(first):
    qi = lax.broadcasted_iota(jnp.int32, (BLK, 2 * BLK), 0)
    kj = lax.broadcasted_iota(jnp.int32, (BLK, 2 * BLK), 1)
    rel = BLK + qi - kj
    return (rel >= 0) & (rel <= BLK) & (kj >= BLK * first)


def _head_lanes(hh):
    lane = lax.broadcasted_iota(jnp.int32, (1, PAIR), 1)
    return (lane >= hh * HEAD_DIM) & (lane < (hh + 1) * HEAD_DIM)


def _attn_fwd(name, g, q, k, v, qc, kc, vc, bias, blocks_per_residue):
    T = q.shape[0]
    nb = T // BLK
    scale = HEAD_DIM ** -0.5

    def body(q_ref, kp_ref, kc_ref, vp_ref, vc_ref, bias_ref, o_ref, lse_ref):
        b = pl.program_id(0)
        valid = _window_mask((b % blocks_per_residue == 0).astype(jnp.int32))
        low = _head_lanes(0)
        for hp in range(ATT_HEADS // 2):
            cols = slice(hp * PAIR, (hp + 1) * PAIR)
            qp = q_ref[:, cols]
            kk = jnp.concatenate([kp_ref[:, cols], kc_ref[:, cols]], axis=0)
            vv = jnp.concatenate([vp_ref[:, cols], vc_ref[:, cols]], axis=0)
            o_h, lse_h = [], []
            for hh in range(2):
                qm = jnp.where(_head_lanes(hh), qp, jnp.zeros_like(qp))
                s = _dot(qm, kk, NT) * scale
                logits = jnp.where(valid, s + bias_ref[g, 2 * hp + hh], NEG_INF)
                m = jnp.max(logits, axis=-1, keepdims=True)
                p = jnp.exp(logits - m)
                den = jnp.sum(p, axis=-1, keepdims=True)
                o_h.append(_dot(p.astype(BF16), vv) / den)
                lse_h.append(m + jnp.log(den))
            o_ref[:, cols] = jnp.where(low, o_h[0], o_h[1])
            lse_ref[:, cols] = jnp.where(low, lse_h[0], lse_h[1])

    def cur(c):
        return pl.BlockSpec((BLK, ATT_WIDTH), lambda b: (b, c))

    def prev(c):
        return pl.BlockSpec((BLK, ATT_WIDTH), lambda b: (jnp.maximum(b - 1, 0), c))

    out = pl.BlockSpec((BLK, ATT_WIDTH), lambda b: (b, 0))
    return pl.pallas_call(
        body, name=name, grid=(nb,),
        in_specs=[cur(qc), prev(kc), cur(kc), prev(vc), cur(vc),
                  pl.BlockSpec(bias.shape, lambda b: (0, 0, 0, 0))],
        out_specs=[out, out],
        out_shape=[jax.ShapeDtypeStruct((T, ATT_WIDTH), F32), jax.ShapeDtypeStruct((T, ATT_WIDTH), F32)],
        compiler_params=_params(dimension_semantics=("parallel",)),
    )(q, k, k, v, v, bias)


def _attn_merge(outs, lses):
    T = outs[0].shape[0]
    tr = _row_tile(T)
    n = len(outs)

    def body(*refs):
        o_refs, l_refs = refs[:n], refs[n:2 * n]
        o_ref, lse_ref = refs[2 * n], refs[2 * n + 1]
        ls = [r[...] for r in l_refs]
        m = functools.reduce(jnp.maximum, ls)
        es = [jnp.exp(l - m) for l in ls]
        tot = functools.reduce(lambda x, y: x + y, es)
        acc = functools.reduce(lambda x, y: x + y, [e * r[...] for e, r in zip(es, o_refs)])
        o_ref[...] = (acc / tot).astype(BF16)
        lse_ref[...] = m + jnp.log(tot)

    row = pl.BlockSpec((tr, ATT_WIDTH), lambda i: (i, 0))
    return pl.pallas_call(
        body, name="attn_merge", grid=(T // tr,),
        in_specs=[row] * (2 * n), out_specs=[row, row],
        out_shape=[jax.ShapeDtypeStruct((T, ATT_WIDTH), BF16), jax.ShapeDtypeStruct((T, ATT_WIDTH), F32)],
        compiler_params=_params(dimension_semantics=("parallel",)),
    )(*outs, *lses)


def _attn_bwd(name, g, q, k, v, qc, kc, vc, do, o, lse, bias, blocks_per_residue):
    T = q.shape[0]
    nb = T // BLK
    scale = HEAD_DIM ** -0.5

    def body(q_ref, kp_ref, kc_ref, vp_ref, vc_ref, do_ref, o_ref, lse_ref, bias_ref,
             dq_ref, dk_ref, dv_ref, db_ref, ck_ref, cv_ref):
        b = pl.program_id(0)

        @pl.when(b == 0)
        def _():
            db_ref[...] = jnp.zeros_like(db_ref)
            ck_ref[...] = jnp.zeros_like(ck_ref)
            cv_ref[...] = jnp.zeros_like(cv_ref)

        @pl.when(b == nb)
        def _():
            dk_ref[...] = ck_ref[...].astype(BF16)
            dv_ref[...] = cv_ref[...].astype(BF16)

        @pl.when(b < nb)
        def _():
            valid = _window_mask((b % blocks_per_residue == 0).astype(jnp.int32))
            for hp in range(ATT_HEADS // 2):
                cols = slice(hp * PAIR, (hp + 1) * PAIR)
                qp = q_ref[:, cols]
                kk = jnp.concatenate([kp_ref[:, cols], kc_ref[:, cols]], axis=0)
                vv = jnp.concatenate([vp_ref[:, cols], vc_ref[:, cols]], axis=0)
                dop = do_ref[:, cols]
                lsep = lse_ref[:, cols]
                prod = dop.astype(F32) * o_ref[:, cols].astype(F32)
                dq = jnp.zeros((BLK, PAIR), F32)
                dk = jnp.zeros((2 * BLK, PAIR), F32)
                dv = jnp.zeros((2 * BLK, PAIR), F32)
                for hh in range(2):
                    lanes = _head_lanes(hh)
                    qm = jnp.where(lanes, qp, jnp.zeros_like(qp))
                    dom = jnp.where(lanes, dop, jnp.zeros_like(dop))
                    km = jnp.where(lanes, kk, jnp.zeros_like(kk))
                    delta = jnp.sum(jnp.where(lanes, prod, 0.0), axis=-1, keepdims=True)
                    lse_h = jnp.max(jnp.where(lanes, lsep, NEG_INF), axis=-1, keepdims=True)
                    s = _dot(qm, kk, NT) * scale
                    logits = jnp.where(valid, s + bias_ref[g, 2 * hp + hh], NEG_INF)
                    p = jnp.exp(logits - lse_h)
                    dv += _dot(p.astype(BF16), dom, TN)
                    ds = p * (_dot(dom, vv, NT) - delta)
                    db_ref[2 * hp + hh] += ds
                    dss = (ds * scale).astype(BF16)
                    dq += _dot(dss, km)
                    dk += _dot(dss, qm, TN)
                dq_ref[:, cols] = dq.astype(BF16)
                dk_ref[:, cols] = (ck_ref[:, cols] + dk[:BLK]).astype(BF16)
                dv_ref[:, cols] = (cv_ref[:, cols] + dv[:BLK]).astype(BF16)
                ck_ref[:, cols] = dk[BLK:]
                cv_ref[:, cols] = dv[BLK:]

    last = nb - 1

    def cur(c):
        return pl.BlockSpec((BLK, ATT_WIDTH), lambda b: (jnp.minimum(b, last), c))

    def prev(c):
        return pl.BlockSpec((BLK, ATT_WIDTH), lambda b: (jnp.clip(b - 1, 0, last), c))

    dbias_shape = (ATT_HEADS, BLK, 2 * BLK)
    return pl.pallas_call(
        body, name=name, grid=(nb + 1,),
        in_specs=[cur(qc), prev(kc), cur(kc), prev(vc), cur(vc), cur(0), cur(0), cur(0),
                  pl.BlockSpec(bias.shape, lambda b: (0, 0, 0, 0))],
        out_specs=[cur(0), prev(0), prev(0), pl.BlockSpec(dbias_shape, lambda b: (0, 0, 0))],
        out_shape=[jax.ShapeDtypeStruct((T, ATT_WIDTH), BF16)] * 3 + [jax.ShapeDtypeStruct(dbias_shape, F32)],
        scratch_shapes=[pltpu.VMEM((BLK, ATT_WIDTH), F32), pltpu.VMEM((BLK, ATT_WIDTH), F32)],
        compiler_params=_params(dimension_semantics=("arbitrary",)),
    )(q, k, k, v, v, do, o, lse, bias)


def _residue_major(a, d):
    T, C = a.shape
    return a.reshape(T // d, d, C).transpose(1, 0, 2).reshape(T, C)


def _position_major(a, d):
    T, C = a.shape
    return a.reshape(d, T // d, C).transpose(1, 0, 2).reshape(T, C)


def _group_operands(qkv, g, d):
    NG = len(DILATIONS)
    if d == 1:
        return [(qkv, part * NG + g) for part in range(3)]
    cols = lambda part: qkv[:, (part * NG + g) * ATT_WIDTH:(part * NG + g + 1) * ATT_WIDTH]
    return [(_residue_major(cols(part), d), 0) for part in range(3)]


def _attention_fwd(qkv, bias):
    T = qkv.shape[0]
    outs, lses = [], []
    for g, d in enumerate(DILATIONS):
        (q, qc), (k, kc), (v, vc) = _group_operands(qkv, g, d)
        o_g, lse_g = _attn_fwd(f"attn_fwd_{g}", g, q, k, v, qc, kc, vc, bias, T // (d * BLK))
        if d > 1:
            o_g, lse_g = _position_major(o_g, d), _position_major(lse_g, d)
        outs.append(o_g)
        lses.append(lse_g)
    return _attn_merge(outs, lses)


def _attention_bwd(qkv, do, o, lse, bias):
    T = qkv.shape[0]
    dqs, dks, dvs, dbs = [], [], [], []
    for g, d in enumerate(DILATIONS):
        (q, qc), (k, kc), (v, vc) = _group_operands(qkv, g, d)
        do_g, o_g, lse_g = (do, o, lse) if d == 1 else tuple(_residue_major(t, d) for t in (do, o, lse))
        dq, dk, dv, db = _attn_bwd(f"attn_bwd_{g}", g, q, k, v, qc, kc, vc, do_g, o_g, lse_g, bias, T // (d * BLK))
        if d > 1:
            dq, dk, dv = (_position_major(t, d) for t in (dq, dk, dv))
        dqs.append(dq)
        dks.append(dk)
        dvs.append(dv)
        dbs.append(db)
    return jnp.concatenate(dqs + dks + dvs, axis=1), jnp.stack(dbs)


def _other_chips(x, y):
    return [(1 - x, y), (x, 1 - y), (1 - x, 1 - y)]


def _shard_region(ref, shape, by_cols, chip, rows=None):
    R, C = shape
    start, size = (0, R) if rows is None else rows
    if by_cols:
        return ref.at[pl.ds(start, size), pl.ds(chip * C, C)]
    return ref.at[pl.ds(chip * R + start, size), :]


def _gather_weights(entries):
    n = len(entries)
    shapes = [e[0].shape[1:] for e in entries]

    def places(ins, outs, sems):
        send_sems, recv_sems, local_sems = sems
        x, y, c = lax.axis_index("x"), lax.axis_index("y"), lax.axis_index("c")

        def landing(f, px, py, pc):
            R = shapes[f][0]
            return _shard_region(outs[f], shapes[f], entries[f][2], 2 * px + py, rows=(pc * (R // 2), R // 2))

        def copy(f, k, block, to, src=None):
            dst = landing(f, *block)
            return pltpu.make_async_remote_copy(
                src_ref=dst if src is None else src, dst_ref=dst,
                send_sem=send_sems.at[6 * f + k], recv_sem=recv_sems.at[6 * f + k],
                device_id=to, device_id_type=MESH)

        def mine(f):
            dst = _shard_region(outs[f], shapes[f], entries[f][2], 2 * x + y)
            return pltpu.make_async_copy(ins[f].at[entries[f][1]], dst, local_sems.at[f])

        def first(f, j):
            R = shapes[f][0]
            src = ins[f].at[entries[f][1], pl.ds(c * (R // 2), R // 2), :]
            return copy(f, j, (x, y, c), (*_other_chips(x, y)[j], c), src=src)

        return x, y, c, copy, mine, first

    def start(ins, outs, sems):
        _, _, _, _, mine, first = places(ins, outs, sems)
        for f in range(n):
            mine(f).start()
        for j in range(3):
            for f in range(n):
                first(f, j).start()

    def mid(ins, outs, sems):
        x, y, c, copy, _, _ = places(ins, outs, sems)
        for j, chip in enumerate(_other_chips(x, y)):
            for f in range(n):
                copy(f, j, (*chip, c), (x, y, c)).wait_recv()
                copy(f, 3 + j, (*chip, c), (x, y, 1 - c)).start()

    def end(ins, outs, sems):
        x, y, c, copy, mine, first = places(ins, outs, sems)
        for j, chip in enumerate(_other_chips(x, y)):
            for f in range(n):
                copy(f, 3 + j, (*chip, 1 - c), (x, y, c)).wait_recv()
        for j, chip in enumerate(_other_chips(x, y)):
            for f in range(n):
                first(f, j).wait_send()
                copy(f, 3 + j, (*chip, c), (x, y, 1 - c)).wait_send()
        for f in range(n):
            mine(f).wait()

    def whole(f):
        R, C = shapes[f]
        return (R, N_CHIPS * C) if entries[f][2] else (N_CHIPS * R, C)

    return _Comm(
        [e[0] for e in entries], [jax.ShapeDtypeStruct(whole(f), BF16) for f in range(n)],
        [pltpu.SemaphoreType.DMA((6 * n,)), pltpu.SemaphoreType.DMA((6 * n,)), pltpu.SemaphoreType.DMA((n,))],
        start, end, mid)


def _scatter_grads(entries):
    n = len(entries)

    def copies(ins, outs, sems):
        send_sems, recv_sems, local_sems = sems
        x, y, c = lax.axis_index("x"), lax.axis_index("y"), lax.axis_index("c")
        me = 2 * x + y

        def piece(f, chip):
            return _shard_region(ins[f], entries[f][1], entries[f][2], chip)

        mine = [pltpu.make_async_copy(piece(f, me), outs[f].at[me], local_sems.at[f]) for f in range(n)]
        sends = [pltpu.make_async_remote_copy(
            src_ref=piece(f, 2 * px + py), dst_ref=outs[f].at[me],
            send_sem=send_sems.at[3 * f + j], recv_sem=recv_sems.at[3 * f + j],
            device_id=(px, py, c), device_id_type=MESH)
            for j, (px, py) in enumerate(_other_chips(x, y)) for f in range(n)]
        return mine, sends

    def start(ins, outs, sems):
        mine, sends = copies(ins, outs, sems)
        for cp in mine + sends:
            cp.start()

    def end(ins, outs, sems):
        mine, sends = copies(ins, outs, sems)
        for cp in sends + mine:
            cp.wait()

    return _Comm(
        [e[0] for e in entries], [jax.ShapeDtypeStruct((N_CHIPS,) + tuple(e[1]), BF16) for e in entries],
        [pltpu.SemaphoreType.DMA((3 * n,)), pltpu.SemaphoreType.DMA((3 * n,)), pltpu.SemaphoreType.DMA((n,))],
        start, end)


def _exchange_sibling(parts):
    n = len(parts)

    def copies(ins, outs, sems):
        send_sems, recv_sems = sems
        sibling = (lax.axis_index("x"), lax.axis_index("y"), 1 - lax.axis_index("c"))
        return [pltpu.make_async_remote_copy(src_ref=ins[i], dst_ref=outs[i], send_sem=send_sems.at[i],
                                             recv_sem=recv_sems.at[i], device_id=sibling, device_id_type=MESH)
                for i in range(n)]

    def start(ins, outs, sems):
        for cp in copies(ins, outs, sems):
            cp.start()

    def end(ins, outs, sems):
        for cp in copies(ins, outs, sems):
            cp.wait()

    return _Comm(parts, [jax.ShapeDtypeStruct(s.shape, s.dtype) for s in parts],
                 [pltpu.SemaphoreType.DMA((n,)), pltpu.SemaphoreType.DMA((n,))], start, end)


def _allgather_small(block):
    m_per, ncol = block.shape

    def body(x_ref, out_ref, send_sems, recv_sems, local_sem):
        x, y, c = lax.axis_index("x"), lax.axis_index("y"), lax.axis_index("c")
        me, sibling = (x, y, c), (x, y, 1 - c)
        chips = _other_chips(x, y)

        def rows(px, py, pc):
            return out_ref.at[4 * px + 2 * py + pc]

        def copy(k, block_of, to, src=None):
            return pltpu.make_async_remote_copy(
                src_ref=rows(*block_of) if src is None else src, dst_ref=rows(*block_of),
                send_sem=send_sems.at[k], recv_sem=recv_sems.at[k], device_id=to, device_id_type=MESH)

        mine = pltpu.make_async_copy(x_ref, rows(*me), local_sem)
        mine.start()
        first = [copy(0, me, sibling, src=x_ref)]
        first += [copy(1 + j, me, (*chip, c), src=x_ref) for j, chip in enumerate(chips)]
        for cp in first:
            cp.start()
        passed = [copy(4 + j, (*chip, c), sibling) for j, chip in enumerate(chips)]
        for j, chip in enumerate(chips):
            copy(1 + j, (*chip, c), me).wait_recv()
            passed[j].start()
        copy(0, sibling, me).wait_recv()
        for j, chip in enumerate(chips):
            copy(4 + j, (*chip, 1 - c), me).wait_recv()
        for cp in first + passed:
            cp.wait_send()
        mine.wait()

    return pl.pallas_call(
        body, name="allgather_small",
        in_specs=[pl.BlockSpec(memory_space=pltpu.VMEM)], out_specs=pl.BlockSpec(memory_space=pltpu.VMEM),
        out_shape=jax.ShapeDtypeStruct((N_DEV, m_per, ncol), block.dtype),
        scratch_shapes=[pltpu.SemaphoreType.DMA((7,)), pltpu.SemaphoreType.DMA((7,)), pltpu.SemaphoreType.DMA],
        compiler_params=_params(),
    )(block)


def _adamw(w, g, m, v):
    m = ADAM_B1 * m + (1.0 - ADAM_B1) * g
    v = ADAM_B2 * v + (1.0 - ADAM_B2) * jnp.square(g)
    m_hat = m / (1.0 - ADAM_B1 ** ADAM_STEP)
    v_hat = v / (1.0 - ADAM_B2 ** ADAM_STEP)
    delta = -ADAM_LR * (m_hat / (jnp.sqrt(v_hat) + ADAM_EPS) + ADAM_WD * w)
    return delta, m, v


def _flat_tile(rows):
    return min(rows, 256)


def _sum_pieces(name, layers):
    L = len(layers)
    P, R, C = layers[0].shape
    tr = _flat_tile(R)

    def body(*refs):
        out_ref = refs[L]
        for l in range(L):
            @pl.when(pl.program_id(0) == l)
            def _(p_ref=refs[l]):
                acc = p_ref[0].astype(F32)
                for j in range(1, P):
                    acc = acc + p_ref[j].astype(F32)
                out_ref[...] = acc

    return pl.pallas_call(
        body, name=name, grid=(L, R // tr),
        in_specs=[pl.BlockSpec((P, tr, C), lambda l, i: (0, i, 0)) for _ in range(L)],
        out_specs=pl.BlockSpec((None, tr, C), lambda l, i: (l, i, 0)),
        out_shape=jax.ShapeDtypeStruct((L, R, C), F32),
        compiler_params=_params(dimension_semantics=("parallel", "parallel")),
    )(*layers)


def _adam_pair(name, w, m, v, part_a, part_b):
    L, R, C = w.shape
    tr = _flat_tile(R)

    def body(w_ref, m_ref, v_ref, a_ref, b_ref, g_ref, d_ref, nm_ref, nv_ref):
        g = a_ref[...] + b_ref[...]
        g_ref[...] = g
        d_ref[...], nm_ref[...], nv_ref[...] = _adamw(w_ref[...], g, m_ref[...], v_ref[...])

    row = pl.BlockSpec((None, tr, C), lambda l, i: (l, i, 0))
    return pl.pallas_call(
        body, name=name, grid=(L, R // tr),
        in_specs=[row] * 5, out_specs=[row] * 4,
        out_shape=[jax.ShapeDtypeStruct((L, R, C), F32)] * 4,
        compiler_params=_params(dimension_semantics=("parallel", "parallel")),
    )(w, m, v, part_a, part_b)


def _adam_small(w, m, v, gathered):
    R, C = w.shape

    def body(w_ref, m_ref, v_ref, p_ref, g_ref, d_ref, nm_ref, nv_ref):
        g = p_ref[0]
        for j in range(1, N_DEV):
            g = g + p_ref[j]
        g_ref[...] = g
        d_ref[...], nm_ref[...], nv_ref[...] = _adamw(w_ref[...], g, m_ref[...], v_ref[...])

    return pl.pallas_call(
        body, name="adam_small",
        out_shape=[jax.ShapeDtypeStruct((R, C), F32)] * 4,
        compiler_params=_params(),
    )(w, m, v, gathered)


SMALL = ("mix_norm_g", "mlp_norm_g", "final_norm_g", "a_ln_g", "a_ln_b", "a_w_s", "a_b_s", "rel_bias")


def _pack_small(arrays, width):
    rows = []
    for a in arrays:
        flat = a.reshape(-1)
        pad = (-flat.shape[0]) % width
        rows.append(jnp.pad(flat, (0, pad)).reshape(-1, width))
    block = jnp.concatenate(rows, axis=0)
    return jnp.pad(block, ((0, (-block.shape[0]) % 8), (0, 0)))


def _unpack_small(block, shapes, width):
    out, row = [], 0
    for shape in shapes:
        size = int(np.prod(shape))
        nrows = -(-size // width)
        out.append(block[row:row + nrows].reshape(-1)[:size].reshape(shape))
        row += nrows
    return out


def kernel(x, mix_norm_g, mlp_norm_g, final_norm_g, a_w_in, a_ln_g, a_ln_b, a_w_s, a_b_s, a_w_out, b_w_qkv, b_w_out, rel_bias, w_up, w_down, loss_target, m_mix_norm_g, m_mlp_norm_g, m_final_norm_g, m_a_w_in, m_a_ln_g, m_a_ln_b, m_a_w_s, m_a_b_s, m_a_w_out, m_b_w_qkv, m_b_w_out, m_rel_bias, m_w_up, m_w_down, v_mix_norm_g, v_mlp_norm_g, v_final_norm_g, v_a_w_in, v_a_ln_g, v_a_ln_b, v_a_w_s, v_a_b_s, v_a_w_out, v_b_w_qkv, v_b_w_out, v_rel_bias, v_w_up, v_w_down):
    T, D = x.shape[1], x.shape[2]
    h0 = x.reshape(T, D)
    target = loss_target.reshape(T, D)
    G = a_w_s.shape[1]

    w_big = [a_w_in, a_w_out, b_w_qkv, b_w_out, w_up, w_down]
    m_big = [m_a_w_in, m_a_w_out, m_b_w_qkv, m_b_w_out, m_w_up, m_w_down]
    v_big = [v_a_w_in, v_a_w_out, v_b_w_qkv, v_b_w_out, v_w_up, v_w_down]
    by_cols = [True, False, True, True, True, False]
    s_in, s_out, s_qkv, s_bo, s_up, s_dn = [w.astype(BF16) for w in w_big]
    W_in, W_out = _run_comm("gather_a", _gather_weights([(s_in, 0, True), (s_out, 0, False)]))

    tril = jnp.tril(jnp.ones((CHUNK, CHUNK), dtype=bool))
    w_tril = jnp.where(tril[None], a_w_s[0], 0.0).astype(BF16)
    w_tril_t = jnp.swapaxes(w_tril, 1, 2)
    b_rows = jnp.broadcast_to(a_b_s[0][:, :, None], (G, CHUNK, CHUNK))
    buckets = _bucket_maps()
    bias = _bias_build(rel_bias, buckets)

    QKV = s_qkv.shape[2] * N_CHIPS
    TM = 1024
    TK_WGRAD = 2048

    def matmul(name, a, b, mode, out, tm=TM, tn=1024, epi=None, extras=(), comm=None):
        return _mm(name, a, b, mode, tm=tm, tn=tn, tk=a.shape[1], outs=[out], epi=epi, extras=extras, comm=comm)

    def wgrad(name, a, b, tn=1024, comm=None):
        return _mm(name, a, b, "tn", tm=1024, tn=tn, tk=TK_WGRAD, outs=[BF16], comm=comm)

    def scatter(*which):
        return _scatter_grads([(g, w_big[i].shape[1:], by_cols[i]) for g, i in which])

    y0 = _rms_fwd("rms_fwd_mix0", h0, mix_norm_g[0:1])
    a_pre, (W_up0,) = matmul("a_in", y0, W_in, "nn", BF16, comm=_gather_weights([(s_up, 0, True)]))
    z = _gate_fwd(a_pre, a_ln_g, a_ln_b, w_tril, b_rows)
    h1 = matmul("a_out", z, W_out, "nn", F32, epi=_epi_residual, extras=(h0,))
    y1 = _rms_fwd("rms_fwd_mlp0", h1, mlp_norm_g[0:1])
    q1, (W_dn0,) = matmul("mlp_up0", y1, W_up0, "nn", BF16, epi=_epi_relu2, comm=_gather_weights([(s_dn, 0, False)]))
    h2, (W_qkv, W_bo) = matmul("mlp_down0", q1, W_dn0, "nn", F32, tm=TM // 2, epi=_epi_residual, extras=(h1,),
                               comm=_gather_weights([(s_qkv, 0, True), (s_bo, 0, True)]))
    y2 = _rms_fwd("rms_fwd_mix1", h2, mix_norm_g[1:2])
    qkv, (W_up1, W_dn1) = matmul("b_qkv", y2, W_qkv, "nn", BF16, tn=QKV // 4,
                                 comm=_gather_weights([(s_up, 1, True), (s_dn, 1, False)]))
    o, lse = _attention_fwd(qkv, bias)
    h3 = matmul("b_out", o, W_bo, "nn", F32, epi=_epi_residual, extras=(h2,))
    y3 = _rms_fwd("rms_fwd_mlp1", h3, mlp_norm_g[1:2])
    q3 = matmul("mlp_up1", y3, W_up1, "nn", BF16, epi=_epi_relu2)
    h4 = matmul("mlp_down1", q3, W_dn1, "nn", F32, tm=TM // 2, epi=_epi_residual, extras=(h3,))
    dh4, dh4_b, d_final_g, loss_tile = _loss_head(h4, final_norm_g.reshape(1, D), target)
    loss = lax.psum(loss_tile[0, 0], ("x", "y", "c"))

    dp3 = matmul("mlp_down_bwd1", dh4_b, W_dn1, "nt", BF16, epi=_epi_relu2_grad, extras=(q3,))
    g_dn1 = wgrad("mlp_down_wgrad1", q3, dh4_b)
    dy3, (r_dn1,) = matmul("mlp_up_bwd1", dp3, W_up1, "nt", F32, tm=TM // 2, comm=scatter((g_dn1, 5)))
    g_up1 = wgrad("mlp_up_wgrad1", y3, dp3)
    dh3, dh3_b, dg_mlp1 = _rms_bwd("rms_bwd_mlp1", dy3, h3, mlp_norm_g[1:2], dh4)
    do = matmul("b_out_bwd", dh3_b, W_bo, "nt", BF16)
    g_bo = wgrad("b_out_wgrad", o, dh3_b)
    dqkv, dbias = _attention_bwd(qkv, do, o, lse, bias)
    d_rel_bias = _bias_scatter(dbias, buckets)
    dy2, (r_up1, r_bo) = matmul("b_qkv_bwd", dqkv, W_qkv, "nt", F32, tm=TM // 2, comm=scatter((g_up1, 4), (g_bo, 3)))
    g_qkv = wgrad("b_qkv_wgrad", y2, dqkv, tn=QKV // 4)
    dh2, dh2_b, dg_mix1 = _rms_bwd("rms_bwd_mix1", dy2, h2, mix_norm_g[1:2], dh3)
    dp1, (r_qkv,) = matmul("mlp_down_bwd0", dh2_b, W_dn0, "nt", BF16, epi=_epi_relu2_grad, extras=(q1,),
                           comm=scatter((g_qkv, 2)))
    g_dn0 = wgrad("mlp_down_wgrad0", q1, dh2_b)
    dy1, (r_dn0,) = matmul("mlp_up_bwd0", dp1, W_up0, "nt", F32, tm=TM // 2, comm=scatter((g_dn0, 5)))
    g_up0 = wgrad("mlp_up_wgrad0", y1, dp1)
    dh1, dh1_b, dg_mlp0 = _rms_bwd("rms_bwd_mlp0", dy1, h1, mlp_norm_g[0:1], dh2)
    dz = matmul("a_out_bwd", dh1_b, W_out, "nt", F32)
    g_out = wgrad("a_out_wgrad", z, dh1_b)
    da, d_ln_g, d_ln_b, d_w_s, d_b_s = _gate_bwd(a_pre, dz, a_ln_g, a_ln_b, w_tril, w_tril_t, b_rows)
    dy0, (r_up0,) = matmul("a_in_bwd", da, W_in, "nt", F32, comm=scatter((g_up0, 4)))
    g_in, (r_out,) = wgrad("a_in_wgrad", y0, da, comm=scatter((g_out, 1)))
    grad_x, _, dg_mix0 = _rms_bwd("rms_bwd_mix0", dy0, h0, mix_norm_g[0:1], dh1)
    (r_in,) = _run_comm("scatter_a_in", scatter((g_in, 0)))

    received = [[r_in], [r_out], [r_qkv], [r_bo], [r_up0, r_up1], [r_dn0, r_dn1]]
    plane = [_sum_pieces(f"sum_pieces{i}", r) for i, r in enumerate(received)]
    other = _run_comm("exchange_sibling", _exchange_sibling(plane))
    big_out = [_adam_pair(f"adam{i}", w_big[i], m_big[i], v_big[i], plane[i], other[i]) for i in range(len(w_big))]

    def unbig(kind):
        return dict(zip(["a_w_in", "a_w_out", "b_w_qkv", "b_w_out", "w_up", "w_down"], [b[kind] for b in big_out]))

    small_w = [mix_norm_g, mlp_norm_g, final_norm_g, a_ln_g, a_ln_b, a_w_s, a_b_s, rel_bias]
    small_m = [m_mix_norm_g, m_mlp_norm_g, m_final_norm_g, m_a_ln_g, m_a_ln_b, m_a_w_s, m_a_b_s, m_rel_bias]
    small_v = [v_mix_norm_g, v_mlp_norm_g, v_final_norm_g, v_a_ln_g, v_a_ln_b, v_a_w_s, v_a_b_s, v_rel_bias]
    small_g = [jnp.concatenate([dg_mix0, dg_mix1]), jnp.concatenate([dg_mlp0, dg_mlp1]), d_final_g,
               d_ln_g, d_ln_b, d_w_s[None], d_b_s[None, :, :, 0], d_rel_bias]
    width = max(D, 128)
    gathered_small = _allgather_small(_pack_small(small_g, width))
    small_out = _adam_small(_pack_small(small_w, width), _pack_small(small_m, width), _pack_small(small_v, width),
                            gathered_small)
    shapes = [w.shape for w in small_w]

    names = ["mix_norm_g", "mlp_norm_g", "final_norm_g", "a_w_in", "a_ln_g", "a_ln_b", "a_w_s", "a_b_s", "a_w_out",
             "b_w_qkv", "b_w_out", "rel_bias", "w_up", "w_down"]
    results = [loss, grad_x.reshape(x.shape)]
    for kind in range(4):
        table = dict(zip(SMALL, _unpack_small(small_out[kind], shapes, width)))
        table.update(unbig(kind))
        results += [table[n] for n in names]
    return tuple(results)
```

```python
import functools
import math

import numpy as np
import jax
import jax.numpy as jnp
from jax import lax
from jax.experimental import pallas as pl
from jax.experimental.pallas import tpu as pltpu

F32 = jnp.float32
BF16 = jnp.bfloat16
MESH = pl.DeviceIdType.MESH
ANY = pl.BlockSpec(memory_space=pl.ANY)

N_CHIPS = 4
N_DEV = 8
VMEM_LIMIT_BYTES = 52 * 1024 * 1024

EPS = 1e-6
NEG_INF = -1e30
CHUNK = 128
GROUP_DIM = 128
HEAD_DIM = 64
ATT_HEADS = 8
ATT_WIDTH = ATT_HEADS * HEAD_DIM
PAIR = 2 * HEAD_DIM
BLK = 128
DILATIONS = (1, 4, 16)
N_BUCKETS = 32
MAX_EXACT = N_BUCKETS // 2
REL_MAX_DISTANCE = 2048

ADAM_LR = 0.001
ADAM_B1 = 0.9
ADAM_B2 = 0.999
ADAM_EPS = 1e-08
ADAM_WD = 0.01
ADAM_STEP = 10

NN = (((1,), (0,)), ((), ()))
NT = (((1,), (1,)), ((), ()))
TN = (((0,), (0,)), ((), ()))


def _params(**kw):
    return pltpu.CompilerParams(vmem_limit_bytes=VMEM_LIMIT_BYTES, **kw)


def _dot(a, b, dims=NN):
    return lax.dot_general(a, b, dims, preferred_element_type=F32)


def _gelu(x):
    return 0.5 * x * (1.0 + lax.erf(x * math.sqrt(0.5)))


def _gelu_grad(x):
    return 0.5 * (1.0 + lax.erf(x * math.sqrt(0.5))) + x * jnp.exp(-0.5 * x * x) * (1.0 / math.sqrt(2.0 * math.pi))


def _mean(x):
    return jnp.mean(x, axis=-1, keepdims=True)


class _Comm:
    def __init__(self, inputs, out_shapes, scratch, start, end, mid=None):
        self.inputs, self.out_shapes, self.scratch = list(inputs), list(out_shapes), list(scratch)
        self.start, self.mid, self.end = start, mid, end


def _run_comm(name, comm):
    n_in, n_out = len(comm.inputs), len(comm.out_shapes)

    def body(*refs):
        parts = refs[:n_in], refs[n_in:n_in + n_out], refs[n_in + n_out:]
        comm.start(*parts)
        if comm.mid is not None:
            comm.mid(*parts)
        comm.end(*parts)

    return pl.pallas_call(
        body, name=name, in_specs=[ANY] * n_in, out_specs=[ANY] * n_out, out_shape=comm.out_shapes,
        scratch_shapes=comm.scratch, compiler_params=_params(),
    )(*comm.inputs)


def _mm(name, a, b, mode, *, tm, tn, tk, outs, epi=None, extras=(), comm=None):
    if mode == "tn":
        K, M = a.shape
    else:
        M, K = a.shape
    N = b.shape[0] if mode == "nt" else b.shape[1]
    tm, tn, tk = min(tm, M), min(tn, N), min(tk, K)
    assert M % tm == 0 and N % tn == 0 and K % tk == 0, (name, M, N, K, tm, tn, tk)
    nk = K // tk
    grid = (M // tm, N // tn, nk)

    if mode == "tn":
        a_spec = pl.BlockSpec((tk, tm), lambda i, j, k: (k, i))
    else:
        a_spec = pl.BlockSpec((tm, tk), lambda i, j, k: (i, k))
    if mode == "nt":
        b_spec = pl.BlockSpec((tn, tk), lambda i, j, k: (j, k))
    else:
        b_spec = pl.BlockSpec((tk, tn), lambda i, j, k: (k, j))
    tile = pl.BlockSpec((tm, tn), lambda i, j, k: (i, j))
    out_shapes = [jax.ShapeDtypeStruct((M, N), dtype) for dtype in outs]
    out_specs = [tile for _ in outs]
    extra_specs = [tile for _ in extras]
    n_extra, n_out = len(extras), len(outs)
    n_cin = len(comm.inputs) if comm else 0
    n_cout = len(comm.out_shapes) if comm else 0
    dims = {"nn": NN, "nt": NT, "tn": TN}[mode]
    steps = grid[0] * grid[1] * grid[2]

    def body(*refs):
        a_ref, b_ref = refs[0], refs[1]
        pos = 2
        extra_refs = refs[pos:pos + n_extra]
        pos += n_extra
        comm_in = refs[pos:pos + n_cin]
        pos += n_cin
        out_refs = refs[pos:pos + n_out]
        pos += n_out
        comm_out = refs[pos:pos + n_cout]
        pos += n_cout
        acc_ref = refs[pos] if nk > 1 else None
        comm_sems = refs[pos + (nk > 1):]
        k = pl.program_id(2)
        step = (pl.program_id(0) * grid[1] + pl.program_id(1)) * nk + k

        if comm is not None:
            @pl.when(step == 0)
            def _():
                comm.start(comm_in, comm_out, comm_sems)

        part = _dot(a_ref[...].astype(BF16), b_ref[...].astype(BF16), dims)

        def finish(acc):
            res = epi(acc, *[e[...] for e in extra_refs]) if epi is not None else (acc,) * n_out
            for o, r in zip(out_refs, res):
                o[...] = r.astype(o.dtype)

        if nk == 1:
            finish(part)
        else:
            @pl.when(k == 0)
            def _():
                acc_ref[...] = part

            @pl.when(k > 0)
            def _():
                acc_ref[...] += part

            @pl.when(k == nk - 1)
            def _():
                finish(acc_ref[...])

        if comm is not None:
            if comm.mid is not None:
                @pl.when(step == (3 * steps) // 4)
                def _():
                    comm.mid(comm_in, comm_out, comm_sems)

            @pl.when(step == steps - 1)
            def _():
                comm.end(comm_in, comm_out, comm_sems)

    order = ("arbitrary",) * 3 if comm else ("parallel", "parallel", "arbitrary")
    res = pl.pallas_call(
        body, name=name, grid=grid,
        in_specs=[a_spec, b_spec] + extra_specs + [ANY] * n_cin,
        out_specs=out_specs + [ANY] * n_cout,
        out_shape=out_shapes + (comm.out_shapes if comm else []),
        scratch_shapes=([pltpu.VMEM((tm, tn), F32)] if nk > 1 else []) + (comm.scratch if comm else []),
        compiler_params=_params(dimension_semantics=order),
    )(a, b, *extras, *(comm.inputs if comm else []))
    mm_out = res[0] if n_out == 1 else list(res[:n_out])
    return (mm_out, list(res[n_out:])) if comm else mm_out


def _epi_residual(acc, res):
    return (res + acc,)


def _epi_relu2(acc):
    return (jnp.square(jnp.maximum(acc, 0.0)),)


def _epi_relu2_grad(acc, q):
    return (acc * (2.0 * jnp.sqrt(q.astype(F32))),)


def _row_tile(T):
    return min(T, 512)


def _rms_fwd(name, h, g):
    T, D = h.shape
    tr = _row_tile(T)

    def body(h_ref, g_ref, y_ref):
        hv = h_ref[...]
        y = hv * lax.rsqrt(_mean(hv * hv) + EPS)
        y_ref[...] = (y * g_ref[...]).astype(BF16)

    return pl.pallas_call(
        body, name=name, grid=(T // tr,),
        in_specs=[pl.BlockSpec((tr, D), lambda i: (i, 0)), pl.BlockSpec((1, D), lambda i: (0, 0))],
        out_specs=pl.BlockSpec((tr, D), lambda i: (i, 0)),
        out_shape=jax.ShapeDtypeStruct((T, D), BF16),
        compiler_params=_params(dimension_semantics=("parallel",)),
    )(h, g)


def _rms_bwd(name, dy, h, g, dres):
    T, D = h.shape
    tr = _row_tile(T)

    def body(dy_ref, h_ref, g_ref, dres_ref, dh_ref, dhb_ref, dg_ref):
        @pl.when(pl.program_id(0) == 0)
        def _():
            dg_ref[...] = jnp.zeros_like(dg_ref)

        hv = h_ref[...]
        r = lax.rsqrt(_mean(hv * hv) + EPS)
        hn = hv * r
        dyv = dy_ref[...]
        dg_ref[...] += jnp.sum(dyv * hn, axis=0, keepdims=True)
        dyg = dyv * g_ref[...]
        dh = dres_ref[...] + r * (dyg - hn * _mean(dyg * hn))
        dh_ref[...] = dh
        dhb_ref[...] = dh.astype(BF16)

    row = pl.BlockSpec((tr, D), lambda i: (i, 0))
    vec = pl.BlockSpec((1, D), lambda i: (0, 0))
    return pl.pallas_call(
        body, name=name, grid=(T // tr,),
        in_specs=[row, row, vec, row], out_specs=[row, row, vec],
        out_shape=[jax.ShapeDtypeStruct((T, D), F32), jax.ShapeDtypeStruct((T, D), BF16),
                   jax.ShapeDtypeStruct((1, D), F32)],
        compiler_params=_params(dimension_semantics=("arbitrary",)),
    )(dy, h, g, dres)


def _loss_head(h, g, target):
    T, D = h.shape
    tr = _row_tile(T)

    def body(h_ref, g_ref, t_ref, dh_ref, dhb_ref, dg_ref, loss_ref):
        @pl.when(pl.program_id(0) == 0)
        def _():
            dg_ref[...] = jnp.zeros_like(dg_ref)
            loss_ref[...] = jnp.zeros_like(loss_ref)

        hv = h_ref[...]
        r = lax.rsqrt(_mean(hv * hv) + EPS)
        hn = hv * r
        gv = g_ref[...]
        diff = hn * gv - t_ref[...]
        loss_ref[...] += 0.5 * jnp.sum(_mean(diff * diff))
        dyv = diff * (1.0 / D)
        dg_ref[...] += jnp.sum(dyv * hn, axis=0, keepdims=True)
        dyg = dyv * gv
        dh = r * (dyg - hn * _mean(dyg * hn))
        dh_ref[...] = dh
        dhb_ref[...] = dh.astype(BF16)

    row = pl.BlockSpec((tr, D), lambda i: (i, 0))
    vec = pl.BlockSpec((1, D), lambda i: (0, 0))
    return pl.pallas_call(
        body, name="loss_head", grid=(T // tr,),
        in_specs=[row, vec, row], out_specs=[row, row, vec, pl.BlockSpec((8, 128), lambda i: (0, 0))],
        out_shape=[jax.ShapeDtypeStruct((T, D), F32), jax.ShapeDtypeStruct((T, D), BF16),
                   jax.ShapeDtypeStruct((1, D), F32), jax.ShapeDtypeStruct((8, 128), F32)],
        compiler_params=_params(dimension_semantics=("arbitrary",)),
    )(h, g, target)


def _gate_tile(T):
    return min(T, 256)


def _gate_fwd(a, ln_g, ln_b, w_tril, b_rows):
    T, W2 = a.shape
    W = W2 // 2
    G = W // GROUP_DIM
    tr = _gate_tile(T)

    def body(a_ref, lng_ref, lnb_ref, w_ref, b_ref, z_ref):
        u = _gelu(a_ref[:, :W].astype(F32))
        vg = _gelu(a_ref[:, W:].astype(F32))
        xc = vg - _mean(vg)
        vn = xc * lax.rsqrt(_mean(xc * xc) + EPS)
        vl = (vn * lng_ref[...] + lnb_ref[...]).astype(BF16)
        for n in range(tr // CHUNK):
            rows = slice(n * CHUNK, (n + 1) * CHUNK)
            for g in range(G):
                cols = slice(g * GROUP_DIM, (g + 1) * GROUP_DIM)
                gate = _dot(w_ref[g], vl[rows, cols]) + b_ref[g]
                z_ref[rows, cols] = (u[rows, cols] * gate).astype(BF16)

    vec = pl.BlockSpec((1, W), lambda i: (0, 0))
    grp = pl.BlockSpec((G, CHUNK, CHUNK), lambda i: (0, 0, 0))
    return pl.pallas_call(
        body, name="gate_fwd", grid=(T // tr,),
        in_specs=[pl.BlockSpec((tr, W2), lambda i: (i, 0)), vec, vec, grp, grp],
        out_specs=pl.BlockSpec((tr, W), lambda i: (i, 0)),
        out_shape=jax.ShapeDtypeStruct((T, W), BF16),
        compiler_params=_params(dimension_semantics=("parallel",)),
    )(a, ln_g, ln_b, w_tril, b_rows)


def _gate_bwd(a, dz, ln_g, ln_b, w_tril, w_tril_t, b_rows):
    T, W2 = a.shape
    W = W2 // 2
    G = W // GROUP_DIM
    tr = _gate_tile(T)
    steps = T // tr

    def body(a_ref, dz_ref, lng_ref, lnb_ref, w_ref, wt_ref, b_ref, da_ref, dlng_ref, dlnb_ref, dw_ref, dbs_ref, dvl_ref):
        step = pl.program_id(0)

        @pl.when(step == 0)
        def _():
            dlng_ref[...] = jnp.zeros_like(dlng_ref)
            dlnb_ref[...] = jnp.zeros_like(dlnb_ref)
            dw_ref[...] = jnp.zeros_like(dw_ref)
            dbs_ref[...] = jnp.zeros_like(dbs_ref)

        au = a_ref[:, :W].astype(F32)
        av = a_ref[:, W:].astype(F32)
        u = _gelu(au)
        vg = _gelu(av)
        xc = vg - _mean(vg)
        rstd = lax.rsqrt(_mean(xc * xc) + EPS)
        vn = xc * rstd
        lng = lng_ref[...]
        vl = (vn * lng + lnb_ref[...]).astype(BF16)
        du_scale = dz_ref[...] * _gelu_grad(au)
        dgate_all = dz_ref[...] * u
        for n in range(tr // CHUNK):
            rows = slice(n * CHUNK, (n + 1) * CHUNK)
            for g in range(G):
                cols = slice(g * GROUP_DIM, (g + 1) * GROUP_DIM)
                vlg = vl[rows, cols]
                gate = _dot(w_ref[g], vlg) + b_ref[g]
                da_ref[rows, cols] = (du_scale[rows, cols] * gate).astype(BF16)
                dgate = dgate_all[rows, cols]
                dbs_ref[g] += dgate
                dgate_b = dgate.astype(BF16)
                dw_ref[g] += _dot(dgate_b, vlg, NT)
                dvl_ref[rows, cols] = _dot(wt_ref[g], dgate_b)
        dvl = dvl_ref[...]
        dlnb_ref[...] += jnp.sum(dvl, axis=0, keepdims=True)
        dlng_ref[...] += jnp.sum(dvl * vn, axis=0, keepdims=True)
        dvn = dvl * lng
        dvg = rstd * (dvn - _mean(dvn) - vn * _mean(dvn * vn))
        da_ref[:, W:] = (dvg * _gelu_grad(av)).astype(BF16)

        @pl.when(step == steps - 1)
        def _():
            t_idx = lax.broadcasted_iota(jnp.int32, (CHUNK, CHUNK), 0)
            s_idx = lax.broadcasted_iota(jnp.int32, (CHUNK, CHUNK), 1)
            for g in range(G):
                dw_ref[g] = jnp.where(s_idx <= t_idx, dw_ref[g], 0.0)
                dbs_ref[g] = jnp.broadcast_to(jnp.sum(dbs_ref[g], axis=-1, keepdims=True), (CHUNK, CHUNK))

    vec = pl.BlockSpec((1, W), lambda i: (0, 0))
    grp = pl.BlockSpec((G, CHUNK, CHUNK), lambda i: (0, 0, 0))
    return pl.pallas_call(
        body, name="gate_bwd", grid=(steps,),
        in_specs=[pl.BlockSpec((tr, W2), lambda i: (i, 0)), pl.BlockSpec((tr, W), lambda i: (i, 0)),
                  vec, vec, grp, grp, grp],
        out_specs=[pl.BlockSpec((tr, W2), lambda i: (i, 0)), vec, vec, grp, grp],
        out_shape=[jax.ShapeDtypeStruct((T, W2), BF16), jax.ShapeDtypeStruct((1, W), F32),
                   jax.ShapeDtypeStruct((1, W), F32), jax.ShapeDtypeStruct((G, CHUNK, CHUNK), F32),
                   jax.ShapeDtypeStruct((G, CHUNK, CHUNK), F32)],
        scratch_shapes=[pltpu.VMEM((tr, W), F32)],
        compiler_params=_params(dimension_semantics=("arbitrary",)),
    )(a, dz, ln_g, ln_b, w_tril, w_tril_t, b_rows)


def _bucket_map(dilation):
    rel = BLK + np.arange(BLK)[:, None] - np.arange(2 * BLK)[None, :]
    dist = np.clip(rel, 0, BLK) * dilation
    nf = np.maximum(dist, 1).astype(np.float32)
    large = MAX_EXACT + (np.log(nf / np.float32(MAX_EXACT)) / np.float32(math.log(REL_MAX_DISTANCE / MAX_EXACT))
                         * np.float32(N_BUCKETS - MAX_EXACT)).astype(np.int32)
    large = np.minimum(large, N_BUCKETS - 1)
    return np.where(dist < MAX_EXACT, dist, large).astype(np.int32)


def _bucket_maps():
    return jnp.asarray(np.stack([_bucket_map(d) for d in DILATIONS]))


def _bias_build(rel_bias, buckets):
    NG = len(DILATIONS)

    def body(table_ref, bucket_ref, out_ref):
        out_ref[...] = jnp.zeros_like(out_ref)
        for g in range(NG):
            bk = bucket_ref[g]
            for b in range(N_BUCKETS):
                hit = bk == b
                for h in range(ATT_HEADS):
                    out_ref[g, h] = jnp.where(hit, table_ref[b, g * ATT_HEADS + h], out_ref[g, h])

    return pl.pallas_call(
        body, name="bias_build",
        in_specs=[pl.BlockSpec(memory_space=pltpu.SMEM), pl.BlockSpec(memory_space=pltpu.VMEM)],
        out_specs=pl.BlockSpec(memory_space=pltpu.VMEM),
        out_shape=jax.ShapeDtypeStruct((NG, ATT_HEADS, BLK, 2 * BLK), F32),
        compiler_params=_params(),
    )(rel_bias, buckets)


def _bias_scatter(dbias, buckets):
    NG = len(DILATIONS)

    def body(dbias_ref, bucket_ref, out_ref):
        for g in range(NG):
            bk = bucket_ref[g]
            for b in range(N_BUCKETS):
                hit = bk == b
                for h in range(ATT_HEADS):
                    out_ref[b, g * ATT_HEADS + h] = jnp.sum(jnp.where(hit, dbias_ref[g, h], 0.0))

    return pl.pallas_call(
        body, name="bias_scatter",
        in_specs=[pl.BlockSpec(memory_space=pltpu.VMEM), pl.BlockSpec(memory_space=pltpu.VMEM)],
        out_specs=pl.BlockSpec(memory_space=pltpu.SMEM),
        out_shape=jax.ShapeDtypeStruct((N_BUCKETS, NG * ATT_HEADS), F32),
        compiler_params=_params(),
    )(dbias, buckets)


def _window_mask(first):
    qi = lax.broadcasted_iota(jnp.int32, (BLK, 2 * BLK), 0)
    kj = lax.broadcasted_iota(jnp.int32, (BLK, 2 * BLK), 1)
    rel = BLK + qi - kj
    return (rel >= 0) & (rel <= BLK) & (kj >= BLK * first)


def _head_lanes(hh):
    lane = lax.broadcasted_iota(jnp.int32, (1, PAIR), 1)
    return (lane >= hh * HEAD_DIM) & (lane < (hh + 1) * HEAD_DIM)


def _attn_fwd(name, g, qkv, qc, kc, vc, bias, stride):
    T = qkv.shape[0]
    nb = T // BLK
    scale = HEAD_DIM ** -0.5

    def body(q_ref, kp_ref, kc_ref, vp_ref, vc_ref, bias_ref, out_ref):
        b = pl.program_id(0)
        valid = _window_mask((b < stride).astype(jnp.int32))
        low = _head_lanes(0)
        for hp in range(ATT_HEADS // 2):
            cols = slice(hp * PAIR, (hp + 1) * PAIR)
            qp = q_ref[:, cols]
            kk = jnp.concatenate([kp_ref[:, cols], kc_ref[:, cols]], axis=0)
            vv = jnp.concatenate([vp_ref[:, cols], vc_ref[:, cols]], axis=0)
            o_h, lse_h = [], []
            for hh in range(2):
                qm = jnp.where(_head_lanes(hh), qp, jnp.zeros_like(qp))
                s = _dot(qm, kk, NT) * scale
                logits = jnp.where(valid, s + bias_ref[g, 2 * hp + hh], NEG_INF)
                m = jnp.max(logits, axis=-1, keepdims=True)
                p = jnp.exp(logits - m)
                den = jnp.sum(p, axis=-1, keepdims=True)
                o_h.append(_dot(p.astype(BF16), vv) / den)
                lse_h.append(m + jnp.log(den))
            out_ref[:, cols] = jnp.where(low, o_h[0], o_h[1])
            out_ref[:, slice(ATT_WIDTH + hp * PAIR, ATT_WIDTH + (hp + 1) * PAIR)] = jnp.where(low, lse_h[0], lse_h[1])

    def cur(c):
        return pl.BlockSpec((BLK, ATT_WIDTH), lambda b: (b, c))

    def prev(c):
        return pl.BlockSpec((BLK, ATT_WIDTH), lambda b: (jnp.maximum(b - stride, 0), c))

    return pl.pallas_call(
        body, name=name, grid=(nb,),
        in_specs=[cur(qc), prev(kc), cur(kc), prev(vc), cur(vc),
                  pl.BlockSpec(bias.shape, lambda b: (0, 0, 0, 0))],
        out_specs=pl.BlockSpec((BLK, 2 * ATT_WIDTH), lambda b: (b, 0)),
        out_shape=jax.ShapeDtypeStruct((T, 2 * ATT_WIDTH), F32),
        compiler_params=_params(dimension_semantics=("parallel",)),
    )(qkv, qkv, qkv, qkv, qkv, bias)


def _attn_merge(parts):
    T = parts[0].shape[0]
    tr = _row_tile(T)
    n = len(parts)

    def body(*refs):
        o_refs, l_refs = refs[:n], refs[n:2 * n]
        o_ref, lse_ref = refs[2 * n], refs[2 * n + 1]
        ls = [r[...] for r in l_refs]
        m = functools.reduce(jnp.maximum, ls)
        es = [jnp.exp(l - m) for l in ls]
        tot = functools.reduce(lambda x, y: x + y, es)
        acc = functools.reduce(lambda x, y: x + y, [e * r[...] for e, r in zip(es, o_refs)])
        o_ref[...] = (acc / tot).astype(BF16)
        lse_ref[...] = m + jnp.log(tot)

    row = pl.BlockSpec((tr, ATT_WIDTH), lambda i: (i, 0))
    row_lse = pl.BlockSpec((tr, ATT_WIDTH), lambda i: (i, 1))
    return pl.pallas_call(
        body, name="attn_merge", grid=(T // tr,),
        in_specs=[row] * n + [row_lse] * n, out_specs=[row, row],
        out_shape=[jax.ShapeDtypeStruct((T, ATT_WIDTH), BF16), jax.ShapeDtypeStruct((T, ATT_WIDTH), F32)],
        compiler_params=_params(dimension_semantics=("parallel",)),
    )(*parts, *parts)


def _attn_bwd(name, g, qkv, qc, kc, vc, do, o, lse, bias, stride):
    T = qkv.shape[0]
    nb = T // BLK
    scale = HEAD_DIM ** -0.5

    def body(q_ref, kp_ref, kc_ref, vp_ref, vc_ref, do_ref, o_ref, lse_ref, bias_ref,
             dq_ref, dkv_ref, db_ref, carry_k, carry_v):
        b = pl.program_id(0)
        ck_ref = carry_k.at[b % stride]
        cv_ref = carry_v.at[b % stride]

        @pl.when(b == 0)
        def _():
            db_ref[...] = jnp.zeros_like(db_ref)
            carry_k[...] = jnp.zeros_like(carry_k)
            carry_v[...] = jnp.zeros_like(carry_v)

        @pl.when(b >= nb)
        def _():
            dkv_ref[:, :ATT_WIDTH] = ck_ref[...].astype(BF16)
            dkv_ref[:, ATT_WIDTH:] = cv_ref[...].astype(BF16)

        @pl.when(b < nb)
        def _():
            valid = _window_mask((b < stride).astype(jnp.int32))
            for hp in range(ATT_HEADS // 2):
                cols = slice(hp * PAIR, (hp + 1) * PAIR)
                qp = q_ref[:, cols]
                kk = jnp.concatenate([kp_ref[:, cols], kc_ref[:, cols]], axis=0)
                vv = jnp.concatenate([vp_ref[:, cols], vc_ref[:, cols]], axis=0)
                dop = do_ref[:, cols]
                lsep = lse_ref[:, cols]
                prod = dop.astype(F32) * o_ref[:, cols].astype(F32)
                dq = jnp.zeros((BLK, PAIR), F32)
                dk = jnp.zeros((2 * BLK, PAIR), F32)
                dv = jnp.zeros((2 * BLK, PAIR), F32)
                for hh in range(2):
                    lanes = _head_lanes(hh)
                    qm = jnp.where(lanes, qp, jnp.zeros_like(qp))
                    dom = jnp.where(lanes, dop, jnp.zeros_like(dop))
                    km = jnp.where(lanes, kk, jnp.zeros_like(kk))
                    delta = jnp.sum(jnp.where(lanes, prod, 0.0), axis=-1, keepdims=True)
                    lse_h = jnp.max(jnp.where(lanes, lsep, NEG_INF), axis=-1, keepdims=True)
                    s = _dot(qm, kk, NT) * scale
                    logits = jnp.where(valid, s + bias_ref[g, 2 * hp + hh], NEG_INF)
                    p = jnp.exp(logits - lse_h)
                    dv += _dot(p.astype(BF16), dom, TN)
                    ds = p * (_dot(dom, vv, NT) - delta)
                    db_ref[2 * hp + hh] += ds
                    dss = (ds * scale).astype(BF16)
                    dq += _dot(dss, km)
                    dk += _dot(dss, qm, TN)
                dq_ref[:, cols] = dq.astype(BF16)
                dkv_ref[:, cols] = (ck_ref[:, cols] + dk[:BLK]).astype(BF16)
                dkv_ref[:, slice(ATT_WIDTH + hp * PAIR, ATT_WIDTH + (hp + 1) * PAIR)] = (
                    cv_ref[:, cols] + dv[:BLK]).astype(BF16)
                ck_ref[:, cols] = dk[BLK:]
                cv_ref[:, cols] = dv[BLK:]

    last = nb - 1

    def cur(c):
        return pl.BlockSpec((BLK, ATT_WIDTH), lambda b: (jnp.minimum(b, last), c))

    def prev(c):
        return pl.BlockSpec((BLK, ATT_WIDTH), lambda b: (jnp.clip(b - stride, 0, last), c))

    dbias_shape = (ATT_HEADS, BLK, 2 * BLK)
    return pl.pallas_call(
        body, name=name, grid=(nb + stride,),
        in_specs=[cur(qc), prev(kc), cur(kc), prev(vc), cur(vc), cur(0), cur(0), cur(0),
                  pl.BlockSpec(bias.shape, lambda b: (0, 0, 0, 0))],
        out_specs=[cur(0), pl.BlockSpec((BLK, 2 * ATT_WIDTH), lambda b: (jnp.clip(b - stride, 0, last), 0)),
                   pl.BlockSpec(dbias_shape, lambda b: (0, 0, 0))],
        out_shape=[jax.ShapeDtypeStruct((T, ATT_WIDTH), BF16), jax.ShapeDtypeStruct((T, 2 * ATT_WIDTH), BF16),
                   jax.ShapeDtypeStruct(dbias_shape, F32)],
        scratch_shapes=[pltpu.VMEM((stride, BLK, ATT_WIDTH), F32), pltpu.VMEM((stride, BLK, ATT_WIDTH), F32)],
        compiler_params=_params(dimension_semantics=("arbitrary",)),
    )(qkv, qkv, qkv, qkv, qkv, do, o, lse, bias)


REORDER_TILE = 256


def _reorder_matrix(d, inverse):
    per = REORDER_TILE // d
    p = np.zeros((REORDER_TILE, REORDER_TILE), np.float32)
    for src in range(REORDER_TILE):
        i, r = divmod(src, d)
        p[r * per + i, src] = 1.0
    return jnp.asarray(p.T if inverse else p, dtype=BF16)


def _reorder_rows(name, src, d, inverse, *, src_col=0, col_stride=1, ncols=1, dst=None, dst_col=0, dst_stride=1,
                  dst_blocks=None):
    T = src.shape[0]
    dtype = src.dtype
    span = BLK * d if d > 1 else min(T, 1024)
    per = REORDER_TILE // d
    tiles = span // REORDER_TILE
    dst_blocks = ncols if dst_blocks is None else dst_blocks

    def apply(p, x):
        if dtype == BF16:
            return _dot(p, x).astype(BF16)
        hi = x.astype(BF16)
        rest = x - hi.astype(F32)
        mid = rest.astype(BF16)
        low = (rest - mid.astype(F32)).astype(BF16)
        return _dot(p, hi) + _dot(p, mid) + _dot(p, low)

    def body(*refs):
        p_ref, x_ref, o_ref = refs[0], refs[1], refs[-1]
        if d == 1:
            o_ref[...] = x_ref[...]
            return
        for t in range(tiles):
            tile_rows = slice(t * REORDER_TILE, (t + 1) * REORDER_TILE)
            chunk = lambda r: slice(r * BLK + t * per, r * BLK + (t + 1) * per)
            if inverse:
                gathered = jnp.concatenate([x_ref[chunk(r), :] for r in range(d)], axis=0)
                o_ref[tile_rows, :] = apply(p_ref[...], gathered)
            else:
                y = apply(p_ref[...], x_ref[tile_rows, :])
                for r in range(d):
                    o_ref[chunk(r), :] = y[r * per:(r + 1) * per]

    in_specs = [pl.BlockSpec((REORDER_TILE, REORDER_TILE), lambda w, k: (0, 0)),
                pl.BlockSpec((span, ATT_WIDTH), lambda w, k: (w, src_col + col_stride * k))]
    operands = [_reorder_matrix(max(d, 2), inverse), src]
    aliases = {}
    if dst is not None:
        in_specs.append(ANY)
        operands.append(dst)
        aliases = {2: 0}
    return pl.pallas_call(
        body, name=name, grid=(T // span, ncols), in_specs=in_specs,
        out_specs=pl.BlockSpec((span, ATT_WIDTH), lambda w, k: (w, dst_col + dst_stride * k)),
        out_shape=jax.ShapeDtypeStruct((T, dst_blocks * ATT_WIDTH), dtype),
        input_output_aliases=aliases,
        compiler_params=_params(dimension_semantics=("parallel", "parallel")),
    )(*operands)


def _group_qkv(qkv, g, d):
    NG = len(DILATIONS)
    if d == 1:
        return qkv, (g, NG + g, 2 * NG + g)
    return _reorder_rows(f"qkv_to_residues{g}", qkv, d, False, src_col=g, col_stride=NG, ncols=3), (0, 1, 2)


def _attention_fwd(qkv, bias):
    T = qkv.shape[0]
    parts = []
    for g, d in enumerate(DILATIONS):
        src, (qc, kc, vc) = _group_qkv(qkv, g, d)
        part = _attn_fwd(f"attn_fwd_{g}", g, src, qc, kc, vc, bias, d)
        parts.append(part if d == 1 else _reorder_rows(f"out_to_positions{g}", part, d, True, ncols=2))
    return _attn_merge(parts)


def _attention_bwd(qkv, do, o, lse, bias):
    T = qkv.shape[0]
    NG = len(DILATIONS)
    dqkv, dbs = None, []
    for g, d in enumerate(DILATIONS):
        src, (qc, kc, vc) = _group_qkv(qkv, g, d)
        do_g, o_g, lse_g = do, o, lse
        if d > 1:
            do_g = _reorder_rows(f"do_to_residues{g}", do, d, False)
            o_g = _reorder_rows(f"o_to_residues{g}", o, d, False)
            lse_g = _reorder_rows(f"lse_to_residues{g}", lse, d, False)
        dq, dkv, db = _attn_bwd(f"attn_bwd_{g}", g, src, qc, kc, vc, do_g, o_g, lse_g, bias, d)
        dqkv = _reorder_rows(f"dq_to_positions{g}", dq, d, True, dst=dqkv, dst_col=g, dst_blocks=3 * NG)
        dqkv = _reorder_rows(f"dkv_to_positions{g}", dkv, d, True, ncols=2, dst=dqkv, dst_col=NG + g, dst_stride=NG,
                             dst_blocks=3 * NG)
        dbs.append(db)
    return dqkv, jnp.stack(dbs)


def _other_chips(x, y):
    return [(1 - x, y), (x, 1 - y), (1 - x, 1 - y)]


def _shard_region(ref, shape, by_cols, chip, rows=None):
    R, C = shape
    start, size = (0, R) if rows is None else rows
    if by_cols:
        return ref.at[pl.ds(start, size), pl.ds(chip * C, C)]
    return ref.at[pl.ds(chip * R + start, size), :]


def _gather_weights(entries):
    n = len(entries)
    shapes = [e[0].shape[1:] for e in entries]

    def places(ins, outs, sems):
        send_sems, recv_sems, local_sems = sems
        x, y, c = lax.axis_index("x"), lax.axis_index("y"), lax.axis_index("c")

        def landing(f, px, py, pc):
            R = shapes[f][0]
            return _shard_region(outs[f], shapes[f], entries[f][2], 2 * px + py, rows=(pc * (R // 2), R // 2))

        def copy(f, k, block, to, src=None):
            dst = landing(f, *block)
            return pltpu.make_async_remote_copy(
                src_ref=dst if src is None else src, dst_ref=dst,
                send_sem=send_sems.at[6 * f + k], recv_sem=recv_sems.at[6 * f + k],
                device_id=to, device_id_type=MESH)

        def mine(f):
            dst = _shard_region(outs[f], shapes[f], entries[f][2], 2 * x + y)
            return pltpu.make_async_copy(ins[f].at[entries[f][1]], dst, local_sems.at[f])

        def first(f, j):
            R = shapes[f][0]
            src = ins[f].at[entries[f][1], pl.ds(c * (R // 2), R // 2), :]
            return copy(f, j, (x, y, c), (*_other_chips(x, y)[j], c), src=src)

        return x, y, c, copy, mine, first

    def start(ins, outs, sems):
        _, _, _, _, mine, first = places(ins, outs, sems)
        for f in range(n):
            mine(f).start()
        for j in range(3):
            for f in range(n):
                first(f, j).start()

    def mid(ins, outs, sems):
        x, y, c, copy, _, _ = places(ins, outs, sems)
        for j, chip in enumerate(_other_chips(x, y)):
            for f in range(n):
                copy(f, j, (*chip, c), (x, y, c)).wait_recv()
                copy(f, 3 + j, (*chip, c), (x, y, 1 - c)).start()

    def end(ins, outs, sems):
        x, y, c, copy, mine, first = places(ins, outs, sems)
        for j, chip in enumerate(_other_chips(x, y)):
            for f in range(n):
                copy(f, 3 + j, (*chip, 1 - c), (x, y, c)).wait_recv()
        for j, chip in enumerate(_other_chips(x, y)):
            for f in range(n):
                first(f, j).wait_send()
                copy(f, 3 + j, (*chip, c), (x, y, 1 - c)).wait_send()
        for f in range(n):
            mine(f).wait()

    def whole(f):
        R, C = shapes[f]
        return (R, N_CHIPS * C) if entries[f][2] else (N_CHIPS * R, C)

    return _Comm(
        [e[0] for e in entries], [jax.ShapeDtypeStruct(whole(f), BF16) for f in range(n)],
        [pltpu.SemaphoreType.DMA((6 * n,)), pltpu.SemaphoreType.DMA((6 * n,)), pltpu.SemaphoreType.DMA((n,))],
        start, end, mid)


def _scatter_grads(entries):
    n = len(entries)

    def copies(ins, outs, sems):
        send_sems, recv_sems, local_sems = sems
        x, y, c = lax.axis_index("x"), lax.axis_index("y"), lax.axis_index("c")
        me = 2 * x + y

        def piece(f, chip):
            return _shard_region(ins[f], entries[f][1], entries[f][2], chip)

        mine = [pltpu.make_async_copy(piece(f, me), outs[f].at[me], local_sems.at[f]) for f in range(n)]
        sends = [pltpu.make_async_remote_copy(
            src_ref=piece(f, 2 * px + py), dst_ref=outs[f].at[me],
            send_sem=send_sems.at[3 * f + j], recv_sem=recv_sems.at[3 * f + j],
            device_id=(px, py, c), device_id_type=MESH)
            for j, (px, py) in enumerate(_other_chips(x, y)) for f in range(n)]
        return mine, sends

    def start(ins, outs, sems):
        mine, sends = copies(ins, outs, sems)
        for cp in mine + sends:
            cp.start()

    def end(ins, outs, sems):
        mine, sends = copies(ins, outs, sems)
        for cp in sends + mine:
            cp.wait()

    return _Comm(
        [e[0] for e in entries], [jax.ShapeDtypeStruct((N_CHIPS,) + tuple(e[1]), BF16) for e in entries],
        [pltpu.SemaphoreType.DMA((3 * n,)), pltpu.SemaphoreType.DMA((3 * n,)), pltpu.SemaphoreType.DMA((n,))],
        start, end)


def _exchange_sibling(parts):
    n = len(parts)

    def copies(ins, outs, sems):
        send_sems, recv_sems = sems
        sibling = (lax.axis_index("x"), lax.axis_index("y"), 1 - lax.axis_index("c"))
        return [pltpu.make_async_remote_copy(src_ref=ins[i], dst_ref=outs[i], send_sem=send_sems.at[i],
                                             recv_sem=recv_sems.at[i], device_id=sibling, device_id_type=MESH)
                for i in range(n)]

    def start(ins, outs, sems):
        for cp in copies(ins, outs, sems):
            cp.start()

    def end(ins, outs, sems):
        for cp in copies(ins, outs, sems):
            cp.wait()

    return _Comm(parts, [jax.ShapeDtypeStruct(s.shape, s.dtype) for s in parts],
                 [pltpu.SemaphoreType.DMA((n,)), pltpu.SemaphoreType.DMA((n,))], start, end)


def _allgather_small(block):
    m_per, ncol = block.shape

    def body(x_ref, out_ref, send_sems, recv_sems, local_sem):
        x, y, c = lax.axis_index("x"), lax.axis_index("y"), lax.axis_index("c")
        me, sibling = (x, y, c), (x, y, 1 - c)
        chips = _other_chips(x, y)

        def rows(px, py, pc):
            return out_ref.at[4 * px + 2 * py + pc]

        def copy(k, block_of, to, src=None):
            return pltpu.make_async_remote_copy(
                src_ref=rows(*block_of) if src is None else src, dst_ref=rows(*block_of),
                send_sem=send_sems.at[k], recv_sem=recv_sems.at[k], device_id=to, device_id_type=MESH)

        mine = pltpu.make_async_copy(x_ref, rows(*me), local_sem)
        mine.start()
        first = [copy(0, me, sibling, src=x_ref)]
        first += [copy(1 + j, me, (*chip, c), src=x_ref) for j, chip in enumerate(chips)]
        for cp in first:
            cp.start()
        passed = [copy(4 + j, (*chip, c), sibling) for j, chip in enumerate(chips)]
        for j, chip in enumerate(chips):
            copy(1 + j, (*chip, c), me).wait_recv()
            passed[j].start()
        copy(0, sibling, me).wait_recv()
        for j, chip in enumerate(chips):
            copy(4 + j, (*chip, 1 - c), me).wait_recv()
        for cp in first + passed:
            cp.wait_send()
        mine.wait()

    return pl.pallas_call(
        body, name="allgather_small",
        in_specs=[pl.BlockSpec(memory_space=pltpu.VMEM)], out_specs=pl.BlockSpec(memory_space=pltpu.VMEM),
        out_shape=jax.ShapeDtypeStruct((N_DEV, m_per, ncol), block.dtype),
        scratch_shapes=[pltpu.SemaphoreType.DMA((7,)), pltpu.SemaphoreType.DMA((7,)), pltpu.SemaphoreType.DMA],
        compiler_params=_params(),
    )(block)


def _adamw(w, g, m, v):
    m = ADAM_B1 * m + (1.0 - ADAM_B1) * g
    v = ADAM_B2 * v + (1.0 - ADAM_B2) * jnp.square(g)
    m_hat = m / (1.0 - ADAM_B1 ** ADAM_STEP)
    v_hat = v / (1.0 - ADAM_B2 ** ADAM_STEP)
    delta = -ADAM_LR * (m_hat / (jnp.sqrt(v_hat) + ADAM_EPS) + ADAM_WD * w)
    return delta, m, v


def _flat_tile(rows):
    return min(rows, 256)


def _sum_pieces(name, layers):
    L = len(layers)
    P, R, C = layers[0].shape
    tr = _flat_tile(R)

    def body(*refs):
        out_ref = refs[L]
        for l in range(L):
            @pl.when(pl.program_id(0) == l)
            def _(p_ref=refs[l]):
                acc = p_ref[0].astype(F32)
                for j in range(1, P):
                    acc = acc + p_ref[j].astype(F32)
                out_ref[...] = acc

    return pl.pallas_call(
        body, name=name, grid=(L, R // tr),
        in_specs=[pl.BlockSpec((P, tr, C), lambda l, i: (0, i, 0)) for _ in range(L)],
        out_specs=pl.BlockSpec((None, tr, C), lambda l, i: (l, i, 0)),
        out_shape=jax.ShapeDtypeStruct((L, R, C), F32),
        compiler_params=_params(dimension_semantics=("parallel", "parallel")),
    )(*layers)


def _adam_pair(name, w, m, v, part_a, part_b):
    L, R, C = w.shape
    tr = _flat_tile(R)

    def body(w_ref, m_ref, v_ref, a_ref, b_ref, g_ref, d_ref, nm_ref, nv_ref):
        g = a_ref[...] + b_ref[...]
        g_ref[...] = g
        d_ref[...], nm_ref[...], nv_ref[...] = _adamw(w_ref[...], g, m_ref[...], v_ref[...])

    row = pl.BlockSpec((None, tr, C), lambda l, i: (l, i, 0))
    return pl.pallas_call(
        body, name=name, grid=(L, R // tr),
        in_specs=[row] * 5, out_specs=[row] * 4,
        out_shape=[jax.ShapeDtypeStruct((L, R, C), F32)] * 4,
        compiler_params=_params(dimension_semantics=("parallel", "parallel")),
    )(w, m, v, part_a, part_b)


def _adam_small(w, m, v, gathered):
    R, C = w.shape

    def body(w_ref, m_ref, v_ref, p_ref, g_ref, d_ref, nm_ref, nv_ref):
        g = p_ref[0]
        for j in range(1, N_DEV):
            g = g + p_ref[j]
        g_ref[...] = g
        d_ref[...], nm_ref[...], nv_ref[...] = _adamw(w_ref[...], g, m_ref[...], v_ref[...])

    return pl.pallas_call(
        body, name="adam_small",
        out_shape=[jax.ShapeDtypeStruct((R, C), F32)] * 4,
        compiler_params=_params(),
    )(w, m, v, gathered)


SMALL = ("mix_norm_g", "mlp_norm_g", "final_norm_g", "a_ln_g", "a_ln_b", "a_w_s", "a_b_s", "rel_bias")


def _pack_small(arrays, width):
    rows = []
    for a in arrays:
        flat = a.reshape(-1)
        pad = (-flat.shape[0]) % width
        rows.append(jnp.pad(flat, (0, pad)).reshape(-1, width))
    block = jnp.concatenate(rows, axis=0)
    return jnp.pad(block, ((0, (-block.shape[0]) % 8), (0, 0)))


def _unpack_small(block, shapes, width):
    out, row = [], 0
    for shape in shapes:
        size = int(np.prod(shape))
        nrows = -(-size // width)
        out.append(block[row:row + nrows].reshape(-1)[:size].reshape(shape))
        row += nrows
    return out


def kernel(x, mix_norm_g, mlp_norm_g, final_norm_g, a_w_in, a_ln_g, a_ln_b, a_w_s, a_b_s, a_w_out, b_w_qkv, b_w_out, rel_bias, w_up, w_down, loss_target, m_mix_norm_g, m_mlp_norm_g, m_final_norm_g, m_a_w_in, m_a_ln_g, m_a_ln_b, m_a_w_s, m_a_b_s, m_a_w_out, m_b_w_qkv, m_b_w_out, m_rel_bias, m_w_up, m_w_down, v_mix_norm_g, v_mlp_norm_g, v_final_norm_g, v_a_w_in, v_a_ln_g, v_a_ln_b, v_a_w_s, v_a_b_s, v_a_w_out, v_b_w_qkv, v_b_w_out, v_rel_bias, v_w_up, v_w_down):
    T, D = x.shape[1], x.shape[2]
    h0 = x.reshape(T, D)
    target = loss_target.reshape(T, D)
    G = a_w_s.shape[1]

    w_big = [a_w_in, a_w_out, b_w_qkv, b_w_out, w_up, w_down]
    m_big = [m_a_w_in, m_a_w_out, m_b_w_qkv, m_b_w_out, m_w_up, m_w_down]
    v_big = [v_a_w_in, v_a_w_out, v_b_w_qkv, v_b_w_out, v_w_up, v_w_down]
    by_cols = [True, False, True, True, True, False]
    s_in, s_out, s_qkv, s_bo, s_up, s_dn = [w.astype(BF16) for w in w_big]
    W_in, W_out = _run_comm("gather_a", _gather_weights([(s_in, 0, True), (s_out, 0, False)]))

    tril = jnp.tril(jnp.ones((CHUNK, CHUNK), dtype=bool))
    w_tril = jnp.where(tril[None], a_w_s[0], 0.0).astype(BF16)
    w_tril_t = jnp.swapaxes(w_tril, 1, 2)
    b_rows = jnp.broadcast_to(a_b_s[0][:, :, None], (G, CHUNK, CHUNK))
    buckets = _bucket_maps()
    bias = _bias_build(rel_bias, buckets)

    QKV = s_qkv.shape[2] * N_CHIPS
    TM = 1024
    TK_WGRAD = 2048

    def matmul(name, a, b, mode, out, tm=TM, tn=1024, epi=None, extras=(), comm=None):
        return _mm(name, a, b, mode, tm=tm, tn=tn, tk=a.shape[1], outs=[out], epi=epi, extras=extras, comm=comm)

    def wgrad(name, a, b, tn=1024, comm=None):
        return _mm(name, a, b, "tn", tm=1024, tn=tn, tk=TK_WGRAD, outs=[BF16], comm=comm)

    def scatter(*which):
        return _scatter_grads([(g, w_big[i].shape[1:], by_cols[i]) for g, i in which])

    y0 = _rms_fwd("rms_fwd_mix0", h0, mix_norm_g[0:1])
    a_pre, (W_up0,) = matmul("a_in", y0, W_in, "nn", BF16, comm=_gather_weights([(s_up, 0, True)]))
    z = _gate_fwd(a_pre, a_ln_g, a_ln_b, w_tril, b_rows)
    h1 = matmul("a_out", z, W_out, "nn", F32, epi=_epi_residual, extras=(h0,))
    y1 = _rms_fwd("rms_fwd_mlp0", h1, mlp_norm_g[0:1])
    q1, (W_dn0,) = matmul("mlp_up0", y1, W_up0, "nn", BF16, epi=_epi_relu2, comm=_gather_weights([(s_dn, 0, False)]))
    h2, (W_qkv, W_bo) = matmul("mlp_down0", q1, W_dn0, "nn", F32, tm=TM // 2, epi=_epi_residual, extras=(h1,),
                               comm=_gather_weights([(s_qkv, 0, True), (s_bo, 0, True)]))
    y2 = _rms_fwd("rms_fwd_mix1", h2, mix_norm_g[1:2])
    qkv, (W_up1, W_dn1) = matmul("b_qkv", y2, W_qkv, "nn", BF16, tn=QKV // 4,
                                 comm=_gather_weights([(s_up, 1, True), (s_dn, 1, False)]))
    o, lse = _attention_fwd(qkv, bias)
    h3 = matmul("b_out", o, W_bo, "nn", F32, epi=_epi_residual, extras=(h2,))
    y3 = _rms_fwd("rms_fwd_mlp1", h3, mlp_norm_g[1:2])
    q3 = matmul("mlp_up1", y3, W_up1, "nn", BF16, epi=_epi_relu2)
    h4 = matmul("mlp_down1", q3, W_dn1, "nn", F32, tm=TM // 2, epi=_epi_residual, extras=(h3,))
    dh4, dh4_b, d_final_g, loss_tile = _loss_head(h4, final_norm_g.reshape(1, D), target)
    loss = lax.psum(loss_tile[0, 0], ("x", "y", "c"))

    dp3 = matmul("mlp_down_bwd1", dh4_b, W_dn1, "nt", BF16, epi=_epi_relu2_grad, extras=(q3,))
    g_dn1 = wgrad("mlp_down_wgrad1", q3, dh4_b)
    dy3, (r_dn1,) = matmul("mlp_up_bwd1", dp3, W_up1, "nt", F32, tm=TM // 2, comm=scatter((g_dn1, 5)))
    g_up1 = wgrad("mlp_up_wgrad1", y3, dp3)
    dh3, dh3_b, dg_mlp1 = _rms_bwd("rms_bwd_mlp1", dy3, h3, mlp_norm_g[1:2], dh4)
    do = matmul("b_out_bwd", dh3_b, W_bo, "nt", BF16)
    g_bo = wgrad("b_out_wgrad", o, dh3_b)
    dqkv, dbias = _attention_bwd(qkv, do, o, lse, bias)
    d_rel_bias = _bias_scatter(dbias, buckets)
    dy2, (r_up1, r_bo) = matmul("b_qkv_bwd", dqkv, W_qkv, "nt", F32, tm=TM // 2, comm=scatter((g_up1, 4), (g_bo, 3)))
    g_qkv = wgrad("b_qkv_wgrad", y2, dqkv, tn=QKV // 4)
    dh2, dh2_b, dg_mix1 = _rms_bwd("rms_bwd_mix1", dy2, h2, mix_norm_g[1:2], dh3)
    dp1, (r_qkv,) = matmul("mlp_down_bwd0", dh2_b, W_dn0, "nt", BF16, epi=_epi_relu2_grad, extras=(q1,),
                           comm=scatter((g_qkv, 2)))
    g_dn0 = wgrad("mlp_down_wgrad0", q1, dh2_b)
    dy1, (r_dn0,) = matmul("mlp_up_bwd0", dp1, W_up0, "nt", F32, tm=TM // 2, comm=scatter((g_dn0, 5)))
    g_up0 = wgrad("mlp_up_wgrad0", y1, dp1)
    dh1, dh1_b, dg_mlp0 = _rms_bwd("rms_bwd_mlp0", dy1, h1, mlp_norm_g[0:1], dh2)
    dz = matmul("a_out_bwd", dh1_b, W_out, "nt", F32)
    g_out = wgrad("a_out_wgrad", z, dh1_b)
    da, d_ln_g, d_ln_b, d_w_s, d_b_s = _gate_bwd(a_pre, dz, a_ln_g, a_ln_b, w_tril, w_tril_t, b_rows)
    dy0, (r_up0,) = matmul("a_in_bwd", da, W_in, "nt", F32, comm=scatter((g_up0, 4)))
    g_in, (r_out,) = wgrad("a_in_wgrad", y0, da, comm=scatter((g_out, 1)))
    grad_x, _, dg_mix0 = _rms_bwd("rms_bwd_mix0", dy0, h0, mix_norm_g[0:1], dh1)
    (r_in,) = _run_comm("scatter_a_in", scatter((g_in, 0)))

    received = [[r_in], [r_out], [r_qkv], [r_bo], [r_up0, r_up1], [r_dn0, r_dn1]]
    plane = [_sum_pieces(f"sum_pieces{i}", r) for i, r in enumerate(received)]
    other = _run_comm("exchange_sibling", _exchange_sibling(plane))
    big_out = [_adam_pair(f"adam{i}", w_big[i], m_big[i], v_big[i], plane[i], other[i]) for i in range(len(w_big))]

    def unbig(kind):
        return dict(zip(["a_w_in", "a_w_out", "b_w_qkv", "b_w_out", "w_up", "w_down"], [b[kind] for b in big_out]))

    small_w = [mix_norm_g, mlp_norm_g, final_norm_g, a_ln_g, a_ln_b, a_w_s, a_b_s, rel_bias]
    small_m = [m_mix_norm_g, m_mlp_norm_g, m_final_norm_g, m_a_ln_g, m_a_ln_b, m_a_w_s, m_a_b_s, m_rel_bias]
    small_v = [v_mix_norm_g, v_mlp_norm_g, v_final_norm_g, v_a_ln_g, v_a_ln_b, v_a_w_s, v_a_b_s, v_rel_bias]
    small_g = [jnp.concatenate([dg_mix0, dg_mix1]), jnp.concatenate([dg_mlp0, dg_mlp1]), d_final_g,
               d_ln_g, d_ln_b, d_w_s[None], d_b_s[None, :, :, 0], d_rel_bias]
    width = max(D, 128)
    gathered_small = _allgather_small(_pack_small(small_g, width))
    small_out = _adam_small(_pack_small(small_w, width), _pack_small(small_m, width), _pack_small(small_v, width),
                            gathered_small)
    shapes = [w.shape for w in small_w]

    names = ["mix_norm_g", "mlp_norm_g", "final_norm_g", "a_w_in", "a_ln_g", "a_ln_b", "a_w_s", "a_b_s", "a_w_out",
             "b_w_qkv", "b_w_out", "rel_bias", "w_up", "w_down"]
    results = [loss, grad_x.reshape(x.shape)]
    for kind in range(4):
        table = dict(zip(SMALL, _unpack_small(small_out[kind], shapes, width)))
        table.update(unbig(kind))
        results += [table[n] for n in names]
    return tuple(results)
```

```python
import functools
import math

import numpy as np
import jax
import jax.numpy as jnp
from jax import lax
from jax.experimental import pallas as pl
from jax.experimental.pallas import tpu as pltpu

F32 = jnp.float32
BF16 = jnp.bfloat16
MESH = pl.DeviceIdType.MESH
ANY = pl.BlockSpec(memory_space=pl.ANY)

N_CHIPS = 4
N_DEV = 8
VMEM_LIMIT_BYTES = 56 * 1024 * 1024

EPS = 1e-6
NEG_INF = -1e30
CHUNK = 128
GROUP_DIM = 128
HEAD_DIM = 64
ATT_HEADS = 8
ATT_WIDTH = ATT_HEADS * HEAD_DIM
PAIR = 2 * HEAD_DIM
BLK = 128
DILATIONS = (1, 4, 16)
N_BUCKETS = 32
MAX_EXACT = N_BUCKETS // 2
REL_MAX_DISTANCE = 2048

ADAM_LR = 0.001
ADAM_B1 = 0.9
ADAM_B2 = 0.999
ADAM_EPS = 1e-08
ADAM_WD = 0.01
ADAM_STEP = 10

NN = (((1,), (0,)), ((), ()))
NT = (((1,), (1,)), ((), ()))
TN = (((0,), (0,)), ((), ()))


def _params(**kw):
    return pltpu.CompilerParams(vmem_limit_bytes=VMEM_LIMIT_BYTES, **kw)


def _dot(a, b, dims=NN):
    return lax.dot_general(a, b, dims, preferred_element_type=F32)


def _gelu(x):
    return 0.5 * x * (1.0 + lax.erf(x * math.sqrt(0.5)))


def _gelu_grad(x):
    return 0.5 * (1.0 + lax.erf(x * math.sqrt(0.5))) + x * jnp.exp(-0.5 * x * x) * (1.0 / math.sqrt(2.0 * math.pi))


def _mean(x):
    return jnp.mean(x, axis=-1, keepdims=True)


class _Comm:
    def __init__(self, inputs, out_shapes, scratch, start, end, mid=None):
        self.inputs, self.out_shapes, self.scratch = list(inputs), list(out_shapes), list(scratch)
        self.start, self.mid, self.end = start, mid, end


def _run_comm(name, comm):
    n_in, n_out = len(comm.inputs), len(comm.out_shapes)

    def body(*refs):
        parts = refs[:n_in], refs[n_in:n_in + n_out], refs[n_in + n_out:]
        comm.start(*parts)
        if comm.mid is not None:
            comm.mid(*parts)
        comm.end(*parts)

    return pl.pallas_call(
        body, name=name, in_specs=[ANY] * n_in, out_specs=[ANY] * n_out, out_shape=comm.out_shapes,
        scratch_shapes=comm.scratch, compiler_params=_params(),
    )(*comm.inputs)


def _mm(name, a, b, mode, *, tm, tn, tk, outs, epi=None, extras=(), vecs=(), col_sums=0, norm_gain=None, comm=None):
    if mode == "tn":
        K, M = a.shape
    else:
        M, K = a.shape
    N = b.shape[0] if mode == "nt" else b.shape[1]
    tm, tn, tk = min(tm, M), min(tn, N), min(tk, K)
    assert M % tm == 0 and N % tn == 0 and K % tk == 0, (name, M, N, K, tm, tn, tk)
    nk = K // tk
    grid = (M // tm, N // tn, nk)

    if mode == "tn":
        a_spec = pl.BlockSpec((tk, tm), lambda i, j, k: (k, i))
    else:
        a_spec = pl.BlockSpec((tm, tk), lambda i, j, k: (i, k))
    if mode == "nt":
        b_spec = pl.BlockSpec((tn, tk), lambda i, j, k: (j, k))
    else:
        b_spec = pl.BlockSpec((tk, tn), lambda i, j, k: (k, j))
    tile = pl.BlockSpec((tm, tn), lambda i, j, k: (i, j))
    vec = pl.BlockSpec((1, tn), lambda i, j, k: (0, j))
    normed = norm_gain is not None
    assert not normed or (mode == "nn" and nk == 1)
    assert col_sums == 0 or grid[1] == 1
    out_shapes = [jax.ShapeDtypeStruct((M, N), dtype) for dtype in outs]
    out_specs = [tile for _ in outs]
    if normed:
        out_shapes.append(jax.ShapeDtypeStruct((M, K), BF16))
        out_specs.append(pl.BlockSpec((tm, K), lambda i, j, k: (i, 0)))
    out_shapes += [jax.ShapeDtypeStruct((1, N), F32)] * col_sums
    out_specs += [vec] * col_sums
    extra_specs = [tile for _ in extras] + [vec for _ in vecs]
    if normed:
        extra_specs.append(pl.BlockSpec((1, K), lambda i, j, k: (0, 0)))
    n_extra, n_out = len(extra_specs), len(out_shapes)
    n_tiles = len(outs)
    n_cin = len(comm.inputs) if comm else 0
    n_cout = len(comm.out_shapes) if comm else 0
    dims = {"nn": NN, "nt": NT, "tn": TN}[mode]
    steps = grid[0] * grid[1] * grid[2]

    def body(*refs):
        a_ref, b_ref = refs[0], refs[1]
        pos = 2
        extra_refs = refs[pos:pos + n_extra]
        pos += n_extra
        comm_in = refs[pos:pos + n_cin]
        pos += n_cin
        out_refs = refs[pos:pos + n_out]
        pos += n_out
        comm_out = refs[pos:pos + n_cout]
        pos += n_cout
        acc_ref = refs[pos] if nk > 1 else None
        pos += nk > 1
        y_ref = refs[pos] if normed else None
        comm_sems = refs[pos + normed:]
        k = pl.program_id(2)
        step = (pl.program_id(0) * grid[1] + pl.program_id(1)) * nk + k

        if comm is not None:
            @pl.when(step == 0)
            def _():
                comm.start(comm_in, comm_out, comm_sems)

        if normed:
            @pl.when(pl.program_id(1) == 0)
            def _():
                hv = a_ref[...]
                y = (hv * lax.rsqrt(_mean(hv * hv) + EPS) * extra_refs[-1][...]).astype(BF16)
                y_ref[...] = y
                out_refs[n_tiles][...] = y

            lhs = y_ref[...]
        else:
            lhs = a_ref[...].astype(BF16)
        part = _dot(lhs, b_ref[...].astype(BF16), dims)

        def finish(acc):
            epi_args = [e[...] for e in extra_refs[:n_extra - normed]]
            res = epi(acc, *epi_args) if epi is not None else (acc,) * n_tiles
            for o, r in zip(out_refs[:n_tiles], res[:n_tiles]):
                o[...] = r.astype(o.dtype)
            if col_sums:
                sums = out_refs[n_out - col_sums:]

                @pl.when(pl.program_id(0) == 0)
                def _():
                    for o in sums:
                        o[...] = jnp.zeros_like(o)

                for o, r in zip(sums, res[n_tiles:]):
                    o[...] += r

        if nk == 1:
            finish(part)
        else:
            @pl.when(k == 0)
            def _():
                acc_ref[...] = part

            @pl.when(k > 0)
            def _():
                acc_ref[...] += part

            @pl.when(k == nk - 1)
            def _():
                finish(acc_ref[...])

        if comm is not None:
            if comm.mid is not None:
                @pl.when(step == (3 * steps) // 4)
                def _():
                    comm.mid(comm_in, comm_out, comm_sems)

            @pl.when(step == steps - 1)
            def _():
                comm.end(comm_in, comm_out, comm_sems)

    sequential = comm is not None or normed or col_sums > 0
    order = ("arbitrary",) * 3 if sequential else ("parallel", "parallel", "arbitrary")
    scratch = [pltpu.VMEM((tm, tn), F32)] if nk > 1 else []
    if normed:
        scratch.append(pltpu.VMEM((tm, K), BF16))
    res = pl.pallas_call(
        body, name=name, grid=grid,
        in_specs=[a_spec, b_spec] + extra_specs + [ANY] * n_cin,
        out_specs=out_specs + [ANY] * n_cout,
        out_shape=out_shapes + (comm.out_shapes if comm else []),
        scratch_shapes=scratch + (comm.scratch if comm else []),
        compiler_params=_params(dimension_semantics=order),
    )(a, b, *extras, *vecs, *([norm_gain] if normed else []), *(comm.inputs if comm else []))
    mm_out = res[0] if n_out == 1 else list(res[:n_out])
    return (mm_out, list(res[n_out:])) if comm else mm_out


def _epi_residual(acc, res):
    return (res + acc,)


def _epi_relu2(acc):
    return (jnp.square(jnp.maximum(acc, 0.0)),)


def _epi_rms_bwd(copies):
    def epi(acc, h, dres, g):
        r = lax.rsqrt(_mean(h * h) + EPS)
        hn = h * r
        dyg = acc * g
        dh = dres + r * (dyg - hn * _mean(dyg * hn))
        return (dh,) * copies + (jnp.sum(acc * hn, axis=0, keepdims=True),)
    return epi


def _epi_relu2_grad(acc, q):
    return (acc * (2.0 * jnp.sqrt(q.astype(F32))),)


def _row_tile(T):
    return min(T, 512)


def _loss_head(h, g, target):
    T, D = h.shape
    tr = _row_tile(T)

    def body(h_ref, g_ref, t_ref, dh_ref, dhb_ref, dg_ref, loss_ref):
        @pl.when(pl.program_id(0) == 0)
        def _():
            dg_ref[...] = jnp.zeros_like(dg_ref)
            loss_ref[...] = jnp.zeros_like(loss_ref)

        hv = h_ref[...]
        r = lax.rsqrt(_mean(hv * hv) + EPS)
        hn = hv * r
        gv = g_ref[...]
        diff = hn * gv - t_ref[...]
        loss_ref[...] += 0.5 * jnp.sum(_mean(diff * diff))
        dyv = diff * (1.0 / D)
        dg_ref[...] += jnp.sum(dyv * hn, axis=0, keepdims=True)
        dyg = dyv * gv
        dh = r * (dyg - hn * _mean(dyg * hn))
        dh_ref[...] = dh
        dhb_ref[...] = dh.astype(BF16)

    row = pl.BlockSpec((tr, D), lambda i: (i, 0))
    vec = pl.BlockSpec((1, D), lambda i: (0, 0))
    return pl.pallas_call(
        body, name="loss_head", grid=(T // tr,),
        in_specs=[row, vec, row], out_specs=[row, row, vec, pl.BlockSpec((8, 128), lambda i: (0, 0))],
        out_shape=[jax.ShapeDtypeStruct((T, D), F32), jax.ShapeDtypeStruct((T, D), BF16),
                   jax.ShapeDtypeStruct((1, D), F32), jax.ShapeDtypeStruct((8, 128), F32)],
        compiler_params=_params(dimension_semantics=("arbitrary",)),
    )(h, g, target)


def _gate_tile(T):
    return min(T, 256)


def _gate_fwd(a, ln_g, ln_b, w_tril, b_rows):
    T, W2 = a.shape
    W = W2 // 2
    G = W // GROUP_DIM
    tr = _gate_tile(T)

    def body(a_ref, lng_ref, lnb_ref, w_ref, b_ref, z_ref):
        u = _gelu(a_ref[:, :W].astype(F32))
        vg = _gelu(a_ref[:, W:].astype(F32))
        xc = vg - _mean(vg)
        vn = xc * lax.rsqrt(_mean(xc * xc) + EPS)
        vl = (vn * lng_ref[...] + lnb_ref[...]).astype(BF16)
        for n in range(tr // CHUNK):
            rows = slice(n * CHUNK, (n + 1) * CHUNK)
            for g in range(G):
                cols = slice(g * GROUP_DIM, (g + 1) * GROUP_DIM)
                gate = _dot(w_ref[g], vl[rows, cols]) + b_ref[g]
                z_ref[rows, cols] = (u[rows, cols] * gate).astype(BF16)

    vec = pl.BlockSpec((1, W), lambda i: (0, 0))
    grp = pl.BlockSpec((G, CHUNK, CHUNK), lambda i: (0, 0, 0))
    return pl.pallas_call(
        body, name="gate_fwd", grid=(T // tr,),
        in_specs=[pl.BlockSpec((tr, W2), lambda i: (i, 0)), vec, vec, grp, grp],
        out_specs=pl.BlockSpec((tr, W), lambda i: (i, 0)),
        out_shape=jax.ShapeDtypeStruct((T, W), BF16),
        compiler_params=_params(dimension_semantics=("parallel",)),
    )(a, ln_g, ln_b, w_tril, b_rows)


def _gate_bwd(a, dz, ln_g, ln_b, w_tril, w_tril_t, b_rows):
    T, W2 = a.shape
    W = W2 // 2
    G = W // GROUP_DIM
    tr = _gate_tile(T)
    steps = T // tr

    def body(a_ref, dz_ref, lng_ref, lnb_ref, w_ref, wt_ref, b_ref, da_ref, dlng_ref, dlnb_ref, dw_ref, dbs_ref, dvl_ref):
        step = pl.program_id(0)

        @pl.when(step == 0)
        def _():
            dlng_ref[...] = jnp.zeros_like(dlng_ref)
            dlnb_ref[...] = jnp.zeros_like(dlnb_ref)
            dw_ref[...] = jnp.zeros_like(dw_ref)
            dbs_ref[...] = jnp.zeros_like(dbs_ref)

        au = a_ref[:, :W].astype(F32)
        av = a_ref[:, W:].astype(F32)
        u = _gelu(au)
        vg = _gelu(av)
        xc = vg - _mean(vg)
        rstd = lax.rsqrt(_mean(xc * xc) + EPS)
        vn = xc * rstd
        lng = lng_ref[...]
        vl = (vn * lng + lnb_ref[...]).astype(BF16)
        du_scale = dz_ref[...] * _gelu_grad(au)
        dgate_all = dz_ref[...] * u
        for n in range(tr // CHUNK):
            rows = slice(n * CHUNK, (n + 1) * CHUNK)
            for g in range(G):
                cols = slice(g * GROUP_DIM, (g + 1) * GROUP_DIM)
                vlg = vl[rows, cols]
                gate = _dot(w_ref[g], vlg) + b_ref[g]
                da_ref[rows, cols] = (du_scale[rows, cols] * gate).astype(BF16)
                dgate = dgate_all[rows, cols]
                dbs_ref[g] += dgate
                dgate_b = dgate.astype(BF16)
                dw_ref[g] += _dot(dgate_b, vlg, NT)
                dvl_ref[rows, cols] = _dot(wt_ref[g], dgate_b)
        dvl = dvl_ref[...]
        dlnb_ref[...] += jnp.sum(dvl, axis=0, keepdims=True)
        dlng_ref[...] += jnp.sum(dvl * vn, axis=0, keepdims=True)
        dvn = dvl * lng
        dvg = rstd * (dvn - _mean(dvn) - vn * _mean(dvn * vn))
        da_ref[:, W:] = (dvg * _gelu_grad(av)).astype(BF16)

        @pl.when(step == steps - 1)
        def _():
            t_idx = lax.broadcasted_iota(jnp.int32, (CHUNK, CHUNK), 0)
            s_idx = lax.broadcasted_iota(jnp.int32, (CHUNK, CHUNK), 1)
            for g in range(G):
                dw_ref[g] = jnp.where(s_idx <= t_idx, dw_ref[g], 0.0)
                dbs_ref[g] = jnp.broadcast_to(jnp.sum(dbs_ref[g], axis=-1, keepdims=True), (CHUNK, CHUNK))

    vec = pl.BlockSpec((1, W), lambda i: (0, 0))
    grp = pl.BlockSpec((G, CHUNK, CHUNK), lambda i: (0, 0, 0))
    return pl.pallas_call(
        body, name="gate_bwd", grid=(steps,),
        in_specs=[pl.BlockSpec((tr, W2), lambda i: (i, 0)), pl.BlockSpec((tr, W), lambda i: (i, 0)),
                  vec, vec, grp, grp, grp],
        out_specs=[pl.BlockSpec((tr, W2), lambda i: (i, 0)), vec, vec, grp, grp],
        out_shape=[jax.ShapeDtypeStruct((T, W2), BF16), jax.ShapeDtypeStruct((1, W), F32),
                   jax.ShapeDtypeStruct((1, W), F32), jax.ShapeDtypeStruct((G, CHUNK, CHUNK), F32),
                   jax.ShapeDtypeStruct((G, CHUNK, CHUNK), F32)],
        scratch_shapes=[pltpu.VMEM((tr, W), F32)],
        compiler_params=_params(dimension_semantics=("arbitrary",)),
    )(a, dz, ln_g, ln_b, w_tril, w_tril_t, b_rows)


def _bucket_map(dilation):
    rel = BLK + np.arange(BLK)[:, None] - np.arange(2 * BLK)[None, :]
    dist = np.clip(rel, 0, BLK) * dilation
    nf = np.maximum(dist, 1).astype(np.float32)
    large = MAX_EXACT + (np.log(nf / np.float32(MAX_EXACT)) / np.float32(math.log(REL_MAX_DISTANCE / MAX_EXACT))
                         * np.float32(N_BUCKETS - MAX_EXACT)).astype(np.int32)
    large = np.minimum(large, N_BUCKETS - 1)
    return np.where(dist < MAX_EXACT, dist, large).astype(np.int32)


def _bucket_maps():
    return jnp.asarray(np.stack([_bucket_map(d) for d in DILATIONS]))


def _bias_build(rel_bias, buckets):
    NG = len(DILATIONS)

    def body(table_ref, bucket_ref, out_ref):
        out_ref[...] = jnp.zeros_like(out_ref)
        for g in range(NG):
            bk = bucket_ref[g]
            for b in range(N_BUCKETS):
                hit = bk == b
                for h in range(ATT_HEADS):
                    out_ref[g, h] = jnp.where(hit, table_ref[b, g * ATT_HEADS + h], out_ref[g, h])

    return pl.pallas_call(
        body, name="bias_build",
        in_specs=[pl.BlockSpec(memory_space=pltpu.SMEM), pl.BlockSpec(memory_space=pltpu.VMEM)],
        out_specs=pl.BlockSpec(memory_space=pltpu.VMEM),
        out_shape=jax.ShapeDtypeStruct((NG, ATT_HEADS, BLK, 2 * BLK), F32),
        compiler_params=_params(),
    )(rel_bias, buckets)


def _bias_scatter(dbias, buckets):
    NG = len(DILATIONS)

    def body(dbias_ref, bucket_ref, out_ref):
        for g in range(NG):
            bk = bucket_ref[g]
            for b in range(N_BUCKETS):
                hit = bk == b
                for h in range(ATT_HEADS):
                    out_ref[b, g * ATT_HEADS + h] = jnp.sum(jnp.where(hit, dbias_ref[g, h], 0.0))

    return pl.pallas_call(
        body, name="bias_scatter",
        in_specs=[pl.BlockSpec(memory_space=pltpu.VMEM), pl.BlockSpec(memory_space=pltpu.VMEM)],
        out_specs=pl.BlockSpec(memory_space=pltpu.SMEM),
        out_shape=jax.ShapeDtypeStruct((N_BUCKETS, NG * ATT_HEADS), F32),
        compiler_params=_params(),
    )(dbias, buckets)


def _window_mask(first):
    qi = lax.broadcasted_iota(jnp.int32, (BLK, 2 * BLK), 0)
    kj = lax.broadcasted_iota(jnp.int32, (BLK, 2 * BLK), 1)
    rel = BLK + qi - kj
    return (rel >= 0) & (rel <= BLK) & (kj >= BLK * first)


def _head_lanes(hh):
    lane = lax.broadcasted_iota(jnp.int32, (1, PAIR), 1)
    return (lane >= hh * HEAD_DIM) & (lane < (hh + 1) * HEAD_DIM)


def _attn_fwd(name, g, qkv, qc, kc, vc, bias, stride):
    T = qkv.shape[0]
    nb = T // BLK
    scale = HEAD_DIM ** -0.5

    def body(q_ref, kp_ref, kc_ref, vp_ref, vc_ref, bias_ref, out_ref):
        b = pl.program_id(0)
        valid = _window_mask((b < stride).astype(jnp.int32))
        low = _head_lanes(0)
        for hp in range(ATT_HEADS // 2):
            cols = slice(hp * PAIR, (hp + 1) * PAIR)
            qp = q_ref[:, cols]
            kk = jnp.concatenate([kp_ref[:, cols], kc_ref[:, cols]], axis=0)
            vv = jnp.concatenate([vp_ref[:, cols], vc_ref[:, cols]], axis=0)
            o_h, lse_h = [], []
            for hh in range(2):
                qm = jnp.where(_head_lanes(hh), qp, jnp.zeros_like(qp))
                s = _dot(qm, kk, NT) * scale
                logits = jnp.where(valid, s + bias_ref[g, 2 * hp + hh], NEG_INF)
                m = jnp.max(logits, axis=-1, keepdims=True)
                p = jnp.exp(logits - m)
                den = jnp.sum(p, axis=-1, keepdims=True)
                o_h.append(_dot(p.astype(BF16), vv) / den)
                lse_h.append(m + jnp.log(den))
            out_ref[:, cols] = jnp.where(low, o_h[0], o_h[1])
            out_ref[:, slice(ATT_WIDTH + hp * PAIR, ATT_WIDTH + (hp + 1) * PAIR)] = jnp.where(low, lse_h[0], lse_h[1])

    def cur(c):
        return pl.BlockSpec((BLK, ATT_WIDTH), lambda b: (b, c))

    def prev(c):
        return pl.BlockSpec((BLK, ATT_WIDTH), lambda b: (jnp.maximum(b - stride, 0), c))

    return pl.pallas_call(
        body, name=name, grid=(nb,),
        in_specs=[cur(qc), prev(kc), cur(kc), prev(vc), cur(vc),
                  pl.BlockSpec(bias.shape, lambda b: (0, 0, 0, 0))],
        out_specs=pl.BlockSpec((BLK, 2 * ATT_WIDTH), lambda b: (b, 0)),
        out_shape=jax.ShapeDtypeStruct((T, 2 * ATT_WIDTH), F32),
        compiler_params=_params(dimension_semantics=("parallel",)),
    )(qkv, qkv, qkv, qkv, qkv, bias)


def _attn_merge(parts):
    T = parts[0].shape[0]
    tr = _row_tile(T)
    n = len(parts)

    def body(*refs):
        o_refs, l_refs = refs[:n], refs[n:2 * n]
        o_ref, lse_ref = refs[2 * n], refs[2 * n + 1]
        ls = [r[...] for r in l_refs]
        m = functools.reduce(jnp.maximum, ls)
        es = [jnp.exp(l - m) for l in ls]
        tot = functools.reduce(lambda x, y: x + y, es)
        acc = functools.reduce(lambda x, y: x + y, [e * r[...] for e, r in zip(es, o_refs)])
        o_ref[...] = (acc / tot).astype(BF16)
        lse_ref[...] = m + jnp.log(tot)

    row = pl.BlockSpec((tr, ATT_WIDTH), lambda i: (i, 0))
    row_lse = pl.BlockSpec((tr, ATT_WIDTH), lambda i: (i, 1))
    return pl.pallas_call(
        body, name="attn_merge", grid=(T // tr,),
        in_specs=[row] * n + [row_lse] * n, out_specs=[row, row],
        out_shape=[jax.ShapeDtypeStruct((T, ATT_WIDTH), BF16), jax.ShapeDtypeStruct((T, ATT_WIDTH), F32)],
        compiler_params=_params(dimension_semantics=("parallel",)),
    )(*parts, *parts)


def _attn_bwd(name, g, qkv, qc, kc, vc, do, o, lse, bias, stride):
    T = qkv.shape[0]
    nb = T // BLK
    scale = HEAD_DIM ** -0.5

    def body(q_ref, kp_ref, kc_ref, vp_ref, vc_ref, do_ref, o_ref, lse_ref, bias_ref,
             dq_ref, dkv_ref, db_ref, carry_k, carry_v):
        b = pl.program_id(0)
        ck_ref = carry_k.at[b % stride]
        cv_ref = carry_v.at[b % stride]

        @pl.when(b == 0)
        def _():
            db_ref[...] = jnp.zeros_like(db_ref)
            carry_k[...] = jnp.zeros_like(carry_k)
            carry_v[...] = jnp.zeros_like(carry_v)

        @pl.when(b >= nb)
        def _():
            dkv_ref[:, :ATT_WIDTH] = ck_ref[...].astype(BF16)
            dkv_ref[:, ATT_WIDTH:] = cv_ref[...].astype(BF16)

        @pl.when(b < nb)
        def _():
            valid = _window_mask((b < stride).astype(jnp.int32))
            for hp in range(ATT_HEADS // 2):
                cols = slice(hp * PAIR, (hp + 1) * PAIR)
                qp = q_ref[:, cols]
                kk = jnp.concatenate([kp_ref[:, cols], kc_ref[:, cols]], axis=0)
                vv = jnp.concatenate([vp_ref[:, cols], vc_ref[:, cols]], axis=0)
                dop = do_ref[:, cols]
                lsep = lse_ref[:, cols]
                prod = dop.astype(F32) * o_ref[:, cols].astype(F32)
                dq = jnp.zeros((BLK, PAIR), F32)
                dk = jnp.zeros((2 * BLK, PAIR), F32)
                dv = jnp.zeros((2 * BLK, PAIR), F32)
                for hh in range(2):
                    lanes = _head_lanes(hh)
                    qm = jnp.where(lanes, qp, jnp.zeros_like(qp))
                    dom = jnp.where(lanes, dop, jnp.zeros_like(dop))
                    km = jnp.where(lanes, kk, jnp.zeros_like(kk))
                    delta = jnp.sum(jnp.where(lanes, prod, 0.0), axis=-1, keepdims=True)
                    lse_h = jnp.max(jnp.where(lanes, lsep, NEG_INF), axis=-1, keepdims=True)
                    s = _dot(qm, kk, NT) * scale
                    logits = jnp.where(valid, s + bias_ref[g, 2 * hp + hh], NEG_INF)
                    p = jnp.exp(logits - lse_h)
                    dv += _dot(p.astype(BF16), dom, TN)
                    ds = p * (_dot(dom, vv, NT) - delta)
                    db_ref[2 * hp + hh] += ds
                    dss = (ds * scale).astype(BF16)
                    dq += _dot(dss, km)
                    dk += _dot(dss, qm, TN)
                dq_ref[:, cols] = dq.astype(BF16)
                dkv_ref[:, cols] = (ck_ref[:, cols] + dk[:BLK]).astype(BF16)
                dkv_ref[:, slice(ATT_WIDTH + hp * PAIR, ATT_WIDTH + (hp + 1) * PAIR)] = (
                    cv_ref[:, cols] + dv[:BLK]).astype(BF16)
                ck_ref[:, cols] = dk[BLK:]
                cv_ref[:, cols] = dv[BLK:]

    last = nb - 1

    def cur(c):
        return pl.BlockSpec((BLK, ATT_WIDTH), lambda b: (jnp.minimum(b, last), c))

    def prev(c):
        return pl.BlockSpec((BLK, ATT_WIDTH), lambda b: (jnp.clip(b - stride, 0, last), c))

    dbias_shape = (ATT_HEADS, BLK, 2 * BLK)
    return pl.pallas_call(
        body, name=name, grid=(nb + stride,),
        in_specs=[cur(qc), prev(kc), cur(kc), prev(vc), cur(vc), cur(0), cur(0), cur(0),
                  pl.BlockSpec(bias.shape, lambda b: (0, 0, 0, 0))],
        out_specs=[cur(0), pl.BlockSpec((BLK, 2 * ATT_WIDTH), lambda b: (jnp.clip(b - stride, 0, last), 0)),
                   pl.BlockSpec(dbias_shape, lambda b: (0, 0, 0))],
        out_shape=[jax.ShapeDtypeStruct((T, ATT_WIDTH), BF16), jax.ShapeDtypeStruct((T, 2 * ATT_WIDTH), BF16),
                   jax.ShapeDtypeStruct(dbias_shape, F32)],
        scratch_shapes=[pltpu.VMEM((stride, BLK, ATT_WIDTH), F32), pltpu.VMEM((stride, BLK, ATT_WIDTH), F32)],
        compiler_params=_params(dimension_semantics=("arbitrary",)),
    )(qkv, qkv, qkv, qkv, qkv, do, o, lse, bias)


REORDER_TILE = 256
REORDER_ROWS = 2048


def _reorder_matrix(d, inverse):
    per = REORDER_TILE // d
    p = np.zeros((REORDER_TILE, REORDER_TILE), np.float32)
    for src in range(REORDER_TILE):
        i, r = divmod(src, d)
        p[r * per + i, src] = 1.0
    return jnp.asarray(p.T if inverse else p, dtype=BF16)


def _reorder_rows(name, src, d, inverse, *, src_col=0, col_stride=1, ncols=1, dst=None, dst_col=0, dst_stride=1,
                  dst_blocks=None):
    T = src.shape[0]
    dtype = src.dtype
    span = BLK * d
    rows = max(span, min(T, REORDER_ROWS))
    per = REORDER_TILE // d
    tiles = span // REORDER_TILE
    dst_blocks = ncols if dst_blocks is None else dst_blocks

    def apply(p, x):
        if dtype == BF16:
            return _dot(p, x).astype(BF16)
        hi = x.astype(BF16)
        rest = x - hi.astype(F32)
        mid = rest.astype(BF16)
        low = (rest - mid.astype(F32)).astype(BF16)
        return _dot(p, hi) + _dot(p, mid) + _dot(p, low)

    def body(*refs):
        p_ref, x_ref, o_ref = refs[0], refs[1], refs[-1]
        if d == 1:
            o_ref[...] = x_ref[...]
            return
        for s in range(rows // span):
            for t in range(tiles):
                base = s * span
                tile_rows = slice(base + t * REORDER_TILE, base + (t + 1) * REORDER_TILE)
                chunk = lambda r: slice(base + r * BLK + t * per, base + r * BLK + (t + 1) * per)
                if inverse:
                    gathered = jnp.concatenate([x_ref[chunk(r), :] for r in range(d)], axis=0)
                    o_ref[tile_rows, :] = apply(p_ref[...], gathered)
                else:
                    y = apply(p_ref[...], x_ref[tile_rows, :])
                    for r in range(d):
                        o_ref[chunk(r), :] = y[r * per:(r + 1) * per]

    in_specs = [pl.BlockSpec((REORDER_TILE, REORDER_TILE), lambda w, k: (0, 0)),
                pl.BlockSpec((rows, ATT_WIDTH), lambda w, k: (w, src_col + col_stride * k))]
    operands = [_reorder_matrix(max(d, 2), inverse), src]
    aliases = {}
    if dst is not None:
        in_specs.append(ANY)
        operands.append(dst)
        aliases = {2: 0}
    return pl.pallas_call(
        body, name=name, grid=(T // rows, ncols), in_specs=in_specs,
        out_specs=pl.BlockSpec((rows, ATT_WIDTH), lambda w, k: (w, dst_col + dst_stride * k)),
        out_shape=jax.ShapeDtypeStruct((T, dst_blocks * ATT_WIDTH), dtype),
        input_output_aliases=aliases,
        compiler_params=_params(dimension_semantics=("parallel", "parallel")),
    )(*operands)


def _group_qkv(qkv, g, d):
    NG = len(DILATIONS)
    if d == 1:
        return qkv, (g, NG + g, 2 * NG + g)
    return _reorder_rows(f"qkv_to_residues{g}", qkv, d, False, src_col=g, col_stride=NG, ncols=3), (0, 1, 2)


def _attention_fwd(qkv, bias):
    T = qkv.shape[0]
    parts = []
    for g, d in enumerate(DILATIONS):
        src, (qc, kc, vc) = _group_qkv(qkv, g, d)
        part = _attn_fwd(f"attn_fwd_{g}", g, src, qc, kc, vc, bias, d)
        parts.append(part if d == 1 else _reorder_rows(f"out_to_positions{g}", part, d, True, ncols=2))
    return _attn_merge(parts)


def _attention_bwd(qkv, do, o, lse, bias):
    T = qkv.shape[0]
    NG = len(DILATIONS)
    dqkv, dbs = None, []
    for g, d in enumerate(DILATIONS):
        src, (qc, kc, vc) = _group_qkv(qkv, g, d)
        do_g, o_g, lse_g = do, o, lse
        if d > 1:
            do_g = _reorder_rows(f"do_to_residues{g}", do, d, False)
            o_g = _reorder_rows(f"o_to_residues{g}", o, d, False)
            lse_g = _reorder_rows(f"lse_to_residues{g}", lse, d, False)
        dq, dkv, db = _attn_bwd(f"attn_bwd_{g}", g, src, qc, kc, vc, do_g, o_g, lse_g, bias, d)
        dqkv = _reorder_rows(f"dq_to_positions{g}", dq, d, True, dst=dqkv, dst_col=g, dst_blocks=3 * NG)
        dqkv = _reorder_rows(f"dkv_to_positions{g}", dkv, d, True, ncols=2, dst=dqkv, dst_col=NG + g, dst_stride=NG,
                             dst_blocks=3 * NG)
        dbs.append(db)
    return dqkv, jnp.stack(dbs)


def _other_chips(x, y):
    return [(1 - x, y), (x, 1 - y), (1 - x, 1 - y)]


def _shard_region(ref, shape, by_cols, chip, rows=None):
    R, C = shape
    start, size = (0, R) if rows is None else rows
    if by_cols:
        return ref.at[pl.ds(start, size), pl.ds(chip * C, C)]
    return ref.at[pl.ds(chip * R + start, size), :]


def _gather_weights(entries):
    n = len(entries)
    shapes = [e[0].shape[1:] for e in entries]

    def places(ins, outs, sems):
        send_sems, recv_sems, local_sems = sems
        x, y, c = lax.axis_index("x"), lax.axis_index("y"), lax.axis_index("c")

        def landing(f, px, py, pc):
            R = shapes[f][0]
            return _shard_region(outs[f], shapes[f], entries[f][2], 2 * px + py, rows=(pc * (R // 2), R // 2))

        def copy(f, k, block, to, src=None):
            dst = landing(f, *block)
            return pltpu.make_async_remote_copy(
                src_ref=dst if src is None else src, dst_ref=dst,
                send_sem=send_sems.at[6 * f + k], recv_sem=recv_sems.at[6 * f + k],
                device_id=to, device_id_type=MESH)

        def mine(f):
            dst = _shard_region(outs[f], shapes[f], entries[f][2], 2 * x + y)
            return pltpu.make_async_copy(ins[f].at[entries[f][1]], dst, local_sems.at[f])

        def first(f, j):
            R = shapes[f][0]
            src = ins[f].at[entries[f][1], pl.ds(c * (R // 2), R // 2), :]
            return copy(f, j, (x, y, c), (*_other_chips(x, y)[j], c), src=src)

        return x, y, c, copy, mine, first

    def start(ins, outs, sems):
        _, _, _, _, mine, first = places(ins, outs, sems)
        for f in range(n):
            mine(f).start()
        for j in range(3):
            for f in range(n):
                first(f, j).start()

    def mid(ins, outs, sems):
        x, y, c, copy, _, _ = places(ins, outs, sems)
        for j, chip in enumerate(_other_chips(x, y)):
            for f in range(n):
                copy(f, j, (*chip, c), (x, y, c)).wait_recv()
                copy(f, 3 + j, (*chip, c), (x, y, 1 - c)).start()

    def end(ins, outs, sems):
        x, y, c, copy, mine, first = places(ins, outs, sems)
        for j, chip in enumerate(_other_chips(x, y)):
            for f in range(n):
                copy(f, 3 + j, (*chip, 1 - c), (x, y, c)).wait_recv()
        for j, chip in enumerate(_other_chips(x, y)):
            for f in range(n):
                first(f, j).wait_send()
                copy(f, 3 + j, (*chip, c), (x, y, 1 - c)).wait_send()
        for f in range(n):
            mine(f).wait()

    def whole(f):
        R, C = shapes[f]
        return (R, N_CHIPS * C) if entries[f][2] else (N_CHIPS * R, C)

    return _Comm(
        [e[0] for e in entries], [jax.ShapeDtypeStruct(whole(f), BF16) for f in range(n)],
        [pltpu.SemaphoreType.DMA((6 * n,)), pltpu.SemaphoreType.DMA((6 * n,)), pltpu.SemaphoreType.DMA((n,))],
        start, end, mid)


def _scatter_grads(entries):
    n = len(entries)

    def copies(ins, outs, sems):
        send_sems, recv_sems, local_sems = sems
        x, y, c = lax.axis_index("x"), lax.axis_index("y"), lax.axis_index("c")
        me = 2 * x + y

        def piece(f, chip):
            return _shard_region(ins[f], entries[f][1], entries[f][2], chip)

        mine = [pltpu.make_async_copy(piece(f, me), outs[f].at[me], local_sems.at[f]) for f in range(n)]
        sends = [pltpu.make_async_remote_copy(
            src_ref=piece(f, 2 * px + py), dst_ref=outs[f].at[me],
            send_sem=send_sems.at[3 * f + j], recv_sem=recv_sems.at[3 * f + j],
            device_id=(px, py, c), device_id_type=MESH)
            for j, (px, py) in enumerate(_other_chips(x, y)) for f in range(n)]
        return mine, sends

    def start(ins, outs, sems):
        mine, sends = copies(ins, outs, sems)
        for cp in mine + sends:
            cp.start()

    def end(ins, outs, sems):
        mine, sends = copies(ins, outs, sems)
        for cp in sends + mine:
            cp.wait()

    return _Comm(
        [e[0] for e in entries], [jax.ShapeDtypeStruct((N_CHIPS,) + tuple(e[1]), BF16) for e in entries],
        [pltpu.SemaphoreType.DMA((3 * n,)), pltpu.SemaphoreType.DMA((3 * n,)), pltpu.SemaphoreType.DMA((n,))],
        start, end)


def _exchange_sibling(parts):
    n = len(parts)

    def copies(ins, outs, sems):
        send_sems, recv_sems = sems
        sibling = (lax.axis_index("x"), lax.axis_index("y"), 1 - lax.axis_index("c"))
        return [pltpu.make_async_remote_copy(src_ref=ins[i], dst_ref=outs[i], send_sem=send_sems.at[i],
                                             recv_sem=recv_sems.at[i], device_id=sibling, device_id_type=MESH)
                for i in range(n)]

    def start(ins, outs, sems):
        for cp in copies(ins, outs, sems):
            cp.start()

    def end(ins, outs, sems):
        for cp in copies(ins, outs, sems):
            cp.wait()

    return _Comm(parts, [jax.ShapeDtypeStruct(s.shape, s.dtype) for s in parts],
                 [pltpu.SemaphoreType.DMA((n,)), pltpu.SemaphoreType.DMA((n,))], start, end)


def _allgather_small(block):
    m_per, ncol = block.shape

    def body(x_ref, out_ref, send_sems, recv_sems, local_sem):
        x, y, c = lax.axis_index("x"), lax.axis_index("y"), lax.axis_index("c")
        me, sibling = (x, y, c), (x, y, 1 - c)
        chips = _other_chips(x, y)

        def rows(px, py, pc):
            return out_ref.at[4 * px + 2 * py + pc]

        def copy(k, block_of, to, src=None):
            return pltpu.make_async_remote_copy(
                src_ref=rows(*block_of) if src is None else src, dst_ref=rows(*block_of),
                send_sem=send_sems.at[k], recv_sem=recv_sems.at[k], device_id=to, device_id_type=MESH)

        mine = pltpu.make_async_copy(x_ref, rows(*me), local_sem)
        mine.start()
        first = [copy(0, me, sibling, src=x_ref)]
        first += [copy(1 + j, me, (*chip, c), src=x_ref) for j, chip in enumerate(chips)]
        for cp in first:
            cp.start()
        passed = [copy(4 + j, (*chip, c), sibling) for j, chip in enumerate(chips)]
        for j, chip in enumerate(chips):
            copy(1 + j, (*chip, c), me).wait_recv()
            passed[j].start()
        copy(0, sibling, me).wait_recv()
        for j, chip in enumerate(chips):
            copy(4 + j, (*chip, 1 - c), me).wait_recv()
        for cp in first + passed:
            cp.wait_send()
        mine.wait()

    return pl.pallas_call(
        body, name="allgather_small",
        in_specs=[pl.BlockSpec(memory_space=pltpu.VMEM)], out_specs=pl.BlockSpec(memory_space=pltpu.VMEM),
        out_shape=jax.ShapeDtypeStruct((N_DEV, m_per, ncol), block.dtype),
        scratch_shapes=[pltpu.SemaphoreType.DMA((7,)), pltpu.SemaphoreType.DMA((7,)), pltpu.SemaphoreType.DMA],
        compiler_params=_params(),
    )(block)


def _adamw(w, g, m, v):
    m = ADAM_B1 * m + (1.0 - ADAM_B1) * g
    v = ADAM_B2 * v + (1.0 - ADAM_B2) * jnp.square(g)
    m_hat = m / (1.0 - ADAM_B1 ** ADAM_STEP)
    v_hat = v / (1.0 - ADAM_B2 ** ADAM_STEP)
    delta = -ADAM_LR * (m_hat / (jnp.sqrt(v_hat) + ADAM_EPS) + ADAM_WD * w)
    return delta, m, v


def _flat_tile(rows):
    return min(rows, 256)


def _sum_pieces(name, layers):
    L = len(layers)
    P, R, C = layers[0].shape
    tr = _flat_tile(R)

    def body(*refs):
        out_ref = refs[L]
        for l in range(L):
            @pl.when(pl.program_id(0) == l)
            def _(p_ref=refs[l]):
                acc = p_ref[0].astype(F32)
                for j in range(1, P):
                    acc = acc + p_ref[j].astype(F32)
                out_ref[...] = acc

    return pl.pallas_call(
        body, name=name, grid=(L, R // tr),
        in_specs=[pl.BlockSpec((P, tr, C), lambda l, i: (0, i, 0)) for _ in range(L)],
        out_specs=pl.BlockSpec((None, tr, C), lambda l, i: (l, i, 0)),
        out_shape=jax.ShapeDtypeStruct((L, R, C), F32),
        compiler_params=_params(dimension_semantics=("parallel", "parallel")),
    )(*layers)


def _adam_pair(name, w, m, v, part_a, part_b):
    L, R, C = w.shape
    tr = _flat_tile(R)

    def body(w_ref, m_ref, v_ref, a_ref, b_ref, g_ref, d_ref, nm_ref, nv_ref):
        g = a_ref[...] + b_ref[...]
        g_ref[...] = g
        d_ref[...], nm_ref[...], nv_ref[...] = _adamw(w_ref[...], g, m_ref[...], v_ref[...])

    row = pl.BlockSpec((None, tr, C), lambda l, i: (l, i, 0))
    return pl.pallas_call(
        body, name=name, grid=(L, R // tr),
        in_specs=[row] * 5, out_specs=[row] * 4,
        out_shape=[jax.ShapeDtypeStruct((L, R, C), F32)] * 4,
        compiler_params=_params(dimension_semantics=("parallel", "parallel")),
    )(w, m, v, part_a, part_b)


def _adam_small(w, m, v, gathered):
    R, C = w.shape

    def body(w_ref, m_ref, v_ref, p_ref, g_ref, d_ref, nm_ref, nv_ref):
        g = p_ref[0]
        for j in range(1, N_DEV):
            g = g + p_ref[j]
        g_ref[...] = g
        d_ref[...], nm_ref[...], nv_ref[...] = _adamw(w_ref[...], g, m_ref[...], v_ref[...])

    return pl.pallas_call(
        body, name="adam_small",
        out_shape=[jax.ShapeDtypeStruct((R, C), F32)] * 4,
        compiler_params=_params(),
    )(w, m, v, gathered)


SMALL = ("mix_norm_g", "mlp_norm_g", "final_norm_g", "a_ln_g", "a_ln_b", "a_w_s", "a_b_s", "rel_bias")


def _pack_small(arrays, width):
    rows = []
    for a in arrays:
        flat = a.reshape(-1)
        pad = (-flat.shape[0]) % width
        rows.append(jnp.pad(flat, (0, pad)).reshape(-1, width))
    block = jnp.concatenate(rows, axis=0)
    return jnp.pad(block, ((0, (-block.shape[0]) % 8), (0, 0)))


def _unpack_small(block, shapes, width):
    out, row = [], 0
    for shape in shapes:
        size = int(np.prod(shape))
        nrows = -(-size // width)
        out.append(block[row:row + nrows].reshape(-1)[:size].reshape(shape))
        row += nrows
    return out


def kernel(x, mix_norm_g, mlp_norm_g, final_norm_g, a_w_in, a_ln_g, a_ln_b, a_w_s, a_b_s, a_w_out, b_w_qkv, b_w_out, rel_bias, w_up, w_down, loss_target, m_mix_norm_g, m_mlp_norm_g, m_final_norm_g, m_a_w_in, m_a_ln_g, m_a_ln_b, m_a_w_s, m_a_b_s, m_a_w_out, m_b_w_qkv, m_b_w_out, m_rel_bias, m_w_up, m_w_down, v_mix_norm_g, v_mlp_norm_g, v_final_norm_g, v_a_w_in, v_a_ln_g, v_a_ln_b, v_a_w_s, v_a_b_s, v_a_w_out, v_b_w_qkv, v_b_w_out, v_rel_bias, v_w_up, v_w_down):
    T, D = x.shape[1], x.shape[2]
    h0 = x.reshape(T, D)
    target = loss_target.reshape(T, D)
    G = a_w_s.shape[1]

    w_big = [a_w_in, a_w_out, b_w_qkv, b_w_out, w_up, w_down]
    m_big = [m_a_w_in, m_a_w_out, m_b_w_qkv, m_b_w_out, m_w_up, m_w_down]
    v_big = [v_a_w_in, v_a_w_out, v_b_w_qkv, v_b_w_out, v_w_up, v_w_down]
    by_cols = [True, False, True, True, True, False]
    s_in, s_out, s_qkv, s_bo, s_up, s_dn = [w.astype(BF16) for w in w_big]
    W_in, W_out = _run_comm("gather_a", _gather_weights([(s_in, 0, True), (s_out, 0, False)]))

    tril = jnp.tril(jnp.ones((CHUNK, CHUNK), dtype=bool))
    w_tril = jnp.where(tril[None], a_w_s[0], 0.0).astype(BF16)
    w_tril_t = jnp.swapaxes(w_tril, 1, 2)
    b_rows = jnp.broadcast_to(a_b_s[0][:, :, None], (G, CHUNK, CHUNK))
    buckets = _bucket_maps()
    bias = _bias_build(rel_bias, buckets)

    QKV = s_qkv.shape[2] * N_CHIPS
    TM = 1024
    TK_WGRAD = 2048

    def matmul(name, a, b, mode, out, tm=TM, tn=1024, **kw):
        outs = out if isinstance(out, list) else [out]
        return _mm(name, a, b, mode, tm=tm, tn=tn, tk=a.shape[1], outs=outs, **kw)

    def norm_bwd(layer_gain, h, dres, copies=2):
        return dict(epi=_epi_rms_bwd(copies), extras=(h, dres), vecs=(layer_gain,), col_sums=1)

    def wgrad(name, a, b, tn=1024, comm=None):
        return _mm(name, a, b, "tn", tm=1024, tn=tn, tk=TK_WGRAD, outs=[BF16], comm=comm)

    def scatter(*which):
        return _scatter_grads([(g, w_big[i].shape[1:], by_cols[i]) for g, i in which])

    (a_pre, y0), (W_up0,) = matmul("a_in", h0, W_in, "nn", BF16, norm_gain=mix_norm_g[0:1],
                                   comm=_gather_weights([(s_up, 0, True)]))
    z = _gate_fwd(a_pre, a_ln_g, a_ln_b, w_tril, b_rows)
    h1 = matmul("a_out", z, W_out, "nn", F32, epi=_epi_residual, extras=(h0,))
    (q1, y1), (W_dn0,) = matmul("mlp_up0", h1, W_up0, "nn", BF16, epi=_epi_relu2, norm_gain=mlp_norm_g[0:1],
                                comm=_gather_weights([(s_dn, 0, False)]))
    h2, (W_qkv, W_bo) = matmul("mlp_down0", q1, W_dn0, "nn", F32, tm=TM // 2, epi=_epi_residual, extras=(h1,),
                               comm=_gather_weights([(s_qkv, 0, True), (s_bo, 0, True)]))
    (qkv, y2), (W_up1, W_dn1) = matmul("b_qkv", h2, W_qkv, "nn", BF16, tn=QKV // 4, norm_gain=mix_norm_g[1:2],
                                       comm=_gather_weights([(s_up, 1, True), (s_dn, 1, False)]))
    o, lse = _attention_fwd(qkv, bias)
    h3 = matmul("b_out", o, W_bo, "nn", F32, epi=_epi_residual, extras=(h2,))
    q3, y3 = matmul("mlp_up1", h3, W_up1, "nn", BF16, epi=_epi_relu2, norm_gain=mlp_norm_g[1:2])
    h4 = matmul("mlp_down1", q3, W_dn1, "nn", F32, tm=TM // 2, epi=_epi_residual, extras=(h3,))
    dh4, dh4_b, d_final_g, loss_tile = _loss_head(h4, final_norm_g.reshape(1, D), target)
    loss = lax.psum(loss_tile[0, 0], ("x", "y", "c"))

    dp3 = matmul("mlp_down_bwd1", dh4_b, W_dn1, "nt", BF16, epi=_epi_relu2_grad, extras=(q3,))
    g_dn1 = wgrad("mlp_down_wgrad1", q3, dh4_b)
    g_up1, (r_dn1,) = wgrad("mlp_up_wgrad1", y3, dp3, comm=scatter((g_dn1, 5)))
    (dh3, dh3_b, dg_mlp1), (r_up1,) = matmul("mlp_up_bwd1", dp3, W_up1, "nt", [F32, BF16], tm=TM // 2,
                                             comm=scatter((g_up1, 4)), **norm_bwd(mlp_norm_g[1:2], h3, dh4))
    do = matmul("b_out_bwd", dh3_b, W_bo, "nt", BF16)
    g_bo = wgrad("b_out_wgrad", o, dh3_b)
    dqkv, dbias = _attention_bwd(qkv, do, o, lse, bias)
    d_rel_bias = _bias_scatter(dbias, buckets)
    (dh2, dh2_b, dg_mix1), (r_bo,) = matmul("b_qkv_bwd", dqkv, W_qkv, "nt", [F32, BF16], tm=TM // 2,
                                            comm=scatter((g_bo, 3)), **norm_bwd(mix_norm_g[1:2], h2, dh3))
    g_qkv = wgrad("b_qkv_wgrad", y2, dqkv, tn=QKV // 4)
    dp1, (r_qkv,) = matmul("mlp_down_bwd0", dh2_b, W_dn0, "nt", BF16, epi=_epi_relu2_grad, extras=(q1,),
                           comm=scatter((g_qkv, 2)))
    g_up0 = wgrad("mlp_up_wgrad0", y1, dp1)
    g_dn0, (r_up0,) = wgrad("mlp_down_wgrad0", q1, dh2_b, comm=scatter((g_up0, 4)))
    (dh1, dh1_b, dg_mlp0), (r_dn0,) = matmul("mlp_up_bwd0", dp1, W_up0, "nt", [F32, BF16], tm=TM // 2,
                                             comm=scatter((g_dn0, 5)), **norm_bwd(mlp_norm_g[0:1], h1, dh2))
    dz = matmul("a_out_bwd", dh1_b, W_out, "nt", F32)
    g_out = wgrad("a_out_wgrad", z, dh1_b)
    da, d_ln_g, d_ln_b, d_w_s, d_b_s = _gate_bwd(a_pre, dz, a_ln_g, a_ln_b, w_tril, w_tril_t, b_rows)
    g_in, (r_out,) = wgrad("a_in_wgrad", y0, da, comm=scatter((g_out, 1)))
    (grad_x, dg_mix0), (r_in,) = matmul("a_in_bwd", da, W_in, "nt", F32, comm=scatter((g_in, 0)),
                                        **norm_bwd(mix_norm_g[0:1], h0, dh1, copies=1))

    received = [[r_in], [r_out], [r_qkv], [r_bo], [r_up0, r_up1], [r_dn0, r_dn1]]
    plane = [_sum_pieces(f"sum_pieces{i}", r) for i, r in enumerate(received)]
    other = _run_comm("exchange_sibling", _exchange_sibling(plane))
    big_out = [_adam_pair(f"adam{i}", w_big[i], m_big[i], v_big[i], plane[i], other[i]) for i in range(len(w_big))]

    def unbig(kind):
        return dict(zip(["a_w_in", "a_w_out", "b_w_qkv", "b_w_out", "w_up", "w_down"], [b[kind] for b in big_out]))

    small_w = [mix_norm_g, mlp_norm_g, final_norm_g, a_ln_g, a_ln_b, a_w_s, a_b_s, rel_bias]
    small_m = [m_mix_norm_g, m_mlp_norm_g, m_final_norm_g, m_a_ln_g, m_a_ln_b, m_a_w_s, m_a_b_s, m_rel_bias]
    small_v = [v_mix_norm_g, v_mlp_norm_g, v_final_norm_g, v_a_ln_g, v_a_ln_b, v_a_w_s, v_a_b_s, v_rel_bias]
    small_g = [jnp.concatenate([dg_mix0, dg_mix1]), jnp.concatenate([dg_mlp0, dg_mlp1]), d_final_g,
               d_ln_g, d_ln_b, d_w_s[None], d_b_s[None, :, :, 0], d_rel_bias]
    width = max(D, 128)
    gathered_small = _allgather_small(_pack_small(small_g, width))
    small_out = _adam_small(_pack_small(small_w, width), _pack_small(small_m, width), _pack_small(small_v, width),
                            gathered_small)
    shapes = [w.shape for w in small_w]

    names = ["mix_norm_g", "mlp_norm_g", "final_norm_g", "a_w_in", "a_ln_g", "a_ln_b", "a_w_s", "a_b_s", "a_w_out",
             "b_w_qkv", "b_w_out", "rel_bias", "w_up", "w_down"]
    results = [loss, grad_x.reshape(x.shape)]
    for kind in range(4):
        table = dict(zip(SMALL, _unpack_small(small_out[kind], shapes, width)))
        table.update(unbig(kind))
        results += [table[n] for n in names]
    return tuple(results)
```

```python
import functools
import math

import numpy as np
import jax
import jax.numpy as jnp
from jax import lax
from jax.experimental import pallas as pl
from jax.experimental.pallas import tpu as pltpu

F32 = jnp.float32
BF16 = jnp.bfloat16
MESH = pl.DeviceIdType.MESH
ANY = pl.BlockSpec(memory_space=pl.ANY)

N_CHIPS = 4
N_DEV = 8
VMEM_LIMIT_BYTES = 56 * 1024 * 1024

EPS = 1e-6
NEG_INF = -1e30
CHUNK = 128
GROUP_DIM = 128
HEAD_DIM = 64
ATT_HEADS = 8
ATT_WIDTH = ATT_HEADS * HEAD_DIM
PAIR = 2 * HEAD_DIM
BLK = 128
DILATIONS = (1, 4, 16)
N_BUCKETS = 32
MAX_EXACT = N_BUCKETS // 2
REL_MAX_DISTANCE = 2048

ADAM_LR = 0.001
ADAM_B1 = 0.9
ADAM_B2 = 0.999
ADAM_EPS = 1e-08
ADAM_WD = 0.01
ADAM_STEP = 10

NN = (((1,), (0,)), ((), ()))
NT = (((1,), (1,)), ((), ()))
TN = (((0,), (0,)), ((), ()))


def _params(**kw):
    return pltpu.CompilerParams(vmem_limit_bytes=VMEM_LIMIT_BYTES, **kw)


def _dot(a, b, dims=NN):
    return lax.dot_general(a, b, dims, preferred_element_type=F32)


def _gelu(x):
    return 0.5 * x * (1.0 + lax.erf(x * math.sqrt(0.5)))


def _gelu_grad(x):
    return 0.5 * (1.0 + lax.erf(x * math.sqrt(0.5))) + x * jnp.exp(-0.5 * x * x) * (1.0 / math.sqrt(2.0 * math.pi))


def _mean(x):
    return jnp.mean(x, axis=-1, keepdims=True)


class _Comm:
    def __init__(self, inputs, out_shapes, scratch, start, end, mid=None):
        self.inputs, self.out_shapes, self.scratch = list(inputs), list(out_shapes), list(scratch)
        self.start, self.mid, self.end = start, mid, end


def _run_comm(name, comm):
    n_in, n_out = len(comm.inputs), len(comm.out_shapes)

    def body(*refs):
        parts = refs[:n_in], refs[n_in:n_in + n_out], refs[n_in + n_out:]
        comm.start(*parts)
        if comm.mid is not None:
            comm.mid(*parts)
        comm.end(*parts)

    return pl.pallas_call(
        body, name=name, in_specs=[ANY] * n_in, out_specs=[ANY] * n_out, out_shape=comm.out_shapes,
        scratch_shapes=comm.scratch, compiler_params=_params(),
    )(*comm.inputs)


def _mm(name, a, b, mode, *, tm, tn, tk, outs, epi=None, extras=(), vecs=(), col_sums=0, norm_gain=None, comm=None):
    if mode == "tn":
        K, M = a.shape
    else:
        M, K = a.shape
    N = b.shape[0] if mode == "nt" else b.shape[1]
    tm, tn, tk = min(tm, M), min(tn, N), min(tk, K)
    assert M % tm == 0 and N % tn == 0 and K % tk == 0, (name, M, N, K, tm, tn, tk)
    nk = K // tk
    grid = (M // tm, N // tn, nk)

    if mode == "tn":
        a_spec = pl.BlockSpec((tk, tm), lambda i, j, k: (k, i))
    else:
        a_spec = pl.BlockSpec((tm, tk), lambda i, j, k: (i, k))
    if mode == "nt":
        b_spec = pl.BlockSpec((tn, tk), lambda i, j, k: (j, k))
    else:
        b_spec = pl.BlockSpec((tk, tn), lambda i, j, k: (k, j))
    tile = pl.BlockSpec((tm, tn), lambda i, j, k: (i, j))
    vec = pl.BlockSpec((1, tn), lambda i, j, k: (0, j))
    normed = norm_gain is not None
    assert not normed or (mode == "nn" and nk == 1)
    assert col_sums == 0 or grid[1] == 1
    out_shapes = [jax.ShapeDtypeStruct((M, N), dtype) for dtype in outs]
    out_specs = [tile for _ in outs]
    if normed:
        out_shapes.append(jax.ShapeDtypeStruct((M, K), BF16))
        out_specs.append(pl.BlockSpec((tm, K), lambda i, j, k: (i, 0)))
    out_shapes += [jax.ShapeDtypeStruct((1, N), F32)] * col_sums
    out_specs += [vec] * col_sums
    extra_specs = [tile for _ in extras] + [vec for _ in vecs]
    if normed:
        extra_specs.append(pl.BlockSpec((1, K), lambda i, j, k: (0, 0)))
    n_extra, n_out = len(extra_specs), len(out_shapes)
    n_tiles = len(outs)
    n_cin = len(comm.inputs) if comm else 0
    n_cout = len(comm.out_shapes) if comm else 0
    dims = {"nn": NN, "nt": NT, "tn": TN}[mode]
    steps = grid[0] * grid[1] * grid[2]

    def body(*refs):
        a_ref, b_ref = refs[0], refs[1]
        pos = 2
        extra_refs = refs[pos:pos + n_extra]
        pos += n_extra
        comm_in = refs[pos:pos + n_cin]
        pos += n_cin
        out_refs = refs[pos:pos + n_out]
        pos += n_out
        comm_out = refs[pos:pos + n_cout]
        pos += n_cout
        acc_ref = refs[pos] if nk > 1 else None
        pos += nk > 1
        y_ref = refs[pos] if normed else None
        comm_sems = refs[pos + normed:]
        k = pl.program_id(2)
        step = (pl.program_id(0) * grid[1] + pl.program_id(1)) * nk + k

        if comm is not None:
            @pl.when(step == 0)
            def _():
                comm.start(comm_in, comm_out, comm_sems)

        if normed:
            @pl.when(pl.program_id(1) == 0)
            def _():
                hv = a_ref[...]
                y = (hv * lax.rsqrt(_mean(hv * hv) + EPS) * extra_refs[-1][...]).astype(BF16)
                y_ref[...] = y
                out_refs[n_tiles][...] = y

            lhs = y_ref[...]
        else:
            lhs = a_ref[...].astype(BF16)
        part = _dot(lhs, b_ref[...].astype(BF16), dims)

        def finish(acc):
            epi_args = [e[...] for e in extra_refs[:n_extra - normed]]
            res = epi(acc, *epi_args) if epi is not None else (acc,) * n_tiles
            for o, r in zip(out_refs[:n_tiles], res[:n_tiles]):
                o[...] = r.astype(o.dtype)
            if col_sums:
                sums = out_refs[n_out - col_sums:]

                @pl.when(pl.program_id(0) == 0)
                def _():
                    for o in sums:
                        o[...] = jnp.zeros_like(o)

                for o, r in zip(sums, res[n_tiles:]):
                    o[...] += r

        if nk == 1:
            finish(part)
        else:
            @pl.when(k == 0)
            def _():
                acc_ref[...] = part

            @pl.when(k > 0)
            def _():
                acc_ref[...] += part

            @pl.when(k == nk - 1)
            def _():
                finish(acc_ref[...])

        if comm is not None:
            if comm.mid is not None:
                @pl.when(step == (3 * steps) // 4)
                def _():
                    comm.mid(comm_in, comm_out, comm_sems)

            @pl.when(step == steps - 1)
            def _():
                comm.end(comm_in, comm_out, comm_sems)

    sequential = comm is not None or normed or col_sums > 0
    order = ("arbitrary",) * 3 if sequential else ("parallel", "parallel", "arbitrary")
    scratch = [pltpu.VMEM((tm, tn), F32)] if nk > 1 else []
    if normed:
        scratch.append(pltpu.VMEM((tm, K), BF16))
    res = pl.pallas_call(
        body, name=name, grid=grid,
        in_specs=[a_spec, b_spec] + extra_specs + [ANY] * n_cin,
        out_specs=out_specs + [ANY] * n_cout,
        out_shape=out_shapes + (comm.out_shapes if comm else []),
        scratch_shapes=scratch + (comm.scratch if comm else []),
        compiler_params=_params(dimension_semantics=order),
    )(a, b, *extras, *vecs, *([norm_gain] if normed else []), *(comm.inputs if comm else []))
    mm_out = res[0] if n_out == 1 else list(res[:n_out])
    return (mm_out, list(res[n_out:])) if comm else mm_out


def _epi_residual(acc, res):
    return (res + acc,)


def _epi_relu2(acc):
    return (jnp.square(jnp.maximum(acc, 0.0)),)


def _epi_rms_bwd(copies):
    def epi(acc, h, dres, g):
        r = lax.rsqrt(_mean(h * h) + EPS)
        hn = h * r
        dyg = acc * g
        dh = dres + r * (dyg - hn * _mean(dyg * hn))
        return (dh,) * copies + (jnp.sum(acc * hn, axis=0, keepdims=True),)
    return epi


def _epi_loss_head(acc, res, target, g):
    h = res + acc
    r = lax.rsqrt(_mean(h * h) + EPS)
    hn = h * r
    diff = hn * g - target
    loss = 0.5 * jnp.sum(_mean(diff * diff))
    dy = diff * (1.0 / h.shape[-1])
    dyg = dy * g
    dh = r * (dyg - hn * _mean(dyg * hn))
    return dh, dh, jnp.sum(dy * hn, axis=0, keepdims=True), jnp.full((1, h.shape[-1]), loss, F32)


def _epi_relu2_grad(acc, q):
    qf = q.astype(F32)
    return (acc * jnp.where(qf > 0.0, (2.0 * qf) * lax.rsqrt(qf), 0.0),)


def _row_tile(T):
    return min(T, 512)


def _gate_tile(T):
    return min(T, 256)


def _gate_fwd(a, ln_g, ln_b, w_tril, b_rows):
    T, W2 = a.shape
    W = W2 // 2
    G = W // GROUP_DIM
    tr = _gate_tile(T)

    def body(a_ref, lng_ref, lnb_ref, w_ref, b_ref, z_ref):
        u = _gelu(a_ref[:, :W].astype(F32))
        vg = _gelu(a_ref[:, W:].astype(F32))
        xc = vg - _mean(vg)
        vn = xc * lax.rsqrt(_mean(xc * xc) + EPS)
        vl = (vn * lng_ref[...] + lnb_ref[...]).astype(BF16)
        for n in range(tr // CHUNK):
            rows = slice(n * CHUNK, (n + 1) * CHUNK)
            for g in range(G):
                cols = slice(g * GROUP_DIM, (g + 1) * GROUP_DIM)
                gate = _dot(w_ref[g], vl[rows, cols]) + b_ref[g]
                z_ref[rows, cols] = (u[rows, cols] * gate).astype(BF16)

    vec = pl.BlockSpec((1, W), lambda i: (0, 0))
    grp = pl.BlockSpec((G, CHUNK, CHUNK), lambda i: (0, 0, 0))
    return pl.pallas_call(
        body, name="gate_fwd", grid=(T // tr,),
        in_specs=[pl.BlockSpec((tr, W2), lambda i: (i, 0)), vec, vec, grp, grp],
        out_specs=pl.BlockSpec((tr, W), lambda i: (i, 0)),
        out_shape=jax.ShapeDtypeStruct((T, W), BF16),
        compiler_params=_params(dimension_semantics=("parallel",)),
    )(a, ln_g, ln_b, w_tril, b_rows)


def _gate_bwd(a, dz, ln_g, ln_b, w_tril, w_tril_t, b_rows):
    T, W2 = a.shape
    W = W2 // 2
    G = W // GROUP_DIM
    tr = _gate_tile(T)
    steps = T // tr

    def body(a_ref, dz_ref, lng_ref, lnb_ref, w_ref, wt_ref, b_ref, da_ref, dlng_ref, dlnb_ref, dw_ref, dbs_ref, dvl_ref):
        step = pl.program_id(0)

        @pl.when(step == 0)
        def _():
            dlng_ref[...] = jnp.zeros_like(dlng_ref)
            dlnb_ref[...] = jnp.zeros_like(dlnb_ref)
            dw_ref[...] = jnp.zeros_like(dw_ref)
            dbs_ref[...] = jnp.zeros_like(dbs_ref)

        au = a_ref[:, :W].astype(F32)
        av = a_ref[:, W:].astype(F32)
        u = _gelu(au)
        vg = _gelu(av)
        xc = vg - _mean(vg)
        rstd = lax.rsqrt(_mean(xc * xc) + EPS)
        vn = xc * rstd
        lng = lng_ref[...]
        vl = (vn * lng + lnb_ref[...]).astype(BF16)
        du_scale = dz_ref[...] * _gelu_grad(au)
        dgate_all = dz_ref[...] * u
        for n in range(tr // CHUNK):
            rows = slice(n * CHUNK, (n + 1) * CHUNK)
            for g in range(G):
                cols = slice(g * GROUP_DIM, (g + 1) * GROUP_DIM)
                vlg = vl[rows, cols]
                gate = _dot(w_ref[g], vlg) + b_ref[g]
                da_ref[rows, cols] = (du_scale[rows, cols] * gate).astype(BF16)
                dgate = dgate_all[rows, cols]
                dbs_ref[g] += dgate
                dgate_b = dgate.astype(BF16)
                dw_ref[g] += _dot(dgate_b, vlg, NT)
                dvl_ref[rows, cols] = _dot(wt_ref[g], dgate_b)
        dvl = dvl_ref[...]
        dlnb_ref[...] += jnp.sum(dvl, axis=0, keepdims=True)
        dlng_ref[...] += jnp.sum(dvl * vn, axis=0, keepdims=True)
        dvn = dvl * lng
        dvg = rstd * (dvn - _mean(dvn) - vn * _mean(dvn * vn))
        da_ref[:, W:] = (dvg * _gelu_grad(av)).astype(BF16)

        @pl.when(step == steps - 1)
        def _():
            t_idx = lax.broadcasted_iota(jnp.int32, (CHUNK, CHUNK), 0)
            s_idx = lax.broadcasted_iota(jnp.int32, (CHUNK, CHUNK), 1)
            for g in range(G):
                dw_ref[g] = jnp.where(s_idx <= t_idx, dw_ref[g], 0.0)
                dbs_ref[g] = jnp.broadcast_to(jnp.sum(dbs_ref[g], axis=-1, keepdims=True), (CHUNK, CHUNK))

    vec = pl.BlockSpec((1, W), lambda i: (0, 0))
    grp = pl.BlockSpec((G, CHUNK, CHUNK), lambda i: (0, 0, 0))
    return pl.pallas_call(
        body, name="gate_bwd", grid=(steps,),
        in_specs=[pl.BlockSpec((tr, W2), lambda i: (i, 0)), pl.BlockSpec((tr, W), lambda i: (i, 0)),
                  vec, vec, grp, grp, grp],
        out_specs=[pl.BlockSpec((tr, W2), lambda i: (i, 0)), vec, vec, grp, grp],
        out_shape=[jax.ShapeDtypeStruct((T, W2), BF16), jax.ShapeDtypeStruct((1, W), F32),
                   jax.ShapeDtypeStruct((1, W), F32), jax.ShapeDtypeStruct((G, CHUNK, CHUNK), F32),
                   jax.ShapeDtypeStruct((G, CHUNK, CHUNK), F32)],
        scratch_shapes=[pltpu.VMEM((tr, W), F32)],
        compiler_params=_params(dimension_semantics=("arbitrary",)),
    )(a, dz, ln_g, ln_b, w_tril, w_tril_t, b_rows)


def _bucket_map(dilation):
    rel = BLK + np.arange(BLK)[:, None] - np.arange(2 * BLK)[None, :]
    dist = np.clip(rel, 0, BLK) * dilation
    nf = np.maximum(dist, 1).astype(np.float32)
    large = MAX_EXACT + (np.log(nf / np.float32(MAX_EXACT)) / np.float32(math.log(REL_MAX_DISTANCE / MAX_EXACT))
                         * np.float32(N_BUCKETS - MAX_EXACT)).astype(np.int32)
    large = np.minimum(large, N_BUCKETS - 1)
    return np.where(dist < MAX_EXACT, dist, large).astype(np.int32)


def _bucket_maps():
    return jnp.asarray(np.stack([_bucket_map(d) for d in DILATIONS]))


def _bias_build(rel_bias, buckets):
    NG = len(DILATIONS)

    def body(table_ref, bucket_ref, out_ref):
        for g in range(NG):
            bk = bucket_ref[g]
            for h in range(ATT_HEADS):
                out_ref[0, g, h] = jnp.zeros((BLK, 2 * BLK), F32)
            for b in range(N_BUCKETS):
                hit = bk == b
                for h in range(ATT_HEADS):
                    out_ref[0, g, h] = jnp.where(hit, table_ref[b, g * ATT_HEADS + h], out_ref[0, g, h])
            for h in range(ATT_HEADS):
                for first in range(2):
                    out_ref[first, g, h] = jnp.where(_window_mask(first), out_ref[0, g, h], NEG_INF)

    return pl.pallas_call(
        body, name="bias_build",
        in_specs=[pl.BlockSpec(memory_space=pltpu.SMEM), pl.BlockSpec(memory_space=pltpu.VMEM)],
        out_specs=pl.BlockSpec(memory_space=pltpu.VMEM),
        out_shape=jax.ShapeDtypeStruct((2, NG, ATT_HEADS, BLK, 2 * BLK), F32),
        compiler_params=_params(),
    )(rel_bias, buckets)


def _bias_scatter(dbias, buckets):
    NG = len(DILATIONS)

    def body(dbias_ref, bucket_ref, out_ref):
        for g in range(NG):
            bk = bucket_ref[g]
            for b in range(N_BUCKETS):
                hit = bk == b
                for h in range(ATT_HEADS):
                    out_ref[b, g * ATT_HEADS + h] = jnp.sum(jnp.where(hit, dbias_ref[g, h], 0.0))

    return pl.pallas_call(
        body, name="bias_scatter",
        in_specs=[pl.BlockSpec(memory_space=pltpu.VMEM), pl.BlockSpec(memory_space=pltpu.VMEM)],
        out_specs=pl.BlockSpec(memory_space=pltpu.SMEM),
        out_shape=jax.ShapeDtypeStruct((N_BUCKETS, NG * ATT_HEADS), F32),
        compiler_params=_params(),
    )(dbias, buckets)


def _window_mask(first):
    qi = lax.broadcasted_iota(jnp.int32, (BLK, 2 * BLK), 0)
    kj = lax.broadcasted_iota(jnp.int32, (BLK, 2 * BLK), 1)
    rel = BLK + qi - kj
    return (rel >= 0) & (rel <= BLK) & (kj >= BLK * first)


def _head_lanes(hh):
    lane = lax.broadcasted_iota(jnp.int32, (1, PAIR), 1)
    return (lane >= hh * HEAD_DIM) & (lane < (hh + 1) * HEAD_DIM)


def _attn_fwd(name, g, qkv, qc, kc, vc, bias, stride):
    T = qkv.shape[0]
    nb = T // BLK
    scale = HEAD_DIM ** -0.5

    def body(q_ref, kp_ref, kc_ref, vp_ref, vc_ref, bias_ref, out_ref):
        b = pl.program_id(0)
        first = (b < stride).astype(jnp.int32)
        low = _head_lanes(0)
        for hp in range(ATT_HEADS // 2):
            cols = slice(hp * PAIR, (hp + 1) * PAIR)
            qp = q_ref[:, cols]
            kk = jnp.concatenate([kp_ref[:, cols], kc_ref[:, cols]], axis=0)
            vv = jnp.concatenate([vp_ref[:, cols], vc_ref[:, cols]], axis=0)
            o_h, lse_h = [], []
            for hh in range(2):
                qm = jnp.where(_head_lanes(hh), qp, jnp.zeros_like(qp))
                s = _dot(qm, kk, NT) * scale
                logits = s + bias_ref[first, 2 * hp + hh]
                m = jnp.max(logits, axis=-1, keepdims=True)
                p = jnp.exp(logits - m)
                den = jnp.sum(p, axis=-1, keepdims=True)
                o_h.append(_dot(p.astype(BF16), vv) / den)
                lse_h.append(m + jnp.log(den))
            out_ref[:, cols] = jnp.where(low, o_h[0], o_h[1])
            out_ref[:, slice(ATT_WIDTH + hp * PAIR, ATT_WIDTH + (hp + 1) * PAIR)] = jnp.where(low, lse_h[0], lse_h[1])

    def cur(c):
        return pl.BlockSpec((BLK, ATT_WIDTH), lambda b: (b, c))

    def prev(c):
        return pl.BlockSpec((BLK, ATT_WIDTH), lambda b: (jnp.maximum(b - stride, 0), c))

    return pl.pallas_call(
        body, name=name, grid=(nb,),
        in_specs=[cur(qc), prev(kc), cur(kc), prev(vc), cur(vc),
                  pl.BlockSpec((2, None, ATT_HEADS, BLK, 2 * BLK), lambda b: (0, g, 0, 0, 0))],
        out_specs=pl.BlockSpec((BLK, 2 * ATT_WIDTH), lambda b: (b, 0)),
        out_shape=jax.ShapeDtypeStruct((T, 2 * ATT_WIDTH), F32),
        compiler_params=_params(dimension_semantics=("parallel",)),
    )(qkv, qkv, qkv, qkv, qkv, bias)


def _attn_merge(parts):
    T = parts[0].shape[0]
    tr = _row_tile(T)
    n = len(parts)

    def body(*refs):
        o_refs, l_refs = refs[:n], refs[n:2 * n]
        o_ref, lse_ref = refs[2 * n], refs[2 * n + 1]
        ls = [r[...] for r in l_refs]
        m = functools.reduce(jnp.maximum, ls)
        es = [jnp.exp(l - m) for l in ls]
        tot = functools.reduce(lambda x, y: x + y, es)
        acc = functools.reduce(lambda x, y: x + y, [e * r[...] for e, r in zip(es, o_refs)])
        o_ref[...] = (acc / tot).astype(BF16)
        lse_ref[...] = m + jnp.log(tot)

    row = pl.BlockSpec((tr, ATT_WIDTH), lambda i: (i, 0))
    row_lse = pl.BlockSpec((tr, ATT_WIDTH), lambda i: (i, 1))
    return pl.pallas_call(
        body, name="attn_merge", grid=(T // tr,),
        in_specs=[row] * n + [row_lse] * n, out_specs=[row, row],
        out_shape=[jax.ShapeDtypeStruct((T, ATT_WIDTH), BF16), jax.ShapeDtypeStruct((T, ATT_WIDTH), F32)],
        compiler_params=_params(dimension_semantics=("parallel",)),
    )(*parts, *parts)


def _attn_bwd(name, g, qkv, qc, kc, vc, do, o, lse, bias, stride):
    T = qkv.shape[0]
    nb = T // BLK
    scale = HEAD_DIM ** -0.5

    def body(q_ref, kp_ref, kc_ref, vp_ref, vc_ref, do_ref, o_ref, lse_ref, bias_ref,
             dq_ref, dkv_ref, db_ref, carry_k, carry_v):
        b = pl.program_id(0)
        ck_ref = carry_k.at[b % stride]
        cv_ref = carry_v.at[b % stride]

        @pl.when(b == 0)
        def _():
            db_ref[...] = jnp.zeros_like(db_ref)
            carry_k[...] = jnp.zeros_like(carry_k)
            carry_v[...] = jnp.zeros_like(carry_v)

        @pl.when(b >= nb)
        def _():
            dkv_ref[:, :ATT_WIDTH] = ck_ref[...].astype(BF16)
            dkv_ref[:, ATT_WIDTH:] = cv_ref[...].astype(BF16)

        @pl.when(b < nb)
        def _():
            first = (b < stride).astype(jnp.int32)
            for hp in range(ATT_HEADS // 2):
                cols = slice(hp * PAIR, (hp + 1) * PAIR)
                qp = q_ref[:, cols]
                kk = jnp.concatenate([kp_ref[:, cols], kc_ref[:, cols]], axis=0)
                vv = jnp.concatenate([vp_ref[:, cols], vc_ref[:, cols]], axis=0)
                dop = do_ref[:, cols]
                lsep = lse_ref[:, cols]
                prod = dop.astype(F32) * o_ref[:, cols].astype(F32)
                dq = jnp.zeros((BLK, PAIR), F32)
                dk = jnp.zeros((2 * BLK, PAIR), F32)
                dv = jnp.zeros((2 * BLK, PAIR), F32)
                for hh in range(2):
                    lanes = _head_lanes(hh)
                    qm = jnp.where(lanes, qp, jnp.zeros_like(qp))
                    dom = jnp.where(lanes, dop, jnp.zeros_like(dop))
                    km = jnp.where(lanes, kk, jnp.zeros_like(kk))
                    delta = jnp.sum(jnp.where(lanes, prod, 0.0), axis=-1, keepdims=True)
                    lse_h = jnp.max(jnp.where(lanes, lsep, NEG_INF), axis=-1, keepdims=True)
                    s = _dot(qm, kk, NT) * scale
                    logits = s + bias_ref[first, 2 * hp + hh]
                    p = jnp.exp(logits - lse_h)
                    dv += _dot(p.astype(BF16), dom, TN)
                    ds = p * (_dot(dom, vv, NT) - delta)
                    db_ref[2 * hp + hh] += ds
                    dss = (ds * scale).astype(BF16)
                    dq += _dot(dss, km)
                    dk += _dot(dss, qm, TN)
                dq_ref[:, cols] = dq.astype(BF16)
                dkv_ref[:, cols] = (ck_ref[:, cols] + dk[:BLK]).astype(BF16)
                dkv_ref[:, slice(ATT_WIDTH + hp * PAIR, ATT_WIDTH + (hp + 1) * PAIR)] = (
                    cv_ref[:, cols] + dv[:BLK]).astype(BF16)
                ck_ref[:, cols] = dk[BLK:]
                cv_ref[:, cols] = dv[BLK:]

    last = nb - 1

    def cur(c):
        return pl.BlockSpec((BLK, ATT_WIDTH), lambda b: (jnp.minimum(b, last), c))

    def prev(c):
        return pl.BlockSpec((BLK, ATT_WIDTH), lambda b: (jnp.clip(b - stride, 0, last), c))

    dbias_shape = (ATT_HEADS, BLK, 2 * BLK)
    return pl.pallas_call(
        body, name=name, grid=(nb + stride,),
        in_specs=[cur(qc), prev(kc), cur(kc), prev(vc), cur(vc), cur(0), cur(0), cur(0),
                  pl.BlockSpec((2, None, ATT_HEADS, BLK, 2 * BLK), lambda b: (0, g, 0, 0, 0))],
        out_specs=[cur(0), pl.BlockSpec((BLK, 2 * ATT_WIDTH), lambda b: (jnp.clip(b - stride, 0, last), 0)),
                   pl.BlockSpec(dbias_shape, lambda b: (0, 0, 0))],
        out_shape=[jax.ShapeDtypeStruct((T, ATT_WIDTH), BF16), jax.ShapeDtypeStruct((T, 2 * ATT_WIDTH), BF16),
                   jax.ShapeDtypeStruct(dbias_shape, F32)],
        scratch_shapes=[pltpu.VMEM((stride, BLK, ATT_WIDTH), F32), pltpu.VMEM((stride, BLK, ATT_WIDTH), F32)],
        compiler_params=_params(dimension_semantics=("arbitrary",)),
    )(qkv, qkv, qkv, qkv, qkv, do, o, lse, bias)


REORDER_TILE = 256
REORDER_ROWS = 2048


def _reorder_matrix(d, inverse):
    per = REORDER_TILE // d
    p = np.zeros((REORDER_TILE, REORDER_TILE), np.float32)
    for src in range(REORDER_TILE):
        i, r = divmod(src, d)
        p[r * per + i, src] = 1.0
    return jnp.asarray(p.T if inverse else p, dtype=BF16)


def _reorder_rows(name, src, d, inverse, *, src_col=0, col_stride=1, ncols=1, dst=None, dst_col=0, dst_stride=1,
                  dst_blocks=None):
    T = src.shape[0]
    dtype = src.dtype
    span = BLK * d
    rows = max(span, min(T, REORDER_ROWS))
    per = REORDER_TILE // d
    tiles = span // REORDER_TILE
    dst_blocks = ncols if dst_blocks is None else dst_blocks

    def apply(p, x):
        if dtype == BF16:
            return _dot(p, x).astype(BF16)
        hi = x.astype(BF16)
        rest = x - hi.astype(F32)
        mid = rest.astype(BF16)
        low = (rest - mid.astype(F32)).astype(BF16)
        return _dot(p, hi) + _dot(p, mid) + _dot(p, low)

    def body(*refs):
        p_ref, x_ref, o_ref = refs[0], refs[1], refs[-1]
        if d == 1:
            o_ref[...] = x_ref[...]
            return
        for s in range(rows // span):
            for t in range(tiles):
                base = s * span
                tile_rows = slice(base + t * REORDER_TILE, base + (t + 1) * REORDER_TILE)
                chunk = lambda r: slice(base + r * BLK + t * per, base + r * BLK + (t + 1) * per)
                if inverse:
                    gathered = jnp.concatenate([x_ref[chunk(r), :] for r in range(d)], axis=0)
                    o_ref[tile_rows, :] = apply(p_ref[...], gathered)
                else:
                    y = apply(p_ref[...], x_ref[tile_rows, :])
                    for r in range(d):
                        o_ref[chunk(r), :] = y[r * per:(r + 1) * per]

    in_specs = [pl.BlockSpec((REORDER_TILE, REORDER_TILE), lambda w, k: (0, 0)),
                pl.BlockSpec((rows, ATT_WIDTH), lambda w, k: (w, src_col + col_stride * k))]
    operands = [_reorder_matrix(max(d, 2), inverse), src]
    aliases = {}
    if dst is not None:
        in_specs.append(ANY)
        operands.append(dst)
        aliases = {2: 0}
    return pl.pallas_call(
        body, name=name, grid=(T // rows, ncols), in_specs=in_specs,
        out_specs=pl.BlockSpec((rows, ATT_WIDTH), lambda w, k: (w, dst_col + dst_stride * k)),
        out_shape=jax.ShapeDtypeStruct((T, dst_blocks * ATT_WIDTH), dtype),
        input_output_aliases=aliases,
        compiler_params=_params(dimension_semantics=("parallel", "parallel")),
    )(*operands)


def _group_qkv(qkv, g, d):
    NG = len(DILATIONS)
    if d == 1:
        return qkv, (g, NG + g, 2 * NG + g)
    return _reorder_rows(f"qkv_to_residues{g}", qkv, d, False, src_col=g, col_stride=NG, ncols=3), (0, 1, 2)


def _attention_fwd(qkv, bias):
    T = qkv.shape[0]
    parts = []
    for g, d in enumerate(DILATIONS):
        src, (qc, kc, vc) = _group_qkv(qkv, g, d)
        part = _attn_fwd(f"attn_fwd_{g}", g, src, qc, kc, vc, bias, d)
        parts.append(part if d == 1 else _reorder_rows(f"out_to_positions{g}", part, d, True, ncols=2))
    return _attn_merge(parts)


def _attention_bwd(qkv, do, o, lse, bias):
    T = qkv.shape[0]
    NG = len(DILATIONS)
    dqkv, dbs = None, []
    for g, d in enumerate(DILATIONS):
        src, (qc, kc, vc) = _group_qkv(qkv, g, d)
        do_g, o_g, lse_g = do, o, lse
        if d > 1:
            do_g = _reorder_rows(f"do_to_residues{g}", do, d, False)
            o_g = _reorder_rows(f"o_to_residues{g}", o, d, False)
            lse_g = _reorder_rows(f"lse_to_residues{g}", lse, d, False)
        dq, dkv, db = _attn_bwd(f"attn_bwd_{g}", g, src, qc, kc, vc, do_g, o_g, lse_g, bias, d)
        dqkv = _reorder_rows(f"dq_to_positions{g}", dq, d, True, dst=dqkv, dst_col=g, dst_blocks=3 * NG)
        dqkv = _reorder_rows(f"dkv_to_positions{g}", dkv, d, True, ncols=2, dst=dqkv, dst_col=NG + g, dst_stride=NG,
                             dst_blocks=3 * NG)
        dbs.append(db)
    return dqkv, jnp.stack(dbs)


def _other_chips(x, y):
    return [(1 - x, y), (x, 1 - y), (1 - x, 1 - y)]


def _shard_region(ref, shape, by_cols, chip, rows=None):
    R, C = shape
    start, size = (0, R) if rows is None else rows
    if by_cols:
        return ref.at[pl.ds(start, size), pl.ds(chip * C, C)]
    return ref.at[pl.ds(chip * R + start, size), :]


def _gather_weights(entries):
    n = len(entries)
    shapes = [e[0].shape[1:] for e in entries]

    def places(ins, outs, sems):
        send_sems, recv_sems, local_sems = sems
        x, y, c = lax.axis_index("x"), lax.axis_index("y"), lax.axis_index("c")

        def landing(f, px, py, pc):
            R = shapes[f][0]
            return _shard_region(outs[f], shapes[f], entries[f][2], 2 * px + py, rows=(pc * (R // 2), R // 2))

        def copy(f, k, block, to, src=None):
            dst = landing(f, *block)
            return pltpu.make_async_remote_copy(
                src_ref=dst if src is None else src, dst_ref=dst,
                send_sem=send_sems.at[6 * f + k], recv_sem=recv_sems.at[6 * f + k],
                device_id=to, device_id_type=MESH)

        def mine(f):
            dst = _shard_region(outs[f], shapes[f], entries[f][2], 2 * x + y)
            return pltpu.make_async_copy(ins[f].at[entries[f][1]], dst, local_sems.at[f])

        def first(f, j):
            R = shapes[f][0]
            src = ins[f].at[entries[f][1], pl.ds(c * (R // 2), R // 2), :]
            return copy(f, j, (x, y, c), (*_other_chips(x, y)[j], c), src=src)

        return x, y, c, copy, mine, first

    def start(ins, outs, sems):
        _, _, _, _, mine, first = places(ins, outs, sems)
        for f in range(n):
            mine(f).start()
        for j in range(3):
            for f in range(n):
                first(f, j).start()

    def mid(ins, outs, sems):
        x, y, c, copy, _, _ = places(ins, outs, sems)
        for j, chip in enumerate(_other_chips(x, y)):
            for f in range(n):
                copy(f, j, (*chip, c), (x, y, c)).wait_recv()
                copy(f, 3 + j, (*chip, c), (x, y, 1 - c)).start()

    def end(ins, outs, sems):
        x, y, c, copy, mine, first = places(ins, outs, sems)
        for j, chip in enumerate(_other_chips(x, y)):
            for f in range(n):
                copy(f, 3 + j, (*chip, 1 - c), (x, y, c)).wait_recv()
        for j, chip in enumerate(_other_chips(x, y)):
            for f in range(n):
                first(f, j).wait_send()
                copy(f, 3 + j, (*chip, c), (x, y, 1 - c)).wait_send()
        for f in range(n):
            mine(f).wait()

    def whole(f):
        R, C = shapes[f]
        return (R, N_CHIPS * C) if entries[f][2] else (N_CHIPS * R, C)

    return _Comm(
        [e[0] for e in entries], [jax.ShapeDtypeStruct(whole(f), BF16) for f in range(n)],
        [pltpu.SemaphoreType.DMA((6 * n,)), pltpu.SemaphoreType.DMA((6 * n,)), pltpu.SemaphoreType.DMA((n,))],
        start, end, mid)


def _scatter_grads(entries):
    n = len(entries)

    def copies(ins, outs, sems):
        send_sems, recv_sems, local_sems = sems
        x, y, c = lax.axis_index("x"), lax.axis_index("y"), lax.axis_index("c")
        me = 2 * x + y

        def piece(f, chip):
            return _shard_region(ins[f], entries[f][1], entries[f][2], chip)

        mine = [pltpu.make_async_copy(piece(f, me), outs[f].at[me], local_sems.at[f]) for f in range(n)]
        sends = [pltpu.make_async_remote_copy(
            src_ref=piece(f, 2 * px + py), dst_ref=outs[f].at[me],
            send_sem=send_sems.at[3 * f + j], recv_sem=recv_sems.at[3 * f + j],
            device_id=(px, py, c), device_id_type=MESH)
            for j, (px, py) in enumerate(_other_chips(x, y)) for f in range(n)]
        return mine, sends

    def start(ins, outs, sems):
        mine, sends = copies(ins, outs, sems)
        for cp in mine + sends:
            cp.start()

    def end(ins, outs, sems):
        mine, sends = copies(ins, outs, sems)
        for cp in sends + mine:
            cp.wait()

    return _Comm(
        [e[0] for e in entries], [jax.ShapeDtypeStruct((N_CHIPS,) + tuple(e[1]), BF16) for e in entries],
        [pltpu.SemaphoreType.DMA((3 * n,)), pltpu.SemaphoreType.DMA((3 * n,)), pltpu.SemaphoreType.DMA((n,))],
        start, end)


def _exchange_sibling(parts):
    n = len(parts)

    def copies(ins, outs, sems):
        send_sems, recv_sems = sems
        sibling = (lax.axis_index("x"), lax.axis_index("y"), 1 - lax.axis_index("c"))
        return [pltpu.make_async_remote_copy(src_ref=ins[i], dst_ref=outs[i], send_sem=send_sems.at[i],
                                             recv_sem=recv_sems.at[i], device_id=sibling, device_id_type=MESH)
                for i in range(n)]

    def start(ins, outs, sems):
        for cp in copies(ins, outs, sems):
            cp.start()

    def end(ins, outs, sems):
        for cp in copies(ins, outs, sems):
            cp.wait()

    return _Comm(parts, [jax.ShapeDtypeStruct(s.shape, s.dtype) for s in parts],
                 [pltpu.SemaphoreType.DMA((n,)), pltpu.SemaphoreType.DMA((n,))], start, end)


def _allgather_small(block):
    m_per, ncol = block.shape

    def body(x_ref, out_ref, send_sems, recv_sems, local_sem):
        x, y, c = lax.axis_index("x"), lax.axis_index("y"), lax.axis_index("c")
        me, sibling = (x, y, c), (x, y, 1 - c)
        chips = _other_chips(x, y)

        def rows(px, py, pc):
            return out_ref.at[4 * px + 2 * py + pc]

        def copy(k, block_of, to, src=None):
            return pltpu.make_async_remote_copy(
                src_ref=rows(*block_of) if src is None else src, dst_ref=rows(*block_of),
                send_sem=send_sems.at[k], recv_sem=recv_sems.at[k], device_id=to, device_id_type=MESH)

        mine = pltpu.make_async_copy(x_ref, rows(*me), local_sem)
        mine.start()
        first = [copy(0, me, sibling, src=x_ref)]
        first += [copy(1 + j, me, (*chip, c), src=x_ref) for j, chip in enumerate(chips)]
        for cp in first:
            cp.start()
        passed = [copy(4 + j, (*chip, c), sibling) for j, chip in enumerate(chips)]
        for j, chip in enumerate(chips):
            copy(1 + j, (*chip, c), me).wait_recv()
            passed[j].start()
        copy(0, sibling, me).wait_recv()
        for j, chip in enumerate(chips):
            copy(4 + j, (*chip, 1 - c), me).wait_recv()
        for cp in first + passed:
            cp.wait_send()
        mine.wait()

    return pl.pallas_call(
        body, name="allgather_small",
        in_specs=[pl.BlockSpec(memory_space=pltpu.VMEM)], out_specs=pl.BlockSpec(memory_space=pltpu.VMEM),
        out_shape=jax.ShapeDtypeStruct((N_DEV, m_per, ncol), block.dtype),
        scratch_shapes=[pltpu.SemaphoreType.DMA((7,)), pltpu.SemaphoreType.DMA((7,)), pltpu.SemaphoreType.DMA],
        compiler_params=_params(),
    )(block)


def _adamw(w, g, m, v):
    m = ADAM_B1 * m + (1.0 - ADAM_B1) * g
    v = ADAM_B2 * v + (1.0 - ADAM_B2) * jnp.square(g)
    m_hat = m / (1.0 - ADAM_B1 ** ADAM_STEP)
    v_hat = v / (1.0 - ADAM_B2 ** ADAM_STEP)
    delta = -ADAM_LR * (m_hat / (jnp.sqrt(v_hat) + ADAM_EPS) + ADAM_WD * w)
    return delta, m, v


def _flat_tile(rows):
    return min(rows, 256)


def _sum_pieces(name, layers):
    L = len(layers)
    P, R, C = layers[0].shape
    tr = _flat_tile(R)

    def body(*refs):
        out_ref = refs[L]
        for l in range(L):
            @pl.when(pl.program_id(0) == l)
            def _(p_ref=refs[l]):
                acc = p_ref[0].astype(F32)
                for j in range(1, P):
                    acc = acc + p_ref[j].astype(F32)
                out_ref[...] = acc

    return pl.pallas_call(
        body, name=name, grid=(L, R // tr),
        in_specs=[pl.BlockSpec((P, tr, C), lambda l, i: (0, i, 0)) for _ in range(L)],
        out_specs=pl.BlockSpec((None, tr, C), lambda l, i: (l, i, 0)),
        out_shape=jax.ShapeDtypeStruct((L, R, C), F32),
        compiler_params=_params(dimension_semantics=("parallel", "parallel")),
    )(*layers)


def _adam_pair(name, w, m, v, part_a, part_b):
    L, R, C = w.shape
    tr = _flat_tile(R)

    def body(w_ref, m_ref, v_ref, a_ref, b_ref, g_ref, d_ref, nm_ref, nv_ref):
        g = a_ref[...] + b_ref[...]
        g_ref[...] = g
        d_ref[...], nm_ref[...], nv_ref[...] = _adamw(w_ref[...], g, m_ref[...], v_ref[...])

    row = pl.BlockSpec((None, tr, C), lambda l, i: (l, i, 0))
    return pl.pallas_call(
        body, name=name, grid=(L, R // tr),
        in_specs=[row] * 5, out_specs=[row] * 4,
        out_shape=[jax.ShapeDtypeStruct((L, R, C), F32)] * 4,
        compiler_params=_params(dimension_semantics=("parallel", "parallel")),
    )(w, m, v, part_a, part_b)


def _adam_small(w, m, v, gathered):
    R, C = w.shape

    def body(w_ref, m_ref, v_ref, p_ref, g_ref, d_ref, nm_ref, nv_ref):
        g = p_ref[0]
        for j in range(1, N_DEV):
            g = g + p_ref[j]
        g_ref[...] = g
        d_ref[...], nm_ref[...], nv_ref[...] = _adamw(w_ref[...], g, m_ref[...], v_ref[...])

    return pl.pallas_call(
        body, name="adam_small",
        out_shape=[jax.ShapeDtypeStruct((R, C), F32)] * 4,
        compiler_params=_params(),
    )(w, m, v, gathered)


SMALL = ("mix_norm_g", "mlp_norm_g", "final_norm_g", "a_ln_g", "a_ln_b", "a_w_s", "a_b_s", "rel_bias")


def _pack_small(arrays, width):
    rows = []
    for a in arrays:
        flat = a.reshape(-1)
        pad = (-flat.shape[0]) % width
        rows.append(jnp.pad(flat, (0, pad)).reshape(-1, width))
    block = jnp.concatenate(rows, axis=0)
    return jnp.pad(block, ((0, (-block.shape[0]) % 8), (0, 0)))


def _unpack_small(block, shapes, width):
    out, row = [], 0
    for shape in shapes:
        size = int(np.prod(shape))
        nrows = -(-size // width)
        out.append(block[row:row + nrows].reshape(-1)[:size].reshape(shape))
        row += nrows
    return out


def kernel(x, mix_norm_g, mlp_norm_g, final_norm_g, a_w_in, a_ln_g, a_ln_b, a_w_s, a_b_s, a_w_out, b_w_qkv, b_w_out, rel_bias, w_up, w_down, loss_target, m_mix_norm_g, m_mlp_norm_g, m_final_norm_g, m_a_w_in, m_a_ln_g, m_a_ln_b, m_a_w_s, m_a_b_s, m_a_w_out, m_b_w_qkv, m_b_w_out, m_rel_bias, m_w_up, m_w_down, v_mix_norm_g, v_mlp_norm_g, v_final_norm_g, v_a_w_in, v_a_ln_g, v_a_ln_b, v_a_w_s, v_a_b_s, v_a_w_out, v_b_w_qkv, v_b_w_out, v_rel_bias, v_w_up, v_w_down):
    T, D = x.shape[1], x.shape[2]
    h0 = x.reshape(T, D)
    target = loss_target.reshape(T, D)
    G = a_w_s.shape[1]

    w_big = [a_w_in, a_w_out, b_w_qkv, b_w_out, w_up, w_down]
    m_big = [m_a_w_in, m_a_w_out, m_b_w_qkv, m_b_w_out, m_w_up, m_w_down]
    v_big = [v_a_w_in, v_a_w_out, v_b_w_qkv, v_b_w_out, v_w_up, v_w_down]
    by_cols = [True, False, True, True, True, False]
    s_in, s_out, s_qkv, s_bo, s_up, s_dn = [w.astype(BF16) for w in w_big]
    (W_in,) = _run_comm("gather_a", _gather_weights([(s_in, 0, True)]))

    tril = jnp.tril(jnp.ones((CHUNK, CHUNK), dtype=bool))
    w_tril = jnp.where(tril[None], a_w_s[0], 0.0).astype(BF16)
    w_tril_t = jnp.swapaxes(w_tril, 1, 2)
    b_rows = jnp.broadcast_to(a_b_s[0][:, :, None], (G, CHUNK, CHUNK))
    buckets = _bucket_maps()
    bias = _bias_build(rel_bias, buckets)

    QKV = s_qkv.shape[2] * N_CHIPS
    TM = 1024
    TK_WGRAD = 2048

    def matmul(name, a, b, mode, out, tm=TM, tn=1024, **kw):
        outs = out if isinstance(out, list) else [out]
        return _mm(name, a, b, mode, tm=tm, tn=tn, tk=a.shape[1], outs=outs, **kw)

    def norm_bwd(layer_gain, h, dres, copies=2):
        return dict(epi=_epi_rms_bwd(copies), extras=(h, dres), vecs=(layer_gain,), col_sums=1)

    def wgrad(name, a, b, tn=1024, comm=None):
        return _mm(name, a, b, "tn", tm=1024, tn=tn, tk=TK_WGRAD, outs=[BF16], comm=comm)

    def scatter(*which):
        return _scatter_grads([(g, w_big[i].shape[1:], by_cols[i]) for g, i in which])

    (a_pre, y0), (W_out, W_up0) = matmul("a_in", h0, W_in, "nn", BF16, norm_gain=mix_norm_g[0:1],
                                         comm=_gather_weights([(s_out, 0, False), (s_up, 0, True)]))
    z = _gate_fwd(a_pre, a_ln_g, a_ln_b, w_tril, b_rows)
    h1 = matmul("a_out", z, W_out, "nn", F32, epi=_epi_residual, extras=(h0,))
    (q1, y1), (W_dn0,) = matmul("mlp_up0", h1, W_up0, "nn", BF16, epi=_epi_relu2, norm_gain=mlp_norm_g[0:1],
                                comm=_gather_weights([(s_dn, 0, False)]))
    h2, (W_qkv, W_bo) = matmul("mlp_down0", q1, W_dn0, "nn", F32, tm=TM // 2, epi=_epi_residual, extras=(h1,),
                               comm=_gather_weights([(s_qkv, 0, True), (s_bo, 0, True)]))
    (qkv, y2), (W_up1,) = matmul("b_qkv", h2, W_qkv, "nn", BF16, tn=QKV // 4, norm_gain=mix_norm_g[1:2],
                                 comm=_gather_weights([(s_up, 1, True)]))
    o, lse = _attention_fwd(qkv, bias)
    h3 = matmul("b_out", o, W_bo, "nn", F32, epi=_epi_residual, extras=(h2,))
    (q3, y3), (W_dn1,) = matmul("mlp_up1", h3, W_up1, "nn", BF16, epi=_epi_relu2, norm_gain=mlp_norm_g[1:2],
                                comm=_gather_weights([(s_dn, 1, False)]))
    dh4, dh4_b, d_final_g, loss_row = matmul("mlp_down1", q3, W_dn1, "nn", [F32, BF16], tm=TM // 2, epi=_epi_loss_head,
                                             extras=(h3, target), vecs=(final_norm_g.reshape(1, D),), col_sums=2)
    loss = lax.psum(loss_row[0, 0], ("x", "y", "c"))

    dp3 = matmul("mlp_down_bwd1", dh4_b, W_dn1, "nt", BF16, epi=_epi_relu2_grad, extras=(q3,))
    g_dn1 = wgrad("mlp_down_wgrad1", q3, dh4_b)
    g_up1, (r_dn1,) = wgrad("mlp_up_wgrad1", y3, dp3, comm=scatter((g_dn1, 5)))
    (dh3, dh3_b, dg_mlp1), (r_up1,) = matmul("mlp_up_bwd1", dp3, W_up1, "nt", [F32, BF16], tm=TM // 2,
                                             comm=scatter((g_up1, 4)), **norm_bwd(mlp_norm_g[1:2], h3, dh4))
    do = matmul("b_out_bwd", dh3_b, W_bo, "nt", BF16)
    g_bo = wgrad("b_out_wgrad", o, dh3_b)
    dqkv, dbias = _attention_bwd(qkv, do, o, lse, bias)
    d_rel_bias = _bias_scatter(dbias, buckets)
    (dh2, dh2_b, dg_mix1), (r_bo,) = matmul("b_qkv_bwd", dqkv, W_qkv, "nt", [F32, BF16], tm=TM // 2,
                                            comm=scatter((g_bo, 3)), **norm_bwd(mix_norm_g[1:2], h2, dh3))
    g_qkv = wgrad("b_qkv_wgrad", y2, dqkv, tn=QKV // 4)
    dp1, (r_qkv,) = matmul("mlp_down_bwd0", dh2_b, W_dn0, "nt", BF16, epi=_epi_relu2_grad, extras=(q1,),
                           comm=scatter((g_qkv, 2)))
    g_up0 = wgrad("mlp_up_wgrad0", y1, dp1)
    g_dn0, (r_up0,) = wgrad("mlp_down_wgrad0", q1, dh2_b, comm=scatter((g_up0, 4)))
    (dh1, dh1_b, dg_mlp0), (r_dn0,) = matmul("mlp_up_bwd0", dp1, W_up0, "nt", [F32, BF16], tm=TM // 2,
                                             comm=scatter((g_dn0, 5)), **norm_bwd(mlp_norm_g[0:1], h1, dh2))
    dz = matmul("a_out_bwd", dh1_b, W_out, "nt", F32)
    g_out = wgrad("a_out_wgrad", z, dh1_b)
    da, d_ln_g, d_ln_b, d_w_s, d_b_s = _gate_bwd(a_pre, dz, a_ln_g, a_ln_b, w_tril, w_tril_t, b_rows)
    g_in, (r_out,) = wgrad("a_in_wgrad", y0, da, comm=scatter((g_out, 1)))
    (grad_x, dg_mix0), (r_in,) = matmul("a_in_bwd", da, W_in, "nt", F32, comm=scatter((g_in, 0)),
                                        **norm_bwd(mix_norm_g[0:1], h0, dh1, copies=1))

    received = [[r_in], [r_out], [r_qkv], [r_bo], [r_up0, r_up1], [r_dn0, r_dn1]]
    plane = [_sum_pieces(f"sum_pieces{i}", r) for i, r in enumerate(received)]
    other = _run_comm("exchange_sibling", _exchange_sibling(plane))
    big_out = [_adam_pair(f"adam{i}", w_big[i], m_big[i], v_big[i], plane[i], other[i]) for i in range(len(w_big))]

    def unbig(kind):
        return dict(zip(["a_w_in", "a_w_out", "b_w_qkv", "b_w_out", "w_up", "w_down"], [b[kind] for b in big_out]))

    small_w = [mix_norm_g, mlp_norm_g, final_norm_g, a_ln_g, a_ln_b, a_w_s, a_b_s, rel_bias]
    small_m = [m_mix_norm_g, m_mlp_norm_g, m_final_norm_g, m_a_ln_g, m_a_ln_b, m_a_w_s, m_a_b_s, m_rel_bias]
    small_v = [v_mix_norm_g, v_mlp_norm_g, v_final_norm_g, v_a_ln_g, v_a_ln_b, v_a_w_s, v_a_b_s, v_rel_bias]
    small_g = [jnp.concatenate([dg_mix0, dg_mix1]), jnp.concatenate([dg_mlp0, dg_mlp1]), d_final_g,
               d_ln_g, d_ln_b, d_w_s[None], d_b_s[None, :, :, 0], d_rel_bias]
    width = max(D, 128)
    gathered_small = _allgather_small(_pack_small(small_g, width))
    small_out = _adam_small(_pack_small(small_w, width), _pack_small(small_m, width), _pack_small(small_v, width),
                            gathered_small)
    shapes = [w.shape for w in small_w]

    names = ["mix_norm_g", "mlp_norm_g", "final_norm_g", "a_w_in", "a_ln_g", "a_ln_b", "a_w_s", "a_b_s", "a_w_out",
             "b_w_qkv", "b_w_out", "rel_bias", "w_up", "w_down"]
    results = [loss, grad_x.reshape(x.shape)]
    for kind in range(4):
        table = dict(zip(SMALL, _unpack_small(small_out[kind], shapes, width)))
        table.update(unbig(kind))
        results += [table[n] for n in names]
    return tuple(results)
```

```python
import functools
import math

import numpy as np
import jax
import jax.numpy as jnp
from jax import lax
from jax.experimental import pallas as pl
from jax.experimental.pallas import tpu as pltpu

F32 = jnp.float32
BF16 = jnp.bfloat16
MESH = pl.DeviceIdType.MESH
ANY = pl.BlockSpec(memory_space=pl.ANY)

N_CHIPS = 4
N_DEV = 8
VMEM_LIMIT_BYTES = 56 * 1024 * 1024

EPS = 1e-6
NEG_INF = -1e30
CHUNK = 128
GROUP_DIM = 128
HEAD_DIM = 64
ATT_HEADS = 8
ATT_WIDTH = ATT_HEADS * HEAD_DIM
PAIR = 2 * HEAD_DIM
BLK = 128
DILATIONS = (1, 4, 16)
N_BUCKETS = 32
MAX_EXACT = N_BUCKETS // 2
REL_MAX_DISTANCE = 2048

ADAM_LR = 0.001
ADAM_B1 = 0.9
ADAM_B2 = 0.999
ADAM_EPS = 1e-08
ADAM_WD = 0.01
ADAM_STEP = 10

NN = (((1,), (0,)), ((), ()))
NT = (((1,), (1,)), ((), ()))
TN = (((0,), (0,)), ((), ()))


def _params(**kw):
    return pltpu.CompilerParams(vmem_limit_bytes=VMEM_LIMIT_BYTES, **kw)


def _dot(a, b, dims=NN):
    return lax.dot_general(a, b, dims, preferred_element_type=F32)


def _gelu(x):
    return 0.5 * x * (1.0 + lax.erf(x * math.sqrt(0.5)))


def _gelu_grad(x):
    return 0.5 * (1.0 + lax.erf(x * math.sqrt(0.5))) + x * jnp.exp(-0.5 * x * x) * (1.0 / math.sqrt(2.0 * math.pi))


def _mean(x):
    return jnp.mean(x, axis=-1, keepdims=True)


class _Comm:
    def __init__(self, inputs, out_shapes, scratch, start, end, mid=None):
        self.inputs, self.out_shapes, self.scratch = list(inputs), list(out_shapes), list(scratch)
        self.start, self.mid, self.end = start, mid, end


def _run_comm(name, comm):
    n_in, n_out = len(comm.inputs), len(comm.out_shapes)

    def body(*refs):
        parts = refs[:n_in], refs[n_in:n_in + n_out], refs[n_in + n_out:]
        comm.start(*parts)
        if comm.mid is not None:
            comm.mid(*parts)
        comm.end(*parts)

    return pl.pallas_call(
        body, name=name, in_specs=[ANY] * n_in, out_specs=[ANY] * n_out, out_shape=comm.out_shapes,
        scratch_shapes=comm.scratch, compiler_params=_params(),
    )(*comm.inputs)


def _mm(name, a, b, mode, *, tm, tn, tk, outs, epi=None, extras=(), vecs=(), col_sums=0, norm_gain=None, comm=None):
    if mode == "tn":
        K, M = a.shape
    else:
        M, K = a.shape
    N = b.shape[0] if mode == "nt" else b.shape[1]
    tm, tn, tk = min(tm, M), min(tn, N), min(tk, K)
    assert M % tm == 0 and N % tn == 0 and K % tk == 0, (name, M, N, K, tm, tn, tk)
    nk = K // tk
    grid = (M // tm, N // tn, nk)

    if mode == "tn":
        a_spec = pl.BlockSpec((tk, tm), lambda i, j, k: (k, i))
    else:
        a_spec = pl.BlockSpec((tm, tk), lambda i, j, k: (i, k))
    if mode == "nt":
        b_spec = pl.BlockSpec((tn, tk), lambda i, j, k: (j, k))
    else:
        b_spec = pl.BlockSpec((tk, tn), lambda i, j, k: (k, j))
    tile = pl.BlockSpec((tm, tn), lambda i, j, k: (i, j))
    vec = pl.BlockSpec((1, tn), lambda i, j, k: (0, j))
    normed = norm_gain is not None
    assert not normed or (mode == "nn" and nk == 1)
    assert col_sums == 0 or grid[1] == 1
    out_shapes = [jax.ShapeDtypeStruct((M, N), dtype) for dtype in outs]
    out_specs = [tile for _ in outs]
    if normed:
        out_shapes.append(jax.ShapeDtypeStruct((M, K), BF16))
        out_specs.append(pl.BlockSpec((tm, K), lambda i, j, k: (i, 0)))
    out_shapes += [jax.ShapeDtypeStruct((1, N), F32)] * col_sums
    out_specs += [vec] * col_sums
    extra_specs = [tile for _ in extras] + [vec for _ in vecs]
    if normed:
        extra_specs.append(pl.BlockSpec((1, K), lambda i, j, k: (0, 0)))
    n_extra, n_out = len(extra_specs), len(out_shapes)
    n_tiles = len(outs)
    n_cin = len(comm.inputs) if comm else 0
    n_cout = len(comm.out_shapes) if comm else 0
    dims = {"nn": NN, "nt": NT, "tn": TN}[mode]
    steps = grid[0] * grid[1] * grid[2]

    def body(*refs):
        a_ref, b_ref = refs[0], refs[1]
        pos = 2
        extra_refs = refs[pos:pos + n_extra]
        pos += n_extra
        comm_in = refs[pos:pos + n_cin]
        pos += n_cin
        out_refs = refs[pos:pos + n_out]
        pos += n_out
        comm_out = refs[pos:pos + n_cout]
        pos += n_cout
        acc_ref = refs[pos] if nk > 1 else None
        pos += nk > 1
        y_ref = refs[pos] if normed else None
        comm_sems = refs[pos + normed:]
        k = pl.program_id(2)
        step = (pl.program_id(0) * grid[1] + pl.program_id(1)) * nk + k

        if comm is not None:
            @pl.when(step == 0)
            def _():
                comm.start(comm_in, comm_out, comm_sems)

        if normed:
            @pl.when(pl.program_id(1) == 0)
            def _():
                hv = a_ref[...]
                y = (hv * lax.rsqrt(_mean(hv * hv) + EPS) * extra_refs[-1][...]).astype(BF16)
                y_ref[...] = y
                out_refs[n_tiles][...] = y

            lhs = y_ref[...]
        else:
            lhs = a_ref[...].astype(BF16)
        part = _dot(lhs, b_ref[...].astype(BF16), dims)

        def finish(acc):
            epi_args = [e[...] for e in extra_refs[:n_extra - normed]]
            res = epi(acc, *epi_args) if epi is not None else (acc,) * n_tiles
            for o, r in zip(out_refs[:n_tiles], res[:n_tiles]):
                o[...] = r.astype(o.dtype)
            if col_sums:
                sums = out_refs[n_out - col_sums:]

                @pl.when(pl.program_id(0) == 0)
                def _():
                    for o in sums:
                        o[...] = jnp.zeros_like(o)

                for o, r in zip(sums, res[n_tiles:]):
                    o[...] += r

        if nk == 1:
            finish(part)
        else:
            @pl.when(k == 0)
            def _():
                acc_ref[...] = part

            @pl.when(k > 0)
            def _():
                acc_ref[...] += part

            @pl.when(k == nk - 1)
            def _():
                finish(acc_ref[...])

        if comm is not None:
            if comm.mid is not None:
                @pl.when(step == (3 * steps) // 4)
                def _():
                    comm.mid(comm_in, comm_out, comm_sems)

            @pl.when(step == steps - 1)
            def _():
                comm.end(comm_in, comm_out, comm_sems)

    sequential = comm is not None or normed or col_sums > 0
    order = ("arbitrary",) * 3 if sequential else ("parallel", "parallel", "arbitrary")
    scratch = [pltpu.VMEM((tm, tn), F32)] if nk > 1 else []
    if normed:
        scratch.append(pltpu.VMEM((tm, K), BF16))
    res = pl.pallas_call(
        body, name=name, grid=grid,
        in_specs=[a_spec, b_spec] + extra_specs + [ANY] * n_cin,
        out_specs=out_specs + [ANY] * n_cout,
        out_shape=out_shapes + (comm.out_shapes if comm else []),
        scratch_shapes=scratch + (comm.scratch if comm else []),
        compiler_params=_params(dimension_semantics=order),
    )(a, b, *extras, *vecs, *([norm_gain] if normed else []), *(comm.inputs if comm else []))
    mm_out = res[0] if n_out == 1 else list(res[:n_out])
    return (mm_out, list(res[n_out:])) if comm else mm_out


def _epi_residual(acc, res):
    return (res + acc,)


def _epi_relu2(acc):
    return (jnp.square(jnp.maximum(acc, 0.0)),)


def _epi_rms_bwd(copies):
    def epi(acc, h, dres, g):
        r = lax.rsqrt(_mean(h * h) + EPS)
        hn = h * r
        dyg = acc * g
        dh = dres + r * (dyg - hn * _mean(dyg * hn))
        return (dh,) * copies + (jnp.sum(acc * hn, axis=0, keepdims=True),)
    return epi


def _epi_loss_head(acc, res, target, g):
    h = res + acc
    r = lax.rsqrt(_mean(h * h) + EPS)
    hn = h * r
    diff = hn * g - target
    loss = 0.5 * jnp.sum(_mean(diff * diff))
    dy = diff * (1.0 / h.shape[-1])
    dyg = dy * g
    dh = r * (dyg - hn * _mean(dyg * hn))
    return dh, dh, jnp.sum(dy * hn, axis=0, keepdims=True), jnp.full((1, h.shape[-1]), loss, F32)


def _epi_relu2_grad(acc, q):
    qf = q.astype(F32)
    return (acc * jnp.where(qf > 0.0, (2.0 * qf) * lax.rsqrt(qf), 0.0),)


def _row_tile(T):
    return min(T, 512)


def _gate_tile(T):
    return min(T, 256)


def _gate_fwd(a, ln_g, ln_b, w_tril, b_rows):
    T, W2 = a.shape
    W = W2 // 2
    G = W // GROUP_DIM
    tr = _gate_tile(T)

    def body(a_ref, lng_ref, lnb_ref, w_ref, b_ref, z_ref):
        u = _gelu(a_ref[:, :W].astype(F32))
        vg = _gelu(a_ref[:, W:].astype(F32))
        xc = vg - _mean(vg)
        vn = xc * lax.rsqrt(_mean(xc * xc) + EPS)
        vl = (vn * lng_ref[...] + lnb_ref[...]).astype(BF16)
        for n in range(tr // CHUNK):
            rows = slice(n * CHUNK, (n + 1) * CHUNK)
            for g in range(G):
                cols = slice(g * GROUP_DIM, (g + 1) * GROUP_DIM)
                gate = _dot(w_ref[g], vl[rows, cols]) + b_ref[g]
                z_ref[rows, cols] = (u[rows, cols] * gate).astype(BF16)

    vec = pl.BlockSpec((1, W), lambda i: (0, 0))
    grp = pl.BlockSpec((G, CHUNK, CHUNK), lambda i: (0, 0, 0))
    return pl.pallas_call(
        body, name="gate_fwd", grid=(T // tr,),
        in_specs=[pl.BlockSpec((tr, W2), lambda i: (i, 0)), vec, vec, grp, grp],
        out_specs=pl.BlockSpec((tr, W), lambda i: (i, 0)),
        out_shape=jax.ShapeDtypeStruct((T, W), BF16),
        compiler_params=_params(dimension_semantics=("parallel",)),
    )(a, ln_g, ln_b, w_tril, b_rows)


def _gate_bwd(a, dz, ln_g, ln_b, w_tril, w_tril_t, b_rows):
    T, W2 = a.shape
    W = W2 // 2
    G = W // GROUP_DIM
    tr = _gate_tile(T)
    steps = T // tr

    def body(a_ref, dz_ref, lng_ref, lnb_ref, w_ref, wt_ref, b_ref, da_ref, dlng_ref, dlnb_ref, dw_ref, dbs_ref, dvl_ref):
        step = pl.program_id(0)

        @pl.when(step == 0)
        def _():
            dlng_ref[...] = jnp.zeros_like(dlng_ref)
            dlnb_ref[...] = jnp.zeros_like(dlnb_ref)
            dw_ref[...] = jnp.zeros_like(dw_ref)
            dbs_ref[...] = jnp.zeros_like(dbs_ref)

        au = a_ref[:, :W].astype(F32)
        av = a_ref[:, W:].astype(F32)
        u = _gelu(au)
        vg = _gelu(av)
        xc = vg - _mean(vg)
        rstd = lax.rsqrt(_mean(xc * xc) + EPS)
        vn = xc * rstd
        lng = lng_ref[...]
        vl = (vn * lng + lnb_ref[...]).astype(BF16)
        du_scale = dz_ref[...] * _gelu_grad(au)
        dgate_all = dz_ref[...] * u
        for n in range(tr // CHUNK):
            rows = slice(n * CHUNK, (n + 1) * CHUNK)
            for g in range(G):
                cols = slice(g * GROUP_DIM, (g + 1) * GROUP_DIM)
                vlg = vl[rows, cols]
                gate = _dot(w_ref[g], vlg) + b_ref[g]
                da_ref[rows, cols] = (du_scale[rows, cols] * gate).astype(BF16)
                dgate = dgate_all[rows, cols]
                dbs_ref[g] += dgate
                dgate_b = dgate.astype(BF16)
                dw_ref[g] += _dot(dgate_b, vlg, NT)
                dvl_ref[rows, cols] = _dot(wt_ref[g], dgate_b)
        dvl = dvl_ref[...]
        dlnb_ref[...] += jnp.sum(dvl, axis=0, keepdims=True)
        dlng_ref[...] += jnp.sum(dvl * vn, axis=0, keepdims=True)
        dvn = dvl * lng
        dvg = rstd * (dvn - _mean(dvn) - vn * _mean(dvn * vn))
        da_ref[:, W:] = (dvg * _gelu_grad(av)).astype(BF16)

        @pl.when(step == steps - 1)
        def _():
            t_idx = lax.broadcasted_iota(jnp.int32, (CHUNK, CHUNK), 0)
            s_idx = lax.broadcasted_iota(jnp.int32, (CHUNK, CHUNK), 1)
            for g in range(G):
                dw_ref[g] = jnp.where(s_idx <= t_idx, dw_ref[g], 0.0)
                dbs_ref[g] = jnp.broadcast_to(jnp.sum(dbs_ref[g], axis=-1, keepdims=True), (CHUNK, CHUNK))

    vec = pl.BlockSpec((1, W), lambda i: (0, 0))
    grp = pl.BlockSpec((G, CHUNK, CHUNK), lambda i: (0, 0, 0))
    return pl.pallas_call(
        body, name="gate_bwd", grid=(steps,),
        in_specs=[pl.BlockSpec((tr, W2), lambda i: (i, 0)), pl.BlockSpec((tr, W), lambda i: (i, 0)),
                  vec, vec, grp, grp, grp],
        out_specs=[pl.BlockSpec((tr, W2), lambda i: (i, 0)), vec, vec, grp, grp],
        out_shape=[jax.ShapeDtypeStruct((T, W2), BF16), jax.ShapeDtypeStruct((1, W), F32),
                   jax.ShapeDtypeStruct((1, W), F32), jax.ShapeDtypeStruct((G, CHUNK, CHUNK), F32),
                   jax.ShapeDtypeStruct((G, CHUNK, CHUNK), F32)],
        scratch_shapes=[pltpu.VMEM((tr, W), F32)],
        compiler_params=_params(dimension_semantics=("arbitrary",)),
    )(a, dz, ln_g, ln_b, w_tril, w_tril_t, b_rows)


def _bucket_map(dilation):
    rel = BLK + np.arange(BLK)[:, None] - np.arange(2 * BLK)[None, :]
    dist = np.clip(rel, 0, BLK) * dilation
    nf = np.maximum(dist, 1).astype(np.float32)
    large = MAX_EXACT + (np.log(nf / np.float32(MAX_EXACT)) / np.float32(math.log(REL_MAX_DISTANCE / MAX_EXACT))
                         * np.float32(N_BUCKETS - MAX_EXACT)).astype(np.int32)
    large = np.minimum(large, N_BUCKETS - 1)
    return np.where(dist < MAX_EXACT, dist, large).astype(np.int32)


def _bucket_maps():
    return jnp.asarray(np.stack([_bucket_map(d) for d in DILATIONS]))


def _bias_build(rel_bias, buckets):
    NG = len(DILATIONS)

    def body(table_ref, bucket_ref, out_ref):
        for g in range(NG):
            bk = bucket_ref[g]
            for h in range(ATT_HEADS):
                out_ref[0, g, h] = jnp.zeros((BLK, 2 * BLK), F32)
            for b in range(N_BUCKETS):
                hit = bk == b
                for h in range(ATT_HEADS):
                    out_ref[0, g, h] = jnp.where(hit, table_ref[b, g * ATT_HEADS + h], out_ref[0, g, h])
            for h in range(ATT_HEADS):
                for first in range(2):
                    out_ref[first, g, h] = jnp.where(_window_mask(first), out_ref[0, g, h], NEG_INF)

    return pl.pallas_call(
        body, name="bias_build",
        in_specs=[pl.BlockSpec(memory_space=pltpu.SMEM), pl.BlockSpec(memory_space=pltpu.VMEM)],
        out_specs=pl.BlockSpec(memory_space=pltpu.VMEM),
        out_shape=jax.ShapeDtypeStruct((2, NG, ATT_HEADS, BLK, 2 * BLK), F32),
        compiler_params=_params(),
    )(rel_bias, buckets)


def _bias_scatter(dbias, buckets):
    NG = len(DILATIONS)

    def body(dbias_ref, bucket_ref, out_ref):
        for g in range(NG):
            bk = bucket_ref[g]
            for b in range(N_BUCKETS):
                hit = bk == b
                for h in range(ATT_HEADS):
                    out_ref[b, g * ATT_HEADS + h] = jnp.sum(jnp.where(hit, dbias_ref[g, h], 0.0))

    return pl.pallas_call(
        body, name="bias_scatter",
        in_specs=[pl.BlockSpec(memory_space=pltpu.VMEM), pl.BlockSpec(memory_space=pltpu.VMEM)],
        out_specs=pl.BlockSpec(memory_space=pltpu.SMEM),
        out_shape=jax.ShapeDtypeStruct((N_BUCKETS, NG * ATT_HEADS), F32),
        compiler_params=_params(),
    )(dbias, buckets)


def _window_mask(first):
    qi = lax.broadcasted_iota(jnp.int32, (BLK, 2 * BLK), 0)
    kj = lax.broadcasted_iota(jnp.int32, (BLK, 2 * BLK), 1)
    rel = BLK + qi - kj
    return (rel >= 0) & (rel <= BLK) & (kj >= BLK * first)


def _head_lanes(hh):
    lane = lax.broadcasted_iota(jnp.int32, (1, PAIR), 1)
    return (lane >= hh * HEAD_DIM) & (lane < (hh + 1) * HEAD_DIM)


def _attn_fwd(name, g, qkv, qc, kc, vc, bias, stride):
    T = qkv.shape[0]
    nb = T // BLK
    scale = HEAD_DIM ** -0.5

    def body(q_ref, kp_ref, kc_ref, vp_ref, vc_ref, bias_ref, out_ref):
        b = pl.program_id(0)
        first = (b < stride).astype(jnp.int32)
        low = _head_lanes(0)
        for hp in range(ATT_HEADS // 2):
            cols = slice(hp * PAIR, (hp + 1) * PAIR)
            qp = q_ref[:, cols]
            kk = jnp.concatenate([kp_ref[:, cols], kc_ref[:, cols]], axis=0)
            vv = jnp.concatenate([vp_ref[:, cols], vc_ref[:, cols]], axis=0)
            o_h, lse_h = [], []
            for hh in range(2):
                qm = jnp.where(_head_lanes(hh), qp, jnp.zeros_like(qp))
                s = _dot(qm, kk, NT) * scale
                logits = s + bias_ref[first, 2 * hp + hh]
                m = jnp.max(logits, axis=-1, keepdims=True)
                p = jnp.exp(logits - m)
                den = jnp.sum(p, axis=-1, keepdims=True)
                o_h.append(_dot(p.astype(BF16), vv) / den)
                lse_h.append(m + jnp.log(den))
            out_ref[:, cols] = jnp.where(low, o_h[0], o_h[1])
            out_ref[:, slice(ATT_WIDTH + hp * PAIR, ATT_WIDTH + (hp + 1) * PAIR)] = jnp.where(low, lse_h[0], lse_h[1])

    def cur(c):
        return pl.BlockSpec((BLK, ATT_WIDTH), lambda b: (b, c))

    def prev(c):
        return pl.BlockSpec((BLK, ATT_WIDTH), lambda b: (jnp.maximum(b - stride, 0), c))

    return pl.pallas_call(
        body, name=name, grid=(nb,),
        in_specs=[cur(qc), prev(kc), cur(kc), prev(vc), cur(vc),
                  pl.BlockSpec((2, None, ATT_HEADS, BLK, 2 * BLK), lambda b: (0, g, 0, 0, 0))],
        out_specs=pl.BlockSpec((BLK, 2 * ATT_WIDTH), lambda b: (b, 0)),
        out_shape=jax.ShapeDtypeStruct((T, 2 * ATT_WIDTH), F32),
        compiler_params=_params(dimension_semantics=("parallel",)),
    )(qkv, qkv, qkv, qkv, qkv, bias)


def _attn_merge(parts):
    T = parts[0].shape[0]
    tr = _row_tile(T)
    n = len(parts)

    def body(*refs):
        o_refs, l_refs = refs[:n], refs[n:2 * n]
        o_ref, lse_ref = refs[2 * n], refs[2 * n + 1]
        ls = [r[...] for r in l_refs]
        m = functools.reduce(jnp.maximum, ls)
        es = [jnp.exp(l - m) for l in ls]
        tot = functools.reduce(lambda x, y: x + y, es)
        acc = functools.reduce(lambda x, y: x + y, [e * r[...] for e, r in zip(es, o_refs)])
        o_ref[...] = (acc / tot).astype(BF16)
        lse_ref[...] = m + jnp.log(tot)

    row = pl.BlockSpec((tr, ATT_WIDTH), lambda i: (i, 0))
    row_lse = pl.BlockSpec((tr, ATT_WIDTH), lambda i: (i, 1))
    return pl.pallas_call(
        body, name="attn_merge", grid=(T // tr,),
        in_specs=[row] * n + [row_lse] * n, out_specs=[row, row],
        out_shape=[jax.ShapeDtypeStruct((T, ATT_WIDTH), BF16), jax.ShapeDtypeStruct((T, ATT_WIDTH), F32)],
        compiler_params=_params(dimension_semantics=("parallel",)),
    )(*parts, *parts)


def _attn_bwd(name, g, qkv, qc, kc, vc, do, o, lse, bias, stride):
    T = qkv.shape[0]
    nb = T // BLK
    scale = HEAD_DIM ** -0.5

    def body(q_ref, kp_ref, kc_ref, vp_ref, vc_ref, do_ref, o_ref, lse_ref, bias_ref,
             dq_ref, dkv_ref, db_ref, carry_k, carry_v):
        b = pl.program_id(0)
        ck_ref = carry_k.at[b % stride]
        cv_ref = carry_v.at[b % stride]

        @pl.when(b == 0)
        def _():
            db_ref[...] = jnp.zeros_like(db_ref)
            carry_k[...] = jnp.zeros_like(carry_k)
            carry_v[...] = jnp.zeros_like(carry_v)

        @pl.when(b >= nb)
        def _():
            dkv_ref[:, :ATT_WIDTH] = ck_ref[...].astype(BF16)
            dkv_ref[:, ATT_WIDTH:] = cv_ref[...].astype(BF16)

        @pl.when(b < nb)
        def _():
            first = (b < stride).astype(jnp.int32)
            for hp in range(ATT_HEADS // 2):
                cols = slice(hp * PAIR, (hp + 1) * PAIR)
                qp = q_ref[:, cols]
                kk = jnp.concatenate([kp_ref[:, cols], kc_ref[:, cols]], axis=0)
                vv = jnp.concatenate([vp_ref[:, cols], vc_ref[:, cols]], axis=0)
                dop = do_ref[:, cols]
                lsep = lse_ref[:, cols]
                prod = dop.astype(F32) * o_ref[:, cols].astype(F32)
                dq = jnp.zeros((BLK, PAIR), F32)
                dk = jnp.zeros((2 * BLK, PAIR), F32)
                dv = jnp.zeros((2 * BLK, PAIR), F32)
                for hh in range(2):
                    lanes = _head_lanes(hh)
                    qm = jnp.where(lanes, qp, jnp.zeros_like(qp))
                    dom = jnp.where(lanes, dop, jnp.zeros_like(dop))
                    km = jnp.where(lanes, kk, jnp.zeros_like(kk))
                    delta = jnp.sum(jnp.where(lanes, prod, 0.0), axis=-1, keepdims=True)
                    lse_h = jnp.max(jnp.where(lanes, lsep, NEG_INF), axis=-1, keepdims=True)
                    s = _dot(qm, kk, NT) * scale
                    logits = s + bias_ref[first, 2 * hp + hh]
                    p = jnp.exp(logits - lse_h)
                    dv += _dot(p.astype(BF16), dom, TN)
                    ds = p * (_dot(dom, vv, NT) - delta)
                    db_ref[2 * hp + hh] += ds
                    dss = (ds * scale).astype(BF16)
                    dq += _dot(dss, km)
                    dk += _dot(dss, qm, TN)
                dq_ref[:, cols] = dq.astype(BF16)
                dkv_ref[:, cols] = (ck_ref[:, cols] + dk[:BLK]).astype(BF16)
                dkv_ref[:, slice(ATT_WIDTH + hp * PAIR, ATT_WIDTH + (hp + 1) * PAIR)] = (
                    cv_ref[:, cols] + dv[:BLK]).astype(BF16)
                ck_ref[:, cols] = dk[BLK:]
                cv_ref[:, cols] = dv[BLK:]

    last = nb - 1

    def cur(c):
        return pl.BlockSpec((BLK, ATT_WIDTH), lambda b: (jnp.minimum(b, last), c))

    def prev(c):
        return pl.BlockSpec((BLK, ATT_WIDTH), lambda b: (jnp.clip(b - stride, 0, last), c))

    dbias_shape = (ATT_HEADS, BLK, 2 * BLK)
    return pl.pallas_call(
        body, name=name, grid=(nb + stride,),
        in_specs=[cur(qc), prev(kc), cur(kc), prev(vc), cur(vc), cur(0), cur(0), cur(0),
                  pl.BlockSpec((2, None, ATT_HEADS, BLK, 2 * BLK), lambda b: (0, g, 0, 0, 0))],
        out_specs=[cur(0), pl.BlockSpec((BLK, 2 * ATT_WIDTH), lambda b: (jnp.clip(b - stride, 0, last), 0)),
                   pl.BlockSpec(dbias_shape, lambda b: (0, 0, 0))],
        out_shape=[jax.ShapeDtypeStruct((T, ATT_WIDTH), BF16), jax.ShapeDtypeStruct((T, 2 * ATT_WIDTH), BF16),
                   jax.ShapeDtypeStruct(dbias_shape, F32)],
        scratch_shapes=[pltpu.VMEM((stride, BLK, ATT_WIDTH), F32), pltpu.VMEM((stride, BLK, ATT_WIDTH), F32)],
        compiler_params=_params(dimension_semantics=("arbitrary",)),
    )(qkv, qkv, qkv, qkv, qkv, do, o, lse, bias)


REORDER_TILE = 256
REORDER_ROWS = 2048


def _reorder_matrix(d, inverse):
    per = REORDER_TILE // d
    p = np.zeros((REORDER_TILE, REORDER_TILE), np.float32)
    for src in range(REORDER_TILE):
        i, r = divmod(src, d)
        p[r * per + i, src] = 1.0
    return jnp.asarray(p.T if inverse else p, dtype=BF16)


def _reorder_rows(name, src, d, inverse, *, src_col=0, col_stride=1, ncols=1, dst=None, dst_col=0, dst_stride=1,
                  dst_blocks=None):
    T = src.shape[0]
    dtype = src.dtype
    span = BLK * d
    rows = max(span, min(T, REORDER_ROWS))
    per = REORDER_TILE // d
    tiles = span // REORDER_TILE
    dst_blocks = ncols if dst_blocks is None else dst_blocks

    def apply(p, x):
        if dtype == BF16:
            return _dot(p, x).astype(BF16)
        hi = x.astype(BF16)
        rest = x - hi.astype(F32)
        mid = rest.astype(BF16)
        low = (rest - mid.astype(F32)).astype(BF16)
        return _dot(p, hi) + _dot(p, mid) + _dot(p, low)

    def body(*refs):
        p_ref, x_ref, o_ref = refs[0], refs[1], refs[-1]
        if d == 1:
            o_ref[...] = x_ref[...]
            return
        for s in range(rows // span):
            for t in range(tiles):
                base = s * span
                tile_rows = slice(base + t * REORDER_TILE, base + (t + 1) * REORDER_TILE)
                chunk = lambda r: slice(base + r * BLK + t * per, base + r * BLK + (t + 1) * per)
                if inverse:
                    gathered = jnp.concatenate([x_ref[chunk(r), :] for r in range(d)], axis=0)
                    o_ref[tile_rows, :] = apply(p_ref[...], gathered)
                else:
                    y = apply(p_ref[...], x_ref[tile_rows, :])
                    for r in range(d):
                        o_ref[chunk(r), :] = y[r * per:(r + 1) * per]

    in_specs = [pl.BlockSpec((REORDER_TILE, REORDER_TILE), lambda w, k: (0, 0)),
                pl.BlockSpec((rows, ATT_WIDTH), lambda w, k: (w, src_col + col_stride * k))]
    operands = [_reorder_matrix(max(d, 2), inverse), src]
    aliases = {}
    if dst is not None:
        in_specs.append(ANY)
        operands.append(dst)
        aliases = {2: 0}
    return pl.pallas_call(
        body, name=name, grid=(T // rows, ncols), in_specs=in_specs,
        out_specs=pl.BlockSpec((rows, ATT_WIDTH), lambda w, k: (w, dst_col + dst_stride * k)),
        out_shape=jax.ShapeDtypeStruct((T, dst_blocks * ATT_WIDTH), dtype),
        input_output_aliases=aliases,
        compiler_params=_params(dimension_semantics=("parallel", "parallel")),
    )(*operands)


def _group_qkv(qkv, g, d):
    NG = len(DILATIONS)
    if d == 1:
        return qkv, (g, NG + g, 2 * NG + g)
    return _reorder_rows(f"qkv_to_residues{g}", qkv, d, False, src_col=g, col_stride=NG, ncols=3), (0, 1, 2)


def _attention_fwd(qkv, bias):
    T = qkv.shape[0]
    parts = []
    for g, d in enumerate(DILATIONS):
        src, (qc, kc, vc) = _group_qkv(qkv, g, d)
        part = _attn_fwd(f"attn_fwd_{g}", g, src, qc, kc, vc, bias, d)
        parts.append(part if d == 1 else _reorder_rows(f"out_to_positions{g}", part, d, True, ncols=2))
    return _attn_merge(parts)


def _attention_bwd(qkv, do, o, lse, bias):
    T = qkv.shape[0]
    NG = len(DILATIONS)
    dqkv, dbs = None, []
    for g, d in enumerate(DILATIONS):
        src, (qc, kc, vc) = _group_qkv(qkv, g, d)
        do_g, o_g, lse_g = do, o, lse
        if d > 1:
            do_g = _reorder_rows(f"do_to_residues{g}", do, d, False)
            o_g = _reorder_rows(f"o_to_residues{g}", o, d, False)
            lse_g = _reorder_rows(f"lse_to_residues{g}", lse, d, False)
        dq, dkv, db = _attn_bwd(f"attn_bwd_{g}", g, src, qc, kc, vc, do_g, o_g, lse_g, bias, d)
        dqkv = _reorder_rows(f"dq_to_positions{g}", dq, d, True, dst=dqkv, dst_col=g, dst_blocks=3 * NG)
        dqkv = _reorder_rows(f"dkv_to_positions{g}", dkv, d, True, ncols=2, dst=dqkv, dst_col=NG + g, dst_stride=NG,
                             dst_blocks=3 * NG)
        dbs.append(db)
    return dqkv, jnp.stack(dbs)


def _other_chips(x, y):
    return [(1 - x, y), (x, 1 - y), (1 - x, 1 - y)]


def _shard_region(ref, shape, by_cols, chip, rows=None):
    R, C = shape
    start, size = (0, R) if rows is None else rows
    if by_cols:
        return ref.at[pl.ds(start, size), pl.ds(chip * C, C)]
    return ref.at[pl.ds(chip * R + start, size), :]


def _gather_weights(entries):
    n = len(entries)
    shapes = [e[0].shape[1:] for e in entries]

    def places(ins, outs, sems):
        send_sems, recv_sems, local_sems = sems
        x, y, c = lax.axis_index("x"), lax.axis_index("y"), lax.axis_index("c")

        def landing(f, px, py, pc):
            R = shapes[f][0]
            return _shard_region(outs[f], shapes[f], entries[f][2], 2 * px + py, rows=(pc * (R // 2), R // 2))

        def copy(f, k, block, to, src=None):
            dst = landing(f, *block)
            return pltpu.make_async_remote_copy(
                src_ref=dst if src is None else src, dst_ref=dst,
                send_sem=send_sems.at[6 * f + k], recv_sem=recv_sems.at[6 * f + k],
                device_id=to, device_id_type=MESH)

        def mine(f):
            dst = _shard_region(outs[f], shapes[f], entries[f][2], 2 * x + y)
            return pltpu.make_async_copy(ins[f].at[entries[f][1]], dst, local_sems.at[f])

        def first(f, j):
            R = shapes[f][0]
            src = ins[f].at[entries[f][1], pl.ds(c * (R // 2), R // 2), :]
            return copy(f, j, (x, y, c), (*_other_chips(x, y)[j], c), src=src)

        return x, y, c, copy, mine, first

    def start(ins, outs, sems):
        _, _, _, _, mine, first = places(ins, outs, sems)
        for f in range(n):
            mine(f).start()
        for j in range(3):
            for f in range(n):
                first(f, j).start()

    def mid(ins, outs, sems):
        x, y, c, copy, _, _ = places(ins, outs, sems)
        for j, chip in enumerate(_other_chips(x, y)):
            for f in range(n):
                copy(f, j, (*chip, c), (x, y, c)).wait_recv()
                copy(f, 3 + j, (*chip, c), (x, y, 1 - c)).start()

    def end(ins, outs, sems):
        x, y, c, copy, mine, first = places(ins, outs, sems)
        for j, chip in enumerate(_other_chips(x, y)):
            for f in range(n):
                copy(f, 3 + j, (*chip, 1 - c), (x, y, c)).wait_recv()
        for j, chip in enumerate(_other_chips(x, y)):
            for f in range(n):
                first(f, j).wait_send()
                copy(f, 3 + j, (*chip, c), (x, y, 1 - c)).wait_send()
        for f in range(n):
            mine(f).wait()

    def whole(f):
        R, C = shapes[f]
        return (R, N_CHIPS * C) if entries[f][2] else (N_CHIPS * R, C)

    return _Comm(
        [e[0] for e in entries], [jax.ShapeDtypeStruct(whole(f), BF16) for f in range(n)],
        [pltpu.SemaphoreType.DMA((6 * n,)), pltpu.SemaphoreType.DMA((6 * n,)), pltpu.SemaphoreType.DMA((n,))],
        start, end, mid)


def _scatter_grads(entries):
    n = len(entries)

    def copies(ins, outs, sems):
        send_sems, recv_sems, local_sems = sems
        x, y, c = lax.axis_index("x"), lax.axis_index("y"), lax.axis_index("c")
        me = 2 * x + y

        def piece(f, chip):
            return _shard_region(ins[f], entries[f][1], entries[f][2], chip)

        mine = [pltpu.make_async_copy(piece(f, me), outs[f].at[me], local_sems.at[f]) for f in range(n)]
        sends = [pltpu.make_async_remote_copy(
            src_ref=piece(f, 2 * px + py), dst_ref=outs[f].at[me],
            send_sem=send_sems.at[3 * f + j], recv_sem=recv_sems.at[3 * f + j],
            device_id=(px, py, c), device_id_type=MESH)
            for j, (px, py) in enumerate(_other_chips(x, y)) for f in range(n)]
        return mine, sends

    def start(ins, outs, sems):
        mine, sends = copies(ins, outs, sems)
        for cp in mine + sends:
            cp.start()

    def end(ins, outs, sems):
        mine, sends = copies(ins, outs, sems)
        for cp in sends + mine:
            cp.wait()

    return _Comm(
        [e[0] for e in entries], [jax.ShapeDtypeStruct((N_CHIPS,) + tuple(e[1]), BF16) for e in entries],
        [pltpu.SemaphoreType.DMA((3 * n,)), pltpu.SemaphoreType.DMA((3 * n,)), pltpu.SemaphoreType.DMA((n,))],
        start, end)


def _exchange_sibling(parts):
    n = len(parts)

    def copies(ins, outs, sems):
        send_sems, recv_sems = sems
        sibling = (lax.axis_index("x"), lax.axis_index("y"), 1 - lax.axis_index("c"))
        return [pltpu.make_async_remote_copy(src_ref=ins[i], dst_ref=outs[i], send_sem=send_sems.at[i],
                                             recv_sem=recv_sems.at[i], device_id=sibling, device_id_type=MESH)
                for i in range(n)]

    def start(ins, outs, sems):
        for cp in copies(ins, outs, sems):
            cp.start()

    def end(ins, outs, sems):
        for cp in copies(ins, outs, sems):
            cp.wait()

    return _Comm(parts, [jax.ShapeDtypeStruct(s.shape, s.dtype) for s in parts],
                 [pltpu.SemaphoreType.DMA((n,)), pltpu.SemaphoreType.DMA((n,))], start, end)


def _allgather_small(block):
    m_per, ncol = block.shape

    def places(ins, outs, sems):
        send_sems, recv_sems, local_sem = sems
        x, y, c = lax.axis_index("x"), lax.axis_index("y"), lax.axis_index("c")

        def rows(px, py, pc):
            return outs[0].at[4 * px + 2 * py + pc]

        def copy(k, block_of, to, src=None):
            return pltpu.make_async_remote_copy(
                src_ref=rows(*block_of) if src is None else src, dst_ref=rows(*block_of),
                send_sem=send_sems.at[k], recv_sem=recv_sems.at[k], device_id=to, device_id_type=MESH)

        mine = pltpu.make_async_copy(ins[0], rows(x, y, c), local_sem.at[0])
        first = [copy(0, (x, y, c), (x, y, 1 - c), src=ins[0])]
        first += [copy(1 + j, (x, y, c), (*chip, c), src=ins[0]) for j, chip in enumerate(_other_chips(x, y))]
        passed = [copy(4 + j, (*chip, c), (x, y, 1 - c)) for j, chip in enumerate(_other_chips(x, y))]
        return x, y, c, copy, mine, first, passed

    def start(ins, outs, sems):
        _, _, _, _, mine, first, _ = places(ins, outs, sems)
        for cp in [mine] + first:
            cp.start()

    def mid(ins, outs, sems):
        x, y, c, copy, _, _, passed = places(ins, outs, sems)
        for j, chip in enumerate(_other_chips(x, y)):
            copy(1 + j, (*chip, c), (x, y, c)).wait_recv()
            passed[j].start()

    def end(ins, outs, sems):
        x, y, c, copy, mine, first, passed = places(ins, outs, sems)
        copy(0, (x, y, 1 - c), (x, y, c)).wait_recv()
        for j, chip in enumerate(_other_chips(x, y)):
            copy(4 + j, (*chip, 1 - c), (x, y, c)).wait_recv()
        for cp in first + passed:
            cp.wait_send()
        mine.wait()

    return _Comm([block], [jax.ShapeDtypeStruct((N_DEV, m_per, ncol), block.dtype)],
                 [pltpu.SemaphoreType.DMA((7,)), pltpu.SemaphoreType.DMA((7,)), pltpu.SemaphoreType.DMA((1,))],
                 start, end, mid)


def _join_comms(*progs):
    def split(parts, counts):
        out, pos = [], 0
        for n in counts:
            out.append(parts[pos:pos + n])
            pos += n
        return out

    def phase(which):
        def run(ins, outs, sems):
            args = zip(split(ins, [len(p.inputs) for p in progs]), split(outs, [len(p.out_shapes) for p in progs]),
                       split(sems, [len(p.scratch) for p in progs]))
            for p, (i, o, s) in zip(progs, args):
                fn = getattr(p, which)
                if fn is not None:
                    fn(i, o, s)
        return run

    return _Comm([a for p in progs for a in p.inputs], [s for p in progs for s in p.out_shapes],
                 [s for p in progs for s in p.scratch], phase("start"), phase("end"), phase("mid"))


def _adamw(w, g, m, v):
    m = ADAM_B1 * m + (1.0 - ADAM_B1) * g
    v = ADAM_B2 * v + (1.0 - ADAM_B2) * jnp.square(g)
    m_hat = m / (1.0 - ADAM_B1 ** ADAM_STEP)
    v_hat = v / (1.0 - ADAM_B2 ** ADAM_STEP)
    delta = -ADAM_LR * (m_hat / (jnp.sqrt(v_hat) + ADAM_EPS) + ADAM_WD * w)
    return delta, m, v


def _flat_tile(rows):
    return min(rows, 256)


def _sum_pieces(name, layers):
    L = len(layers)
    P, R, C = layers[0].shape
    tr = _flat_tile(R)

    def body(*refs):
        out_ref = refs[L]
        for l in range(L):
            @pl.when(pl.program_id(0) == l)
            def _(p_ref=refs[l]):
                acc = p_ref[0].astype(F32)
                for j in range(1, P):
                    acc = acc + p_ref[j].astype(F32)
                out_ref[...] = acc

    return pl.pallas_call(
        body, name=name, grid=(L, R // tr),
        in_specs=[pl.BlockSpec((P, tr, C), lambda l, i: (0, i, 0)) for _ in range(L)],
        out_specs=pl.BlockSpec((None, tr, C), lambda l, i: (l, i, 0)),
        out_shape=jax.ShapeDtypeStruct((L, R, C), F32),
        compiler_params=_params(dimension_semantics=("parallel", "parallel")),
    )(*layers)


def _adam_pair(name, w, m, v, part_a, part_b):
    L, R, C = w.shape
    tr = _flat_tile(R)

    def body(w_ref, m_ref, v_ref, a_ref, b_ref, g_ref, d_ref, nm_ref, nv_ref):
        g = a_ref[...] + b_ref[...]
        g_ref[...] = g
        d_ref[...], nm_ref[...], nv_ref[...] = _adamw(w_ref[...], g, m_ref[...], v_ref[...])

    row = pl.BlockSpec((None, tr, C), lambda l, i: (l, i, 0))
    return pl.pallas_call(
        body, name=name, grid=(L, R // tr),
        in_specs=[row] * 5, out_specs=[row] * 4,
        out_shape=[jax.ShapeDtypeStruct((L, R, C), F32)] * 4,
        compiler_params=_params(dimension_semantics=("parallel", "parallel")),
    )(w, m, v, part_a, part_b)


def _adam_small(w, m, v, gathered):
    R, C = w.shape

    def body(w_ref, m_ref, v_ref, p_ref, g_ref, d_ref, nm_ref, nv_ref):
        g = p_ref[0]
        for j in range(1, N_DEV):
            g = g + p_ref[j]
        g_ref[...] = g
        d_ref[...], nm_ref[...], nv_ref[...] = _adamw(w_ref[...], g, m_ref[...], v_ref[...])

    return pl.pallas_call(
        body, name="adam_small",
        out_shape=[jax.ShapeDtypeStruct((R, C), F32)] * 4,
        compiler_params=_params(),
    )(w, m, v, gathered)


SMALL = ("mix_norm_g", "mlp_norm_g", "final_norm_g", "a_ln_g", "a_ln_b", "a_w_s", "a_b_s", "rel_bias")


def _pack_small(arrays, width):
    rows = []
    for a in arrays:
        flat = a.reshape(-1)
        pad = (-flat.shape[0]) % width
        rows.append(jnp.pad(flat, (0, pad)).reshape(-1, width))
    block = jnp.concatenate(rows, axis=0)
    return jnp.pad(block, ((0, (-block.shape[0]) % 8), (0, 0)))


def _unpack_small(block, shapes, width):
    out, row = [], 0
    for shape in shapes:
        size = int(np.prod(shape))
        nrows = -(-size // width)
        out.append(block[row:row + nrows].reshape(-1)[:size].reshape(shape))
        row += nrows
    return out


def kernel(x, mix_norm_g, mlp_norm_g, final_norm_g, a_w_in, a_ln_g, a_ln_b, a_w_s, a_b_s, a_w_out, b_w_qkv, b_w_out, rel_bias, w_up, w_down, loss_target, m_mix_norm_g, m_mlp_norm_g, m_final_norm_g, m_a_w_in, m_a_ln_g, m_a_ln_b, m_a_w_s, m_a_b_s, m_a_w_out, m_b_w_qkv, m_b_w_out, m_rel_bias, m_w_up, m_w_down, v_mix_norm_g, v_mlp_norm_g, v_final_norm_g, v_a_w_in, v_a_ln_g, v_a_ln_b, v_a_w_s, v_a_b_s, v_a_w_out, v_b_w_qkv, v_b_w_out, v_rel_bias, v_w_up, v_w_down):
    T, D = x.shape[1], x.shape[2]
    h0 = x.reshape(T, D)
    target = loss_target.reshape(T, D)
    G = a_w_s.shape[1]

    w_big = [a_w_in, a_w_out, b_w_qkv, b_w_out, w_up, w_down]
    m_big = [m_a_w_in, m_a_w_out, m_b_w_qkv, m_b_w_out, m_w_up, m_w_down]
    v_big = [v_a_w_in, v_a_w_out, v_b_w_qkv, v_b_w_out, v_w_up, v_w_down]
    by_cols = [True, False, True, True, True, False]
    s_in, s_out, s_qkv, s_bo, s_up, s_dn = [w.astype(BF16) for w in w_big]
    (W_in,) = _run_comm("gather_a", _gather_weights([(s_in, 0, True)]))

    tril = jnp.tril(jnp.ones((CHUNK, CHUNK), dtype=bool))
    w_tril = jnp.where(tril[None], a_w_s[0], 0.0).astype(BF16)
    w_tril_t = jnp.swapaxes(w_tril, 1, 2)
    b_rows = jnp.broadcast_to(a_b_s[0][:, :, None], (G, CHUNK, CHUNK))
    buckets = _bucket_maps()
    bias = _bias_build(rel_bias, buckets)

    QKV = s_qkv.shape[2] * N_CHIPS
    TM = 1024
    TK_WGRAD = 2048

    def matmul(name, a, b, mode, out, tm=TM, tn=1024, **kw):
        outs = out if isinstance(out, list) else [out]
        return _mm(name, a, b, mode, tm=tm, tn=tn, tk=a.shape[1], outs=outs, **kw)

    def norm_bwd(layer_gain, h, dres, copies=2):
        return dict(epi=_epi_rms_bwd(copies), extras=(h, dres), vecs=(layer_gain,), col_sums=1)

    def wgrad(name, a, b, tn=1024, comm=None):
        return _mm(name, a, b, "tn", tm=1024, tn=tn, tk=TK_WGRAD, outs=[BF16], comm=comm)

    def scatter(*which):
        return _scatter_grads([(g, w_big[i].shape[1:], by_cols[i]) for g, i in which])

    (a_pre, y0), (W_out, W_up0) = matmul("a_in", h0, W_in, "nn", BF16, norm_gain=mix_norm_g[0:1],
                                         comm=_gather_weights([(s_out, 0, False), (s_up, 0, True)]))
    z = _gate_fwd(a_pre, a_ln_g, a_ln_b, w_tril, b_rows)
    h1 = matmul("a_out", z, W_out, "nn", F32, epi=_epi_residual, extras=(h0,))
    (q1, y1), (W_dn0,) = matmul("mlp_up0", h1, W_up0, "nn", BF16, epi=_epi_relu2, norm_gain=mlp_norm_g[0:1],
                                comm=_gather_weights([(s_dn, 0, False)]))
    h2, (W_qkv, W_bo) = matmul("mlp_down0", q1, W_dn0, "nn", F32, tm=TM // 2, epi=_epi_residual, extras=(h1,),
                               comm=_gather_weights([(s_qkv, 0, True), (s_bo, 0, True)]))
    (qkv, y2), (W_up1,) = matmul("b_qkv", h2, W_qkv, "nn", BF16, tn=QKV // 4, norm_gain=mix_norm_g[1:2],
                                 comm=_gather_weights([(s_up, 1, True)]))
    o, lse = _attention_fwd(qkv, bias)
    h3 = matmul("b_out", o, W_bo, "nn", F32, epi=_epi_residual, extras=(h2,))
    (q3, y3), (W_dn1,) = matmul("mlp_up1", h3, W_up1, "nn", BF16, epi=_epi_relu2, norm_gain=mlp_norm_g[1:2],
                                comm=_gather_weights([(s_dn, 1, False)]))
    dh4, dh4_b, d_final_g, loss_row = matmul("mlp_down1", q3, W_dn1, "nn", [F32, BF16], tm=TM // 2, epi=_epi_loss_head,
                                             extras=(h3, target), vecs=(final_norm_g.reshape(1, D),), col_sums=2)
    loss = lax.psum(loss_row[0, 0], ("x", "y", "c"))

    dp3 = matmul("mlp_down_bwd1", dh4_b, W_dn1, "nt", BF16, epi=_epi_relu2_grad, extras=(q3,))
    g_dn1 = wgrad("mlp_down_wgrad1", q3, dh4_b)
    g_up1, (r_dn1,) = wgrad("mlp_up_wgrad1", y3, dp3, comm=scatter((g_dn1, 5)))
    (dh3, dh3_b, dg_mlp1), (r_up1,) = matmul("mlp_up_bwd1", dp3, W_up1, "nt", [F32, BF16], tm=TM // 2,
                                             comm=scatter((g_up1, 4)), **norm_bwd(mlp_norm_g[1:2], h3, dh4))
    do = matmul("b_out_bwd", dh3_b, W_bo, "nt", BF16)
    g_bo = wgrad("b_out_wgrad", o, dh3_b)
    dqkv, dbias = _attention_bwd(qkv, do, o, lse, bias)
    d_rel_bias = _bias_scatter(dbias, buckets)
    (dh2, dh2_b, dg_mix1), (r_bo,) = matmul("b_qkv_bwd", dqkv, W_qkv, "nt", [F32, BF16], tm=TM // 2,
                                            comm=scatter((g_bo, 3)), **norm_bwd(mix_norm_g[1:2], h2, dh3))
    g_qkv = wgrad("b_qkv_wgrad", y2, dqkv, tn=QKV // 4)
    dp1, (r_qkv,) = matmul("mlp_down_bwd0", dh2_b, W_dn0, "nt", BF16, epi=_epi_relu2_grad, extras=(q1,),
                           comm=scatter((g_qkv, 2)))
    g_up0 = wgrad("mlp_up_wgrad0", y1, dp1)
    g_dn0, (r_up0,) = wgrad("mlp_down_wgrad0", q1, dh2_b, comm=scatter((g_up0, 4)))
    (dh1, dh1_b, dg_mlp0), (r_dn0,) = matmul("mlp_up_bwd0", dp1, W_up0, "nt", [F32, BF16], tm=TM // 2,
                                             comm=scatter((g_dn0, 5)), **norm_bwd(mlp_norm_g[0:1], h1, dh2))
    dz = matmul("a_out_bwd", dh1_b, W_out, "nt", F32)
    g_out = wgrad("a_out_wgrad", z, dh1_b)
    da, d_ln_g, d_ln_b, d_w_s, d_b_s = _gate_bwd(a_pre, dz, a_ln_g, a_ln_b, w_tril, w_tril_t, b_rows)
    g_in, (r_out,) = wgrad("a_in_wgrad", y0, da, comm=scatter((g_out, 1)))
    grad_x, dg_mix0 = matmul("a_in_bwd", da, W_in, "nt", F32, **norm_bwd(mix_norm_g[0:1], h0, dh1, copies=1))

    small_w = [mix_norm_g, mlp_norm_g, final_norm_g, a_ln_g, a_ln_b, a_w_s, a_b_s, rel_bias]
    small_m = [m_mix_norm_g, m_mlp_norm_g, m_final_norm_g, m_a_ln_g, m_a_ln_b, m_a_w_s, m_a_b_s, m_rel_bias]
    small_v = [v_mix_norm_g, v_mlp_norm_g, v_final_norm_g, v_a_ln_g, v_a_ln_b, v_a_w_s, v_a_b_s, v_rel_bias]
    small_g = [jnp.concatenate([dg_mix0, dg_mix1]), jnp.concatenate([dg_mlp0, dg_mlp1]), d_final_g,
               d_ln_g, d_ln_b, d_w_s[None], d_b_s[None, :, :, 0], d_rel_bias]
    width = max(D, 128)
    received = [None, [r_out], [r_qkv], [r_bo], [r_up0, r_up1], [r_dn0, r_dn1]]
    plane = [None] + [_sum_pieces(f"sum_pieces{i}", received[i]) for i in range(1, len(w_big))]
    tail = _run_comm("tail_comm", _join_comms(scatter((g_in, 0)), _exchange_sibling(plane[1:]),
                                              _allgather_small(_pack_small(small_g, width))))
    r_in, other, gathered_small = tail[0], [None] + list(tail[1:len(w_big)]), tail[len(w_big)]
    plane[0] = _sum_pieces("sum_pieces0", [r_in])
    (other[0],) = _run_comm("exchange_a_in", _exchange_sibling([plane[0]]))
    big_out = [_adam_pair(f"adam{i}", w_big[i], m_big[i], v_big[i], plane[i], other[i]) for i in range(len(w_big))]

    def unbig(kind):
        return dict(zip(["a_w_in", "a_w_out", "b_w_qkv", "b_w_out", "w_up", "w_down"], [b[kind] for b in big_out]))

    small_out = _adam_small(_pack_small(small_w, width), _pack_small(small_m, width), _pack_small(small_v, width),
                            gathered_small)
    shapes = [w.shape for w in small_w]

    names = ["mix_norm_g", "mlp_norm_g", "final_norm_g", "a_w_in", "a_ln_g", "a_ln_b", "a_w_s", "a_b_s", "a_w_out",
             "b_w_qkv", "b_w_out", "rel_bias", "w_up", "w_down"]
    results = [loss, grad_x.reshape(x.shape)]
    for kind in range(4):
        table = dict(zip(SMALL, _unpack_small(small_out[kind], shapes, width)))
        table.update(unbig(kind))
        results += [table[n] for n in names]
    return tuple(results)
```

```python
import functools
import math

import numpy as np
import jax
import jax.numpy as jnp
from jax import lax
from jax.experimental import pallas as pl
from jax.experimental.pallas import tpu as pltpu

F32 = jnp.float32
BF16 = jnp.bfloat16
MESH = pl.DeviceIdType.MESH
ANY = pl.BlockSpec(memory_space=pl.ANY)

N_CHIPS = 4
N_DEV = 8
VMEM_LIMIT_BYTES = 56 * 1024 * 1024

EPS = 1e-6
NEG_INF = -1e30
CHUNK = 128
GROUP_DIM = 128
HEAD_DIM = 64
ATT_HEADS = 8
ATT_WIDTH = ATT_HEADS * HEAD_DIM
PAIR = 2 * HEAD_DIM
BLK = 128
DILATIONS = (1, 4, 16)
N_BUCKETS = 32
MAX_EXACT = N_BUCKETS // 2
REL_MAX_DISTANCE = 2048

ADAM_LR = 0.001
ADAM_B1 = 0.9
ADAM_B2 = 0.999
ADAM_EPS = 1e-08
ADAM_WD = 0.01
ADAM_STEP = 10

NN = (((1,), (0,)), ((), ()))
NT = (((1,), (1,)), ((), ()))
TN = (((0,), (0,)), ((), ()))


def _params(**kw):
    return pltpu.CompilerParams(vmem_limit_bytes=VMEM_LIMIT_BYTES, **kw)


def _dot(a, b, dims=NN):
    return lax.dot_general(a, b, dims, preferred_element_type=F32)


def _gelu(x):
    return 0.5 * x * (1.0 + lax.erf(x * math.sqrt(0.5)))


def _gelu_grad(x):
    return 0.5 * (1.0 + lax.erf(x * math.sqrt(0.5))) + x * jnp.exp(-0.5 * x * x) * (1.0 / math.sqrt(2.0 * math.pi))


def _mean(x):
    return jnp.mean(x, axis=-1, keepdims=True)


class _Comm:
    def __init__(self, inputs, out_shapes, scratch, start, end, mid=None):
        self.inputs, self.out_shapes, self.scratch = list(inputs), list(out_shapes), list(scratch)
        self.start, self.mid, self.end = start, mid, end


def _run_comm(name, comm):
    n_in, n_out = len(comm.inputs), len(comm.out_shapes)

    def body(*refs):
        parts = refs[:n_in], refs[n_in:n_in + n_out], refs[n_in + n_out:]
        comm.start(*parts)
        if comm.mid is not None:
            comm.mid(*parts)
        comm.end(*parts)

    return pl.pallas_call(
        body, name=name, in_specs=[ANY] * n_in, out_specs=[ANY] * n_out, out_shape=comm.out_shapes,
        scratch_shapes=comm.scratch, compiler_params=_params(),
    )(*comm.inputs)


def _mm(name, a, b, mode, *, tm, tn, tk, outs, epi=None, extras=(), vecs=(), col_sums=0, norm_gain=None, comm=None):
    if mode == "tn":
        K, M = a.shape
    else:
        M, K = a.shape
    N = b.shape[0] if mode == "nt" else b.shape[1]
    tm, tn, tk = min(tm, M), min(tn, N), min(tk, K)
    assert M % tm == 0 and N % tn == 0 and K % tk == 0, (name, M, N, K, tm, tn, tk)
    nk = K // tk
    grid = (M // tm, N // tn, nk)

    if mode == "tn":
        a_spec = pl.BlockSpec((tk, tm), lambda i, j, k: (k, i))
    else:
        a_spec = pl.BlockSpec((tm, tk), lambda i, j, k: (i, k))
    if mode == "nt":
        b_spec = pl.BlockSpec((tn, tk), lambda i, j, k: (j, k))
    else:
        b_spec = pl.BlockSpec((tk, tn), lambda i, j, k: (k, j))
    tile = pl.BlockSpec((tm, tn), lambda i, j, k: (i, j))
    vec = pl.BlockSpec((1, tn), lambda i, j, k: (0, j))
    normed = norm_gain is not None
    assert not normed or (mode == "nn" and nk == 1)
    assert col_sums == 0 or grid[1] == 1
    out_shapes = [jax.ShapeDtypeStruct((M, N), dtype) for dtype in outs]
    out_specs = [tile for _ in outs]
    if normed:
        out_shapes.append(jax.ShapeDtypeStruct((M, K), BF16))
        out_specs.append(pl.BlockSpec((tm, K), lambda i, j, k: (i, 0)))
    out_shapes += [jax.ShapeDtypeStruct((1, N), F32)] * col_sums
    out_specs += [vec] * col_sums
    extra_specs = [tile for _ in extras] + [vec for _ in vecs]
    if normed:
        extra_specs.append(pl.BlockSpec((1, K), lambda i, j, k: (0, 0)))
    n_extra, n_out = len(extra_specs), len(out_shapes)
    n_tiles = len(outs)
    n_cin = len(comm.inputs) if comm else 0
    n_cout = len(comm.out_shapes) if comm else 0
    dims = {"nn": NN, "nt": NT, "tn": TN}[mode]
    steps = grid[0] * grid[1] * grid[2]

    def body(*refs):
        a_ref, b_ref = refs[0], refs[1]
        pos = 2
        extra_refs = refs[pos:pos + n_extra]
        pos += n_extra
        comm_in = refs[pos:pos + n_cin]
        pos += n_cin
        out_refs = refs[pos:pos + n_out]
        pos += n_out
        comm_out = refs[pos:pos + n_cout]
        pos += n_cout
        acc_ref = refs[pos] if nk > 1 else None
        pos += nk > 1
        y_ref = refs[pos] if normed else None
        comm_sems = refs[pos + normed:]
        k = pl.program_id(2)
        step = (pl.program_id(0) * grid[1] + pl.program_id(1)) * nk + k

        if comm is not None:
            @pl.when(step == 0)
            def _():
                comm.start(comm_in, comm_out, comm_sems)

        if normed:
            @pl.when(pl.program_id(1) == 0)
            def _():
                hv = a_ref[...]
                y = (hv * lax.rsqrt(_mean(hv * hv) + EPS) * extra_refs[-1][...]).astype(BF16)
                y_ref[...] = y
                out_refs[n_tiles][...] = y

            lhs = y_ref[...]
        else:
            lhs = a_ref[...].astype(BF16)
        part = _dot(lhs, b_ref[...].astype(BF16), dims)

        def finish(acc):
            epi_args = [e[...] for e in extra_refs[:n_extra - normed]]
            res = epi(acc, *epi_args) if epi is not None else (acc,) * n_tiles
            for o, r in zip(out_refs[:n_tiles], res[:n_tiles]):
                o[...] = r.astype(o.dtype)
            if col_sums:
                sums = out_refs[n_out - col_sums:]

                @pl.when(pl.program_id(0) == 0)
                def _():
                    for o in sums:
                        o[...] = jnp.zeros_like(o)

                for o, r in zip(sums, res[n_tiles:]):
                    o[...] += r

        if nk == 1:
            finish(part)
        else:
            @pl.when(k == 0)
            def _():
                acc_ref[...] = part

            @pl.when(k > 0)
            def _():
                acc_ref[...] += part

            @pl.when(k == nk - 1)
            def _():
                finish(acc_ref[...])

        if comm is not None:
            if comm.mid is not None:
                @pl.when(step == (3 * steps) // 4)
                def _():
                    comm.mid(comm_in, comm_out, comm_sems)

            @pl.when(step == steps - 1)
            def _():
                comm.end(comm_in, comm_out, comm_sems)

    sequential = comm is not None or normed or col_sums > 0
    order = ("arbitrary",) * 3 if sequential else ("parallel", "parallel", "arbitrary")
    scratch = [pltpu.VMEM((tm, tn), F32)] if nk > 1 else []
    if normed:
        scratch.append(pltpu.VMEM((tm, K), BF16))
    res = pl.pallas_call(
        body, name=name, grid=grid,
        in_specs=[a_spec, b_spec] + extra_specs + [ANY] * n_cin,
        out_specs=out_specs + [ANY] * n_cout,
        out_shape=out_shapes + (comm.out_shapes if comm else []),
        scratch_shapes=scratch + (comm.scratch if comm else []),
        compiler_params=_params(dimension_semantics=order),
    )(a, b, *extras, *vecs, *([norm_gain] if normed else []), *(comm.inputs if comm else []))
    mm_out = res[0] if n_out == 1 else list(res[:n_out])
    return (mm_out, list(res[n_out:])) if comm else mm_out


def _epi_residual(acc, res):
    return (res + acc,)


def _epi_relu2(acc):
    return (jnp.square(jnp.maximum(acc, 0.0)),)


def _epi_rms_bwd(copies):
    def epi(acc, h, dres, g):
        r = lax.rsqrt(_mean(h * h) + EPS)
        hn = h * r
        dyg = acc * g
        dh = dres + r * (dyg - hn * _mean(dyg * hn))
        return (dh,) * copies + (jnp.sum(acc * hn, axis=0, keepdims=True),)
    return epi


def _epi_loss_head(acc, res, target, g):
    h = res + acc
    r = lax.rsqrt(_mean(h * h) + EPS)
    hn = h * r
    diff = hn * g - target
    loss = 0.5 * jnp.sum(_mean(diff * diff))
    dy = diff * (1.0 / h.shape[-1])
    dyg = dy * g
    dh = r * (dyg - hn * _mean(dyg * hn))
    return dh, dh, jnp.sum(dy * hn, axis=0, keepdims=True), jnp.full((1, h.shape[-1]), loss, F32)


def _epi_relu2_grad(acc, q):
    qf = q.astype(F32)
    return (acc * jnp.where(qf > 0.0, (2.0 * qf) * lax.rsqrt(qf), 0.0),)


def _row_tile(T):
    return min(T, 512)


def _gate_tile(T):
    return min(T, 256)


def _gate_fwd(a, ln_g, ln_b, w_tril, b_rows):
    T, W2 = a.shape
    W = W2 // 2
    G = W // GROUP_DIM
    tr = _gate_tile(T)

    def body(a_ref, lng_ref, lnb_ref, w_ref, b_ref, z_ref):
        u = _gelu(a_ref[:, :W].astype(F32))
        vg = _gelu(a_ref[:, W:].astype(F32))
        xc = vg - _mean(vg)
        vn = xc * lax.rsqrt(_mean(xc * xc) + EPS)
        vl = (vn * lng_ref[...] + lnb_ref[...]).astype(BF16)
        for n in range(tr // CHUNK):
            rows = slice(n * CHUNK, (n + 1) * CHUNK)
            for g in range(G):
                cols = slice(g * GROUP_DIM, (g + 1) * GROUP_DIM)
                gate = _dot(w_ref[g], vl[rows, cols]) + b_ref[g]
                z_ref[rows, cols] = (u[rows, cols] * gate).astype(BF16)

    vec = pl.BlockSpec((1, W), lambda i: (0, 0))
    grp = pl.BlockSpec((G, CHUNK, CHUNK), lambda i: (0, 0, 0))
    return pl.pallas_call(
        body, name="gate_fwd", grid=(T // tr,),
        in_specs=[pl.BlockSpec((tr, W2), lambda i: (i, 0)), vec, vec, grp, grp],
        out_specs=pl.BlockSpec((tr, W), lambda i: (i, 0)),
        out_shape=jax.ShapeDtypeStruct((T, W), BF16),
        compiler_params=_params(dimension_semantics=("parallel",)),
    )(a, ln_g, ln_b, w_tril, b_rows)


def _gate_bwd(a, dz, ln_g, ln_b, w_tril, w_tril_t, b_rows):
    T, W2 = a.shape
    W = W2 // 2
    G = W // GROUP_DIM
    tr = _gate_tile(T)
    steps = T // tr

    def body(a_ref, dz_ref, lng_ref, lnb_ref, w_ref, wt_ref, b_ref, da_ref, dlng_ref, dlnb_ref, dw_ref, dbs_ref, dvl_ref):
        step = pl.program_id(0)

        @pl.when(step == 0)
        def _():
            dlng_ref[...] = jnp.zeros_like(dlng_ref)
            dlnb_ref[...] = jnp.zeros_like(dlnb_ref)
            dw_ref[...] = jnp.zeros_like(dw_ref)
            dbs_ref[...] = jnp.zeros_like(dbs_ref)

        au = a_ref[:, :W].astype(F32)
        av = a_ref[:, W:].astype(F32)
        u = _gelu(au)
        vg = _gelu(av)
        xc = vg - _mean(vg)
        rstd = lax.rsqrt(_mean(xc * xc) + EPS)
        vn = xc * rstd
        lng = lng_ref[...]
        vl = (vn * lng + lnb_ref[...]).astype(BF16)
        du_scale = dz_ref[...] * _gelu_grad(au)
        dgate_all = dz_ref[...] * u
        for n in range(tr // CHUNK):
            rows = slice(n * CHUNK, (n + 1) * CHUNK)
            for g in range(G):
                cols = slice(g * GROUP_DIM, (g + 1) * GROUP_DIM)
                vlg = vl[rows, cols]
                gate = _dot(w_ref[g], vlg) + b_ref[g]
                da_ref[rows, cols] = (du_scale[rows, cols] * gate).astype(BF16)
                dgate = dgate_all[rows, cols]
                dbs_ref[g] += dgate
                dgate_b = dgate.astype(BF16)
                dw_ref[g] += _dot(dgate_b, vlg, NT)
                dvl_ref[rows, cols] = _dot(wt_ref[g], dgate_b)
        dvl = dvl_ref[...]
        dlnb_ref[...] += jnp.sum(dvl, axis=0, keepdims=True)
        dlng_ref[...] += jnp.sum(dvl * vn, axis=0, keepdims=True)
        dvn = dvl * lng
        dvg = rstd * (dvn - _mean(dvn) - vn * _mean(dvn * vn))
        da_ref[:, W:] = (dvg * _gelu_grad(av)).astype(BF16)

        @pl.when(step == steps - 1)
        def _():
            t_idx = lax.broadcasted_iota(jnp.int32, (CHUNK, CHUNK), 0)
            s_idx = lax.broadcasted_iota(jnp.int32, (CHUNK, CHUNK), 1)
            for g in range(G):
                dw_ref[g] = jnp.where(s_idx <= t_idx, dw_ref[g], 0.0)
                dbs_ref[g] = jnp.broadcast_to(jnp.sum(dbs_ref[g], axis=-1, keepdims=True), (CHUNK, CHUNK))

    vec = pl.BlockSpec((1, W), lambda i: (0, 0))
    grp = pl.BlockSpec((G, CHUNK, CHUNK), lambda i: (0, 0, 0))
    return pl.pallas_call(
        body, name="gate_bwd", grid=(steps,),
        in_specs=[pl.BlockSpec((tr, W2), lambda i: (i, 0)), pl.BlockSpec((tr, W), lambda i: (i, 0)),
                  vec, vec, grp, grp, grp],
        out_specs=[pl.BlockSpec((tr, W2), lambda i: (i, 0)), vec, vec, grp, grp],
        out_shape=[jax.ShapeDtypeStruct((T, W2), BF16), jax.ShapeDtypeStruct((1, W), F32),
                   jax.ShapeDtypeStruct((1, W), F32), jax.ShapeDtypeStruct((G, CHUNK, CHUNK), F32),
                   jax.ShapeDtypeStruct((G, CHUNK, CHUNK), F32)],
        scratch_shapes=[pltpu.VMEM((tr, W), F32)],
        compiler_params=_params(dimension_semantics=("arbitrary",)),
    )(a, dz, ln_g, ln_b, w_tril, w_tril_t, b_rows)


def _bucket_map(dilation):
    rel = BLK + np.arange(BLK)[:, None] - np.arange(2 * BLK)[None, :]
    dist = np.clip(rel, 0, BLK) * dilation
    nf = np.maximum(dist, 1).astype(np.float32)
    large = MAX_EXACT + (np.log(nf / np.float32(MAX_EXACT)) / np.float32(math.log(REL_MAX_DISTANCE / MAX_EXACT))
                         * np.float32(N_BUCKETS - MAX_EXACT)).astype(np.int32)
    large = np.minimum(large, N_BUCKETS - 1)
    return np.where(dist < MAX_EXACT, dist, large).astype(np.int32)


def _bucket_maps():
    return jnp.asarray(np.stack([_bucket_map(d) for d in DILATIONS]))


def _bias_build(rel_bias, buckets):
    NG = len(DILATIONS)

    def body(table_ref, bucket_ref, out_ref):
        for g in range(NG):
            bk = bucket_ref[g]
            for h in range(ATT_HEADS):
                out_ref[0, g, h] = jnp.zeros((BLK, 2 * BLK), F32)
            for b in range(N_BUCKETS):
                hit = bk == b
                for h in range(ATT_HEADS):
                    out_ref[0, g, h] = jnp.where(hit, table_ref[b, g * ATT_HEADS + h], out_ref[0, g, h])
            for h in range(ATT_HEADS):
                for first in range(2):
                    out_ref[first, g, h] = jnp.where(_window_mask(first), out_ref[0, g, h], NEG_INF)

    return pl.pallas_call(
        body, name="bias_build",
        in_specs=[pl.BlockSpec(memory_space=pltpu.SMEM), pl.BlockSpec(memory_space=pltpu.VMEM)],
        out_specs=pl.BlockSpec(memory_space=pltpu.VMEM),
        out_shape=jax.ShapeDtypeStruct((2, NG, ATT_HEADS, BLK, 2 * BLK), F32),
        compiler_params=_params(),
    )(rel_bias, buckets)


def _bias_scatter(dbias, buckets):
    NG = len(DILATIONS)

    def body(dbias_ref, bucket_ref, out_ref):
        for g in range(NG):
            bk = bucket_ref[g]
            for b in range(N_BUCKETS):
                hit = bk == b
                for h in range(ATT_HEADS):
                    out_ref[b, g * ATT_HEADS + h] = jnp.sum(jnp.where(hit, dbias_ref[g, h], 0.0))

    return pl.pallas_call(
        body, name="bias_scatter",
        in_specs=[pl.BlockSpec(memory_space=pltpu.VMEM), pl.BlockSpec(memory_space=pltpu.VMEM)],
        out_specs=pl.BlockSpec(memory_space=pltpu.SMEM),
        out_shape=jax.ShapeDtypeStruct((N_BUCKETS, NG * ATT_HEADS), F32),
        compiler_params=_params(),
    )(dbias, buckets)


def _window_mask(first):
    qi = lax.broadcasted_iota(jnp.int32, (BLK, 2 * BLK), 0)
    kj = lax.broadcasted_iota(jnp.int32, (BLK, 2 * BLK), 1)
    rel = BLK + qi - kj
    return (rel >= 0) & (rel <= BLK) & (kj >= BLK * first)


def _head_lanes(hh):
    lane = lax.broadcasted_iota(jnp.int32, (1, PAIR), 1)
    return (lane >= hh * HEAD_DIM) & (lane < (hh + 1) * HEAD_DIM)


ATT_STEP_BLOCKS = 8


def _attn_steps(stride):
    per_step = math.gcd(stride, ATT_STEP_BLOCKS)
    return per_step, stride // per_step


def _attn_fwd(name, g, qkv, qc, kc, vc, bias, stride):
    T = qkv.shape[0]
    per_step, lag = _attn_steps(stride)
    rows = per_step * BLK
    scale = HEAD_DIM ** -0.5

    def body(q_ref, kp_ref, kc_ref, vp_ref, vc_ref, bias_ref, out_ref):
        first = (pl.program_id(0) < lag).astype(jnp.int32)
        low = _head_lanes(0)

        def block(j, carry):
            at = pl.ds(pl.multiple_of(j * BLK, BLK), BLK)
            for hp in range(ATT_HEADS // 2):
                cols = slice(hp * PAIR, (hp + 1) * PAIR)
                qp = q_ref[at, cols]
                kk = jnp.concatenate([kp_ref[at, cols], kc_ref[at, cols]], axis=0)
                vv = jnp.concatenate([vp_ref[at, cols], vc_ref[at, cols]], axis=0)
                o_h, lse_h = [], []
                for hh in range(2):
                    qm = jnp.where(_head_lanes(hh), qp, jnp.zeros_like(qp))
                    s = _dot(qm, kk, NT) * scale
                    logits = s + bias_ref[first, 2 * hp + hh]
                    m = jnp.max(logits, axis=-1, keepdims=True)
                    p = jnp.exp(logits - m)
                    den = jnp.sum(p, axis=-1, keepdims=True)
                    o_h.append(_dot(p.astype(BF16), vv) / den)
                    lse_h.append(m + jnp.log(den))
                out_ref[at, cols] = jnp.where(low, o_h[0], o_h[1])
                out_ref[at, slice(ATT_WIDTH + hp * PAIR, ATT_WIDTH + (hp + 1) * PAIR)] = (
                    jnp.where(low, lse_h[0], lse_h[1]))
            return carry

        lax.fori_loop(0, per_step, block, 0)

    def cur(c):
        return pl.BlockSpec((rows, ATT_WIDTH), lambda s: (s, c))

    def prev(c):
        return pl.BlockSpec((rows, ATT_WIDTH), lambda s: (jnp.maximum(s - lag, 0), c))

    return pl.pallas_call(
        body, name=name, grid=(T // rows,),
        in_specs=[cur(qc), prev(kc), cur(kc), prev(vc), cur(vc),
                  pl.BlockSpec((2, None, ATT_HEADS, BLK, 2 * BLK), lambda s: (0, g, 0, 0, 0))],
        out_specs=pl.BlockSpec((rows, 2 * ATT_WIDTH), lambda s: (s, 0)),
        out_shape=jax.ShapeDtypeStruct((T, 2 * ATT_WIDTH), F32),
        compiler_params=_params(dimension_semantics=("parallel",)),
    )(qkv, qkv, qkv, qkv, qkv, bias)


def _attn_merge(parts):
    T = parts[0].shape[0]
    tr = _row_tile(T)
    n = len(parts)

    def body(*refs):
        o_refs, l_refs = refs[:n], refs[n:2 * n]
        o_ref, lse_ref = refs[2 * n], refs[2 * n + 1]
        ls = [r[...] for r in l_refs]
        m = functools.reduce(jnp.maximum, ls)
        es = [jnp.exp(l - m) for l in ls]
        tot = functools.reduce(lambda x, y: x + y, es)
        acc = functools.reduce(lambda x, y: x + y, [e * r[...] for e, r in zip(es, o_refs)])
        o_ref[...] = (acc / tot).astype(BF16)
        lse_ref[...] = m + jnp.log(tot)

    row = pl.BlockSpec((tr, ATT_WIDTH), lambda i: (i, 0))
    row_lse = pl.BlockSpec((tr, ATT_WIDTH), lambda i: (i, 1))
    return pl.pallas_call(
        body, name="attn_merge", grid=(T // tr,),
        in_specs=[row] * n + [row_lse] * n, out_specs=[row, row],
        out_shape=[jax.ShapeDtypeStruct((T, ATT_WIDTH), BF16), jax.ShapeDtypeStruct((T, ATT_WIDTH), F32)],
        compiler_params=_params(dimension_semantics=("parallel",)),
    )(*parts, *parts)


def _attn_bwd(name, g, qkv, qc, kc, vc, do, o, lse, bias, stride):
    T = qkv.shape[0]
    per_step, lag = _attn_steps(stride)
    rows = per_step * BLK
    steps = T // rows
    scale = HEAD_DIM ** -0.5

    def body(q_ref, kp_ref, kc_ref, vp_ref, vc_ref, do_ref, o_ref, lse_ref, bias_ref,
             dq_ref, dkv_ref, db_ref, carry_k, carry_v):
        step = pl.program_id(0)
        slot0 = (step % lag) * per_step

        @pl.when(step == 0)
        def _():
            db_ref[...] = jnp.zeros_like(db_ref)
            carry_k[...] = jnp.zeros_like(carry_k)
            carry_v[...] = jnp.zeros_like(carry_v)

        @pl.when(step >= steps)
        def _():
            def flush(j, carry):
                at = pl.ds(pl.multiple_of(j * BLK, BLK), BLK)
                dkv_ref[at, :ATT_WIDTH] = carry_k[slot0 + j].astype(BF16)
                dkv_ref[at, ATT_WIDTH:] = carry_v[slot0 + j].astype(BF16)
                return carry

            lax.fori_loop(0, per_step, flush, 0)

        @pl.when(step < steps)
        def _():
            first = (step < lag).astype(jnp.int32)

            def block(j, carry):
                at = pl.ds(pl.multiple_of(j * BLK, BLK), BLK)
                ck_ref = carry_k.at[slot0 + j]
                cv_ref = carry_v.at[slot0 + j]
                for hp in range(ATT_HEADS // 2):
                    cols = slice(hp * PAIR, (hp + 1) * PAIR)
                    qp = q_ref[at, cols]
                    kk = jnp.concatenate([kp_ref[at, cols], kc_ref[at, cols]], axis=0)
                    vv = jnp.concatenate([vp_ref[at, cols], vc_ref[at, cols]], axis=0)
                    dop = do_ref[at, cols]
                    lsep = lse_ref[at, cols]
                    prod = dop.astype(F32) * o_ref[at, cols].astype(F32)
                    dq = jnp.zeros((BLK, PAIR), F32)
                    dk = jnp.zeros((2 * BLK, PAIR), F32)
                    dv = jnp.zeros((2 * BLK, PAIR), F32)
                    for hh in range(2):
                        lanes = _head_lanes(hh)
                        qm = jnp.where(lanes, qp, jnp.zeros_like(qp))
                        dom = jnp.where(lanes, dop, jnp.zeros_like(dop))
                        km = jnp.where(lanes, kk, jnp.zeros_like(kk))
                        delta = jnp.sum(jnp.where(lanes, prod, 0.0), axis=-1, keepdims=True)
                        lse_h = jnp.max(jnp.where(lanes, lsep, NEG_INF), axis=-1, keepdims=True)
                        s = _dot(qm, kk, NT) * scale
                        logits = s + bias_ref[first, 2 * hp + hh]
                        p = jnp.exp(logits - lse_h)
                        dv += _dot(p.astype(BF16), dom, TN)
                        ds = p * (_dot(dom, vv, NT) - delta)
                        db_ref[2 * hp + hh] += ds
                        dss = (ds * scale).astype(BF16)
                        dq += _dot(dss, km)
                        dk += _dot(dss, qm, TN)
                    dq_ref[at, cols] = dq.astype(BF16)
                    dkv_ref[at, cols] = (ck_ref[:, cols] + dk[:BLK]).astype(BF16)
                    dkv_ref[at, slice(ATT_WIDTH + hp * PAIR, ATT_WIDTH + (hp + 1) * PAIR)] = (
                        cv_ref[:, cols] + dv[:BLK]).astype(BF16)
                    ck_ref[:, cols] = dk[BLK:]
                    cv_ref[:, cols] = dv[BLK:]
                return carry

            lax.fori_loop(0, per_step, block, 0)

    last = steps - 1

    def cur(c):
        return pl.BlockSpec((rows, ATT_WIDTH), lambda s: (jnp.minimum(s, last), c))

    def prev(c):
        return pl.BlockSpec((rows, ATT_WIDTH), lambda s: (jnp.clip(s - lag, 0, last), c))

    dbias_shape = (ATT_HEADS, BLK, 2 * BLK)
    return pl.pallas_call(
        body, name=name, grid=(steps + lag,),
        in_specs=[cur(qc), prev(kc), cur(kc), prev(vc), cur(vc), cur(0), cur(0), cur(0),
                  pl.BlockSpec((2, None, ATT_HEADS, BLK, 2 * BLK), lambda s: (0, g, 0, 0, 0))],
        out_specs=[cur(0), pl.BlockSpec((rows, 2 * ATT_WIDTH), lambda s: (jnp.clip(s - lag, 0, last), 0)),
                   pl.BlockSpec(dbias_shape, lambda s: (0, 0, 0))],
        out_shape=[jax.ShapeDtypeStruct((T, ATT_WIDTH), BF16), jax.ShapeDtypeStruct((T, 2 * ATT_WIDTH), BF16),
                   jax.ShapeDtypeStruct(dbias_shape, F32)],
        scratch_shapes=[pltpu.VMEM((stride, BLK, ATT_WIDTH), F32), pltpu.VMEM((stride, BLK, ATT_WIDTH), F32)],
        compiler_params=_params(dimension_semantics=("arbitrary",)),
    )(qkv, qkv, qkv, qkv, qkv, do, o, lse, bias)


REORDER_TILE = 256
REORDER_ROWS = 2048


def _reorder_matrix(d, inverse):
    per = REORDER_TILE // d
    p = np.zeros((REORDER_TILE, REORDER_TILE), np.float32)
    for src in range(REORDER_TILE):
        i, r = divmod(src, d)
        p[r * per + i, src] = 1.0
    return jnp.asarray(p.T if inverse else p, dtype=BF16)


def _reorder_rows(name, src, d, inverse, *, src_col=0, col_stride=1, ncols=1, dst=None, dst_col=0, dst_stride=1,
                  dst_blocks=None):
    T = src.shape[0]
    dtype = src.dtype
    span = BLK * d
    rows = max(span, min(T, REORDER_ROWS))
    per = REORDER_TILE // d
    tiles = span // REORDER_TILE
    dst_blocks = ncols if dst_blocks is None else dst_blocks

    def apply(p, x):
        if dtype == BF16:
            return _dot(p, x).astype(BF16)
        hi = x.astype(BF16)
        rest = x - hi.astype(F32)
        mid = rest.astype(BF16)
        low = (rest - mid.astype(F32)).astype(BF16)
        return _dot(p, hi) + _dot(p, mid) + _dot(p, low)

    def body(*refs):
        p_ref, x_ref, o_ref = refs[0], refs[1], refs[-1]
        if d == 1:
            o_ref[...] = x_ref[...]
            return
        for s in range(rows // span):
            for t in range(tiles):
                base = s * span
                tile_rows = slice(base + t * REORDER_TILE, base + (t + 1) * REORDER_TILE)
                chunk = lambda r: slice(base + r * BLK + t * per, base + r * BLK + (t + 1) * per)
                if inverse:
                    gathered = jnp.concatenate([x_ref[chunk(r), :] for r in range(d)], axis=0)
                    o_ref[tile_rows, :] = apply(p_ref[...], gathered)
                else:
                    y = apply(p_ref[...], x_ref[tile_rows, :])
                    for r in range(d):
                        o_ref[chunk(r), :] = y[r * per:(r + 1) * per]

    in_specs = [pl.BlockSpec((REORDER_TILE, REORDER_TILE), lambda w, k: (0, 0)),
                pl.BlockSpec((rows, ATT_WIDTH), lambda w, k: (w, src_col + col_stride * k))]
    operands = [_reorder_matrix(max(d, 2), inverse), src]
    aliases = {}
    if dst is not None:
        in_specs.append(ANY)
        operands.append(dst)
        aliases = {2: 0}
    return pl.pallas_call(
        body, name=name, grid=(T // rows, ncols), in_specs=in_specs,
        out_specs=pl.BlockSpec((rows, ATT_WIDTH), lambda w, k: (w, dst_col + dst_stride * k)),
        out_shape=jax.ShapeDtypeStruct((T, dst_blocks * ATT_WIDTH), dtype),
        input_output_aliases=aliases,
        compiler_params=_params(dimension_semantics=("parallel", "parallel")),
    )(*operands)


def _group_qkv(qkv, g, d):
    NG = len(DILATIONS)
    if d == 1:
        return qkv, (g, NG + g, 2 * NG + g)
    return _reorder_rows(f"qkv_to_residues{g}", qkv, d, False, src_col=g, col_stride=NG, ncols=3), (0, 1, 2)


def _attention_fwd(qkv, bias):
    T = qkv.shape[0]
    parts = []
    for g, d in enumerate(DILATIONS):
        src, (qc, kc, vc) = _group_qkv(qkv, g, d)
        part = _attn_fwd(f"attn_fwd_{g}", g, src, qc, kc, vc, bias, d)
        parts.append(part if d == 1 else _reorder_rows(f"out_to_positions{g}", part, d, True, ncols=2))
    return _attn_merge(parts)


def _attention_bwd(qkv, do, o, lse, bias):
    T = qkv.shape[0]
    NG = len(DILATIONS)
    dqkv, dbs = None, []
    for g, d in enumerate(DILATIONS):
        src, (qc, kc, vc) = _group_qkv(qkv, g, d)
        do_g, o_g, lse_g = do, o, lse
        if d > 1:
            do_g = _reorder_rows(f"do_to_residues{g}", do, d, False)
            o_g = _reorder_rows(f"o_to_residues{g}", o, d, False)
            lse_g = _reorder_rows(f"lse_to_residues{g}", lse, d, False)
        dq, dkv, db = _attn_bwd(f"attn_bwd_{g}", g, src, qc, kc, vc, do_g, o_g, lse_g, bias, d)
        dqkv = _reorder_rows(f"dq_to_positions{g}", dq, d, True, dst=dqkv, dst_col=g, dst_blocks=3 * NG)
        dqkv = _reorder_rows(f"dkv_to_positions{g}", dkv, d, True, ncols=2, dst=dqkv, dst_col=NG + g, dst_stride=NG,
                             dst_blocks=3 * NG)
        dbs.append(db)
    return dqkv, jnp.stack(dbs)


def _other_chips(x, y):
    return [(1 - x, y), (x, 1 - y), (1 - x, 1 - y)]


def _shard_region(ref, shape, by_cols, chip, rows=None):
    R, C = shape
    start, size = (0, R) if rows is None else rows
    if by_cols:
        return ref.at[pl.ds(start, size), pl.ds(chip * C, C)]
    return ref.at[pl.ds(chip * R + start, size), :]


def _gather_weights(entries):
    n = len(entries)
    shapes = [e[0].shape[1:] for e in entries]

    def places(ins, outs, sems):
        send_sems, recv_sems, local_sems = sems
        x, y, c = lax.axis_index("x"), lax.axis_index("y"), lax.axis_index("c")

        def landing(f, px, py, pc):
            R = shapes[f][0]
            return _shard_region(outs[f], shapes[f], entries[f][2], 2 * px + py, rows=(pc * (R // 2), R // 2))

        def copy(f, k, block, to, src=None):
            dst = landing(f, *block)
            return pltpu.make_async_remote_copy(
                src_ref=dst if src is None else src, dst_ref=dst,
                send_sem=send_sems.at[6 * f + k], recv_sem=recv_sems.at[6 * f + k],
                device_id=to, device_id_type=MESH)

        def mine(f):
            dst = _shard_region(outs[f], shapes[f], entries[f][2], 2 * x + y)
            return pltpu.make_async_copy(ins[f].at[entries[f][1]], dst, local_sems.at[f])

        def first(f, j):
            R = shapes[f][0]
            src = ins[f].at[entries[f][1], pl.ds(c * (R // 2), R // 2), :]
            return copy(f, j, (x, y, c), (*_other_chips(x, y)[j], c), src=src)

        return x, y, c, copy, mine, first

    def start(ins, outs, sems):
        _, _, _, _, mine, first = places(ins, outs, sems)
        for f in range(n):
            mine(f).start()
        for j in range(3):
            for f in range(n):
                first(f, j).start()

    def mid(ins, outs, sems):
        x, y, c, copy, _, _ = places(ins, outs, sems)
        for j, chip in enumerate(_other_chips(x, y)):
            for f in range(n):
                copy(f, j, (*chip, c), (x, y, c)).wait_recv()
                copy(f, 3 + j, (*chip, c), (x, y, 1 - c)).start()

    def end(ins, outs, sems):
        x, y, c, copy, mine, first = places(ins, outs, sems)
        for j, chip in enumerate(_other_chips(x, y)):
            for f in range(n):
                copy(f, 3 + j, (*chip, 1 - c), (x, y, c)).wait_recv()
        for j, chip in enumerate(_other_chips(x, y)):
            for f in range(n):
                first(f, j).wait_send()
                copy(f, 3 + j, (*chip, c), (x, y, 1 - c)).wait_send()
        for f in range(n):
            mine(f).wait()

    def whole(f):
        R, C = shapes[f]
        return (R, N_CHIPS * C) if entries[f][2] else (N_CHIPS * R, C)

    return _Comm(
        [e[0] for e in entries], [jax.ShapeDtypeStruct(whole(f), BF16) for f in range(n)],
        [pltpu.SemaphoreType.DMA((6 * n,)), pltpu.SemaphoreType.DMA((6 * n,)), pltpu.SemaphoreType.DMA((n,))],
        start, end, mid)


def _scatter_grads(entries):
    n = len(entries)

    def copies(ins, outs, sems):
        send_sems, recv_sems, local_sems = sems
        x, y, c = lax.axis_index("x"), lax.axis_index("y"), lax.axis_index("c")
        me = 2 * x + y

        def piece(f, chip):
            return _shard_region(ins[f], entries[f][1], entries[f][2], chip)

        mine = [pltpu.make_async_copy(piece(f, me), outs[f].at[me], local_sems.at[f]) for f in range(n)]
        sends = [pltpu.make_async_remote_copy(
            src_ref=piece(f, 2 * px + py), dst_ref=outs[f].at[me],
            send_sem=send_sems.at[3 * f + j], recv_sem=recv_sems.at[3 * f + j],
            device_id=(px, py, c), device_id_type=MESH)
            for j, (px, py) in enumerate(_other_chips(x, y)) for f in range(n)]
        return mine, sends

    def start(ins, outs, sems):
        mine, sends = copies(ins, outs, sems)
        for cp in mine + sends:
            cp.start()

    def end(ins, outs, sems):
        mine, sends = copies(ins, outs, sems)
        for cp in sends + mine:
            cp.wait()

    return _Comm(
        [e[0] for e in entries], [jax.ShapeDtypeStruct((N_CHIPS,) + tuple(e[1]), BF16) for e in entries],
        [pltpu.SemaphoreType.DMA((3 * n,)), pltpu.SemaphoreType.DMA((3 * n,)), pltpu.SemaphoreType.DMA((n,))],
        start, end)


def _exchange_sibling(parts):
    n = len(parts)

    def copies(ins, outs, sems):
        send_sems, recv_sems = sems
        sibling = (lax.axis_index("x"), lax.axis_index("y"), 1 - lax.axis_index("c"))
        return [pltpu.make_async_remote_copy(src_ref=ins[i], dst_ref=outs[i], send_sem=send_sems.at[i],
                                             recv_sem=recv_sems.at[i], device_id=sibling, device_id_type=MESH)
                for i in range(n)]

    def start(ins, outs, sems):
        for cp in copies(ins, outs, sems):
            cp.start()

    def end(ins, outs, sems):
        for cp in copies(ins, outs, sems):
            cp.wait()

    return _Comm(parts, [jax.ShapeDtypeStruct(s.shape, s.dtype) for s in parts],
                 [pltpu.SemaphoreType.DMA((n,)), pltpu.SemaphoreType.DMA((n,))], start, end)


def _allgather_small(block):
    m_per, ncol = block.shape

    def places(ins, outs, sems):
        send_sems, recv_sems, local_sem = sems
        x, y, c = lax.axis_index("x"), lax.axis_index("y"), lax.axis_index("c")

        def rows(px, py, pc):
            return outs[0].at[4 * px + 2 * py + pc]

        def copy(k, block_of, to, src=None):
            return pltpu.make_async_remote_copy(
                src_ref=rows(*block_of) if src is None else src, dst_ref=rows(*block_of),
                send_sem=send_sems.at[k], recv_sem=recv_sems.at[k], device_id=to, device_id_type=MESH)

        mine = pltpu.make_async_copy(ins[0], rows(x, y, c), local_sem.at[0])
        first = [copy(0, (x, y, c), (x, y, 1 - c), src=ins[0])]
        first += [copy(1 + j, (x, y, c), (*chip, c), src=ins[0]) for j, chip in enumerate(_other_chips(x, y))]
        passed = [copy(4 + j, (*chip, c), (x, y, 1 - c)) for j, chip in enumerate(_other_chips(x, y))]
        return x, y, c, copy, mine, first, passed

    def start(ins, outs, sems):
        _, _, _, _, mine, first, _ = places(ins, outs, sems)
        for cp in [mine] + first:
            cp.start()

    def mid(ins, outs, sems):
        x, y, c, copy, _, _, passed = places(ins, outs, sems)
        for j, chip in enumerate(_other_chips(x, y)):
            copy(1 + j, (*chip, c), (x, y, c)).wait_recv()
            passed[j].start()

    def end(ins, outs, sems):
        x, y, c, copy, mine, first, passed = places(ins, outs, sems)
        copy(0, (x, y, 1 - c), (x, y, c)).wait_recv()
        for j, chip in enumerate(_other_chips(x, y)):
            copy(4 + j, (*chip, 1 - c), (x, y, c)).wait_recv()
        for cp in first + passed:
            cp.wait_send()
        mine.wait()

    return _Comm([block], [jax.ShapeDtypeStruct((N_DEV, m_per, ncol), block.dtype)],
                 [pltpu.SemaphoreType.DMA((7,)), pltpu.SemaphoreType.DMA((7,)), pltpu.SemaphoreType.DMA((1,))],
                 start, end, mid)


def _join_comms(*progs):
    def split(parts, counts):
        out, pos = [], 0
        for n in counts:
            out.append(parts[pos:pos + n])
            pos += n
        return out

    def phase(which):
        def run(ins, outs, sems):
            args = zip(split(ins, [len(p.inputs) for p in progs]), split(outs, [len(p.out_shapes) for p in progs]),
                       split(sems, [len(p.scratch) for p in progs]))
            for p, (i, o, s) in zip(progs, args):
                fn = getattr(p, which)
                if fn is not None:
                    fn(i, o, s)
        return run

    return _Comm([a for p in progs for a in p.inputs], [s for p in progs for s in p.out_shapes],
                 [s for p in progs for s in p.scratch], phase("start"), phase("end"), phase("mid"))


def _adamw(w, g, m, v):
    m = ADAM_B1 * m + (1.0 - ADAM_B1) * g
    v = ADAM_B2 * v + (1.0 - ADAM_B2) * jnp.square(g)
    m_hat = m / (1.0 - ADAM_B1 ** ADAM_STEP)
    v_hat = v / (1.0 - ADAM_B2 ** ADAM_STEP)
    delta = -ADAM_LR * (m_hat / (jnp.sqrt(v_hat) + ADAM_EPS) + ADAM_WD * w)
    return delta, m, v


def _flat_tile(rows):
    return min(rows, 256)


def _sum_pieces(name, layers):
    L = len(layers)
    P, R, C = layers[0].shape
    tr = _flat_tile(R)

    def body(*refs):
        out_ref = refs[L]
        for l in range(L):
            @pl.when(pl.program_id(0) == l)
            def _(p_ref=refs[l]):
                acc = p_ref[0].astype(F32)
                for j in range(1, P):
                    acc = acc + p_ref[j].astype(F32)
                out_ref[...] = acc

    return pl.pallas_call(
        body, name=name, grid=(L, R // tr),
        in_specs=[pl.BlockSpec((P, tr, C), lambda l, i: (0, i, 0)) for _ in range(L)],
        out_specs=pl.BlockSpec((None, tr, C), lambda l, i: (l, i, 0)),
        out_shape=jax.ShapeDtypeStruct((L, R, C), F32),
        compiler_params=_params(dimension_semantics=("parallel", "parallel")),
    )(*layers)


def _adam_pair(name, w, m, v, part_a, part_b):
    L, R, C = w.shape
    tr = _flat_tile(R)

    def body(w_ref, m_ref, v_ref, a_ref, b_ref, g_ref, d_ref, nm_ref, nv_ref):
        g = a_ref[...] + b_ref[...]
        g_ref[...] = g
        d_ref[...], nm_ref[...], nv_ref[...] = _adamw(w_ref[...], g, m_ref[...], v_ref[...])

    row = pl.BlockSpec((None, tr, C), lambda l, i: (l, i, 0))
    return pl.pallas_call(
        body, name=name, grid=(L, R // tr),
        in_specs=[row] * 5, out_specs=[row] * 4,
        out_shape=[jax.ShapeDtypeStruct((L, R, C), F32)] * 4,
        compiler_params=_params(dimension_semantics=("parallel", "parallel")),
    )(w, m, v, part_a, part_b)


def _adam_small(w, m, v, gathered):
    R, C = w.shape

    def body(w_ref, m_ref, v_ref, p_ref, g_ref, d_ref, nm_ref, nv_ref):
        g = p_ref[0]
        for j in range(1, N_DEV):
            g = g + p_ref[j]
        g_ref[...] = g
        d_ref[...], nm_ref[...], nv_ref[...] = _adamw(w_ref[...], g, m_ref[...], v_ref[...])

    return pl.pallas_call(
        body, name="adam_small",
        out_shape=[jax.ShapeDtypeStruct((R, C), F32)] * 4,
        compiler_params=_params(),
    )(w, m, v, gathered)


SMALL = ("mix_norm_g", "mlp_norm_g", "final_norm_g", "a_ln_g", "a_ln_b", "a_w_s", "a_b_s", "rel_bias")


def _pack_small(arrays, width):
    rows = []
    for a in arrays:
        flat = a.reshape(-1)
        pad = (-flat.shape[0]) % width
        rows.append(jnp.pad(flat, (0, pad)).reshape(-1, width))
    block = jnp.concatenate(rows, axis=0)
    return jnp.pad(block, ((0, (-block.shape[0]) % 8), (0, 0)))


def _unpack_small(block, shapes, width):
    out, row = [], 0
    for shape in shapes:
        size = int(np.prod(shape))
        nrows = -(-size // width)
        out.append(block[row:row + nrows].reshape(-1)[:size].reshape(shape))
        row += nrows
    return out


def kernel(x, mix_norm_g, mlp_norm_g, final_norm_g, a_w_in, a_ln_g, a_ln_b, a_w_s, a_b_s, a_w_out, b_w_qkv, b_w_out, rel_bias, w_up, w_down, loss_target, m_mix_norm_g, m_mlp_norm_g, m_final_norm_g, m_a_w_in, m_a_ln_g, m_a_ln_b, m_a_w_s, m_a_b_s, m_a_w_out, m_b_w_qkv, m_b_w_out, m_rel_bias, m_w_up, m_w_down, v_mix_norm_g, v_mlp_norm_g, v_final_norm_g, v_a_w_in, v_a_ln_g, v_a_ln_b, v_a_w_s, v_a_b_s, v_a_w_out, v_b_w_qkv, v_b_w_out, v_rel_bias, v_w_up, v_w_down):
    T, D = x.shape[1], x.shape[2]
    h0 = x.reshape(T, D)
    target = loss_target.reshape(T, D)
    G = a_w_s.shape[1]

    w_big = [a_w_in, a_w_out, b_w_qkv, b_w_out, w_up, w_down]
    m_big = [m_a_w_in, m_a_w_out, m_b_w_qkv, m_b_w_out, m_w_up, m_w_down]
    v_big = [v_a_w_in, v_a_w_out, v_b_w_qkv, v_b_w_out, v_w_up, v_w_down]
    by_cols = [True, False, True, True, True, False]
    s_in, s_out, s_qkv, s_bo, s_up, s_dn = [w.astype(BF16) for w in w_big]
    (W_in,) = _run_comm("gather_a", _gather_weights([(s_in, 0, True)]))

    tril = jnp.tril(jnp.ones((CHUNK, CHUNK), dtype=bool))
    w_tril = jnp.where(tril[None], a_w_s[0], 0.0).astype(BF16)
    w_tril_t = jnp.swapaxes(w_tril, 1, 2)
    b_rows = jnp.broadcast_to(a_b_s[0][:, :, None], (G, CHUNK, CHUNK))
    buckets = _bucket_maps()
    bias = _bias_build(rel_bias, buckets)

    QKV = s_qkv.shape[2] * N_CHIPS
    TM = 1024
    TK_WGRAD = 4096

    def matmul(name, a, b, mode, out, tm=TM, tn=1024, **kw):
        outs = out if isinstance(out, list) else [out]
        return _mm(name, a, b, mode, tm=tm, tn=tn, tk=a.shape[1], outs=outs, **kw)

    def norm_bwd(layer_gain, h, dres, copies=2):
        return dict(epi=_epi_rms_bwd(copies), extras=(h, dres), vecs=(layer_gain,), col_sums=1)

    def wgrad(name, a, b, tn=1024, tk=TK_WGRAD, comm=None):
        return _mm(name, a, b, "tn", tm=1024, tn=tn, tk=tk, outs=[BF16], comm=comm)

    def scatter(*which):
        return _scatter_grads([(g, w_big[i].shape[1:], by_cols[i]) for g, i in which])

    (a_pre, y0), (W_out, W_up0) = matmul("a_in", h0, W_in, "nn", BF16, norm_gain=mix_norm_g[0:1],
                                         comm=_gather_weights([(s_out, 0, False), (s_up, 0, True)]))
    z = _gate_fwd(a_pre, a_ln_g, a_ln_b, w_tril, b_rows)
    h1 = matmul("a_out", z, W_out, "nn", F32, epi=_epi_residual, extras=(h0,))
    (q1, y1), (W_dn0,) = matmul("mlp_up0", h1, W_up0, "nn", BF16, epi=_epi_relu2, norm_gain=mlp_norm_g[0:1],
                                comm=_gather_weights([(s_dn, 0, False)]))
    h2, (W_qkv, W_bo) = matmul("mlp_down0", q1, W_dn0, "nn", F32, tm=TM // 2, epi=_epi_residual, extras=(h1,),
                               comm=_gather_weights([(s_qkv, 0, True), (s_bo, 0, True)]))
    (qkv, y2), (W_up1,) = matmul("b_qkv", h2, W_qkv, "nn", BF16, tn=QKV // 3, norm_gain=mix_norm_g[1:2],
                                 comm=_gather_weights([(s_up, 1, True)]))
    o, lse = _attention_fwd(qkv, bias)
    h3 = matmul("b_out", o, W_bo, "nn", F32, epi=_epi_residual, extras=(h2,))
    (q3, y3), (W_dn1,) = matmul("mlp_up1", h3, W_up1, "nn", BF16, epi=_epi_relu2, norm_gain=mlp_norm_g[1:2],
                                comm=_gather_weights([(s_dn, 1, False)]))
    dh4, dh4_b, d_final_g, loss_row = matmul("mlp_down1", q3, W_dn1, "nn", [F32, BF16], tm=TM // 2, epi=_epi_loss_head,
                                             extras=(h3, target), vecs=(final_norm_g.reshape(1, D),), col_sums=2)
    loss = lax.psum(loss_row[0, 0], ("x", "y", "c"))

    dp3 = matmul("mlp_down_bwd1", dh4_b, W_dn1, "nt", BF16, epi=_epi_relu2_grad, extras=(q3,))
    g_dn1 = wgrad("mlp_down_wgrad1", q3, dh4_b)
    g_up1, (r_dn1,) = wgrad("mlp_up_wgrad1", y3, dp3, comm=scatter((g_dn1, 5)))
    (dh3, dh3_b, dg_mlp1), (r_up1,) = matmul("mlp_up_bwd1", dp3, W_up1, "nt", [F32, BF16], tm=TM // 2,
                                             comm=scatter((g_up1, 4)), **norm_bwd(mlp_norm_g[1:2], h3, dh4))
    do = matmul("b_out_bwd", dh3_b, W_bo, "nt", BF16)
    g_bo = wgrad("b_out_wgrad", o, dh3_b)
    dqkv, dbias = _attention_bwd(qkv, do, o, lse, bias)
    d_rel_bias = _bias_scatter(dbias, buckets)
    (dh2, dh2_b, dg_mix1), (r_bo,) = matmul("b_qkv_bwd", dqkv, W_qkv, "nt", [F32, BF16], tm=TM // 2,
                                            comm=scatter((g_bo, 3)), **norm_bwd(mix_norm_g[1:2], h2, dh3))
    g_qkv = wgrad("b_qkv_wgrad", y2, dqkv, tn=QKV // 3, tk=TK_WGRAD // 2)
    dp1, (r_qkv,) = matmul("mlp_down_bwd0", dh2_b, W_dn0, "nt", BF16, epi=_epi_relu2_grad, extras=(q1,),
                           comm=scatter((g_qkv, 2)))
    g_up0 = wgrad("mlp_up_wgrad0", y1, dp1)
    g_dn0, (r_up0,) = wgrad("mlp_down_wgrad0", q1, dh2_b, comm=scatter((g_up0, 4)))
    (dh1, dh1_b, dg_mlp0), (r_dn0,) = matmul("mlp_up_bwd0", dp1, W_up0, "nt", [F32, BF16], tm=TM // 2,
                                             comm=scatter((g_dn0, 5)), **norm_bwd(mlp_norm_g[0:1], h1, dh2))
    dz = matmul("a_out_bwd", dh1_b, W_out, "nt", F32)
    g_out = wgrad("a_out_wgrad", z, dh1_b)
    da, d_ln_g, d_ln_b, d_w_s, d_b_s = _gate_bwd(a_pre, dz, a_ln_g, a_ln_b, w_tril, w_tril_t, b_rows)
    g_in, (r_out,) = wgrad("a_in_wgrad", y0, da, comm=scatter((g_out, 1)))
    grad_x, dg_mix0 = matmul("a_in_bwd", da, W_in, "nt", F32, **norm_bwd(mix_norm_g[0:1], h0, dh1, copies=1))

    small_w = [mix_norm_g, mlp_norm_g, final_norm_g, a_ln_g, a_ln_b, a_w_s, a_b_s, rel_bias]
    small_m = [m_mix_norm_g, m_mlp_norm_g, m_final_norm_g, m_a_ln_g, m_a_ln_b, m_a_w_s, m_a_b_s, m_rel_bias]
    small_v = [v_mix_norm_g, v_mlp_norm_g, v_final_norm_g, v_a_ln_g, v_a_ln_b, v_a_w_s, v_a_b_s, v_rel_bias]
    small_g = [jnp.concatenate([dg_mix0, dg_mix1]), jnp.concatenate([dg_mlp0, dg_mlp1]), d_final_g,
               d_ln_g, d_ln_b, d_w_s[None], d_b_s[None, :, :, 0], d_rel_bias]
    width = max(D, 128)
    received = [None, [r_out], [r_qkv], [r_bo], [r_up0, r_up1], [r_dn0, r_dn1]]
    plane = [None] + [_sum_pieces(f"sum_pieces{i}", received[i]) for i in range(1, len(w_big))]
    tail = _run_comm("tail_comm", _join_comms(scatter((g_in, 0)), _exchange_sibling(plane[1:]),
                                              _allgather_small(_pack_small(small_g, width))))
    r_in, other, gathered_small = tail[0], [None] + list(tail[1:len(w_big)]), tail[len(w_big)]
    plane[0] = _sum_pieces("sum_pieces0", [r_in])
    (other[0],) = _run_comm("exchange_a_in", _exchange_sibling([plane[0]]))
    big_out = [_adam_pair(f"adam{i}", w_big[i], m_big[i], v_big[i], plane[i], other[i]) for i in range(len(w_big))]

    def unbig(kind):
        return dict(zip(["a_w_in", "a_w_out", "b_w_qkv", "b_w_out", "w_up", "w_down"], [b[kind] for b in big_out]))

    small_out = _adam_small(_pack_small(small_w, width), _pack_small(small_m, width), _pack_small(small_v, width),
                            gathered_small)
    shapes = [w.shape for w in small_w]

    names = ["mix_norm_g", "mlp_norm_g", "final_norm_g", "a_w_in", "a_ln_g", "a_ln_b", "a_w_s", "a_b_s", "a_w_out",
             "b_w_qkv", "b_w_out", "rel_bias", "w_up", "w_down"]
    results = [loss, grad_x.reshape(x.shape)]
    for kind in range(4):
        table = dict(zip(SMALL, _unpack_small(small_out[kind], shapes, width)))
        table.update(unbig(kind))
        results += [table[n] for n in names]
    return tuple(results)
```

```python
import functools
import math

import numpy as np
import jax
import jax.numpy as jnp
from jax import lax
from jax.experimental import pallas as pl
from jax.experimental.pallas import tpu as pltpu

F32 = jnp.float32
BF16 = jnp.bfloat16
MESH = pl.DeviceIdType.MESH
ANY = pl.BlockSpec(memory_space=pl.ANY)

N_CHIPS = 4
N_DEV = 8
VMEM_LIMIT_BYTES = 56 * 1024 * 1024

EPS = 1e-6
NEG_INF = -1e30
CHUNK = 128
GROUP_DIM = 128
HEAD_DIM = 64
ATT_HEADS = 8
ATT_WIDTH = ATT_HEADS * HEAD_DIM
PAIR = 2 * HEAD_DIM
BLK = 128
DILATIONS = (1, 4, 16)
N_BUCKETS = 32
MAX_EXACT = N_BUCKETS // 2
REL_MAX_DISTANCE = 2048

ADAM_LR = 0.001
ADAM_B1 = 0.9
ADAM_B2 = 0.999
ADAM_EPS = 1e-08
ADAM_WD = 0.01
ADAM_STEP = 10

NN = (((1,), (0,)), ((), ()))
NT = (((1,), (1,)), ((), ()))
TN = (((0,), (0,)), ((), ()))


def _params(**kw):
    return pltpu.CompilerParams(vmem_limit_bytes=VMEM_LIMIT_BYTES, **kw)


def _dot(a, b, dims=NN):
    return lax.dot_general(a, b, dims, preferred_element_type=F32)


def _gelu(x):
    return 0.5 * x * (1.0 + lax.erf(x * math.sqrt(0.5)))


def _gelu_grad(x):
    return 0.5 * (1.0 + lax.erf(x * math.sqrt(0.5))) + x * jnp.exp(-0.5 * x * x) * (1.0 / math.sqrt(2.0 * math.pi))


def _mean(x):
    return jnp.mean(x, axis=-1, keepdims=True)


class _Comm:
    def __init__(self, inputs, out_shapes, scratch, start, end, mid=None):
        self.inputs, self.out_shapes, self.scratch = list(inputs), list(out_shapes), list(scratch)
        self.start, self.mid, self.end = start, mid, end


def _run_comm(name, comm):
    n_in, n_out = len(comm.inputs), len(comm.out_shapes)

    def body(*refs):
        parts = refs[:n_in], refs[n_in:n_in + n_out], refs[n_in + n_out:]
        comm.start(*parts)
        if comm.mid is not None:
            comm.mid(*parts)
        comm.end(*parts)

    return pl.pallas_call(
        body, name=name, in_specs=[ANY] * n_in, out_specs=[ANY] * n_out, out_shape=comm.out_shapes,
        scratch_shapes=comm.scratch, compiler_params=_params(),
    )(*comm.inputs)


def _mm(name, a, b, mode, *, tm, tn, tk, outs, epi=None, extras=(), vecs=(), col_sums=0, norm_gain=None, comm=None):
    if mode == "tn":
        K, M = a.shape
    else:
        M, K = a.shape
    N = b.shape[0] if mode == "nt" else b.shape[1]
    tm, tn, tk = min(tm, M), min(tn, N), min(tk, K)
    assert M % tm == 0 and N % tn == 0 and K % tk == 0, (name, M, N, K, tm, tn, tk)
    nk = K // tk
    grid = (M // tm, N // tn, nk)

    if mode == "tn":
        a_spec = pl.BlockSpec((tk, tm), lambda i, j, k: (k, i))
    else:
        a_spec = pl.BlockSpec((tm, tk), lambda i, j, k: (i, k))
    if mode == "nt":
        b_spec = pl.BlockSpec((tn, tk), lambda i, j, k: (j, k))
    else:
        b_spec = pl.BlockSpec((tk, tn), lambda i, j, k: (k, j))
    tile = pl.BlockSpec((tm, tn), lambda i, j, k: (i, j))
    vec = pl.BlockSpec((1, tn), lambda i, j, k: (0, j))
    normed = norm_gain is not None
    assert not normed or (mode == "nn" and nk == 1 and tm % grid[1] == 0)
    assert col_sums == 0 or grid[1] == 1
    out_shapes = [jax.ShapeDtypeStruct((M, N), dtype) for dtype in outs]
    out_specs = [tile for _ in outs]
    extra_specs = [tile for _ in extras] + [vec for _ in vecs]
    if normed:
        part_rows = tm // grid[1]
        last_part = M // part_rows - 1
        a_spec = pl.BlockSpec((tm, K), lambda i, j, k: (0, 0))
        out_shapes.append(jax.ShapeDtypeStruct((M, K), BF16))
        out_specs.append(pl.BlockSpec((part_rows, K), lambda i, j, k: (i * grid[1] + j, 0)))
        extra_specs.append(pl.BlockSpec((1, K), lambda i, j, k: (0, 0)))
        extra_specs.append(pl.BlockSpec((part_rows, K),
                                        lambda i, j, k: (jnp.minimum((i + 1) * grid[1] + j, last_part), 0)))
    out_shapes += [jax.ShapeDtypeStruct((1, N), F32)] * col_sums
    out_specs += [vec] * col_sums
    n_extra, n_out = len(extra_specs), len(out_shapes)
    n_tiles = len(outs)
    n_cin = len(comm.inputs) if comm else 0
    n_cout = len(comm.out_shapes) if comm else 0
    dims = {"nn": NN, "nt": NT, "tn": TN}[mode]
    steps = grid[0] * grid[1] * grid[2]

    def body(*refs):
        a_ref, b_ref = refs[0], refs[1]
        pos = 2
        extra_refs = refs[pos:pos + n_extra]
        pos += n_extra
        comm_in = refs[pos:pos + n_cin]
        pos += n_cin
        out_refs = refs[pos:pos + n_out]
        pos += n_out
        comm_out = refs[pos:pos + n_cout]
        pos += n_cout
        acc_ref = refs[pos] if nk > 1 else None
        pos += nk > 1
        y_refs = refs[pos:pos + 2 * normed]
        comm_sems = refs[pos + 2 * normed:]
        k = pl.program_id(2)
        step = (pl.program_id(0) * grid[1] + pl.program_id(1)) * nk + k

        if comm is not None:
            @pl.when(step == 0)
            def _():
                comm.start(comm_in, comm_out, comm_sems)

        def finish(acc):
            epi_args = [e[...] for e in extra_refs[:n_extra - 2 * normed]]
            res = epi(acc, *epi_args) if epi is not None else (acc,) * n_tiles
            for o, r in zip(out_refs[:n_tiles], res[:n_tiles]):
                o[...] = r.astype(o.dtype)
            if col_sums:
                sums = out_refs[n_out - col_sums:]

                @pl.when(pl.program_id(0) == 0)
                def _():
                    for o in sums:
                        o[...] = jnp.zeros_like(o)

                for o, r in zip(sums, res[n_tiles:]):
                    o[...] += r

        if normed:
            gain_ref, ahead_ref = extra_refs[-2], extra_refs[-1]

            def norm(hv):
                return (hv * lax.rsqrt(_mean(hv * hv) + EPS) * gain_ref[...]).astype(BF16)

            @pl.when(step == 0)
            def _():
                y_refs[0][...] = norm(a_ref[...])

            part_at = pl.ds(pl.multiple_of(pl.program_id(1) * part_rows, part_rows), part_rows)
            for parity in range(2):
                @pl.when(pl.program_id(0) % 2 == parity)
                def _(y_now=y_refs[parity], y_next=y_refs[1 - parity]):
                    finish(_dot(y_now[...], b_ref[...].astype(BF16), dims))
                    out_refs[n_tiles][...] = y_now[part_at, :]
                    y_next[part_at, :] = norm(ahead_ref[...])
        elif nk == 1:
            finish(_dot(a_ref[...].astype(BF16), b_ref[...].astype(BF16), dims))
        else:
            part = _dot(a_ref[...].astype(BF16), b_ref[...].astype(BF16), dims)

            @pl.when(k == 0)
            def _():
                acc_ref[...] = part

            @pl.when(k > 0)
            def _():
                acc_ref[...] += part

            @pl.when(k == nk - 1)
            def _():
                finish(acc_ref[...])

        if comm is not None:
            if comm.mid is not None:
                @pl.when(step == (3 * steps) // 4)
                def _():
                    comm.mid(comm_in, comm_out, comm_sems)

            @pl.when(step == steps - 1)
            def _():
                comm.end(comm_in, comm_out, comm_sems)

    sequential = comm is not None or normed or col_sums > 0
    order = ("arbitrary",) * 3 if sequential else ("parallel", "parallel", "arbitrary")
    scratch = [pltpu.VMEM((tm, tn), F32)] if nk > 1 else []
    if normed:
        scratch += [pltpu.VMEM((tm, K), BF16)] * 2
    res = pl.pallas_call(
        body, name=name, grid=grid,
        in_specs=[a_spec, b_spec] + extra_specs + [ANY] * n_cin,
        out_specs=out_specs + [ANY] * n_cout,
        out_shape=out_shapes + (comm.out_shapes if comm else []),
        scratch_shapes=scratch + (comm.scratch if comm else []),
        compiler_params=_params(dimension_semantics=order),
    )(a, b, *extras, *vecs, *([norm_gain, a] if normed else []), *(comm.inputs if comm else []))
    mm_out = res[0] if n_out == 1 else list(res[:n_out])
    return (mm_out, list(res[n_out:])) if comm else mm_out


def _epi_residual(acc, res):
    return (res + acc,)


def _epi_relu2(acc):
    return (jnp.square(jnp.maximum(acc, 0.0)),)


def _epi_rms_bwd(copies):
    def epi(acc, h, dres, g):
        r = lax.rsqrt(_mean(h * h) + EPS)
        hn = h * r
        dyg = acc * g
        dh = dres + r * (dyg - hn * _mean(dyg * hn))
        return (dh,) * copies + (jnp.sum(acc * hn, axis=0, keepdims=True),)
    return epi


def _epi_loss_head(acc, res, target, g):
    h = res + acc
    r = lax.rsqrt(_mean(h * h) + EPS)
    hn = h * r
    diff = hn * g - target
    loss = 0.5 * jnp.sum(_mean(diff * diff))
    dy = diff * (1.0 / h.shape[-1])
    dyg = dy * g
    dh = r * (dyg - hn * _mean(dyg * hn))
    return dh, dh, jnp.sum(dy * hn, axis=0, keepdims=True), jnp.full((1, h.shape[-1]), loss, F32)


def _epi_relu2_grad(acc, q):
    qf = q.astype(F32)
    return (acc * jnp.where(qf > 0.0, (2.0 * qf) * lax.rsqrt(qf), 0.0),)


def _row_tile(T):
    return min(T, 512)


def _gate_tile(T):
    return min(T, 256)


def _gate_fwd(a, ln_g, ln_b, w_tril, b_rows):
    T, W2 = a.shape
    W = W2 // 2
    G = W // GROUP_DIM
    tr = _gate_tile(T)

    def body(a_ref, lng_ref, lnb_ref, w_ref, b_ref, z_ref):
        u = _gelu(a_ref[:, :W].astype(F32))
        vg = _gelu(a_ref[:, W:].astype(F32))
        xc = vg - _mean(vg)
        vn = xc * lax.rsqrt(_mean(xc * xc) + EPS)
        vl = (vn * lng_ref[...] + lnb_ref[...]).astype(BF16)
        for n in range(tr // CHUNK):
            rows = slice(n * CHUNK, (n + 1) * CHUNK)
            for g in range(G):
                cols = slice(g * GROUP_DIM, (g + 1) * GROUP_DIM)
                gate = _dot(w_ref[g], vl[rows, cols]) + b_ref[g]
                z_ref[rows, cols] = (u[rows, cols] * gate).astype(BF16)

    vec = pl.BlockSpec((1, W), lambda i: (0, 0))
    grp = pl.BlockSpec((G, CHUNK, CHUNK), lambda i: (0, 0, 0))
    return pl.pallas_call(
        body, name="gate_fwd", grid=(T // tr,),
        in_specs=[pl.BlockSpec((tr, W2), lambda i: (i, 0)), vec, vec, grp, grp],
        out_specs=pl.BlockSpec((tr, W), lambda i: (i, 0)),
        out_shape=jax.ShapeDtypeStruct((T, W), BF16),
        compiler_params=_params(dimension_semantics=("parallel",)),
    )(a, ln_g, ln_b, w_tril, b_rows)


def _gate_bwd(a, dz, ln_g, ln_b, w_tril, w_tril_t, b_rows):
    T, W2 = a.shape
    W = W2 // 2
    G = W // GROUP_DIM
    tr = _gate_tile(T)
    steps = T // tr

    def body(a_ref, dz_ref, lng_ref, lnb_ref, w_ref, wt_ref, b_ref, da_ref, dlng_ref, dlnb_ref, dw_ref, dbs_ref, dvl_ref):
        step = pl.program_id(0)

        @pl.when(step == 0)
        def _():
            dlng_ref[...] = jnp.zeros_like(dlng_ref)
            dlnb_ref[...] = jnp.zeros_like(dlnb_ref)
            dw_ref[...] = jnp.zeros_like(dw_ref)
            dbs_ref[...] = jnp.zeros_like(dbs_ref)

        au = a_ref[:, :W].astype(F32)
        av = a_ref[:, W:].astype(F32)
        u = _gelu(au)
        vg = _gelu(av)
        xc = vg - _mean(vg)
        rstd = lax.rsqrt(_mean(xc * xc) + EPS)
        vn = xc * rstd
        lng = lng_ref[...]
        vl = (vn * lng + lnb_ref[...]).astype(BF16)
        du_scale = dz_ref[...] * _gelu_grad(au)
        dgate_all = dz_ref[...] * u
        for n in range(tr // CHUNK):
            rows = slice(n * CHUNK, (n + 1) * CHUNK)
            for g in range(G):
                cols = slice(g * GROUP_DIM, (g + 1) * GROUP_DIM)
                vlg = vl[rows, cols]
                gate = _dot(w_ref[g], vlg) + b_ref[g]
                da_ref[rows, cols] = (du_scale[rows, cols] * gate).astype(BF16)
                dgate = dgate_all[rows, cols]
                dbs_ref[g] += dgate
                dgate_b = dgate.astype(BF16)
                dw_ref[g] += _dot(dgate_b, vlg, NT)
                dvl_ref[rows, cols] = _dot(wt_ref[g], dgate_b)
        dvl = dvl_ref[...]
        dlnb_ref[...] += jnp.sum(dvl, axis=0, keepdims=True)
        dlng_ref[...] += jnp.sum(dvl * vn, axis=0, keepdims=True)
        dvn = dvl * lng
        dvg = rstd * (dvn - _mean(dvn) - vn * _mean(dvn * vn))
        da_ref[:, W:] = (dvg * _gelu_grad(av)).astype(BF16)

        @pl.when(step == steps - 1)
        def _():
            t_idx = lax.broadcasted_iota(jnp.int32, (CHUNK, CHUNK), 0)
            s_idx = lax.broadcasted_iota(jnp.int32, (CHUNK, CHUNK), 1)
            for g in range(G):
                dw_ref[g] = jnp.where(s_idx <= t_idx, dw_ref[g], 0.0)
                dbs_ref[g] = jnp.broadcast_to(jnp.sum(dbs_ref[g], axis=-1, keepdims=True), (CHUNK, CHUNK))

    vec = pl.BlockSpec((1, W), lambda i: (0, 0))
    grp = pl.BlockSpec((G, CHUNK, CHUNK), lambda i: (0, 0, 0))
    return pl.pallas_call(
        body, name="gate_bwd", grid=(steps,),
        in_specs=[pl.BlockSpec((tr, W2), lambda i: (i, 0)), pl.BlockSpec((tr, W), lambda i: (i, 0)),
                  vec, vec, grp, grp, grp],
        out_specs=[pl.BlockSpec((tr, W2), lambda i: (i, 0)), vec, vec, grp, grp],
        out_shape=[jax.ShapeDtypeStruct((T, W2), BF16), jax.ShapeDtypeStruct((1, W), F32),
                   jax.ShapeDtypeStruct((1, W), F32), jax.ShapeDtypeStruct((G, CHUNK, CHUNK), F32),
                   jax.ShapeDtypeStruct((G, CHUNK, CHUNK), F32)],
        scratch_shapes=[pltpu.VMEM((tr, W), F32)],
        compiler_params=_params(dimension_semantics=("arbitrary",)),
    )(a, dz, ln_g, ln_b, w_tril, w_tril_t, b_rows)


def _bucket_map(dilation):
    rel = BLK + np.arange(BLK)[:, None] - np.arange(2 * BLK)[None, :]
    dist = np.clip(rel, 0, BLK) * dilation
    nf = np.maximum(dist, 1).astype(np.float32)
    large = MAX_EXACT + (np.log(nf / np.float32(MAX_EXACT)) / np.float32(math.log(REL_MAX_DISTANCE / MAX_EXACT))
                         * np.float32(N_BUCKETS - MAX_EXACT)).astype(np.int32)
    large = np.minimum(large, N_BUCKETS - 1)
    return np.where(dist < MAX_EXACT, dist, large).astype(np.int32)


def _bucket_maps():
    return jnp.asarray(np.stack([_bucket_map(d) for d in DILATIONS]))


def _bias_build(rel_bias, buckets):
    NG = len(DILATIONS)

    def body(table_ref, bucket_ref, out_ref):
        for g in range(NG):
            bk = bucket_ref[g]
            for h in range(ATT_HEADS):
                out_ref[0, g, h] = jnp.zeros((BLK, 2 * BLK), F32)
            for b in range(N_BUCKETS):
                hit = bk == b
                for h in range(ATT_HEADS):
                    out_ref[0, g, h] = jnp.where(hit, table_ref[b, g * ATT_HEADS + h], out_ref[0, g, h])
            for h in range(ATT_HEADS):
                for first in range(2):
                    out_ref[first, g, h] = jnp.where(_window_mask(first), out_ref[0, g, h], NEG_INF)

    return pl.pallas_call(
        body, name="bias_build",
        in_specs=[pl.BlockSpec(memory_space=pltpu.SMEM), pl.BlockSpec(memory_space=pltpu.VMEM)],
        out_specs=pl.BlockSpec(memory_space=pltpu.VMEM),
        out_shape=jax.ShapeDtypeStruct((2, NG, ATT_HEADS, BLK, 2 * BLK), F32),
        compiler_params=_params(),
    )(rel_bias, buckets)


def _bias_scatter(dbias, buckets):
    NG = len(DILATIONS)

    def body(dbias_ref, bucket_ref, out_ref):
        for g in range(NG):
            bk = bucket_ref[g]
            for b in range(N_BUCKETS):
                hit = bk == b
                for h in range(ATT_HEADS):
                    out_ref[b, g * ATT_HEADS + h] = jnp.sum(jnp.where(hit, dbias_ref[g, h], 0.0))

    return pl.pallas_call(
        body, name="bias_scatter",
        in_specs=[pl.BlockSpec(memory_space=pltpu.VMEM), pl.BlockSpec(memory_space=pltpu.VMEM)],
        out_specs=pl.BlockSpec(memory_space=pltpu.SMEM),
        out_shape=jax.ShapeDtypeStruct((N_BUCKETS, NG * ATT_HEADS), F32),
        compiler_params=_params(),
    )(dbias, buckets)


def _window_mask(first):
    qi = lax.broadcasted_iota(jnp.int32, (BLK, 2 * BLK), 0)
    kj = lax.broadcasted_iota(jnp.int32, (BLK, 2 * BLK), 1)
    rel = BLK + qi - kj
    return (rel >= 0) & (rel <= BLK) & (kj >= BLK * first)


def _head_lanes(hh):
    lane = lax.broadcasted_iota(jnp.int32, (1, PAIR), 1)
    return (lane >= hh * HEAD_DIM) & (lane < (hh + 1) * HEAD_DIM)


ATT_STEP_BLOCKS = 8


def _attn_steps(stride):
    per_step = math.gcd(stride, ATT_STEP_BLOCKS)
    return per_step, stride // per_step


def _attn_fwd(name, g, qkv, qc, kc, vc, bias, stride):
    T = qkv.shape[0]
    per_step, lag = _attn_steps(stride)
    rows = per_step * BLK
    scale = HEAD_DIM ** -0.5

    def body(q_ref, kp_ref, kc_ref, vp_ref, vc_ref, bias_ref, out_ref):
        first = (pl.program_id(0) < lag).astype(jnp.int32)
        low = _head_lanes(0)

        def block(j, carry):
            at = pl.ds(pl.multiple_of(j * BLK, BLK), BLK)
            for hp in range(ATT_HEADS // 2):
                cols = slice(hp * PAIR, (hp + 1) * PAIR)
                qp = q_ref[at, cols]
                kk = jnp.concatenate([kp_ref[at, cols], kc_ref[at, cols]], axis=0)
                vv = jnp.concatenate([vp_ref[at, cols], vc_ref[at, cols]], axis=0)
                o_h, lse_h = [], []
                for hh in range(2):
                    qm = jnp.where(_head_lanes(hh), qp, jnp.zeros_like(qp))
                    s = _dot(qm, kk, NT) * scale
                    logits = s + bias_ref[first, 2 * hp + hh]
                    m = jnp.max(logits, axis=-1, keepdims=True)
                    p = jnp.exp(logits - m)
                    den = jnp.sum(p, axis=-1, keepdims=True)
                    o_h.append(_dot(p.astype(BF16), vv) / den)
                    lse_h.append(m + jnp.log(den))
                out_ref[at, cols] = jnp.where(low, o_h[0], o_h[1])
                out_ref[at, slice(ATT_WIDTH + hp * PAIR, ATT_WIDTH + (hp + 1) * PAIR)] = (
                    jnp.where(low, lse_h[0], lse_h[1]))
            return carry

        lax.fori_loop(0, per_step, block, 0)

    def cur(c):
        return pl.BlockSpec((rows, ATT_WIDTH), lambda s: (s, c))

    def prev(c):
        return pl.BlockSpec((rows, ATT_WIDTH), lambda s: (jnp.maximum(s - lag, 0), c))

    return pl.pallas_call(
        body, name=name, grid=(T // rows,),
        in_specs=[cur(qc), prev(kc), cur(kc), prev(vc), cur(vc),
                  pl.BlockSpec((2, None, ATT_HEADS, BLK, 2 * BLK), lambda s: (0, g, 0, 0, 0))],
        out_specs=pl.BlockSpec((rows, 2 * ATT_WIDTH), lambda s: (s, 0)),
        out_shape=jax.ShapeDtypeStruct((T, 2 * ATT_WIDTH), F32),
        compiler_params=_params(dimension_semantics=("parallel",)),
    )(qkv, qkv, qkv, qkv, qkv, bias)


def _attn_merge(parts):
    T = parts[0].shape[0]
    tr = _row_tile(T)
    n = len(parts)

    def body(*refs):
        o_refs, l_refs = refs[:n], refs[n:2 * n]
        o_ref, lse_ref = refs[2 * n], refs[2 * n + 1]
        ls = [r[...] for r in l_refs]
        m = functools.reduce(jnp.maximum, ls)
        es = [jnp.exp(l - m) for l in ls]
        tot = functools.reduce(lambda x, y: x + y, es)
        acc = functools.reduce(lambda x, y: x + y, [e * r[...] for e, r in zip(es, o_refs)])
        o_ref[...] = (acc / tot).astype(BF16)
        lse_ref[...] = m + jnp.log(tot)

    row = pl.BlockSpec((tr, ATT_WIDTH), lambda i: (i, 0))
    row_lse = pl.BlockSpec((tr, ATT_WIDTH), lambda i: (i, 1))
    return pl.pallas_call(
        body, name="attn_merge", grid=(T // tr,),
        in_specs=[row] * n + [row_lse] * n, out_specs=[row, row],
        out_shape=[jax.ShapeDtypeStruct((T, ATT_WIDTH), BF16), jax.ShapeDtypeStruct((T, ATT_WIDTH), F32)],
        compiler_params=_params(dimension_semantics=("parallel",)),
    )(*parts, *parts)


def _attn_bwd(name, g, qkv, qc, kc, vc, do, o, lse, bias, stride):
    T = qkv.shape[0]
    per_step, lag = _attn_steps(stride)
    rows = per_step * BLK
    steps = T // rows
    scale = HEAD_DIM ** -0.5

    def body(q_ref, kp_ref, kc_ref, vp_ref, vc_ref, do_ref, o_ref, lse_ref, bias_ref,
             dq_ref, dkv_ref, db_ref, carry_k, carry_v):
        step = pl.program_id(0)
        slot0 = (step % lag) * per_step

        @pl.when(step == 0)
        def _():
            db_ref[...] = jnp.zeros_like(db_ref)
            carry_k[...] = jnp.zeros_like(carry_k)
            carry_v[...] = jnp.zeros_like(carry_v)

        @pl.when(step >= steps)
        def _():
            def flush(j, carry):
                at = pl.ds(pl.multiple_of(j * BLK, BLK), BLK)
                dkv_ref[at, :ATT_WIDTH] = carry_k[slot0 + j].astype(BF16)
                dkv_ref[at, ATT_WIDTH:] = carry_v[slot0 + j].astype(BF16)
                return carry

            lax.fori_loop(0, per_step, flush, 0)

        @pl.when(step < steps)
        def _():
            first = (step < lag).astype(jnp.int32)

            def block(j, carry):
                at = pl.ds(pl.multiple_of(j * BLK, BLK), BLK)
                ck_ref = carry_k.at[slot0 + j]
                cv_ref = carry_v.at[slot0 + j]
                for hp in range(ATT_HEADS // 2):
                    cols = slice(hp * PAIR, (hp + 1) * PAIR)
                    qp = q_ref[at, cols]
                    kk = jnp.concatenate([kp_ref[at, cols], kc_ref[at, cols]], axis=0)
                    vv = jnp.concatenate([vp_ref[at, cols], vc_ref[at, cols]], axis=0)
                    dop = do_ref[at, cols]
                    lsep = lse_ref[at, cols]
                    prod = dop.astype(F32) * o_ref[at, cols].astype(F32)
                    dq = jnp.zeros((BLK, PAIR), F32)
                    dk = jnp.zeros((2 * BLK, PAIR), F32)
                    dv = jnp.zeros((2 * BLK, PAIR), F32)
                    for hh in range(2):
                        lanes = _head_lanes(hh)
                        qm = jnp.where(lanes, qp, jnp.zeros_like(qp))
                        dom = jnp.where(lanes, dop, jnp.zeros_like(dop))
                        km = jnp.where(lanes, kk, jnp.zeros_like(kk))
                        delta = jnp.sum(jnp.where(lanes, prod, 0.0), axis=-1, keepdims=True)
                        lse_h = jnp.max(jnp.where(lanes, lsep, NEG_INF), axis=-1, keepdims=True)
                        s = _dot(qm, kk, NT) * scale
                        logits = s + bias_ref[first, 2 * hp + hh]
                        p = jnp.exp(logits - lse_h)
                        dv += _dot(p.astype(BF16), dom, TN)
                        ds = p * (_dot(dom, vv, NT) - delta)
                        db_ref[2 * hp + hh] += ds
                        dss = (ds * scale).astype(BF16)
                        dq += _dot(dss, km)
                        dk += _dot(dss, qm, TN)
                    dq_ref[at, cols] = dq.astype(BF16)
                    dkv_ref[at, cols] = (ck_ref[:, cols] + dk[:BLK]).astype(BF16)
                    dkv_ref[at, slice(ATT_WIDTH + hp * PAIR, ATT_WIDTH + (hp + 1) * PAIR)] = (
                        cv_ref[:, cols] + dv[:BLK]).astype(BF16)
                    ck_ref[:, cols] = dk[BLK:]
                    cv_ref[:, cols] = dv[BLK:]
                return carry

            lax.fori_loop(0, per_step, block, 0)

    last = steps - 1

    def cur(c):
        return pl.BlockSpec((rows, ATT_WIDTH), lambda s: (jnp.minimum(s, last), c))

    def prev(c):
        return pl.BlockSpec((rows, ATT_WIDTH), lambda s: (jnp.clip(s - lag, 0, last), c))

    dbias_shape = (ATT_HEADS, BLK, 2 * BLK)
    return pl.pallas_call(
        body, name=name, grid=(steps + lag,),
        in_specs=[cur(qc), prev(kc), cur(kc), prev(vc), cur(vc), cur(0), cur(0), cur(0),
                  pl.BlockSpec((2, None, ATT_HEADS, BLK, 2 * BLK), lambda s: (0, g, 0, 0, 0))],
        out_specs=[cur(0), pl.BlockSpec((rows, 2 * ATT_WIDTH), lambda s: (jnp.clip(s - lag, 0, last), 0)),
                   pl.BlockSpec(dbias_shape, lambda s: (0, 0, 0))],
        out_shape=[jax.ShapeDtypeStruct((T, ATT_WIDTH), BF16), jax.ShapeDtypeStruct((T, 2 * ATT_WIDTH), BF16),
                   jax.ShapeDtypeStruct(dbias_shape, F32)],
        scratch_shapes=[pltpu.VMEM((stride, BLK, ATT_WIDTH), F32), pltpu.VMEM((stride, BLK, ATT_WIDTH), F32)],
        compiler_params=_params(dimension_semantics=("arbitrary",)),
    )(qkv, qkv, qkv, qkv, qkv, do, o, lse, bias)


REORDER_TILE = 256
REORDER_ROWS = 2048


def _reorder_matrix(d, inverse):
    per = REORDER_TILE // d
    p = np.zeros((REORDER_TILE, REORDER_TILE), np.float32)
    for src in range(REORDER_TILE):
        i, r = divmod(src, d)
        p[r * per + i, src] = 1.0
    return jnp.asarray(p.T if inverse else p, dtype=BF16)


def _reorder_rows(name, src, d, inverse, *, src_col=0, col_stride=1, ncols=1, dst=None, dst_col=0, dst_stride=1,
                  dst_blocks=None):
    T = src.shape[0]
    dtype = src.dtype
    span = BLK * d
    rows = max(span, min(T, REORDER_ROWS))
    per = REORDER_TILE // d
    tiles = span // REORDER_TILE
    dst_blocks = ncols if dst_blocks is None else dst_blocks

    def apply(p, x):
        if dtype == BF16:
            return _dot(p, x).astype(BF16)
        hi = x.astype(BF16)
        rest = x - hi.astype(F32)
        mid = rest.astype(BF16)
        low = (rest - mid.astype(F32)).astype(BF16)
        return _dot(p, hi) + _dot(p, mid) + _dot(p, low)

    def body(*refs):
        p_ref, x_ref, o_ref = refs[0], refs[1], refs[-1]
        if d == 1:
            o_ref[...] = x_ref[...]
            return
        for s in range(rows // span):
            for t in range(tiles):
                base = s * span
                tile_rows = slice(base + t * REORDER_TILE, base + (t + 1) * REORDER_TILE)
                chunk = lambda r: slice(base + r * BLK + t * per, base + r * BLK + (t + 1) * per)
                if inverse:
                    gathered = jnp.concatenate([x_ref[chunk(r), :] for r in range(d)], axis=0)
                    o_ref[tile_rows, :] = apply(p_ref[...], gathered)
                else:
                    y = apply(p_ref[...], x_ref[tile_rows, :])
                    for r in range(d):
                        o_ref[chunk(r), :] = y[r * per:(r + 1) * per]

    in_specs = [pl.BlockSpec((REORDER_TILE, REORDER_TILE), lambda w, k: (0, 0)),
                pl.BlockSpec((rows, ATT_WIDTH), lambda w, k: (w, src_col + col_stride * k))]
    operands = [_reorder_matrix(max(d, 2), inverse), src]
    aliases = {}
    if dst is not None:
        in_specs.append(ANY)
        operands.append(dst)
        aliases = {2: 0}
    return pl.pallas_call(
        body, name=name, grid=(T // rows, ncols), in_specs=in_specs,
        out_specs=pl.BlockSpec((rows, ATT_WIDTH), lambda w, k: (w, dst_col + dst_stride * k)),
        out_shape=jax.ShapeDtypeStruct((T, dst_blocks * ATT_WIDTH), dtype),
        input_output_aliases=aliases,
        compiler_params=_params(dimension_semantics=("parallel", "parallel")),
    )(*operands)


def _group_qkv(qkv, g, d):
    NG = len(DILATIONS)
    if d == 1:
        return qkv, (g, NG + g, 2 * NG + g)
    return _reorder_rows(f"qkv_to_residues{g}", qkv, d, False, src_col=g, col_stride=NG, ncols=3), (0, 1, 2)


def _attention_fwd(qkv, bias):
    T = qkv.shape[0]
    parts = []
    for g, d in enumerate(DILATIONS):
        src, (qc, kc, vc) = _group_qkv(qkv, g, d)
        part = _attn_fwd(f"attn_fwd_{g}", g, src, qc, kc, vc, bias, d)
        parts.append(part if d == 1 else _reorder_rows(f"out_to_positions{g}", part, d, True, ncols=2))
    return _attn_merge(parts)


def _attention_bwd(qkv, do, o, lse, bias):
    T = qkv.shape[0]
    NG = len(DILATIONS)
    dqkv, dbs = None, []
    for g, d in enumerate(DILATIONS):
        src, (qc, kc, vc) = _group_qkv(qkv, g, d)
        do_g, o_g, lse_g = do, o, lse
        if d > 1:
            do_g = _reorder_rows(f"do_to_residues{g}", do, d, False)
            o_g = _reorder_rows(f"o_to_residues{g}", o, d, False)
            lse_g = _reorder_rows(f"lse_to_residues{g}", lse, d, False)
        dq, dkv, db = _attn_bwd(f"attn_bwd_{g}", g, src, qc, kc, vc, do_g, o_g, lse_g, bias, d)
        dqkv = _reorder_rows(f"dq_to_positions{g}", dq, d, True, dst=dqkv, dst_col=g, dst_blocks=3 * NG)
        dqkv = _reorder_rows(f"dkv_to_positions{g}", dkv, d, True, ncols=2, dst=dqkv, dst_col=NG + g, dst_stride=NG,
                             dst_blocks=3 * NG)
        dbs.append(db)
    return dqkv, jnp.stack(dbs)


def _other_chips(x, y):
    return [(1 - x, y), (x, 1 - y), (1 - x, 1 - y)]


def _shard_region(ref, shape, by_cols, chip, rows=None):
    R, C = shape
    start, size = (0, R) if rows is None else rows
    if by_cols:
        return ref.at[pl.ds(start, size), pl.ds(chip * C, C)]
    return ref.at[pl.ds(chip * R + start, size), :]


def _gather_weights(entries):
    n = len(entries)
    shapes = [e[0].shape[1:] for e in entries]

    def places(ins, outs, sems):
        send_sems, recv_sems, local_sems = sems
        x, y, c = lax.axis_index("x"), lax.axis_index("y"), lax.axis_index("c")

        def landing(f, px, py, pc):
            R = shapes[f][0]
            return _shard_region(outs[f], shapes[f], entries[f][2], 2 * px + py, rows=(pc * (R // 2), R // 2))

        def copy(f, k, block, to, src=None):
            dst = landing(f, *block)
            return pltpu.make_async_remote_copy(
                src_ref=dst if src is None else src, dst_ref=dst,
                send_sem=send_sems.at[6 * f + k], recv_sem=recv_sems.at[6 * f + k],
                device_id=to, device_id_type=MESH)

        def mine(f):
            dst = _shard_region(outs[f], shapes[f], entries[f][2], 2 * x + y)
            return pltpu.make_async_copy(ins[f].at[entries[f][1]], dst, local_sems.at[f])

        def first(f, j):
            R = shapes[f][0]
            src = ins[f].at[entries[f][1], pl.ds(c * (R // 2), R // 2), :]
            return copy(f, j, (x, y, c), (*_other_chips(x, y)[j], c), src=src)

        return x, y, c, copy, mine, first

    def start(ins, outs, sems):
        _, _, _, _, mine, first = places(ins, outs, sems)
        for f in range(n):
            mine(f).start()
        for j in range(3):
            for f in range(n):
                first(f, j).start()

    def mid(ins, outs, sems):
        x, y, c, copy, _, _ = places(ins, outs, sems)
        for j, chip in enumerate(_other_chips(x, y)):
            for f in range(n):
                copy(f, j, (*chip, c), (x, y, c)).wait_recv()
                copy(f, 3 + j, (*chip, c), (x, y, 1 - c)).start()

    def end(ins, outs, sems):
        x, y, c, copy, mine, first = places(ins, outs, sems)
        for j, chip in enumerate(_other_chips(x, y)):
            for f in range(n):
                copy(f, 3 + j, (*chip, 1 - c), (x, y, c)).wait_recv()
        for j, chip in enumerate(_other_chips(x, y)):
            for f in range(n):
                first(f, j).wait_send()
                copy(f, 3 + j, (*chip, c), (x, y, 1 - c)).wait_send()
        for f in range(n):
            mine(f).wait()

    def whole(f):
        R, C = shapes[f]
        return (R, N_CHIPS * C) if entries[f][2] else (N_CHIPS * R, C)

    return _Comm(
        [e[0] for e in entries], [jax.ShapeDtypeStruct(whole(f), BF16) for f in range(n)],
        [pltpu.SemaphoreType.DMA((6 * n,)), pltpu.SemaphoreType.DMA((6 * n,)), pltpu.SemaphoreType.DMA((n,))],
        start, end, mid)


def _scatter_grads(entries):
    n = len(entries)

    def copies(ins, outs, sems):
        send_sems, recv_sems, local_sems = sems
        x, y, c = lax.axis_index("x"), lax.axis_index("y"), lax.axis_index("c")
        me = 2 * x + y

        def piece(f, chip):
            return _shard_region(ins[f], entries[f][1], entries[f][2], chip)

        mine = [pltpu.make_async_copy(piece(f, me), outs[f].at[me], local_sems.at[f]) for f in range(n)]
        sends = [pltpu.make_async_remote_copy(
            src_ref=piece(f, 2 * px + py), dst_ref=outs[f].at[me],
            send_sem=send_sems.at[3 * f + j], recv_sem=recv_sems.at[3 * f + j],
            device_id=(px, py, c), device_id_type=MESH)
            for j, (px, py) in enumerate(_other_chips(x, y)) for f in range(n)]
        return mine, sends

    def start(ins, outs, sems):
        mine, sends = copies(ins, outs, sems)
        for cp in mine + sends:
            cp.start()

    def end(ins, outs, sems):
        mine, sends = copies(ins, outs, sems)
        for cp in sends + mine:
            cp.wait()

    return _Comm(
        [e[0] for e in entries], [jax.ShapeDtypeStruct((N_CHIPS,) + tuple(e[1]), BF16) for e in entries],
        [pltpu.SemaphoreType.DMA((3 * n,)), pltpu.SemaphoreType.DMA((3 * n,)), pltpu.SemaphoreType.DMA((n,))],
        start, end)


def _exchange_sibling(parts):
    n = len(parts)

    def copies(ins, outs, sems):
        send_sems, recv_sems = sems
        sibling = (lax.axis_index("x"), lax.axis_index("y"), 1 - lax.axis_index("c"))
        return [pltpu.make_async_remote_copy(src_ref=ins[i], dst_ref=outs[i], send_sem=send_sems.at[i],
                                             recv_sem=recv_sems.at[i], device_id=sibling, device_id_type=MESH)
                for i in range(n)]

    def start(ins, outs, sems):
        for cp in copies(ins, outs, sems):
            cp.start()

    def end(ins, outs, sems):
        for cp in copies(ins, outs, sems):
            cp.wait()

    return _Comm(parts, [jax.ShapeDtypeStruct(s.shape, s.dtype) for s in parts],
                 [pltpu.SemaphoreType.DMA((n,)), pltpu.SemaphoreType.DMA((n,))], start, end)


def _allgather_small(block):
    m_per, ncol = block.shape

    def places(ins, outs, sems):
        send_sems, recv_sems, local_sem = sems
        x, y, c = lax.axis_index("x"), lax.axis_index("y"), lax.axis_index("c")

        def rows(px, py, pc):
            return outs[0].at[4 * px + 2 * py + pc]

        def copy(k, block_of, to, src=None):
            return pltpu.make_async_remote_copy(
                src_ref=rows(*block_of) if src is None else src, dst_ref=rows(*block_of),
                send_sem=send_sems.at[k], recv_sem=recv_sems.at[k], device_id=to, device_id_type=MESH)

        mine = pltpu.make_async_copy(ins[0], rows(x, y, c), local_sem.at[0])
        first = [copy(0, (x, y, c), (x, y, 1 - c), src=ins[0])]
        first += [copy(1 + j, (x, y, c), (*chip, c), src=ins[0]) for j, chip in enumerate(_other_chips(x, y))]
        passed = [copy(4 + j, (*chip, c), (x, y, 1 - c)) for j, chip in enumerate(_other_chips(x, y))]
        return x, y, c, copy, mine, first, passed

    def start(ins, outs, sems):
        _, _, _, _, mine, first, _ = places(ins, outs, sems)
        for cp in [mine] + first:
            cp.start()

    def mid(ins, outs, sems):
        x, y, c, copy, _, _, passed = places(ins, outs, sems)
        for j, chip in enumerate(_other_chips(x, y)):
            copy(1 + j, (*chip, c), (x, y, c)).wait_recv()
            passed[j].start()

    def end(ins, outs, sems):
        x, y, c, copy, mine, first, passed = places(ins, outs, sems)
        copy(0, (x, y, 1 - c), (x, y, c)).wait_recv()
        for j, chip in enumerate(_other_chips(x, y)):
            copy(4 + j, (*chip, 1 - c), (x, y, c)).wait_recv()
        for cp in first + passed:
            cp.wait_send()
        mine.wait()

    return _Comm([block], [jax.ShapeDtypeStruct((N_DEV, m_per, ncol), block.dtype)],
                 [pltpu.SemaphoreType.DMA((7,)), pltpu.SemaphoreType.DMA((7,)), pltpu.SemaphoreType.DMA((1,))],
                 start, end, mid)


def _join_comms(*progs):
    def split(parts, counts):
        out, pos = [], 0
        for n in counts:
            out.append(parts[pos:pos + n])
            pos += n
        return out

    def phase(which):
        def run(ins, outs, sems):
            args = zip(split(ins, [len(p.inputs) for p in progs]), split(outs, [len(p.out_shapes) for p in progs]),
                       split(sems, [len(p.scratch) for p in progs]))
            for p, (i, o, s) in zip(progs, args):
                fn = getattr(p, which)
                if fn is not None:
                    fn(i, o, s)
        return run

    return _Comm([a for p in progs for a in p.inputs], [s for p in progs for s in p.out_shapes],
                 [s for p in progs for s in p.scratch], phase("start"), phase("end"), phase("mid"))


def _adamw(w, g, m, v):
    m = ADAM_B1 * m + (1.0 - ADAM_B1) * g
    v = ADAM_B2 * v + (1.0 - ADAM_B2) * jnp.square(g)
    m_hat = m / (1.0 - ADAM_B1 ** ADAM_STEP)
    v_hat = v / (1.0 - ADAM_B2 ** ADAM_STEP)
    delta = -ADAM_LR * (m_hat / (jnp.sqrt(v_hat) + ADAM_EPS) + ADAM_WD * w)
    return delta, m, v


def _flat_tile(rows):
    return min(rows, 256)


def _sum_pieces(name, layers):
    L = len(layers)
    P, R, C = layers[0].shape
    tr = min(R, 2 * _flat_tile(R))

    def body(*refs):
        out_ref = refs[L]
        for l in range(L):
            @pl.when(pl.program_id(0) == l)
            def _(p_ref=refs[l]):
                acc = p_ref[0].astype(F32)
                for j in range(1, P):
                    acc = acc + p_ref[j].astype(F32)
                out_ref[...] = acc

    return pl.pallas_call(
        body, name=name, grid=(L, R // tr),
        in_specs=[pl.BlockSpec((P, tr, C), lambda l, i: (0, i, 0)) for _ in range(L)],
        out_specs=pl.BlockSpec((None, tr, C), lambda l, i: (l, i, 0)),
        out_shape=jax.ShapeDtypeStruct((L, R, C), F32),
        compiler_params=_params(dimension_semantics=("parallel", "parallel")),
    )(*layers)


def _adam_pair(name, w, m, v, part_a, part_b):
    L, R, C = w.shape
    tr = _flat_tile(R)

    def body(w_ref, m_ref, v_ref, a_ref, b_ref, g_ref, d_ref, nm_ref, nv_ref):
        g = a_ref[...] + b_ref[...]
        g_ref[...] = g
        d_ref[...], nm_ref[...], nv_ref[...] = _adamw(w_ref[...], g, m_ref[...], v_ref[...])

    row = pl.BlockSpec((None, tr, C), lambda l, i: (l, i, 0))
    return pl.pallas_call(
        body, name=name, grid=(L, R // tr),
        in_specs=[row] * 5, out_specs=[row] * 4,
        out_shape=[jax.ShapeDtypeStruct((L, R, C), F32)] * 4,
        compiler_params=_params(dimension_semantics=("parallel", "parallel")),
    )(w, m, v, part_a, part_b)


def _adam_small(w, m, v, gathered):
    R, C = w.shape

    def body(w_ref, m_ref, v_ref, p_ref, g_ref, d_ref, nm_ref, nv_ref):
        g = p_ref[0]
        for j in range(1, N_DEV):
            g = g + p_ref[j]
        g_ref[...] = g
        d_ref[...], nm_ref[...], nv_ref[...] = _adamw(w_ref[...], g, m_ref[...], v_ref[...])

    return pl.pallas_call(
        body, name="adam_small",
        out_shape=[jax.ShapeDtypeStruct((R, C), F32)] * 4,
        compiler_params=_params(),
    )(w, m, v, gathered)


SMALL = ("mix_norm_g", "mlp_norm_g", "final_norm_g", "a_ln_g", "a_ln_b", "a_w_s", "a_b_s", "rel_bias")


def _pack_small(arrays, width):
    rows = []
    for a in arrays:
        flat = a.reshape(-1)
        pad = (-flat.shape[0]) % width
        rows.append(jnp.pad(flat, (0, pad)).reshape(-1, width))
    block = jnp.concatenate(rows, axis=0)
    return jnp.pad(block, ((0, (-block.shape[0]) % 8), (0, 0)))


def _unpack_small(block, shapes, width):
    out, row = [], 0
    for shape in shapes:
        size = int(np.prod(shape))
        nrows = -(-size // width)
        out.append(block[row:row + nrows].reshape(-1)[:size].reshape(shape))
        row += nrows
    return out


def kernel(x, mix_norm_g, mlp_norm_g, final_norm_g, a_w_in, a_ln_g, a_ln_b, a_w_s, a_b_s, a_w_out, b_w_qkv, b_w_out, rel_bias, w_up, w_down, loss_target, m_mix_norm_g, m_mlp_norm_g, m_final_norm_g, m_a_w_in, m_a_ln_g, m_a_ln_b, m_a_w_s, m_a_b_s, m_a_w_out, m_b_w_qkv, m_b_w_out, m_rel_bias, m_w_up, m_w_down, v_mix_norm_g, v_mlp_norm_g, v_final_norm_g, v_a_w_in, v_a_ln_g, v_a_ln_b, v_a_w_s, v_a_b_s, v_a_w_out, v_b_w_qkv, v_b_w_out, v_rel_bias, v_w_up, v_w_down):
    T, D = x.shape[1], x.shape[2]
    h0 = x.reshape(T, D)
    target = loss_target.reshape(T, D)
    G = a_w_s.shape[1]

    w_big = [a_w_in, a_w_out, b_w_qkv, b_w_out, w_up, w_down]
    m_big = [m_a_w_in, m_a_w_out, m_b_w_qkv, m_b_w_out, m_w_up, m_w_down]
    v_big = [v_a_w_in, v_a_w_out, v_b_w_qkv, v_b_w_out, v_w_up, v_w_down]
    by_cols = [True, False, True, True, True, False]
    s_in, s_out, s_qkv, s_bo, s_up, s_dn = [w.astype(BF16) for w in w_big]
    (W_in,) = _run_comm("gather_a", _gather_weights([(s_in, 0, True)]))

    tril = jnp.tril(jnp.ones((CHUNK, CHUNK), dtype=bool))
    w_tril = jnp.where(tril[None], a_w_s[0], 0.0).astype(BF16)
    w_tril_t = jnp.swapaxes(w_tril, 1, 2)
    b_rows = jnp.broadcast_to(a_b_s[0][:, :, None], (G, CHUNK, CHUNK))
    buckets = _bucket_maps()
    bias = _bias_build(rel_bias, buckets)

    QKV = s_qkv.shape[2] * N_CHIPS
    TM = 1024
    TK_WGRAD = 4096

    def matmul(name, a, b, mode, out, tm=TM, tn=1024, **kw):
        outs = out if isinstance(out, list) else [out]
        return _mm(name, a, b, mode, tm=tm, tn=tn, tk=a.shape[1], outs=outs, **kw)

    def norm_bwd(layer_gain, h, dres, copies=2):
        return dict(epi=_epi_rms_bwd(copies), extras=(h, dres), vecs=(layer_gain,), col_sums=1)

    def wgrad(name, a, b, tn=1024, tk=TK_WGRAD, comm=None):
        return _mm(name, a, b, "tn", tm=1024, tn=tn, tk=tk, outs=[BF16], comm=comm)

    def scatter(*which):
        return _scatter_grads([(g, w_big[i].shape[1:], by_cols[i]) for g, i in which])

    (a_pre, y0), (W_out, W_up0) = matmul("a_in", h0, W_in, "nn", BF16, norm_gain=mix_norm_g[0:1],
                                         comm=_gather_weights([(s_out, 0, False), (s_up, 0, True)]))
    z = _gate_fwd(a_pre, a_ln_g, a_ln_b, w_tril, b_rows)
    h1 = matmul("a_out", z, W_out, "nn", F32, epi=_epi_residual, extras=(h0,))
    (q1, y1), (W_dn0,) = matmul("mlp_up0", h1, W_up0, "nn", BF16, epi=_epi_relu2, norm_gain=mlp_norm_g[0:1],
                                comm=_gather_weights([(s_dn, 0, False)]))
    h2, (W_qkv, W_bo) = matmul("mlp_down0", q1, W_dn0, "nn", F32, tm=TM // 2, epi=_epi_residual, extras=(h1,),
                               comm=_gather_weights([(s_qkv, 0, True), (s_bo, 0, True)]))
    (qkv, y2), (W_up1,) = matmul("b_qkv", h2, W_qkv, "nn", BF16, tn=QKV // 4, norm_gain=mix_norm_g[1:2],
                                 comm=_gather_weights([(s_up, 1, True)]))
    o, lse = _attention_fwd(qkv, bias)
    h3 = matmul("b_out", o, W_bo, "nn", F32, epi=_epi_residual, extras=(h2,))
    (q3, y3), (W_dn1,) = matmul("mlp_up1", h3, W_up1, "nn", BF16, epi=_epi_relu2, norm_gain=mlp_norm_g[1:2],
                                comm=_gather_weights([(s_dn, 1, False)]))
    dh4, dh4_b, d_final_g, loss_row = matmul("mlp_down1", q3, W_dn1, "nn", [F32, BF16], tm=TM // 2, epi=_epi_loss_head,
                                             extras=(h3, target), vecs=(final_norm_g.reshape(1, D),), col_sums=2)

    dp3 = matmul("mlp_down_bwd1", dh4_b, W_dn1, "nt", BF16, epi=_epi_relu2_grad, extras=(q3,))
    g_dn1 = wgrad("mlp_down_wgrad1", q3, dh4_b)
    g_up1, (r_dn1,) = wgrad("mlp_up_wgrad1", y3, dp3, comm=scatter((g_dn1, 5)))
    (dh3, dh3_b, dg_mlp1), (r_up1,) = matmul("mlp_up_bwd1", dp3, W_up1, "nt", [F32, BF16], tm=TM // 2,
                                             comm=scatter((g_up1, 4)), **norm_bwd(mlp_norm_g[1:2], h3, dh4))
    do = matmul("b_out_bwd", dh3_b, W_bo, "nt", BF16)
    g_bo = wgrad("b_out_wgrad", o, dh3_b)
    dqkv, dbias = _attention_bwd(qkv, do, o, lse, bias)
    d_rel_bias = _bias_scatter(dbias, buckets)
    (dh2, dh2_b, dg_mix1), (r_bo,) = matmul("b_qkv_bwd", dqkv, W_qkv, "nt", [F32, BF16], tm=TM // 2,
                                            comm=scatter((g_bo, 3)), **norm_bwd(mix_norm_g[1:2], h2, dh3))
    g_qkv = wgrad("b_qkv_wgrad", y2, dqkv, tn=QKV // 3, tk=TK_WGRAD // 2)
    dp1, (r_qkv,) = matmul("mlp_down_bwd0", dh2_b, W_dn0, "nt", BF16, epi=_epi_relu2_grad, extras=(q1,),
                           comm=scatter((g_qkv, 2)))
    g_up0 = wgrad("mlp_up_wgrad0", y1, dp1)
    g_dn0, (r_up0,) = wgrad("mlp_down_wgrad0", q1, dh2_b, comm=scatter((g_up0, 4)))
    (dh1, dh1_b, dg_mlp0), (r_dn0,) = matmul("mlp_up_bwd0", dp1, W_up0, "nt", [F32, BF16], tm=TM // 2,
                                             comm=scatter((g_dn0, 5)), **norm_bwd(mlp_norm_g[0:1], h1, dh2))
    dz = matmul("a_out_bwd", dh1_b, W_out, "nt", F32)
    g_out = wgrad("a_out_wgrad", z, dh1_b)
    da, d_ln_g, d_ln_b, d_w_s, d_b_s = _gate_bwd(a_pre, dz, a_ln_g, a_ln_b, w_tril, w_tril_t, b_rows)
    g_in, (r_out,) = wgrad("a_in_wgrad", y0, da, comm=scatter((g_out, 1)))
    grad_x, dg_mix0 = matmul("a_in_bwd", da, W_in, "nt", F32, **norm_bwd(mix_norm_g[0:1], h0, dh1, copies=1))

    unused = jnp.zeros((1, 1), F32)
    small_w = [mix_norm_g, mlp_norm_g, final_norm_g, a_ln_g, a_ln_b, a_w_s, a_b_s, rel_bias, unused]
    small_m = [m_mix_norm_g, m_mlp_norm_g, m_final_norm_g, m_a_ln_g, m_a_ln_b, m_a_w_s, m_a_b_s, m_rel_bias, unused]
    small_v = [v_mix_norm_g, v_mlp_norm_g, v_final_norm_g, v_a_ln_g, v_a_ln_b, v_a_w_s, v_a_b_s, v_rel_bias, unused]
    small_g = [jnp.concatenate([dg_mix0, dg_mix1]), jnp.concatenate([dg_mlp0, dg_mlp1]), d_final_g,
               d_ln_g, d_ln_b, d_w_s[None], d_b_s[None, :, :, 0], d_rel_bias, loss_row[:, :1]]
    width = max(D, 128)
    received = [None, [r_out], [r_qkv], [r_bo], [r_up0, r_up1], [r_dn0, r_dn1]]
    plane = [None] + [_sum_pieces(f"sum_pieces{i}", received[i]) for i in range(1, len(w_big))]
    tail = _run_comm("tail_comm", _join_comms(scatter((g_in, 0)), _exchange_sibling(plane[1:]),
                                              _allgather_small(_pack_small(small_g, width))))
    r_in, other, gathered_small = tail[0], [None] + list(tail[1:len(w_big)]), tail[len(w_big)]
    plane[0] = _sum_pieces("sum_pieces0", [r_in])
    (other[0],) = _run_comm("exchange_a_in", _exchange_sibling([plane[0]]))
    big_out = [_adam_pair(f"adam{i}", w_big[i], m_big[i], v_big[i], plane[i], other[i]) for i in range(len(w_big))]

    def unbig(kind):
        return dict(zip(["a_w_in", "a_w_out", "b_w_qkv", "b_w_out", "w_up", "w_down"], [b[kind] for b in big_out]))

    small_out = _adam_small(_pack_small(small_w, width), _pack_small(small_m, width), _pack_small(small_v, width),
                            gathered_small)
    shapes = [w.shape for w in small_w]
    loss = _unpack_small(small_out[0], shapes, width)[-1][0, 0]

    names = ["mix_norm_g", "mlp_norm_g", "final_norm_g", "a_w_in", "a_ln_g", "a_ln_b", "a_w_s", "a_b_s", "a_w_out",
             "b_w_qkv", "b_w_out", "rel_bias", "w_up", "w_down"]
    results = [loss, grad_x.reshape(x.shape)]
    for kind in range(4):
        table = dict(zip(SMALL, _unpack_small(small_out[kind], shapes, width)))
        table.update(unbig(kind))
        results += [table[n] for n in names]
    return tuple(results)
```

```python
import functools
import math

import numpy as np
import jax
import jax.numpy as jnp
from jax import lax
from jax.experimental import pallas as pl
from jax.experimental.pallas import tpu as pltpu

F32 = jnp.float32
BF16 = jnp.bfloat16
MESH = pl.DeviceIdType.MESH
ANY = pl.BlockSpec(memory_space=pl.ANY)

N_CHIPS = 4
N_DEV = 8
VMEM_LIMIT_BYTES = 56 * 1024 * 1024

EPS = 1e-6
NEG_INF = -1e30
CHUNK = 128
GROUP_DIM = 128
HEAD_DIM = 64
ATT_HEADS = 8
ATT_WIDTH = ATT_HEADS * HEAD_DIM
PAIR = 2 * HEAD_DIM
BLK = 128
DILATIONS = (1, 4, 16)
N_BUCKETS = 32
MAX_EXACT = N_BUCKETS // 2
REL_MAX_DISTANCE = 2048

ADAM_LR = 0.001
ADAM_B1 = 0.9
ADAM_B2 = 0.999
ADAM_EPS = 1e-08
ADAM_WD = 0.01
ADAM_STEP = 10

NN = (((1,), (0,)), ((), ()))
NT = (((1,), (1,)), ((), ()))
TN = (((0,), (0,)), ((), ()))


def _params(**kw):
    return pltpu.CompilerParams(vmem_limit_bytes=VMEM_LIMIT_BYTES, **kw)


def _dot(a, b, dims=NN):
    return lax.dot_general(a, b, dims, preferred_element_type=F32)


def _gelu(x):
    return 0.5 * x * (1.0 + lax.erf(x * math.sqrt(0.5)))


def _gelu_grad(x):
    return 0.5 * (1.0 + lax.erf(x * math.sqrt(0.5))) + x * jnp.exp(-0.5 * x * x) * (1.0 / math.sqrt(2.0 * math.pi))


def _mean(x):
    return jnp.mean(x, axis=-1, keepdims=True)


class _Comm:
    def __init__(self, inputs, out_shapes, scratch, start, end, mid=None):
        self.inputs, self.out_shapes, self.scratch = list(inputs), list(out_shapes), list(scratch)
        self.start, self.mid, self.end = start, mid, end


def _run_comm(name, comm):
    n_in, n_out = len(comm.inputs), len(comm.out_shapes)

    def body(*refs):
        parts = refs[:n_in], refs[n_in:n_in + n_out], refs[n_in + n_out:]
        comm.start(*parts)
        if comm.mid is not None:
            comm.mid(*parts)
        comm.end(*parts)

    return pl.pallas_call(
        body, name=name, in_specs=[ANY] * n_in, out_specs=[ANY] * n_out, out_shape=comm.out_shapes,
        scratch_shapes=comm.scratch, compiler_params=_params(),
    )(*comm.inputs)


def _mm(name, a, b, mode, *, tm, tn, tk, outs, epi=None, extras=(), vecs=(), col_sums=0, norm_gain=None, comm=None):
    if mode == "tn":
        K, M = a.shape
    else:
        M, K = a.shape
    N = b.shape[0] if mode == "nt" else b.shape[1]
    tm, tn, tk = min(tm, M), min(tn, N), min(tk, K)
    assert M % tm == 0 and N % tn == 0 and K % tk == 0, (name, M, N, K, tm, tn, tk)
    nk = K // tk
    grid = (M // tm, N // tn, nk)

    if mode == "tn":
        a_spec = pl.BlockSpec((tk, tm), lambda i, j, k: (k, i))
    else:
        a_spec = pl.BlockSpec((tm, tk), lambda i, j, k: (i, k))
    if mode == "nt":
        b_spec = pl.BlockSpec((tn, tk), lambda i, j, k: (j, k))
    else:
        b_spec = pl.BlockSpec((tk, tn), lambda i, j, k: (k, j))
    tile = pl.BlockSpec((tm, tn), lambda i, j, k: (i, j))
    vec = pl.BlockSpec((1, tn), lambda i, j, k: (0, j))
    normed = norm_gain is not None
    assert not normed or (mode == "nn" and nk == 1 and tm % grid[1] == 0)
    assert col_sums == 0 or grid[1] == 1
    out_shapes = [jax.ShapeDtypeStruct((M, N), dtype) for dtype in outs]
    out_specs = [tile for _ in outs]
    extra_specs = [tile for _ in extras] + [vec for _ in vecs]
    if normed:
        part_rows = tm // grid[1]
        last_part = M // part_rows - 1
        a_spec = pl.BlockSpec((tm, K), lambda i, j, k: (0, 0))
        out_shapes.append(jax.ShapeDtypeStruct((M, K), BF16))
        out_specs.append(pl.BlockSpec((part_rows, K), lambda i, j, k: (i * grid[1] + j, 0)))
        extra_specs.append(pl.BlockSpec((1, K), lambda i, j, k: (0, 0)))
        extra_specs.append(pl.BlockSpec((part_rows, K),
                                        lambda i, j, k: (jnp.minimum((i + 1) * grid[1] + j, last_part), 0)))
    out_shapes += [jax.ShapeDtypeStruct((1, N), F32)] * col_sums
    out_specs += [vec] * col_sums
    n_extra, n_out = len(extra_specs), len(out_shapes)
    n_tiles = len(outs)
    n_cin = len(comm.inputs) if comm else 0
    n_cout = len(comm.out_shapes) if comm else 0
    dims = {"nn": NN, "nt": NT, "tn": TN}[mode]
    steps = grid[0] * grid[1] * grid[2]

    def body(*refs):
        a_ref, b_ref = refs[0], refs[1]
        pos = 2
        extra_refs = refs[pos:pos + n_extra]
        pos += n_extra
        comm_in = refs[pos:pos + n_cin]
        pos += n_cin
        out_refs = refs[pos:pos + n_out]
        pos += n_out
        comm_out = refs[pos:pos + n_cout]
        pos += n_cout
        acc_ref = refs[pos] if nk > 1 else None
        pos += nk > 1
        y_refs = refs[pos:pos + 2 * normed]
        comm_sems = refs[pos + 2 * normed:]
        k = pl.program_id(2)
        step = (pl.program_id(0) * grid[1] + pl.program_id(1)) * nk + k

        if comm is not None:
            @pl.when(step == 0)
            def _():
                comm.start(comm_in, comm_out, comm_sems)

        def finish(acc):
            epi_args = [e[...] for e in extra_refs[:n_extra - 2 * normed]]
            res = epi(acc, *epi_args) if epi is not None else (acc,) * n_tiles
            for o, r in zip(out_refs[:n_tiles], res[:n_tiles]):
                o[...] = r.astype(o.dtype)
            if col_sums:
                sums = out_refs[n_out - col_sums:]

                @pl.when(pl.program_id(0) == 0)
                def _():
                    for o in sums:
                        o[...] = jnp.zeros_like(o)

                for o, r in zip(sums, res[n_tiles:]):
                    o[...] += r

        if normed:
            gain_ref, ahead_ref = extra_refs[-2], extra_refs[-1]

            def norm(hv):
                return (hv * lax.rsqrt(_mean(hv * hv) + EPS) * gain_ref[...]).astype(BF16)

            @pl.when(step == 0)
            def _():
                y_refs[0][...] = norm(a_ref[...])

            part_at = pl.ds(pl.multiple_of(pl.program_id(1) * part_rows, part_rows), part_rows)
            for parity in range(2):
                @pl.when(pl.program_id(0) % 2 == parity)
                def _(y_now=y_refs[parity], y_next=y_refs[1 - parity]):
                    finish(_dot(y_now[...], b_ref[...].astype(BF16), dims))
                    out_refs[n_tiles][...] = y_now[part_at, :]
                    y_next[part_at, :] = norm(ahead_ref[...])
        elif nk == 1:
            finish(_dot(a_ref[...].astype(BF16), b_ref[...].astype(BF16), dims))
        else:
            part = _dot(a_ref[...].astype(BF16), b_ref[...].astype(BF16), dims)

            @pl.when(k == 0)
            def _():
                acc_ref[...] = part

            @pl.when(k > 0)
            def _():
                acc_ref[...] += part

            @pl.when(k == nk - 1)
            def _():
                finish(acc_ref[...])

        if comm is not None:
            if comm.mid is not None:
                @pl.when(step == (3 * steps) // 4)
                def _():
                    comm.mid(comm_in, comm_out, comm_sems)

            @pl.when(step == steps - 1)
            def _():
                comm.end(comm_in, comm_out, comm_sems)

    sequential = comm is not None or normed or col_sums > 0
    order = ("arbitrary",) * 3 if sequential else ("parallel", "parallel", "arbitrary")
    scratch = [pltpu.VMEM((tm, tn), F32)] if nk > 1 else []
    if normed:
        scratch += [pltpu.VMEM((tm, K), BF16)] * 2
    res = pl.pallas_call(
        body, name=name, grid=grid,
        in_specs=[a_spec, b_spec] + extra_specs + [ANY] * n_cin,
        out_specs=out_specs + [ANY] * n_cout,
        out_shape=out_shapes + (comm.out_shapes if comm else []),
        scratch_shapes=scratch + (comm.scratch if comm else []),
        compiler_params=_params(dimension_semantics=order),
    )(a, b, *extras, *vecs, *([norm_gain, a] if normed else []), *(comm.inputs if comm else []))
    mm_out = res[0] if n_out == 1 else list(res[:n_out])
    return (mm_out, list(res[n_out:])) if comm else mm_out


def _epi_residual(acc, res):
    return (res + acc,)


def _epi_relu2(acc):
    return (jnp.square(jnp.maximum(acc, 0.0)),)


def _epi_rms_bwd(copies):
    def epi(acc, h, dres, g):
        r = lax.rsqrt(_mean(h * h) + EPS)
        hn = h * r
        dyg = acc * g
        dh = dres + r * (dyg - hn * _mean(dyg * hn))
        return (dh,) * copies + (jnp.sum(acc * hn, axis=0, keepdims=True),)
    return epi


def _epi_loss_head(acc, res, target, g):
    h = res + acc
    r = lax.rsqrt(_mean(h * h) + EPS)
    hn = h * r
    diff = hn * g - target
    loss = 0.5 * jnp.sum(_mean(diff * diff))
    dy = diff * (1.0 / h.shape[-1])
    dyg = dy * g
    dh = r * (dyg - hn * _mean(dyg * hn))
    return dh, dh, jnp.sum(dy * hn, axis=0, keepdims=True), jnp.full((1, h.shape[-1]), loss, F32)


def _epi_relu2_grad(acc, q):
    qf = q.astype(F32)
    return (acc * jnp.where(qf > 0.0, (2.0 * qf) * lax.rsqrt(qf), 0.0),)


def _row_tile(T):
    return min(T, 512)


def _gate_tile(T):
    return min(T, 256)


def _gate_fwd(a, ln_g, ln_b, w_tril, b_rows):
    T, W2 = a.shape
    W = W2 // 2
    G = W // GROUP_DIM
    tr = _gate_tile(T)

    def body(a_ref, lng_ref, lnb_ref, w_ref, b_ref, z_ref):
        u = _gelu(a_ref[:, :W].astype(F32))
        vg = _gelu(a_ref[:, W:].astype(F32))
        xc = vg - _mean(vg)
        vn = xc * lax.rsqrt(_mean(xc * xc) + EPS)
        vl = (vn * lng_ref[...] + lnb_ref[...]).astype(BF16)
        for n in range(tr // CHUNK):
            rows = slice(n * CHUNK, (n + 1) * CHUNK)
            for g in range(G):
                cols = slice(g * GROUP_DIM, (g + 1) * GROUP_DIM)
                gate = _dot(w_ref[g], vl[rows, cols]) + b_ref[g]
                z_ref[rows, cols] = (u[rows, cols] * gate).astype(BF16)

    vec = pl.BlockSpec((1, W), lambda i: (0, 0))
    grp = pl.BlockSpec((G, CHUNK, CHUNK), lambda i: (0, 0, 0))
    return pl.pallas_call(
        body, name="gate_fwd", grid=(T // tr,),
        in_specs=[pl.BlockSpec((tr, W2), lambda i: (i, 0)), vec, vec, grp, grp],
        out_specs=pl.BlockSpec((tr, W), lambda i: (i, 0)),
        out_shape=jax.ShapeDtypeStruct((T, W), BF16),
        compiler_params=_params(dimension_semantics=("parallel",)),
    )(a, ln_g, ln_b, w_tril, b_rows)


def _gate_bwd(a, dz, ln_g, ln_b, w_tril, w_tril_t, b_rows):
    T, W2 = a.shape
    W = W2 // 2
    G = W // GROUP_DIM
    tr = _gate_tile(T)
    steps = T // tr

    def body(a_ref, dz_ref, lng_ref, lnb_ref, w_ref, wt_ref, b_ref, da_ref, dlng_ref, dlnb_ref, dw_ref, dbs_ref, dvl_ref):
        step = pl.program_id(0)

        @pl.when(step == 0)
        def _():
            dlng_ref[...] = jnp.zeros_like(dlng_ref)
            dlnb_ref[...] = jnp.zeros_like(dlnb_ref)
            dw_ref[...] = jnp.zeros_like(dw_ref)
            dbs_ref[...] = jnp.zeros_like(dbs_ref)

        au = a_ref[:, :W].astype(F32)
        av = a_ref[:, W:].astype(F32)
        u = _gelu(au)
        vg = _gelu(av)
        xc = vg - _mean(vg)
        rstd = lax.rsqrt(_mean(xc * xc) + EPS)
        vn = xc * rstd
        lng = lng_ref[...]
        vl = (vn * lng + lnb_ref[...]).astype(BF16)
        du_scale = dz_ref[...] * _gelu_grad(au)
        dgate_all = dz_ref[...] * u
        for n in range(tr // CHUNK):
            rows = slice(n * CHUNK, (n + 1) * CHUNK)
            for g in range(G):
                cols = slice(g * GROUP_DIM, (g + 1) * GROUP_DIM)
                vlg = vl[rows, cols]
                gate = _dot(w_ref[g], vlg) + b_ref[g]
                da_ref[rows, cols] = (du_scale[rows, cols] * gate).astype(BF16)
                dgate = dgate_all[rows, cols]
                dbs_ref[g] += dgate
                dgate_b = dgate.astype(BF16)
                dw_ref[g] += _dot(dgate_b, vlg, NT)
                dvl_ref[rows, cols] = _dot(wt_ref[g], dgate_b)
        dvl = dvl_ref[...]
        dlnb_ref[...] += jnp.sum(dvl, axis=0, keepdims=True)
        dlng_ref[...] += jnp.sum(dvl * vn, axis=0, keepdims=True)
        dvn = dvl * lng
        dvg = rstd * (dvn - _mean(dvn) - vn * _mean(dvn * vn))
        da_ref[:, W:] = (dvg * _gelu_grad(av)).astype(BF16)

        @pl.when(step == steps - 1)
        def _():
            t_idx = lax.broadcasted_iota(jnp.int32, (CHUNK, CHUNK), 0)
            s_idx = lax.broadcasted_iota(jnp.int32, (CHUNK, CHUNK), 1)
            for g in range(G):
                dw_ref[g] = jnp.where(s_idx <= t_idx, dw_ref[g], 0.0)
                dbs_ref[g] = jnp.broadcast_to(jnp.sum(dbs_ref[g], axis=-1, keepdims=True), (CHUNK, CHUNK))

    vec = pl.BlockSpec((1, W), lambda i: (0, 0))
    grp = pl.BlockSpec((G, CHUNK, CHUNK), lambda i: (0, 0, 0))
    return pl.pallas_call(
        body, name="gate_bwd", grid=(steps,),
        in_specs=[pl.BlockSpec((tr, W2), lambda i: (i, 0)), pl.BlockSpec((tr, W), lambda i: (i, 0)),
                  vec, vec, grp, grp, grp],
        out_specs=[pl.BlockSpec((tr, W2), lambda i: (i, 0)), vec, vec, grp, grp],
        out_shape=[jax.ShapeDtypeStruct((T, W2), BF16), jax.ShapeDtypeStruct((1, W), F32),
                   jax.ShapeDtypeStruct((1, W), F32), jax.ShapeDtypeStruct((G, CHUNK, CHUNK), F32),
                   jax.ShapeDtypeStruct((G, CHUNK, CHUNK), F32)],
        scratch_shapes=[pltpu.VMEM((tr, W), F32)],
        compiler_params=_params(dimension_semantics=("arbitrary",)),
    )(a, dz, ln_g, ln_b, w_tril, w_tril_t, b_rows)


def _bucket_map(dilation):
    rel = BLK + np.arange(BLK)[:, None] - np.arange(2 * BLK)[None, :]
    dist = np.clip(rel, 0, BLK) * dilation
    nf = np.maximum(dist, 1).astype(np.float32)
    large = MAX_EXACT + (np.log(nf / np.float32(MAX_EXACT)) / np.float32(math.log(REL_MAX_DISTANCE / MAX_EXACT))
                         * np.float32(N_BUCKETS - MAX_EXACT)).astype(np.int32)
    large = np.minimum(large, N_BUCKETS - 1)
    return np.where(dist < MAX_EXACT, dist, large).astype(np.int32)


def _bucket_maps():
    return jnp.asarray(np.stack([_bucket_map(d) for d in DILATIONS]))


def _bias_build(rel_bias, buckets):
    NG = len(DILATIONS)

    def body(table_ref, bucket_ref, out_ref):
        for g in range(NG):
            bk = bucket_ref[g]
            for h in range(ATT_HEADS):
                out_ref[0, g, h] = jnp.zeros((BLK, 2 * BLK), F32)
            for b in range(N_BUCKETS):
                hit = bk == b
                for h in range(ATT_HEADS):
                    out_ref[0, g, h] = jnp.where(hit, table_ref[b, g * ATT_HEADS + h], out_ref[0, g, h])
            for h in range(ATT_HEADS):
                for first in range(2):
                    out_ref[first, g, h] = jnp.where(_window_mask(first), out_ref[0, g, h], NEG_INF)

    return pl.pallas_call(
        body, name="bias_build",
        in_specs=[pl.BlockSpec(memory_space=pltpu.SMEM), pl.BlockSpec(memory_space=pltpu.VMEM)],
        out_specs=pl.BlockSpec(memory_space=pltpu.VMEM),
        out_shape=jax.ShapeDtypeStruct((2, NG, ATT_HEADS, BLK, 2 * BLK), F32),
        compiler_params=_params(),
    )(rel_bias, buckets)


def _bias_scatter(dbias, buckets):
    NG = len(DILATIONS)

    def body(dbias_ref, bucket_ref, out_ref):
        for g in range(NG):
            bk = bucket_ref[g]
            for b in range(N_BUCKETS):
                hit = bk == b
                for h in range(ATT_HEADS):
                    out_ref[b, g * ATT_HEADS + h] = jnp.sum(jnp.where(hit, dbias_ref[g, h], 0.0))

    return pl.pallas_call(
        body, name="bias_scatter",
        in_specs=[pl.BlockSpec(memory_space=pltpu.VMEM), pl.BlockSpec(memory_space=pltpu.VMEM)],
        out_specs=pl.BlockSpec(memory_space=pltpu.SMEM),
        out_shape=jax.ShapeDtypeStruct((N_BUCKETS, NG * ATT_HEADS), F32),
        compiler_params=_params(),
    )(dbias, buckets)


def _window_mask(first):
    qi = lax.broadcasted_iota(jnp.int32, (BLK, 2 * BLK), 0)
    kj = lax.broadcasted_iota(jnp.int32, (BLK, 2 * BLK), 1)
    rel = BLK + qi - kj
    return (rel >= 0) & (rel <= BLK) & (kj >= BLK * first)


def _head_lanes(hh):
    lane = lax.broadcasted_iota(jnp.int32, (1, PAIR), 1)
    return (lane >= hh * HEAD_DIM) & (lane < (hh + 1) * HEAD_DIM)


ATT_STEP_BLOCKS = 8


def _attn_steps(stride):
    per_step = math.gcd(stride, ATT_STEP_BLOCKS)
    return per_step, stride // per_step


def _attn_fwd(name, g, qkv, qc, kc, vc, bias, stride):
    T = qkv.shape[0]
    per_step, lag = _attn_steps(stride)
    rows = per_step * BLK
    scale = HEAD_DIM ** -0.5

    def body(q_ref, kp_ref, kc_ref, vp_ref, vc_ref, bias_ref, out_ref):
        first = (pl.program_id(0) < lag).astype(jnp.int32)
        low = _head_lanes(0)

        def block(j, carry):
            at = pl.ds(pl.multiple_of(j * BLK, BLK), BLK)
            for hp in range(ATT_HEADS // 2):
                cols = slice(hp * PAIR, (hp + 1) * PAIR)
                qp = q_ref[at, cols]
                kk = jnp.concatenate([kp_ref[at, cols], kc_ref[at, cols]], axis=0)
                vv = jnp.concatenate([vp_ref[at, cols], vc_ref[at, cols]], axis=0)
                o_h, lse_h = [], []
                for hh in range(2):
                    qm = jnp.where(_head_lanes(hh), qp, jnp.zeros_like(qp))
                    s = _dot(qm, kk, NT) * scale
                    logits = s + bias_ref[first, 2 * hp + hh]
                    m = jnp.max(logits, axis=-1, keepdims=True)
                    p = jnp.exp(logits - m)
                    den = jnp.sum(p, axis=-1, keepdims=True)
                    o_h.append(_dot(p.astype(BF16), vv) / den)
                    lse_h.append(m + jnp.log(den))
                out_ref[at, cols] = jnp.where(low, o_h[0], o_h[1])
                out_ref[at, slice(ATT_WIDTH + hp * PAIR, ATT_WIDTH + (hp + 1) * PAIR)] = (
                    jnp.where(low, lse_h[0], lse_h[1]))
            return carry

        lax.fori_loop(0, per_step, block, 0)

    def cur(c):
        return pl.BlockSpec((rows, ATT_WIDTH), lambda s: (s, c))

    def prev(c):
        return pl.BlockSpec((rows, ATT_WIDTH), lambda s: (jnp.maximum(s - lag, 0), c))

    return pl.pallas_call(
        body, name=name, grid=(T // rows,),
        in_specs=[cur(qc), prev(kc), cur(kc), prev(vc), cur(vc),
                  pl.BlockSpec((2, None, ATT_HEADS, BLK, 2 * BLK), lambda s: (0, g, 0, 0, 0))],
        out_specs=pl.BlockSpec((rows, 2 * ATT_WIDTH), lambda s: (s, 0)),
        out_shape=jax.ShapeDtypeStruct((T, 2 * ATT_WIDTH), F32),
        compiler_params=_params(dimension_semantics=("parallel",)),
    )(qkv, qkv, qkv, qkv, qkv, bias)


def _permute_f32(p, x):
    hi = x.astype(BF16)
    rest = x - hi.astype(F32)
    mid = rest.astype(BF16)
    low = (rest - mid.astype(F32)).astype(BF16)
    return _dot(p, hi) + _dot(p, mid) + _dot(p, low)


def _attn_merge(parts):
    T = parts[0].shape[0]
    rows = min(T, REORDER_ROWS)
    n = len(parts)
    ncol = ATT_WIDTH // PAIR

    def body(*refs):
        p_refs, o_refs, l_refs = refs[:n], refs[n:2 * n], refs[2 * n:3 * n]
        o_ref, lse_ref = refs[3 * n], refs[3 * n + 1]

        def positions(ref, g, start):
            d = DILATIONS[g]
            if d == 1:
                return ref[start:start + REORDER_TILE, :]
            span, per = BLK * d, REORDER_TILE // d
            base, t = start // span * span, start % span // REORDER_TILE
            chunks = [ref[base + r * BLK + t * per:base + r * BLK + (t + 1) * per, :] for r in range(d)]
            return _permute_f32(p_refs[g][...], jnp.concatenate(chunks, axis=0))

        for start in range(0, rows, REORDER_TILE):
            ls = [positions(l_refs[g], g, start) for g in range(n)]
            m = functools.reduce(jnp.maximum, ls)
            es = [jnp.exp(l - m) for l in ls]
            tot = functools.reduce(lambda x, y: x + y, es)
            acc = functools.reduce(lambda x, y: x + y, [e * positions(o_refs[g], g, start) for g, e in enumerate(es)])
            o_ref[start:start + REORDER_TILE, :] = (acc / tot).astype(BF16)
            lse_ref[start:start + REORDER_TILE, :] = m + jnp.log(tot)

    matrix = pl.BlockSpec((REORDER_TILE, REORDER_TILE), lambda w, c: (0, 0))
    col = pl.BlockSpec((rows, PAIR), lambda w, c: (w, c))
    col_lse = pl.BlockSpec((rows, PAIR), lambda w, c: (w, ncol + c))
    return pl.pallas_call(
        body, name="attn_merge", grid=(T // rows, ncol),
        in_specs=[matrix] * n + [col] * n + [col_lse] * n, out_specs=[col, col],
        out_shape=[jax.ShapeDtypeStruct((T, ATT_WIDTH), BF16), jax.ShapeDtypeStruct((T, ATT_WIDTH), F32)],
        compiler_params=_params(dimension_semantics=("parallel", "parallel")),
    )(*[_reorder_matrix(max(d, 2), True) for d in DILATIONS], *parts, *parts)


def _attn_bwd(name, g, qkv, qc, kc, vc, do, o, lse, bias, stride):
    T = qkv.shape[0]
    per_step, lag = _attn_steps(stride)
    rows = per_step * BLK
    steps = T // rows
    scale = HEAD_DIM ** -0.5

    def body(q_ref, kp_ref, kc_ref, vp_ref, vc_ref, do_ref, o_ref, lse_ref, bias_ref,
             dq_ref, dkv_ref, db_ref, carry_k, carry_v):
        step = pl.program_id(0)
        slot0 = (step % lag) * per_step

        @pl.when(step == 0)
        def _():
            db_ref[...] = jnp.zeros_like(db_ref)
            carry_k[...] = jnp.zeros_like(carry_k)
            carry_v[...] = jnp.zeros_like(carry_v)

        @pl.when(step >= steps)
        def _():
            def flush(j, carry):
                at = pl.ds(pl.multiple_of(j * BLK, BLK), BLK)
                dkv_ref[at, :ATT_WIDTH] = carry_k[slot0 + j].astype(BF16)
                dkv_ref[at, ATT_WIDTH:] = carry_v[slot0 + j].astype(BF16)
                return carry

            lax.fori_loop(0, per_step, flush, 0)

        @pl.when(step < steps)
        def _():
            first = (step < lag).astype(jnp.int32)

            def block(j, carry):
                at = pl.ds(pl.multiple_of(j * BLK, BLK), BLK)
                ck_ref = carry_k.at[slot0 + j]
                cv_ref = carry_v.at[slot0 + j]
                for hp in range(ATT_HEADS // 2):
                    cols = slice(hp * PAIR, (hp + 1) * PAIR)
                    qp = q_ref[at, cols]
                    kk = jnp.concatenate([kp_ref[at, cols], kc_ref[at, cols]], axis=0)
                    vv = jnp.concatenate([vp_ref[at, cols], vc_ref[at, cols]], axis=0)
                    dop = do_ref[at, cols]
                    lsep = lse_ref[at, cols]
                    prod = dop.astype(F32) * o_ref[at, cols].astype(F32)
                    dq = jnp.zeros((BLK, PAIR), F32)
                    dk = jnp.zeros((2 * BLK, PAIR), F32)
                    dv = jnp.zeros((2 * BLK, PAIR), F32)
                    for hh in range(2):
                        lanes = _head_lanes(hh)
                        qm = jnp.where(lanes, qp, jnp.zeros_like(qp))
                        dom = jnp.where(lanes, dop, jnp.zeros_like(dop))
                        km = jnp.where(lanes, kk, jnp.zeros_like(kk))
                        delta = jnp.sum(jnp.where(lanes, prod, 0.0), axis=-1, keepdims=True)
                        lse_h = jnp.max(jnp.where(lanes, lsep, NEG_INF), axis=-1, keepdims=True)
                        s = _dot(qm, kk, NT) * scale
                        logits = s + bias_ref[first, 2 * hp + hh]
                        p = jnp.exp(logits - lse_h)
                        dv += _dot(p.astype(BF16), dom, TN)
                        ds = p * (_dot(dom, vv, NT) - delta)
                        db_ref[2 * hp + hh] += ds
                        dss = (ds * scale).astype(BF16)
                        dq += _dot(dss, km)
                        dk += _dot(dss, qm, TN)
                    dq_ref[at, cols] = dq.astype(BF16)
                    dkv_ref[at, cols] = (ck_ref[:, cols] + dk[:BLK]).astype(BF16)
                    dkv_ref[at, slice(ATT_WIDTH + hp * PAIR, ATT_WIDTH + (hp + 1) * PAIR)] = (
                        cv_ref[:, cols] + dv[:BLK]).astype(BF16)
                    ck_ref[:, cols] = dk[BLK:]
                    cv_ref[:, cols] = dv[BLK:]
                return carry

            lax.fori_loop(0, per_step, block, 0)

    last = steps - 1

    def cur(c):
        return pl.BlockSpec((rows, ATT_WIDTH), lambda s: (jnp.minimum(s, last), c))

    def prev(c):
        return pl.BlockSpec((rows, ATT_WIDTH), lambda s: (jnp.clip(s - lag, 0, last), c))

    dbias_shape = (ATT_HEADS, BLK, 2 * BLK)
    return pl.pallas_call(
        body, name=name, grid=(steps + lag,),
        in_specs=[cur(qc), prev(kc), cur(kc), prev(vc), cur(vc), cur(0), cur(0), cur(0),
                  pl.BlockSpec((2, None, ATT_HEADS, BLK, 2 * BLK), lambda s: (0, g, 0, 0, 0))],
        out_specs=[cur(0), pl.BlockSpec((rows, 2 * ATT_WIDTH), lambda s: (jnp.clip(s - lag, 0, last), 0)),
                   pl.BlockSpec(dbias_shape, lambda s: (0, 0, 0))],
        out_shape=[jax.ShapeDtypeStruct((T, ATT_WIDTH), BF16), jax.ShapeDtypeStruct((T, 2 * ATT_WIDTH), BF16),
                   jax.ShapeDtypeStruct(dbias_shape, F32)],
        scratch_shapes=[pltpu.VMEM((stride, BLK, ATT_WIDTH), F32), pltpu.VMEM((stride, BLK, ATT_WIDTH), F32)],
        compiler_params=_params(dimension_semantics=("arbitrary",)),
    )(qkv, qkv, qkv, qkv, qkv, do, o, lse, bias)


REORDER_TILE = 256
REORDER_ROWS = 2048


def _reorder_matrix(d, inverse):
    per = REORDER_TILE // d
    p = np.zeros((REORDER_TILE, REORDER_TILE), np.float32)
    for src in range(REORDER_TILE):
        i, r = divmod(src, d)
        p[r * per + i, src] = 1.0
    return jnp.asarray(p.T if inverse else p, dtype=BF16)


def _reorder_rows(name, src, d, inverse, *, src_col=0, col_stride=1, ncols=1, dst=None, dst_col=0, dst_stride=1,
                  dst_blocks=None):
    T = src.shape[0]
    dtype = src.dtype
    span = BLK * d
    rows = max(span, min(T, REORDER_ROWS))
    per = REORDER_TILE // d
    tiles = span // REORDER_TILE
    dst_blocks = ncols if dst_blocks is None else dst_blocks

    def apply(p, x):
        return _dot(p, x).astype(BF16) if dtype == BF16 else _permute_f32(p, x)

    def body(*refs):
        p_ref, x_ref, o_ref = refs[0], refs[1], refs[-1]
        if d == 1:
            o_ref[...] = x_ref[...]
            return
        for s in range(rows // span):
            for t in range(tiles):
                base = s * span
                tile_rows = slice(base + t * REORDER_TILE, base + (t + 1) * REORDER_TILE)
                chunk = lambda r: slice(base + r * BLK + t * per, base + r * BLK + (t + 1) * per)
                if inverse:
                    gathered = jnp.concatenate([x_ref[chunk(r), :] for r in range(d)], axis=0)
                    o_ref[tile_rows, :] = apply(p_ref[...], gathered)
                else:
                    y = apply(p_ref[...], x_ref[tile_rows, :])
                    for r in range(d):
                        o_ref[chunk(r), :] = y[r * per:(r + 1) * per]

    in_specs = [pl.BlockSpec((REORDER_TILE, REORDER_TILE), lambda w, k: (0, 0)),
                pl.BlockSpec((rows, ATT_WIDTH), lambda w, k: (w, src_col + col_stride * k))]
    operands = [_reorder_matrix(max(d, 2), inverse), src]
    aliases = {}
    if dst is not None:
        in_specs.append(ANY)
        operands.append(dst)
        aliases = {2: 0}
    return pl.pallas_call(
        body, name=name, grid=(T // rows, ncols), in_specs=in_specs,
        out_specs=pl.BlockSpec((rows, ATT_WIDTH), lambda w, k: (w, dst_col + dst_stride * k)),
        out_shape=jax.ShapeDtypeStruct((T, dst_blocks * ATT_WIDTH), dtype),
        input_output_aliases=aliases,
        compiler_params=_params(dimension_semantics=("parallel", "parallel")),
    )(*operands)


def _group_qkv(qkv, g, d):
    NG = len(DILATIONS)
    if d == 1:
        return qkv, (g, NG + g, 2 * NG + g)
    return _reorder_rows(f"qkv_to_residues{g}", qkv, d, False, src_col=g, col_stride=NG, ncols=3), (0, 1, 2)


def _attention_fwd(qkv, bias):
    T = qkv.shape[0]
    parts = []
    for g, d in enumerate(DILATIONS):
        src, (qc, kc, vc) = _group_qkv(qkv, g, d)
        parts.append(_attn_fwd(f"attn_fwd_{g}", g, src, qc, kc, vc, bias, d))
    return _attn_merge(parts)


def _attention_bwd(qkv, do, o, lse, bias):
    T = qkv.shape[0]
    NG = len(DILATIONS)
    dqkv, dbs = None, []
    for g, d in enumerate(DILATIONS):
        src, (qc, kc, vc) = _group_qkv(qkv, g, d)
        do_g, o_g, lse_g = do, o, lse
        if d > 1:
            do_g = _reorder_rows(f"do_to_residues{g}", do, d, False)
            o_g = _reorder_rows(f"o_to_residues{g}", o, d, False)
            lse_g = _reorder_rows(f"lse_to_residues{g}", lse, d, False)
        dq, dkv, db = _attn_bwd(f"attn_bwd_{g}", g, src, qc, kc, vc, do_g, o_g, lse_g, bias, d)
        dqkv = _reorder_rows(f"dq_to_positions{g}", dq, d, True, dst=dqkv, dst_col=g, dst_blocks=3 * NG)
        dqkv = _reorder_rows(f"dkv_to_positions{g}", dkv, d, True, ncols=2, dst=dqkv, dst_col=NG + g, dst_stride=NG,
                             dst_blocks=3 * NG)
        dbs.append(db)
    return dqkv, jnp.stack(dbs)


def _other_chips(x, y):
    return [(1 - x, y), (x, 1 - y), (1 - x, 1 - y)]


def _shard_region(ref, shape, by_cols, chip, rows=None):
    R, C = shape
    start, size = (0, R) if rows is None else rows
    if by_cols:
        return ref.at[pl.ds(start, size), pl.ds(chip * C, C)]
    return ref.at[pl.ds(chip * R + start, size), :]


def _gather_weights(entries):
    n = len(entries)
    shapes = [e[0].shape[1:] for e in entries]

    def places(ins, outs, sems):
        send_sems, recv_sems, local_sems = sems
        x, y, c = lax.axis_index("x"), lax.axis_index("y"), lax.axis_index("c")

        def landing(f, px, py, pc):
            R = shapes[f][0]
            return _shard_region(outs[f], shapes[f], entries[f][2], 2 * px + py, rows=(pc * (R // 2), R // 2))

        def copy(f, k, block, to, src=None):
            dst = landing(f, *block)
            return pltpu.make_async_remote_copy(
                src_ref=dst if src is None else src, dst_ref=dst,
                send_sem=send_sems.at[6 * f + k], recv_sem=recv_sems.at[6 * f + k],
                device_id=to, device_id_type=MESH)

        def mine(f):
            dst = _shard_region(outs[f], shapes[f], entries[f][2], 2 * x + y)
            return pltpu.make_async_copy(ins[f].at[entries[f][1]], dst, local_sems.at[f])

        def first(f, j):
            R = shapes[f][0]
            src = ins[f].at[entries[f][1], pl.ds(c * (R // 2), R // 2), :]
            return copy(f, j, (x, y, c), (*_other_chips(x, y)[j], c), src=src)

        return x, y, c, copy, mine, first

    def start(ins, outs, sems):
        _, _, _, _, mine, first = places(ins, outs, sems)
        for f in range(n):
            mine(f).start()
        for j in range(3):
            for f in range(n):
                first(f, j).start()

    def mid(ins, outs, sems):
        x, y, c, copy, _, _ = places(ins, outs, sems)
        for j, chip in enumerate(_other_chips(x, y)):
            for f in range(n):
                copy(f, j, (*chip, c), (x, y, c)).wait_recv()
                copy(f, 3 + j, (*chip, c), (x, y, 1 - c)).start()

    def end(ins, outs, sems):
        x, y, c, copy, mine, first = places(ins, outs, sems)
        for j, chip in enumerate(_other_chips(x, y)):
            for f in range(n):
                copy(f, 3 + j, (*chip, 1 - c), (x, y, c)).wait_recv()
        for j, chip in enumerate(_other_chips(x, y)):
            for f in range(n):
                first(f, j).wait_send()
                copy(f, 3 + j, (*chip, c), (x, y, 1 - c)).wait_send()
        for f in range(n):
            mine(f).wait()

    def whole(f):
        R, C = shapes[f]
        return (R, N_CHIPS * C) if entries[f][2] else (N_CHIPS * R, C)

    return _Comm(
        [e[0] for e in entries], [jax.ShapeDtypeStruct(whole(f), BF16) for f in range(n)],
        [pltpu.SemaphoreType.DMA((6 * n,)), pltpu.SemaphoreType.DMA((6 * n,)), pltpu.SemaphoreType.DMA((n,))],
        start, end, mid)


def _scatter_grads(entries):
    n = len(entries)

    def copies(ins, outs, sems):
        send_sems, recv_sems, local_sems = sems
        x, y, c = lax.axis_index("x"), lax.axis_index("y"), lax.axis_index("c")
        me = 2 * x + y

        def piece(f, chip):
            return _shard_region(ins[f], entries[f][1], entries[f][2], chip)

        mine = [pltpu.make_async_copy(piece(f, me), outs[f].at[me], local_sems.at[f]) for f in range(n)]
        sends = [pltpu.make_async_remote_copy(
            src_ref=piece(f, 2 * px + py), dst_ref=outs[f].at[me],
            send_sem=send_sems.at[3 * f + j], recv_sem=recv_sems.at[3 * f + j],
            device_id=(px, py, c), device_id_type=MESH)
            for j, (px, py) in enumerate(_other_chips(x, y)) for f in range(n)]
        return mine, sends

    def start(ins, outs, sems):
        mine, sends = copies(ins, outs, sems)
        for cp in mine + sends:
            cp.start()

    def end(ins, outs, sems):
        mine, sends = copies(ins, outs, sems)
        for cp in sends + mine:
            cp.wait()

    return _Comm(
        [e[0] for e in entries], [jax.ShapeDtypeStruct((N_CHIPS,) + tuple(e[1]), BF16) for e in entries],
        [pltpu.SemaphoreType.DMA((3 * n,)), pltpu.SemaphoreType.DMA((3 * n,)), pltpu.SemaphoreType.DMA((n,))],
        start, end)


def _exchange_sibling(parts):
    n = len(parts)

    def copies(ins, outs, sems):
        send_sems, recv_sems = sems
        sibling = (lax.axis_index("x"), lax.axis_index("y"), 1 - lax.axis_index("c"))
        return [pltpu.make_async_remote_copy(src_ref=ins[i], dst_ref=outs[i], send_sem=send_sems.at[i],
                                             recv_sem=recv_sems.at[i], device_id=sibling, device_id_type=MESH)
                for i in range(n)]

    def start(ins, outs, sems):
        for cp in copies(ins, outs, sems):
            cp.start()

    def end(ins, outs, sems):
        for cp in copies(ins, outs, sems):
            cp.wait()

    return _Comm(parts, [jax.ShapeDtypeStruct(s.shape, s.dtype) for s in parts],
                 [pltpu.SemaphoreType.DMA((n,)), pltpu.SemaphoreType.DMA((n,))], start, end)


def _allgather_small(block):
    m_per, ncol = block.shape

    def places(ins, outs, sems):
        send_sems, recv_sems, local_sem = sems
        x, y, c = lax.axis_index("x"), lax.axis_index("y"), lax.axis_index("c")

        def rows(px, py, pc):
            return outs[0].at[4 * px + 2 * py + pc]

        def copy(k, block_of, to, src=None):
            return pltpu.make_async_remote_copy(
                src_ref=rows(*block_of) if src is None else src, dst_ref=rows(*block_of),
                send_sem=send_sems.at[k], recv_sem=recv_sems.at[k], device_id=to, device_id_type=MESH)

        mine = pltpu.make_async_copy(ins[0], rows(x, y, c), local_sem.at[0])
        first = [copy(0, (x, y, c), (x, y, 1 - c), src=ins[0])]
        first += [copy(1 + j, (x, y, c), (*chip, c), src=ins[0]) for j, chip in enumerate(_other_chips(x, y))]
        passed = [copy(4 + j, (*chip, c), (x, y, 1 - c)) for j, chip in enumerate(_other_chips(x, y))]
        return x, y, c, copy, mine, first, passed

    def start(ins, outs, sems):
        _, _, _, _, mine, first, _ = places(ins, outs, sems)
        for cp in [mine] + first:
            cp.start()

    def mid(ins, outs, sems):
        x, y, c, copy, _, _, passed = places(ins, outs, sems)
        for j, chip in enumerate(_other_chips(x, y)):
            copy(1 + j, (*chip, c), (x, y, c)).wait_recv()
            passed[j].start()

    def end(ins, outs, sems):
        x, y, c, copy, mine, first, passed = places(ins, outs, sems)
        copy(0, (x, y, 1 - c), (x, y, c)).wait_recv()
        for j, chip in enumerate(_other_chips(x, y)):
            copy(4 + j, (*chip, 1 - c), (x, y, c)).wait_recv()
        for cp in first + passed:
            cp.wait_send()
        mine.wait()

    return _Comm([block], [jax.ShapeDtypeStruct((N_DEV, m_per, ncol), block.dtype)],
                 [pltpu.SemaphoreType.DMA((7,)), pltpu.SemaphoreType.DMA((7,)), pltpu.SemaphoreType.DMA((1,))],
                 start, end, mid)


def _join_comms(*progs):
    def split(parts, counts):
        out, pos = [], 0
        for n in counts:
            out.append(parts[pos:pos + n])
            pos += n
        return out

    def phase(which):
        def run(ins, outs, sems):
            args = zip(split(ins, [len(p.inputs) for p in progs]), split(outs, [len(p.out_shapes) for p in progs]),
                       split(sems, [len(p.scratch) for p in progs]))
            for p, (i, o, s) in zip(progs, args):
                fn = getattr(p, which)
                if fn is not None:
                    fn(i, o, s)
        return run

    return _Comm([a for p in progs for a in p.inputs], [s for p in progs for s in p.out_shapes],
                 [s for p in progs for s in p.scratch], phase("start"), phase("end"), phase("mid"))


def _adamw(w, g, m, v):
    m = ADAM_B1 * m + (1.0 - ADAM_B1) * g
    v = ADAM_B2 * v + (1.0 - ADAM_B2) * jnp.square(g)
    m_hat = m / (1.0 - ADAM_B1 ** ADAM_STEP)
    v_hat = v / (1.0 - ADAM_B2 ** ADAM_STEP)
    delta = -ADAM_LR * (m_hat / (jnp.sqrt(v_hat) + ADAM_EPS) + ADAM_WD * w)
    return delta, m, v


def _flat_tile(rows):
    return min(rows, 256)


def _sum_pieces(name, layers):
    L = len(layers)
    P, R, C = layers[0].shape
    tr = min(R, 2 * _flat_tile(R))

    def body(*refs):
        out_ref = refs[L]
        for l in range(L):
            @pl.when(pl.program_id(0) == l)
            def _(p_ref=refs[l]):
                acc = p_ref[0].astype(F32)
                for j in range(1, P):
                    acc = acc + p_ref[j].astype(F32)
                out_ref[...] = acc

    return pl.pallas_call(
        body, name=name, grid=(L, R // tr),
        in_specs=[pl.BlockSpec((P, tr, C), lambda l, i: (0, i, 0)) for _ in range(L)],
        out_specs=pl.BlockSpec((None, tr, C), lambda l, i: (l, i, 0)),
        out_shape=jax.ShapeDtypeStruct((L, R, C), F32),
        compiler_params=_params(dimension_semantics=("parallel", "parallel")),
    )(*layers)


def _adam_pair(name, w, m, v, part_a, part_b):
    L, R, C = w.shape
    tr = _flat_tile(R)

    def body(w_ref, m_ref, v_ref, a_ref, b_ref, g_ref, d_ref, nm_ref, nv_ref):
        g = a_ref[...] + b_ref[...]
        g_ref[...] = g
        d_ref[...], nm_ref[...], nv_ref[...] = _adamw(w_ref[...], g, m_ref[...], v_ref[...])

    row = pl.BlockSpec((None, tr, C), lambda l, i: (l, i, 0))
    return pl.pallas_call(
        body, name=name, grid=(L, R // tr),
        in_specs=[row] * 5, out_specs=[row] * 4,
        out_shape=[jax.ShapeDtypeStruct((L, R, C), F32)] * 4,
        compiler_params=_params(dimension_semantics=("parallel", "parallel")),
    )(w, m, v, part_a, part_b)


def _adam_small(w, m, v, gathered):
    R, C = w.shape

    def body(w_ref, m_ref, v_ref, p_ref, g_ref, d_ref, nm_ref, nv_ref):
        g = p_ref[0]
        for j in range(1, N_DEV):
            g = g + p_ref[j]
        g_ref[...] = g
        d_ref[...], nm_ref[...], nv_ref[...] = _adamw(w_ref[...], g, m_ref[...], v_ref[...])

    return pl.pallas_call(
        body, name="adam_small",
        out_shape=[jax.ShapeDtypeStruct((R, C), F32)] * 4,
        compiler_params=_params(),
    )(w, m, v, gathered)


SMALL = ("mix_norm_g", "mlp_norm_g", "final_norm_g", "a_ln_g", "a_ln_b", "a_w_s", "a_b_s", "rel_bias")


def _pack_small(arrays, width):
    rows = []
    for a in arrays:
        flat = a.reshape(-1)
        pad = (-flat.shape[0]) % width
        rows.append(jnp.pad(flat, (0, pad)).reshape(-1, width))
    block = jnp.concatenate(rows, axis=0)
    return jnp.pad(block, ((0, (-block.shape[0]) % 8), (0, 0)))


def _unpack_small(block, shapes, width):
    out, row = [], 0
    for shape in shapes:
        size = int(np.prod(shape))
        nrows = -(-size // width)
        out.append(block[row:row + nrows].reshape(-1)[:size].reshape(shape))
        row += nrows
    return out


def kernel(x, mix_norm_g, mlp_norm_g, final_norm_g, a_w_in, a_ln_g, a_ln_b, a_w_s, a_b_s, a_w_out, b_w_qkv, b_w_out, rel_bias, w_up, w_down, loss_target, m_mix_norm_g, m_mlp_norm_g, m_final_norm_g, m_a_w_in, m_a_ln_g, m_a_ln_b, m_a_w_s, m_a_b_s, m_a_w_out, m_b_w_qkv, m_b_w_out, m_rel_bias, m_w_up, m_w_down, v_mix_norm_g, v_mlp_norm_g, v_final_norm_g, v_a_w_in, v_a_ln_g, v_a_ln_b, v_a_w_s, v_a_b_s, v_a_w_out, v_b_w_qkv, v_b_w_out, v_rel_bias, v_w_up, v_w_down):
    T, D = x.shape[1], x.shape[2]
    h0 = x.reshape(T, D)
    target = loss_target.reshape(T, D)
    G = a_w_s.shape[1]

    w_big = [a_w_in, a_w_out, b_w_qkv, b_w_out, w_up, w_down]
    m_big = [m_a_w_in, m_a_w_out, m_b_w_qkv, m_b_w_out, m_w_up, m_w_down]
    v_big = [v_a_w_in, v_a_w_out, v_b_w_qkv, v_b_w_out, v_w_up, v_w_down]
    by_cols = [True, False, True, True, True, False]
    s_in, s_out, s_qkv, s_bo, s_up, s_dn = [w.astype(BF16) for w in w_big]
    (W_in,) = _run_comm("gather_a", _gather_weights([(s_in, 0, True)]))

    tril = jnp.tril(jnp.ones((CHUNK, CHUNK), dtype=bool))
    w_tril = jnp.where(tril[None], a_w_s[0], 0.0).astype(BF16)
    w_tril_t = jnp.swapaxes(w_tril, 1, 2)
    b_rows = jnp.broadcast_to(a_b_s[0][:, :, None], (G, CHUNK, CHUNK))
    buckets = _bucket_maps()
    bias = _bias_build(rel_bias, buckets)

    QKV = s_qkv.shape[2] * N_CHIPS
    TM = 1024
    TK_WGRAD = 4096

    def matmul(name, a, b, mode, out, tm=TM, tn=1024, **kw):
        outs = out if isinstance(out, list) else [out]
        return _mm(name, a, b, mode, tm=tm, tn=tn, tk=a.shape[1], outs=outs, **kw)

    def norm_bwd(layer_gain, h, dres, copies=2):
        return dict(epi=_epi_rms_bwd(copies), extras=(h, dres), vecs=(layer_gain,), col_sums=1)

    def wgrad(name, a, b, tn=1024, tk=TK_WGRAD, comm=None):
        return _mm(name, a, b, "tn", tm=1024, tn=tn, tk=tk, outs=[BF16], comm=comm)

    def scatter(*which):
        return _scatter_grads([(g, w_big[i].shape[1:], by_cols[i]) for g, i in which])

    (a_pre, y0), (W_out, W_up0) = matmul("a_in", h0, W_in, "nn", BF16, norm_gain=mix_norm_g[0:1],
                                         comm=_gather_weights([(s_out, 0, False), (s_up, 0, True)]))
    z = _gate_fwd(a_pre, a_ln_g, a_ln_b, w_tril, b_rows)
    h1 = matmul("a_out", z, W_out, "nn", F32, epi=_epi_residual, extras=(h0,))
    (q1, y1), (W_dn0,) = matmul("mlp_up0", h1, W_up0, "nn", BF16, epi=_epi_relu2, norm_gain=mlp_norm_g[0:1],
                                comm=_gather_weights([(s_dn, 0, False)]))
    h2, (W_qkv, W_bo) = matmul("mlp_down0", q1, W_dn0, "nn", F32, tm=TM // 2, epi=_epi_residual, extras=(h1,),
                               comm=_gather_weights([(s_qkv, 0, True), (s_bo, 0, True)]))
    (qkv, y2), (W_up1,) = matmul("b_qkv", h2, W_qkv, "nn", BF16, tn=QKV // 4, norm_gain=mix_norm_g[1:2],
                                 comm=_gather_weights([(s_up, 1, True)]))
    o, lse = _attention_fwd(qkv, bias)
    h3 = matmul("b_out", o, W_bo, "nn", F32, epi=_epi_residual, extras=(h2,))
    (q3, y3), (W_dn1,) = matmul("mlp_up1", h3, W_up1, "nn", BF16, epi=_epi_relu2, norm_gain=mlp_norm_g[1:2],
                                comm=_gather_weights([(s_dn, 1, False)]))
    dh4, dh4_b, d_final_g, loss_row = matmul("mlp_down1", q3, W_dn1, "nn", [F32, BF16], tm=TM // 2, epi=_epi_loss_head,
                                             extras=(h3, target), vecs=(final_norm_g.reshape(1, D),), col_sums=2)

    dp3 = matmul("mlp_down_bwd1", dh4_b, W_dn1, "nt", BF16, epi=_epi_relu2_grad, extras=(q3,))
    g_dn1 = wgrad("mlp_down_wgrad1", q3, dh4_b)
    g_up1, (r_dn1,) = wgrad("mlp_up_wgrad1", y3, dp3, comm=scatter((g_dn1, 5)))
    (dh3, dh3_b, dg_mlp1), (r_up1,) = matmul("mlp_up_bwd1", dp3, W_up1, "nt", [F32, BF16], tm=TM // 2,
                                             comm=scatter((g_up1, 4)), **norm_bwd(mlp_norm_g[1:2], h3, dh4))
    do = matmul("b_out_bwd", dh3_b, W_bo, "nt", BF16)
    g_bo = wgrad("b_out_wgrad", o, dh3_b)
    dqkv, dbias = _attention_bwd(qkv, do, o, lse, bias)
    d_rel_bias = _bias_scatter(dbias, buckets)
    (dh2, dh2_b, dg_mix1), (r_bo,) = matmul("b_qkv_bwd", dqkv, W_qkv, "nt", [F32, BF16], tm=TM // 2,
                                            comm=scatter((g_bo, 3)), **norm_bwd(mix_norm_g[1:2], h2, dh3))
    g_qkv = wgrad("b_qkv_wgrad", y2, dqkv, tn=QKV // 3, tk=TK_WGRAD // 2)
    dp1, (r_qkv,) = matmul("mlp_down_bwd0", dh2_b, W_dn0, "nt", BF16, epi=_epi_relu2_grad, extras=(q1,),
                           comm=scatter((g_qkv, 2)))
    g_up0 = wgrad("mlp_up_wgrad0", y1, dp1)
    g_dn0, (r_up0,) = wgrad("mlp_down_wgrad0", q1, dh2_b, comm=scatter((g_up0, 4)))
    (dh1, dh1_b, dg_mlp0), (r_dn0,) = matmul("mlp_up_bwd0", dp1, W_up0, "nt", [F32, BF16], tm=TM // 2,
                                             comm=scatter((g_dn0, 5)), **norm_bwd(mlp_norm_g[0:1], h1, dh2))
    dz = matmul("a_out_bwd", dh1_b, W_out, "nt", F32)
    g_out = wgrad("a_out_wgrad", z, dh1_b)
    da, d_ln_g, d_ln_b, d_w_s, d_b_s = _gate_bwd(a_pre, dz, a_ln_g, a_ln_b, w_tril, w_tril_t, b_rows)
    g_in, (r_out,) = wgrad("a_in_wgrad", y0, da, comm=scatter((g_out, 1)))
    grad_x, dg_mix0 = matmul("a_in_bwd", da, W_in, "nt", F32, **norm_bwd(mix_norm_g[0:1], h0, dh1, copies=1))

    unused = jnp.zeros((1, 1), F32)
    small_w = [mix_norm_g, mlp_norm_g, final_norm_g, a_ln_g, a_ln_b, a_w_s, a_b_s, rel_bias, unused]
    small_m = [m_mix_norm_g, m_mlp_norm_g, m_final_norm_g, m_a_ln_g, m_a_ln_b, m_a_w_s, m_a_b_s, m_rel_bias, unused]
    small_v = [v_mix_norm_g, v_mlp_norm_g, v_final_norm_g, v_a_ln_g, v_a_ln_b, v_a_w_s, v_a_b_s, v_rel_bias, unused]
    small_g = [jnp.concatenate([dg_mix0, dg_mix1]), jnp.concatenate([dg_mlp0, dg_mlp1]), d_final_g,
               d_ln_g, d_ln_b, d_w_s[None], d_b_s[None, :, :, 0], d_rel_bias, loss_row[:, :1]]
    width = max(D, 128)
    received = [None, [r_out], [r_qkv], [r_bo], [r_up0, r_up1], [r_dn0, r_dn1]]
    plane = [None] + [_sum_pieces(f"sum_pieces{i}", received[i]) for i in range(1, len(w_big))]
    tail = _run_comm("tail_comm", _join_comms(scatter((g_in, 0)), _exchange_sibling(plane[1:]),
                                              _allgather_small(_pack_small(small_g, width))))
    r_in, other, gathered_small = tail[0], [None] + list(tail[1:len(w_big)]), tail[len(w_big)]
    plane[0] = _sum_pieces("sum_pieces0", [r_in])
    (other[0],) = _run_comm("exchange_a_in", _exchange_sibling([plane[0]]))
    big_out = [_adam_pair(f"adam{i}", w_big[i], m_big[i], v_big[i], plane[i], other[i]) for i in range(len(w_big))]

    def unbig(kind):
        return dict(zip(["a_w_in", "a_w_out", "b_w_qkv", "b_w_out", "w_up", "w_down"], [b[kind] for b in big_out]))

    small_out = _adam_small(_pack_small(small_w, width), _pack_small(small_m, width), _pack_small(small_v, width),
                            gathered_small)
    shapes = [w.shape for w in small_w]
    loss = _unpack_small(small_out[0], shapes, width)[-1][0, 0]

    names = ["mix_norm_g", "mlp_norm_g", "final_norm_g", "a_w_in", "a_ln_g", "a_ln_b", "a_w_s", "a_b_s", "a_w_out",
             "b_w_qkv", "b_w_out", "rel_bias", "w_up", "w_down"]
    results = [loss, grad_x.reshape(x.shape)]
    for kind in range(4):
        table = dict(zip(SMALL, _unpack_small(small_out[kind], shapes, width)))
        table.update(unbig(kind))
        results += [table[n] for n in names]
    return tuple(results)
```

```python
import functools
import math

import numpy as np
import jax
import jax.numpy as jnp
from jax import lax
from jax.experimental import pallas as pl
from jax.experimental.pallas import tpu as pltpu

F32 = jnp.float32
BF16 = jnp.bfloat16
MESH = pl.DeviceIdType.MESH
ANY = pl.BlockSpec(memory_space=pl.ANY)

N_CHIPS = 4
N_DEV = 8
VMEM_LIMIT_BYTES = 56 * 1024 * 1024

EPS = 1e-6
NEG_INF = -1e30
CHUNK = 128
GROUP_DIM = 128
HEAD_DIM = 64
ATT_HEADS = 8
ATT_WIDTH = ATT_HEADS * HEAD_DIM
PAIR = 2 * HEAD_DIM
BLK = 128
DILATIONS = (1, 4, 16)
N_BUCKETS = 32
MAX_EXACT = N_BUCKETS // 2
REL_MAX_DISTANCE = 2048

ADAM_LR = 0.001
ADAM_B1 = 0.9
ADAM_B2 = 0.999
ADAM_EPS = 1e-08
ADAM_WD = 0.01
ADAM_STEP = 10

NN = (((1,), (0,)), ((), ()))
NT = (((1,), (1,)), ((), ()))
TN = (((0,), (0,)), ((), ()))


def _params(**kw):
    return pltpu.CompilerParams(vmem_limit_bytes=VMEM_LIMIT_BYTES, **kw)


def _dot(a, b, dims=NN):
    return lax.dot_general(a, b, dims, preferred_element_type=F32)


def _gelu(x):
    return 0.5 * x * (1.0 + lax.erf(x * math.sqrt(0.5)))


def _gelu_grad(x):
    return 0.5 * (1.0 + lax.erf(x * math.sqrt(0.5))) + x * jnp.exp(-0.5 * x * x) * (1.0 / math.sqrt(2.0 * math.pi))


def _mean(x):
    return jnp.mean(x, axis=-1, keepdims=True)


class _Comm:
    def __init__(self, inputs, out_shapes, scratch, start, end, mid=None):
        self.inputs, self.out_shapes, self.scratch = list(inputs), list(out_shapes), list(scratch)
        self.start, self.mid, self.end = start, mid, end


def _run_comm(name, comm):
    n_in, n_out = len(comm.inputs), len(comm.out_shapes)

    def body(*refs):
        parts = refs[:n_in], refs[n_in:n_in + n_out], refs[n_in + n_out:]
        comm.start(*parts)
        if comm.mid is not None:
            comm.mid(*parts)
        comm.end(*parts)

    return pl.pallas_call(
        body, name=name, in_specs=[ANY] * n_in, out_specs=[ANY] * n_out, out_shape=comm.out_shapes,
        scratch_shapes=comm.scratch, compiler_params=_params(),
    )(*comm.inputs)


def _mm(name, a, b, mode, *, tm, tn, tk, outs, epi=None, extras=(), vecs=(), col_sums=0, norm_gain=None, comm=None):
    if mode == "tn":
        K, M = a.shape
    else:
        M, K = a.shape
    N = b.shape[0] if mode == "nt" else b.shape[1]
    tm, tn, tk = min(tm, M), min(tn, N), min(tk, K)
    assert M % tm == 0 and N % tn == 0 and K % tk == 0, (name, M, N, K, tm, tn, tk)
    nk = K // tk
    grid = (M // tm, N // tn, nk)

    if mode == "tn":
        a_spec = pl.BlockSpec((tk, tm), lambda i, j, k: (k, i))
    else:
        a_spec = pl.BlockSpec((tm, tk), lambda i, j, k: (i, k))
    if mode == "nt":
        b_spec = pl.BlockSpec((tn, tk), lambda i, j, k: (j, k))
    else:
        b_spec = pl.BlockSpec((tk, tn), lambda i, j, k: (k, j))
    tile = pl.BlockSpec((tm, tn), lambda i, j, k: (i, j))
    vec = pl.BlockSpec((1, tn), lambda i, j, k: (0, j))
    normed = norm_gain is not None
    assert not normed or (mode == "nn" and nk == 1 and tm % grid[1] == 0)
    assert col_sums == 0 or grid[1] == 1
    out_shapes = [jax.ShapeDtypeStruct((M, N), dtype) for dtype in outs]
    out_specs = [tile for _ in outs]
    extra_specs = [tile for _ in extras] + [vec for _ in vecs]
    if normed:
        part_rows = tm // grid[1]
        last_part = M // part_rows - 1
        a_spec = pl.BlockSpec((tm, K), lambda i, j, k: (0, 0))
        out_shapes.append(jax.ShapeDtypeStruct((M, K), BF16))
        out_specs.append(pl.BlockSpec((part_rows, K), lambda i, j, k: (i * grid[1] + j, 0)))
        extra_specs.append(pl.BlockSpec((1, K), lambda i, j, k: (0, 0)))
        extra_specs.append(pl.BlockSpec((part_rows, K),
                                        lambda i, j, k: (jnp.minimum((i + 1) * grid[1] + j, last_part), 0)))
    out_shapes += [jax.ShapeDtypeStruct((1, N), F32)] * col_sums
    out_specs += [vec] * col_sums
    n_extra, n_out = len(extra_specs), len(out_shapes)
    n_tiles = len(outs)
    n_cin = len(comm.inputs) if comm else 0
    n_cout = len(comm.out_shapes) if comm else 0
    dims = {"nn": NN, "nt": NT, "tn": TN}[mode]
    steps = grid[0] * grid[1] * grid[2]

    def body(*refs):
        a_ref, b_ref = refs[0], refs[1]
        pos = 2
        extra_refs = refs[pos:pos + n_extra]
        pos += n_extra
        comm_in = refs[pos:pos + n_cin]
        pos += n_cin
        out_refs = refs[pos:pos + n_out]
        pos += n_out
        comm_out = refs[pos:pos + n_cout]
        pos += n_cout
        acc_ref = refs[pos] if nk > 1 else None
        pos += nk > 1
        y_refs = refs[pos:pos + 2 * normed]
        comm_sems = refs[pos + 2 * normed:]
        k = pl.program_id(2)
        step = (pl.program_id(0) * grid[1] + pl.program_id(1)) * nk + k

        if comm is not None:
            @pl.when(step == 0)
            def _():
                comm.start(comm_in, comm_out, comm_sems)

        def finish(acc):
            epi_args = [e[...] for e in extra_refs[:n_extra - 2 * normed]]
            res = epi(acc, *epi_args) if epi is not None else (acc,) * n_tiles
            for o, r in zip(out_refs[:n_tiles], res[:n_tiles]):
                o[...] = r.astype(o.dtype)
            if col_sums:
                sums = out_refs[n_out - col_sums:]

                @pl.when(pl.program_id(0) == 0)
                def _():
                    for o in sums:
                        o[...] = jnp.zeros_like(o)

                for o, r in zip(sums, res[n_tiles:]):
                    o[...] += r

        if normed:
            gain_ref, ahead_ref = extra_refs[-2], extra_refs[-1]

            def norm(hv):
                return (hv * lax.rsqrt(_mean(hv * hv) + EPS) * gain_ref[...]).astype(BF16)

            @pl.when(step == 0)
            def _():
                y_refs[0][...] = norm(a_ref[...])

            part_at = pl.ds(pl.multiple_of(pl.program_id(1) * part_rows, part_rows), part_rows)
            for parity in range(2):
                @pl.when(pl.program_id(0) % 2 == parity)
                def _(y_now=y_refs[parity], y_next=y_refs[1 - parity]):
                    finish(_dot(y_now[...], b_ref[...].astype(BF16), dims))
                    out_refs[n_tiles][...] = y_now[part_at, :]
                    y_next[part_at, :] = norm(ahead_ref[...])
        elif nk == 1:
            finish(_dot(a_ref[...].astype(BF16), b_ref[...].astype(BF16), dims))
        else:
            part = _dot(a_ref[...].astype(BF16), b_ref[...].astype(BF16), dims)

            @pl.when(k == 0)
            def _():
                acc_ref[...] = part

            @pl.when(k > 0)
            def _():
                acc_ref[...] += part

            @pl.when(k == nk - 1)
            def _():
                finish(acc_ref[...])

        if comm is not None:
            if comm.mid is not None:
                @pl.when(step == (3 * steps) // 4)
                def _():
                    comm.mid(comm_in, comm_out, comm_sems)

            @pl.when(step == steps - 1)
            def _():
                comm.end(comm_in, comm_out, comm_sems)

    sequential = comm is not None or normed or col_sums > 0
    order = ("arbitrary",) * 3 if sequential else ("parallel", "parallel", "arbitrary")
    scratch = [pltpu.VMEM((tm, tn), F32)] if nk > 1 else []
    if normed:
        scratch += [pltpu.VMEM((tm, K), BF16)] * 2
    res = pl.pallas_call(
        body, name=name, grid=grid,
        in_specs=[a_spec, b_spec] + extra_specs + [ANY] * n_cin,
        out_specs=out_specs + [ANY] * n_cout,
        out_shape=out_shapes + (comm.out_shapes if comm else []),
        scratch_shapes=scratch + (comm.scratch if comm else []),
        compiler_params=_params(dimension_semantics=order),
    )(a, b, *extras, *vecs, *([norm_gain, a] if normed else []), *(comm.inputs if comm else []))
    mm_out = res[0] if n_out == 1 else list(res[:n_out])
    return (mm_out, list(res[n_out:])) if comm else mm_out


def _epi_residual(acc, res):
    return (res + acc,)


def _epi_relu2(acc):
    return (jnp.square(jnp.maximum(acc, 0.0)),)


def _epi_rms_bwd(copies):
    def epi(acc, h, dres, g):
        r = lax.rsqrt(_mean(h * h) + EPS)
        hn = h * r
        dyg = acc * g
        dh = dres + r * (dyg - hn * _mean(dyg * hn))
        return (dh,) * copies + (jnp.sum(acc * hn, axis=0, keepdims=True),)
    return epi


def _epi_loss_head(acc, res, target, g):
    h = res + acc
    r = lax.rsqrt(_mean(h * h) + EPS)
    hn = h * r
    diff = hn * g - target
    loss = 0.5 * jnp.sum(_mean(diff * diff))
    dy = diff * (1.0 / h.shape[-1])
    dyg = dy * g
    dh = r * (dyg - hn * _mean(dyg * hn))
    return dh, dh, jnp.sum(dy * hn, axis=0, keepdims=True), jnp.full((1, h.shape[-1]), loss, F32)


def _epi_relu2_grad(acc, q):
    qf = q.astype(F32)
    return (acc * jnp.where(qf > 0.0, (2.0 * qf) * lax.rsqrt(qf), 0.0),)


def _row_tile(T):
    return min(T, 512)


def _gate_tile(T):
    return min(T, 256)


def _gate_fwd(a, ln_g, ln_b, w_tril, b_rows):
    T, W2 = a.shape
    W = W2 // 2
    G = W // GROUP_DIM
    tr = _gate_tile(T)

    def body(a_ref, lng_ref, lnb_ref, w_ref, b_ref, z_ref):
        u = _gelu(a_ref[:, :W].astype(F32))
        vg = _gelu(a_ref[:, W:].astype(F32))
        xc = vg - _mean(vg)
        vn = xc * lax.rsqrt(_mean(xc * xc) + EPS)
        vl = (vn * lng_ref[...] + lnb_ref[...]).astype(BF16)
        for n in range(tr // CHUNK):
            rows = slice(n * CHUNK, (n + 1) * CHUNK)
            for g in range(G):
                cols = slice(g * GROUP_DIM, (g + 1) * GROUP_DIM)
                gate = _dot(w_ref[g], vl[rows, cols]) + b_ref[g]
                z_ref[rows, cols] = (u[rows, cols] * gate).astype(BF16)

    vec = pl.BlockSpec((1, W), lambda i: (0, 0))
    grp = pl.BlockSpec((G, CHUNK, CHUNK), lambda i: (0, 0, 0))
    return pl.pallas_call(
        body, name="gate_fwd", grid=(T // tr,),
        in_specs=[pl.BlockSpec((tr, W2), lambda i: (i, 0)), vec, vec, grp, grp],
        out_specs=pl.BlockSpec((tr, W), lambda i: (i, 0)),
        out_shape=jax.ShapeDtypeStruct((T, W), BF16),
        compiler_params=_params(dimension_semantics=("parallel",)),
    )(a, ln_g, ln_b, w_tril, b_rows)


def _gate_bwd(a, dz, ln_g, ln_b, w_tril, w_tril_t, b_rows):
    T, W2 = a.shape
    W = W2 // 2
    G = W // GROUP_DIM
    tr = _gate_tile(T)
    steps = T // tr

    def body(a_ref, dz_ref, lng_ref, lnb_ref, w_ref, wt_ref, b_ref, da_ref, dlng_ref, dlnb_ref, dw_ref, dbs_ref, dvl_ref):
        step = pl.program_id(0)

        @pl.when(step == 0)
        def _():
            dlng_ref[...] = jnp.zeros_like(dlng_ref)
            dlnb_ref[...] = jnp.zeros_like(dlnb_ref)
            dw_ref[...] = jnp.zeros_like(dw_ref)
            dbs_ref[...] = jnp.zeros_like(dbs_ref)

        au = a_ref[:, :W].astype(F32)
        av = a_ref[:, W:].astype(F32)
        u = _gelu(au)
        vg = _gelu(av)
        xc = vg - _mean(vg)
        rstd = lax.rsqrt(_mean(xc * xc) + EPS)
        vn = xc * rstd
        lng = lng_ref[...]
        vl = (vn * lng + lnb_ref[...]).astype(BF16)
        du_scale = dz_ref[...] * _gelu_grad(au)
        dgate_all = dz_ref[...] * u
        for n in range(tr // CHUNK):
            rows = slice(n * CHUNK, (n + 1) * CHUNK)
            for g in range(G):
                cols = slice(g * GROUP_DIM, (g + 1) * GROUP_DIM)
                vlg = vl[rows, cols]
                gate = _dot(w_ref[g], vlg) + b_ref[g]
                da_ref[rows, cols] = (du_scale[rows, cols] * gate).astype(BF16)
                dgate = dgate_all[rows, cols]
                dbs_ref[g] += dgate
                dgate_b = dgate.astype(BF16)
                dw_ref[g] += _dot(dgate_b, vlg, NT)
                dvl_ref[rows, cols] = _dot(wt_ref[g], dgate_b)
        dvl = dvl_ref[...]
        dlnb_ref[...] += jnp.sum(dvl, axis=0, keepdims=True)
        dlng_ref[...] += jnp.sum(dvl * vn, axis=0, keepdims=True)
        dvn = dvl * lng
        dvg = rstd * (dvn - _mean(dvn) - vn * _mean(dvn * vn))
        da_ref[:, W:] = (dvg * _gelu_grad(av)).astype(BF16)

        @pl.when(step == steps - 1)
        def _():
            t_idx = lax.broadcasted_iota(jnp.int32, (CHUNK, CHUNK), 0)
            s_idx = lax.broadcasted_iota(jnp.int32, (CHUNK, CHUNK), 1)
            for g in range(G):
                dw_ref[g] = jnp.where(s_idx <= t_idx, dw_ref[g], 0.0)
                dbs_ref[g] = jnp.broadcast_to(jnp.sum(dbs_ref[g], axis=-1, keepdims=True), (CHUNK, CHUNK))

    vec = pl.BlockSpec((1, W), lambda i: (0, 0))
    grp = pl.BlockSpec((G, CHUNK, CHUNK), lambda i: (0, 0, 0))
    return pl.pallas_call(
        body, name="gate_bwd", grid=(steps,),
        in_specs=[pl.BlockSpec((tr, W2), lambda i: (i, 0)), pl.BlockSpec((tr, W), lambda i: (i, 0)),
                  vec, vec, grp, grp, grp],
        out_specs=[pl.BlockSpec((tr, W2), lambda i: (i, 0)), vec, vec, grp, grp],
        out_shape=[jax.ShapeDtypeStruct((T, W2), BF16), jax.ShapeDtypeStruct((1, W), F32),
                   jax.ShapeDtypeStruct((1, W), F32), jax.ShapeDtypeStruct((G, CHUNK, CHUNK), F32),
                   jax.ShapeDtypeStruct((G, CHUNK, CHUNK), F32)],
        scratch_shapes=[pltpu.VMEM((tr, W), F32)],
        compiler_params=_params(dimension_semantics=("arbitrary",)),
    )(a, dz, ln_g, ln_b, w_tril, w_tril_t, b_rows)


def _bucket_map(dilation):
    rel = BLK + np.arange(BLK)[:, None] - np.arange(2 * BLK)[None, :]
    dist = np.clip(rel, 0, BLK) * dilation
    nf = np.maximum(dist, 1).astype(np.float32)
    large = MAX_EXACT + (np.log(nf / np.float32(MAX_EXACT)) / np.float32(math.log(REL_MAX_DISTANCE / MAX_EXACT))
                         * np.float32(N_BUCKETS - MAX_EXACT)).astype(np.int32)
    large = np.minimum(large, N_BUCKETS - 1)
    return np.where(dist < MAX_EXACT, dist, large).astype(np.int32)


def _bucket_maps():
    return jnp.asarray(np.stack([_bucket_map(d) for d in DILATIONS]))


def _bias_build(rel_bias, buckets):
    NG = len(DILATIONS)

    def body(table_ref, bucket_ref, out_ref):
        for g in range(NG):
            bk = bucket_ref[g]
            for h in range(ATT_HEADS):
                out_ref[0, g, h] = jnp.zeros((BLK, 2 * BLK), F32)
            for b in range(N_BUCKETS):
                hit = bk == b
                for h in range(ATT_HEADS):
                    out_ref[0, g, h] = jnp.where(hit, table_ref[b, g * ATT_HEADS + h], out_ref[0, g, h])
            for h in range(ATT_HEADS):
                for first in range(2):
                    out_ref[first, g, h] = jnp.where(_window_mask(first), out_ref[0, g, h], NEG_INF)

    return pl.pallas_call(
        body, name="bias_build",
        in_specs=[pl.BlockSpec(memory_space=pltpu.SMEM), pl.BlockSpec(memory_space=pltpu.VMEM)],
        out_specs=pl.BlockSpec(memory_space=pltpu.VMEM),
        out_shape=jax.ShapeDtypeStruct((2, NG, ATT_HEADS, BLK, 2 * BLK), F32),
        compiler_params=_params(),
    )(rel_bias, buckets)


def _bias_scatter(dbias, buckets):
    NG = len(DILATIONS)

    def body(dbias_ref, bucket_ref, out_ref):
        for g in range(NG):
            bk = bucket_ref[g]
            for b in range(N_BUCKETS):
                hit = bk == b
                for h in range(ATT_HEADS):
                    out_ref[b, g * ATT_HEADS + h] = jnp.sum(jnp.where(hit, dbias_ref[g, h], 0.0))

    return pl.pallas_call(
        body, name="bias_scatter",
        in_specs=[pl.BlockSpec(memory_space=pltpu.VMEM), pl.BlockSpec(memory_space=pltpu.VMEM)],
        out_specs=pl.BlockSpec(memory_space=pltpu.SMEM),
        out_shape=jax.ShapeDtypeStruct((N_BUCKETS, NG * ATT_HEADS), F32),
        compiler_params=_params(),
    )(dbias, buckets)


def _window_mask(first):
    qi = lax.broadcasted_iota(jnp.int32, (BLK, 2 * BLK), 0)
    kj = lax.broadcasted_iota(jnp.int32, (BLK, 2 * BLK), 1)
    rel = BLK + qi - kj
    return (rel >= 0) & (rel <= BLK) & (kj >= BLK * first)


def _head_lanes(hh):
    lane = lax.broadcasted_iota(jnp.int32, (1, PAIR), 1)
    return (lane >= hh * HEAD_DIM) & (lane < (hh + 1) * HEAD_DIM)


ATT_STEP_BLOCKS = 8


def _attn_steps(stride):
    per_step = math.gcd(stride, ATT_STEP_BLOCKS)
    return per_step, stride // per_step


def _attn_fwd(name, g, qkv, qc, kc, vc, bias, stride):
    T = qkv.shape[0]
    per_step, lag = _attn_steps(stride)
    rows = per_step * BLK
    scale = HEAD_DIM ** -0.5

    def body(q_ref, kp_ref, kc_ref, vp_ref, vc_ref, bias_ref, out_ref):
        first = (pl.program_id(0) < lag).astype(jnp.int32)
        low = _head_lanes(0)

        def block(j, carry):
            at = pl.ds(pl.multiple_of(j * BLK, BLK), BLK)
            for hp in range(ATT_HEADS // 2):
                cols = slice(hp * PAIR, (hp + 1) * PAIR)
                qp = q_ref[at, cols]
                kk = jnp.concatenate([kp_ref[at, cols], kc_ref[at, cols]], axis=0)
                vv = jnp.concatenate([vp_ref[at, cols], vc_ref[at, cols]], axis=0)
                o_h, lse_h = [], []
                for hh in range(2):
                    qm = jnp.where(_head_lanes(hh), qp, jnp.zeros_like(qp))
                    s = _dot(qm, kk, NT) * scale
                    logits = s + bias_ref[first, 2 * hp + hh]
                    m = jnp.max(logits, axis=-1, keepdims=True)
                    p = jnp.exp(logits - m)
                    den = jnp.sum(p, axis=-1, keepdims=True)
                    o_h.append(_dot(p.astype(BF16), vv) / den)
                    lse_h.append(m + jnp.log(den))
                out_ref[at, cols] = jnp.where(low, o_h[0], o_h[1])
                out_ref[at, slice(ATT_WIDTH + hp * PAIR, ATT_WIDTH + (hp + 1) * PAIR)] = (
                    jnp.where(low, lse_h[0], lse_h[1]))
            return carry

        lax.fori_loop(0, per_step, block, 0)

    def cur(c):
        return pl.BlockSpec((rows, ATT_WIDTH), lambda s: (s, c))

    def prev(c):
        return pl.BlockSpec((rows, ATT_WIDTH), lambda s: (jnp.maximum(s - lag, 0), c))

    return pl.pallas_call(
        body, name=name, grid=(T // rows,),
        in_specs=[cur(qc), prev(kc), cur(kc), prev(vc), cur(vc),
                  pl.BlockSpec((2, None, ATT_HEADS, BLK, 2 * BLK), lambda s: (0, g, 0, 0, 0))],
        out_specs=pl.BlockSpec((rows, 2 * ATT_WIDTH), lambda s: (s, 0)),
        out_shape=jax.ShapeDtypeStruct((T, 2 * ATT_WIDTH), F32),
        compiler_params=_params(dimension_semantics=("parallel",)),
    )(qkv, qkv, qkv, qkv, qkv, bias)


def _permute_f32(p, x):
    hi = x.astype(BF16)
    rest = x - hi.astype(F32)
    mid = rest.astype(BF16)
    low = (rest - mid.astype(F32)).astype(BF16)
    return _dot(p, hi) + _dot(p, mid) + _dot(p, low)


def _attn_merge(parts):
    T = parts[0].shape[0]
    rows = min(T, REORDER_ROWS)
    n = len(parts)
    ncol = ATT_WIDTH // PAIR

    def body(*refs):
        p_refs, o_refs, l_refs = refs[:n], refs[n:2 * n], refs[2 * n:3 * n]
        o_ref, lse_ref = refs[3 * n], refs[3 * n + 1]

        def positions(ref, g, start):
            d = DILATIONS[g]
            if d == 1:
                return ref[start:start + REORDER_TILE, :]
            span, per = BLK * d, REORDER_TILE // d
            base, t = start // span * span, start % span // REORDER_TILE
            chunks = [ref[base + r * BLK + t * per:base + r * BLK + (t + 1) * per, :] for r in range(d)]
            return _permute_f32(p_refs[g][...], jnp.concatenate(chunks, axis=0))

        for start in range(0, rows, REORDER_TILE):
            ls = [positions(l_refs[g], g, start) for g in range(n)]
            m = functools.reduce(jnp.maximum, ls)
            es = [jnp.exp(l - m) for l in ls]
            tot = functools.reduce(lambda x, y: x + y, es)
            acc = functools.reduce(lambda x, y: x + y, [e * positions(o_refs[g], g, start) for g, e in enumerate(es)])
            o_ref[start:start + REORDER_TILE, :] = (acc / tot).astype(BF16)
            lse_ref[start:start + REORDER_TILE, :] = m + jnp.log(tot)

    matrix = pl.BlockSpec((REORDER_TILE, REORDER_TILE), lambda w, c: (0, 0))
    col = pl.BlockSpec((rows, PAIR), lambda w, c: (w, c))
    col_lse = pl.BlockSpec((rows, PAIR), lambda w, c: (w, ncol + c))
    return pl.pallas_call(
        body, name="attn_merge", grid=(T // rows, ncol),
        in_specs=[matrix] * n + [col] * n + [col_lse] * n, out_specs=[col, col],
        out_shape=[jax.ShapeDtypeStruct((T, ATT_WIDTH), BF16), jax.ShapeDtypeStruct((T, ATT_WIDTH), F32)],
        compiler_params=_params(dimension_semantics=("parallel", "parallel")),
    )(*[_reorder_matrix(max(d, 2), True) for d in DILATIONS], *parts, *parts)


def _attn_bwd(name, g, qkv, qc, kc, vc, do, o, lse, bias, stride, comm=None):
    T = qkv.shape[0]
    per_step, lag = _attn_steps(stride)
    rows = per_step * BLK
    steps = T // rows
    scale = HEAD_DIM ** -0.5
    n_cin = len(comm.inputs) if comm else 0
    n_cout = len(comm.out_shapes) if comm else 0
    assert comm is None or comm.mid is None

    def body(*refs):
        q_ref, kp_ref, kc_ref, vp_ref, vc_ref, do_ref, o_ref, lse_ref, bias_ref = refs[:9]
        comm_in = refs[9:9 + n_cin]
        dq_ref, dkv_ref, db_ref = refs[9 + n_cin:12 + n_cin]
        comm_out = refs[12 + n_cin:12 + n_cin + n_cout]
        carry_k, carry_v = refs[12 + n_cin + n_cout:14 + n_cin + n_cout]
        comm_sems = refs[14 + n_cin + n_cout:]
        step = pl.program_id(0)

        if comm is not None:
            @pl.when(step == 0)
            def _():
                comm.start(comm_in, comm_out, comm_sems)

            @pl.when(step == steps + lag - 1)
            def _():
                comm.end(comm_in, comm_out, comm_sems)

        slot0 = (step % lag) * per_step

        @pl.when(step == 0)
        def _():
            db_ref[...] = jnp.zeros_like(db_ref)
            carry_k[...] = jnp.zeros_like(carry_k)
            carry_v[...] = jnp.zeros_like(carry_v)

        @pl.when(step >= steps)
        def _():
            def flush(j, carry):
                at = pl.ds(pl.multiple_of(j * BLK, BLK), BLK)
                dkv_ref[at, :ATT_WIDTH] = carry_k[slot0 + j].astype(BF16)
                dkv_ref[at, ATT_WIDTH:] = carry_v[slot0 + j].astype(BF16)
                return carry

            lax.fori_loop(0, per_step, flush, 0)

        @pl.when(step < steps)
        def _():
            first = (step < lag).astype(jnp.int32)

            def block(j, carry):
                at = pl.ds(pl.multiple_of(j * BLK, BLK), BLK)
                ck_ref = carry_k.at[slot0 + j]
                cv_ref = carry_v.at[slot0 + j]
                for hp in range(ATT_HEADS // 2):
                    cols = slice(hp * PAIR, (hp + 1) * PAIR)
                    qp = q_ref[at, cols]
                    kk = jnp.concatenate([kp_ref[at, cols], kc_ref[at, cols]], axis=0)
                    vv = jnp.concatenate([vp_ref[at, cols], vc_ref[at, cols]], axis=0)
                    dop = do_ref[at, cols]
                    lsep = lse_ref[at, cols]
                    prod = dop.astype(F32) * o_ref[at, cols].astype(F32)
                    dq = jnp.zeros((BLK, PAIR), F32)
                    dk = jnp.zeros((2 * BLK, PAIR), F32)
                    dv = jnp.zeros((2 * BLK, PAIR), F32)
                    for hh in range(2):
                        lanes = _head_lanes(hh)
                        qm = jnp.where(lanes, qp, jnp.zeros_like(qp))
                        dom = jnp.where(lanes, dop, jnp.zeros_like(dop))
                        km = jnp.where(lanes, kk, jnp.zeros_like(kk))
                        delta = jnp.sum(jnp.where(lanes, prod, 0.0), axis=-1, keepdims=True)
                        lse_h = jnp.max(jnp.where(lanes, lsep, NEG_INF), axis=-1, keepdims=True)
                        s = _dot(qm, kk, NT) * scale
                        logits = s + bias_ref[first, 2 * hp + hh]
                        p = jnp.exp(logits - lse_h)
                        dv += _dot(p.astype(BF16), dom, TN)
                        ds = p * (_dot(dom, vv, NT) - delta)
                        db_ref[2 * hp + hh] += ds
                        dss = (ds * scale).astype(BF16)
                        dq += _dot(dss, km)
                        dk += _dot(dss, qm, TN)
                    dq_ref[at, cols] = dq.astype(BF16)
                    dkv_ref[at, cols] = (ck_ref[:, cols] + dk[:BLK]).astype(BF16)
                    dkv_ref[at, slice(ATT_WIDTH + hp * PAIR, ATT_WIDTH + (hp + 1) * PAIR)] = (
                        cv_ref[:, cols] + dv[:BLK]).astype(BF16)
                    ck_ref[:, cols] = dk[BLK:]
                    cv_ref[:, cols] = dv[BLK:]
                return carry

            lax.fori_loop(0, per_step, block, 0)

    last = steps - 1

    def cur(c):
        return pl.BlockSpec((rows, ATT_WIDTH), lambda s: (jnp.minimum(s, last), c))

    def prev(c):
        return pl.BlockSpec((rows, ATT_WIDTH), lambda s: (jnp.clip(s - lag, 0, last), c))

    dbias_shape = (ATT_HEADS, BLK, 2 * BLK)
    res = pl.pallas_call(
        body, name=name, grid=(steps + lag,),
        in_specs=[cur(qc), prev(kc), cur(kc), prev(vc), cur(vc), cur(0), cur(0), cur(0),
                  pl.BlockSpec((2, None, ATT_HEADS, BLK, 2 * BLK), lambda s: (0, g, 0, 0, 0))] + [ANY] * n_cin,
        out_specs=[cur(0), pl.BlockSpec((rows, 2 * ATT_WIDTH), lambda s: (jnp.clip(s - lag, 0, last), 0)),
                   pl.BlockSpec(dbias_shape, lambda s: (0, 0, 0))] + [ANY] * n_cout,
        out_shape=[jax.ShapeDtypeStruct((T, ATT_WIDTH), BF16), jax.ShapeDtypeStruct((T, 2 * ATT_WIDTH), BF16),
                   jax.ShapeDtypeStruct(dbias_shape, F32)] + (comm.out_shapes if comm else []),
        scratch_shapes=[pltpu.VMEM((stride, BLK, ATT_WIDTH), F32), pltpu.VMEM((stride, BLK, ATT_WIDTH), F32)]
        + (comm.scratch if comm else []),
        compiler_params=_params(dimension_semantics=("arbitrary",)),
    )(qkv, qkv, qkv, qkv, qkv, do, o, lse, bias, *(comm.inputs if comm else []))
    return res[0], res[1], res[2], list(res[3:])


REORDER_TILE = 256
REORDER_ROWS = 2048


def _reorder_matrix(d, inverse):
    per = REORDER_TILE // d
    p = np.zeros((REORDER_TILE, REORDER_TILE), np.float32)
    for src in range(REORDER_TILE):
        i, r = divmod(src, d)
        p[r * per + i, src] = 1.0
    return jnp.asarray(p.T if inverse else p, dtype=BF16)


def _reorder_rows(name, src, d, inverse, *, src_col=0, col_stride=1, ncols=1, dst=None, dst_col=0, dst_stride=1,
                  dst_blocks=None):
    T = src.shape[0]
    dtype = src.dtype
    span = BLK * d
    rows = max(span, min(T, REORDER_ROWS))
    per = REORDER_TILE // d
    tiles = span // REORDER_TILE
    dst_blocks = ncols if dst_blocks is None else dst_blocks

    def apply(p, x):
        return _dot(p, x).astype(BF16) if dtype == BF16 else _permute_f32(p, x)

    def body(*refs):
        p_ref, x_ref, o_ref = refs[0], refs[1], refs[-1]
        if d == 1:
            o_ref[...] = x_ref[...]
            return
        for s in range(rows // span):
            for t in range(tiles):
                base = s * span
                tile_rows = slice(base + t * REORDER_TILE, base + (t + 1) * REORDER_TILE)
                chunk = lambda r: slice(base + r * BLK + t * per, base + r * BLK + (t + 1) * per)
                if inverse:
                    gathered = jnp.concatenate([x_ref[chunk(r), :] for r in range(d)], axis=0)
                    o_ref[tile_rows, :] = apply(p_ref[...], gathered)
                else:
                    y = apply(p_ref[...], x_ref[tile_rows, :])
                    for r in range(d):
                        o_ref[chunk(r), :] = y[r * per:(r + 1) * per]

    in_specs = [pl.BlockSpec((REORDER_TILE, REORDER_TILE), lambda w, k: (0, 0)),
                pl.BlockSpec((rows, ATT_WIDTH), lambda w, k: (w, src_col + col_stride * k))]
    operands = [_reorder_matrix(max(d, 2), inverse), src]
    aliases = {}
    if dst is not None:
        in_specs.append(ANY)
        operands.append(dst)
        aliases = {2: 0}
    return pl.pallas_call(
        body, name=name, grid=(T // rows, ncols), in_specs=in_specs,
        out_specs=pl.BlockSpec((rows, ATT_WIDTH), lambda w, k: (w, dst_col + dst_stride * k)),
        out_shape=jax.ShapeDtypeStruct((T, dst_blocks * ATT_WIDTH), dtype),
        input_output_aliases=aliases,
        compiler_params=_params(dimension_semantics=("parallel", "parallel")),
    )(*operands)


def _group_qkv(qkv, g, d):
    NG = len(DILATIONS)
    if d == 1:
        return qkv, (g, NG + g, 2 * NG + g)
    return _reorder_rows(f"qkv_to_residues{g}", qkv, d, False, src_col=g, col_stride=NG, ncols=3), (0, 1, 2)


def _attention_fwd(qkv, bias):
    T = qkv.shape[0]
    parts = []
    for g, d in enumerate(DILATIONS):
        src, (qc, kc, vc) = _group_qkv(qkv, g, d)
        parts.append(_attn_fwd(f"attn_fwd_{g}", g, src, qc, kc, vc, bias, d))
    return _attn_merge(parts)


def _attention_bwd(qkv, do, o, lse, bias, comms):
    NG = len(DILATIONS)
    dqkv, dbs, carried = None, [], []
    for g, d in enumerate(DILATIONS):
        src, (qc, kc, vc) = _group_qkv(qkv, g, d)
        do_g, o_g, lse_g = do, o, lse
        if d > 1:
            do_g = _reorder_rows(f"do_to_residues{g}", do, d, False)
            o_g = _reorder_rows(f"o_to_residues{g}", o, d, False)
            lse_g = _reorder_rows(f"lse_to_residues{g}", lse, d, False)
        dq, dkv, db, sent = _attn_bwd(f"attn_bwd_{g}", g, src, qc, kc, vc, do_g, o_g, lse_g, bias, d, comm=comms[g])
        dqkv = _reorder_rows(f"dq_to_positions{g}", dq, d, True, dst=dqkv, dst_col=g, dst_blocks=3 * NG)
        dqkv = _reorder_rows(f"dkv_to_positions{g}", dkv, d, True, ncols=2, dst=dqkv, dst_col=NG + g, dst_stride=NG,
                             dst_blocks=3 * NG)
        dbs.append(db)
        carried.append(sent)
    return dqkv, jnp.stack(dbs), carried


def _other_chips(x, y):
    return [(1 - x, y), (x, 1 - y), (1 - x, 1 - y)]


def _shard_region(ref, shape, by_cols, chip, rows=None):
    R, C = shape
    start, size = (0, R) if rows is None else rows
    if by_cols:
        return ref.at[pl.ds(start, size), pl.ds(chip * C, C)]
    return ref.at[pl.ds(chip * R + start, size), :]


def _gather_weights(entries):
    n = len(entries)
    shapes = [e[0].shape[1:] for e in entries]

    def places(ins, outs, sems):
        send_sems, recv_sems, local_sems = sems
        x, y, c = lax.axis_index("x"), lax.axis_index("y"), lax.axis_index("c")

        def landing(f, px, py, pc):
            R = shapes[f][0]
            return _shard_region(outs[f], shapes[f], entries[f][2], 2 * px + py, rows=(pc * (R // 2), R // 2))

        def copy(f, k, block, to, src=None):
            dst = landing(f, *block)
            return pltpu.make_async_remote_copy(
                src_ref=dst if src is None else src, dst_ref=dst,
                send_sem=send_sems.at[6 * f + k], recv_sem=recv_sems.at[6 * f + k],
                device_id=to, device_id_type=MESH)

        def mine(f):
            dst = _shard_region(outs[f], shapes[f], entries[f][2], 2 * x + y)
            return pltpu.make_async_copy(ins[f].at[entries[f][1]], dst, local_sems.at[f])

        def first(f, j):
            R = shapes[f][0]
            src = ins[f].at[entries[f][1], pl.ds(c * (R // 2), R // 2), :]
            return copy(f, j, (x, y, c), (*_other_chips(x, y)[j], c), src=src)

        return x, y, c, copy, mine, first

    def start(ins, outs, sems):
        _, _, _, _, mine, first = places(ins, outs, sems)
        for f in range(n):
            mine(f).start()
        for j in range(3):
            for f in range(n):
                first(f, j).start()

    def mid(ins, outs, sems):
        x, y, c, copy, _, _ = places(ins, outs, sems)
        for j, chip in enumerate(_other_chips(x, y)):
            for f in range(n):
                copy(f, j, (*chip, c), (x, y, c)).wait_recv()
                copy(f, 3 + j, (*chip, c), (x, y, 1 - c)).start()

    def end(ins, outs, sems):
        x, y, c, copy, mine, first = places(ins, outs, sems)
        for j, chip in enumerate(_other_chips(x, y)):
            for f in range(n):
                copy(f, 3 + j, (*chip, 1 - c), (x, y, c)).wait_recv()
        for j, chip in enumerate(_other_chips(x, y)):
            for f in range(n):
                first(f, j).wait_send()
                copy(f, 3 + j, (*chip, c), (x, y, 1 - c)).wait_send()
        for f in range(n):
            mine(f).wait()

    def whole(f):
        R, C = shapes[f]
        return (R, N_CHIPS * C) if entries[f][2] else (N_CHIPS * R, C)

    return _Comm(
        [e[0] for e in entries], [jax.ShapeDtypeStruct(whole(f), BF16) for f in range(n)],
        [pltpu.SemaphoreType.DMA((6 * n,)), pltpu.SemaphoreType.DMA((6 * n,)), pltpu.SemaphoreType.DMA((n,))],
        start, end, mid)


def _scatter_grads(entries):
    n = len(entries)

    def copies(ins, outs, sems):
        send_sems, recv_sems, local_sems = sems
        x, y, c = lax.axis_index("x"), lax.axis_index("y"), lax.axis_index("c")
        me = 2 * x + y

        def piece(f, chip):
            return _shard_region(ins[f], entries[f][1], entries[f][2], chip)

        mine = [pltpu.make_async_copy(piece(f, me), outs[f].at[me], local_sems.at[f]) for f in range(n)]
        sends = [pltpu.make_async_remote_copy(
            src_ref=piece(f, 2 * px + py), dst_ref=outs[f].at[me],
            send_sem=send_sems.at[3 * f + j], recv_sem=recv_sems.at[3 * f + j],
            device_id=(px, py, c), device_id_type=MESH)
            for j, (px, py) in enumerate(_other_chips(x, y)) for f in range(n)]
        return mine, sends

    def start(ins, outs, sems):
        mine, sends = copies(ins, outs, sems)
        for cp in mine + sends:
            cp.start()

    def end(ins, outs, sems):
        mine, sends = copies(ins, outs, sems)
        for cp in sends + mine:
            cp.wait()

    return _Comm(
        [e[0] for e in entries], [jax.ShapeDtypeStruct((N_CHIPS,) + tuple(e[1]), BF16) for e in entries],
        [pltpu.SemaphoreType.DMA((3 * n,)), pltpu.SemaphoreType.DMA((3 * n,)), pltpu.SemaphoreType.DMA((n,))],
        start, end)


def _exchange_sibling(parts):
    n = len(parts)

    def copies(ins, outs, sems):
        send_sems, recv_sems = sems
        sibling = (lax.axis_index("x"), lax.axis_index("y"), 1 - lax.axis_index("c"))
        return [pltpu.make_async_remote_copy(src_ref=ins[i], dst_ref=outs[i], send_sem=send_sems.at[i],
                                             recv_sem=recv_sems.at[i], device_id=sibling, device_id_type=MESH)
                for i in range(n)]

    def start(ins, outs, sems):
        for cp in copies(ins, outs, sems):
            cp.start()

    def end(ins, outs, sems):
        for cp in copies(ins, outs, sems):
            cp.wait()

    return _Comm(parts, [jax.ShapeDtypeStruct(s.shape, s.dtype) for s in parts],
                 [pltpu.SemaphoreType.DMA((n,)), pltpu.SemaphoreType.DMA((n,))], start, end)


def _allgather_small(block):
    m_per, ncol = block.shape

    def places(ins, outs, sems):
        send_sems, recv_sems, local_sem = sems
        x, y, c = lax.axis_index("x"), lax.axis_index("y"), lax.axis_index("c")

        def rows(px, py, pc):
            return outs[0].at[4 * px + 2 * py + pc]

        def copy(k, block_of, to, src=None):
            return pltpu.make_async_remote_copy(
                src_ref=rows(*block_of) if src is None else src, dst_ref=rows(*block_of),
                send_sem=send_sems.at[k], recv_sem=recv_sems.at[k], device_id=to, device_id_type=MESH)

        mine = pltpu.make_async_copy(ins[0], rows(x, y, c), local_sem.at[0])
        first = [copy(0, (x, y, c), (x, y, 1 - c), src=ins[0])]
        first += [copy(1 + j, (x, y, c), (*chip, c), src=ins[0]) for j, chip in enumerate(_other_chips(x, y))]
        passed = [copy(4 + j, (*chip, c), (x, y, 1 - c)) for j, chip in enumerate(_other_chips(x, y))]
        return x, y, c, copy, mine, first, passed

    def start(ins, outs, sems):
        _, _, _, _, mine, first, _ = places(ins, outs, sems)
        for cp in [mine] + first:
            cp.start()

    def mid(ins, outs, sems):
        x, y, c, copy, _, _, passed = places(ins, outs, sems)
        for j, chip in enumerate(_other_chips(x, y)):
            copy(1 + j, (*chip, c), (x, y, c)).wait_recv()
            passed[j].start()

    def end(ins, outs, sems):
        x, y, c, copy, mine, first, passed = places(ins, outs, sems)
        copy(0, (x, y, 1 - c), (x, y, c)).wait_recv()
        for j, chip in enumerate(_other_chips(x, y)):
            copy(4 + j, (*chip, 1 - c), (x, y, c)).wait_recv()
        for cp in first + passed:
            cp.wait_send()
        mine.wait()

    return _Comm([block], [jax.ShapeDtypeStruct((N_DEV, m_per, ncol), block.dtype)],
                 [pltpu.SemaphoreType.DMA((7,)), pltpu.SemaphoreType.DMA((7,)), pltpu.SemaphoreType.DMA((1,))],
                 start, end, mid)


def _join_comms(*progs):
    def split(parts, counts):
        out, pos = [], 0
        for n in counts:
            out.append(parts[pos:pos + n])
            pos += n
        return out

    def phase(which):
        def run(ins, outs, sems):
            args = zip(split(ins, [len(p.inputs) for p in progs]), split(outs, [len(p.out_shapes) for p in progs]),
                       split(sems, [len(p.scratch) for p in progs]))
            for p, (i, o, s) in zip(progs, args):
                fn = getattr(p, which)
                if fn is not None:
                    fn(i, o, s)
        return run

    return _Comm([a for p in progs for a in p.inputs], [s for p in progs for s in p.out_shapes],
                 [s for p in progs for s in p.scratch], phase("start"), phase("end"), phase("mid"))


def _adamw(w, g, m, v):
    m = ADAM_B1 * m + (1.0 - ADAM_B1) * g
    v = ADAM_B2 * v + (1.0 - ADAM_B2) * jnp.square(g)
    m_hat = m / (1.0 - ADAM_B1 ** ADAM_STEP)
    v_hat = v / (1.0 - ADAM_B2 ** ADAM_STEP)
    delta = -ADAM_LR * (m_hat / (jnp.sqrt(v_hat) + ADAM_EPS) + ADAM_WD * w)
    return delta, m, v


def _flat_tile(rows):
    return min(rows, 512)


def _sum_pieces(name, layers):
    L = len(layers)
    P, R, C = layers[0].shape
    tr = _flat_tile(R)

    def body(*refs):
        out_ref = refs[L]
        for l in range(L):
            @pl.when(pl.program_id(0) == l)
            def _(p_ref=refs[l]):
                acc = p_ref[0].astype(F32)
                for j in range(1, P):
                    acc = acc + p_ref[j].astype(F32)
                out_ref[...] = acc

    return pl.pallas_call(
        body, name=name, grid=(L, R // tr),
        in_specs=[pl.BlockSpec((P, tr, C), lambda l, i: (0, i, 0)) for _ in range(L)],
        out_specs=pl.BlockSpec((None, tr, C), lambda l, i: (l, i, 0)),
        out_shape=jax.ShapeDtypeStruct((L, R, C), F32),
        compiler_params=_params(dimension_semantics=("parallel", "parallel")),
    )(*layers)


def _adam_pair(name, w, m, v, part_a, part_b):
    L, R, C = w.shape
    tr = _flat_tile(R)

    def body(w_ref, m_ref, v_ref, a_ref, b_ref, g_ref, d_ref, nm_ref, nv_ref):
        g = a_ref[...] + b_ref[...]
        g_ref[...] = g
        d_ref[...], nm_ref[...], nv_ref[...] = _adamw(w_ref[...], g, m_ref[...], v_ref[...])

    row = pl.BlockSpec((None, tr, C), lambda l, i: (l, i, 0))
    return pl.pallas_call(
        body, name=name, grid=(L, R // tr),
        in_specs=[row] * 5, out_specs=[row] * 4,
        out_shape=[jax.ShapeDtypeStruct((L, R, C), F32)] * 4,
        compiler_params=_params(dimension_semantics=("parallel", "parallel")),
    )(w, m, v, part_a, part_b)


def _adam_small(w, m, v, gathered):
    R, C = w.shape

    def body(w_ref, m_ref, v_ref, p_ref, g_ref, d_ref, nm_ref, nv_ref):
        g = p_ref[0]
        for j in range(1, N_DEV):
            g = g + p_ref[j]
        g_ref[...] = g
        d_ref[...], nm_ref[...], nv_ref[...] = _adamw(w_ref[...], g, m_ref[...], v_ref[...])

    return pl.pallas_call(
        body, name="adam_small",
        out_shape=[jax.ShapeDtypeStruct((R, C), F32)] * 4,
        compiler_params=_params(),
    )(w, m, v, gathered)


SMALL = ("mix_norm_g", "mlp_norm_g", "final_norm_g", "a_ln_g", "a_ln_b", "a_w_s", "a_b_s", "rel_bias")


def _pack_small(arrays, width):
    rows = []
    for a in arrays:
        flat = a.reshape(-1)
        pad = (-flat.shape[0]) % width
        rows.append(jnp.pad(flat, (0, pad)).reshape(-1, width))
    block = jnp.concatenate(rows, axis=0)
    return jnp.pad(block, ((0, (-block.shape[0]) % 8), (0, 0)))


def _unpack_small(block, shapes, width):
    out, row = [], 0
    for shape in shapes:
        size = int(np.prod(shape))
        nrows = -(-size // width)
        out.append(block[row:row + nrows].reshape(-1)[:size].reshape(shape))
        row += nrows
    return out


def kernel(x, mix_norm_g, mlp_norm_g, final_norm_g, a_w_in, a_ln_g, a_ln_b, a_w_s, a_b_s, a_w_out, b_w_qkv, b_w_out, rel_bias, w_up, w_down, loss_target, m_mix_norm_g, m_mlp_norm_g, m_final_norm_g, m_a_w_in, m_a_ln_g, m_a_ln_b, m_a_w_s, m_a_b_s, m_a_w_out, m_b_w_qkv, m_b_w_out, m_rel_bias, m_w_up, m_w_down, v_mix_norm_g, v_mlp_norm_g, v_final_norm_g, v_a_w_in, v_a_ln_g, v_a_ln_b, v_a_w_s, v_a_b_s, v_a_w_out, v_b_w_qkv, v_b_w_out, v_rel_bias, v_w_up, v_w_down):
    T, D = x.shape[1], x.shape[2]
    h0 = x.reshape(T, D)
    target = loss_target.reshape(T, D)
    G = a_w_s.shape[1]

    w_big = [a_w_in, a_w_out, b_w_qkv, b_w_out, w_up, w_down]
    m_big = [m_a_w_in, m_a_w_out, m_b_w_qkv, m_b_w_out, m_w_up, m_w_down]
    v_big = [v_a_w_in, v_a_w_out, v_b_w_qkv, v_b_w_out, v_w_up, v_w_down]
    by_cols = [True, False, True, True, True, False]
    s_in, s_out, s_qkv, s_bo, s_up, s_dn = [w.astype(BF16) for w in w_big]
    (W_in,) = _run_comm("gather_a", _gather_weights([(s_in, 0, True)]))

    tril = jnp.tril(jnp.ones((CHUNK, CHUNK), dtype=bool))
    w_tril = jnp.where(tril[None], a_w_s[0], 0.0).astype(BF16)
    w_tril_t = jnp.swapaxes(w_tril, 1, 2)
    b_rows = jnp.broadcast_to(a_b_s[0][:, :, None], (G, CHUNK, CHUNK))
    buckets = _bucket_maps()
    bias = _bias_build(rel_bias, buckets)

    QKV = s_qkv.shape[2] * N_CHIPS
    TM = 1024
    TK_WGRAD = 4096

    def matmul(name, a, b, mode, out, tm=TM, tn=1024, **kw):
        outs = out if isinstance(out, list) else [out]
        return _mm(name, a, b, mode, tm=tm, tn=tn, tk=a.shape[1], outs=outs, **kw)

    def norm_bwd(layer_gain, h, dres, copies=2):
        return dict(epi=_epi_rms_bwd(copies), extras=(h, dres), vecs=(layer_gain,), col_sums=1)

    def wgrad(name, a, b, tn=1024, tk=TK_WGRAD, comm=None):
        return _mm(name, a, b, "tn", tm=1024, tn=tn, tk=tk, outs=[BF16], comm=comm)

    def scatter(*which):
        return _scatter_grads([(g, w_big[i].shape[1:], by_cols[i]) for g, i in which])

    (a_pre, y0), (W_out, W_up0) = matmul("a_in", h0, W_in, "nn", BF16, norm_gain=mix_norm_g[0:1],
                                         comm=_gather_weights([(s_out, 0, False), (s_up, 0, True)]))
    z = _gate_fwd(a_pre, a_ln_g, a_ln_b, w_tril, b_rows)
    h1 = matmul("a_out", z, W_out, "nn", F32, epi=_epi_residual, extras=(h0,))
    (q1, y1), (W_dn0,) = matmul("mlp_up0", h1, W_up0, "nn", BF16, epi=_epi_relu2, norm_gain=mlp_norm_g[0:1],
                                comm=_gather_weights([(s_dn, 0, False)]))
    h2, (W_qkv, W_bo) = matmul("mlp_down0", q1, W_dn0, "nn", F32, tm=TM // 2, epi=_epi_residual, extras=(h1,),
                               comm=_gather_weights([(s_qkv, 0, True), (s_bo, 0, True)]))
    (qkv, y2), (W_up1,) = matmul("b_qkv", h2, W_qkv, "nn", BF16, tn=QKV // 4, norm_gain=mix_norm_g[1:2],
                                 comm=_gather_weights([(s_up, 1, True)]))
    o, lse = _attention_fwd(qkv, bias)
    h3 = matmul("b_out", o, W_bo, "nn", F32, epi=_epi_residual, extras=(h2,))
    (q3, y3), (W_dn1,) = matmul("mlp_up1", h3, W_up1, "nn", BF16, epi=_epi_relu2, norm_gain=mlp_norm_g[1:2],
                                comm=_gather_weights([(s_dn, 1, False)]))
    dh4, dh4_b, d_final_g, loss_row = matmul("mlp_down1", q3, W_dn1, "nn", [F32, BF16], tm=TM // 2, epi=_epi_loss_head,
                                             extras=(h3, target), vecs=(final_norm_g.reshape(1, D),), col_sums=2)

    dp3 = matmul("mlp_down_bwd1", dh4_b, W_dn1, "nt", BF16, epi=_epi_relu2_grad, extras=(q3,))
    g_dn1 = wgrad("mlp_down_wgrad1", q3, dh4_b)
    g_up1 = wgrad("mlp_up_wgrad1", y3, dp3)
    dh3, dh3_b, dg_mlp1 = matmul("mlp_up_bwd1", dp3, W_up1, "nt", [F32, BF16], tm=TM // 2,
                                 **norm_bwd(mlp_norm_g[1:2], h3, dh4))
    do = matmul("b_out_bwd", dh3_b, W_bo, "nt", BF16)
    g_bo = wgrad("b_out_wgrad", o, dh3_b)
    dqkv, dbias, ((r_dn1,), (r_up1,), (r_bo,)) = _attention_bwd(
        qkv, do, o, lse, bias, [scatter((g_dn1, 5)), scatter((g_up1, 4)), scatter((g_bo, 3))])
    d_rel_bias = _bias_scatter(dbias, buckets)
    dh2, dh2_b, dg_mix1 = matmul("b_qkv_bwd", dqkv, W_qkv, "nt", [F32, BF16], tm=TM // 2,
                                 **norm_bwd(mix_norm_g[1:2], h2, dh3))
    g_qkv = wgrad("b_qkv_wgrad", y2, dqkv, tn=QKV // 3, tk=TK_WGRAD // 2)
    dp1, (r_qkv,) = matmul("mlp_down_bwd0", dh2_b, W_dn0, "nt", BF16, epi=_epi_relu2_grad, extras=(q1,),
                           comm=scatter((g_qkv, 2)))
    g_up0 = wgrad("mlp_up_wgrad0", y1, dp1)
    g_dn0, (r_up0,) = wgrad("mlp_down_wgrad0", q1, dh2_b, comm=scatter((g_up0, 4)))
    (dh1, dh1_b, dg_mlp0), (r_dn0,) = matmul("mlp_up_bwd0", dp1, W_up0, "nt", [F32, BF16], tm=TM // 2,
                                             comm=scatter((g_dn0, 5)), **norm_bwd(mlp_norm_g[0:1], h1, dh2))
    dz = matmul("a_out_bwd", dh1_b, W_out, "nt", F32)
    g_out = wgrad("a_out_wgrad", z, dh1_b)
    da, d_ln_g, d_ln_b, d_w_s, d_b_s = _gate_bwd(a_pre, dz, a_ln_g, a_ln_b, w_tril, w_tril_t, b_rows)
    g_in, (r_out,) = wgrad("a_in_wgrad", y0, da, comm=scatter((g_out, 1)))
    grad_x, dg_mix0 = matmul("a_in_bwd", da, W_in, "nt", F32, **norm_bwd(mix_norm_g[0:1], h0, dh1, copies=1))

    unused = jnp.zeros((1, 1), F32)
    small_w = [mix_norm_g, mlp_norm_g, final_norm_g, a_ln_g, a_ln_b, a_w_s, a_b_s, rel_bias, unused]
    small_m = [m_mix_norm_g, m_mlp_norm_g, m_final_norm_g, m_a_ln_g, m_a_ln_b, m_a_w_s, m_a_b_s, m_rel_bias, unused]
    small_v = [v_mix_norm_g, v_mlp_norm_g, v_final_norm_g, v_a_ln_g, v_a_ln_b, v_a_w_s, v_a_b_s, v_rel_bias, unused]
    small_g = [jnp.concatenate([dg_mix0, dg_mix1]), jnp.concatenate([dg_mlp0, dg_mlp1]), d_final_g,
               d_ln_g, d_ln_b, d_w_s[None], d_b_s[None, :, :, 0], d_rel_bias, loss_row[:, :1]]
    width = max(D, 128)
    received = [None, [r_out], [r_qkv], [r_bo], [r_up0, r_up1], [r_dn0, r_dn1]]
    plane = [None] + [_sum_pieces(f"sum_pieces{i}", received[i]) for i in range(1, len(w_big))]
    tail = _run_comm("tail_comm", _join_comms(scatter((g_in, 0)), _exchange_sibling(plane[1:]),
                                              _allgather_small(_pack_small(small_g, width))))
    r_in, other, gathered_small = tail[0], [None] + list(tail[1:len(w_big)]), tail[len(w_big)]
    plane[0] = _sum_pieces("sum_pieces0", [r_in])
    (other[0],) = _run_comm("exchange_a_in", _exchange_sibling([plane[0]]))
    big_out = [_adam_pair(f"adam{i}", w_big[i], m_big[i], v_big[i], plane[i], other[i]) for i in range(len(w_big))]

    def unbig(kind):
        return dict(zip(["a_w_in", "a_w_out", "b_w_qkv", "b_w_out", "w_up", "w_down"], [b[kind] for b in big_out]))

    small_out = _adam_small(_pack_small(small_w, width), _pack_small(small_m, width), _pack_small(small_v, width),
                            gathered_small)
    shapes = [w.shape for w in small_w]
    loss = _unpack_small(small_out[0], shapes, width)[-1][0, 0]

    names = ["mix_norm_g", "mlp_norm_g", "final_norm_g", "a_w_in", "a_ln_g", "a_ln_b", "a_w_s", "a_b_s", "a_w_out",
             "b_w_qkv", "b_w_out", "rel_bias", "w_up", "w_down"]
    results = [loss, grad_x.reshape(x.shape)]
    for kind in range(4):
        table = dict(zip(SMALL, _unpack_small(small_out[kind], shapes, width)))
        table.update(unbig(kind))
        results += [table[n] for n in names]
    return tuple(results)
```

```python
import functools
import math

import numpy as np
import jax
import jax.numpy as jnp
from jax import lax
from jax.experimental import pallas as pl
from jax.experimental.pallas import tpu as pltpu

F32 = jnp.float32
BF16 = jnp.bfloat16
MESH = pl.DeviceIdType.MESH
ANY = pl.BlockSpec(memory_space=pl.ANY)

N_CHIPS = 4
N_DEV = 8
VMEM_LIMIT_BYTES = 56 * 1024 * 1024

EPS = 1e-6
NEG_INF = -1e30
CHUNK = 128
GROUP_DIM = 128
HEAD_DIM = 64
ATT_HEADS = 8
ATT_WIDTH = ATT_HEADS * HEAD_DIM
PAIR = 2 * HEAD_DIM
BLK = 128
DILATIONS = (1, 4, 16)
N_BUCKETS = 32
MAX_EXACT = N_BUCKETS // 2
REL_MAX_DISTANCE = 2048

ADAM_LR = 0.001
ADAM_B1 = 0.9
ADAM_B2 = 0.999
ADAM_EPS = 1e-08
ADAM_WD = 0.01
ADAM_STEP = 10

NN = (((1,), (0,)), ((), ()))
NT = (((1,), (1,)), ((), ()))
TN = (((0,), (0,)), ((), ()))


def _params(**kw):
    return pltpu.CompilerParams(vmem_limit_bytes=VMEM_LIMIT_BYTES, **kw)


def _dot(a, b, dims=NN):
    return lax.dot_general(a, b, dims, preferred_element_type=F32)


def _gelu(x):
    return 0.5 * x * (1.0 + lax.erf(x * math.sqrt(0.5)))


def _gelu_grad(x):
    return 0.5 * (1.0 + lax.erf(x * math.sqrt(0.5))) + x * jnp.exp(-0.5 * x * x) * (1.0 / math.sqrt(2.0 * math.pi))


def _mean(x):
    return jnp.mean(x, axis=-1, keepdims=True)


class _Comm:
    def __init__(self, inputs, out_shapes, scratch, start, end, mid=None):
        self.inputs, self.out_shapes, self.scratch = list(inputs), list(out_shapes), list(scratch)
        self.start, self.mid, self.end = start, mid, end


def _run_comm(name, comm):
    n_in, n_out = len(comm.inputs), len(comm.out_shapes)

    def body(*refs):
        parts = refs[:n_in], refs[n_in:n_in + n_out], refs[n_in + n_out:]
        comm.start(*parts)
        if comm.mid is not None:
            comm.mid(*parts)
        comm.end(*parts)

    return pl.pallas_call(
        body, name=name, in_specs=[ANY] * n_in, out_specs=[ANY] * n_out, out_shape=comm.out_shapes,
        scratch_shapes=comm.scratch, compiler_params=_params(),
    )(*comm.inputs)


def _mm(name, a, b, mode, *, tm, tn, tk, outs, epi=None, extras=(), vecs=(), col_sums=0, norm_gain=None, comm=None):
    if mode == "tn":
        K, M = a.shape
    else:
        M, K = a.shape
    N = b.shape[0] if mode == "nt" else b.shape[1]
    tm, tn, tk = min(tm, M), min(tn, N), min(tk, K)
    assert M % tm == 0 and N % tn == 0 and K % tk == 0, (name, M, N, K, tm, tn, tk)
    nk = K // tk
    grid = (M // tm, N // tn, nk)

    if mode == "tn":
        a_spec = pl.BlockSpec((tk, tm), lambda i, j, k: (k, i))
    else:
        a_spec = pl.BlockSpec((tm, tk), lambda i, j, k: (i, k))
    if mode == "nt":
        b_spec = pl.BlockSpec((tn, tk), lambda i, j, k: (j, k))
    else:
        b_spec = pl.BlockSpec((tk, tn), lambda i, j, k: (k, j))
    tile = pl.BlockSpec((tm, tn), lambda i, j, k: (i, j))
    vec = pl.BlockSpec((1, tn), lambda i, j, k: (0, j))
    normed = norm_gain is not None
    assert not normed or (mode == "nn" and nk == 1 and tm % grid[1] == 0)
    assert col_sums == 0 or grid[1] == 1
    out_shapes = [jax.ShapeDtypeStruct((M, N), dtype) for dtype in outs]
    out_specs = [tile for _ in outs]
    extra_specs = [tile for _ in extras] + [vec for _ in vecs]
    if normed:
        part_rows = tm // grid[1]
        last_part = M // part_rows - 1
        a_spec = pl.BlockSpec((tm, K), lambda i, j, k: (0, 0))
        out_shapes.append(jax.ShapeDtypeStruct((M, K), BF16))
        out_specs.append(pl.BlockSpec((part_rows, K), lambda i, j, k: (i * grid[1] + j, 0)))
        extra_specs.append(pl.BlockSpec((1, K), lambda i, j, k: (0, 0)))
        extra_specs.append(pl.BlockSpec((part_rows, K),
                                        lambda i, j, k: (jnp.minimum((i + 1) * grid[1] + j, last_part), 0)))
    out_shapes += [jax.ShapeDtypeStruct((1, N), F32)] * col_sums
    out_specs += [vec] * col_sums
    n_extra, n_out = len(extra_specs), len(out_shapes)
    n_tiles = len(outs)
    n_cin = len(comm.inputs) if comm else 0
    n_cout = len(comm.out_shapes) if comm else 0
    dims = {"nn": NN, "nt": NT, "tn": TN}[mode]
    steps = grid[0] * grid[1] * grid[2]

    def body(*refs):
        a_ref, b_ref = refs[0], refs[1]
        pos = 2
        extra_refs = refs[pos:pos + n_extra]
        pos += n_extra
        comm_in = refs[pos:pos + n_cin]
        pos += n_cin
        out_refs = refs[pos:pos + n_out]
        pos += n_out
        comm_out = refs[pos:pos + n_cout]
        pos += n_cout
        acc_ref = refs[pos] if nk > 1 else None
        pos += nk > 1
        y_refs = refs[pos:pos + 2 * normed]
        comm_sems = refs[pos + 2 * normed:]
        k = pl.program_id(2)
        step = (pl.program_id(0) * grid[1] + pl.program_id(1)) * nk + k

        if comm is not None:
            @pl.when(step == 0)
            def _():
                comm.start(comm_in, comm_out, comm_sems)

        def finish(acc):
            epi_args = [e[...] for e in extra_refs[:n_extra - 2 * normed]]
            res = epi(acc, *epi_args) if epi is not None else (acc,) * n_tiles
            for o, r in zip(out_refs[:n_tiles], res[:n_tiles]):
                o[...] = r.astype(o.dtype)
            if col_sums:
                sums = out_refs[n_out - col_sums:]

                @pl.when(pl.program_id(0) == 0)
                def _():
                    for o in sums:
                        o[...] = jnp.zeros_like(o)

                for o, r in zip(sums, res[n_tiles:]):
                    o[...] += r

        if normed:
            gain_ref, ahead_ref = extra_refs[-2], extra_refs[-1]

            def norm(hv):
                return (hv * lax.rsqrt(_mean(hv * hv) + EPS) * gain_ref[...]).astype(BF16)

            @pl.when(step == 0)
            def _():
                y_refs[0][...] = norm(a_ref[...])

            part_at = pl.ds(pl.multiple_of(pl.program_id(1) * part_rows, part_rows), part_rows)
            for parity in range(2):
                @pl.when(pl.program_id(0) % 2 == parity)
                def _(y_now=y_refs[parity], y_next=y_refs[1 - parity]):
                    finish(_dot(y_now[...], b_ref[...].astype(BF16), dims))
                    out_refs[n_tiles][...] = y_now[part_at, :]
                    y_next[part_at, :] = norm(ahead_ref[...])
        elif nk == 1:
            finish(_dot(a_ref[...].astype(BF16), b_ref[...].astype(BF16), dims))
        else:
            part = _dot(a_ref[...].astype(BF16), b_ref[...].astype(BF16), dims)

            @pl.when(k == 0)
            def _():
                acc_ref[...] = part

            @pl.when(k > 0)
            def _():
                acc_ref[...] += part

            @pl.when(k == nk - 1)
            def _():
                finish(acc_ref[...])

        if comm is not None:
            if comm.mid is not None:
                @pl.when(step == (3 * steps) // 4)
                def _():
                    comm.mid(comm_in, comm_out, comm_sems)

            @pl.when(step == steps - 1)
            def _():
                comm.end(comm_in, comm_out, comm_sems)

    sequential = comm is not None or normed or col_sums > 0
    order = ("arbitrary",) * 3 if sequential else ("parallel", "parallel", "arbitrary")
    scratch = [pltpu.VMEM((tm, tn), F32)] if nk > 1 else []
    if normed:
        scratch += [pltpu.VMEM((tm, K), BF16)] * 2
    res = pl.pallas_call(
        body, name=name, grid=grid,
        in_specs=[a_spec, b_spec] + extra_specs + [ANY] * n_cin,
        out_specs=out_specs + [ANY] * n_cout,
        out_shape=out_shapes + (comm.out_shapes if comm else []),
        scratch_shapes=scratch + (comm.scratch if comm else []),
        compiler_params=_params(dimension_semantics=order),
    )(a, b, *extras, *vecs, *([norm_gain, a] if normed else []), *(comm.inputs if comm else []))
    mm_out = res[0] if n_out == 1 else list(res[:n_out])
    return (mm_out, list(res[n_out:])) if comm else mm_out


def _epi_residual(acc, res):
    return (res + acc,)


def _epi_relu2(acc):
    return (jnp.square(jnp.maximum(acc, 0.0)),)


def _epi_rms_bwd(copies):
    def epi(acc, h, dres, g):
        r = lax.rsqrt(_mean(h * h) + EPS)
        hn = h * r
        dyg = acc * g
        dh = dres + r * (dyg - hn * _mean(dyg * hn))
        return (dh,) * copies + (jnp.sum(acc * hn, axis=0, keepdims=True),)
    return epi


def _epi_loss_head(acc, res, target, g):
    h = res + acc
    r = lax.rsqrt(_mean(h * h) + EPS)
    hn = h * r
    diff = hn * g - target
    loss = 0.5 * jnp.sum(_mean(diff * diff))
    dy = diff * (1.0 / h.shape[-1])
    dyg = dy * g
    dh = r * (dyg - hn * _mean(dyg * hn))
    return dh, dh, jnp.sum(dy * hn, axis=0, keepdims=True), jnp.full((1, h.shape[-1]), loss, F32)


def _epi_relu2_grad(acc, q):
    qf = q.astype(F32)
    return (acc * jnp.where(qf > 0.0, (2.0 * qf) * lax.rsqrt(qf), 0.0),)


def _row_tile(T):
    return min(T, 512)


def _gate_tile(T):
    return min(T, 256)


def _gate_fwd(a, ln_g, ln_b, w_tril, b_rows):
    T, W2 = a.shape
    W = W2 // 2
    G = W // GROUP_DIM
    tr = _gate_tile(T)

    def body(a_ref, lng_ref, lnb_ref, w_ref, b_ref, z_ref):
        u = _gelu(a_ref[:, :W].astype(F32))
        vg = _gelu(a_ref[:, W:].astype(F32))
        xc = vg - _mean(vg)
        vn = xc * lax.rsqrt(_mean(xc * xc) + EPS)
        vl = (vn * lng_ref[...] + lnb_ref[...]).astype(BF16)
        for n in range(tr // CHUNK):
            rows = slice(n * CHUNK, (n + 1) * CHUNK)
            for g in range(G):
                cols = slice(g * GROUP_DIM, (g + 1) * GROUP_DIM)
                gate = _dot(w_ref[g], vl[rows, cols]) + b_ref[g]
                z_ref[rows, cols] = (u[rows, cols] * gate).astype(BF16)

    vec = pl.BlockSpec((1, W), lambda i: (0, 0))
    grp = pl.BlockSpec((G, CHUNK, CHUNK), lambda i: (0, 0, 0))
    return pl.pallas_call(
        body, name="gate_fwd", grid=(T // tr,),
        in_specs=[pl.BlockSpec((tr, W2), lambda i: (i, 0)), vec, vec, grp, grp],
        out_specs=pl.BlockSpec((tr, W), lambda i: (i, 0)),
        out_shape=jax.ShapeDtypeStruct((T, W), BF16),
        compiler_params=_params(dimension_semantics=("parallel",)),
    )(a, ln_g, ln_b, w_tril, b_rows)


def _gate_bwd(a, dz, ln_g, ln_b, w_tril, w_tril_t, b_rows):
    T, W2 = a.shape
    W = W2 // 2
    G = W // GROUP_DIM
    tr = _gate_tile(T)
    steps = T // tr

    def body(a_ref, dz_ref, lng_ref, lnb_ref, w_ref, wt_ref, b_ref, da_ref, dlng_ref, dlnb_ref, dw_ref, dbs_ref, dvl_ref):
        step = pl.program_id(0)

        @pl.when(step == 0)
        def _():
            dlng_ref[...] = jnp.zeros_like(dlng_ref)
            dlnb_ref[...] = jnp.zeros_like(dlnb_ref)
            dw_ref[...] = jnp.zeros_like(dw_ref)
            dbs_ref[...] = jnp.zeros_like(dbs_ref)

        au = a_ref[:, :W].astype(F32)
        av = a_ref[:, W:].astype(F32)
        u = _gelu(au)
        vg = _gelu(av)
        xc = vg - _mean(vg)
        rstd = lax.rsqrt(_mean(xc * xc) + EPS)
        vn = xc * rstd
        lng = lng_ref[...]
        vl = (vn * lng + lnb_ref[...]).astype(BF16)
        du_scale = dz_ref[...] * _gelu_grad(au)
        dgate_all = dz_ref[...] * u
        for n in range(tr // CHUNK):
            rows = slice(n * CHUNK, (n + 1) * CHUNK)
            for g in range(G):
                cols = slice(g * GROUP_DIM, (g + 1) * GROUP_DIM)
                vlg = vl[rows, cols]
                gate = _dot(w_ref[g], vlg) + b_ref[g]
                da_ref[rows, cols] = (du_scale[rows, cols] * gate).astype(BF16)
                dgate = dgate_all[rows, cols]
                dbs_ref[g] += dgate
                dgate_b = dgate.astype(BF16)
                dw_ref[g] += _dot(dgate_b, vlg, NT)
                dvl_ref[rows, cols] = _dot(wt_ref[g], dgate_b)
        dvl = dvl_ref[...]
        dlnb_ref[...] += jnp.sum(dvl, axis=0, keepdims=True)
        dlng_ref[...] += jnp.sum(dvl * vn, axis=0, keepdims=True)
        dvn = dvl * lng
        dvg = rstd * (dvn - _mean(dvn) - vn * _mean(dvn * vn))
        da_ref[:, W:] = (dvg * _gelu_grad(av)).astype(BF16)

        @pl.when(step == steps - 1)
        def _():
            t_idx = lax.broadcasted_iota(jnp.int32, (CHUNK, CHUNK), 0)
            s_idx = lax.broadcasted_iota(jnp.int32, (CHUNK, CHUNK), 1)
            for g in range(G):
                dw_ref[g] = jnp.where(s_idx <= t_idx, dw_ref[g], 0.0)
                dbs_ref[g] = jnp.broadcast_to(jnp.sum(dbs_ref[g], axis=-1, keepdims=True), (CHUNK, CHUNK))

    vec = pl.BlockSpec((1, W), lambda i: (0, 0))
    grp = pl.BlockSpec((G, CHUNK, CHUNK), lambda i: (0, 0, 0))
    return pl.pallas_call(
        body, name="gate_bwd", grid=(steps,),
        in_specs=[pl.BlockSpec((tr, W2), lambda i: (i, 0)), pl.BlockSpec((tr, W), lambda i: (i, 0)),
                  vec, vec, grp, grp, grp],
        out_specs=[pl.BlockSpec((tr, W2), lambda i: (i, 0)), vec, vec, grp, grp],
        out_shape=[jax.ShapeDtypeStruct((T, W2), BF16), jax.ShapeDtypeStruct((1, W), F32),
                   jax.ShapeDtypeStruct((1, W), F32), jax.ShapeDtypeStruct((G, CHUNK, CHUNK), F32),
                   jax.ShapeDtypeStruct((G, CHUNK, CHUNK), F32)],
        scratch_shapes=[pltpu.VMEM((tr, W), F32)],
        compiler_params=_params(dimension_semantics=("arbitrary",)),
    )(a, dz, ln_g, ln_b, w_tril, w_tril_t, b_rows)


def _bucket_map(dilation):
    rel = BLK + np.arange(BLK)[:, None] - np.arange(2 * BLK)[None, :]
    dist = np.clip(rel, 0, BLK) * dilation
    nf = np.maximum(dist, 1).astype(np.float32)
    large = MAX_EXACT + (np.log(nf / np.float32(MAX_EXACT)) / np.float32(math.log(REL_MAX_DISTANCE / MAX_EXACT))
                         * np.float32(N_BUCKETS - MAX_EXACT)).astype(np.int32)
    large = np.minimum(large, N_BUCKETS - 1)
    return np.where(dist < MAX_EXACT, dist, large).astype(np.int32)


def _bucket_maps():
    return jnp.asarray(np.stack([_bucket_map(d) for d in DILATIONS]))


def _bias_build(rel_bias, buckets):
    NG = len(DILATIONS)

    def body(table_ref, bucket_ref, out_ref):
        for g in range(NG):
            bk = bucket_ref[g]
            for h in range(ATT_HEADS):
                out_ref[0, g, h] = jnp.zeros((BLK, 2 * BLK), F32)
            for b in range(N_BUCKETS):
                hit = bk == b
                for h in range(ATT_HEADS):
                    out_ref[0, g, h] = jnp.where(hit, table_ref[b, g * ATT_HEADS + h], out_ref[0, g, h])
            for h in range(ATT_HEADS):
                for first in range(2):
                    out_ref[first, g, h] = jnp.where(_window_mask(first), out_ref[0, g, h], NEG_INF)

    return pl.pallas_call(
        body, name="bias_build",
        in_specs=[pl.BlockSpec(memory_space=pltpu.SMEM), pl.BlockSpec(memory_space=pltpu.VMEM)],
        out_specs=pl.BlockSpec(memory_space=pltpu.VMEM),
        out_shape=jax.ShapeDtypeStruct((2, NG, ATT_HEADS, BLK, 2 * BLK), F32),
        compiler_params=_params(),
    )(rel_bias, buckets)


def _bias_scatter(dbias, buckets):
    NG = len(DILATIONS)

    def body(dbias_ref, bucket_ref, out_ref):
        for g in range(NG):
            bk = bucket_ref[g]
            for b in range(N_BUCKETS):
                hit = bk == b
                for h in range(ATT_HEADS):
                    out_ref[b, g * ATT_HEADS + h] = jnp.sum(jnp.where(hit, dbias_ref[g, h], 0.0))

    return pl.pallas_call(
        body, name="bias_scatter",
        in_specs=[pl.BlockSpec(memory_space=pltpu.VMEM), pl.BlockSpec(memory_space=pltpu.VMEM)],
        out_specs=pl.BlockSpec(memory_space=pltpu.SMEM),
        out_shape=jax.ShapeDtypeStruct((N_BUCKETS, NG * ATT_HEADS), F32),
        compiler_params=_params(),
    )(dbias, buckets)


def _window_mask(first):
    qi = lax.broadcasted_iota(jnp.int32, (BLK, 2 * BLK), 0)
    kj = lax.broadcasted_iota(jnp.int32, (BLK, 2 * BLK), 1)
    rel = BLK + qi - kj
    return (rel >= 0) & (rel <= BLK) & (kj >= BLK * first)


def _head_lanes(hh):
    lane = lax.broadcasted_iota(jnp.int32, (1, PAIR), 1)
    return (lane >= hh * HEAD_DIM) & (lane < (hh + 1) * HEAD_DIM)


ATT_STEP_BLOCKS = 8


def _attn_steps(stride):
    per_step = math.gcd(stride, ATT_STEP_BLOCKS)
    return per_step, stride // per_step


def _attn_fwd(name, g, qkv, qc, kc, vc, bias, stride):
    T = qkv.shape[0]
    per_step, lag = _attn_steps(stride)
    chained = stride == 1
    if chained:
        per_step, lag = min(ATT_STEP_BLOCKS, T // BLK), 1
    rows = per_step * BLK
    scale = HEAD_DIM ** -0.5

    def body(q_ref, kp_ref, kc_ref, vp_ref, vc_ref, bias_ref, out_ref, *chain):
        step = pl.program_id(0)
        low = _head_lanes(0)
        if chained:
            for cat, before, now in zip(chain, (kp_ref, vp_ref), (kc_ref, vc_ref)):
                cat[:BLK, :] = before[...]
                cat[BLK:, :] = now[...]

        def block(j, carry):
            at = pl.ds(pl.multiple_of(j * BLK, BLK), BLK)
            if chained:
                first = ((step == 0) & (j == 0)).astype(jnp.int32)
                after = pl.ds(pl.multiple_of((j + 1) * BLK, BLK), BLK)
                keys = lambda cols: jnp.concatenate([chain[0][at, cols], chain[0][after, cols]], axis=0)
                values = lambda cols: jnp.concatenate([chain[1][at, cols], chain[1][after, cols]], axis=0)
            else:
                first = (step < lag).astype(jnp.int32)
                keys = lambda cols: jnp.concatenate([kp_ref[at, cols], kc_ref[at, cols]], axis=0)
                values = lambda cols: jnp.concatenate([vp_ref[at, cols], vc_ref[at, cols]], axis=0)
            for hp in range(ATT_HEADS // 2):
                cols = slice(hp * PAIR, (hp + 1) * PAIR)
                qp = q_ref[at, cols]
                kk = keys(cols)
                vv = values(cols)
                o_h, lse_h = [], []
                for hh in range(2):
                    qm = jnp.where(_head_lanes(hh), qp, jnp.zeros_like(qp))
                    s = _dot(qm, kk, NT) * scale
                    logits = s + bias_ref[first, 2 * hp + hh]
                    m = jnp.max(logits, axis=-1, keepdims=True)
                    p = jnp.exp(logits - m)
                    den = jnp.sum(p, axis=-1, keepdims=True)
                    o_h.append(_dot(p.astype(BF16), vv) / den)
                    lse_h.append(m + jnp.log(den))
                out_ref[at, cols] = jnp.where(low, o_h[0], o_h[1])
                out_ref[at, slice(ATT_WIDTH + hp * PAIR, ATT_WIDTH + (hp + 1) * PAIR)] = (
                    jnp.where(low, lse_h[0], lse_h[1]))
            return carry

        lax.fori_loop(0, per_step, block, 0)

    def cur(c):
        return pl.BlockSpec((rows, ATT_WIDTH), lambda s: (s, c))

    def prev(c):
        if chained:
            return pl.BlockSpec((BLK, ATT_WIDTH), lambda s: (jnp.maximum(s * per_step - 1, 0), c))
        return pl.BlockSpec((rows, ATT_WIDTH), lambda s: (jnp.maximum(s - lag, 0), c))

    return pl.pallas_call(
        body, name=name, grid=(T // rows,),
        in_specs=[cur(qc), prev(kc), cur(kc), prev(vc), cur(vc),
                  pl.BlockSpec((2, None, ATT_HEADS, BLK, 2 * BLK), lambda s: (0, g, 0, 0, 0))],
        out_specs=pl.BlockSpec((rows, 2 * ATT_WIDTH), lambda s: (s, 0)),
        out_shape=jax.ShapeDtypeStruct((T, 2 * ATT_WIDTH), F32),
        scratch_shapes=[pltpu.VMEM((rows + BLK, ATT_WIDTH), BF16)] * 2 if chained else [],
        compiler_params=_params(dimension_semantics=("parallel",)),
    )(qkv, qkv, qkv, qkv, qkv, bias)


def _permute_f32(p, x):
    hi = x.astype(BF16)
    rest = x - hi.astype(F32)
    mid = rest.astype(BF16)
    low = (rest - mid.astype(F32)).astype(BF16)
    return _dot(p, hi) + _dot(p, mid) + _dot(p, low)


def _attn_merge(parts):
    T = parts[0].shape[0]
    rows = min(T, REORDER_ROWS)
    n = len(parts)
    ncol = ATT_WIDTH // PAIR

    def body(*refs):
        p_refs, o_refs, l_refs = refs[:n], refs[n:2 * n], refs[2 * n:3 * n]
        o_ref, lse_ref = refs[3 * n], refs[3 * n + 1]

        def positions(ref, g, start):
            d = DILATIONS[g]
            if d == 1:
                return ref[start:start + REORDER_TILE, :]
            span, per = BLK * d, REORDER_TILE // d
            base, t = start // span * span, start % span // REORDER_TILE
            chunks = [ref[base + r * BLK + t * per:base + r * BLK + (t + 1) * per, :] for r in range(d)]
            return _permute_f32(p_refs[g][...], jnp.concatenate(chunks, axis=0))

        for start in range(0, rows, REORDER_TILE):
            ls = [positions(l_refs[g], g, start) for g in range(n)]
            m = functools.reduce(jnp.maximum, ls)
            es = [jnp.exp(l - m) for l in ls]
            tot = functools.reduce(lambda x, y: x + y, es)
            acc = functools.reduce(lambda x, y: x + y, [e * positions(o_refs[g], g, start) for g, e in enumerate(es)])
            o_ref[start:start + REORDER_TILE, :] = (acc / tot).astype(BF16)
            lse_ref[start:start + REORDER_TILE, :] = m + jnp.log(tot)

    matrix = pl.BlockSpec((REORDER_TILE, REORDER_TILE), lambda w, c: (0, 0))
    col = pl.BlockSpec((rows, PAIR), lambda w, c: (w, c))
    col_lse = pl.BlockSpec((rows, PAIR), lambda w, c: (w, ncol + c))
    return pl.pallas_call(
        body, name="attn_merge", grid=(T // rows, ncol),
        in_specs=[matrix] * n + [col] * n + [col_lse] * n, out_specs=[col, col],
        out_shape=[jax.ShapeDtypeStruct((T, ATT_WIDTH), BF16), jax.ShapeDtypeStruct((T, ATT_WIDTH), F32)],
        compiler_params=_params(dimension_semantics=("parallel", "parallel")),
    )(*[_reorder_matrix(max(d, 2), True) for d in DILATIONS], *parts, *parts)


def _attn_bwd(name, g, qkv, qc, kc, vc, do, o, lse, bias, stride, comm=None):
    T = qkv.shape[0]
    per_step, lag = _attn_steps(stride)
    rows = per_step * BLK
    steps = T // rows
    scale = HEAD_DIM ** -0.5
    n_cin = len(comm.inputs) if comm else 0
    n_cout = len(comm.out_shapes) if comm else 0
    assert comm is None or comm.mid is None

    def body(*refs):
        q_ref, kp_ref, kc_ref, vp_ref, vc_ref, do_ref, o_ref, lse_ref, bias_ref = refs[:9]
        comm_in = refs[9:9 + n_cin]
        dq_ref, dkv_ref, db_ref = refs[9 + n_cin:12 + n_cin]
        comm_out = refs[12 + n_cin:12 + n_cin + n_cout]
        carry_k, carry_v = refs[12 + n_cin + n_cout:14 + n_cin + n_cout]
        comm_sems = refs[14 + n_cin + n_cout:]
        step = pl.program_id(0)

        if comm is not None:
            @pl.when(step == 0)
            def _():
                comm.start(comm_in, comm_out, comm_sems)

            @pl.when(step == steps + lag - 1)
            def _():
                comm.end(comm_in, comm_out, comm_sems)

        slot0 = (step % lag) * per_step

        @pl.when(step == 0)
        def _():
            db_ref[...] = jnp.zeros_like(db_ref)
            carry_k[...] = jnp.zeros_like(carry_k)
            carry_v[...] = jnp.zeros_like(carry_v)

        @pl.when(step >= steps)
        def _():
            def flush(j, carry):
                at = pl.ds(pl.multiple_of(j * BLK, BLK), BLK)
                dkv_ref[at, :ATT_WIDTH] = carry_k[slot0 + j].astype(BF16)
                dkv_ref[at, ATT_WIDTH:] = carry_v[slot0 + j].astype(BF16)
                return carry

            lax.fori_loop(0, per_step, flush, 0)

        @pl.when(step < steps)
        def _():
            first = (step < lag).astype(jnp.int32)

            def block(j, carry):
                at = pl.ds(pl.multiple_of(j * BLK, BLK), BLK)
                ck_ref = carry_k.at[slot0 + j]
                cv_ref = carry_v.at[slot0 + j]
                for hp in range(ATT_HEADS // 2):
                    cols = slice(hp * PAIR, (hp + 1) * PAIR)
                    qp = q_ref[at, cols]
                    kk = jnp.concatenate([kp_ref[at, cols], kc_ref[at, cols]], axis=0)
                    vv = jnp.concatenate([vp_ref[at, cols], vc_ref[at, cols]], axis=0)
                    dop = do_ref[at, cols]
                    lsep = lse_ref[at, cols]
                    prod = dop.astype(F32) * o_ref[at, cols].astype(F32)
                    dq = jnp.zeros((BLK, PAIR), F32)
                    dk = jnp.zeros((2 * BLK, PAIR), F32)
                    dv = jnp.zeros((2 * BLK, PAIR), F32)
                    for hh in range(2):
                        lanes = _head_lanes(hh)
                        qm = jnp.where(lanes, qp, jnp.zeros_like(qp))
                        dom = jnp.where(lanes, dop, jnp.zeros_like(dop))
                        km = jnp.where(lanes, kk, jnp.zeros_like(kk))
                        delta = jnp.sum(jnp.where(lanes, prod, 0.0), axis=-1, keepdims=True)
                        lse_h = jnp.max(jnp.where(lanes, lsep, NEG_INF), axis=-1, keepdims=True)
                        s = _dot(qm, kk, NT) * scale
                        logits = s + bias_ref[first, 2 * hp + hh]
                        p = jnp.exp(logits - lse_h)
                        dv += _dot(p.astype(BF16), dom, TN)
                        ds = p * (_dot(dom, vv, NT) - delta)
                        db_ref[2 * hp + hh] += ds
                        dss = (ds * scale).astype(BF16)
                        dq += _dot(dss, km)
                        dk += _dot(dss, qm, TN)
                    dq_ref[at, cols] = dq.astype(BF16)
                    dkv_ref[at, cols] = (ck_ref[:, cols] + dk[:BLK]).astype(BF16)
                    dkv_ref[at, slice(ATT_WIDTH + hp * PAIR, ATT_WIDTH + (hp + 1) * PAIR)] = (
                        cv_ref[:, cols] + dv[:BLK]).astype(BF16)
                    ck_ref[:, cols] = dk[BLK:]
                    cv_ref[:, cols] = dv[BLK:]
                return carry

            lax.fori_loop(0, per_step, block, 0)

    last = steps - 1

    def cur(c):
        return pl.BlockSpec((rows, ATT_WIDTH), lambda s: (jnp.minimum(s, last), c))

    def prev(c):
        return pl.BlockSpec((rows, ATT_WIDTH), lambda s: (jnp.clip(s - lag, 0, last), c))

    dbias_shape = (ATT_HEADS, BLK, 2 * BLK)
    res = pl.pallas_call(
        body, name=name, grid=(steps + lag,),
        in_specs=[cur(qc), prev(kc), cur(kc), prev(vc), cur(vc), cur(0), cur(0), cur(0),
                  pl.BlockSpec((2, None, ATT_HEADS, BLK, 2 * BLK), lambda s: (0, g, 0, 0, 0))] + [ANY] * n_cin,
        out_specs=[cur(0), pl.BlockSpec((rows, 2 * ATT_WIDTH), lambda s: (jnp.clip(s - lag, 0, last), 0)),
                   pl.BlockSpec(dbias_shape, lambda s: (0, 0, 0))] + [ANY] * n_cout,
        out_shape=[jax.ShapeDtypeStruct((T, ATT_WIDTH), BF16), jax.ShapeDtypeStruct((T, 2 * ATT_WIDTH), BF16),
                   jax.ShapeDtypeStruct(dbias_shape, F32)] + (comm.out_shapes if comm else []),
        scratch_shapes=[pltpu.VMEM((stride, BLK, ATT_WIDTH), F32), pltpu.VMEM((stride, BLK, ATT_WIDTH), F32)]
        + (comm.scratch if comm else []),
        compiler_params=_params(dimension_semantics=("arbitrary",)),
    )(qkv, qkv, qkv, qkv, qkv, do, o, lse, bias, *(comm.inputs if comm else []))
    return res[0], res[1], res[2], list(res[3:])


REORDER_TILE = 256
REORDER_ROWS = 2048


def _reorder_matrix(d, inverse):
    per = REORDER_TILE // d
    p = np.zeros((REORDER_TILE, REORDER_TILE), np.float32)
    for src in range(REORDER_TILE):
        i, r = divmod(src, d)
        p[r * per + i, src] = 1.0
    return jnp.asarray(p.T if inverse else p, dtype=BF16)


def _reorder_rows(name, src, d, inverse, *, src_col=0, col_stride=1, ncols=1, dst=None, dst_col=0, dst_stride=1,
                  dst_blocks=None):
    T = src.shape[0]
    dtype = src.dtype
    span = BLK * d
    rows = max(span, min(T, REORDER_ROWS))
    per = REORDER_TILE // d
    tiles = span // REORDER_TILE
    dst_blocks = ncols if dst_blocks is None else dst_blocks

    def apply(p, x):
        return _dot(p, x).astype(BF16) if dtype == BF16 else _permute_f32(p, x)

    def body(*refs):
        p_ref, x_ref, o_ref = refs[0], refs[1], refs[-1]
        if d == 1:
            o_ref[...] = x_ref[...]
            return
        for s in range(rows // span):
            for t in range(tiles):
                base = s * span
                tile_rows = slice(base + t * REORDER_TILE, base + (t + 1) * REORDER_TILE)
                chunk = lambda r: slice(base + r * BLK + t * per, base + r * BLK + (t + 1) * per)
                if inverse:
                    gathered = jnp.concatenate([x_ref[chunk(r), :] for r in range(d)], axis=0)
                    o_ref[tile_rows, :] = apply(p_ref[...], gathered)
                else:
                    y = apply(p_ref[...], x_ref[tile_rows, :])
                    for r in range(d):
                        o_ref[chunk(r), :] = y[r * per:(r + 1) * per]

    in_specs = [pl.BlockSpec((REORDER_TILE, REORDER_TILE), lambda w, k: (0, 0)),
                pl.BlockSpec((rows, ATT_WIDTH), lambda w, k: (w, src_col + col_stride * k))]
    operands = [_reorder_matrix(max(d, 2), inverse), src]
    aliases = {}
    if dst is not None:
        in_specs.append(ANY)
        operands.append(dst)
        aliases = {2: 0}
    return pl.pallas_call(
        body, name=name, grid=(T // rows, ncols), in_specs=in_specs,
        out_specs=pl.BlockSpec((rows, ATT_WIDTH), lambda w, k: (w, dst_col + dst_stride * k)),
        out_shape=jax.ShapeDtypeStruct((T, dst_blocks * ATT_WIDTH), dtype),
        input_output_aliases=aliases,
        compiler_params=_params(dimension_semantics=("parallel", "parallel")),
    )(*operands)


def _group_qkv(qkv, g, d):
    NG = len(DILATIONS)
    if d == 1:
        return qkv, (g, NG + g, 2 * NG + g)
    return _reorder_rows(f"qkv_to_residues{g}", qkv, d, False, src_col=g, col_stride=NG, ncols=3), (0, 1, 2)


def _attention_fwd(qkv, bias):
    T = qkv.shape[0]
    parts = []
    for g, d in enumerate(DILATIONS):
        src, (qc, kc, vc) = _group_qkv(qkv, g, d)
        parts.append(_attn_fwd(f"attn_fwd_{g}", g, src, qc, kc, vc, bias, d))
    return _attn_merge(parts)


def _attention_bwd(qkv, do, o, lse, bias, comms):
    NG = len(DILATIONS)
    dqkv, dbs, carried = None, [], []
    for g, d in enumerate(DILATIONS):
        src, (qc, kc, vc) = _group_qkv(qkv, g, d)
        do_g, o_g, lse_g = do, o, lse
        if d > 1:
            do_g = _reorder_rows(f"do_to_residues{g}", do, d, False)
            o_g = _reorder_rows(f"o_to_residues{g}", o, d, False)
            lse_g = _reorder_rows(f"lse_to_residues{g}", lse, d, False)
        dq, dkv, db, sent = _attn_bwd(f"attn_bwd_{g}", g, src, qc, kc, vc, do_g, o_g, lse_g, bias, d, comm=comms[g])
        dqkv = _reorder_rows(f"dq_to_positions{g}", dq, d, True, dst=dqkv, dst_col=g, dst_blocks=3 * NG)
        dqkv = _reorder_rows(f"dkv_to_positions{g}", dkv, d, True, ncols=2, dst=dqkv, dst_col=NG + g, dst_stride=NG,
                             dst_blocks=3 * NG)
        dbs.append(db)
        carried.append(sent)
    return dqkv, jnp.stack(dbs), carried


def _other_chips(x, y):
    return [(1 - x, y), (x, 1 - y), (1 - x, 1 - y)]


def _shard_region(ref, shape, by_cols, chip, rows=None):
    R, C = shape
    start, size = (0, R) if rows is None else rows
    if by_cols:
        return ref.at[pl.ds(start, size), pl.ds(chip * C, C)]
    return ref.at[pl.ds(chip * R + start, size), :]


def _gather_weights(entries):
    n = len(entries)
    shapes = [e[0].shape[1:] for e in entries]

    def places(ins, outs, sems):
        send_sems, recv_sems, local_sems = sems
        x, y, c = lax.axis_index("x"), lax.axis_index("y"), lax.axis_index("c")

        def landing(f, px, py, pc):
            R = shapes[f][0]
            return _shard_region(outs[f], shapes[f], entries[f][2], 2 * px + py, rows=(pc * (R // 2), R // 2))

        def copy(f, k, block, to, src=None):
            dst = landing(f, *block)
            return pltpu.make_async_remote_copy(
                src_ref=dst if src is None else src, dst_ref=dst,
                send_sem=send_sems.at[6 * f + k], recv_sem=recv_sems.at[6 * f + k],
                device_id=to, device_id_type=MESH)

        def mine(f):
            dst = _shard_region(outs[f], shapes[f], entries[f][2], 2 * x + y)
            return pltpu.make_async_copy(ins[f].at[entries[f][1]], dst, local_sems.at[f])

        def first(f, j):
            R = shapes[f][0]
            src = ins[f].at[entries[f][1], pl.ds(c * (R // 2), R // 2), :]
            return copy(f, j, (x, y, c), (*_other_chips(x, y)[j], c), src=src)

        return x, y, c, copy, mine, first

    def start(ins, outs, sems):
        _, _, _, _, mine, first = places(ins, outs, sems)
        for f in range(n):
            mine(f).start()
        for j in range(3):
            for f in range(n):
                first(f, j).start()

    def mid(ins, outs, sems):
        x, y, c, copy, _, _ = places(ins, outs, sems)
        for j, chip in enumerate(_other_chips(x, y)):
            for f in range(n):
                copy(f, j, (*chip, c), (x, y, c)).wait_recv()
                copy(f, 3 + j, (*chip, c), (x, y, 1 - c)).start()

    def end(ins, outs, sems):
        x, y, c, copy, mine, first = places(ins, outs, sems)
        for j, chip in enumerate(_other_chips(x, y)):
            for f in range(n):
                copy(f, 3 + j, (*chip, 1 - c), (x, y, c)).wait_recv()
        for j, chip in enumerate(_other_chips(x, y)):
            for f in range(n):
                first(f, j).wait_send()
                copy(f, 3 + j, (*chip, c), (x, y, 1 - c)).wait_send()
        for f in range(n):
            mine(f).wait()

    def whole(f):
        R, C = shapes[f]
        return (R, N_CHIPS * C) if entries[f][2] else (N_CHIPS * R, C)

    return _Comm(
        [e[0] for e in entries], [jax.ShapeDtypeStruct(whole(f), BF16) for f in range(n)],
        [pltpu.SemaphoreType.DMA((6 * n,)), pltpu.SemaphoreType.DMA((6 * n,)), pltpu.SemaphoreType.DMA((n,))],
        start, end, mid)


def _scatter_grads(entries):
    n = len(entries)

    def copies(ins, outs, sems):
        send_sems, recv_sems, local_sems = sems
        x, y, c = lax.axis_index("x"), lax.axis_index("y"), lax.axis_index("c")
        me = 2 * x + y

        def piece(f, chip):
            return _shard_region(ins[f], entries[f][1], entries[f][2], chip)

        mine = [pltpu.make_async_copy(piece(f, me), outs[f].at[me], local_sems.at[f]) for f in range(n)]
        sends = [pltpu.make_async_remote_copy(
            src_ref=piece(f, 2 * px + py), dst_ref=outs[f].at[me],
            send_sem=send_sems.at[3 * f + j], recv_sem=recv_sems.at[3 * f + j],
            device_id=(px, py, c), device_id_type=MESH)
            for j, (px, py) in enumerate(_other_chips(x, y)) for f in range(n)]
        return mine, sends

    def start(ins, outs, sems):
        mine, sends = copies(ins, outs, sems)
        for cp in mine + sends:
            cp.start()

    def end(ins, outs, sems):
        mine, sends = copies(ins, outs, sems)
        for cp in sends + mine:
            cp.wait()

    return _Comm(
        [e[0] for e in entries], [jax.ShapeDtypeStruct((N_CHIPS,) + tuple(e[1]), BF16) for e in entries],
        [pltpu.SemaphoreType.DMA((3 * n,)), pltpu.SemaphoreType.DMA((3 * n,)), pltpu.SemaphoreType.DMA((n,))],
        start, end)


def _exchange_sibling(parts):
    n = len(parts)

    def copies(ins, outs, sems):
        send_sems, recv_sems = sems
        sibling = (lax.axis_index("x"), lax.axis_index("y"), 1 - lax.axis_index("c"))
        return [pltpu.make_async_remote_copy(src_ref=ins[i], dst_ref=outs[i], send_sem=send_sems.at[i],
                                             recv_sem=recv_sems.at[i], device_id=sibling, device_id_type=MESH)
                for i in range(n)]

    def start(ins, outs, sems):
        for cp in copies(ins, outs, sems):
            cp.start()

    def end(ins, outs, sems):
        for cp in copies(ins, outs, sems):
            cp.wait()

    return _Comm(parts, [jax.ShapeDtypeStruct(s.shape, s.dtype) for s in parts],
                 [pltpu.SemaphoreType.DMA((n,)), pltpu.SemaphoreType.DMA((n,))], start, end)


def _allgather_small(block):
    m_per, ncol = block.shape

    def places(ins, outs, sems):
        send_sems, recv_sems, local_sem = sems
        x, y, c = lax.axis_index("x"), lax.axis_index("y"), lax.axis_index("c")

        def rows(px, py, pc):
            return outs[0].at[4 * px + 2 * py + pc]

        def copy(k, block_of, to, src=None):
            return pltpu.make_async_remote_copy(
                src_ref=rows(*block_of) if src is None else src, dst_ref=rows(*block_of),
                send_sem=send_sems.at[k], recv_sem=recv_sems.at[k], device_id=to, device_id_type=MESH)

        mine = pltpu.make_async_copy(ins[0], rows(x, y, c), local_sem.at[0])
        first = [copy(0, (x, y, c), (x, y, 1 - c), src=ins[0])]
        first += [copy(1 + j, (x, y, c), (*chip, c), src=ins[0]) for j, chip in enumerate(_other_chips(x, y))]
        passed = [copy(4 + j, (*chip, c), (x, y, 1 - c)) for j, chip in enumerate(_other_chips(x, y))]
        return x, y, c, copy, mine, first, passed

    def start(ins, outs, sems):
        _, _, _, _, mine, first, _ = places(ins, outs, sems)
        for cp in [mine] + first:
            cp.start()

    def mid(ins, outs, sems):
        x, y, c, copy, _, _, passed = places(ins, outs, sems)
        for j, chip in enumerate(_other_chips(x, y)):
            copy(1 + j, (*chip, c), (x, y, c)).wait_recv()
            passed[j].start()

    def end(ins, outs, sems):
        x, y, c, copy, mine, first, passed = places(ins, outs, sems)
        copy(0, (x, y, 1 - c), (x, y, c)).wait_recv()
        for j, chip in enumerate(_other_chips(x, y)):
            copy(4 + j, (*chip, 1 - c), (x, y, c)).wait_recv()
        for cp in first + passed:
            cp.wait_send()
        mine.wait()

    return _Comm([block], [jax.ShapeDtypeStruct((N_DEV, m_per, ncol), block.dtype)],
                 [pltpu.SemaphoreType.DMA((7,)), pltpu.SemaphoreType.DMA((7,)), pltpu.SemaphoreType.DMA((1,))],
                 start, end, mid)


def _join_comms(*progs):
    def split(parts, counts):
        out, pos = [], 0
        for n in counts:
            out.append(parts[pos:pos + n])
            pos += n
        return out

    def phase(which):
        def run(ins, outs, sems):
            args = zip(split(ins, [len(p.inputs) for p in progs]), split(outs, [len(p.out_shapes) for p in progs]),
                       split(sems, [len(p.scratch) for p in progs]))
            for p, (i, o, s) in zip(progs, args):
                fn = getattr(p, which)
                if fn is not None:
                    fn(i, o, s)
        return run

    return _Comm([a for p in progs for a in p.inputs], [s for p in progs for s in p.out_shapes],
                 [s for p in progs for s in p.scratch], phase("start"), phase("end"), phase("mid"))


def _adamw(w, g, m, v):
    m = ADAM_B1 * m + (1.0 - ADAM_B1) * g
    v = ADAM_B2 * v + (1.0 - ADAM_B2) * jnp.square(g)
    m_hat = m / (1.0 - ADAM_B1 ** ADAM_STEP)
    v_hat = v / (1.0 - ADAM_B2 ** ADAM_STEP)
    delta = -ADAM_LR * (m_hat / (jnp.sqrt(v_hat) + ADAM_EPS) + ADAM_WD * w)
    return delta, m, v


def _flat_tile(rows):
    return min(rows, 512)


def _sum_pieces(name, layers):
    L = len(layers)
    P, R, C = layers[0].shape
    tr = _flat_tile(R)

    def body(*refs):
        out_ref = refs[L]
        for l in range(L):
            @pl.when(pl.program_id(0) == l)
            def _(p_ref=refs[l]):
                acc = p_ref[0].astype(F32)
                for j in range(1, P):
                    acc = acc + p_ref[j].astype(F32)
                out_ref[...] = acc

    return pl.pallas_call(
        body, name=name, grid=(L, R // tr),
        in_specs=[pl.BlockSpec((P, tr, C), lambda l, i: (0, i, 0)) for _ in range(L)],
        out_specs=pl.BlockSpec((None, tr, C), lambda l, i: (l, i, 0)),
        out_shape=jax.ShapeDtypeStruct((L, R, C), F32),
        compiler_params=_params(dimension_semantics=("parallel", "parallel")),
    )(*layers)


def _adam_pair(name, w, m, v, part_a, part_b):
    L, R, C = w.shape
    tr = _flat_tile(R)

    def body(w_ref, m_ref, v_ref, a_ref, b_ref, g_ref, d_ref, nm_ref, nv_ref):
        g = a_ref[...] + b_ref[...]
        g_ref[...] = g
        d_ref[...], nm_ref[...], nv_ref[...] = _adamw(w_ref[...], g, m_ref[...], v_ref[...])

    row = pl.BlockSpec((None, tr, C), lambda l, i: (l, i, 0))
    return pl.pallas_call(
        body, name=name, grid=(L, R // tr),
        in_specs=[row] * 5, out_specs=[row] * 4,
        out_shape=[jax.ShapeDtypeStruct((L, R, C), F32)] * 4,
        compiler_params=_params(dimension_semantics=("parallel", "parallel")),
    )(w, m, v, part_a, part_b)


def _adam_small(w, m, v, gathered):
    R, C = w.shape

    def body(w_ref, m_ref, v_ref, p_ref, g_ref, d_ref, nm_ref, nv_ref):
        g = p_ref[0]
        for j in range(1, N_DEV):
            g = g + p_ref[j]
        g_ref[...] = g
        d_ref[...], nm_ref[...], nv_ref[...] = _adamw(w_ref[...], g, m_ref[...], v_ref[...])

    return pl.pallas_call(
        body, name="adam_small",
        out_shape=[jax.ShapeDtypeStruct((R, C), F32)] * 4,
        compiler_params=_params(),
    )(w, m, v, gathered)


SMALL = ("mix_norm_g", "mlp_norm_g", "final_norm_g", "a_ln_g", "a_ln_b", "a_w_s", "a_b_s", "rel_bias")


def _pack_small(arrays, width):
    rows = []
    for a in arrays:
        flat = a.reshape(-1)
        pad = (-flat.shape[0]) % width
        rows.append(jnp.pad(flat, (0, pad)).reshape(-1, width))
    block = jnp.concatenate(rows, axis=0)
    return jnp.pad(block, ((0, (-block.shape[0]) % 8), (0, 0)))


def _unpack_small(block, shapes, width):
    out, row = [], 0
    for shape in shapes:
        size = int(np.prod(shape))
        nrows = -(-size // width)
        out.append(block[row:row + nrows].reshape(-1)[:size].reshape(shape))
        row += nrows
    return out


def kernel(x, mix_norm_g, mlp_norm_g, final_norm_g, a_w_in, a_ln_g, a_ln_b, a_w_s, a_b_s, a_w_out, b_w_qkv, b_w_out, rel_bias, w_up, w_down, loss_target, m_mix_norm_g, m_mlp_norm_g, m_final_norm_g, m_a_w_in, m_a_ln_g, m_a_ln_b, m_a_w_s, m_a_b_s, m_a_w_out, m_b_w_qkv, m_b_w_out, m_rel_bias, m_w_up, m_w_down, v_mix_norm_g, v_mlp_norm_g, v_final_norm_g, v_a_w_in, v_a_ln_g, v_a_ln_b, v_a_w_s, v_a_b_s, v_a_w_out, v_b_w_qkv, v_b_w_out, v_rel_bias, v_w_up, v_w_down):
    T, D = x.shape[1], x.shape[2]
    h0 = x.reshape(T, D)
    target = loss_target.reshape(T, D)
    G = a_w_s.shape[1]

    w_big = [a_w_in, a_w_out, b_w_qkv, b_w_out, w_up, w_down]
    m_big = [m_a_w_in, m_a_w_out, m_b_w_qkv, m_b_w_out, m_w_up, m_w_down]
    v_big = [v_a_w_in, v_a_w_out, v_b_w_qkv, v_b_w_out, v_w_up, v_w_down]
    by_cols = [True, False, True, True, True, False]
    s_in, s_out, s_qkv, s_bo, s_up, s_dn = [w.astype(BF16) for w in w_big]
    W_in, W_out = _run_comm("gather_a", _gather_weights([(s_in, 0, True), (s_out, 0, False)]))

    tril = jnp.tril(jnp.ones((CHUNK, CHUNK), dtype=bool))
    w_tril = jnp.where(tril[None], a_w_s[0], 0.0).astype(BF16)
    w_tril_t = jnp.swapaxes(w_tril, 1, 2)
    b_rows = jnp.broadcast_to(a_b_s[0][:, :, None], (G, CHUNK, CHUNK))
    buckets = _bucket_maps()
    bias = _bias_build(rel_bias, buckets)

    QKV = s_qkv.shape[2] * N_CHIPS
    TM = 1024
    TK_WGRAD = 4096

    def matmul(name, a, b, mode, out, tm=TM, tn=1024, **kw):
        outs = out if isinstance(out, list) else [out]
        return _mm(name, a, b, mode, tm=tm, tn=tn, tk=a.shape[1], outs=outs, **kw)

    def norm_bwd(layer_gain, h, dres, copies=2):
        return dict(epi=_epi_rms_bwd(copies), extras=(h, dres), vecs=(layer_gain,), col_sums=1)

    def wgrad(name, a, b, tn=1024, tk=TK_WGRAD, comm=None):
        return _mm(name, a, b, "tn", tm=1024, tn=tn, tk=tk, outs=[BF16], comm=comm)

    def scatter(*which):
        return _scatter_grads([(g, w_big[i].shape[1:], by_cols[i]) for g, i in which])

    (a_pre, y0), (W_up0,) = matmul("a_in", h0, W_in, "nn", BF16, norm_gain=mix_norm_g[0:1],
                                   comm=_gather_weights([(s_up, 0, True)]))
    z = _gate_fwd(a_pre, a_ln_g, a_ln_b, w_tril, b_rows)
    h1 = matmul("a_out", z, W_out, "nn", F32, epi=_epi_residual, extras=(h0,))
    (q1, y1), (W_dn0,) = matmul("mlp_up0", h1, W_up0, "nn", BF16, epi=_epi_relu2, norm_gain=mlp_norm_g[0:1],
                                comm=_gather_weights([(s_dn, 0, False)]))
    h2, (W_qkv, W_bo) = matmul("mlp_down0", q1, W_dn0, "nn", F32, tm=TM // 2, epi=_epi_residual, extras=(h1,),
                               comm=_gather_weights([(s_qkv, 0, True), (s_bo, 0, True)]))
    (qkv, y2), (W_up1,) = matmul("b_qkv", h2, W_qkv, "nn", BF16, tn=QKV // 4, norm_gain=mix_norm_g[1:2],
                                 comm=_gather_weights([(s_up, 1, True)]))
    o, lse = _attention_fwd(qkv, bias)
    h3 = matmul("b_out", o, W_bo, "nn", F32, epi=_epi_residual, extras=(h2,))
    (q3, y3), (W_dn1,) = matmul("mlp_up1", h3, W_up1, "nn", BF16, epi=_epi_relu2, norm_gain=mlp_norm_g[1:2],
                                comm=_gather_weights([(s_dn, 1, False)]))
    dh4, dh4_b, d_final_g, loss_row = matmul("mlp_down1", q3, W_dn1, "nn", [F32, BF16], tm=TM // 2, epi=_epi_loss_head,
                                             extras=(h3, target), vecs=(final_norm_g.reshape(1, D),), col_sums=2)

    dp3 = matmul("mlp_down_bwd1", dh4_b, W_dn1, "nt", BF16, epi=_epi_relu2_grad, extras=(q3,))
    g_dn1 = wgrad("mlp_down_wgrad1", q3, dh4_b)
    g_up1 = wgrad("mlp_up_wgrad1", y3, dp3)
    dh3, dh3_b, dg_mlp1 = matmul("mlp_up_bwd1", dp3, W_up1, "nt", [F32, BF16], tm=TM // 2,
                                 **norm_bwd(mlp_norm_g[1:2], h3, dh4))
    do = matmul("b_out_bwd", dh3_b, W_bo, "nt", BF16)
    g_bo = wgrad("b_out_wgrad", o, dh3_b)
    dqkv, dbias, ((r_dn1,), (r_up1,), (r_bo,)) = _attention_bwd(
        qkv, do, o, lse, bias, [scatter((g_dn1, 5)), scatter((g_up1, 4)), scatter((g_bo, 3))])
    d_rel_bias = _bias_scatter(dbias, buckets)
    dh2, dh2_b, dg_mix1 = matmul("b_qkv_bwd", dqkv, W_qkv, "nt", [F32, BF16], tm=TM // 2,
                                 **norm_bwd(mix_norm_g[1:2], h2, dh3))
    g_qkv = wgrad("b_qkv_wgrad", y2, dqkv, tn=QKV // 3, tk=TK_WGRAD // 2)
    dp1, (r_qkv,) = matmul("mlp_down_bwd0", dh2_b, W_dn0, "nt", BF16, epi=_epi_relu2_grad, extras=(q1,),
                           comm=scatter((g_qkv, 2)))
    g_up0 = wgrad("mlp_up_wgrad0", y1, dp1)
    g_dn0, (r_up0,) = wgrad("mlp_down_wgrad0", q1, dh2_b, comm=scatter((g_up0, 4)))
    (dh1, dh1_b, dg_mlp0), (r_dn0,) = matmul("mlp_up_bwd0", dp1, W_up0, "nt", [F32, BF16], tm=TM // 2,
                                             comm=scatter((g_dn0, 5)), **norm_bwd(mlp_norm_g[0:1], h1, dh2))
    dz = matmul("a_out_bwd", dh1_b, W_out, "nt", F32)
    g_out = wgrad("a_out_wgrad", z, dh1_b)
    da, d_ln_g, d_ln_b, d_w_s, d_b_s = _gate_bwd(a_pre, dz, a_ln_g, a_ln_b, w_tril, w_tril_t, b_rows)
    g_in, (r_out,) = wgrad("a_in_wgrad", y0, da, comm=scatter((g_out, 1)))
    grad_x, dg_mix0 = matmul("a_in_bwd", da, W_in, "nt", F32, **norm_bwd(mix_norm_g[0:1], h0, dh1, copies=1))

    unused = jnp.zeros((1, 1), F32)
    small_w = [mix_norm_g, mlp_norm_g, final_norm_g, a_ln_g, a_ln_b, a_w_s, a_b_s, rel_bias, unused]
    small_m = [m_mix_norm_g, m_mlp_norm_g, m_final_norm_g, m_a_ln_g, m_a_ln_b, m_a_w_s, m_a_b_s, m_rel_bias, unused]
    small_v = [v_mix_norm_g, v_mlp_norm_g, v_final_norm_g, v_a_ln_g, v_a_ln_b, v_a_w_s, v_a_b_s, v_rel_bias, unused]
    small_g = [jnp.concatenate([dg_mix0, dg_mix1]), jnp.concatenate([dg_mlp0, dg_mlp1]), d_final_g,
               d_ln_g, d_ln_b, d_w_s[None], d_b_s[None, :, :, 0], d_rel_bias, loss_row[:, :1]]
    width = max(D, 128)
    received = [None, [r_out], [r_qkv], [r_bo], [r_up0, r_up1], [r_dn0, r_dn1]]
    plane = [None] + [_sum_pieces(f"sum_pieces{i}", received[i]) for i in range(1, len(w_big))]
    tail = _run_comm("tail_comm", _join_comms(scatter((g_in, 0)), _exchange_sibling(plane[1:]),
                                              _allgather_small(_pack_small(small_g, width))))
    r_in, other, gathered_small = tail[0], [None] + list(tail[1:len(w_big)]), tail[len(w_big)]
    plane[0] = _sum_pieces("sum_pieces0", [r_in])
    (other[0],) = _run_comm("exchange_a_in", _exchange_sibling([plane[0]]))
    big_out = [_adam_pair(f"adam{i}", w_big[i], m_big[i], v_big[i], plane[i], other[i]) for i in range(len(w_big))]

    def unbig(kind):
        return dict(zip(["a_w_in", "a_w_out", "b_w_qkv", "b_w_out", "w_up", "w_down"], [b[kind] for b in big_out]))

    small_out = _adam_small(_pack_small(small_w, width), _pack_small(small_m, width), _pack_small(small_v, width),
                            gathered_small)
    shapes = [w.shape for w in small_w]
    loss = _unpack_small(small_out[0], shapes, width)[-1][0, 0]

    names = ["mix_norm_g", "mlp_norm_g", "final_norm_g", "a_w_in", "a_ln_g", "a_ln_b", "a_w_s", "a_b_s", "a_w_out",
             "b_w_qkv", "b_w_out", "rel_bias", "w_up", "w_down"]
    results = [loss, grad_x.reshape(x.shape)]
    for kind in range(4):
        table = dict(zip(SMALL, _unpack_small(small_out[kind], shapes, width)))
        table.update(unbig(kind))
        results += [table[n] for n in names]
    return tuple(results)
```

```python
import functools
import math

import numpy as np
import jax
import jax.numpy as jnp
from jax import lax
from jax.experimental import pallas as pl
from jax.experimental.pallas import tpu as pltpu

F32 = jnp.float32
BF16 = jnp.bfloat16
MESH = pl.DeviceIdType.MESH
ANY = pl.BlockSpec(memory_space=pl.ANY)

N_CHIPS = 4
N_DEV = 8
VMEM_LIMIT_BYTES = 56 * 1024 * 1024

EPS = 1e-6
NEG_INF = -1e30
CHUNK = 128
GROUP_DIM = 128
HEAD_DIM = 64
ATT_HEADS = 8
ATT_WIDTH = ATT_HEADS * HEAD_DIM
PAIR = 2 * HEAD_DIM
BLK = 128
DILATIONS = (1, 4, 16)
N_BUCKETS = 32
MAX_EXACT = N_BUCKETS // 2
REL_MAX_DISTANCE = 2048

ADAM_LR = 0.001
ADAM_B1 = 0.9
ADAM_B2 = 0.999
ADAM_EPS = 1e-08
ADAM_WD = 0.01
ADAM_STEP = 10

NN = (((1,), (0,)), ((), ()))
NT = (((1,), (1,)), ((), ()))
TN = (((0,), (0,)), ((), ()))


def _params(**kw):
    return pltpu.CompilerParams(vmem_limit_bytes=VMEM_LIMIT_BYTES, **kw)


def _dot(a, b, dims=NN):
    return lax.dot_general(a, b, dims, preferred_element_type=F32)


def _gelu(x):
    return 0.5 * x * (1.0 + lax.erf(x * math.sqrt(0.5)))


def _gelu_grad(x):
    return 0.5 * (1.0 + lax.erf(x * math.sqrt(0.5))) + x * jnp.exp(-0.5 * x * x) * (1.0 / math.sqrt(2.0 * math.pi))


def _mean(x):
    return jnp.mean(x, axis=-1, keepdims=True)


class _Comm:
    def __init__(self, inputs, out_shapes, scratch, start, end, mid=None):
        self.inputs, self.out_shapes, self.scratch = list(inputs), list(out_shapes), list(scratch)
        self.start, self.mid, self.end = start, mid, end


def _run_comm(name, comm):
    n_in, n_out = len(comm.inputs), len(comm.out_shapes)

    def body(*refs):
        parts = refs[:n_in], refs[n_in:n_in + n_out], refs[n_in + n_out:]
        comm.start(*parts)
        if comm.mid is not None:
            comm.mid(*parts)
        comm.end(*parts)

    return pl.pallas_call(
        body, name=name, in_specs=[ANY] * n_in, out_specs=[ANY] * n_out, out_shape=comm.out_shapes,
        scratch_shapes=comm.scratch, compiler_params=_params(),
    )(*comm.inputs)


def _mm(name, a, b, mode, *, tm, tn, tk, outs, epi=None, extras=(), vecs=(), col_sums=0, norm_gain=None, comm=None):
    if mode == "tn":
        K, M = a.shape
    else:
        M, K = a.shape
    N = b.shape[0] if mode == "nt" else b.shape[1]
    tm, tn, tk = min(tm, M), min(tn, N), min(tk, K)
    assert M % tm == 0 and N % tn == 0 and K % tk == 0, (name, M, N, K, tm, tn, tk)
    nk = K // tk
    grid = (M // tm, N // tn, nk)

    if mode == "tn":
        a_spec = pl.BlockSpec((tk, tm), lambda i, j, k: (k, i))
    else:
        a_spec = pl.BlockSpec((tm, tk), lambda i, j, k: (i, k))
    if mode == "nt":
        b_spec = pl.BlockSpec((tn, tk), lambda i, j, k: (j, k))
    else:
        b_spec = pl.BlockSpec((tk, tn), lambda i, j, k: (k, j))
    tile = pl.BlockSpec((tm, tn), lambda i, j, k: (i, j))
    vec = pl.BlockSpec((1, tn), lambda i, j, k: (0, j))
    normed = norm_gain is not None
    assert not normed or (mode == "nn" and nk == 1 and tm % grid[1] == 0)
    assert col_sums == 0 or grid[1] == 1
    out_shapes = [jax.ShapeDtypeStruct((M, N), dtype) for dtype in outs]
    out_specs = [tile for _ in outs]
    extra_specs = [tile for _ in extras] + [vec for _ in vecs]
    if normed:
        part_rows = tm // grid[1]
        last_part = M // part_rows - 1
        a_spec = pl.BlockSpec((tm, K), lambda i, j, k: (0, 0))
        out_shapes.append(jax.ShapeDtypeStruct((M, K), BF16))
        out_specs.append(pl.BlockSpec((part_rows, K), lambda i, j, k: (i * grid[1] + j, 0)))
        extra_specs.append(pl.BlockSpec((1, K), lambda i, j, k: (0, 0)))
        extra_specs.append(pl.BlockSpec((part_rows, K),
                                        lambda i, j, k: (jnp.minimum((i + 1) * grid[1] + j, last_part), 0)))
    out_shapes += [jax.ShapeDtypeStruct((1, N), F32)] * col_sums
    out_specs += [vec] * col_sums
    n_extra, n_out = len(extra_specs), len(out_shapes)
    n_tiles = len(outs)
    n_cin = len(comm.inputs) if comm else 0
    n_cout = len(comm.out_shapes) if comm else 0
    dims = {"nn": NN, "nt": NT, "tn": TN}[mode]
    steps = grid[0] * grid[1] * grid[2]

    def body(*refs):
        a_ref, b_ref = refs[0], refs[1]
        pos = 2
        extra_refs = refs[pos:pos + n_extra]
        pos += n_extra
        comm_in = refs[pos:pos + n_cin]
        pos += n_cin
        out_refs = refs[pos:pos + n_out]
        pos += n_out
        comm_out = refs[pos:pos + n_cout]
        pos += n_cout
        acc_ref = refs[pos] if nk > 1 else None
        pos += nk > 1
        y_refs = refs[pos:pos + 2 * normed]
        comm_sems = refs[pos + 2 * normed:]
        k = pl.program_id(2)
        step = (pl.program_id(0) * grid[1] + pl.program_id(1)) * nk + k

        if comm is not None:
            @pl.when(step == 0)
            def _():
                comm.start(comm_in, comm_out, comm_sems)

        def finish(acc):
            epi_args = [e[...] for e in extra_refs[:n_extra - 2 * normed]]
            res = epi(acc, *epi_args) if epi is not None else (acc,) * n_tiles
            for o, r in zip(out_refs[:n_tiles], res[:n_tiles]):
                o[...] = r.astype(o.dtype)
            if col_sums:
                sums = out_refs[n_out - col_sums:]

                @pl.when(pl.program_id(0) == 0)
                def _():
                    for o in sums:
                        o[...] = jnp.zeros_like(o)

                for o, r in zip(sums, res[n_tiles:]):
                    o[...] += r

        if normed:
            gain_ref, ahead_ref = extra_refs[-2], extra_refs[-1]

            def norm(hv):
                return (hv * lax.rsqrt(_mean(hv * hv) + EPS) * gain_ref[...]).astype(BF16)

            @pl.when(step == 0)
            def _():
                y_refs[0][...] = norm(a_ref[...])

            part_at = pl.ds(pl.multiple_of(pl.program_id(1) * part_rows, part_rows), part_rows)
            for parity in range(2):
                @pl.when(pl.program_id(0) % 2 == parity)
                def _(y_now=y_refs[parity], y_next=y_refs[1 - parity]):
                    finish(_dot(y_now[...], b_ref[...].astype(BF16), dims))
                    out_refs[n_tiles][...] = y_now[part_at, :]
                    y_next[part_at, :] = norm(ahead_ref[...])
        elif nk == 1:
            finish(_dot(a_ref[...].astype(BF16), b_ref[...].astype(BF16), dims))
        else:
            part = _dot(a_ref[...].astype(BF16), b_ref[...].astype(BF16), dims)

            @pl.when(k == 0)
            def _():
                acc_ref[...] = part

            @pl.when(k > 0)
            def _():
                acc_ref[...] += part

            @pl.when(k == nk - 1)
            def _():
                finish(acc_ref[...])

        if comm is not None:
            if comm.mid is not None:
                @pl.when(step == (3 * steps) // 4)
                def _():
                    comm.mid(comm_in, comm_out, comm_sems)

            @pl.when(step == steps - 1)
            def _():
                comm.end(comm_in, comm_out, comm_sems)

    sequential = comm is not None or normed or col_sums > 0
    order = ("arbitrary",) * 3 if sequential else ("parallel", "parallel", "arbitrary")
    scratch = [pltpu.VMEM((tm, tn), F32)] if nk > 1 else []
    if normed:
        scratch += [pltpu.VMEM((tm, K), BF16)] * 2
    res = pl.pallas_call(
        body, name=name, grid=grid,
        in_specs=[a_spec, b_spec] + extra_specs + [ANY] * n_cin,
        out_specs=out_specs + [ANY] * n_cout,
        out_shape=out_shapes + (comm.out_shapes if comm else []),
        scratch_shapes=scratch + (comm.scratch if comm else []),
        compiler_params=_params(dimension_semantics=order),
    )(a, b, *extras, *vecs, *([norm_gain, a] if normed else []), *(comm.inputs if comm else []))
    mm_out = res[0] if n_out == 1 else list(res[:n_out])
    return (mm_out, list(res[n_out:])) if comm else mm_out


def _epi_residual(acc, res):
    return (res + acc,)


def _epi_relu2(acc):
    return (jnp.square(jnp.maximum(acc, 0.0)),)


def _epi_rms_bwd(copies):
    def epi(acc, h, dres, g):
        r = lax.rsqrt(_mean(h * h) + EPS)
        hn = h * r
        dyg = acc * g
        dh = dres + r * (dyg - hn * _mean(dyg * hn))
        return (dh,) * copies + (jnp.sum(acc * hn, axis=0, keepdims=True),)
    return epi


def _epi_loss_head(acc, res, target, g):
    h = res + acc
    r = lax.rsqrt(_mean(h * h) + EPS)
    hn = h * r
    diff = hn * g - target
    loss = 0.5 * jnp.sum(_mean(diff * diff))
    dy = diff * (1.0 / h.shape[-1])
    dyg = dy * g
    dh = r * (dyg - hn * _mean(dyg * hn))
    return dh, dh, jnp.sum(dy * hn, axis=0, keepdims=True), jnp.full((1, h.shape[-1]), loss, F32)


def _epi_relu2_grad(acc, q):
    qf = q.astype(F32)
    return (acc * jnp.where(qf > 0.0, (2.0 * qf) * lax.rsqrt(qf), 0.0),)


def _row_tile(T):
    return min(T, 512)


def _gate_tile(T):
    return min(T, 256)


def _gate_fwd(a, ln_g, ln_b, w_tril, b_rows):
    T, W2 = a.shape
    W = W2 // 2
    G = W // GROUP_DIM
    tr = _gate_tile(T)

    def body(a_ref, lng_ref, lnb_ref, w_ref, b_ref, z_ref):
        u = _gelu(a_ref[:, :W].astype(F32))
        vg = _gelu(a_ref[:, W:].astype(F32))
        xc = vg - _mean(vg)
        vn = xc * lax.rsqrt(_mean(xc * xc) + EPS)
        vl = (vn * lng_ref[...] + lnb_ref[...]).astype(BF16)
        for n in range(tr // CHUNK):
            rows = slice(n * CHUNK, (n + 1) * CHUNK)
            for g in range(G):
                cols = slice(g * GROUP_DIM, (g + 1) * GROUP_DIM)
                gate = _dot(w_ref[g], vl[rows, cols]) + b_ref[g]
                z_ref[rows, cols] = (u[rows, cols] * gate).astype(BF16)

    vec = pl.BlockSpec((1, W), lambda i: (0, 0))
    grp = pl.BlockSpec((G, CHUNK, CHUNK), lambda i: (0, 0, 0))
    return pl.pallas_call(
        body, name="gate_fwd", grid=(T // tr,),
        in_specs=[pl.BlockSpec((tr, W2), lambda i: (i, 0)), vec, vec, grp, grp],
        out_specs=pl.BlockSpec((tr, W), lambda i: (i, 0)),
        out_shape=jax.ShapeDtypeStruct((T, W), BF16),
        compiler_params=_params(dimension_semantics=("parallel",)),
    )(a, ln_g, ln_b, w_tril, b_rows)


def _gate_bwd(a, dz, ln_g, ln_b, w_tril, w_tril_t, b_rows):
    T, W2 = a.shape
    W = W2 // 2
    G = W // GROUP_DIM
    tr = _gate_tile(T)
    steps = T // tr

    def body(a_ref, dz_ref, lng_ref, lnb_ref, w_ref, wt_ref, b_ref, da_ref, dlng_ref, dlnb_ref, dw_ref, dbs_ref, dvl_ref):
        step = pl.program_id(0)

        @pl.when(step == 0)
        def _():
            dlng_ref[...] = jnp.zeros_like(dlng_ref)
            dlnb_ref[...] = jnp.zeros_like(dlnb_ref)
            dw_ref[...] = jnp.zeros_like(dw_ref)
            dbs_ref[...] = jnp.zeros_like(dbs_ref)

        au = a_ref[:, :W].astype(F32)
        av = a_ref[:, W:].astype(F32)
        u = _gelu(au)
        vg = _gelu(av)
        xc = vg - _mean(vg)
        rstd = lax.rsqrt(_mean(xc * xc) + EPS)
        vn = xc * rstd
        lng = lng_ref[...]
        vl = (vn * lng + lnb_ref[...]).astype(BF16)
        du_scale = dz_ref[...] * _gelu_grad(au)
        dgate_all = dz_ref[...] * u
        for n in range(tr // CHUNK):
            rows = slice(n * CHUNK, (n + 1) * CHUNK)
            for g in range(G):
                cols = slice(g * GROUP_DIM, (g + 1) * GROUP_DIM)
                vlg = vl[rows, cols]
                gate = _dot(w_ref[g], vlg) + b_ref[g]
                da_ref[rows, cols] = (du_scale[rows, cols] * gate).astype(BF16)
                dgate = dgate_all[rows, cols]
                dbs_ref[g] += dgate
                dgate_b = dgate.astype(BF16)
                dw_ref[g] += _dot(dgate_b, vlg, NT)
                dvl_ref[rows, cols] = _dot(wt_ref[g], dgate_b)
        dvl = dvl_ref[...]
        dlnb_ref[...] += jnp.sum(dvl, axis=0, keepdims=True)
        dlng_ref[...] += jnp.sum(dvl * vn, axis=0, keepdims=True)
        dvn = dvl * lng
        dvg = rstd * (dvn - _mean(dvn) - vn * _mean(dvn * vn))
        da_ref[:, W:] = (dvg * _gelu_grad(av)).astype(BF16)

        @pl.when(step == steps - 1)
        def _():
            t_idx = lax.broadcasted_iota(jnp.int32, (CHUNK, CHUNK), 0)
            s_idx = lax.broadcasted_iota(jnp.int32, (CHUNK, CHUNK), 1)
            for g in range(G):
                dw_ref[g] = jnp.where(s_idx <= t_idx, dw_ref[g], 0.0)
                dbs_ref[g] = jnp.broadcast_to(jnp.sum(dbs_ref[g], axis=-1, keepdims=True), (CHUNK, CHUNK))

    vec = pl.BlockSpec((1, W), lambda i: (0, 0))
    grp = pl.BlockSpec((G, CHUNK, CHUNK), lambda i: (0, 0, 0))
    return pl.pallas_call(
        body, name="gate_bwd", grid=(steps,),
        in_specs=[pl.BlockSpec((tr, W2), lambda i: (i, 0)), pl.BlockSpec((tr, W), lambda i: (i, 0)),
                  vec, vec, grp, grp, grp],
        out_specs=[pl.BlockSpec((tr, W2), lambda i: (i, 0)), vec, vec, grp, grp],
        out_shape=[jax.ShapeDtypeStruct((T, W2), BF16), jax.ShapeDtypeStruct((1, W), F32),
                   jax.ShapeDtypeStruct((1, W), F32), jax.ShapeDtypeStruct((G, CHUNK, CHUNK), F32),
                   jax.ShapeDtypeStruct((G, CHUNK, CHUNK), F32)],
        scratch_shapes=[pltpu.VMEM((tr, W), F32)],
        compiler_params=_params(dimension_semantics=("arbitrary",)),
    )(a, dz, ln_g, ln_b, w_tril, w_tril_t, b_rows)


def _bucket_map(dilation):
    rel = BLK + np.arange(BLK)[:, None] - np.arange(2 * BLK)[None, :]
    dist = np.clip(rel, 0, BLK) * dilation
    nf = np.maximum(dist, 1).astype(np.float32)
    large = MAX_EXACT + (np.log(nf / np.float32(MAX_EXACT)) / np.float32(math.log(REL_MAX_DISTANCE / MAX_EXACT))
                         * np.float32(N_BUCKETS - MAX_EXACT)).astype(np.int32)
    large = np.minimum(large, N_BUCKETS - 1)
    return np.where(dist < MAX_EXACT, dist, large).astype(np.int32)


def _bucket_maps():
    return jnp.asarray(np.stack([_bucket_map(d) for d in DILATIONS]))


def _bias_build(rel_bias, buckets):
    NG = len(DILATIONS)

    def body(table_ref, bucket_ref, out_ref):
        for g in range(NG):
            bk = bucket_ref[g]
            for h in range(ATT_HEADS):
                out_ref[0, g, h] = jnp.zeros((BLK, 2 * BLK), F32)
            for b in range(N_BUCKETS):
                hit = bk == b
                for h in range(ATT_HEADS):
                    out_ref[0, g, h] = jnp.where(hit, table_ref[b, g * ATT_HEADS + h], out_ref[0, g, h])
            for h in range(ATT_HEADS):
                for first in range(2):
                    out_ref[first, g, h] = jnp.where(_window_mask(first), out_ref[0, g, h], NEG_INF)

    return pl.pallas_call(
        body, name="bias_build",
        in_specs=[pl.BlockSpec(memory_space=pltpu.SMEM), pl.BlockSpec(memory_space=pltpu.VMEM)],
        out_specs=pl.BlockSpec(memory_space=pltpu.VMEM),
        out_shape=jax.ShapeDtypeStruct((2, NG, ATT_HEADS, BLK, 2 * BLK), F32),
        compiler_params=_params(),
    )(rel_bias, buckets)


def _bias_scatter(dbias, buckets):
    NG = len(DILATIONS)

    def body(dbias_ref, bucket_ref, out_ref):
        for g in range(NG):
            bk = bucket_ref[g]
            for b in range(N_BUCKETS):
                hit = bk == b
                for h in range(ATT_HEADS):
                    out_ref[b, g * ATT_HEADS + h] = jnp.sum(jnp.where(hit, dbias_ref[g, h], 0.0))

    return pl.pallas_call(
        body, name="bias_scatter",
        in_specs=[pl.BlockSpec(memory_space=pltpu.VMEM), pl.BlockSpec(memory_space=pltpu.VMEM)],
        out_specs=pl.BlockSpec(memory_space=pltpu.SMEM),
        out_shape=jax.ShapeDtypeStruct((N_BUCKETS, NG * ATT_HEADS), F32),
        compiler_params=_params(),
    )(dbias, buckets)


def _window_mask(first):
    qi = lax.broadcasted_iota(jnp.int32, (BLK, 2 * BLK), 0)
    kj = lax.broadcasted_iota(jnp.int32, (BLK, 2 * BLK), 1)
    rel = BLK + qi - kj
    return (rel >= 0) & (rel <= BLK) & (kj >= BLK * first)


def _head_lanes(hh):
    lane = lax.broadcasted_iota(jnp.int32, (1, PAIR), 1)
    return (lane >= hh * HEAD_DIM) & (lane < (hh + 1) * HEAD_DIM)


ATT_STEP_BLOCKS = 8


def _attn_steps(stride):
    per_step = math.gcd(stride, ATT_STEP_BLOCKS)
    return per_step, stride // per_step


def _attn_fwd(name, g, qkv, qc, kc, vc, bias, stride):
    T = qkv.shape[0]
    per_step, lag = _attn_steps(stride)
    chained = stride == 1
    if chained:
        per_step, lag = min(ATT_STEP_BLOCKS, T // BLK), 1
    rows = per_step * BLK
    scale = HEAD_DIM ** -0.5

    def body(q_ref, kp_ref, kc_ref, vp_ref, vc_ref, bias_ref, out_ref, *chain):
        step = pl.program_id(0)
        low = _head_lanes(0)
        if chained:
            for cat, before, now in zip(chain, (kp_ref, vp_ref), (kc_ref, vc_ref)):
                cat[:BLK, :] = before[...]
                cat[BLK:, :] = now[...]

        def block(j, carry):
            at = pl.ds(pl.multiple_of(j * BLK, BLK), BLK)
            if chained:
                first = ((step == 0) & (j == 0)).astype(jnp.int32)
                after = pl.ds(pl.multiple_of((j + 1) * BLK, BLK), BLK)
                keys = lambda cols: jnp.concatenate([chain[0][at, cols], chain[0][after, cols]], axis=0)
                values = lambda cols: jnp.concatenate([chain[1][at, cols], chain[1][after, cols]], axis=0)
            else:
                first = (step < lag).astype(jnp.int32)
                keys = lambda cols: jnp.concatenate([kp_ref[at, cols], kc_ref[at, cols]], axis=0)
                values = lambda cols: jnp.concatenate([vp_ref[at, cols], vc_ref[at, cols]], axis=0)
            for hp in range(ATT_HEADS // 2):
                cols = slice(hp * PAIR, (hp + 1) * PAIR)
                qp = q_ref[at, cols]
                kk = keys(cols)
                vv = values(cols)
                o_h, lse_h = [], []
                for hh in range(2):
                    qm = jnp.where(_head_lanes(hh), qp, jnp.zeros_like(qp))
                    s = _dot(qm, kk, NT) * scale
                    logits = s + bias_ref[first, 2 * hp + hh]
                    m = jnp.max(logits, axis=-1, keepdims=True)
                    p = jnp.exp(logits - m)
                    den = jnp.sum(p, axis=-1, keepdims=True)
                    o_h.append(_dot(p.astype(BF16), vv) / den)
                    lse_h.append(m + jnp.log(den))
                out_ref[at, cols] = jnp.where(low, o_h[0], o_h[1])
                out_ref[at, slice(ATT_WIDTH + hp * PAIR, ATT_WIDTH + (hp + 1) * PAIR)] = (
                    jnp.where(low, lse_h[0], lse_h[1]))
            return carry

        lax.fori_loop(0, per_step, block, 0)

    def cur(c):
        return pl.BlockSpec((rows, ATT_WIDTH), lambda s: (s, c))

    def prev(c):
        if chained:
            return pl.BlockSpec((BLK, ATT_WIDTH), lambda s: (jnp.maximum(s * per_step - 1, 0), c))
        return pl.BlockSpec((rows, ATT_WIDTH), lambda s: (jnp.maximum(s - lag, 0), c))

    return pl.pallas_call(
        body, name=name, grid=(T // rows,),
        in_specs=[cur(qc), prev(kc), cur(kc), prev(vc), cur(vc),
                  pl.BlockSpec((2, None, ATT_HEADS, BLK, 2 * BLK), lambda s: (0, g, 0, 0, 0))],
        out_specs=pl.BlockSpec((rows, 2 * ATT_WIDTH), lambda s: (s, 0)),
        out_shape=jax.ShapeDtypeStruct((T, 2 * ATT_WIDTH), F32),
        scratch_shapes=[pltpu.VMEM((rows + BLK, ATT_WIDTH), BF16)] * 2 if chained else [],
        compiler_params=_params(dimension_semantics=("parallel",)),
    )(qkv, qkv, qkv, qkv, qkv, bias)


def _permute_f32(p, x):
    hi = x.astype(BF16)
    rest = x - hi.astype(F32)
    mid = rest.astype(BF16)
    low = (rest - mid.astype(F32)).astype(BF16)
    return _dot(p, hi) + _dot(p, mid) + _dot(p, low)


def _attn_merge(parts):
    T = parts[0].shape[0]
    rows = min(T, REORDER_ROWS)
    n = len(parts)
    width = 2 * PAIR
    ncol = ATT_WIDTH // width

    def body(*refs):
        p_refs, o_refs, l_refs = refs[:n], refs[n:2 * n], refs[2 * n:3 * n]
        o_ref, lse_ref = refs[3 * n], refs[3 * n + 1]

        def positions(ref, g, start):
            d = DILATIONS[g]
            if d == 1:
                return ref[start:start + REORDER_TILE, :]
            span, per = BLK * d, REORDER_TILE // d
            base, t = start // span * span, start % span // REORDER_TILE
            chunks = [ref[base + r * BLK + t * per:base + r * BLK + (t + 1) * per, :] for r in range(d)]
            return _permute_f32(p_refs[g][...], jnp.concatenate(chunks, axis=0))

        for start in range(0, rows, REORDER_TILE):
            ls = [positions(l_refs[g], g, start) for g in range(n)]
            m = functools.reduce(jnp.maximum, ls)
            es = [jnp.exp(l - m) for l in ls]
            tot = functools.reduce(lambda x, y: x + y, es)
            acc = functools.reduce(lambda x, y: x + y, [e * positions(o_refs[g], g, start) for g, e in enumerate(es)])
            o_ref[start:start + REORDER_TILE, :] = (acc / tot).astype(BF16)
            lse_ref[start:start + REORDER_TILE, :] = m + jnp.log(tot)

    matrix = pl.BlockSpec((REORDER_TILE, REORDER_TILE), lambda w, c: (0, 0))
    col = pl.BlockSpec((rows, width), lambda w, c: (w, c))
    col_lse = pl.BlockSpec((rows, width), lambda w, c: (w, ncol + c))
    return pl.pallas_call(
        body, name="attn_merge", grid=(T // rows, ncol),
        in_specs=[matrix] * n + [col] * n + [col_lse] * n, out_specs=[col, col],
        out_shape=[jax.ShapeDtypeStruct((T, ATT_WIDTH), BF16), jax.ShapeDtypeStruct((T, ATT_WIDTH), F32)],
        compiler_params=_params(dimension_semantics=("parallel", "parallel")),
    )(*[_reorder_matrix(max(d, 2), True) for d in DILATIONS], *parts, *parts)


def _attn_bwd(name, g, qkv, qc, kc, vc, do, o, lse, bias, stride, comm=None):
    T = qkv.shape[0]
    per_step, lag = _attn_steps(stride)
    rows = per_step * BLK
    steps = T // rows
    scale = HEAD_DIM ** -0.5
    n_cin = len(comm.inputs) if comm else 0
    n_cout = len(comm.out_shapes) if comm else 0
    assert comm is None or comm.mid is None

    def body(*refs):
        q_ref, kp_ref, kc_ref, vp_ref, vc_ref, do_ref, o_ref, lse_ref, bias_ref = refs[:9]
        comm_in = refs[9:9 + n_cin]
        dq_ref, dkv_ref, db_ref = refs[9 + n_cin:12 + n_cin]
        comm_out = refs[12 + n_cin:12 + n_cin + n_cout]
        carry_k, carry_v = refs[12 + n_cin + n_cout:14 + n_cin + n_cout]
        comm_sems = refs[14 + n_cin + n_cout:]
        step = pl.program_id(0)

        if comm is not None:
            @pl.when(step == 0)
            def _():
                comm.start(comm_in, comm_out, comm_sems)

            @pl.when(step == steps + lag - 1)
            def _():
                comm.end(comm_in, comm_out, comm_sems)

        slot0 = (step % lag) * per_step

        @pl.when(step == 0)
        def _():
            db_ref[...] = jnp.zeros_like(db_ref)
            carry_k[...] = jnp.zeros_like(carry_k)
            carry_v[...] = jnp.zeros_like(carry_v)

        @pl.when(step >= steps)
        def _():
            def flush(j, carry):
                at = pl.ds(pl.multiple_of(j * BLK, BLK), BLK)
                dkv_ref[at, :ATT_WIDTH] = carry_k[slot0 + j].astype(BF16)
                dkv_ref[at, ATT_WIDTH:] = carry_v[slot0 + j].astype(BF16)
                return carry

            lax.fori_loop(0, per_step, flush, 0)

        @pl.when(step < steps)
        def _():
            first = (step < lag).astype(jnp.int32)

            def block(j, carry):
                at = pl.ds(pl.multiple_of(j * BLK, BLK), BLK)
                ck_ref = carry_k.at[slot0 + j]
                cv_ref = carry_v.at[slot0 + j]
                for hp in range(ATT_HEADS // 2):
                    cols = slice(hp * PAIR, (hp + 1) * PAIR)
                    qp = q_ref[at, cols]
                    kk = jnp.concatenate([kp_ref[at, cols], kc_ref[at, cols]], axis=0)
                    vv = jnp.concatenate([vp_ref[at, cols], vc_ref[at, cols]], axis=0)
                    dop = do_ref[at, cols]
                    lsep = lse_ref[at, cols]
                    prod = dop.astype(F32) * o_ref[at, cols].astype(F32)
                    dq = jnp.zeros((BLK, PAIR), F32)
                    dk = jnp.zeros((2 * BLK, PAIR), F32)
                    dv = jnp.zeros((2 * BLK, PAIR), F32)
                    for hh in range(2):
                        lanes = _head_lanes(hh)
                        qm = jnp.where(lanes, qp, jnp.zeros_like(qp))
                        dom = jnp.where(lanes, dop, jnp.zeros_like(dop))
                        km = jnp.where(lanes, kk, jnp.zeros_like(kk))
                        delta = jnp.sum(jnp.where(lanes, prod, 0.0), axis=-1, keepdims=True)
                        lse_h = jnp.max(jnp.where(lanes, lsep, NEG_INF), axis=-1, keepdims=True)
                        s = _dot(qm, kk, NT) * scale
                        logits = s + bias_ref[first, 2 * hp + hh]
                        p = jnp.exp(logits - lse_h)
                        dv += _dot(p.astype(BF16), dom, TN)
                        ds = p * (_dot(dom, vv, NT) - delta)
                        db_ref[2 * hp + hh] += ds
                        dss = (ds * scale).astype(BF16)
                        dq += _dot(dss, km)
                        dk += _dot(dss, qm, TN)
                    dq_ref[at, cols] = dq.astype(BF16)
                    dkv_ref[at, cols] = (ck_ref[:, cols] + dk[:BLK]).astype(BF16)
                    dkv_ref[at, slice(ATT_WIDTH + hp * PAIR, ATT_WIDTH + (hp + 1) * PAIR)] = (
                        cv_ref[:, cols] + dv[:BLK]).astype(BF16)
                    ck_ref[:, cols] = dk[BLK:]
                    cv_ref[:, cols] = dv[BLK:]
                return carry

            lax.fori_loop(0, per_step, block, 0)

    last = steps - 1

    def cur(c):
        return pl.BlockSpec((rows, ATT_WIDTH), lambda s: (jnp.minimum(s, last), c))

    def prev(c):
        return pl.BlockSpec((rows, ATT_WIDTH), lambda s: (jnp.clip(s - lag, 0, last), c))

    dbias_shape = (ATT_HEADS, BLK, 2 * BLK)
    res = pl.pallas_call(
        body, name=name, grid=(steps + lag,),
        in_specs=[cur(qc), prev(kc), cur(kc), prev(vc), cur(vc), cur(0), cur(0), cur(0),
                  pl.BlockSpec((2, None, ATT_HEADS, BLK, 2 * BLK), lambda s: (0, g, 0, 0, 0))] + [ANY] * n_cin,
        out_specs=[cur(0), pl.BlockSpec((rows, 2 * ATT_WIDTH), lambda s: (jnp.clip(s - lag, 0, last), 0)),
                   pl.BlockSpec(dbias_shape, lambda s: (0, 0, 0))] + [ANY] * n_cout,
        out_shape=[jax.ShapeDtypeStruct((T, ATT_WIDTH), BF16), jax.ShapeDtypeStruct((T, 2 * ATT_WIDTH), BF16),
                   jax.ShapeDtypeStruct(dbias_shape, F32)] + (comm.out_shapes if comm else []),
        scratch_shapes=[pltpu.VMEM((stride, BLK, ATT_WIDTH), F32), pltpu.VMEM((stride, BLK, ATT_WIDTH), F32)]
        + (comm.scratch if comm else []),
        compiler_params=_params(dimension_semantics=("arbitrary",)),
    )(qkv, qkv, qkv, qkv, qkv, do, o, lse, bias, *(comm.inputs if comm else []))
    return res[0], res[1], res[2], list(res[3:])


REORDER_TILE = 256
REORDER_ROWS = 2048


def _reorder_matrix(d, inverse):
    per = REORDER_TILE // d
    p = np.zeros((REORDER_TILE, REORDER_TILE), np.float32)
    for src in range(REORDER_TILE):
        i, r = divmod(src, d)
        p[r * per + i, src] = 1.0
    return jnp.asarray(p.T if inverse else p, dtype=BF16)


def _reorder_rows(name, src, d, inverse, *, src_col=0, col_stride=1, ncols=1, dst=None, dst_col=0, dst_stride=1,
                  dst_blocks=None):
    T = src.shape[0]
    dtype = src.dtype
    span = BLK * d
    rows = max(span, min(T, REORDER_ROWS))
    per = REORDER_TILE // d
    tiles = span // REORDER_TILE
    dst_blocks = ncols if dst_blocks is None else dst_blocks

    def apply(p, x):
        return _dot(p, x).astype(BF16) if dtype == BF16 else _permute_f32(p, x)

    def body(*refs):
        p_ref, x_ref, o_ref = refs[0], refs[1], refs[-1]
        if d == 1:
            o_ref[...] = x_ref[...]
            return
        for s in range(rows // span):
            for t in range(tiles):
                base = s * span
                tile_rows = slice(base + t * REORDER_TILE, base + (t + 1) * REORDER_TILE)
                chunk = lambda r: slice(base + r * BLK + t * per, base + r * BLK + (t + 1) * per)
                if inverse:
                    gathered = jnp.concatenate([x_ref[chunk(r), :] for r in range(d)], axis=0)
                    o_ref[tile_rows, :] = apply(p_ref[...], gathered)
                else:
                    y = apply(p_ref[...], x_ref[tile_rows, :])
                    for r in range(d):
                        o_ref[chunk(r), :] = y[r * per:(r + 1) * per]

    in_specs = [pl.BlockSpec((REORDER_TILE, REORDER_TILE), lambda w, k: (0, 0)),
                pl.BlockSpec((rows, ATT_WIDTH), lambda w, k: (w, src_col + col_stride * k))]
    operands = [_reorder_matrix(max(d, 2), inverse), src]
    aliases = {}
    if dst is not None:
        in_specs.append(ANY)
        operands.append(dst)
        aliases = {2: 0}
    return pl.pallas_call(
        body, name=name, grid=(T // rows, ncols), in_specs=in_specs,
        out_specs=pl.BlockSpec((rows, ATT_WIDTH), lambda w, k: (w, dst_col + dst_stride * k)),
        out_shape=jax.ShapeDtypeStruct((T, dst_blocks * ATT_WIDTH), dtype),
        input_output_aliases=aliases,
        compiler_params=_params(dimension_semantics=("parallel", "parallel")),
    )(*operands)


def _group_qkv(qkv, g, d):
    NG = len(DILATIONS)
    if d == 1:
        return qkv, (g, NG + g, 2 * NG + g)
    return _reorder_rows(f"qkv_to_residues{g}", qkv, d, False, src_col=g, col_stride=NG, ncols=3), (0, 1, 2)


def _attention_fwd(qkv, bias):
    T = qkv.shape[0]
    parts = []
    for g, d in enumerate(DILATIONS):
        src, (qc, kc, vc) = _group_qkv(qkv, g, d)
        parts.append(_attn_fwd(f"attn_fwd_{g}", g, src, qc, kc, vc, bias, d))
    return _attn_merge(parts)


def _attention_bwd(qkv, do, o, lse, bias, comms):
    NG = len(DILATIONS)
    dqkv, dbs, carried = None, [], []
    for g, d in enumerate(DILATIONS):
        src, (qc, kc, vc) = _group_qkv(qkv, g, d)
        do_g, o_g, lse_g = do, o, lse
        if d > 1:
            do_g = _reorder_rows(f"do_to_residues{g}", do, d, False)
            o_g = _reorder_rows(f"o_to_residues{g}", o, d, False)
            lse_g = _reorder_rows(f"lse_to_residues{g}", lse, d, False)
        dq, dkv, db, sent = _attn_bwd(f"attn_bwd_{g}", g, src, qc, kc, vc, do_g, o_g, lse_g, bias, d, comm=comms[g])
        dqkv = _reorder_rows(f"dq_to_positions{g}", dq, d, True, dst=dqkv, dst_col=g, dst_blocks=3 * NG)
        dqkv = _reorder_rows(f"dkv_to_positions{g}", dkv, d, True, ncols=2, dst=dqkv, dst_col=NG + g, dst_stride=NG,
                             dst_blocks=3 * NG)
        dbs.append(db)
        carried.append(sent)
    return dqkv, jnp.stack(dbs), carried


def _other_chips(x, y):
    return [(1 - x, y), (x, 1 - y), (1 - x, 1 - y)]


def _shard_region(ref, shape, by_cols, chip, rows=None):
    R, C = shape
    start, size = (0, R) if rows is None else rows
    if by_cols:
        return ref.at[pl.ds(start, size), pl.ds(chip * C, C)]
    return ref.at[pl.ds(chip * R + start, size), :]


def _gather_weights(entries):
    n = len(entries)
    shapes = [e[0].shape[1:] for e in entries]

    def places(ins, outs, sems):
        send_sems, recv_sems, local_sems = sems
        x, y, c = lax.axis_index("x"), lax.axis_index("y"), lax.axis_index("c")

        def landing(f, px, py, pc):
            R = shapes[f][0]
            return _shard_region(outs[f], shapes[f], entries[f][2], 2 * px + py, rows=(pc * (R // 2), R // 2))

        def copy(f, k, block, to, src=None):
            dst = landing(f, *block)
            return pltpu.make_async_remote_copy(
                src_ref=dst if src is None else src, dst_ref=dst,
                send_sem=send_sems.at[6 * f + k], recv_sem=recv_sems.at[6 * f + k],
                device_id=to, device_id_type=MESH)

        def mine(f):
            dst = _shard_region(outs[f], shapes[f], entries[f][2], 2 * x + y)
            return pltpu.make_async_copy(ins[f].at[entries[f][1]], dst, local_sems.at[f])

        def first(f, j):
            R = shapes[f][0]
            src = ins[f].at[entries[f][1], pl.ds(c * (R // 2), R // 2), :]
            return copy(f, j, (x, y, c), (*_other_chips(x, y)[j], c), src=src)

        return x, y, c, copy, mine, first

    def start(ins, outs, sems):
        _, _, _, _, mine, first = places(ins, outs, sems)
        for f in range(n):
            mine(f).start()
        for j in range(3):
            for f in range(n):
                first(f, j).start()

    def mid(ins, outs, sems):
        x, y, c, copy, _, _ = places(ins, outs, sems)
        for j, chip in enumerate(_other_chips(x, y)):
            for f in range(n):
                copy(f, j, (*chip, c), (x, y, c)).wait_recv()
                copy(f, 3 + j, (*chip, c), (x, y, 1 - c)).start()

    def end(ins, outs, sems):
        x, y, c, copy, mine, first = places(ins, outs, sems)
        for j, chip in enumerate(_other_chips(x, y)):
            for f in range(n):
                copy(f, 3 + j, (*chip, 1 - c), (x, y, c)).wait_recv()
        for j, chip in enumerate(_other_chips(x, y)):
            for f in range(n):
                first(f, j).wait_send()
                copy(f, 3 + j, (*chip, c), (x, y, 1 - c)).wait_send()
        for f in range(n):
            mine(f).wait()

    def whole(f):
        R, C = shapes[f]
        return (R, N_CHIPS * C) if entries[f][2] else (N_CHIPS * R, C)

    return _Comm(
        [e[0] for e in entries], [jax.ShapeDtypeStruct(whole(f), BF16) for f in range(n)],
        [pltpu.SemaphoreType.DMA((6 * n,)), pltpu.SemaphoreType.DMA((6 * n,)), pltpu.SemaphoreType.DMA((n,))],
        start, end, mid)


def _scatter_grads(entries):
    n = len(entries)

    def copies(ins, outs, sems):
        send_sems, recv_sems, local_sems = sems
        x, y, c = lax.axis_index("x"), lax.axis_index("y"), lax.axis_index("c")
        me = 2 * x + y

        def piece(f, chip):
            return _shard_region(ins[f], entries[f][1], entries[f][2], chip)

        mine = [pltpu.make_async_copy(piece(f, me), outs[f].at[me], local_sems.at[f]) for f in range(n)]
        sends = [pltpu.make_async_remote_copy(
            src_ref=piece(f, 2 * px + py), dst_ref=outs[f].at[me],
            send_sem=send_sems.at[3 * f + j], recv_sem=recv_sems.at[3 * f + j],
            device_id=(px, py, c), device_id_type=MESH)
            for j, (px, py) in enumerate(_other_chips(x, y)) for f in range(n)]
        return mine, sends

    def start(ins, outs, sems):
        mine, sends = copies(ins, outs, sems)
        for cp in mine + sends:
            cp.start()

    def end(ins, outs, sems):
        mine, sends = copies(ins, outs, sems)
        for cp in sends + mine:
            cp.wait()

    return _Comm(
        [e[0] for e in entries], [jax.ShapeDtypeStruct((N_CHIPS,) + tuple(e[1]), BF16) for e in entries],
        [pltpu.SemaphoreType.DMA((3 * n,)), pltpu.SemaphoreType.DMA((3 * n,)), pltpu.SemaphoreType.DMA((n,))],
        start, end)


def _exchange_sibling(parts):
    n = len(parts)

    def copies(ins, outs, sems):
        send_sems, recv_sems = sems
        sibling = (lax.axis_index("x"), lax.axis_index("y"), 1 - lax.axis_index("c"))
        return [pltpu.make_async_remote_copy(src_ref=ins[i], dst_ref=outs[i], send_sem=send_sems.at[i],
                                             recv_sem=recv_sems.at[i], device_id=sibling, device_id_type=MESH)
                for i in range(n)]

    def start(ins, outs, sems):
        for cp in copies(ins, outs, sems):
            cp.start()

    def end(ins, outs, sems):
        for cp in copies(ins, outs, sems):
            cp.wait()

    return _Comm(parts, [jax.ShapeDtypeStruct(s.shape, s.dtype) for s in parts],
                 [pltpu.SemaphoreType.DMA((n,)), pltpu.SemaphoreType.DMA((n,))], start, end)


def _allgather_small(block):
    m_per, ncol = block.shape

    def places(ins, outs, sems):
        send_sems, recv_sems, local_sem = sems
        x, y, c = lax.axis_index("x"), lax.axis_index("y"), lax.axis_index("c")

        def rows(px, py, pc):
            return outs[0].at[4 * px + 2 * py + pc]

        def copy(k, block_of, to, src=None):
            return pltpu.make_async_remote_copy(
                src_ref=rows(*block_of) if src is None else src, dst_ref=rows(*block_of),
                send_sem=send_sems.at[k], recv_sem=recv_sems.at[k], device_id=to, device_id_type=MESH)

        mine = pltpu.make_async_copy(ins[0], rows(x, y, c), local_sem.at[0])
        first = [copy(0, (x, y, c), (x, y, 1 - c), src=ins[0])]
        first += [copy(1 + j, (x, y, c), (*chip, c), src=ins[0]) for j, chip in enumerate(_other_chips(x, y))]
        passed = [copy(4 + j, (*chip, c), (x, y, 1 - c)) for j, chip in enumerate(_other_chips(x, y))]
        return x, y, c, copy, mine, first, passed

    def start(ins, outs, sems):
        _, _, _, _, mine, first, _ = places(ins, outs, sems)
        for cp in [mine] + first:
            cp.start()

    def mid(ins, outs, sems):
        x, y, c, copy, _, _, passed = places(ins, outs, sems)
        for j, chip in enumerate(_other_chips(x, y)):
            copy(1 + j, (*chip, c), (x, y, c)).wait_recv()
            passed[j].start()

    def end(ins, outs, sems):
        x, y, c, copy, mine, first, passed = places(ins, outs, sems)
        copy(0, (x, y, 1 - c), (x, y, c)).wait_recv()
        for j, chip in enumerate(_other_chips(x, y)):
            copy(4 + j, (*chip, 1 - c), (x, y, c)).wait_recv()
        for cp in first + passed:
            cp.wait_send()
        mine.wait()

    return _Comm([block], [jax.ShapeDtypeStruct((N_DEV, m_per, ncol), block.dtype)],
                 [pltpu.SemaphoreType.DMA((7,)), pltpu.SemaphoreType.DMA((7,)), pltpu.SemaphoreType.DMA((1,))],
                 start, end, mid)


def _join_comms(*progs):
    def split(parts, counts):
        out, pos = [], 0
        for n in counts:
            out.append(parts[pos:pos + n])
            pos += n
        return out

    def phase(which):
        def run(ins, outs, sems):
            args = zip(split(ins, [len(p.inputs) for p in progs]), split(outs, [len(p.out_shapes) for p in progs]),
                       split(sems, [len(p.scratch) for p in progs]))
            for p, (i, o, s) in zip(progs, args):
                fn = getattr(p, which)
                if fn is not None:
                    fn(i, o, s)
        return run

    return _Comm([a for p in progs for a in p.inputs], [s for p in progs for s in p.out_shapes],
                 [s for p in progs for s in p.scratch], phase("start"), phase("end"), phase("mid"))


def _adamw(w, g, m, v):
    m = ADAM_B1 * m + (1.0 - ADAM_B1) * g
    v = ADAM_B2 * v + (1.0 - ADAM_B2) * jnp.square(g)
    m_hat = m / (1.0 - ADAM_B1 ** ADAM_STEP)
    v_hat = v / (1.0 - ADAM_B2 ** ADAM_STEP)
    delta = -ADAM_LR * (m_hat / (jnp.sqrt(v_hat) + ADAM_EPS) + ADAM_WD * w)
    return delta, m, v


def _flat_tile(rows):
    return min(rows, 512)


def _sum_pieces(name, layers):
    L = len(layers)
    P, R, C = layers[0].shape
    tr = _flat_tile(R)

    def body(*refs):
        out_ref = refs[L]
        for l in range(L):
            @pl.when(pl.program_id(0) == l)
            def _(p_ref=refs[l]):
                acc = p_ref[0].astype(F32)
                for j in range(1, P):
                    acc = acc + p_ref[j].astype(F32)
                out_ref[...] = acc

    return pl.pallas_call(
        body, name=name, grid=(L, R // tr),
        in_specs=[pl.BlockSpec((P, tr, C), lambda l, i: (0, i, 0)) for _ in range(L)],
        out_specs=pl.BlockSpec((None, tr, C), lambda l, i: (l, i, 0)),
        out_shape=jax.ShapeDtypeStruct((L, R, C), F32),
        compiler_params=_params(dimension_semantics=("parallel", "parallel")),
    )(*layers)


def _adam_pair(name, w, m, v, part_a, part_b):
    L, R, C = w.shape
    tr = _flat_tile(R)

    def body(w_ref, m_ref, v_ref, a_ref, b_ref, g_ref, d_ref, nm_ref, nv_ref):
        g = a_ref[...] + b_ref[...]
        g_ref[...] = g
        d_ref[...], nm_ref[...], nv_ref[...] = _adamw(w_ref[...], g, m_ref[...], v_ref[...])

    row = pl.BlockSpec((None, tr, C), lambda l, i: (l, i, 0))
    return pl.pallas_call(
        body, name=name, grid=(L, R // tr),
        in_specs=[row] * 5, out_specs=[row] * 4,
        out_shape=[jax.ShapeDtypeStruct((L, R, C), F32)] * 4,
        compiler_params=_params(dimension_semantics=("parallel", "parallel")),
    )(w, m, v, part_a, part_b)


def _adam_small(w, m, v, gathered):
    R, C = w.shape

    def body(w_ref, m_ref, v_ref, p_ref, g_ref, d_ref, nm_ref, nv_ref):
        g = p_ref[0]
        for j in range(1, N_DEV):
            g = g + p_ref[j]
        g_ref[...] = g
        d_ref[...], nm_ref[...], nv_ref[...] = _adamw(w_ref[...], g, m_ref[...], v_ref[...])

    return pl.pallas_call(
        body, name="adam_small",
        out_shape=[jax.ShapeDtypeStruct((R, C), F32)] * 4,
        compiler_params=_params(),
    )(w, m, v, gathered)


SMALL = ("mix_norm_g", "mlp_norm_g", "final_norm_g", "a_ln_g", "a_ln_b", "a_w_s", "a_b_s", "rel_bias")


def _pack_small(arrays, width):
    rows = []
    for a in arrays:
        flat = a.reshape(-1)
        pad = (-flat.shape[0]) % width
        rows.append(jnp.pad(flat, (0, pad)).reshape(-1, width))
    block = jnp.concatenate(rows, axis=0)
    return jnp.pad(block, ((0, (-block.shape[0]) % 8), (0, 0)))


def _unpack_small(block, shapes, width):
    out, row = [], 0
    for shape in shapes:
        size = int(np.prod(shape))
        nrows = -(-size // width)
        out.append(block[row:row + nrows].reshape(-1)[:size].reshape(shape))
        row += nrows
    return out


def kernel(x, mix_norm_g, mlp_norm_g, final_norm_g, a_w_in, a_ln_g, a_ln_b, a_w_s, a_b_s, a_w_out, b_w_qkv, b_w_out, rel_bias, w_up, w_down, loss_target, m_mix_norm_g, m_mlp_norm_g, m_final_norm_g, m_a_w_in, m_a_ln_g, m_a_ln_b, m_a_w_s, m_a_b_s, m_a_w_out, m_b_w_qkv, m_b_w_out, m_rel_bias, m_w_up, m_w_down, v_mix_norm_g, v_mlp_norm_g, v_final_norm_g, v_a_w_in, v_a_ln_g, v_a_ln_b, v_a_w_s, v_a_b_s, v_a_w_out, v_b_w_qkv, v_b_w_out, v_rel_bias, v_w_up, v_w_down):
    T, D = x.shape[1], x.shape[2]
    h0 = x.reshape(T, D)
    target = loss_target.reshape(T, D)
    G = a_w_s.shape[1]

    w_big = [a_w_in, a_w_out, b_w_qkv, b_w_out, w_up, w_down]
    m_big = [m_a_w_in, m_a_w_out, m_b_w_qkv, m_b_w_out, m_w_up, m_w_down]
    v_big = [v_a_w_in, v_a_w_out, v_b_w_qkv, v_b_w_out, v_w_up, v_w_down]
    by_cols = [True, False, True, True, True, False]
    s_in, s_out, s_qkv, s_bo, s_up, s_dn = [w.astype(BF16) for w in w_big]
    W_in, W_out = _run_comm("gather_a", _gather_weights([(s_in, 0, True), (s_out, 0, False)]))

    tril = jnp.tril(jnp.ones((CHUNK, CHUNK), dtype=bool))
    w_tril = jnp.where(tril[None], a_w_s[0], 0.0).astype(BF16)
    w_tril_t = jnp.swapaxes(w_tril, 1, 2)
    b_rows = jnp.broadcast_to(a_b_s[0][:, :, None], (G, CHUNK, CHUNK))
    buckets = _bucket_maps()
    bias = _bias_build(rel_bias, buckets)

    QKV = s_qkv.shape[2] * N_CHIPS
    TM = 1024
    TK_WGRAD = 4096

    def matmul(name, a, b, mode, out, tm=TM, tn=1024, **kw):
        outs = out if isinstance(out, list) else [out]
        return _mm(name, a, b, mode, tm=tm, tn=tn, tk=a.shape[1], outs=outs, **kw)

    def norm_bwd(layer_gain, h, dres, copies=2):
        return dict(epi=_epi_rms_bwd(copies), extras=(h, dres), vecs=(layer_gain,), col_sums=1)

    def wgrad(name, a, b, tn=1024, tk=TK_WGRAD, comm=None):
        return _mm(name, a, b, "tn", tm=1024, tn=tn, tk=tk, outs=[BF16], comm=comm)

    def scatter(*which):
        return _scatter_grads([(g, w_big[i].shape[1:], by_cols[i]) for g, i in which])

    (a_pre, y0), (W_up0,) = matmul("a_in", h0, W_in, "nn", BF16, norm_gain=mix_norm_g[0:1],
                                   comm=_gather_weights([(s_up, 0, True)]))
    z = _gate_fwd(a_pre, a_ln_g, a_ln_b, w_tril, b_rows)
    h1 = matmul("a_out", z, W_out, "nn", F32, epi=_epi_residual, extras=(h0,))
    (q1, y1), (W_dn0,) = matmul("mlp_up0", h1, W_up0, "nn", BF16, epi=_epi_relu2, norm_gain=mlp_norm_g[0:1],
                                comm=_gather_weights([(s_dn, 0, False)]))
    h2, (W_qkv, W_bo) = matmul("mlp_down0", q1, W_dn0, "nn", F32, tm=TM // 2, epi=_epi_residual, extras=(h1,),
                               comm=_gather_weights([(s_qkv, 0, True), (s_bo, 0, True)]))
    (qkv, y2), (W_up1,) = matmul("b_qkv", h2, W_qkv, "nn", BF16, tn=QKV // 4, norm_gain=mix_norm_g[1:2],
                                 comm=_gather_weights([(s_up, 1, True)]))
    o, lse = _attention_fwd(qkv, bias)
    h3 = matmul("b_out", o, W_bo, "nn", F32, epi=_epi_residual, extras=(h2,))
    (q3, y3), (W_dn1,) = matmul("mlp_up1", h3, W_up1, "nn", BF16, epi=_epi_relu2, norm_gain=mlp_norm_g[1:2],
                                comm=_gather_weights([(s_dn, 1, False)]))
    dh4, dh4_b, d_final_g, loss_row = matmul("mlp_down1", q3, W_dn1, "nn", [F32, BF16], tm=TM // 2, epi=_epi_loss_head,
                                             extras=(h3, target), vecs=(final_norm_g.reshape(1, D),), col_sums=2)

    dp3 = matmul("mlp_down_bwd1", dh4_b, W_dn1, "nt", BF16, epi=_epi_relu2_grad, extras=(q3,))
    g_dn1 = wgrad("mlp_down_wgrad1", q3, dh4_b)
    g_up1 = wgrad("mlp_up_wgrad1", y3, dp3)
    dh3, dh3_b, dg_mlp1 = matmul("mlp_up_bwd1", dp3, W_up1, "nt", [F32, BF16], tm=TM // 2,
                                 **norm_bwd(mlp_norm_g[1:2], h3, dh4))
    do = matmul("b_out_bwd", dh3_b, W_bo, "nt", BF16)
    g_bo = wgrad("b_out_wgrad", o, dh3_b)
    dqkv, dbias, ((r_dn1,), (r_up1,), (r_bo,)) = _attention_bwd(
        qkv, do, o, lse, bias, [scatter((g_dn1, 5)), scatter((g_up1, 4)), scatter((g_bo, 3))])
    d_rel_bias = _bias_scatter(dbias, buckets)
    dh2, dh2_b, dg_mix1 = matmul("b_qkv_bwd", dqkv, W_qkv, "nt", [F32, BF16], tm=TM // 2,
                                 **norm_bwd(mix_norm_g[1:2], h2, dh3))
    g_qkv = wgrad("b_qkv_wgrad", y2, dqkv, tn=QKV // 3, tk=TK_WGRAD // 2)
    dp1, (r_qkv,) = matmul("mlp_down_bwd0", dh2_b, W_dn0, "nt", BF16, epi=_epi_relu2_grad, extras=(q1,),
                           comm=scatter((g_qkv, 2)))
    g_up0 = wgrad("mlp_up_wgrad0", y1, dp1)
    g_dn0, (r_up0,) = wgrad("mlp_down_wgrad0", q1, dh2_b, comm=scatter((g_up0, 4)))
    (dh1, dh1_b, dg_mlp0), (r_dn0,) = matmul("mlp_up_bwd0", dp1, W_up0, "nt", [F32, BF16], tm=TM // 2,
                                             comm=scatter((g_dn0, 5)), **norm_bwd(mlp_norm_g[0:1], h1, dh2))
    dz = matmul("a_out_bwd", dh1_b, W_out, "nt", F32)
    g_out = wgrad("a_out_wgrad", z, dh1_b)
    da, d_ln_g, d_ln_b, d_w_s, d_b_s = _gate_bwd(a_pre, dz, a_ln_g, a_ln_b, w_tril, w_tril_t, b_rows)
    g_in, (r_out,) = wgrad("a_in_wgrad", y0, da, comm=scatter((g_out, 1)))
    grad_x, dg_mix0 = matmul("a_in_bwd", da, W_in, "nt", F32, **norm_bwd(mix_norm_g[0:1], h0, dh1, copies=1))

    unused = jnp.zeros((1, 1), F32)
    small_w = [mix_norm_g, mlp_norm_g, final_norm_g, a_ln_g, a_ln_b, a_w_s, a_b_s, rel_bias, unused]
    small_m = [m_mix_norm_g, m_mlp_norm_g, m_final_norm_g, m_a_ln_g, m_a_ln_b, m_a_w_s, m_a_b_s, m_rel_bias, unused]
    small_v = [v_mix_norm_g, v_mlp_norm_g, v_final_norm_g, v_a_ln_g, v_a_ln_b, v_a_w_s, v_a_b_s, v_rel_bias, unused]
    small_g = [jnp.concatenate([dg_mix0, dg_mix1]), jnp.concatenate([dg_mlp0, dg_mlp1]), d_final_g,
               d_ln_g, d_ln_b, d_w_s[None], d_b_s[None, :, :, 0], d_rel_bias, loss_row[:, :1]]
    width = max(D, 128)
    received = [None, [r_out], [r_qkv], [r_bo], [r_up0, r_up1], [r_dn0, r_dn1]]
    plane = [None] + [_sum_pieces(f"sum_pieces{i}", received[i]) for i in range(1, len(w_big))]
    tail = _run_comm("tail_comm", _join_comms(scatter((g_in, 0)), _exchange_sibling(plane[1:]),
                                              _allgather_small(_pack_small(small_g, width))))
    r_in, other, gathered_small = tail[0], [None] + list(tail[1:len(w_big)]), tail[len(w_big)]
    plane[0] = _sum_pieces("sum_pieces0", [r_in])
    (other[0],) = _run_comm("exchange_a_in", _exchange_sibling([plane[0]]))
    big_out = [_adam_pair(f"adam{i}", w_big[i], m_big[i], v_big[i], plane[i], other[i]) for i in range(len(w_big))]

    def unbig(kind):
        return dict(zip(["a_w_in", "a_w_out", "b_w_qkv", "b_w_out", "w_up", "w_down"], [b[kind] for b in big_out]))

    small_out = _adam_small(_pack_small(small_w, width), _pack_small(small_m, width), _pack_small(small_v, width),
                            gathered_small)
    shapes = [w.shape for w in small_w]
    loss = _unpack_small(small_out[0], shapes, width)[-1][0, 0]

    names = ["mix_norm_g", "mlp_norm_g", "final_norm_g", "a_w_in", "a_ln_g", "a_ln_b", "a_w_s", "a_b_s", "a_w_out",
             "b_w_qkv", "b_w_out", "rel_bias", "w_up", "w_down"]
    results = [loss, grad_x.reshape(x.shape)]
    for kind in range(4):
        table = dict(zip(SMALL, _unpack_small(small_out[kind], shapes, width)))
        table.update(unbig(kind))
        results += [table[n] for n in names]
    return tuple(results)
```

```python
import functools
import math

import numpy as np
import jax
import jax.numpy as jnp
from jax import lax
from jax.experimental import pallas as pl
from jax.experimental.pallas import tpu as pltpu

F32 = jnp.float32
BF16 = jnp.bfloat16
MESH = pl.DeviceIdType.MESH
ANY = pl.BlockSpec(memory_space=pl.ANY)

N_CHIPS = 4
N_DEV = 8
VMEM_LIMIT_BYTES = 56 * 1024 * 1024

EPS = 1e-6
NEG_INF = -1e30
CHUNK = 128
GROUP_DIM = 128
HEAD_DIM = 64
ATT_HEADS = 8
ATT_WIDTH = ATT_HEADS * HEAD_DIM
PAIR = 2 * HEAD_DIM
BLK = 128
DILATIONS = (1, 4, 16)
N_BUCKETS = 32
MAX_EXACT = N_BUCKETS // 2
REL_MAX_DISTANCE = 2048

ADAM_LR = 0.001
ADAM_B1 = 0.9
ADAM_B2 = 0.999
ADAM_EPS = 1e-08
ADAM_WD = 0.01
ADAM_STEP = 10

NN = (((1,), (0,)), ((), ()))
NT = (((1,), (1,)), ((), ()))
TN = (((0,), (0,)), ((), ()))


def _params(**kw):
    return pltpu.CompilerParams(vmem_limit_bytes=VMEM_LIMIT_BYTES, **kw)


def _dot(a, b, dims=NN):
    return lax.dot_general(a, b, dims, preferred_element_type=F32)


def _gelu(x):
    return 0.5 * x * (1.0 + lax.erf(x * math.sqrt(0.5)))


def _gelu_grad(x):
    return 0.5 * (1.0 + lax.erf(x * math.sqrt(0.5))) + x * jnp.exp(-0.5 * x * x) * (1.0 / math.sqrt(2.0 * math.pi))


def _mean(x):
    return jnp.mean(x, axis=-1, keepdims=True)


class _Comm:
    def __init__(self, inputs, out_shapes, scratch, start, end, mid=None):
        self.inputs, self.out_shapes, self.scratch = list(inputs), list(out_shapes), list(scratch)
        self.start, self.mid, self.end = start, mid, end


def _run_comm(name, comm):
    n_in, n_out = len(comm.inputs), len(comm.out_shapes)

    def body(*refs):
        parts = refs[:n_in], refs[n_in:n_in + n_out], refs[n_in + n_out:]
        comm.start(*parts)
        if comm.mid is not None:
            comm.mid(*parts)
        comm.end(*parts)

    return pl.pallas_call(
        body, name=name, in_specs=[ANY] * n_in, out_specs=[ANY] * n_out, out_shape=comm.out_shapes,
        scratch_shapes=comm.scratch, compiler_params=_params(),
    )(*comm.inputs)


def _mm(name, a, b, mode, *, tm, tn, tk, outs, epi=None, extras=(), vecs=(), col_sums=0, norm_gain=None, comm=None):
    if mode == "tn":
        K, M = a.shape
    else:
        M, K = a.shape
    N = b.shape[0] if mode == "nt" else b.shape[1]
    tm, tn, tk = min(tm, M), min(tn, N), min(tk, K)
    assert M % tm == 0 and N % tn == 0 and K % tk == 0, (name, M, N, K, tm, tn, tk)
    nk = K // tk
    grid = (M // tm, N // tn, nk)

    if mode == "tn":
        a_spec = pl.BlockSpec((tk, tm), lambda i, j, k: (k, i))
    else:
        a_spec = pl.BlockSpec((tm, tk), lambda i, j, k: (i, k))
    if mode == "nt":
        b_spec = pl.BlockSpec((tn, tk), lambda i, j, k: (j, k))
    else:
        b_spec = pl.BlockSpec((tk, tn), lambda i, j, k: (k, j))
    tile = pl.BlockSpec((tm, tn), lambda i, j, k: (i, j))
    vec = pl.BlockSpec((1, tn), lambda i, j, k: (0, j))
    normed = norm_gain is not None
    assert not normed or (mode == "nn" and nk == 1 and tm % grid[1] == 0)
    assert col_sums == 0 or grid[1] == 1
    out_shapes = [jax.ShapeDtypeStruct((M, N), dtype) for dtype in outs]
    out_specs = [tile for _ in outs]
    extra_specs = [tile for _ in extras] + [vec for _ in vecs]
    if normed:
        part_rows = tm // grid[1]
        last_part = M // part_rows - 1
        a_spec = pl.BlockSpec((tm, K), lambda i, j, k: (0, 0))
        out_shapes.append(jax.ShapeDtypeStruct((M, K), BF16))
        out_specs.append(pl.BlockSpec((part_rows, K), lambda i, j, k: (i * grid[1] + j, 0)))
        extra_specs.append(pl.BlockSpec((1, K), lambda i, j, k: (0, 0)))
        extra_specs.append(pl.BlockSpec((part_rows, K),
                                        lambda i, j, k: (jnp.minimum((i + 1) * grid[1] + j, last_part), 0)))
    out_shapes += [jax.ShapeDtypeStruct((1, N), F32)] * col_sums
    out_specs += [vec] * col_sums
    n_extra, n_out = len(extra_specs), len(out_shapes)
    n_tiles = len(outs)
    n_cin = len(comm.inputs) if comm else 0
    n_cout = len(comm.out_shapes) if comm else 0
    dims = {"nn": NN, "nt": NT, "tn": TN}[mode]
    steps = grid[0] * grid[1] * grid[2]

    def body(*refs):
        a_ref, b_ref = refs[0], refs[1]
        pos = 2
        extra_refs = refs[pos:pos + n_extra]
        pos += n_extra
        comm_in = refs[pos:pos + n_cin]
        pos += n_cin
        out_refs = refs[pos:pos + n_out]
        pos += n_out
        comm_out = refs[pos:pos + n_cout]
        pos += n_cout
        acc_ref = refs[pos] if nk > 1 else None
        pos += nk > 1
        y_refs = refs[pos:pos + 2 * normed]
        comm_sems = refs[pos + 2 * normed:]
        k = pl.program_id(2)
        step = (pl.program_id(0) * grid[1] + pl.program_id(1)) * nk + k

        if comm is not None:
            @pl.when(step == 0)
            def _():
                comm.start(comm_in, comm_out, comm_sems)

        def finish(acc):
            epi_args = [e[...] for e in extra_refs[:n_extra - 2 * normed]]
            res = epi(acc, *epi_args) if epi is not None else (acc,) * n_tiles
            for o, r in zip(out_refs[:n_tiles], res[:n_tiles]):
                o[...] = r.astype(o.dtype)
            if col_sums:
                sums = out_refs[n_out - col_sums:]

                @pl.when(pl.program_id(0) == 0)
                def _():
                    for o in sums:
                        o[...] = jnp.zeros_like(o)

                for o, r in zip(sums, res[n_tiles:]):
                    o[...] += r

        if normed:
            gain_ref, ahead_ref = extra_refs[-2], extra_refs[-1]

            def norm(hv):
                return (hv * lax.rsqrt(_mean(hv * hv) + EPS) * gain_ref[...]).astype(BF16)

            @pl.when(step == 0)
            def _():
                y_refs[0][...] = norm(a_ref[...])

            part_at = pl.ds(pl.multiple_of(pl.program_id(1) * part_rows, part_rows), part_rows)
            for parity in range(2):
                @pl.when(pl.program_id(0) % 2 == parity)
                def _(y_now=y_refs[parity], y_next=y_refs[1 - parity]):
                    finish(_dot(y_now[...], b_ref[...].astype(BF16), dims))
                    out_refs[n_tiles][...] = y_now[part_at, :]
                    y_next[part_at, :] = norm(ahead_ref[...])
        elif nk == 1:
            finish(_dot(a_ref[...].astype(BF16), b_ref[...].astype(BF16), dims))
        else:
            part = _dot(a_ref[...].astype(BF16), b_ref[...].astype(BF16), dims)

            @pl.when(k == 0)
            def _():
                acc_ref[...] = part

            @pl.when(k > 0)
            def _():
                acc_ref[...] += part

            @pl.when(k == nk - 1)
            def _():
                finish(acc_ref[...])

        if comm is not None:
            if comm.mid is not None:
                @pl.when(step == (3 * steps) // 4)
                def _():
                    comm.mid(comm_in, comm_out, comm_sems)

            @pl.when(step == steps - 1)
            def _():
                comm.end(comm_in, comm_out, comm_sems)

    sequential = comm is not None or normed or col_sums > 0
    order = ("arbitrary",) * 3 if sequential else ("parallel", "parallel", "arbitrary")
    scratch = [pltpu.VMEM((tm, tn), F32)] if nk > 1 else []
    if normed:
        scratch += [pltpu.VMEM((tm, K), BF16)] * 2
    res = pl.pallas_call(
        body, name=name, grid=grid,
        in_specs=[a_spec, b_spec] + extra_specs + [ANY] * n_cin,
        out_specs=out_specs + [ANY] * n_cout,
        out_shape=out_shapes + (comm.out_shapes if comm else []),
        scratch_shapes=scratch + (comm.scratch if comm else []),
        compiler_params=_params(dimension_semantics=order),
    )(a, b, *extras, *vecs, *([norm_gain, a] if normed else []), *(comm.inputs if comm else []))
    mm_out = res[0] if n_out == 1 else list(res[:n_out])
    return (mm_out, list(res[n_out:])) if comm else mm_out


def _epi_residual(acc, res):
    return (res + acc,)


def _epi_relu2(acc):
    return (jnp.square(jnp.maximum(acc, 0.0)),)


def _epi_rms_bwd(copies):
    def epi(acc, h, dres, g):
        r = lax.rsqrt(_mean(h * h) + EPS)
        hn = h * r
        dyg = acc * g
        dh = dres + r * (dyg - hn * _mean(dyg * hn))
        return (dh,) * copies + (jnp.sum(acc * hn, axis=0, keepdims=True),)
    return epi


def _epi_loss_head(acc, res, target, g):
    h = res + acc
    r = lax.rsqrt(_mean(h * h) + EPS)
    hn = h * r
    diff = hn * g - target
    loss = 0.5 * jnp.sum(_mean(diff * diff))
    dy = diff * (1.0 / h.shape[-1])
    dyg = dy * g
    dh = r * (dyg - hn * _mean(dyg * hn))
    return dh, dh, jnp.sum(dy * hn, axis=0, keepdims=True), jnp.full((1, h.shape[-1]), loss, F32)


def _epi_relu2_grad(acc, q):
    qf = q.astype(F32)
    return (acc * jnp.where(qf > 0.0, (2.0 * qf) * lax.rsqrt(qf), 0.0),)


def _row_tile(T):
    return min(T, 512)


def _gate_tile(T):
    return min(T, 256)


def _gate_fwd(a, ln_g, ln_b, w_tril, b_rows):
    T, W2 = a.shape
    W = W2 // 2
    G = W // GROUP_DIM
    tr = _gate_tile(T)

    def body(a_ref, lng_ref, lnb_ref, w_ref, b_ref, z_ref):
        u = _gelu(a_ref[:, :W].astype(F32))
        vg = _gelu(a_ref[:, W:].astype(F32))
        xc = vg - _mean(vg)
        vn = xc * lax.rsqrt(_mean(xc * xc) + EPS)
        vl = (vn * lng_ref[...] + lnb_ref[...]).astype(BF16)
        for n in range(tr // CHUNK):
            rows = slice(n * CHUNK, (n + 1) * CHUNK)
            for g in range(G):
                cols = slice(g * GROUP_DIM, (g + 1) * GROUP_DIM)
                gate = _dot(w_ref[g], vl[rows, cols]) + b_ref[g]
                z_ref[rows, cols] = (u[rows, cols] * gate).astype(BF16)

    vec = pl.BlockSpec((1, W), lambda i: (0, 0))
    grp = pl.BlockSpec((G, CHUNK, CHUNK), lambda i: (0, 0, 0))
    return pl.pallas_call(
        body, name="gate_fwd", grid=(T // tr,),
        in_specs=[pl.BlockSpec((tr, W2), lambda i: (i, 0)), vec, vec, grp, grp],
        out_specs=pl.BlockSpec((tr, W), lambda i: (i, 0)),
        out_shape=jax.ShapeDtypeStruct((T, W), BF16),
        compiler_params=_params(dimension_semantics=("parallel",)),
    )(a, ln_g, ln_b, w_tril, b_rows)


def _gate_bwd(a, dz, ln_g, ln_b, w_tril, w_tril_t, b_rows):
    T, W2 = a.shape
    W = W2 // 2
    G = W // GROUP_DIM
    tr = _gate_tile(T)
    steps = T // tr

    def body(a_ref, dz_ref, lng_ref, lnb_ref, w_ref, wt_ref, b_ref, da_ref, dlng_ref, dlnb_ref, dw_ref, dbs_ref, dvl_ref):
        step = pl.program_id(0)

        @pl.when(step == 0)
        def _():
            dlng_ref[...] = jnp.zeros_like(dlng_ref)
            dlnb_ref[...] = jnp.zeros_like(dlnb_ref)
            dw_ref[...] = jnp.zeros_like(dw_ref)
            dbs_ref[...] = jnp.zeros_like(dbs_ref)

        au = a_ref[:, :W].astype(F32)
        av = a_ref[:, W:].astype(F32)
        u = _gelu(au)
        vg = _gelu(av)
        xc = vg - _mean(vg)
        rstd = lax.rsqrt(_mean(xc * xc) + EPS)
        vn = xc * rstd
        lng = lng_ref[...]
        vl = (vn * lng + lnb_ref[...]).astype(BF16)
        du_scale = dz_ref[...] * _gelu_grad(au)
        dgate_all = dz_ref[...] * u
        for n in range(tr // CHUNK):
            rows = slice(n * CHUNK, (n + 1) * CHUNK)
            for g in range(G):
                cols = slice(g * GROUP_DIM, (g + 1) * GROUP_DIM)
                vlg = vl[rows, cols]
                gate = _dot(w_ref[g], vlg) + b_ref[g]
                da_ref[rows, cols] = (du_scale[rows, cols] * gate).astype(BF16)
                dgate = dgate_all[rows, cols]
                dbs_ref[g] += dgate
                dgate_b = dgate.astype(BF16)
                dw_ref[g] += _dot(dgate_b, vlg, NT)
                dvl_ref[rows, cols] = _dot(wt_ref[g], dgate_b)
        dvl = dvl_ref[...]
        dlnb_ref[...] += jnp.sum(dvl, axis=0, keepdims=True)
        dlng_ref[...] += jnp.sum(dvl * vn, axis=0, keepdims=True)
        dvn = dvl * lng
        dvg = rstd * (dvn - _mean(dvn) - vn * _mean(dvn * vn))
        da_ref[:, W:] = (dvg * _gelu_grad(av)).astype(BF16)

        @pl.when(step == steps - 1)
        def _():
            t_idx = lax.broadcasted_iota(jnp.int32, (CHUNK, CHUNK), 0)
            s_idx = lax.broadcasted_iota(jnp.int32, (CHUNK, CHUNK), 1)
            for g in range(G):
                dw_ref[g] = jnp.where(s_idx <= t_idx, dw_ref[g], 0.0)
                dbs_ref[g] = jnp.broadcast_to(jnp.sum(dbs_ref[g], axis=-1, keepdims=True), (CHUNK, CHUNK))

    vec = pl.BlockSpec((1, W), lambda i: (0, 0))
    grp = pl.BlockSpec((G, CHUNK, CHUNK), lambda i: (0, 0, 0))
    return pl.pallas_call(
        body, name="gate_bwd", grid=(steps,),
        in_specs=[pl.BlockSpec((tr, W2), lambda i: (i, 0)), pl.BlockSpec((tr, W), lambda i: (i, 0)),
                  vec, vec, grp, grp, grp],
        out_specs=[pl.BlockSpec((tr, W2), lambda i: (i, 0)), vec, vec, grp, grp],
        out_shape=[jax.ShapeDtypeStruct((T, W2), BF16), jax.ShapeDtypeStruct((1, W), F32),
                   jax.ShapeDtypeStruct((1, W), F32), jax.ShapeDtypeStruct((G, CHUNK, CHUNK), F32),
                   jax.ShapeDtypeStruct((G, CHUNK, CHUNK), F32)],
        scratch_shapes=[pltpu.VMEM((tr, W), F32)],
        compiler_params=_params(dimension_semantics=("arbitrary",)),
    )(a, dz, ln_g, ln_b, w_tril, w_tril_t, b_rows)


def _bucket_map(dilation):
    rel = BLK + np.arange(BLK)[:, None] - np.arange(2 * BLK)[None, :]
    dist = np.clip(rel, 0, BLK) * dilation
    nf = np.maximum(dist, 1).astype(np.float32)
    large = MAX_EXACT + (np.log(nf / np.float32(MAX_EXACT)) / np.float32(math.log(REL_MAX_DISTANCE / MAX_EXACT))
                         * np.float32(N_BUCKETS - MAX_EXACT)).astype(np.int32)
    large = np.minimum(large, N_BUCKETS - 1)
    return np.where(dist < MAX_EXACT, dist, large).astype(np.int32)


def _bucket_maps():
    return jnp.asarray(np.stack([_bucket_map(d) for d in DILATIONS]))


def _bias_build(rel_bias, buckets):
    NG = len(DILATIONS)

    def body(table_ref, bucket_ref, out_ref):
        for g in range(NG):
            bk = bucket_ref[g]
            for h in range(ATT_HEADS):
                out_ref[0, g, h] = jnp.zeros((BLK, 2 * BLK), F32)
            for b in range(N_BUCKETS):
                hit = bk == b
                for h in range(ATT_HEADS):
                    out_ref[0, g, h] = jnp.where(hit, table_ref[b, g * ATT_HEADS + h], out_ref[0, g, h])
            for h in range(ATT_HEADS):
                for first in range(2):
                    out_ref[first, g, h] = jnp.where(_window_mask(first), out_ref[0, g, h], NEG_INF)

    return pl.pallas_call(
        body, name="bias_build",
        in_specs=[pl.BlockSpec(memory_space=pltpu.SMEM), pl.BlockSpec(memory_space=pltpu.VMEM)],
        out_specs=pl.BlockSpec(memory_space=pltpu.VMEM),
        out_shape=jax.ShapeDtypeStruct((2, NG, ATT_HEADS, BLK, 2 * BLK), F32),
        compiler_params=_params(),
    )(rel_bias, buckets)


def _bias_scatter(dbias, buckets):
    NG = len(DILATIONS)

    def body(dbias_ref, bucket_ref, out_ref):
        for g in range(NG):
            bk = bucket_ref[g]
            for b in range(N_BUCKETS):
                hit = bk == b
                for h in range(ATT_HEADS):
                    out_ref[b, g * ATT_HEADS + h] = jnp.sum(jnp.where(hit, dbias_ref[g, h], 0.0))

    return pl.pallas_call(
        body, name="bias_scatter",
        in_specs=[pl.BlockSpec(memory_space=pltpu.VMEM), pl.BlockSpec(memory_space=pltpu.VMEM)],
        out_specs=pl.BlockSpec(memory_space=pltpu.SMEM),
        out_shape=jax.ShapeDtypeStruct((N_BUCKETS, NG * ATT_HEADS), F32),
        compiler_params=_params(),
    )(dbias, buckets)


def _window_mask(first):
    qi = lax.broadcasted_iota(jnp.int32, (BLK, 2 * BLK), 0)
    kj = lax.broadcasted_iota(jnp.int32, (BLK, 2 * BLK), 1)
    rel = BLK + qi - kj
    return (rel >= 0) & (rel <= BLK) & (kj >= BLK * first)


def _head_lanes(hh):
    lane = lax.broadcasted_iota(jnp.int32, (1, PAIR), 1)
    return (lane >= hh * HEAD_DIM) & (lane < (hh + 1) * HEAD_DIM)


ATT_STEP_BLOCKS = 8


def _attn_steps(stride):
    per_step = math.gcd(stride, ATT_STEP_BLOCKS)
    return per_step, stride // per_step


def _attn_fwd(name, g, qkv, qc, kc, vc, bias, stride):
    T = qkv.shape[0]
    per_step, lag = _attn_steps(stride)
    chained = stride == 1
    if chained:
        per_step, lag = min(ATT_STEP_BLOCKS, T // BLK), 1
    rows = per_step * BLK
    scale = HEAD_DIM ** -0.5

    def body(q_ref, kp_ref, kc_ref, vp_ref, vc_ref, bias_ref, out_ref, *chain):
        step = pl.program_id(0)
        low = _head_lanes(0)
        if chained:
            for cat, before, now in zip(chain, (kp_ref, vp_ref), (kc_ref, vc_ref)):
                cat[:BLK, :] = before[...]
                cat[BLK:, :] = now[...]

        def block(j, carry):
            at = pl.ds(pl.multiple_of(j * BLK, BLK), BLK)
            if chained:
                first = ((step == 0) & (j == 0)).astype(jnp.int32)
                after = pl.ds(pl.multiple_of((j + 1) * BLK, BLK), BLK)
                keys = lambda cols: jnp.concatenate([chain[0][at, cols], chain[0][after, cols]], axis=0)
                values = lambda cols: jnp.concatenate([chain[1][at, cols], chain[1][after, cols]], axis=0)
            else:
                first = (step < lag).astype(jnp.int32)
                keys = lambda cols: jnp.concatenate([kp_ref[at, cols], kc_ref[at, cols]], axis=0)
                values = lambda cols: jnp.concatenate([vp_ref[at, cols], vc_ref[at, cols]], axis=0)
            for hp in range(ATT_HEADS // 2):
                cols = slice(hp * PAIR, (hp + 1) * PAIR)
                qp = q_ref[at, cols]
                kk = keys(cols)
                vv = values(cols)
                o_h, lse_h = [], []
                for hh in range(2):
                    qm = jnp.where(_head_lanes(hh), qp, jnp.zeros_like(qp))
                    s = _dot(qm, kk, NT) * scale
                    logits = s + bias_ref[first, 2 * hp + hh]
                    m = jnp.max(logits, axis=-1, keepdims=True)
                    p = jnp.exp(logits - m)
                    den = jnp.sum(p, axis=-1, keepdims=True)
                    o_h.append(_dot(p.astype(BF16), vv) / den)
                    lse_h.append(m + jnp.log(den))
                out_ref[at, cols] = jnp.where(low, o_h[0], o_h[1])
                out_ref[at, slice(ATT_WIDTH + hp * PAIR, ATT_WIDTH + (hp + 1) * PAIR)] = (
                    jnp.where(low, lse_h[0], lse_h[1]))
            return carry

        lax.fori_loop(0, per_step, block, 0)

    def cur(c):
        return pl.BlockSpec((rows, ATT_WIDTH), lambda s: (s, c))

    def prev(c):
        if chained:
            return pl.BlockSpec((BLK, ATT_WIDTH), lambda s: (jnp.maximum(s * per_step - 1, 0), c))
        return pl.BlockSpec((rows, ATT_WIDTH), lambda s: (jnp.maximum(s - lag, 0), c))

    return pl.pallas_call(
        body, name=name, grid=(T // rows,),
        in_specs=[cur(qc), prev(kc), cur(kc), prev(vc), cur(vc),
                  pl.BlockSpec((2, None, ATT_HEADS, BLK, 2 * BLK), lambda s: (0, g, 0, 0, 0))],
        out_specs=pl.BlockSpec((rows, 2 * ATT_WIDTH), lambda s: (s, 0)),
        out_shape=jax.ShapeDtypeStruct((T, 2 * ATT_WIDTH), F32),
        scratch_shapes=[pltpu.VMEM((rows + BLK, ATT_WIDTH), BF16)] * 2 if chained else [],
        compiler_params=_params(dimension_semantics=("parallel",)),
    )(qkv, qkv, qkv, qkv, qkv, bias)


def _permute_f32(p, x):
    hi = x.astype(BF16)
    rest = x - hi.astype(F32)
    mid = rest.astype(BF16)
    low = (rest - mid.astype(F32)).astype(BF16)
    return _dot(p, hi) + _dot(p, mid) + _dot(p, low)


def _attn_merge(parts):
    T = parts[0].shape[0]
    rows = min(T, REORDER_ROWS)
    n = len(parts)
    width = 2 * PAIR
    ncol = ATT_WIDTH // width

    def body(*refs):
        p_refs, o_refs, l_refs = refs[:n], refs[n:2 * n], refs[2 * n:3 * n]
        o_ref, lse_ref = refs[3 * n], refs[3 * n + 1]

        def positions(ref, g, start):
            d = DILATIONS[g]
            if d == 1:
                return ref[start:start + REORDER_TILE, :]
            span, per = BLK * d, REORDER_TILE // d
            base, t = start // span * span, start % span // REORDER_TILE
            chunks = [ref[base + r * BLK + t * per:base + r * BLK + (t + 1) * per, :] for r in range(d)]
            return _permute_f32(p_refs[g][...], jnp.concatenate(chunks, axis=0))

        for start in range(0, rows, REORDER_TILE):
            ls = [positions(l_refs[g], g, start) for g in range(n)]
            m = functools.reduce(jnp.maximum, ls)
            es = [jnp.exp(l - m) for l in ls]
            tot = functools.reduce(lambda x, y: x + y, es)
            acc = functools.reduce(lambda x, y: x + y, [e * positions(o_refs[g], g, start) for g, e in enumerate(es)])
            o_ref[start:start + REORDER_TILE, :] = (acc / tot).astype(BF16)
            lse_ref[start:start + REORDER_TILE, :] = m + jnp.log(tot)

    matrix = pl.BlockSpec((REORDER_TILE, REORDER_TILE), lambda w, c: (0, 0))
    col = pl.BlockSpec((rows, width), lambda w, c: (w, c))
    col_lse = pl.BlockSpec((rows, width), lambda w, c: (w, ncol + c))
    return pl.pallas_call(
        body, name="attn_merge", grid=(T // rows, ncol),
        in_specs=[matrix] * n + [col] * n + [col_lse] * n, out_specs=[col, col],
        out_shape=[jax.ShapeDtypeStruct((T, ATT_WIDTH), BF16), jax.ShapeDtypeStruct((T, ATT_WIDTH), F32)],
        compiler_params=_params(dimension_semantics=("parallel", "parallel")),
    )(*[_reorder_matrix(max(d, 2), True) for d in DILATIONS], *parts, *parts)


def _attn_bwd(name, g, qkv, qc, kc, vc, do, o, lse, bias, stride, comm=None):
    T = qkv.shape[0]
    per_step, lag = _attn_steps(stride)
    rows = per_step * BLK
    steps = T // rows
    scale = HEAD_DIM ** -0.5
    n_cin = len(comm.inputs) if comm else 0
    n_cout = len(comm.out_shapes) if comm else 0
    assert comm is None or comm.mid is None

    def body(*refs):
        q_ref, kp_ref, kc_ref, vp_ref, vc_ref, do_ref, o_ref, lse_ref, bias_ref = refs[:9]
        comm_in = refs[9:9 + n_cin]
        dq_ref, dkv_ref, db_ref = refs[9 + n_cin:12 + n_cin]
        comm_out = refs[12 + n_cin:12 + n_cin + n_cout]
        carry_k, carry_v = refs[12 + n_cin + n_cout:14 + n_cin + n_cout]
        comm_sems = refs[14 + n_cin + n_cout:]
        step = pl.program_id(0)

        if comm is not None:
            @pl.when(step == 0)
            def _():
                comm.start(comm_in, comm_out, comm_sems)

            @pl.when(step == steps + lag - 1)
            def _():
                comm.end(comm_in, comm_out, comm_sems)

        slot0 = (step % lag) * per_step

        @pl.when(step == 0)
        def _():
            db_ref[...] = jnp.zeros_like(db_ref)
            carry_k[...] = jnp.zeros_like(carry_k)
            carry_v[...] = jnp.zeros_like(carry_v)

        @pl.when(step >= steps)
        def _():
            def flush(j, carry):
                at = pl.ds(pl.multiple_of(j * BLK, BLK), BLK)
                dkv_ref[at, :ATT_WIDTH] = carry_k[slot0 + j].astype(BF16)
                dkv_ref[at, ATT_WIDTH:] = carry_v[slot0 + j].astype(BF16)
                return carry

            lax.fori_loop(0, per_step, flush, 0)

        @pl.when(step < steps)
        def _():
            first = (step < lag).astype(jnp.int32)

            def block(j, carry):
                at = pl.ds(pl.multiple_of(j * BLK, BLK), BLK)
                ck_ref = carry_k.at[slot0 + j]
                cv_ref = carry_v.at[slot0 + j]
                for hp in range(ATT_HEADS // 2):
                    cols = slice(hp * PAIR, (hp + 1) * PAIR)
                    qp = q_ref[at, cols]
                    kk = jnp.concatenate([kp_ref[at, cols], kc_ref[at, cols]], axis=0)
                    vv = jnp.concatenate([vp_ref[at, cols], vc_ref[at, cols]], axis=0)
                    dop = do_ref[at, cols]
                    lsep = lse_ref[at, cols]
                    prod = dop.astype(F32) * o_ref[at, cols].astype(F32)
                    dq = jnp.zeros((BLK, PAIR), F32)
                    dk = jnp.zeros((2 * BLK, PAIR), F32)
                    dv = jnp.zeros((2 * BLK, PAIR), F32)
                    for hh in range(2):
                        lanes = _head_lanes(hh)
                        qm = jnp.where(lanes, qp, jnp.zeros_like(qp))
                        dom = jnp.where(lanes, dop, jnp.zeros_like(dop))
                        km = jnp.where(lanes, kk, jnp.zeros_like(kk))
                        delta = jnp.sum(jnp.where(lanes, prod, 0.0), axis=-1, keepdims=True)
                        lse_h = jnp.max(jnp.where(lanes, lsep, NEG_INF), axis=-1, keepdims=True)
                        s = _dot(qm, kk, NT) * scale
                        logits = s + bias_ref[first, 2 * hp + hh]
                        p = jnp.exp(logits - lse_h)
                        dv += _dot(p.astype(BF16), dom, TN)
                        ds = p * (_dot(dom, vv, NT) - delta)
                        db_ref[2 * hp + hh] += ds
                        dss = (ds * scale).astype(BF16)
                        dq += _dot(dss, km)
                        dk += _dot(dss, qm, TN)
                    dq_ref[at, cols] = dq.astype(BF16)
                    dkv_ref[at, cols] = (ck_ref[:, cols] + dk[:BLK]).astype(BF16)
                    dkv_ref[at, slice(ATT_WIDTH + hp * PAIR, ATT_WIDTH + (hp + 1) * PAIR)] = (
                        cv_ref[:, cols] + dv[:BLK]).astype(BF16)
                    ck_ref[:, cols] = dk[BLK:]
                    cv_ref[:, cols] = dv[BLK:]
                return carry

            lax.fori_loop(0, per_step, block, 0)

    last = steps - 1

    def cur(c):
        return pl.BlockSpec((rows, ATT_WIDTH), lambda s: (jnp.minimum(s, last), c))

    def prev(c):
        return pl.BlockSpec((rows, ATT_WIDTH), lambda s: (jnp.clip(s - lag, 0, last), c))

    dbias_shape = (ATT_HEADS, BLK, 2 * BLK)
    res = pl.pallas_call(
        body, name=name, grid=(steps + lag,),
        in_specs=[cur(qc), prev(kc), cur(kc), prev(vc), cur(vc), cur(0), cur(0), cur(0),
                  pl.BlockSpec((2, None, ATT_HEADS, BLK, 2 * BLK), lambda s: (0, g, 0, 0, 0))] + [ANY] * n_cin,
        out_specs=[cur(0), pl.BlockSpec((rows, 2 * ATT_WIDTH), lambda s: (jnp.clip(s - lag, 0, last), 0)),
                   pl.BlockSpec(dbias_shape, lambda s: (0, 0, 0))] + [ANY] * n_cout,
        out_shape=[jax.ShapeDtypeStruct((T, ATT_WIDTH), BF16), jax.ShapeDtypeStruct((T, 2 * ATT_WIDTH), BF16),
                   jax.ShapeDtypeStruct(dbias_shape, F32)] + (comm.out_shapes if comm else []),
        scratch_shapes=[pltpu.VMEM((stride, BLK, ATT_WIDTH), F32), pltpu.VMEM((stride, BLK, ATT_WIDTH), F32)]
        + (comm.scratch if comm else []),
        compiler_params=_params(dimension_semantics=("arbitrary",)),
    )(qkv, qkv, qkv, qkv, qkv, do, o, lse, bias, *(comm.inputs if comm else []))
    return res[0], res[1], res[2], list(res[3:])


REORDER_TILE = 256
REORDER_ROWS = 2048


def _reorder_matrix(d, inverse):
    per = REORDER_TILE // d
    p = np.zeros((REORDER_TILE, REORDER_TILE), np.float32)
    for src in range(REORDER_TILE):
        i, r = divmod(src, d)
        p[r * per + i, src] = 1.0
    return jnp.asarray(p.T if inverse else p, dtype=BF16)


def _reorder_rows(name, src, d, inverse, *, src_col=0, col_stride=1, ncols=1, dst=None, dst_col=0, dst_stride=1,
                  dst_blocks=None):
    T = src.shape[0]
    dtype = src.dtype
    span = BLK * d
    rows = max(span, min(T, REORDER_ROWS))
    per = REORDER_TILE // d
    tiles = span // REORDER_TILE
    dst_blocks = ncols if dst_blocks is None else dst_blocks

    def apply(p, x):
        return _dot(p, x).astype(BF16) if dtype == BF16 else _permute_f32(p, x)

    def body(*refs):
        p_ref, x_ref, o_ref = refs[0], refs[1], refs[-1]
        if d == 1:
            o_ref[...] = x_ref[...]
            return
        for s in range(rows // span):
            for t in range(tiles):
                base = s * span
                tile_rows = slice(base + t * REORDER_TILE, base + (t + 1) * REORDER_TILE)
                chunk = lambda r: slice(base + r * BLK + t * per, base + r * BLK + (t + 1) * per)
                if inverse:
                    gathered = jnp.concatenate([x_ref[chunk(r), :] for r in range(d)], axis=0)
                    o_ref[tile_rows, :] = apply(p_ref[...], gathered)
                else:
                    y = apply(p_ref[...], x_ref[tile_rows, :])
                    for r in range(d):
                        o_ref[chunk(r), :] = y[r * per:(r + 1) * per]

    in_specs = [pl.BlockSpec((REORDER_TILE, REORDER_TILE), lambda w, k: (0, 0)),
                pl.BlockSpec((rows, ATT_WIDTH), lambda w, k: (w, src_col + col_stride * k))]
    operands = [_reorder_matrix(max(d, 2), inverse), src]
    aliases = {}
    if dst is not None:
        in_specs.append(ANY)
        operands.append(dst)
        aliases = {2: 0}
    return pl.pallas_call(
        body, name=name, grid=(T // rows, ncols), in_specs=in_specs,
        out_specs=pl.BlockSpec((rows, ATT_WIDTH), lambda w, k: (w, dst_col + dst_stride * k)),
        out_shape=jax.ShapeDtypeStruct((T, dst_blocks * ATT_WIDTH), dtype),
        input_output_aliases=aliases,
        compiler_params=_params(dimension_semantics=("parallel", "parallel")),
    )(*operands)


def _group_qkv(qkv, g, d):
    NG = len(DILATIONS)
    if d == 1:
        return qkv, (g, NG + g, 2 * NG + g)
    return _reorder_rows(f"qkv_to_residues{g}", qkv, d, False, src_col=g, col_stride=NG, ncols=3), (0, 1, 2)


def _attention_fwd(qkv, bias):
    T = qkv.shape[0]
    parts = []
    for g, d in enumerate(DILATIONS):
        src, (qc, kc, vc) = _group_qkv(qkv, g, d)
        parts.append(_attn_fwd(f"attn_fwd_{g}", g, src, qc, kc, vc, bias, d))
    return _attn_merge(parts)


def _attention_bwd(qkv, do, o, lse, bias, comms):
    NG = len(DILATIONS)
    dqkv, dbs, carried = None, [], []
    for g, d in enumerate(DILATIONS):
        src, (qc, kc, vc) = _group_qkv(qkv, g, d)
        do_g, o_g, lse_g = do, o, lse
        if d > 1:
            do_g = _reorder_rows(f"do_to_residues{g}", do, d, False)
            o_g = _reorder_rows(f"o_to_residues{g}", o, d, False)
            lse_g = _reorder_rows(f"lse_to_residues{g}", lse, d, False)
        dq, dkv, db, sent = _attn_bwd(f"attn_bwd_{g}", g, src, qc, kc, vc, do_g, o_g, lse_g, bias, d, comm=comms[g])
        dqkv = _reorder_rows(f"dq_to_positions{g}", dq, d, True, dst=dqkv, dst_col=g, dst_blocks=3 * NG)
        dqkv = _reorder_rows(f"dkv_to_positions{g}", dkv, d, True, ncols=2, dst=dqkv, dst_col=NG + g, dst_stride=NG,
                             dst_blocks=3 * NG)
        dbs.append(db)
        carried.append(sent)
    return dqkv, jnp.stack(dbs), carried


def _other_chips(x, y):
    return [(1 - x, y), (x, 1 - y), (1 - x, 1 - y)]


def _shard_region(ref, shape, by_cols, chip, rows=None):
    R, C = shape
    start, size = (0, R) if rows is None else rows
    if by_cols:
        return ref.at[pl.ds(start, size), pl.ds(chip * C, C)]
    return ref.at[pl.ds(chip * R + start, size), :]


def _gather_weights(entries):
    n = len(entries)
    shapes = [e[0].shape[1:] for e in entries]

    def places(ins, outs, sems):
        send_sems, recv_sems, local_sems = sems
        x, y, c = lax.axis_index("x"), lax.axis_index("y"), lax.axis_index("c")

        def landing(f, px, py, pc):
            R = shapes[f][0]
            return _shard_region(outs[f], shapes[f], entries[f][2], 2 * px + py, rows=(pc * (R // 2), R // 2))

        def copy(f, k, block, to, src=None):
            dst = landing(f, *block)
            return pltpu.make_async_remote_copy(
                src_ref=dst if src is None else src, dst_ref=dst,
                send_sem=send_sems.at[6 * f + k], recv_sem=recv_sems.at[6 * f + k],
                device_id=to, device_id_type=MESH)

        def mine(f):
            dst = _shard_region(outs[f], shapes[f], entries[f][2], 2 * x + y)
            return pltpu.make_async_copy(ins[f].at[entries[f][1]], dst, local_sems.at[f])

        def first(f, j):
            R = shapes[f][0]
            src = ins[f].at[entries[f][1], pl.ds(c * (R // 2), R // 2), :]
            return copy(f, j, (x, y, c), (*_other_chips(x, y)[j], c), src=src)

        return x, y, c, copy, mine, first

    def start(ins, outs, sems):
        _, _, _, _, mine, first = places(ins, outs, sems)
        for f in range(n):
            mine(f).start()
        for j in range(3):
            for f in range(n):
                first(f, j).start()

    def mid(ins, outs, sems):
        x, y, c, copy, _, _ = places(ins, outs, sems)
        for j, chip in enumerate(_other_chips(x, y)):
            for f in range(n):
                copy(f, j, (*chip, c), (x, y, c)).wait_recv()
                copy(f, 3 + j, (*chip, c), (x, y, 1 - c)).start()

    def end(ins, outs, sems):
        x, y, c, copy, mine, first = places(ins, outs, sems)
        for j, chip in enumerate(_other_chips(x, y)):
            for f in range(n):
                copy(f, 3 + j, (*chip, 1 - c), (x, y, c)).wait_recv()
        for j, chip in enumerate(_other_chips(x, y)):
            for f in range(n):
                first(f, j).wait_send()
                copy(f, 3 + j, (*chip, c), (x, y, 1 - c)).wait_send()
        for f in range(n):
            mine(f).wait()

    def whole(f):
        R, C = shapes[f]
        return (R, N_CHIPS * C) if entries[f][2] else (N_CHIPS * R, C)

    return _Comm(
        [e[0] for e in entries], [jax.ShapeDtypeStruct(whole(f), BF16) for f in range(n)],
        [pltpu.SemaphoreType.DMA((6 * n,)), pltpu.SemaphoreType.DMA((6 * n,)), pltpu.SemaphoreType.DMA((n,))],
        start, end, mid)


def _scatter_grads(entries):
    n = len(entries)

    def copies(ins, outs, sems):
        send_sems, recv_sems, local_sems = sems
        x, y, c = lax.axis_index("x"), lax.axis_index("y"), lax.axis_index("c")
        me = 2 * x + y

        def piece(f, chip):
            return _shard_region(ins[f], entries[f][1], entries[f][2], chip)

        mine = [pltpu.make_async_copy(piece(f, me), outs[f].at[me], local_sems.at[f]) for f in range(n)]
        sends = [pltpu.make_async_remote_copy(
            src_ref=piece(f, 2 * px + py), dst_ref=outs[f].at[me],
            send_sem=send_sems.at[3 * f + j], recv_sem=recv_sems.at[3 * f + j],
            device_id=(px, py, c), device_id_type=MESH)
            for j, (px, py) in enumerate(_other_chips(x, y)) for f in range(n)]
        return mine, sends

    def start(ins, outs, sems):
        mine, sends = copies(ins, outs, sems)
        for cp in mine + sends:
            cp.start()

    def end(ins, outs, sems):
        mine, sends = copies(ins, outs, sems)
        for cp in sends + mine:
            cp.wait()

    return _Comm(
        [e[0] for e in entries], [jax.ShapeDtypeStruct((N_CHIPS,) + tuple(e[1]), BF16) for e in entries],
        [pltpu.SemaphoreType.DMA((3 * n,)), pltpu.SemaphoreType.DMA((3 * n,)), pltpu.SemaphoreType.DMA((n,))],
        start, end)


def _exchange_sibling(parts):
    n = len(parts)

    def copies(ins, outs, sems):
        send_sems, recv_sems = sems
        sibling = (lax.axis_index("x"), lax.axis_index("y"), 1 - lax.axis_index("c"))
        return [pltpu.make_async_remote_copy(src_ref=ins[i], dst_ref=outs[i], send_sem=send_sems.at[i],
                                             recv_sem=recv_sems.at[i], device_id=sibling, device_id_type=MESH)
                for i in range(n)]

    def start(ins, outs, sems):
        for cp in copies(ins, outs, sems):
            cp.start()

    def end(ins, outs, sems):
        for cp in copies(ins, outs, sems):
            cp.wait()

    return _Comm(parts, [jax.ShapeDtypeStruct(s.shape, s.dtype) for s in parts],
                 [pltpu.SemaphoreType.DMA((n,)), pltpu.SemaphoreType.DMA((n,))], start, end)


def _allgather_small(block):
    m_per, ncol = block.shape

    def places(ins, outs, sems):
        send_sems, recv_sems, local_sem = sems
        x, y, c = lax.axis_index("x"), lax.axis_index("y"), lax.axis_index("c")

        def rows(px, py, pc):
            return outs[0].at[4 * px + 2 * py + pc]

        def copy(k, block_of, to, src=None):
            return pltpu.make_async_remote_copy(
                src_ref=rows(*block_of) if src is None else src, dst_ref=rows(*block_of),
                send_sem=send_sems.at[k], recv_sem=recv_sems.at[k], device_id=to, device_id_type=MESH)

        mine = pltpu.make_async_copy(ins[0], rows(x, y, c), local_sem.at[0])
        first = [copy(0, (x, y, c), (x, y, 1 - c), src=ins[0])]
        first += [copy(1 + j, (x, y, c), (*chip, c), src=ins[0]) for j, chip in enumerate(_other_chips(x, y))]
        passed = [copy(4 + j, (*chip, c), (x, y, 1 - c)) for j, chip in enumerate(_other_chips(x, y))]
        return x, y, c, copy, mine, first, passed

    def start(ins, outs, sems):
        _, _, _, _, mine, first, _ = places(ins, outs, sems)
        for cp in [mine] + first:
            cp.start()

    def mid(ins, outs, sems):
        x, y, c, copy, _, _, passed = places(ins, outs, sems)
        for j, chip in enumerate(_other_chips(x, y)):
            copy(1 + j, (*chip, c), (x, y, c)).wait_recv()
            passed[j].start()

    def end(ins, outs, sems):
        x, y, c, copy, mine, first, passed = places(ins, outs, sems)
        copy(0, (x, y, 1 - c), (x, y, c)).wait_recv()
        for j, chip in enumerate(_other_chips(x, y)):
            copy(4 + j, (*chip, 1 - c), (x, y, c)).wait_recv()
        for cp in first + passed:
            cp.wait_send()
        mine.wait()

    return _Comm([block], [jax.ShapeDtypeStruct((N_DEV, m_per, ncol), block.dtype)],
                 [pltpu.SemaphoreType.DMA((7,)), pltpu.SemaphoreType.DMA((7,)), pltpu.SemaphoreType.DMA((1,))],
                 start, end, mid)


def _join_comms(*progs):
    def split(parts, counts):
        out, pos = [], 0
        for n in counts:
            out.append(parts[pos:pos + n])
            pos += n
        return out

    def phase(which):
        def run(ins, outs, sems):
            args = zip(split(ins, [len(p.inputs) for p in progs]), split(outs, [len(p.out_shapes) for p in progs]),
                       split(sems, [len(p.scratch) for p in progs]))
            for p, (i, o, s) in zip(progs, args):
                fn = getattr(p, which)
                if fn is not None:
                    fn(i, o, s)
        return run

    return _Comm([a for p in progs for a in p.inputs], [s for p in progs for s in p.out_shapes],
                 [s for p in progs for s in p.scratch], phase("start"), phase("end"), phase("mid"))


def _adamw(w, g, m, v):
    m = ADAM_B1 * m + (1.0 - ADAM_B1) * g
    v = ADAM_B2 * v + (1.0 - ADAM_B2) * jnp.square(g)
    m_hat = m / (1.0 - ADAM_B1 ** ADAM_STEP)
    v_hat = v / (1.0 - ADAM_B2 ** ADAM_STEP)
    delta = -ADAM_LR * (m_hat / (jnp.sqrt(v_hat) + ADAM_EPS) + ADAM_WD * w)
    return delta, m, v


def _flat_tile(rows):
    return min(rows, 512)


def _sum_pieces(name, layers):
    L = len(layers)
    P, R, C = layers[0].shape
    tr = _flat_tile(R)

    def body(*refs):
        out_ref = refs[L]
        for l in range(L):
            @pl.when(pl.program_id(0) == l)
            def _(p_ref=refs[l]):
                acc = p_ref[0].astype(F32)
                for j in range(1, P):
                    acc = acc + p_ref[j].astype(F32)
                out_ref[...] = acc

    return pl.pallas_call(
        body, name=name, grid=(L, R // tr),
        in_specs=[pl.BlockSpec((P, tr, C), lambda l, i: (0, i, 0)) for _ in range(L)],
        out_specs=pl.BlockSpec((None, tr, C), lambda l, i: (l, i, 0)),
        out_shape=jax.ShapeDtypeStruct((L, R, C), F32),
        compiler_params=_params(dimension_semantics=("parallel", "parallel")),
    )(*layers)


def _adam_pair(name, w, m, v, part_a, part_b):
    L, R, C = w.shape
    tr = _flat_tile(R)

    def body(w_ref, m_ref, v_ref, a_ref, b_ref, g_ref, d_ref, nm_ref, nv_ref):
        g = a_ref[...] + b_ref[...]
        g_ref[...] = g
        d_ref[...], nm_ref[...], nv_ref[...] = _adamw(w_ref[...], g, m_ref[...], v_ref[...])

    row = pl.BlockSpec((None, tr, C), lambda l, i: (l, i, 0))
    return pl.pallas_call(
        body, name=name, grid=(L, R // tr),
        in_specs=[row] * 5, out_specs=[row] * 4,
        out_shape=[jax.ShapeDtypeStruct((L, R, C), F32)] * 4,
        compiler_params=_params(dimension_semantics=("parallel", "parallel")),
    )(w, m, v, part_a, part_b)


def _adam_small(name, w, m, v, gathered):
    R, C = w.shape

    def body(w_ref, m_ref, v_ref, p_ref, g_ref, d_ref, nm_ref, nv_ref):
        g = p_ref[0]
        for j in range(1, N_DEV):
            g = g + p_ref[j]
        g_ref[...] = g
        d_ref[...], nm_ref[...], nv_ref[...] = _adamw(w_ref[...], g, m_ref[...], v_ref[...])

    return pl.pallas_call(
        body, name=name,
        out_shape=[jax.ShapeDtypeStruct((R, C), F32)] * 4,
        compiler_params=_params(),
    )(w, m, v, gathered)


SMALL_EARLY = ("mlp_norm_g", "final_norm_g", "a_ln_g", "a_ln_b", "a_w_s", "a_b_s", "rel_bias")


SUBLANES = 8


def _packed_rows(size, width):
    return -(-size // (SUBLANES * width)) * SUBLANES


def _pack_small(arrays, width):
    rows = []
    for a in arrays:
        flat = a.reshape(-1)
        nrows = _packed_rows(flat.shape[0], width)
        rows.append(jnp.pad(flat, (0, nrows * width - flat.shape[0])).reshape(nrows, width))
    return jnp.concatenate(rows, axis=0)


def _unpack_small(block, shapes, width):
    out, row = [], 0
    for shape in shapes:
        size = int(np.prod(shape))
        out.append(block[row:row + _packed_rows(size, width)].reshape(-1)[:size].reshape(shape))
        row += _packed_rows(size, width)
    return out


def kernel(x, mix_norm_g, mlp_norm_g, final_norm_g, a_w_in, a_ln_g, a_ln_b, a_w_s, a_b_s, a_w_out, b_w_qkv, b_w_out, rel_bias, w_up, w_down, loss_target, m_mix_norm_g, m_mlp_norm_g, m_final_norm_g, m_a_w_in, m_a_ln_g, m_a_ln_b, m_a_w_s, m_a_b_s, m_a_w_out, m_b_w_qkv, m_b_w_out, m_rel_bias, m_w_up, m_w_down, v_mix_norm_g, v_mlp_norm_g, v_final_norm_g, v_a_w_in, v_a_ln_g, v_a_ln_b, v_a_w_s, v_a_b_s, v_a_w_out, v_b_w_qkv, v_b_w_out, v_rel_bias, v_w_up, v_w_down):
    T, D = x.shape[1], x.shape[2]
    h0 = x.reshape(T, D)
    target = loss_target.reshape(T, D)
    G = a_w_s.shape[1]

    w_big = [a_w_in, a_w_out, b_w_qkv, b_w_out, w_up, w_down]
    m_big = [m_a_w_in, m_a_w_out, m_b_w_qkv, m_b_w_out, m_w_up, m_w_down]
    v_big = [v_a_w_in, v_a_w_out, v_b_w_qkv, v_b_w_out, v_w_up, v_w_down]
    by_cols = [True, False, True, True, True, False]
    s_in, s_out, s_qkv, s_bo, s_up, s_dn = [w.astype(BF16) for w in w_big]
    W_in, W_out = _run_comm("gather_a", _gather_weights([(s_in, 0, True), (s_out, 0, False)]))

    tril = jnp.tril(jnp.ones((CHUNK, CHUNK), dtype=bool))
    w_tril = jnp.where(tril[None], a_w_s[0], 0.0).astype(BF16)
    w_tril_t = jnp.swapaxes(w_tril, 1, 2)
    b_rows = jnp.broadcast_to(a_b_s[0][:, :, None], (G, CHUNK, CHUNK))
    buckets = _bucket_maps()
    bias = _bias_build(rel_bias, buckets)

    QKV = s_qkv.shape[2] * N_CHIPS
    TM = 1024
    TK_WGRAD = 4096

    def matmul(name, a, b, mode, out, tm=TM, tn=1024, **kw):
        outs = out if isinstance(out, list) else [out]
        return _mm(name, a, b, mode, tm=tm, tn=tn, tk=a.shape[1], outs=outs, **kw)

    def norm_bwd(layer_gain, h, dres, copies=2):
        return dict(epi=_epi_rms_bwd(copies), extras=(h, dres), vecs=(layer_gain,), col_sums=1)

    def wgrad(name, a, b, tn=1024, tk=TK_WGRAD, comm=None):
        return _mm(name, a, b, "tn", tm=1024, tn=tn, tk=tk, outs=[BF16], comm=comm)

    def scatter(*which):
        return _scatter_grads([(g, w_big[i].shape[1:], by_cols[i]) for g, i in which])

    (a_pre, y0), (W_up0,) = matmul("a_in", h0, W_in, "nn", BF16, norm_gain=mix_norm_g[0:1],
                                   comm=_gather_weights([(s_up, 0, True)]))
    z = _gate_fwd(a_pre, a_ln_g, a_ln_b, w_tril, b_rows)
    h1 = matmul("a_out", z, W_out, "nn", F32, epi=_epi_residual, extras=(h0,))
    (q1, y1), (W_dn0,) = matmul("mlp_up0", h1, W_up0, "nn", BF16, epi=_epi_relu2, norm_gain=mlp_norm_g[0:1],
                                comm=_gather_weights([(s_dn, 0, False)]))
    h2, (W_qkv, W_bo) = matmul("mlp_down0", q1, W_dn0, "nn", F32, tm=TM // 2, epi=_epi_residual, extras=(h1,),
                               comm=_gather_weights([(s_qkv, 0, True), (s_bo, 0, True)]))
    (qkv, y2), (W_up1,) = matmul("b_qkv", h2, W_qkv, "nn", BF16, tn=QKV // 4, norm_gain=mix_norm_g[1:2],
                                 comm=_gather_weights([(s_up, 1, True)]))
    o, lse = _attention_fwd(qkv, bias)
    h3 = matmul("b_out", o, W_bo, "nn", F32, epi=_epi_residual, extras=(h2,))
    (q3, y3), (W_dn1,) = matmul("mlp_up1", h3, W_up1, "nn", BF16, epi=_epi_relu2, norm_gain=mlp_norm_g[1:2],
                                comm=_gather_weights([(s_dn, 1, False)]))
    dh4, dh4_b, d_final_g, loss_row = matmul("mlp_down1", q3, W_dn1, "nn", [F32, BF16], tm=TM // 2, epi=_epi_loss_head,
                                             extras=(h3, target), vecs=(final_norm_g.reshape(1, D),), col_sums=2)

    dp3 = matmul("mlp_down_bwd1", dh4_b, W_dn1, "nt", BF16, epi=_epi_relu2_grad, extras=(q3,))
    g_dn1 = wgrad("mlp_down_wgrad1", q3, dh4_b)
    g_up1 = wgrad("mlp_up_wgrad1", y3, dp3)
    dh3, dh3_b, dg_mlp1 = matmul("mlp_up_bwd1", dp3, W_up1, "nt", [F32, BF16], tm=TM // 2,
                                 **norm_bwd(mlp_norm_g[1:2], h3, dh4))
    do = matmul("b_out_bwd", dh3_b, W_bo, "nt", BF16)
    g_bo = wgrad("b_out_wgrad", o, dh3_b)
    dqkv, dbias, ((r_dn1,), (r_up1,), (r_bo,)) = _attention_bwd(
        qkv, do, o, lse, bias, [scatter((g_dn1, 5)), scatter((g_up1, 4)), scatter((g_bo, 3))])
    d_rel_bias = _bias_scatter(dbias, buckets)
    dh2, dh2_b, dg_mix1 = matmul("b_qkv_bwd", dqkv, W_qkv, "nt", [F32, BF16], tm=TM // 2,
                                 **norm_bwd(mix_norm_g[1:2], h2, dh3))
    g_qkv = wgrad("b_qkv_wgrad", y2, dqkv, tn=QKV // 3, tk=TK_WGRAD // 2)
    dp1, (r_qkv,) = matmul("mlp_down_bwd0", dh2_b, W_dn0, "nt", BF16, epi=_epi_relu2_grad, extras=(q1,),
                           comm=scatter((g_qkv, 2)))
    g_up0 = wgrad("mlp_up_wgrad0", y1, dp1)
    g_dn0, (r_up0,) = wgrad("mlp_down_wgrad0", q1, dh2_b, comm=scatter((g_up0, 4)))
    (dh1, dh1_b, dg_mlp0), (r_dn0,) = matmul("mlp_up_bwd0", dp1, W_up0, "nt", [F32, BF16], tm=TM // 2,
                                             comm=scatter((g_dn0, 5)), **norm_bwd(mlp_norm_g[0:1], h1, dh2))
    dz = matmul("a_out_bwd", dh1_b, W_out, "nt", F32)
    g_out = wgrad("a_out_wgrad", z, dh1_b)
    da, d_ln_g, d_ln_b, d_w_s, d_b_s = _gate_bwd(a_pre, dz, a_ln_g, a_ln_b, w_tril, w_tril_t, b_rows)
    width = max(D, 128)
    unused = jnp.zeros((1, 1), F32)
    early_w = [mlp_norm_g, final_norm_g, a_ln_g, a_ln_b, a_w_s, a_b_s, rel_bias, unused]
    early_m = [m_mlp_norm_g, m_final_norm_g, m_a_ln_g, m_a_ln_b, m_a_w_s, m_a_b_s, m_rel_bias, unused]
    early_v = [v_mlp_norm_g, v_final_norm_g, v_a_ln_g, v_a_ln_b, v_a_w_s, v_a_b_s, v_rel_bias, unused]
    early_g = [jnp.concatenate([dg_mlp0, dg_mlp1]), d_final_g, d_ln_g, d_ln_b, d_w_s[None], d_b_s[None, :, :, 0],
               d_rel_bias, loss_row[:, :1]]
    g_in, (r_out, gathered_early) = wgrad("a_in_wgrad", y0, da, comm=_join_comms(
        scatter((g_out, 1)), _allgather_small(_pack_small(early_g, width))))
    grad_x, dg_mix0 = matmul("a_in_bwd", da, W_in, "nt", F32, **norm_bwd(mix_norm_g[0:1], h0, dh1, copies=1))

    late_g = [jnp.concatenate([dg_mix0, dg_mix1])]
    received = [None, [r_out], [r_qkv], [r_bo], [r_up0, r_up1], [r_dn0, r_dn1]]
    plane = [None] + [_sum_pieces(f"sum_pieces{i}", received[i]) for i in range(1, len(w_big))]
    tail = _run_comm("tail_comm", _join_comms(scatter((g_in, 0)), _exchange_sibling(plane[1:]),
                                              _allgather_small(_pack_small(late_g, width))))
    r_in, other, gathered_late = tail[0], [None] + list(tail[1:len(w_big)]), tail[len(w_big)]
    plane[0] = _sum_pieces("sum_pieces0", [r_in])
    (other[0],) = _run_comm("exchange_a_in", _exchange_sibling([plane[0]]))
    big_out = [_adam_pair(f"adam{i}", w_big[i], m_big[i], v_big[i], plane[i], other[i]) for i in range(len(w_big))]

    def small_step(name, ws, ms, vs, gathered):
        out = _adam_small(name, _pack_small(ws, width), _pack_small(ms, width), _pack_small(vs, width), gathered)
        return [_unpack_small(o, [w.shape for w in ws], width) for o in out]

    early_out = small_step("adam_small_early", early_w, early_m, early_v, gathered_early)
    late_out = small_step("adam_small_late", [mix_norm_g], [m_mix_norm_g], [v_mix_norm_g], gathered_late)
    loss = early_out[0][-1][0, 0]

    names = ["mix_norm_g", "mlp_norm_g", "final_norm_g", "a_w_in", "a_ln_g", "a_ln_b", "a_w_s", "a_b_s", "a_w_out",
             "b_w_qkv", "b_w_out", "rel_bias", "w_up", "w_down"]
    results = [loss, grad_x.reshape(x.shape)]
    for kind in range(4):
        table = dict(zip(SMALL_EARLY, early_out[kind]))
        table["mix_norm_g"] = late_out[kind][0]
        table.update(zip(["a_w_in", "a_w_out", "b_w_qkv", "b_w_out", "w_up", "w_down"], [b[kind] for b in big_out]))
        results += [table[n] for n in names]
    return tuple(results)
```

```python
import functools
import math

import numpy as np
import jax
import jax.numpy as jnp
from jax import lax
from jax.experimental import pallas as pl
from jax.experimental.pallas import tpu as pltpu

F32 = jnp.float32
BF16 = jnp.bfloat16
MESH = pl.DeviceIdType.MESH
ANY = pl.BlockSpec(memory_space=pl.ANY)

N_CHIPS = 4
N_DEV = 8
VMEM_LIMIT_BYTES = 56 * 1024 * 1024

EPS = 1e-6
NEG_INF = -1e30
CHUNK = 128
GROUP_DIM = 128
HEAD_DIM = 64
ATT_HEADS = 8
ATT_WIDTH = ATT_HEADS * HEAD_DIM
PAIR = 2 * HEAD_DIM
BLK = 128
DILATIONS = (1, 4, 16)
N_BUCKETS = 32
MAX_EXACT = N_BUCKETS // 2
REL_MAX_DISTANCE = 2048

ADAM_LR = 0.001
ADAM_B1 = 0.9
ADAM_B2 = 0.999
ADAM_EPS = 1e-08
ADAM_WD = 0.01
ADAM_STEP = 10

NN = (((1,), (0,)), ((), ()))
NT = (((1,), (1,)), ((), ()))
TN = (((0,), (0,)), ((), ()))


def _params(**kw):
    return pltpu.CompilerParams(vmem_limit_bytes=VMEM_LIMIT_BYTES, **kw)


def _dot(a, b, dims=NN):
    return lax.dot_general(a, b, dims, preferred_element_type=F32)


def _gelu(x):
    return 0.5 * x * (1.0 + lax.erf(x * math.sqrt(0.5)))


def _gelu_grad(x):
    return 0.5 * (1.0 + lax.erf(x * math.sqrt(0.5))) + x * jnp.exp(-0.5 * x * x) * (1.0 / math.sqrt(2.0 * math.pi))


def _mean(x):
    return jnp.mean(x, axis=-1, keepdims=True)


class _Comm:
    def __init__(self, inputs, out_shapes, scratch, start, end, mid=None):
        self.inputs, self.out_shapes, self.scratch = list(inputs), list(out_shapes), list(scratch)
        self.start, self.mid, self.end = start, mid, end


def _run_comm(name, comm):
    n_in, n_out = len(comm.inputs), len(comm.out_shapes)

    def body(*refs):
        parts = refs[:n_in], refs[n_in:n_in + n_out], refs[n_in + n_out:]
        comm.start(*parts)
        if comm.mid is not None:
            comm.mid(*parts)
        comm.end(*parts)

    return pl.pallas_call(
        body, name=name, in_specs=[ANY] * n_in, out_specs=[ANY] * n_out, out_shape=comm.out_shapes,
        scratch_shapes=comm.scratch, compiler_params=_params(),
    )(*comm.inputs)


def _mm(name, a, b, mode, *, tm, tn, tk, outs, epi=None, extras=(), vecs=(), col_sums=0, norm_gain=None, comm=None):
    if mode == "tn":
        K, M = a.shape
    else:
        M, K = a.shape
    N = b.shape[0] if mode == "nt" else b.shape[1]
    tm, tn, tk = min(tm, M), min(tn, N), min(tk, K)
    assert M % tm == 0 and N % tn == 0 and K % tk == 0, (name, M, N, K, tm, tn, tk)
    nk = K // tk
    grid = (M // tm, N // tn, nk)

    if mode == "tn":
        a_spec = pl.BlockSpec((tk, tm), lambda i, j, k: (k, i))
    else:
        a_spec = pl.BlockSpec((tm, tk), lambda i, j, k: (i, k))
    if mode == "nt":
        b_spec = pl.BlockSpec((tn, tk), lambda i, j, k: (j, k))
    else:
        b_spec = pl.BlockSpec((tk, tn), lambda i, j, k: (k, j))
    tile = pl.BlockSpec((tm, tn), lambda i, j, k: (i, j))
    vec = pl.BlockSpec((1, tn), lambda i, j, k: (0, j))
    normed = norm_gain is not None
    assert not normed or (mode == "nn" and nk == 1 and tm % grid[1] == 0)
    assert col_sums == 0 or grid[1] == 1
    out_shapes = [jax.ShapeDtypeStruct((M, N), dtype) for dtype in outs]
    out_specs = [tile for _ in outs]
    extra_specs = [tile for _ in extras] + [vec for _ in vecs]
    if normed:
        part_rows = tm // grid[1]
        last_part = M // part_rows - 1
        a_spec = pl.BlockSpec((tm, K), lambda i, j, k: (0, 0))
        out_shapes.append(jax.ShapeDtypeStruct((M, K), BF16))
        out_specs.append(pl.BlockSpec((part_rows, K), lambda i, j, k: (i * grid[1] + j, 0)))
        extra_specs.append(pl.BlockSpec((1, K), lambda i, j, k: (0, 0)))
        extra_specs.append(pl.BlockSpec((part_rows, K),
                                        lambda i, j, k: (jnp.minimum((i + 1) * grid[1] + j, last_part), 0)))
    out_shapes += [jax.ShapeDtypeStruct((1, N), F32)] * col_sums
    out_specs += [vec] * col_sums
    n_extra, n_out = len(extra_specs), len(out_shapes)
    n_tiles = len(outs)
    n_cin = len(comm.inputs) if comm else 0
    n_cout = len(comm.out_shapes) if comm else 0
    dims = {"nn": NN, "nt": NT, "tn": TN}[mode]
    steps = grid[0] * grid[1] * grid[2]

    def body(*refs):
        a_ref, b_ref = refs[0], refs[1]
        pos = 2
        extra_refs = refs[pos:pos + n_extra]
        pos += n_extra
        comm_in = refs[pos:pos + n_cin]
        pos += n_cin
        out_refs = refs[pos:pos + n_out]
        pos += n_out
        comm_out = refs[pos:pos + n_cout]
        pos += n_cout
        acc_ref = refs[pos] if nk > 1 else None
        pos += nk > 1
        y_refs = refs[pos:pos + 2 * normed]
        comm_sems = refs[pos + 2 * normed:]
        k = pl.program_id(2)
        step = (pl.program_id(0) * grid[1] + pl.program_id(1)) * nk + k

        if comm is not None:
            @pl.when(step == 0)
            def _():
                comm.start(comm_in, comm_out, comm_sems)

        def finish(acc):
            epi_args = [e[...] for e in extra_refs[:n_extra - 2 * normed]]
            res = epi(acc, *epi_args) if epi is not None else (acc,) * n_tiles
            for o, r in zip(out_refs[:n_tiles], res[:n_tiles]):
                o[...] = r.astype(o.dtype)
            if col_sums:
                sums = out_refs[n_out - col_sums:]

                @pl.when(pl.program_id(0) == 0)
                def _():
                    for o in sums:
                        o[...] = jnp.zeros_like(o)

                for o, r in zip(sums, res[n_tiles:]):
                    o[...] += r

        if normed:
            gain_ref, ahead_ref = extra_refs[-2], extra_refs[-1]

            def norm(hv):
                return (hv * lax.rsqrt(_mean(hv * hv) + EPS) * gain_ref[...]).astype(BF16)

            @pl.when(step == 0)
            def _():
                y_refs[0][...] = norm(a_ref[...])

            part_at = pl.ds(pl.multiple_of(pl.program_id(1) * part_rows, part_rows), part_rows)
            for parity in range(2):
                @pl.when(pl.program_id(0) % 2 == parity)
                def _(y_now=y_refs[parity], y_next=y_refs[1 - parity]):
                    finish(_dot(y_now[...], b_ref[...].astype(BF16), dims))
                    out_refs[n_tiles][...] = y_now[part_at, :]
                    y_next[part_at, :] = norm(ahead_ref[...])
        elif nk == 1:
            finish(_dot(a_ref[...].astype(BF16), b_ref[...].astype(BF16), dims))
        else:
            part = _dot(a_ref[...].astype(BF16), b_ref[...].astype(BF16), dims)

            @pl.when(k == 0)
            def _():
                acc_ref[...] = part

            @pl.when(k > 0)
            def _():
                acc_ref[...] += part

            @pl.when(k == nk - 1)
            def _():
                finish(acc_ref[...])

        if comm is not None:
            if comm.mid is not None:
                @pl.when(step == (3 * steps) // 4)
                def _():
                    comm.mid(comm_in, comm_out, comm_sems)

            @pl.when(step == steps - 1)
            def _():
                comm.end(comm_in, comm_out, comm_sems)

    sequential = comm is not None or normed or col_sums > 0
    order = ("arbitrary",) * 3 if sequential else ("parallel", "parallel", "arbitrary")
    scratch = [pltpu.VMEM((tm, tn), F32)] if nk > 1 else []
    if normed:
        scratch += [pltpu.VMEM((tm, K), BF16)] * 2
    res = pl.pallas_call(
        body, name=name, grid=grid,
        in_specs=[a_spec, b_spec] + extra_specs + [ANY] * n_cin,
        out_specs=out_specs + [ANY] * n_cout,
        out_shape=out_shapes + (comm.out_shapes if comm else []),
        scratch_shapes=scratch + (comm.scratch if comm else []),
        compiler_params=_params(dimension_semantics=order),
    )(a, b, *extras, *vecs, *([norm_gain, a] if normed else []), *(comm.inputs if comm else []))
    mm_out = res[0] if n_out == 1 else list(res[:n_out])
    return (mm_out, list(res[n_out:])) if comm else mm_out


def _epi_residual(acc, res):
    return (res + acc,)


def _epi_relu2(acc):
    return (jnp.square(jnp.maximum(acc, 0.0)),)


def _epi_rms_bwd(copies):
    def epi(acc, h, dres, g):
        r = lax.rsqrt(_mean(h * h) + EPS)
        hn = h * r
        dyg = acc * g
        dh = dres + r * (dyg - hn * _mean(dyg * hn))
        return (dh,) * copies + (jnp.sum(acc * hn, axis=0, keepdims=True),)
    return epi


def _epi_loss_head(acc, res, target, g):
    h = res + acc
    r = lax.rsqrt(_mean(h * h) + EPS)
    hn = h * r
    diff = hn * g - target
    loss = 0.5 * jnp.sum(_mean(diff * diff))
    dy = diff * (1.0 / h.shape[-1])
    dyg = dy * g
    dh = r * (dyg - hn * _mean(dyg * hn))
    return dh, dh, jnp.sum(dy * hn, axis=0, keepdims=True), jnp.full((1, h.shape[-1]), loss, F32)


def _epi_relu2_grad(acc, q):
    qf = q.astype(F32)
    return (acc * jnp.where(qf > 0.0, (2.0 * qf) * lax.rsqrt(qf), 0.0),)


def _row_tile(T):
    return min(T, 512)


def _gate_tile(T):
    return min(T, 256)


def _gate_fwd(a, ln_g, ln_b, w_tril, b_rows):
    T, W2 = a.shape
    W = W2 // 2
    G = W // GROUP_DIM
    tr = _gate_tile(T)

    def body(a_ref, lng_ref, lnb_ref, w_ref, b_ref, z_ref):
        u = _gelu(a_ref[:, :W].astype(F32))
        vg = _gelu(a_ref[:, W:].astype(F32))
        xc = vg - _mean(vg)
        vn = xc * lax.rsqrt(_mean(xc * xc) + EPS)
        vl = (vn * lng_ref[...] + lnb_ref[...]).astype(BF16)
        for n in range(tr // CHUNK):
            rows = slice(n * CHUNK, (n + 1) * CHUNK)
            for g in range(G):
                cols = slice(g * GROUP_DIM, (g + 1) * GROUP_DIM)
                gate = _dot(w_ref[g], vl[rows, cols]) + b_ref[g]
                z_ref[rows, cols] = (u[rows, cols] * gate).astype(BF16)

    vec = pl.BlockSpec((1, W), lambda i: (0, 0))
    grp = pl.BlockSpec((G, CHUNK, CHUNK), lambda i: (0, 0, 0))
    return pl.pallas_call(
        body, name="gate_fwd", grid=(T // tr,),
        in_specs=[pl.BlockSpec((tr, W2), lambda i: (i, 0)), vec, vec, grp, grp],
        out_specs=pl.BlockSpec((tr, W), lambda i: (i, 0)),
        out_shape=jax.ShapeDtypeStruct((T, W), BF16),
        compiler_params=_params(dimension_semantics=("parallel",)),
    )(a, ln_g, ln_b, w_tril, b_rows)


def _gate_bwd(a, dz, ln_g, ln_b, w_tril, w_tril_t, b_rows):
    T, W2 = a.shape
    W = W2 // 2
    G = W // GROUP_DIM
    tr = _gate_tile(T)
    steps = T // tr

    def body(a_ref, dz_ref, lng_ref, lnb_ref, w_ref, wt_ref, b_ref, da_ref, dlng_ref, dlnb_ref, dw_ref, dbs_ref, dvl_ref):
        step = pl.program_id(0)

        @pl.when(step == 0)
        def _():
            dlng_ref[...] = jnp.zeros_like(dlng_ref)
            dlnb_ref[...] = jnp.zeros_like(dlnb_ref)
            dw_ref[...] = jnp.zeros_like(dw_ref)
            dbs_ref[...] = jnp.zeros_like(dbs_ref)

        au = a_ref[:, :W].astype(F32)
        av = a_ref[:, W:].astype(F32)
        u = _gelu(au)
        vg = _gelu(av)
        xc = vg - _mean(vg)
        rstd = lax.rsqrt(_mean(xc * xc) + EPS)
        vn = xc * rstd
        lng = lng_ref[...]
        vl = (vn * lng + lnb_ref[...]).astype(BF16)
        du_scale = dz_ref[...] * _gelu_grad(au)
        dgate_all = dz_ref[...] * u
        for n in range(tr // CHUNK):
            rows = slice(n * CHUNK, (n + 1) * CHUNK)
            for g in range(G):
                cols = slice(g * GROUP_DIM, (g + 1) * GROUP_DIM)
                vlg = vl[rows, cols]
                gate = _dot(w_ref[g], vlg) + b_ref[g]
                da_ref[rows, cols] = (du_scale[rows, cols] * gate).astype(BF16)
                dgate = dgate_all[rows, cols]
                dbs_ref[g] += dgate
                dgate_b = dgate.astype(BF16)
                dw_ref[g] += _dot(dgate_b, vlg, NT)
                dvl_ref[rows, cols] = _dot(wt_ref[g], dgate_b)
        dvl = dvl_ref[...]
        dlnb_ref[...] += jnp.sum(dvl, axis=0, keepdims=True)
        dlng_ref[...] += jnp.sum(dvl * vn, axis=0, keepdims=True)
        dvn = dvl * lng
        dvg = rstd * (dvn - _mean(dvn) - vn * _mean(dvn * vn))
        da_ref[:, W:] = (dvg * _gelu_grad(av)).astype(BF16)

        @pl.when(step == steps - 1)
        def _():
            t_idx = lax.broadcasted_iota(jnp.int32, (CHUNK, CHUNK), 0)
            s_idx = lax.broadcasted_iota(jnp.int32, (CHUNK, CHUNK), 1)
            for g in range(G):
                dw_ref[g] = jnp.where(s_idx <= t_idx, dw_ref[g], 0.0)
                dbs_ref[g] = jnp.broadcast_to(jnp.sum(dbs_ref[g], axis=-1, keepdims=True), (CHUNK, CHUNK))

    vec = pl.BlockSpec((1, W), lambda i: (0, 0))
    grp = pl.BlockSpec((G, CHUNK, CHUNK), lambda i: (0, 0, 0))
    return pl.pallas_call(
        body, name="gate_bwd", grid=(steps,),
        in_specs=[pl.BlockSpec((tr, W2), lambda i: (i, 0)), pl.BlockSpec((tr, W), lambda i: (i, 0)),
                  vec, vec, grp, grp, grp],
        out_specs=[pl.BlockSpec((tr, W2), lambda i: (i, 0)), vec, vec, grp, grp],
        out_shape=[jax.ShapeDtypeStruct((T, W2), BF16), jax.ShapeDtypeStruct((1, W), F32),
                   jax.ShapeDtypeStruct((1, W), F32), jax.ShapeDtypeStruct((G, CHUNK, CHUNK), F32),
                   jax.ShapeDtypeStruct((G, CHUNK, CHUNK), F32)],
        scratch_shapes=[pltpu.VMEM((tr, W), F32)],
        compiler_params=_params(dimension_semantics=("arbitrary",)),
    )(a, dz, ln_g, ln_b, w_tril, w_tril_t, b_rows)


def _bucket_map(dilation):
    rel = BLK + np.arange(BLK)[:, None] - np.arange(2 * BLK)[None, :]
    dist = np.clip(rel, 0, BLK) * dilation
    nf = np.maximum(dist, 1).astype(np.float32)
    large = MAX_EXACT + (np.log(nf / np.float32(MAX_EXACT)) / np.float32(math.log(REL_MAX_DISTANCE / MAX_EXACT))
                         * np.float32(N_BUCKETS - MAX_EXACT)).astype(np.int32)
    large = np.minimum(large, N_BUCKETS - 1)
    return np.where(dist < MAX_EXACT, dist, large).astype(np.int32)


def _bucket_maps():
    return jnp.asarray(np.stack([_bucket_map(d) for d in DILATIONS]))


def _bias_build(rel_bias, buckets):
    NG = len(DILATIONS)

    def body(table_ref, bucket_ref, out_ref):
        for g in range(NG):
            bk = bucket_ref[g]
            for h in range(ATT_HEADS):
                out_ref[0, g, h] = jnp.zeros((BLK, 2 * BLK), F32)
            for b in range(N_BUCKETS):
                hit = bk == b
                for h in range(ATT_HEADS):
                    out_ref[0, g, h] = jnp.where(hit, table_ref[b, g * ATT_HEADS + h], out_ref[0, g, h])
            for h in range(ATT_HEADS):
                for first in range(2):
                    out_ref[first, g, h] = jnp.where(_window_mask(first), out_ref[0, g, h], NEG_INF)

    return pl.pallas_call(
        body, name="bias_build",
        in_specs=[pl.BlockSpec(memory_space=pltpu.SMEM), pl.BlockSpec(memory_space=pltpu.VMEM)],
        out_specs=pl.BlockSpec(memory_space=pltpu.VMEM),
        out_shape=jax.ShapeDtypeStruct((2, NG, ATT_HEADS, BLK, 2 * BLK), F32),
        compiler_params=_params(),
    )(rel_bias, buckets)


def _bias_scatter(dbias, buckets):
    NG = len(DILATIONS)

    def body(dbias_ref, bucket_ref, out_ref):
        for g in range(NG):
            bk = bucket_ref[g]
            for b in range(N_BUCKETS):
                hit = bk == b
                for h in range(ATT_HEADS):
                    out_ref[b, g * ATT_HEADS + h] = jnp.sum(jnp.where(hit, dbias_ref[g, h], 0.0))

    return pl.pallas_call(
        body, name="bias_scatter",
        in_specs=[pl.BlockSpec(memory_space=pltpu.VMEM), pl.BlockSpec(memory_space=pltpu.VMEM)],
        out_specs=pl.BlockSpec(memory_space=pltpu.SMEM),
        out_shape=jax.ShapeDtypeStruct((N_BUCKETS, NG * ATT_HEADS), F32),
        compiler_params=_params(),
    )(dbias, buckets)


def _window_mask(first):
    qi = lax.broadcasted_iota(jnp.int32, (BLK, 2 * BLK), 0)
    kj = lax.broadcasted_iota(jnp.int32, (BLK, 2 * BLK), 1)
    rel = BLK + qi - kj
    return (rel >= 0) & (rel <= BLK) & (kj >= BLK * first)


def _head_lanes(hh):
    lane = lax.broadcasted_iota(jnp.int32, (1, PAIR), 1)
    return (lane >= hh * HEAD_DIM) & (lane < (hh + 1) * HEAD_DIM)


ATT_STEP_BLOCKS = 8


def _attn_steps(stride):
    per_step = math.gcd(stride, ATT_STEP_BLOCKS)
    return per_step, stride // per_step


def _attn_fwd(name, g, qkv, qc, kc, vc, bias, stride):
    T = qkv.shape[0]
    per_step, lag = _attn_steps(stride)
    chained = stride == 1
    if chained:
        per_step, lag = min(ATT_STEP_BLOCKS, T // BLK), 1
    rows = per_step * BLK
    scale = HEAD_DIM ** -0.5

    def body(q_ref, kp_ref, kc_ref, vp_ref, vc_ref, bias_ref, out_ref, *chain):
        step = pl.program_id(0)
        low = _head_lanes(0)
        if chained:
            for cat, before, now in zip(chain, (kp_ref, vp_ref), (kc_ref, vc_ref)):
                cat[:BLK, :] = before[...]
                cat[BLK:, :] = now[...]

        def block(j, carry):
            at = pl.ds(pl.multiple_of(j * BLK, BLK), BLK)
            if chained:
                first = ((step == 0) & (j == 0)).astype(jnp.int32)
                after = pl.ds(pl.multiple_of((j + 1) * BLK, BLK), BLK)
                keys = lambda cols: jnp.concatenate([chain[0][at, cols], chain[0][after, cols]], axis=0)
                values = lambda cols: jnp.concatenate([chain[1][at, cols], chain[1][after, cols]], axis=0)
            else:
                first = (step < lag).astype(jnp.int32)
                keys = lambda cols: jnp.concatenate([kp_ref[at, cols], kc_ref[at, cols]], axis=0)
                values = lambda cols: jnp.concatenate([vp_ref[at, cols], vc_ref[at, cols]], axis=0)
            for hp in range(ATT_HEADS // 2):
                cols = slice(hp * PAIR, (hp + 1) * PAIR)
                qp = q_ref[at, cols]
                kk = keys(cols)
                vv = values(cols)
                o_h, lse_h = [], []
                for hh in range(2):
                    qm = jnp.where(_head_lanes(hh), qp, jnp.zeros_like(qp))
                    s = _dot(qm, kk, NT) * scale
                    logits = s + bias_ref[first, 2 * hp + hh]
                    m = jnp.max(logits, axis=-1, keepdims=True)
                    p = jnp.exp(logits - m)
                    den = jnp.sum(p, axis=-1, keepdims=True)
                    o_h.append(_dot(p.astype(BF16), vv) / den)
                    lse_h.append(m + jnp.log(den))
                out_ref[at, cols] = jnp.where(low, o_h[0], o_h[1])
                out_ref[at, slice(ATT_WIDTH + hp * PAIR, ATT_WIDTH + (hp + 1) * PAIR)] = (
                    jnp.where(low, lse_h[0], lse_h[1]))
            return carry

        lax.fori_loop(0, per_step, block, 0)

    def cur(c):
        return pl.BlockSpec((rows, ATT_WIDTH), lambda s: (s, c))

    def prev(c):
        if chained:
            return pl.BlockSpec((BLK, ATT_WIDTH), lambda s: (jnp.maximum(s * per_step - 1, 0), c))
        return pl.BlockSpec((rows, ATT_WIDTH), lambda s: (jnp.maximum(s - lag, 0), c))

    return pl.pallas_call(
        body, name=name, grid=(T // rows,),
        in_specs=[cur(qc), prev(kc), cur(kc), prev(vc), cur(vc),
                  pl.BlockSpec((2, None, ATT_HEADS, BLK, 2 * BLK), lambda s: (0, g, 0, 0, 0))],
        out_specs=pl.BlockSpec((rows, 2 * ATT_WIDTH), lambda s: (s, 0)),
        out_shape=jax.ShapeDtypeStruct((T, 2 * ATT_WIDTH), F32),
        scratch_shapes=[pltpu.VMEM((rows + BLK, ATT_WIDTH), BF16)] * 2 if chained else [],
        compiler_params=_params(dimension_semantics=("parallel",)),
    )(qkv, qkv, qkv, qkv, qkv, bias)


def _permute_f32(p, x):
    hi = x.astype(BF16)
    rest = x - hi.astype(F32)
    mid = rest.astype(BF16)
    low = (rest - mid.astype(F32)).astype(BF16)
    return _dot(p, hi) + _dot(p, mid) + _dot(p, low)


def _attn_merge(parts):
    T = parts[0].shape[0]
    rows = min(T, REORDER_ROWS)
    n = len(parts)
    width = 2 * PAIR
    ncol = ATT_WIDTH // width

    def body(*refs):
        p_refs, o_refs, l_refs = refs[:n], refs[n:2 * n], refs[2 * n:3 * n]
        o_ref, lse_ref = refs[3 * n], refs[3 * n + 1]

        def positions(ref, g, start):
            d = DILATIONS[g]
            if d == 1:
                return ref[start:start + REORDER_TILE, :]
            span, per = BLK * d, REORDER_TILE // d
            base, t = start // span * span, start % span // REORDER_TILE
            chunks = [ref[base + r * BLK + t * per:base + r * BLK + (t + 1) * per, :] for r in range(d)]
            return _permute_f32(p_refs[g][...], jnp.concatenate(chunks, axis=0))

        for start in range(0, rows, REORDER_TILE):
            ls = [positions(l_refs[g], g, start) for g in range(n)]
            m = functools.reduce(jnp.maximum, ls)
            es = [jnp.exp(l - m) for l in ls]
            tot = functools.reduce(lambda x, y: x + y, es)
            acc = functools.reduce(lambda x, y: x + y, [e * positions(o_refs[g], g, start) for g, e in enumerate(es)])
            o_ref[start:start + REORDER_TILE, :] = (acc / tot).astype(BF16)
            lse_ref[start:start + REORDER_TILE, :] = m + jnp.log(tot)

    matrix = pl.BlockSpec((REORDER_TILE, REORDER_TILE), lambda w, c: (0, 0))
    col = pl.BlockSpec((rows, width), lambda w, c: (w, c))
    col_lse = pl.BlockSpec((rows, width), lambda w, c: (w, ncol + c))
    return pl.pallas_call(
        body, name="attn_merge", grid=(T // rows, ncol),
        in_specs=[matrix] * n + [col] * n + [col_lse] * n, out_specs=[col, col],
        out_shape=[jax.ShapeDtypeStruct((T, ATT_WIDTH), BF16), jax.ShapeDtypeStruct((T, ATT_WIDTH), F32)],
        compiler_params=_params(dimension_semantics=("parallel", "parallel")),
    )(*[_reorder_matrix(max(d, 2), True) for d in DILATIONS], *parts, *parts)


def _attn_bwd(name, g, qkv, qc, kc, vc, do, o, lse, bias, stride, comm=None):
    T = qkv.shape[0]
    per_step, lag = _attn_steps(stride)
    rows = per_step * BLK
    steps = T // rows
    scale = HEAD_DIM ** -0.5
    n_cin = len(comm.inputs) if comm else 0
    n_cout = len(comm.out_shapes) if comm else 0
    assert comm is None or comm.mid is None

    def body(*refs):
        q_ref, kp_ref, kc_ref, vp_ref, vc_ref, do_ref, o_ref, lse_ref, bias_ref = refs[:9]
        comm_in = refs[9:9 + n_cin]
        dq_ref, dkv_ref, db_ref = refs[9 + n_cin:12 + n_cin]
        comm_out = refs[12 + n_cin:12 + n_cin + n_cout]
        carry_k, carry_v = refs[12 + n_cin + n_cout:14 + n_cin + n_cout]
        comm_sems = refs[14 + n_cin + n_cout:]
        step = pl.program_id(0)

        if comm is not None:
            @pl.when(step == 0)
            def _():
                comm.start(comm_in, comm_out, comm_sems)

            @pl.when(step == steps + lag - 1)
            def _():
                comm.end(comm_in, comm_out, comm_sems)

        slot0 = (step % lag) * per_step

        @pl.when(step == 0)
        def _():
            db_ref[...] = jnp.zeros_like(db_ref)
            carry_k[...] = jnp.zeros_like(carry_k)
            carry_v[...] = jnp.zeros_like(carry_v)

        @pl.when(step >= steps)
        def _():
            def flush(j, carry):
                at = pl.ds(pl.multiple_of(j * BLK, BLK), BLK)
                dkv_ref[at, :ATT_WIDTH] = carry_k[slot0 + j].astype(BF16)
                dkv_ref[at, ATT_WIDTH:] = carry_v[slot0 + j].astype(BF16)
                return carry

            lax.fori_loop(0, per_step, flush, 0)

        @pl.when(step < steps)
        def _():
            first = (step < lag).astype(jnp.int32)

            def block(j, carry):
                at = pl.ds(pl.multiple_of(j * BLK, BLK), BLK)
                ck_ref = carry_k.at[slot0 + j]
                cv_ref = carry_v.at[slot0 + j]
                for hp in range(ATT_HEADS // 2):
                    cols = slice(hp * PAIR, (hp + 1) * PAIR)
                    qp = q_ref[at, cols]
                    kk = jnp.concatenate([kp_ref[at, cols], kc_ref[at, cols]], axis=0)
                    vv = jnp.concatenate([vp_ref[at, cols], vc_ref[at, cols]], axis=0)
                    dop = do_ref[at, cols]
                    lsep = lse_ref[at, cols]
                    prod = dop.astype(F32) * o_ref[at, cols].astype(F32)
                    dq = jnp.zeros((BLK, PAIR), F32)
                    dk = jnp.zeros((2 * BLK, PAIR), F32)
                    dv = jnp.zeros((2 * BLK, PAIR), F32)
                    for hh in range(2):
                        lanes = _head_lanes(hh)
                        qm = jnp.where(lanes, qp, jnp.zeros_like(qp))
                        dom = jnp.where(lanes, dop, jnp.zeros_like(dop))
                        km = jnp.where(lanes, kk, jnp.zeros_like(kk))
                        delta = jnp.sum(jnp.where(lanes, prod, 0.0), axis=-1, keepdims=True)
                        lse_h = jnp.max(jnp.where(lanes, lsep, NEG_INF), axis=-1, keepdims=True)
                        s = _dot(qm, kk, NT) * scale
                        logits = s + bias_ref[first, 2 * hp + hh]
                        p = jnp.exp(logits - lse_h)
                        dv += _dot(p.astype(BF16), dom, TN)
                        ds = p * (_dot(dom, vv, NT) - delta)
                        db_ref[2 * hp + hh] += ds
                        dss = (ds * scale).astype(BF16)
                        dq += _dot(dss, km)
                        dk += _dot(dss, qm, TN)
                    dq_ref[at, cols] = dq.astype(BF16)
                    dkv_ref[at, cols] = (ck_ref[:, cols] + dk[:BLK]).astype(BF16)
                    dkv_ref[at, slice(ATT_WIDTH + hp * PAIR, ATT_WIDTH + (hp + 1) * PAIR)] = (
                        cv_ref[:, cols] + dv[:BLK]).astype(BF16)
                    ck_ref[:, cols] = dk[BLK:]
                    cv_ref[:, cols] = dv[BLK:]
                return carry

            lax.fori_loop(0, per_step, block, 0)

    last = steps - 1

    def cur(c):
        return pl.BlockSpec((rows, ATT_WIDTH), lambda s: (jnp.minimum(s, last), c))

    def prev(c):
        return pl.BlockSpec((rows, ATT_WIDTH), lambda s: (jnp.clip(s - lag, 0, last), c))

    dbias_shape = (ATT_HEADS, BLK, 2 * BLK)
    res = pl.pallas_call(
        body, name=name, grid=(steps + lag,),
        in_specs=[cur(qc), prev(kc), cur(kc), prev(vc), cur(vc), cur(0), cur(0), cur(0),
                  pl.BlockSpec((2, None, ATT_HEADS, BLK, 2 * BLK), lambda s: (0, g, 0, 0, 0))] + [ANY] * n_cin,
        out_specs=[cur(0), pl.BlockSpec((rows, 2 * ATT_WIDTH), lambda s: (jnp.clip(s - lag, 0, last), 0)),
                   pl.BlockSpec(dbias_shape, lambda s: (0, 0, 0))] + [ANY] * n_cout,
        out_shape=[jax.ShapeDtypeStruct((T, ATT_WIDTH), BF16), jax.ShapeDtypeStruct((T, 2 * ATT_WIDTH), BF16),
                   jax.ShapeDtypeStruct(dbias_shape, F32)] + (comm.out_shapes if comm else []),
        scratch_shapes=[pltpu.VMEM((stride, BLK, ATT_WIDTH), F32), pltpu.VMEM((stride, BLK, ATT_WIDTH), F32)]
        + (comm.scratch if comm else []),
        compiler_params=_params(dimension_semantics=("arbitrary",)),
    )(qkv, qkv, qkv, qkv, qkv, do, o, lse, bias, *(comm.inputs if comm else []))
    return res[0], res[1], res[2], list(res[3:])


REORDER_TILE = 256
REORDER_ROWS = 2048


def _reorder_matrix(d, inverse):
    per = REORDER_TILE // d
    p = np.zeros((REORDER_TILE, REORDER_TILE), np.float32)
    for src in range(REORDER_TILE):
        i, r = divmod(src, d)
        p[r * per + i, src] = 1.0
    return jnp.asarray(p.T if inverse else p, dtype=BF16)


def _reorder_rows(name, src, d, inverse, *, src_col=0, col_stride=1, ncols=1, dst=None, dst_col=0, dst_stride=1,
                  dst_blocks=None):
    T = src.shape[0]
    dtype = src.dtype
    span = BLK * d
    rows = max(span, min(T, REORDER_ROWS))
    per = REORDER_TILE // d
    tiles = span // REORDER_TILE
    dst_blocks = ncols if dst_blocks is None else dst_blocks

    def apply(p, x):
        return _dot(p, x).astype(BF16) if dtype == BF16 else _permute_f32(p, x)

    def body(*refs):
        p_ref, x_ref, o_ref = refs[0], refs[1], refs[-1]
        if d == 1:
            o_ref[...] = x_ref[...]
            return
        for s in range(rows // span):
            for t in range(tiles):
                base = s * span
                tile_rows = slice(base + t * REORDER_TILE, base + (t + 1) * REORDER_TILE)
                chunk = lambda r: slice(base + r * BLK + t * per, base + r * BLK + (t + 1) * per)
                if inverse:
                    gathered = jnp.concatenate([x_ref[chunk(r), :] for r in range(d)], axis=0)
                    o_ref[tile_rows, :] = apply(p_ref[...], gathered)
                else:
                    y = apply(p_ref[...], x_ref[tile_rows, :])
                    for r in range(d):
                        o_ref[chunk(r), :] = y[r * per:(r + 1) * per]

    in_specs = [pl.BlockSpec((REORDER_TILE, REORDER_TILE), lambda w, k: (0, 0)),
                pl.BlockSpec((rows, ATT_WIDTH), lambda w, k: (w, src_col + col_stride * k))]
    operands = [_reorder_matrix(max(d, 2), inverse), src]
    aliases = {}
    if dst is not None:
        in_specs.append(ANY)
        operands.append(dst)
        aliases = {2: 0}
    return pl.pallas_call(
        body, name=name, grid=(T // rows, ncols), in_specs=in_specs,
        out_specs=pl.BlockSpec((rows, ATT_WIDTH), lambda w, k: (w, dst_col + dst_stride * k)),
        out_shape=jax.ShapeDtypeStruct((T, dst_blocks * ATT_WIDTH), dtype),
        input_output_aliases=aliases,
        compiler_params=_params(dimension_semantics=("parallel", "parallel")),
    )(*operands)


def _group_qkv(qkv, g, d):
    NG = len(DILATIONS)
    if d == 1:
        return qkv, (g, NG + g, 2 * NG + g)
    return _reorder_rows(f"qkv_to_residues{g}", qkv, d, False, src_col=g, col_stride=NG, ncols=3), (0, 1, 2)


def _attention_fwd(qkv, bias):
    T = qkv.shape[0]
    parts = []
    for g, d in enumerate(DILATIONS):
        src, (qc, kc, vc) = _group_qkv(qkv, g, d)
        parts.append(_attn_fwd(f"attn_fwd_{g}", g, src, qc, kc, vc, bias, d))
    return _attn_merge(parts)


def _attention_bwd(qkv, do, o, lse, bias, comms):
    NG = len(DILATIONS)
    dqkv, dbs, carried = None, [], []
    for g, d in enumerate(DILATIONS):
        src, (qc, kc, vc) = _group_qkv(qkv, g, d)
        do_g, o_g, lse_g = do, o, lse
        if d > 1:
            do_g = _reorder_rows(f"do_to_residues{g}", do, d, False)
            o_g = _reorder_rows(f"o_to_residues{g}", o, d, False)
            lse_g = _reorder_rows(f"lse_to_residues{g}", lse, d, False)
        dq, dkv, db, sent = _attn_bwd(f"attn_bwd_{g}", g, src, qc, kc, vc, do_g, o_g, lse_g, bias, d, comm=comms[g])
        dqkv = _reorder_rows(f"dq_to_positions{g}", dq, d, True, dst=dqkv, dst_col=g, dst_blocks=3 * NG)
        dqkv = _reorder_rows(f"dkv_to_positions{g}", dkv, d, True, ncols=2, dst=dqkv, dst_col=NG + g, dst_stride=NG,
                             dst_blocks=3 * NG)
        dbs.append(db)
        carried.append(sent)
    return dqkv, jnp.stack(dbs), carried


def _other_chips(x, y):
    return [(1 - x, y), (x, 1 - y), (1 - x, 1 - y)]


def _shard_region(ref, shape, by_cols, chip, rows=None):
    R, C = shape
    start, size = (0, R) if rows is None else rows
    if by_cols:
        return ref.at[pl.ds(start, size), pl.ds(chip * C, C)]
    return ref.at[pl.ds(chip * R + start, size), :]


def _gather_weights(entries):
    n = len(entries)
    shapes = [e[0].shape[1:] for e in entries]

    def places(ins, outs, sems):
        send_sems, recv_sems, local_sems = sems
        x, y, c = lax.axis_index("x"), lax.axis_index("y"), lax.axis_index("c")

        def landing(f, px, py, pc):
            R = shapes[f][0]
            return _shard_region(outs[f], shapes[f], entries[f][2], 2 * px + py, rows=(pc * (R // 2), R // 2))

        def copy(f, k, block, to, src=None):
            dst = landing(f, *block)
            return pltpu.make_async_remote_copy(
                src_ref=dst if src is None else src, dst_ref=dst,
                send_sem=send_sems.at[6 * f + k], recv_sem=recv_sems.at[6 * f + k],
                device_id=to, device_id_type=MESH)

        def mine(f):
            dst = _shard_region(outs[f], shapes[f], entries[f][2], 2 * x + y)
            return pltpu.make_async_copy(ins[f].at[entries[f][1]], dst, local_sems.at[f])

        def first(f, j):
            R = shapes[f][0]
            src = ins[f].at[entries[f][1], pl.ds(c * (R // 2), R // 2), :]
            return copy(f, j, (x, y, c), (*_other_chips(x, y)[j], c), src=src)

        return x, y, c, copy, mine, first

    def start(ins, outs, sems):
        _, _, _, _, mine, first = places(ins, outs, sems)
        for f in range(n):
            mine(f).start()
        for j in range(3):
            for f in range(n):
                first(f, j).start()

    def mid(ins, outs, sems):
        x, y, c, copy, _, _ = places(ins, outs, sems)
        for j, chip in enumerate(_other_chips(x, y)):
            for f in range(n):
                copy(f, j, (*chip, c), (x, y, c)).wait_recv()
                copy(f, 3 + j, (*chip, c), (x, y, 1 - c)).start()

    def end(ins, outs, sems):
        x, y, c, copy, mine, first = places(ins, outs, sems)
        for j, chip in enumerate(_other_chips(x, y)):
            for f in range(n):
                copy(f, 3 + j, (*chip, 1 - c), (x, y, c)).wait_recv()
        for j, chip in enumerate(_other_chips(x, y)):
            for f in range(n):
                first(f, j).wait_send()
                copy(f, 3 + j, (*chip, c), (x, y, 1 - c)).wait_send()
        for f in range(n):
            mine(f).wait()

    def whole(f):
        R, C = shapes[f]
        return (R, N_CHIPS * C) if entries[f][2] else (N_CHIPS * R, C)

    return _Comm(
        [e[0] for e in entries], [jax.ShapeDtypeStruct(whole(f), BF16) for f in range(n)],
        [pltpu.SemaphoreType.DMA((6 * n,)), pltpu.SemaphoreType.DMA((6 * n,)), pltpu.SemaphoreType.DMA((n,))],
        start, end, mid)


def _scatter_grads(entries):
    n = len(entries)

    def copies(ins, outs, sems):
        send_sems, recv_sems, local_sems = sems
        x, y, c = lax.axis_index("x"), lax.axis_index("y"), lax.axis_index("c")
        me = 2 * x + y

        def piece(f, chip):
            return _shard_region(ins[f], entries[f][1], entries[f][2], chip)

        mine = [pltpu.make_async_copy(piece(f, me), outs[f].at[me], local_sems.at[f]) for f in range(n)]
        sends = [pltpu.make_async_remote_copy(
            src_ref=piece(f, 2 * px + py), dst_ref=outs[f].at[me],
            send_sem=send_sems.at[3 * f + j], recv_sem=recv_sems.at[3 * f + j],
            device_id=(px, py, c), device_id_type=MESH)
            for j, (px, py) in enumerate(_other_chips(x, y)) for f in range(n)]
        return mine, sends

    def start(ins, outs, sems):
        mine, sends = copies(ins, outs, sems)
        for cp in mine + sends:
            cp.start()

    def end(ins, outs, sems):
        mine, sends = copies(ins, outs, sems)
        for cp in sends + mine:
            cp.wait()

    return _Comm(
        [e[0] for e in entries], [jax.ShapeDtypeStruct((N_CHIPS,) + tuple(e[1]), BF16) for e in entries],
        [pltpu.SemaphoreType.DMA((3 * n,)), pltpu.SemaphoreType.DMA((3 * n,)), pltpu.SemaphoreType.DMA((n,))],
        start, end)


def _exchange_sibling(parts):
    n = len(parts)

    def copies(ins, outs, sems):
        send_sems, recv_sems = sems
        sibling = (lax.axis_index("x"), lax.axis_index("y"), 1 - lax.axis_index("c"))
        return [pltpu.make_async_remote_copy(src_ref=ins[i], dst_ref=outs[i], send_sem=send_sems.at[i],
                                             recv_sem=recv_sems.at[i], device_id=sibling, device_id_type=MESH)
                for i in range(n)]

    def start(ins, outs, sems):
        for cp in copies(ins, outs, sems):
            cp.start()

    def end(ins, outs, sems):
        for cp in copies(ins, outs, sems):
            cp.wait()

    return _Comm(parts, [jax.ShapeDtypeStruct(s.shape, s.dtype) for s in parts],
                 [pltpu.SemaphoreType.DMA((n,)), pltpu.SemaphoreType.DMA((n,))], start, end)


def _allgather_small(block):
    m_per, ncol = block.shape

    def places(ins, outs, sems):
        send_sems, recv_sems, local_sem = sems
        x, y, c = lax.axis_index("x"), lax.axis_index("y"), lax.axis_index("c")

        def rows(px, py, pc):
            return outs[0].at[4 * px + 2 * py + pc]

        def copy(k, block_of, to, src=None):
            return pltpu.make_async_remote_copy(
                src_ref=rows(*block_of) if src is None else src, dst_ref=rows(*block_of),
                send_sem=send_sems.at[k], recv_sem=recv_sems.at[k], device_id=to, device_id_type=MESH)

        mine = pltpu.make_async_copy(ins[0], rows(x, y, c), local_sem.at[0])
        first = [copy(0, (x, y, c), (x, y, 1 - c), src=ins[0])]
        first += [copy(1 + j, (x, y, c), (*chip, c), src=ins[0]) for j, chip in enumerate(_other_chips(x, y))]
        passed = [copy(4 + j, (*chip, c), (x, y, 1 - c)) for j, chip in enumerate(_other_chips(x, y))]
        return x, y, c, copy, mine, first, passed

    def start(ins, outs, sems):
        _, _, _, _, mine, first, _ = places(ins, outs, sems)
        for cp in [mine] + first:
            cp.start()

    def mid(ins, outs, sems):
        x, y, c, copy, _, _, passed = places(ins, outs, sems)
        for j, chip in enumerate(_other_chips(x, y)):
            copy(1 + j, (*chip, c), (x, y, c)).wait_recv()
            passed[j].start()

    def end(ins, outs, sems):
        x, y, c, copy, mine, first, passed = places(ins, outs, sems)
        copy(0, (x, y, 1 - c), (x, y, c)).wait_recv()
        for j, chip in enumerate(_other_chips(x, y)):
            copy(4 + j, (*chip, 1 - c), (x, y, c)).wait_recv()
        for cp in first + passed:
            cp.wait_send()
        mine.wait()

    return _Comm([block], [jax.ShapeDtypeStruct((N_DEV, m_per, ncol), block.dtype)],
                 [pltpu.SemaphoreType.DMA((7,)), pltpu.SemaphoreType.DMA((7,)), pltpu.SemaphoreType.DMA((1,))],
                 start, end, mid)


def _join_comms(*progs):
    def split(parts, counts):
        out, pos = [], 0
        for n in counts:
            out.append(parts[pos:pos + n])
            pos += n
        return out

    def phase(which):
        def run(ins, outs, sems):
            args = zip(split(ins, [len(p.inputs) for p in progs]), split(outs, [len(p.out_shapes) for p in progs]),
                       split(sems, [len(p.scratch) for p in progs]))
            for p, (i, o, s) in zip(progs, args):
                fn = getattr(p, which)
                if fn is not None:
                    fn(i, o, s)
        return run

    return _Comm([a for p in progs for a in p.inputs], [s for p in progs for s in p.out_shapes],
                 [s for p in progs for s in p.scratch], phase("start"), phase("end"), phase("mid"))


def _adamw(w, g, m, v):
    m = ADAM_B1 * m + (1.0 - ADAM_B1) * g
    v = ADAM_B2 * v + (1.0 - ADAM_B2) * jnp.square(g)
    m_hat = m / (1.0 - ADAM_B1 ** ADAM_STEP)
    v_hat = v / (1.0 - ADAM_B2 ** ADAM_STEP)
    delta = -ADAM_LR * (m_hat / (jnp.sqrt(v_hat) + ADAM_EPS) + ADAM_WD * w)
    return delta, m, v


def _flat_tile(rows):
    return min(rows, 512)


def _sum_pieces(name, layers):
    L = len(layers)
    P, R, C = layers[0].shape
    tr = _flat_tile(R)

    def body(*refs):
        out_ref = refs[L]
        for l in range(L):
            @pl.when(pl.program_id(0) == l)
            def _(p_ref=refs[l]):
                acc = p_ref[0].astype(F32)
                for j in range(1, P):
                    acc = acc + p_ref[j].astype(F32)
                out_ref[...] = acc

    return pl.pallas_call(
        body, name=name, grid=(L, R // tr),
        in_specs=[pl.BlockSpec((P, tr, C), lambda l, i: (0, i, 0)) for _ in range(L)],
        out_specs=pl.BlockSpec((None, tr, C), lambda l, i: (l, i, 0)),
        out_shape=jax.ShapeDtypeStruct((L, R, C), F32),
        compiler_params=_params(dimension_semantics=("parallel", "parallel")),
    )(*layers)


def _adam_pair(name, w, m, v, part_a, part_b):
    L, R, C = w.shape
    tr = _flat_tile(R)

    def body(w_ref, m_ref, v_ref, a_ref, b_ref, g_ref, d_ref, nm_ref, nv_ref):
        g = a_ref[...] + b_ref[...]
        g_ref[...] = g
        d_ref[...], nm_ref[...], nv_ref[...] = _adamw(w_ref[...], g, m_ref[...], v_ref[...])

    row = pl.BlockSpec((None, tr, C), lambda l, i: (l, i, 0))
    return pl.pallas_call(
        body, name=name, grid=(L, R // tr),
        in_specs=[row] * 5, out_specs=[row] * 4,
        out_shape=[jax.ShapeDtypeStruct((L, R, C), F32)] * 4,
        compiler_params=_params(dimension_semantics=("parallel", "parallel")),
    )(w, m, v, part_a, part_b)


def _adam_small(w, m, v, gathered):
    R, C = w.shape

    def body(w_ref, m_ref, v_ref, p_ref, g_ref, d_ref, nm_ref, nv_ref):
        g = p_ref[0]
        for j in range(1, N_DEV):
            g = g + p_ref[j]
        g_ref[...] = g
        d_ref[...], nm_ref[...], nv_ref[...] = _adamw(w_ref[...], g, m_ref[...], v_ref[...])

    return pl.pallas_call(
        body, name="adam_small",
        out_shape=[jax.ShapeDtypeStruct((R, C), F32)] * 4,
        compiler_params=_params(),
    )(w, m, v, gathered)


SMALL = ("mix_norm_g", "mlp_norm_g", "final_norm_g", "a_ln_g", "a_ln_b", "a_w_s", "a_b_s", "rel_bias")


def _pack_small(arrays, width):
    rows = []
    for a in arrays:
        flat = a.reshape(-1)
        pad = (-flat.shape[0]) % width
        rows.append(jnp.pad(flat, (0, pad)).reshape(-1, width))
    block = jnp.concatenate(rows, axis=0)
    return jnp.pad(block, ((0, (-block.shape[0]) % 8), (0, 0)))


def _unpack_small(block, shapes, width):
    out, row = [], 0
    for shape in shapes:
        size = int(np.prod(shape))
        nrows = -(-size // width)
        out.append(block[row:row + nrows].reshape(-1)[:size].reshape(shape))
        row += nrows
    return out


def kernel(x, mix_norm_g, mlp_norm_g, final_norm_g, a_w_in, a_ln_g, a_ln_b, a_w_s, a_b_s, a_w_out, b_w_qkv, b_w_out, rel_bias, w_up, w_down, loss_target, m_mix_norm_g, m_mlp_norm_g, m_final_norm_g, m_a_w_in, m_a_ln_g, m_a_ln_b, m_a_w_s, m_a_b_s, m_a_w_out, m_b_w_qkv, m_b_w_out, m_rel_bias, m_w_up, m_w_down, v_mix_norm_g, v_mlp_norm_g, v_final_norm_g, v_a_w_in, v_a_ln_g, v_a_ln_b, v_a_w_s, v_a_b_s, v_a_w_out, v_b_w_qkv, v_b_w_out, v_rel_bias, v_w_up, v_w_down):
    T, D = x.shape[1], x.shape[2]
    h0 = x.reshape(T, D)
    target = loss_target.reshape(T, D)
    G = a_w_s.shape[1]

    w_big = [a_w_in, a_w_out, b_w_qkv, b_w_out, w_up, w_down]
    m_big = [m_a_w_in, m_a_w_out, m_b_w_qkv, m_b_w_out, m_w_up, m_w_down]
    v_big = [v_a_w_in, v_a_w_out, v_b_w_qkv, v_b_w_out, v_w_up, v_w_down]
    by_cols = [True, False, True, True, True, False]
    s_in, s_out, s_qkv, s_bo, s_up, s_dn = [w.astype(BF16) for w in w_big]
    W_in, W_out = _run_comm("gather_a", _gather_weights([(s_in, 0, True), (s_out, 0, False)]))

    tril = jnp.tril(jnp.ones((CHUNK, CHUNK), dtype=bool))
    w_tril = jnp.where(tril[None], a_w_s[0], 0.0).astype(BF16)
    w_tril_t = jnp.swapaxes(w_tril, 1, 2)
    b_rows = jnp.broadcast_to(a_b_s[0][:, :, None], (G, CHUNK, CHUNK))
    buckets = _bucket_maps()
    bias = _bias_build(rel_bias, buckets)

    QKV = s_qkv.shape[2] * N_CHIPS
    TM = 1024
    TK_WGRAD = 4096

    def matmul(name, a, b, mode, out, tm=TM, tn=1024, **kw):
        outs = out if isinstance(out, list) else [out]
        return _mm(name, a, b, mode, tm=tm, tn=tn, tk=a.shape[1], outs=outs, **kw)

    def norm_bwd(layer_gain, h, dres, copies=2):
        return dict(epi=_epi_rms_bwd(copies), extras=(h, dres), vecs=(layer_gain,), col_sums=1)

    def wgrad(name, a, b, tn=1024, tk=TK_WGRAD, comm=None):
        return _mm(name, a, b, "tn", tm=1024, tn=tn, tk=tk, outs=[BF16], comm=comm)

    def scatter(*which):
        return _scatter_grads([(g, w_big[i].shape[1:], by_cols[i]) for g, i in which])

    (a_pre, y0), (W_up0,) = matmul("a_in", h0, W_in, "nn", BF16, norm_gain=mix_norm_g[0:1],
                                   comm=_gather_weights([(s_up, 0, True)]))
    z = _gate_fwd(a_pre, a_ln_g, a_ln_b, w_tril, b_rows)
    h1 = matmul("a_out", z, W_out, "nn", F32, epi=_epi_residual, extras=(h0,))
    (q1, y1), (W_dn0,) = matmul("mlp_up0", h1, W_up0, "nn", BF16, epi=_epi_relu2, norm_gain=mlp_norm_g[0:1],
                                comm=_gather_weights([(s_dn, 0, False)]))
    h2, (W_qkv, W_bo) = matmul("mlp_down0", q1, W_dn0, "nn", F32, tm=TM // 2, epi=_epi_residual, extras=(h1,),
                               comm=_gather_weights([(s_qkv, 0, True), (s_bo, 0, True)]))
    (qkv, y2), (W_up1,) = matmul("b_qkv", h2, W_qkv, "nn", BF16, tn=QKV // 4, norm_gain=mix_norm_g[1:2],
                                 comm=_gather_weights([(s_up, 1, True)]))
    o, lse = _attention_fwd(qkv, bias)
    h3 = matmul("b_out", o, W_bo, "nn", F32, epi=_epi_residual, extras=(h2,))
    (q3, y3), (W_dn1,) = matmul("mlp_up1", h3, W_up1, "nn", BF16, epi=_epi_relu2, norm_gain=mlp_norm_g[1:2],
                                comm=_gather_weights([(s_dn, 1, False)]))
    dh4, dh4_b, d_final_g, loss_row = matmul("mlp_down1", q3, W_dn1, "nn", [F32, BF16], tm=TM // 2, epi=_epi_loss_head,
                                             extras=(h3, target), vecs=(final_norm_g.reshape(1, D),), col_sums=2)

    dp3 = matmul("mlp_down_bwd1", dh4_b, W_dn1, "nt", BF16, epi=_epi_relu2_grad, extras=(q3,))
    g_dn1 = wgrad("mlp_down_wgrad1", q3, dh4_b)
    g_up1 = wgrad("mlp_up_wgrad1", y3, dp3)
    dh3, dh3_b, dg_mlp1 = matmul("mlp_up_bwd1", dp3, W_up1, "nt", [F32, BF16], tm=TM // 2,
                                 **norm_bwd(mlp_norm_g[1:2], h3, dh4))
    do = matmul("b_out_bwd", dh3_b, W_bo, "nt", BF16)
    g_bo = wgrad("b_out_wgrad", o, dh3_b)
    dqkv, dbias, ((r_dn1,), (r_up1,), (r_bo,)) = _attention_bwd(
        qkv, do, o, lse, bias, [scatter((g_dn1, 5)), scatter((g_up1, 4)), scatter((g_bo, 3))])
    d_rel_bias = _bias_scatter(dbias, buckets)
    dh2, dh2_b, dg_mix1 = matmul("b_qkv_bwd", dqkv, W_qkv, "nt", [F32, BF16], tm=TM // 2,
                                 **norm_bwd(mix_norm_g[1:2], h2, dh3))
    g_qkv = wgrad("b_qkv_wgrad", y2, dqkv, tn=QKV // 4)
    dp1, (r_qkv,) = matmul("mlp_down_bwd0", dh2_b, W_dn0, "nt", BF16, epi=_epi_relu2_grad, extras=(q1,),
                           comm=scatter((g_qkv, 2)))
    g_up0 = wgrad("mlp_up_wgrad0", y1, dp1)
    g_dn0, (r_up0,) = wgrad("mlp_down_wgrad0", q1, dh2_b, comm=scatter((g_up0, 4)))
    (dh1, dh1_b, dg_mlp0), (r_dn0,) = matmul("mlp_up_bwd0", dp1, W_up0, "nt", [F32, BF16], tm=TM // 2,
                                             comm=scatter((g_dn0, 5)), **norm_bwd(mlp_norm_g[0:1], h1, dh2))
    dz = matmul("a_out_bwd", dh1_b, W_out, "nt", F32)
    g_out = wgrad("a_out_wgrad", z, dh1_b)
    da, d_ln_g, d_ln_b, d_w_s, d_b_s = _gate_bwd(a_pre, dz, a_ln_g, a_ln_b, w_tril, w_tril_t, b_rows)
    g_in, (r_out,) = wgrad("a_in_wgrad", y0, da, comm=scatter((g_out, 1)))
    grad_x, dg_mix0 = matmul("a_in_bwd", da, W_in, "nt", F32, **norm_bwd(mix_norm_g[0:1], h0, dh1, copies=1))

    unused = jnp.zeros((1, 1), F32)
    small_w = [mix_norm_g, mlp_norm_g, final_norm_g, a_ln_g, a_ln_b, a_w_s, a_b_s, rel_bias, unused]
    small_m = [m_mix_norm_g, m_mlp_norm_g, m_final_norm_g, m_a_ln_g, m_a_ln_b, m_a_w_s, m_a_b_s, m_rel_bias, unused]
    small_v = [v_mix_norm_g, v_mlp_norm_g, v_final_norm_g, v_a_ln_g, v_a_ln_b, v_a_w_s, v_a_b_s, v_rel_bias, unused]
    small_g = [jnp.concatenate([dg_mix0, dg_mix1]), jnp.concatenate([dg_mlp0, dg_mlp1]), d_final_g,
               d_ln_g, d_ln_b, d_w_s[None], d_b_s[None, :, :, 0], d_rel_bias, loss_row[:, :1]]
    width = max(D, 128)
    received = [None, [r_out], [r_qkv], [r_bo], [r_up0, r_up1], [r_dn0, r_dn1]]
    plane = [None] + [_sum_pieces(f"sum_pieces{i}", received[i]) for i in range(1, len(w_big))]
    tail = _run_comm("tail_comm", _join_comms(scatter((g_in, 0)), _exchange_sibling(plane[1:]),
                                              _allgather_small(_pack_small(small_g, width))))
    r_in, other, gathered_small = tail[0], [None] + list(tail[1:len(w_big)]), tail[len(w_big)]
    plane[0] = _sum_pieces("sum_pieces0", [r_in])
    (other[0],) = _run_comm("exchange_a_in", _exchange_sibling([plane[0]]))
    big_out = [_adam_pair(f"adam{i}", w_big[i], m_big[i], v_big[i], plane[i], other[i]) for i in range(len(w_big))]

    def unbig(kind):
        return dict(zip(["a_w_in", "a_w_out", "b_w_qkv", "b_w_out", "w_up", "w_down"], [b[kind] for b in big_out]))

    small_out = _adam_small(_pack_small(small_w, width), _pack_small(small_m, width), _pack_small(small_v, width),
                            gathered_small)
    shapes = [w.shape for w in small_w]
    loss = _unpack_small(small_out[0], shapes, width)[-1][0, 0]

    names = ["mix_norm_g", "mlp_norm_g", "final_norm_g", "a_w_in", "a_ln_g", "a_ln_b", "a_w_s", "a_b_s", "a_w_out",
             "b_w_qkv", "b_w_out", "rel_bias", "w_up", "w_down"]
    results = [loss, grad_x.reshape(x.shape)]
    for kind in range(4):
        table = dict(zip(SMALL, _unpack_small(small_out[kind], shapes, width)))
        table.update(unbig(kind))
        results += [table[n] for n in names]
    return tuple(results)
```

```python
import functools
import math

import numpy as np
import jax
import jax.numpy as jnp
from jax import lax
from jax.experimental import pallas as pl
from jax.experimental.pallas import tpu as pltpu

F32 = jnp.float32
BF16 = jnp.bfloat16
MESH = pl.DeviceIdType.MESH
ANY = pl.BlockSpec(memory_space=pl.ANY)

N_CHIPS = 4
N_DEV = 8
VMEM_LIMIT_BYTES = 56 * 1024 * 1024

EPS = 1e-6
NEG_INF = -1e30
CHUNK = 128
GROUP_DIM = 128
HEAD_DIM = 64
ATT_HEADS = 8
ATT_WIDTH = ATT_HEADS * HEAD_DIM
PAIR = 2 * HEAD_DIM
BLK = 128
DILATIONS = (1, 4, 16)
N_BUCKETS = 32
MAX_EXACT = N_BUCKETS // 2
REL_MAX_DISTANCE = 2048

ADAM_LR = 0.001
ADAM_B1 = 0.9
ADAM_B2 = 0.999
ADAM_EPS = 1e-08
ADAM_WD = 0.01
ADAM_STEP = 10

NN = (((1,), (0,)), ((), ()))
NT = (((1,), (1,)), ((), ()))
TN = (((0,), (0,)), ((), ()))


def _params(**kw):
    return pltpu.CompilerParams(vmem_limit_bytes=VMEM_LIMIT_BYTES, **kw)


def _dot(a, b, dims=NN):
    return lax.dot_general(a, b, dims, preferred_element_type=F32)


def _gelu(x):
    return 0.5 * x * (1.0 + lax.erf(x * math.sqrt(0.5)))


def _gelu_grad(x):
    return 0.5 * (1.0 + lax.erf(x * math.sqrt(0.5))) + x * jnp.exp(-0.5 * x * x) * (1.0 / math.sqrt(2.0 * math.pi))


def _mean(x):
    return jnp.mean(x, axis=-1, keepdims=True)


class _Comm:
    def __init__(self, inputs, out_shapes, scratch, start, end, mid=None):
        self.inputs, self.out_shapes, self.scratch = list(inputs), list(out_shapes), list(scratch)
        self.start, self.mid, self.end = start, mid, end


def _run_comm(name, comm):
    n_in, n_out = len(comm.inputs), len(comm.out_shapes)

    def body(*refs):
        parts = refs[:n_in], refs[n_in:n_in + n_out], refs[n_in + n_out:]
        comm.start(*parts)
        if comm.mid is not None:
            comm.mid(*parts)
        comm.end(*parts)

    return pl.pallas_call(
        body, name=name, in_specs=[ANY] * n_in, out_specs=[ANY] * n_out, out_shape=comm.out_shapes,
        scratch_shapes=comm.scratch, compiler_params=_params(),
    )(*comm.inputs)


def _mm(name, a, b, mode, *, tm, tn, tk, outs, epi=None, extras=(), vecs=(), col_sums=0, norm_gain=None, comm=None):
    if mode == "tn":
        K, M = a.shape
    else:
        M, K = a.shape
    N = b.shape[0] if mode == "nt" else b.shape[1]
    tm, tn, tk = min(tm, M), min(tn, N), min(tk, K)
    assert M % tm == 0 and N % tn == 0 and K % tk == 0, (name, M, N, K, tm, tn, tk)
    nk = K // tk
    grid = (M // tm, N // tn, nk)

    if mode == "tn":
        a_spec = pl.BlockSpec((tk, tm), lambda i, j, k: (k, i))
    else:
        a_spec = pl.BlockSpec((tm, tk), lambda i, j, k: (i, k))
    if mode == "nt":
        b_spec = pl.BlockSpec((tn, tk), lambda i, j, k: (j, k))
    else:
        b_spec = pl.BlockSpec((tk, tn), lambda i, j, k: (k, j))
    tile = pl.BlockSpec((tm, tn), lambda i, j, k: (i, j))
    vec = pl.BlockSpec((1, tn), lambda i, j, k: (0, j))
    normed = norm_gain is not None
    assert not normed or (mode == "nn" and nk == 1 and tm % grid[1] == 0)
    assert col_sums == 0 or grid[1] == 1
    out_shapes = [jax.ShapeDtypeStruct((M, N), dtype) for dtype in outs]
    out_specs = [tile for _ in outs]
    extra_specs = [tile for _ in extras] + [vec for _ in vecs]
    if normed:
        part_rows = tm // grid[1]
        last_part = M // part_rows - 1
        a_spec = pl.BlockSpec((tm, K), lambda i, j, k: (0, 0))
        out_shapes.append(jax.ShapeDtypeStruct((M, K), BF16))
        out_specs.append(pl.BlockSpec((part_rows, K), lambda i, j, k: (i * grid[1] + j, 0)))
        extra_specs.append(pl.BlockSpec((1, K), lambda i, j, k: (0, 0)))
        extra_specs.append(pl.BlockSpec((part_rows, K),
                                        lambda i, j, k: (jnp.minimum((i + 1) * grid[1] + j, last_part), 0)))
    out_shapes += [jax.ShapeDtypeStruct((1, N), F32)] * col_sums
    out_specs += [vec] * col_sums
    n_extra, n_out = len(extra_specs), len(out_shapes)
    n_tiles = len(outs)
    n_cin = len(comm.inputs) if comm else 0
    n_cout = len(comm.out_shapes) if comm else 0
    dims = {"nn": NN, "nt": NT, "tn": TN}[mode]
    steps = grid[0] * grid[1] * grid[2]

    def body(*refs):
        a_ref, b_ref = refs[0], refs[1]
        pos = 2
        extra_refs = refs[pos:pos + n_extra]
        pos += n_extra
        comm_in = refs[pos:pos + n_cin]
        pos += n_cin
        out_refs = refs[pos:pos + n_out]
        pos += n_out
        comm_out = refs[pos:pos + n_cout]
        pos += n_cout
        acc_ref = refs[pos] if nk > 1 else None
        pos += nk > 1
        y_refs = refs[pos:pos + 2 * normed]
        comm_sems = refs[pos + 2 * normed:]
        k = pl.program_id(2)
        step = (pl.program_id(0) * grid[1] + pl.program_id(1)) * nk + k

        if comm is not None:
            @pl.when(step == 0)
            def _():
                comm.start(comm_in, comm_out, comm_sems)

        def finish(acc):
            epi_args = [e[...] for e in extra_refs[:n_extra - 2 * normed]]
            res = epi(acc, *epi_args) if epi is not None else (acc,) * n_tiles
            for o, r in zip(out_refs[:n_tiles], res[:n_tiles]):
                o[...] = r.astype(o.dtype)
            if col_sums:
                sums = out_refs[n_out - col_sums:]

                @pl.when(pl.program_id(0) == 0)
                def _():
                    for o in sums:
                        o[...] = jnp.zeros_like(o)

                for o, r in zip(sums, res[n_tiles:]):
                    o[...] += r

        if normed:
            gain_ref, ahead_ref = extra_refs[-2], extra_refs[-1]

            def norm(hv):
                return (hv * lax.rsqrt(_mean(hv * hv) + EPS) * gain_ref[...]).astype(BF16)

            @pl.when(step == 0)
            def _():
                y_refs[0][...] = norm(a_ref[...])

            part_at = pl.ds(pl.multiple_of(pl.program_id(1) * part_rows, part_rows), part_rows)
            for parity in range(2):
                @pl.when(pl.program_id(0) % 2 == parity)
                def _(y_now=y_refs[parity], y_next=y_refs[1 - parity]):
                    finish(_dot(y_now[...], b_ref[...].astype(BF16), dims))
                    out_refs[n_tiles][...] = y_now[part_at, :]
                    y_next[part_at, :] = norm(ahead_ref[...])
        elif nk == 1:
            finish(_dot(a_ref[...].astype(BF16), b_ref[...].astype(BF16), dims))
        else:
            part = _dot(a_ref[...].astype(BF16), b_ref[...].astype(BF16), dims)

            @pl.when(k == 0)
            def _():
                acc_ref[...] = part

            @pl.when(k > 0)
            def _():
                acc_ref[...] += part

            @pl.when(k == nk - 1)
            def _():
                finish(acc_ref[...])

        if comm is not None:
            if comm.mid is not None:
                @pl.when(step == (3 * steps) // 4)
                def _():
                    comm.mid(comm_in, comm_out, comm_sems)

            @pl.when(step == steps - 1)
            def _():
                comm.end(comm_in, comm_out, comm_sems)

    sequential = comm is not None or normed or col_sums > 0
    order = ("arbitrary",) * 3 if sequential else ("parallel", "parallel", "arbitrary")
    scratch = [pltpu.VMEM((tm, tn), F32)] if nk > 1 else []
    if normed:
        scratch += [pltpu.VMEM((tm, K), BF16)] * 2
    res = pl.pallas_call(
        body, name=name, grid=grid,
        in_specs=[a_spec, b_spec] + extra_specs + [ANY] * n_cin,
        out_specs=out_specs + [ANY] * n_cout,
        out_shape=out_shapes + (comm.out_shapes if comm else []),
        scratch_shapes=scratch + (comm.scratch if comm else []),
        compiler_params=_params(dimension_semantics=order),
    )(a, b, *extras, *vecs, *([norm_gain, a] if normed else []), *(comm.inputs if comm else []))
    mm_out = res[0] if n_out == 1 else list(res[:n_out])
    return (mm_out, list(res[n_out:])) if comm else mm_out


def _epi_residual(acc, res):
    return (res + acc,)


def _epi_relu2(acc):
    return (jnp.square(jnp.maximum(acc, 0.0)),)


def _epi_rms_bwd(copies):
    def epi(acc, h, dres, g):
        r = lax.rsqrt(_mean(h * h) + EPS)
        hn = h * r
        dyg = acc * g
        dh = dres + r * (dyg - hn * _mean(dyg * hn))
        return (dh,) * copies + (jnp.sum(acc * hn, axis=0, keepdims=True),)
    return epi


def _epi_loss_head(acc, res, target, g):
    h = res + acc
    r = lax.rsqrt(_mean(h * h) + EPS)
    hn = h * r
    diff = hn * g - target
    loss = 0.5 * jnp.sum(_mean(diff * diff))
    dy = diff * (1.0 / h.shape[-1])
    dyg = dy * g
    dh = r * (dyg - hn * _mean(dyg * hn))
    return dh, dh, jnp.sum(dy * hn, axis=0, keepdims=True), jnp.full((1, h.shape[-1]), loss, F32)


def _epi_relu2_grad(acc, q):
    qf = q.astype(F32)
    return (acc * jnp.where(qf > 0.0, (2.0 * qf) * lax.rsqrt(qf), 0.0),)


def _row_tile(T):
    return min(T, 512)


def _gate_tile(T):
    return min(T, 256)


def _gate_fwd(a, ln_g, ln_b, w_tril, b_rows):
    T, W2 = a.shape
    W = W2 // 2
    G = W // GROUP_DIM
    tr = _gate_tile(T)

    def body(a_ref, lng_ref, lnb_ref, w_ref, b_ref, z_ref, saved_ref):
        au = a_ref[:, :W].astype(F32)
        av = a_ref[:, W:].astype(F32)
        u = _gelu(au)
        vg = _gelu(av)
        xc = vg - _mean(vg)
        rstd = lax.rsqrt(_mean(xc * xc) + EPS)
        vn = xc * rstd
        vl = (vn * lng_ref[...] + lnb_ref[...]).astype(BF16)
        saved_ref[:, :W] = u.astype(BF16)
        saved_ref[:, W:2 * W] = _gelu_grad(au).astype(BF16)
        saved_ref[:, 2 * W:3 * W] = vn.astype(BF16)
        saved_ref[:, 3 * W:] = (rstd * _gelu_grad(av)).astype(BF16)
        for n in range(tr // CHUNK):
            rows = slice(n * CHUNK, (n + 1) * CHUNK)
            for g in range(G):
                cols = slice(g * GROUP_DIM, (g + 1) * GROUP_DIM)
                gate = _dot(w_ref[g], vl[rows, cols]) + b_ref[g]
                z_ref[rows, cols] = (u[rows, cols] * gate).astype(BF16)

    vec = pl.BlockSpec((1, W), lambda i: (0, 0))
    grp = pl.BlockSpec((G, CHUNK, CHUNK), lambda i: (0, 0, 0))
    return pl.pallas_call(
        body, name="gate_fwd", grid=(T // tr,),
        in_specs=[pl.BlockSpec((tr, W2), lambda i: (i, 0)), vec, vec, grp, grp],
        out_specs=[pl.BlockSpec((tr, W), lambda i: (i, 0)), pl.BlockSpec((tr, 4 * W), lambda i: (i, 0))],
        out_shape=[jax.ShapeDtypeStruct((T, W), BF16), jax.ShapeDtypeStruct((T, 4 * W), BF16)],
        compiler_params=_params(dimension_semantics=("parallel",)),
    )(a, ln_g, ln_b, w_tril, b_rows)


def _gate_bwd(saved, dz, ln_g, ln_b, w_tril, w_tril_t, b_rows):
    T, W = dz.shape
    W2 = 2 * W
    G = W // GROUP_DIM
    tr = _gate_tile(T)
    steps = T // tr

    def body(saved_ref, dz_ref, lng_ref, lnb_ref, w_ref, wt_ref, b_ref, da_ref, dlng_ref, dlnb_ref, dw_ref, dbs_ref,
             dvl_ref):
        step = pl.program_id(0)

        @pl.when(step == 0)
        def _():
            dlng_ref[...] = jnp.zeros_like(dlng_ref)
            dlnb_ref[...] = jnp.zeros_like(dlnb_ref)
            dw_ref[...] = jnp.zeros_like(dw_ref)
            dbs_ref[...] = jnp.zeros_like(dbs_ref)

        vn = saved_ref[:, 2 * W:3 * W].astype(F32)
        lng = lng_ref[...]
        vl = (vn * lng + lnb_ref[...]).astype(BF16)
        du_scale = dz_ref[...] * saved_ref[:, W:2 * W].astype(F32)
        dgate_all = dz_ref[...] * saved_ref[:, :W].astype(F32)
        for n in range(tr // CHUNK):
            rows = slice(n * CHUNK, (n + 1) * CHUNK)
            for g in range(G):
                cols = slice(g * GROUP_DIM, (g + 1) * GROUP_DIM)
                vlg = vl[rows, cols]
                gate = _dot(w_ref[g], vlg) + b_ref[g]
                da_ref[rows, cols] = (du_scale[rows, cols] * gate).astype(BF16)
                dgate = dgate_all[rows, cols]
                dbs_ref[g] += dgate
                dgate_b = dgate.astype(BF16)
                dw_ref[g] += _dot(dgate_b, vlg, NT)
                dvl_ref[rows, cols] = _dot(wt_ref[g], dgate_b)
        dvl = dvl_ref[...]
        dlnb_ref[...] += jnp.sum(dvl, axis=0, keepdims=True)
        dlng_ref[...] += jnp.sum(dvl * vn, axis=0, keepdims=True)
        dvn = dvl * lng
        dvg = dvn - _mean(dvn) - vn * _mean(dvn * vn)
        da_ref[:, W:] = (dvg * saved_ref[:, 3 * W:].astype(F32)).astype(BF16)

        @pl.when(step == steps - 1)
        def _():
            t_idx = lax.broadcasted_iota(jnp.int32, (CHUNK, CHUNK), 0)
            s_idx = lax.broadcasted_iota(jnp.int32, (CHUNK, CHUNK), 1)
            for g in range(G):
                dw_ref[g] = jnp.where(s_idx <= t_idx, dw_ref[g], 0.0)
                dbs_ref[g] = jnp.broadcast_to(jnp.sum(dbs_ref[g], axis=-1, keepdims=True), (CHUNK, CHUNK))

    vec = pl.BlockSpec((1, W), lambda i: (0, 0))
    grp = pl.BlockSpec((G, CHUNK, CHUNK), lambda i: (0, 0, 0))
    return pl.pallas_call(
        body, name="gate_bwd", grid=(steps,),
        in_specs=[pl.BlockSpec((tr, 4 * W), lambda i: (i, 0)), pl.BlockSpec((tr, W), lambda i: (i, 0)),
                  vec, vec, grp, grp, grp],
        out_specs=[pl.BlockSpec((tr, W2), lambda i: (i, 0)), vec, vec, grp, grp],
        out_shape=[jax.ShapeDtypeStruct((T, W2), BF16), jax.ShapeDtypeStruct((1, W), F32),
                   jax.ShapeDtypeStruct((1, W), F32), jax.ShapeDtypeStruct((G, CHUNK, CHUNK), F32),
                   jax.ShapeDtypeStruct((G, CHUNK, CHUNK), F32)],
        scratch_shapes=[pltpu.VMEM((tr, W), F32)],
        compiler_params=_params(dimension_semantics=("arbitrary",)),
    )(saved, dz, ln_g, ln_b, w_tril, w_tril_t, b_rows)


def _bucket_map(dilation):
    rel = BLK + np.arange(BLK)[:, None] - np.arange(2 * BLK)[None, :]
    dist = np.clip(rel, 0, BLK) * dilation
    nf = np.maximum(dist, 1).astype(np.float32)
    large = MAX_EXACT + (np.log(nf / np.float32(MAX_EXACT)) / np.float32(math.log(REL_MAX_DISTANCE / MAX_EXACT))
                         * np.float32(N_BUCKETS - MAX_EXACT)).astype(np.int32)
    large = np.minimum(large, N_BUCKETS - 1)
    return np.where(dist < MAX_EXACT, dist, large).astype(np.int32)


def _bucket_maps():
    return jnp.asarray(np.stack([_bucket_map(d) for d in DILATIONS]))


def _bias_build(rel_bias, buckets):
    NG = len(DILATIONS)

    def body(table_ref, bucket_ref, out_ref):
        for g in range(NG):
            bk = bucket_ref[g]
            for h in range(ATT_HEADS):
                out_ref[0, g, h] = jnp.zeros((BLK, 2 * BLK), F32)
            for b in range(N_BUCKETS):
                hit = bk == b
                for h in range(ATT_HEADS):
                    out_ref[0, g, h] = jnp.where(hit, table_ref[b, g * ATT_HEADS + h], out_ref[0, g, h])
            for h in range(ATT_HEADS):
                for first in range(2):
                    out_ref[first, g, h] = jnp.where(_window_mask(first), out_ref[0, g, h], NEG_INF)

    return pl.pallas_call(
        body, name="bias_build",
        in_specs=[pl.BlockSpec(memory_space=pltpu.SMEM), pl.BlockSpec(memory_space=pltpu.VMEM)],
        out_specs=pl.BlockSpec(memory_space=pltpu.VMEM),
        out_shape=jax.ShapeDtypeStruct((2, NG, ATT_HEADS, BLK, 2 * BLK), F32),
        compiler_params=_params(),
    )(rel_bias, buckets)


def _bias_scatter(dbias, buckets):
    NG = len(DILATIONS)

    def body(dbias_ref, bucket_ref, out_ref):
        for g in range(NG):
            bk = bucket_ref[g]
            for b in range(N_BUCKETS):
                hit = bk == b
                for h in range(ATT_HEADS):
                    out_ref[b, g * ATT_HEADS + h] = jnp.sum(jnp.where(hit, dbias_ref[g, h], 0.0))

    return pl.pallas_call(
        body, name="bias_scatter",
        in_specs=[pl.BlockSpec(memory_space=pltpu.VMEM), pl.BlockSpec(memory_space=pltpu.VMEM)],
        out_specs=pl.BlockSpec(memory_space=pltpu.SMEM),
        out_shape=jax.ShapeDtypeStruct((N_BUCKETS, NG * ATT_HEADS), F32),
        compiler_params=_params(),
    )(dbias, buckets)


def _window_mask(first):
    qi = lax.broadcasted_iota(jnp.int32, (BLK, 2 * BLK), 0)
    kj = lax.broadcasted_iota(jnp.int32, (BLK, 2 * BLK), 1)
    rel = BLK + qi - kj
    return (rel >= 0) & (rel <= BLK) & (kj >= BLK * first)


def _head_lanes(hh):
    lane = lax.broadcasted_iota(jnp.int32, (1, PAIR), 1)
    return (lane >= hh * HEAD_DIM) & (lane < (hh + 1) * HEAD_DIM)


ATT_STEP_BLOCKS = 8


def _attn_steps(stride):
    per_step = math.gcd(stride, ATT_STEP_BLOCKS)
    return per_step, stride // per_step


def _attn_fwd(name, g, qkv, qc, kc, vc, bias, stride):
    T = qkv.shape[0]
    per_step, lag = _attn_steps(stride)
    chained = stride == 1
    if chained:
        per_step, lag = min(ATT_STEP_BLOCKS, T // BLK), 1
    rows = per_step * BLK
    scale = HEAD_DIM ** -0.5

    def body(q_ref, kp_ref, kc_ref, vp_ref, vc_ref, bias_ref, out_ref, *chain):
        step = pl.program_id(0)
        low = _head_lanes(0)
        if chained:
            for cat, before, now in zip(chain, (kp_ref, vp_ref), (kc_ref, vc_ref)):
                cat[:BLK, :] = before[...]
                cat[BLK:, :] = now[...]

        def block(j, carry):
            at = pl.ds(pl.multiple_of(j * BLK, BLK), BLK)
            if chained:
                first = ((step == 0) & (j == 0)).astype(jnp.int32)
                after = pl.ds(pl.multiple_of((j + 1) * BLK, BLK), BLK)
                keys = lambda cols: jnp.concatenate([chain[0][at, cols], chain[0][after, cols]], axis=0)
                values = lambda cols: jnp.concatenate([chain[1][at, cols], chain[1][after, cols]], axis=0)
            else:
                first = (step < lag).astype(jnp.int32)
                keys = lambda cols: jnp.concatenate([kp_ref[at, cols], kc_ref[at, cols]], axis=0)
                values = lambda cols: jnp.concatenate([vp_ref[at, cols], vc_ref[at, cols]], axis=0)
            for hp in range(ATT_HEADS // 2):
                cols = slice(hp * PAIR, (hp + 1) * PAIR)
                qp = q_ref[at, cols]
                kk = keys(cols)
                vv = values(cols)
                o_h, lse_h = [], []
                for hh in range(2):
                    qm = jnp.where(_head_lanes(hh), qp, jnp.zeros_like(qp))
                    s = _dot(qm, kk, NT) * scale
                    logits = s + bias_ref[first, 2 * hp + hh]
                    m = jnp.max(logits, axis=-1, keepdims=True)
                    p = jnp.exp(logits - m)
                    den = jnp.sum(p, axis=-1, keepdims=True)
                    o_h.append(_dot(p.astype(BF16), vv) / den)
                    lse_h.append(m + jnp.log(den))
                out_ref[at, cols] = jnp.where(low, o_h[0], o_h[1])
                out_ref[at, slice(ATT_WIDTH + hp * PAIR, ATT_WIDTH + (hp + 1) * PAIR)] = (
                    jnp.where(low, lse_h[0], lse_h[1]))
            return carry

        lax.fori_loop(0, per_step, block, 0)

    def cur(c):
        return pl.BlockSpec((rows, ATT_WIDTH), lambda s: (s, c))

    def prev(c):
        if chained:
            return pl.BlockSpec((BLK, ATT_WIDTH), lambda s: (jnp.maximum(s * per_step - 1, 0), c))
        return pl.BlockSpec((rows, ATT_WIDTH), lambda s: (jnp.maximum(s - lag, 0), c))

    return pl.pallas_call(
        body, name=name, grid=(T // rows,),
        in_specs=[cur(qc), prev(kc), cur(kc), prev(vc), cur(vc),
                  pl.BlockSpec((2, None, ATT_HEADS, BLK, 2 * BLK), lambda s: (0, g, 0, 0, 0))],
        out_specs=pl.BlockSpec((rows, 2 * ATT_WIDTH), lambda s: (s, 0)),
        out_shape=jax.ShapeDtypeStruct((T, 2 * ATT_WIDTH), F32),
        scratch_shapes=[pltpu.VMEM((rows + BLK, ATT_WIDTH), BF16)] * 2 if chained else [],
        compiler_params=_params(dimension_semantics=("parallel",)),
    )(qkv, qkv, qkv, qkv, qkv, bias)


def _permute_f32(p, x):
    hi = x.astype(BF16)
    rest = x - hi.astype(F32)
    mid = rest.astype(BF16)
    low = (rest - mid.astype(F32)).astype(BF16)
    return _dot(p, hi) + _dot(p, mid) + _dot(p, low)


def _attn_merge(parts):
    T = parts[0].shape[0]
    rows = min(T, REORDER_ROWS)
    n = len(parts)
    width = 2 * PAIR
    ncol = ATT_WIDTH // width

    def body(*refs):
        p_refs, o_refs, l_refs = refs[:n], refs[n:2 * n], refs[2 * n:3 * n]
        o_ref, lse_ref = refs[3 * n], refs[3 * n + 1]

        def positions(ref, g, start):
            d = DILATIONS[g]
            if d == 1:
                return ref[start:start + REORDER_TILE, :]
            span, per = BLK * d, REORDER_TILE // d
            base, t = start // span * span, start % span // REORDER_TILE
            chunks = [ref[base + r * BLK + t * per:base + r * BLK + (t + 1) * per, :] for r in range(d)]
            return _permute_f32(p_refs[g][...], jnp.concatenate(chunks, axis=0))

        for start in range(0, rows, REORDER_TILE):
            ls = [positions(l_refs[g], g, start) for g in range(n)]
            m = functools.reduce(jnp.maximum, ls)
            es = [jnp.exp(l - m) for l in ls]
            tot = functools.reduce(lambda x, y: x + y, es)
            acc = functools.reduce(lambda x, y: x + y, [e * positions(o_refs[g], g, start) for g, e in enumerate(es)])
            o_ref[start:start + REORDER_TILE, :] = (acc / tot).astype(BF16)
            lse_ref[start:start + REORDER_TILE, :] = m + jnp.log(tot)

    matrix = pl.BlockSpec((REORDER_TILE, REORDER_TILE), lambda w, c: (0, 0))
    col = pl.BlockSpec((rows, width), lambda w, c: (w, c))
    col_lse = pl.BlockSpec((rows, width), lambda w, c: (w, ncol + c))
    return pl.pallas_call(
        body, name="attn_merge", grid=(T // rows, ncol),
        in_specs=[matrix] * n + [col] * n + [col_lse] * n, out_specs=[col, col],
        out_shape=[jax.ShapeDtypeStruct((T, ATT_WIDTH), BF16), jax.ShapeDtypeStruct((T, ATT_WIDTH), F32)],
        compiler_params=_params(dimension_semantics=("parallel", "parallel")),
    )(*[_reorder_matrix(max(d, 2), True) for d in DILATIONS], *parts, *parts)


def _attn_bwd(name, g, qkv, qc, kc, vc, do, o, lse, bias, stride, comm=None):
    T = qkv.shape[0]
    per_step, lag = _attn_steps(stride)
    rows = per_step * BLK
    steps = T // rows
    scale = HEAD_DIM ** -0.5
    n_cin = len(comm.inputs) if comm else 0
    n_cout = len(comm.out_shapes) if comm else 0
    assert comm is None or comm.mid is None

    def body(*refs):
        q_ref, kp_ref, kc_ref, vp_ref, vc_ref, do_ref, o_ref, lse_ref, bias_ref = refs[:9]
        comm_in = refs[9:9 + n_cin]
        dq_ref, dkv_ref, db_ref = refs[9 + n_cin:12 + n_cin]
        comm_out = refs[12 + n_cin:12 + n_cin + n_cout]
        carry_k, carry_v = refs[12 + n_cin + n_cout:14 + n_cin + n_cout]
        comm_sems = refs[14 + n_cin + n_cout:]
        step = pl.program_id(0)

        if comm is not None:
            @pl.when(step == 0)
            def _():
                comm.start(comm_in, comm_out, comm_sems)

            @pl.when(step == steps + lag - 1)
            def _():
                comm.end(comm_in, comm_out, comm_sems)

        slot0 = (step % lag) * per_step

        @pl.when(step == 0)
        def _():
            db_ref[...] = jnp.zeros_like(db_ref)
            carry_k[...] = jnp.zeros_like(carry_k)
            carry_v[...] = jnp.zeros_like(carry_v)

        @pl.when(step >= steps)
        def _():
            def flush(j, carry):
                at = pl.ds(pl.multiple_of(j * BLK, BLK), BLK)
                dkv_ref[at, :ATT_WIDTH] = carry_k[slot0 + j].astype(BF16)
                dkv_ref[at, ATT_WIDTH:] = carry_v[slot0 + j].astype(BF16)
                return carry

            lax.fori_loop(0, per_step, flush, 0)

        @pl.when(step < steps)
        def _():
            first = (step < lag).astype(jnp.int32)

            def block(j, carry):
                at = pl.ds(pl.multiple_of(j * BLK, BLK), BLK)
                ck_ref = carry_k.at[slot0 + j]
                cv_ref = carry_v.at[slot0 + j]
                for hp in range(ATT_HEADS // 2):
                    cols = slice(hp * PAIR, (hp + 1) * PAIR)
                    qp = q_ref[at, cols]
                    kk = jnp.concatenate([kp_ref[at, cols], kc_ref[at, cols]], axis=0)
                    vv = jnp.concatenate([vp_ref[at, cols], vc_ref[at, cols]], axis=0)
                    dop = do_ref[at, cols]
                    lsep = lse_ref[at, cols]
                    prod = dop.astype(F32) * o_ref[at, cols].astype(F32)
                    dq = jnp.zeros((BLK, PAIR), F32)
                    dk = jnp.zeros((2 * BLK, PAIR), F32)
                    dv = jnp.zeros((2 * BLK, PAIR), F32)
                    for hh in range(2):
                        lanes = _head_lanes(hh)
                        qm = jnp.where(lanes, qp, jnp.zeros_like(qp))
                        dom = jnp.where(lanes, dop, jnp.zeros_like(dop))
                        km = jnp.where(lanes, kk, jnp.zeros_like(kk))
                        delta = jnp.sum(jnp.where(lanes, prod, 0.0), axis=-1, keepdims=True)
                        lse_h = jnp.max(jnp.where(lanes, lsep, NEG_INF), axis=-1, keepdims=True)
                        s = _dot(qm, kk, NT) * scale
                        logits = s + bias_ref[first, 2 * hp + hh]
                        p = jnp.exp(logits - lse_h)
                        dv += _dot(p.astype(BF16), dom, TN)
                        ds = p * (_dot(dom, vv, NT) - delta)
                        db_ref[2 * hp + hh] += ds
                        dss = (ds * scale).astype(BF16)
                        dq += _dot(dss, km)
                        dk += _dot(dss, qm, TN)
                    dq_ref[at, cols] = dq.astype(BF16)
                    dkv_ref[at, cols] = (ck_ref[:, cols] + dk[:BLK]).astype(BF16)
                    dkv_ref[at, slice(ATT_WIDTH + hp * PAIR, ATT_WIDTH + (hp + 1) * PAIR)] = (
                        cv_ref[:, cols] + dv[:BLK]).astype(BF16)
                    ck_ref[:, cols] = dk[BLK:]
                    cv_ref[:, cols] = dv[BLK:]
                return carry

            lax.fori_loop(0, per_step, block, 0)

    last = steps - 1

    def cur(c):
        return pl.BlockSpec((rows, ATT_WIDTH), lambda s: (jnp.minimum(s, last), c))

    def prev(c):
        return pl.BlockSpec((rows, ATT_WIDTH), lambda s: (jnp.clip(s - lag, 0, last), c))

    dbias_shape = (ATT_HEADS, BLK, 2 * BLK)
    res = pl.pallas_call(
        body, name=name, grid=(steps + lag,),
        in_specs=[cur(qc), prev(kc), cur(kc), prev(vc), cur(vc), cur(0), cur(0), cur(0),
                  pl.BlockSpec((2, None, ATT_HEADS, BLK, 2 * BLK), lambda s: (0, g, 0, 0, 0))] + [ANY] * n_cin,
        out_specs=[cur(0), pl.BlockSpec((rows, 2 * ATT_WIDTH), lambda s: (jnp.clip(s - lag, 0, last), 0)),
                   pl.BlockSpec(dbias_shape, lambda s: (0, 0, 0))] + [ANY] * n_cout,
        out_shape=[jax.ShapeDtypeStruct((T, ATT_WIDTH), BF16), jax.ShapeDtypeStruct((T, 2 * ATT_WIDTH), BF16),
                   jax.ShapeDtypeStruct(dbias_shape, F32)] + (comm.out_shapes if comm else []),
        scratch_shapes=[pltpu.VMEM((stride, BLK, ATT_WIDTH), F32), pltpu.VMEM((stride, BLK, ATT_WIDTH), F32)]
        + (comm.scratch if comm else []),
        compiler_params=_params(dimension_semantics=("arbitrary",)),
    )(qkv, qkv, qkv, qkv, qkv, do, o, lse, bias, *(comm.inputs if comm else []))
    return res[0], res[1], res[2], list(res[3:])


REORDER_TILE = 256
REORDER_ROWS = 2048


def _reorder_matrix(d, inverse):
    per = REORDER_TILE // d
    p = np.zeros((REORDER_TILE, REORDER_TILE), np.float32)
    for src in range(REORDER_TILE):
        i, r = divmod(src, d)
        p[r * per + i, src] = 1.0
    return jnp.asarray(p.T if inverse else p, dtype=BF16)


def _reorder_rows(name, src, d, inverse, *, src_col=0, col_stride=1, ncols=1, dst=None, dst_col=0, dst_stride=1,
                  dst_blocks=None):
    T = src.shape[0]
    dtype = src.dtype
    span = BLK * d
    rows = max(span, min(T, REORDER_ROWS))
    per = REORDER_TILE // d
    tiles = span // REORDER_TILE
    dst_blocks = ncols if dst_blocks is None else dst_blocks

    def apply(p, x):
        return _dot(p, x).astype(BF16) if dtype == BF16 else _permute_f32(p, x)

    def body(*refs):
        p_ref, x_ref, o_ref = refs[0], refs[1], refs[-1]
        if d == 1:
            o_ref[...] = x_ref[...]
            return
        for s in range(rows // span):
            for t in range(tiles):
                base = s * span
                tile_rows = slice(base + t * REORDER_TILE, base + (t + 1) * REORDER_TILE)
                chunk = lambda r: slice(base + r * BLK + t * per, base + r * BLK + (t + 1) * per)
                if inverse:
                    gathered = jnp.concatenate([x_ref[chunk(r), :] for r in range(d)], axis=0)
                    o_ref[tile_rows, :] = apply(p_ref[...], gathered)
                else:
                    y = apply(p_ref[...], x_ref[tile_rows, :])
                    for r in range(d):
                        o_ref[chunk(r), :] = y[r * per:(r + 1) * per]

    in_specs = [pl.BlockSpec((REORDER_TILE, REORDER_TILE), lambda w, k: (0, 0)),
                pl.BlockSpec((rows, ATT_WIDTH), lambda w, k: (w, src_col + col_stride * k))]
    operands = [_reorder_matrix(max(d, 2), inverse), src]
    aliases = {}
    if dst is not None:
        in_specs.append(ANY)
        operands.append(dst)
        aliases = {2: 0}
    return pl.pallas_call(
        body, name=name, grid=(T // rows, ncols), in_specs=in_specs,
        out_specs=pl.BlockSpec((rows, ATT_WIDTH), lambda w, k: (w, dst_col + dst_stride * k)),
        out_shape=jax.ShapeDtypeStruct((T, dst_blocks * ATT_WIDTH), dtype),
        input_output_aliases=aliases,
        compiler_params=_params(dimension_semantics=("parallel", "parallel")),
    )(*operands)


def _group_qkv(qkv, g, d):
    NG = len(DILATIONS)
    if d == 1:
        return qkv, (g, NG + g, 2 * NG + g)
    return _reorder_rows(f"qkv_to_residues{g}", qkv, d, False, src_col=g, col_stride=NG, ncols=3), (0, 1, 2)


def _attention_fwd(qkv, bias):
    T = qkv.shape[0]
    parts = []
    for g, d in enumerate(DILATIONS):
        src, (qc, kc, vc) = _group_qkv(qkv, g, d)
        parts.append(_attn_fwd(f"attn_fwd_{g}", g, src, qc, kc, vc, bias, d))
    return _attn_merge(parts)


def _attention_bwd(qkv, do, o, lse, bias, comms):
    NG = len(DILATIONS)
    dqkv, dbs, carried = None, [], []
    for g, d in enumerate(DILATIONS):
        src, (qc, kc, vc) = _group_qkv(qkv, g, d)
        do_g, o_g, lse_g = do, o, lse
        if d > 1:
            do_g = _reorder_rows(f"do_to_residues{g}", do, d, False)
            o_g = _reorder_rows(f"o_to_residues{g}", o, d, False)
            lse_g = _reorder_rows(f"lse_to_residues{g}", lse, d, False)
        dq, dkv, db, sent = _attn_bwd(f"attn_bwd_{g}", g, src, qc, kc, vc, do_g, o_g, lse_g, bias, d, comm=comms[g])
        dqkv = _reorder_rows(f"dq_to_positions{g}", dq, d, True, dst=dqkv, dst_col=g, dst_blocks=3 * NG)
        dqkv = _reorder_rows(f"dkv_to_positions{g}", dkv, d, True, ncols=2, dst=dqkv, dst_col=NG + g, dst_stride=NG,
                             dst_blocks=3 * NG)
        dbs.append(db)
        carried.append(sent)
    return dqkv, jnp.stack(dbs), carried


def _other_chips(x, y):
    return [(1 - x, y), (x, 1 - y), (1 - x, 1 - y)]


def _shard_region(ref, shape, by_cols, chip, rows=None):
    R, C = shape
    start, size = (0, R) if rows is None else rows
    if by_cols:
        return ref.at[pl.ds(start, size), pl.ds(chip * C, C)]
    return ref.at[pl.ds(chip * R + start, size), :]


def _gather_weights(entries):
    n = len(entries)
    shapes = [e[0].shape[1:] for e in entries]

    def places(ins, outs, sems):
        send_sems, recv_sems, local_sems = sems
        x, y, c = lax.axis_index("x"), lax.axis_index("y"), lax.axis_index("c")

        def landing(f, px, py, pc):
            R = shapes[f][0]
            return _shard_region(outs[f], shapes[f], entries[f][2], 2 * px + py, rows=(pc * (R // 2), R // 2))

        def copy(f, k, block, to, src=None):
            dst = landing(f, *block)
            return pltpu.make_async_remote_copy(
                src_ref=dst if src is None else src, dst_ref=dst,
                send_sem=send_sems.at[6 * f + k], recv_sem=recv_sems.at[6 * f + k],
                device_id=to, device_id_type=MESH)

        def mine(f):
            dst = _shard_region(outs[f], shapes[f], entries[f][2], 2 * x + y)
            return pltpu.make_async_copy(ins[f].at[entries[f][1]], dst, local_sems.at[f])

        def first(f, j):
            R = shapes[f][0]
            src = ins[f].at[entries[f][1], pl.ds(c * (R // 2), R // 2), :]
            return copy(f, j, (x, y, c), (*_other_chips(x, y)[j], c), src=src)

        return x, y, c, copy, mine, first

    def start(ins, outs, sems):
        _, _, _, _, mine, first = places(ins, outs, sems)
        for f in range(n):
            mine(f).start()
        for j in range(3):
            for f in range(n):
                first(f, j).start()

    def mid(ins, outs, sems):
        x, y, c, copy, _, _ = places(ins, outs, sems)
        for j, chip in enumerate(_other_chips(x, y)):
            for f in range(n):
                copy(f, j, (*chip, c), (x, y, c)).wait_recv()
                copy(f, 3 + j, (*chip, c), (x, y, 1 - c)).start()

    def end(ins, outs, sems):
        x, y, c, copy, mine, first = places(ins, outs, sems)
        for j, chip in enumerate(_other_chips(x, y)):
            for f in range(n):
                copy(f, 3 + j, (*chip, 1 - c), (x, y, c)).wait_recv()
        for j, chip in enumerate(_other_chips(x, y)):
            for f in range(n):
                first(f, j).wait_send()
                copy(f, 3 + j, (*chip, c), (x, y, 1 - c)).wait_send()
        for f in range(n):
            mine(f).wait()

    def whole(f):
        R, C = shapes[f]
        return (R, N_CHIPS * C) if entries[f][2] else (N_CHIPS * R, C)

    return _Comm(
        [e[0] for e in entries], [jax.ShapeDtypeStruct(whole(f), BF16) for f in range(n)],
        [pltpu.SemaphoreType.DMA((6 * n,)), pltpu.SemaphoreType.DMA((6 * n,)), pltpu.SemaphoreType.DMA((n,))],
        start, end, mid)


def _scatter_grads(entries):
    n = len(entries)

    def copies(ins, outs, sems):
        send_sems, recv_sems, local_sems = sems
        x, y, c = lax.axis_index("x"), lax.axis_index("y"), lax.axis_index("c")
        me = 2 * x + y

        def piece(f, chip):
            return _shard_region(ins[f], entries[f][1], entries[f][2], chip)

        mine = [pltpu.make_async_copy(piece(f, me), outs[f].at[me], local_sems.at[f]) for f in range(n)]
        sends = [pltpu.make_async_remote_copy(
            src_ref=piece(f, 2 * px + py), dst_ref=outs[f].at[me],
            send_sem=send_sems.at[3 * f + j], recv_sem=recv_sems.at[3 * f + j],
            device_id=(px, py, c), device_id_type=MESH)
            for j, (px, py) in enumerate(_other_chips(x, y)) for f in range(n)]
        return mine, sends

    def start(ins, outs, sems):
        mine, sends = copies(ins, outs, sems)
        for cp in mine + sends:
            cp.start()

    def end(ins, outs, sems):
        mine, sends = copies(ins, outs, sems)
        for cp in sends + mine:
            cp.wait()

    return _Comm(
        [e[0] for e in entries], [jax.ShapeDtypeStruct((N_CHIPS,) + tuple(e[1]), BF16) for e in entries],
        [pltpu.SemaphoreType.DMA((3 * n,)), pltpu.SemaphoreType.DMA((3 * n,)), pltpu.SemaphoreType.DMA((n,))],
        start, end)


def _exchange_sibling(parts):
    n = len(parts)

    def copies(ins, outs, sems):
        send_sems, recv_sems = sems
        sibling = (lax.axis_index("x"), lax.axis_index("y"), 1 - lax.axis_index("c"))
        return [pltpu.make_async_remote_copy(src_ref=ins[i], dst_ref=outs[i], send_sem=send_sems.at[i],
                                             recv_sem=recv_sems.at[i], device_id=sibling, device_id_type=MESH)
                for i in range(n)]

    def start(ins, outs, sems):
        for cp in copies(ins, outs, sems):
            cp.start()

    def end(ins, outs, sems):
        for cp in copies(ins, outs, sems):
            cp.wait()

    return _Comm(parts, [jax.ShapeDtypeStruct(s.shape, s.dtype) for s in parts],
                 [pltpu.SemaphoreType.DMA((n,)), pltpu.SemaphoreType.DMA((n,))], start, end)


def _allgather_small(block):
    m_per, ncol = block.shape

    def places(ins, outs, sems):
        send_sems, recv_sems, local_sem = sems
        x, y, c = lax.axis_index("x"), lax.axis_index("y"), lax.axis_index("c")

        def rows(px, py, pc):
            return outs[0].at[4 * px + 2 * py + pc]

        def copy(k, block_of, to, src=None):
            return pltpu.make_async_remote_copy(
                src_ref=rows(*block_of) if src is None else src, dst_ref=rows(*block_of),
                send_sem=send_sems.at[k], recv_sem=recv_sems.at[k], device_id=to, device_id_type=MESH)

        mine = pltpu.make_async_copy(ins[0], rows(x, y, c), local_sem.at[0])
        first = [copy(0, (x, y, c), (x, y, 1 - c), src=ins[0])]
        first += [copy(1 + j, (x, y, c), (*chip, c), src=ins[0]) for j, chip in enumerate(_other_chips(x, y))]
        passed = [copy(4 + j, (*chip, c), (x, y, 1 - c)) for j, chip in enumerate(_other_chips(x, y))]
        return x, y, c, copy, mine, first, passed

    def start(ins, outs, sems):
        _, _, _, _, mine, first, _ = places(ins, outs, sems)
        for cp in [mine] + first:
            cp.start()

    def mid(ins, outs, sems):
        x, y, c, copy, _, _, passed = places(ins, outs, sems)
        for j, chip in enumerate(_other_chips(x, y)):
            copy(1 + j, (*chip, c), (x, y, c)).wait_recv()
            passed[j].start()

    def end(ins, outs, sems):
        x, y, c, copy, mine, first, passed = places(ins, outs, sems)
        copy(0, (x, y, 1 - c), (x, y, c)).wait_recv()
        for j, chip in enumerate(_other_chips(x, y)):
            copy(4 + j, (*chip, 1 - c), (x, y, c)).wait_recv()
        for cp in first + passed:
            cp.wait_send()
        mine.wait()

    return _Comm([block], [jax.ShapeDtypeStruct((N_DEV, m_per, ncol), block.dtype)],
                 [pltpu.SemaphoreType.DMA((7,)), pltpu.SemaphoreType.DMA((7,)), pltpu.SemaphoreType.DMA((1,))],
                 start, end, mid)


def _join_comms(*progs):
    def split(parts, counts):
        out, pos = [], 0
        for n in counts:
            out.append(parts[pos:pos + n])
            pos += n
        return out

    def phase(which):
        def run(ins, outs, sems):
            args = zip(split(ins, [len(p.inputs) for p in progs]), split(outs, [len(p.out_shapes) for p in progs]),
                       split(sems, [len(p.scratch) for p in progs]))
            for p, (i, o, s) in zip(progs, args):
                fn = getattr(p, which)
                if fn is not None:
                    fn(i, o, s)
        return run

    return _Comm([a for p in progs for a in p.inputs], [s for p in progs for s in p.out_shapes],
                 [s for p in progs for s in p.scratch], phase("start"), phase("end"), phase("mid"))


def _adamw(w, g, m, v):
    m = ADAM_B1 * m + (1.0 - ADAM_B1) * g
    v = ADAM_B2 * v + (1.0 - ADAM_B2) * jnp.square(g)
    m_hat = m / (1.0 - ADAM_B1 ** ADAM_STEP)
    v_hat = v / (1.0 - ADAM_B2 ** ADAM_STEP)
    delta = -ADAM_LR * (m_hat / (jnp.sqrt(v_hat) + ADAM_EPS) + ADAM_WD * w)
    return delta, m, v


def _flat_tile(rows):
    return min(rows, 512)


def _sum_pieces(name, layers):
    L = len(layers)
    P, R, C = layers[0].shape
    tr = _flat_tile(R)

    def body(*refs):
        out_ref = refs[L]
        for l in range(L):
            @pl.when(pl.program_id(0) == l)
            def _(p_ref=refs[l]):
                acc = p_ref[0].astype(F32)
                for j in range(1, P):
                    acc = acc + p_ref[j].astype(F32)
                out_ref[...] = acc

    return pl.pallas_call(
        body, name=name, grid=(L, R // tr),
        in_specs=[pl.BlockSpec((P, tr, C), lambda l, i: (0, i, 0)) for _ in range(L)],
        out_specs=pl.BlockSpec((None, tr, C), lambda l, i: (l, i, 0)),
        out_shape=jax.ShapeDtypeStruct((L, R, C), F32),
        compiler_params=_params(dimension_semantics=("parallel", "parallel")),
    )(*layers)


def _adam_pair(name, w, m, v, part_a, part_b):
    L, R, C = w.shape
    tr = _flat_tile(R)

    def body(w_ref, m_ref, v_ref, a_ref, b_ref, g_ref, d_ref, nm_ref, nv_ref):
        g = a_ref[...] + b_ref[...]
        g_ref[...] = g
        d_ref[...], nm_ref[...], nv_ref[...] = _adamw(w_ref[...], g, m_ref[...], v_ref[...])

    row = pl.BlockSpec((None, tr, C), lambda l, i: (l, i, 0))
    return pl.pallas_call(
        body, name=name, grid=(L, R // tr),
        in_specs=[row] * 5, out_specs=[row] * 4,
        out_shape=[jax.ShapeDtypeStruct((L, R, C), F32)] * 4,
        compiler_params=_params(dimension_semantics=("parallel", "parallel")),
    )(w, m, v, part_a, part_b)


def _adam_small(w, m, v, gathered):
    R, C = w.shape

    def body(w_ref, m_ref, v_ref, p_ref, g_ref, d_ref, nm_ref, nv_ref):
        g = p_ref[0]
        for j in range(1, N_DEV):
            g = g + p_ref[j]
        g_ref[...] = g
        d_ref[...], nm_ref[...], nv_ref[...] = _adamw(w_ref[...], g, m_ref[...], v_ref[...])

    return pl.pallas_call(
        body, name="adam_small",
        out_shape=[jax.ShapeDtypeStruct((R, C), F32)] * 4,
        compiler_params=_params(),
    )(w, m, v, gathered)


SMALL = ("mix_norm_g", "mlp_norm_g", "final_norm_g", "a_ln_g", "a_ln_b", "a_w_s", "a_b_s", "rel_bias")


def _pack_small(arrays, width):
    rows = []
    for a in arrays:
        flat = a.reshape(-1)
        pad = (-flat.shape[0]) % width
        rows.append(jnp.pad(flat, (0, pad)).reshape(-1, width))
    block = jnp.concatenate(rows, axis=0)
    return jnp.pad(block, ((0, (-block.shape[0]) % 8), (0, 0)))


def _unpack_small(block, shapes, width):
    out, row = [], 0
    for shape in shapes:
        size = int(np.prod(shape))
        nrows = -(-size // width)
        out.append(block[row:row + nrows].reshape(-1)[:size].reshape(shape))
        row += nrows
    return out


def kernel(x, mix_norm_g, mlp_norm_g, final_norm_g, a_w_in, a_ln_g, a_ln_b, a_w_s, a_b_s, a_w_out, b_w_qkv, b_w_out, rel_bias, w_up, w_down, loss_target, m_mix_norm_g, m_mlp_norm_g, m_final_norm_g, m_a_w_in, m_a_ln_g, m_a_ln_b, m_a_w_s, m_a_b_s, m_a_w_out, m_b_w_qkv, m_b_w_out, m_rel_bias, m_w_up, m_w_down, v_mix_norm_g, v_mlp_norm_g, v_final_norm_g, v_a_w_in, v_a_ln_g, v_a_ln_b, v_a_w_s, v_a_b_s, v_a_w_out, v_b_w_qkv, v_b_w_out, v_rel_bias, v_w_up, v_w_down):
    T, D = x.shape[1], x.shape[2]
    h0 = x.reshape(T, D)
    target = loss_target.reshape(T, D)
    G = a_w_s.shape[1]

    w_big = [a_w_in, a_w_out, b_w_qkv, b_w_out, w_up, w_down]
    m_big = [m_a_w_in, m_a_w_out, m_b_w_qkv, m_b_w_out, m_w_up, m_w_down]
    v_big = [v_a_w_in, v_a_w_out, v_b_w_qkv, v_b_w_out, v_w_up, v_w_down]
    by_cols = [True, False, True, True, True, False]
    s_in, s_out, s_qkv, s_bo, s_up, s_dn = [w.astype(BF16) for w in w_big]
    W_in, W_out = _run_comm("gather_a", _gather_weights([(s_in, 0, True), (s_out, 0, False)]))

    tril = jnp.tril(jnp.ones((CHUNK, CHUNK), dtype=bool))
    w_tril = jnp.where(tril[None], a_w_s[0], 0.0).astype(BF16)
    w_tril_t = jnp.swapaxes(w_tril, 1, 2)
    b_rows = jnp.broadcast_to(a_b_s[0][:, :, None], (G, CHUNK, CHUNK))
    buckets = _bucket_maps()
    bias = _bias_build(rel_bias, buckets)

    QKV = s_qkv.shape[2] * N_CHIPS
    TM = 1024
    TK_WGRAD = 4096

    def matmul(name, a, b, mode, out, tm=TM, tn=1024, **kw):
        outs = out if isinstance(out, list) else [out]
        return _mm(name, a, b, mode, tm=tm, tn=tn, tk=a.shape[1], outs=outs, **kw)

    def norm_bwd(layer_gain, h, dres, copies=2):
        return dict(epi=_epi_rms_bwd(copies), extras=(h, dres), vecs=(layer_gain,), col_sums=1)

    def wgrad(name, a, b, tn=1024, tk=TK_WGRAD, comm=None):
        return _mm(name, a, b, "tn", tm=1024, tn=tn, tk=tk, outs=[BF16], comm=comm)

    def scatter(*which):
        return _scatter_grads([(g, w_big[i].shape[1:], by_cols[i]) for g, i in which])

    (a_pre, y0), (W_up0,) = matmul("a_in", h0, W_in, "nn", BF16, norm_gain=mix_norm_g[0:1],
                                   comm=_gather_weights([(s_up, 0, True)]))
    z, gate_saved = _gate_fwd(a_pre, a_ln_g, a_ln_b, w_tril, b_rows)
    h1 = matmul("a_out", z, W_out, "nn", F32, epi=_epi_residual, extras=(h0,))
    (q1, y1), (W_dn0,) = matmul("mlp_up0", h1, W_up0, "nn", BF16, epi=_epi_relu2, norm_gain=mlp_norm_g[0:1],
                                comm=_gather_weights([(s_dn, 0, False)]))
    h2, (W_qkv, W_bo) = matmul("mlp_down0", q1, W_dn0, "nn", F32, tm=TM // 2, epi=_epi_residual, extras=(h1,),
                               comm=_gather_weights([(s_qkv, 0, True), (s_bo, 0, True)]))
    (qkv, y2), (W_up1,) = matmul("b_qkv", h2, W_qkv, "nn", BF16, tn=QKV // 4, norm_gain=mix_norm_g[1:2],
                                 comm=_gather_weights([(s_up, 1, True)]))
    o, lse = _attention_fwd(qkv, bias)
    h3 = matmul("b_out", o, W_bo, "nn", F32, epi=_epi_residual, extras=(h2,))
    (q3, y3), (W_dn1,) = matmul("mlp_up1", h3, W_up1, "nn", BF16, epi=_epi_relu2, norm_gain=mlp_norm_g[1:2],
                                comm=_gather_weights([(s_dn, 1, False)]))
    dh4, dh4_b, d_final_g, loss_row = matmul("mlp_down1", q3, W_dn1, "nn", [F32, BF16], tm=TM // 2, epi=_epi_loss_head,
                                             extras=(h3, target), vecs=(final_norm_g.reshape(1, D),), col_sums=2)

    dp3 = matmul("mlp_down_bwd1", dh4_b, W_dn1, "nt", BF16, epi=_epi_relu2_grad, extras=(q3,))
    g_dn1 = wgrad("mlp_down_wgrad1", q3, dh4_b)
    g_up1 = wgrad("mlp_up_wgrad1", y3, dp3)
    dh3, dh3_b, dg_mlp1 = matmul("mlp_up_bwd1", dp3, W_up1, "nt", [F32, BF16], tm=TM // 2,
                                 **norm_bwd(mlp_norm_g[1:2], h3, dh4))
    do = matmul("b_out_bwd", dh3_b, W_bo, "nt", BF16)
    g_bo = wgrad("b_out_wgrad", o, dh3_b)
    dqkv, dbias, ((r_dn1,), (r_up1,), (r_bo,)) = _attention_bwd(
        qkv, do, o, lse, bias, [scatter((g_dn1, 5)), scatter((g_up1, 4)), scatter((g_bo, 3))])
    d_rel_bias = _bias_scatter(dbias, buckets)
    dh2, dh2_b, dg_mix1 = matmul("b_qkv_bwd", dqkv, W_qkv, "nt", [F32, BF16], tm=TM // 2,
                                 **norm_bwd(mix_norm_g[1:2], h2, dh3))
    g_qkv = wgrad("b_qkv_wgrad", y2, dqkv, tn=QKV // 3, tk=TK_WGRAD // 2)
    dp1, (r_qkv,) = matmul("mlp_down_bwd0", dh2_b, W_dn0, "nt", BF16, epi=_epi_relu2_grad, extras=(q1,),
                           comm=scatter((g_qkv, 2)))
    g_up0 = wgrad("mlp_up_wgrad0", y1, dp1)
    g_dn0, (r_up0,) = wgrad("mlp_down_wgrad0", q1, dh2_b, comm=scatter((g_up0, 4)))
    (dh1, dh1_b, dg_mlp0), (r_dn0,) = matmul("mlp_up_bwd0", dp1, W_up0, "nt", [F32, BF16], tm=TM // 2,
                                             comm=scatter((g_dn0, 5)), **norm_bwd(mlp_norm_g[0:1], h1, dh2))
    dz = matmul("a_out_bwd", dh1_b, W_out, "nt", F32)
    g_out = wgrad("a_out_wgrad", z, dh1_b)
    da, d_ln_g, d_ln_b, d_w_s, d_b_s = _gate_bwd(gate_saved, dz, a_ln_g, a_ln_b, w_tril, w_tril_t, b_rows)
    g_in, (r_out,) = wgrad("a_in_wgrad", y0, da, comm=scatter((g_out, 1)))
    grad_x, dg_mix0 = matmul("a_in_bwd", da, W_in, "nt", F32, **norm_bwd(mix_norm_g[0:1], h0, dh1, copies=1))

    unused = jnp.zeros((1, 1), F32)
    small_w = [mix_norm_g, mlp_norm_g, final_norm_g, a_ln_g, a_ln_b, a_w_s, a_b_s, rel_bias, unused]
    small_m = [m_mix_norm_g, m_mlp_norm_g, m_final_norm_g, m_a_ln_g, m_a_ln_b, m_a_w_s, m_a_b_s, m_rel_bias, unused]
    small_v = [v_mix_norm_g, v_mlp_norm_g, v_final_norm_g, v_a_ln_g, v_a_ln_b, v_a_w_s, v_a_b_s, v_rel_bias, unused]
    small_g = [jnp.concatenate([dg_mix0, dg_mix1]), jnp.concatenate([dg_mlp0, dg_mlp1]), d_final_g,
               d_ln_g, d_ln_b, d_w_s[None], d_b_s[None, :, :, 0], d_rel_bias, loss_row[:, :1]]
    width = max(D, 128)
    received = [None, [r_out], [r_qkv], [r_bo], [r_up0, r_up1], [r_dn0, r_dn1]]
    plane = [None] + [_sum_pieces(f"sum_pieces{i}", received[i]) for i in range(1, len(w_big))]
    tail = _run_comm("tail_comm", _join_comms(scatter((g_in, 0)), _exchange_sibling(plane[1:]),
                                              _allgather_small(_pack_small(small_g, width))))
    r_in, other, gathered_small = tail[0], [None] + list(tail[1:len(w_big)]), tail[len(w_big)]
    plane[0] = _sum_pieces("sum_pieces0", [r_in])
    (other[0],) = _run_comm("exchange_a_in", _exchange_sibling([plane[0]]))
    big_out = [_adam_pair(f"adam{i}", w_big[i], m_big[i], v_big[i], plane[i], other[i]) for i in range(len(w_big))]

    def unbig(kind):
        return dict(zip(["a_w_in", "a_w_out", "b_w_qkv", "b_w_out", "w_up", "w_down"], [b[kind] for b in big_out]))

    small_out = _adam_small(_pack_small(small_w, width), _pack_small(small_m, width), _pack_small(small_v, width),
                            gathered_small)
    shapes = [w.shape for w in small_w]
    loss = _unpack_small(small_out[0], shapes, width)[-1][0, 0]

    names = ["mix_norm_g", "mlp_norm_g", "final_norm_g", "a_w_in", "a_ln_g", "a_ln_b", "a_w_s", "a_b_s", "a_w_out",
             "b_w_qkv", "b_w_out", "rel_bias", "w_up", "w_down"]
    results = [loss, grad_x.reshape(x.shape)]
    for kind in range(4):
        table = dict(zip(SMALL, _unpack_small(small_out[kind], shapes, width)))
        table.update(unbig(kind))
        results += [table[n] for n in names]
    return tuple(results)
```

```python
import functools
import math

import numpy as np
import jax
import jax.numpy as jnp
from jax import lax
from jax.experimental import pallas as pl
from jax.experimental.pallas import tpu as pltpu

F32 = jnp.float32
BF16 = jnp.bfloat16
MESH = pl.DeviceIdType.MESH
ANY = pl.BlockSpec(memory_space=pl.ANY)

N_CHIPS = 4
N_DEV = 8
VMEM_LIMIT_BYTES = 56 * 1024 * 1024

EPS = 1e-6
NEG_INF = -1e30
CHUNK = 128
GROUP_DIM = 128
HEAD_DIM = 64
ATT_HEADS = 8
ATT_WIDTH = ATT_HEADS * HEAD_DIM
PAIR = 2 * HEAD_DIM
BLK = 128
DILATIONS = (1, 4, 16)
N_BUCKETS = 32
MAX_EXACT = N_BUCKETS // 2
REL_MAX_DISTANCE = 2048

ADAM_LR = 0.001
ADAM_B1 = 0.9
ADAM_B2 = 0.999
ADAM_EPS = 1e-08
ADAM_WD = 0.01
ADAM_STEP = 10

NN = (((1,), (0,)), ((), ()))
NT = (((1,), (1,)), ((), ()))
TN = (((0,), (0,)), ((), ()))


def _params(**kw):
    return pltpu.CompilerParams(vmem_limit_bytes=VMEM_LIMIT_BYTES, **kw)


def _dot(a, b, dims=NN):
    return lax.dot_general(a, b, dims, preferred_element_type=F32)


def _gelu(x):
    return 0.5 * x * (1.0 + lax.erf(x * math.sqrt(0.5)))


def _gelu_grad(x):
    return 0.5 * (1.0 + lax.erf(x * math.sqrt(0.5))) + x * jnp.exp(-0.5 * x * x) * (1.0 / math.sqrt(2.0 * math.pi))


def _mean(x):
    return jnp.mean(x, axis=-1, keepdims=True)


class _Comm:
    def __init__(self, inputs, out_shapes, scratch, start, end, mid=None):
        self.inputs, self.out_shapes, self.scratch = list(inputs), list(out_shapes), list(scratch)
        self.start, self.mid, self.end = start, mid, end


def _run_comm(name, comm):
    n_in, n_out = len(comm.inputs), len(comm.out_shapes)

    def body(*refs):
        parts = refs[:n_in], refs[n_in:n_in + n_out], refs[n_in + n_out:]
        comm.start(*parts)
        if comm.mid is not None:
            comm.mid(*parts)
        comm.end(*parts)

    return pl.pallas_call(
        body, name=name, in_specs=[ANY] * n_in, out_specs=[ANY] * n_out, out_shape=comm.out_shapes,
        scratch_shapes=comm.scratch, compiler_params=_params(),
    )(*comm.inputs)


def _mm(name, a, b, mode, *, tm, tn, tk, outs, epi=None, extras=(), vecs=(), col_sums=0, norm_gain=None, comm=None):
    if mode == "tn":
        K, M = a.shape
    else:
        M, K = a.shape
    N = b.shape[0] if mode == "nt" else b.shape[1]
    tm, tn, tk = min(tm, M), min(tn, N), min(tk, K)
    assert M % tm == 0 and N % tn == 0 and K % tk == 0, (name, M, N, K, tm, tn, tk)
    nk = K // tk
    grid = (M // tm, N // tn, nk)

    if mode == "tn":
        a_spec = pl.BlockSpec((tk, tm), lambda i, j, k: (k, i))
    else:
        a_spec = pl.BlockSpec((tm, tk), lambda i, j, k: (i, k))
    if mode == "nt":
        b_spec = pl.BlockSpec((tn, tk), lambda i, j, k: (j, k))
    else:
        b_spec = pl.BlockSpec((tk, tn), lambda i, j, k: (k, j))
    tile = pl.BlockSpec((tm, tn), lambda i, j, k: (i, j))
    vec = pl.BlockSpec((1, tn), lambda i, j, k: (0, j))
    normed = norm_gain is not None
    assert not normed or (mode == "nn" and nk == 1 and tm % grid[1] == 0)
    assert col_sums == 0 or grid[1] == 1
    out_shapes = [jax.ShapeDtypeStruct((M, N), dtype) for dtype in outs]
    out_specs = [tile for _ in outs]
    extra_specs = [tile for _ in extras] + [vec for _ in vecs]
    if normed:
        part_rows = tm // grid[1]
        last_part = M // part_rows - 1
        a_spec = pl.BlockSpec((tm, K), lambda i, j, k: (0, 0))
        out_shapes.append(jax.ShapeDtypeStruct((M, K), BF16))
        out_specs.append(pl.BlockSpec((part_rows, K), lambda i, j, k: (i * grid[1] + j, 0)))
        extra_specs.append(pl.BlockSpec((1, K), lambda i, j, k: (0, 0)))
        extra_specs.append(pl.BlockSpec((part_rows, K),
                                        lambda i, j, k: (jnp.minimum((i + 1) * grid[1] + j, last_part), 0)))
    out_shapes += [jax.ShapeDtypeStruct((1, N), F32)] * col_sums
    out_specs += [vec] * col_sums
    n_extra, n_out = len(extra_specs), len(out_shapes)
    n_tiles = len(outs)
    n_cin = len(comm.inputs) if comm else 0
    n_cout = len(comm.out_shapes) if comm else 0
    dims = {"nn": NN, "nt": NT, "tn": TN}[mode]
    steps = grid[0] * grid[1] * grid[2]

    def body(*refs):
        a_ref, b_ref = refs[0], refs[1]
        pos = 2
        extra_refs = refs[pos:pos + n_extra]
        pos += n_extra
        comm_in = refs[pos:pos + n_cin]
        pos += n_cin
        out_refs = refs[pos:pos + n_out]
        pos += n_out
        comm_out = refs[pos:pos + n_cout]
        pos += n_cout
        acc_ref = refs[pos] if nk > 1 else None
        pos += nk > 1
        y_refs = refs[pos:pos + 2 * normed]
        comm_sems = refs[pos + 2 * normed:]
        k = pl.program_id(2)
        step = (pl.program_id(0) * grid[1] + pl.program_id(1)) * nk + k

        if comm is not None:
            @pl.when(step == 0)
            def _():
                comm.start(comm_in, comm_out, comm_sems)

        def finish(acc):
            epi_args = [e[...] for e in extra_refs[:n_extra - 2 * normed]]
            res = epi(acc, *epi_args) if epi is not None else (acc,) * n_tiles
            for o, r in zip(out_refs[:n_tiles], res[:n_tiles]):
                o[...] = r.astype(o.dtype)
            if col_sums:
                sums = out_refs[n_out - col_sums:]

                @pl.when(pl.program_id(0) == 0)
                def _():
                    for o in sums:
                        o[...] = jnp.zeros_like(o)

                for o, r in zip(sums, res[n_tiles:]):
                    o[...] += r

        if normed:
            gain_ref, ahead_ref = extra_refs[-2], extra_refs[-1]

            def norm(hv):
                return (hv * lax.rsqrt(_mean(hv * hv) + EPS) * gain_ref[...]).astype(BF16)

            @pl.when(step == 0)
            def _():
                y_refs[0][...] = norm(a_ref[...])

            part_at = pl.ds(pl.multiple_of(pl.program_id(1) * part_rows, part_rows), part_rows)
            for parity in range(2):
                @pl.when(pl.program_id(0) % 2 == parity)
                def _(y_now=y_refs[parity], y_next=y_refs[1 - parity]):
                    finish(_dot(y_now[...], b_ref[...].astype(BF16), dims))
                    out_refs[n_tiles][...] = y_now[part_at, :]
                    y_next[part_at, :] = norm(ahead_ref[...])
        elif nk == 1:
            finish(_dot(a_ref[...].astype(BF16), b_ref[...].astype(BF16), dims))
        else:
            part = _dot(a_ref[...].astype(BF16), b_ref[...].astype(BF16), dims)

            @pl.when(k == 0)
            def _():
                acc_ref[...] = part

            @pl.when(k > 0)
            def _():
                acc_ref[...] += part

            @pl.when(k == nk - 1)
            def _():
                finish(acc_ref[...])

        if comm is not None:
            if comm.mid is not None:
                @pl.when(step == (3 * steps) // 4)
                def _():
                    comm.mid(comm_in, comm_out, comm_sems)

            @pl.when(step == steps - 1)
            def _():
                comm.end(comm_in, comm_out, comm_sems)

    sequential = comm is not None or normed or col_sums > 0
    order = ("arbitrary",) * 3 if sequential else ("parallel", "parallel", "arbitrary")
    scratch = [pltpu.VMEM((tm, tn), F32)] if nk > 1 else []
    if normed:
        scratch += [pltpu.VMEM((tm, K), BF16)] * 2
    res = pl.pallas_call(
        body, name=name, grid=grid,
        in_specs=[a_spec, b_spec] + extra_specs + [ANY] * n_cin,
        out_specs=out_specs + [ANY] * n_cout,
        out_shape=out_shapes + (comm.out_shapes if comm else []),
        scratch_shapes=scratch + (comm.scratch if comm else []),
        compiler_params=_params(dimension_semantics=order),
    )(a, b, *extras, *vecs, *([norm_gain, a] if normed else []), *(comm.inputs if comm else []))
    mm_out = res[0] if n_out == 1 else list(res[:n_out])
    return (mm_out, list(res[n_out:])) if comm else mm_out


def _epi_residual(acc, res):
    return (res + acc,)


def _epi_relu2(acc):
    return (jnp.square(jnp.maximum(acc, 0.0)),)


def _epi_rms_bwd(copies):
    def epi(acc, h, dres, g):
        r = lax.rsqrt(_mean(h * h) + EPS)
        hn = h * r
        dyg = acc * g
        dh = dres + r * (dyg - hn * _mean(dyg * hn))
        return (dh,) * copies + (jnp.sum(acc * hn, axis=0, keepdims=True),)
    return epi


def _epi_loss_head(acc, res, target, g):
    h = res + acc
    r = lax.rsqrt(_mean(h * h) + EPS)
    hn = h * r
    diff = hn * g - target
    loss = 0.5 * jnp.sum(_mean(diff * diff))
    dy = diff * (1.0 / h.shape[-1])
    dyg = dy * g
    dh = r * (dyg - hn * _mean(dyg * hn))
    return dh, dh, jnp.sum(dy * hn, axis=0, keepdims=True), jnp.full((1, h.shape[-1]), loss, F32)


def _epi_relu2_grad(acc, q):
    qf = q.astype(F32)
    return (acc * jnp.where(qf > 0.0, (2.0 * qf) * lax.rsqrt(qf), 0.0),)


def _row_tile(T):
    return min(T, 512)


def _gate_tile(T):
    return min(T, 256)


def _gate_fwd(a, ln_g, ln_b, w_tril, b_rows):
    T, W2 = a.shape
    W = W2 // 2
    G = W // GROUP_DIM
    tr = _gate_tile(T)

    def body(a_ref, lng_ref, lnb_ref, w_ref, b_ref, z_ref):
        u = _gelu(a_ref[:, :W].astype(F32))
        vg = _gelu(a_ref[:, W:].astype(F32))
        xc = vg - _mean(vg)
        vn = xc * lax.rsqrt(_mean(xc * xc) + EPS)
        vl = (vn * lng_ref[...] + lnb_ref[...]).astype(BF16)
        for n in range(tr // CHUNK):
            rows = slice(n * CHUNK, (n + 1) * CHUNK)
            for g in range(G):
                cols = slice(g * GROUP_DIM, (g + 1) * GROUP_DIM)
                gate = _dot(w_ref[g], vl[rows, cols]) + b_ref[g]
                z_ref[rows, cols] = (u[rows, cols] * gate).astype(BF16)

    vec = pl.BlockSpec((1, W), lambda i: (0, 0))
    grp = pl.BlockSpec((G, CHUNK, CHUNK), lambda i: (0, 0, 0))
    return pl.pallas_call(
        body, name="gate_fwd", grid=(T // tr,),
        in_specs=[pl.BlockSpec((tr, W2), lambda i: (i, 0)), vec, vec, grp, grp],
        out_specs=pl.BlockSpec((tr, W), lambda i: (i, 0)),
        out_shape=jax.ShapeDtypeStruct((T, W), BF16),
        compiler_params=_params(dimension_semantics=("parallel",)),
    )(a, ln_g, ln_b, w_tril, b_rows)


def _gate_bwd(a, dz, ln_g, ln_b, w_tril, w_tril_t, b_rows):
    T, W2 = a.shape
    W = W2 // 2
    G = W // GROUP_DIM
    tr = _gate_tile(T)
    steps = T // tr

    def body(a_ref, dz_ref, lng_ref, lnb_ref, w_ref, wt_ref, b_ref, da_ref, dlng_ref, dlnb_ref, dw_ref, dbs_ref, dvl_ref):
        step = pl.program_id(0)

        @pl.when(step == 0)
        def _():
            dlng_ref[...] = jnp.zeros_like(dlng_ref)
            dlnb_ref[...] = jnp.zeros_like(dlnb_ref)
            dw_ref[...] = jnp.zeros_like(dw_ref)
            dbs_ref[...] = jnp.zeros_like(dbs_ref)

        au = a_ref[:, :W].astype(F32)
        av = a_ref[:, W:].astype(F32)
        u = _gelu(au)
        vg = _gelu(av)
        xc = vg - _mean(vg)
        rstd = lax.rsqrt(_mean(xc * xc) + EPS)
        vn = xc * rstd
        lng = lng_ref[...]
        vl = (vn * lng + lnb_ref[...]).astype(BF16)
        du_scale = dz_ref[...] * _gelu_grad(au)
        dgate_all = dz_ref[...] * u
        for n in range(tr // CHUNK):
            rows = slice(n * CHUNK, (n + 1) * CHUNK)
            for g in range(G):
                cols = slice(g * GROUP_DIM, (g + 1) * GROUP_DIM)
                vlg = vl[rows, cols]
                gate = _dot(w_ref[g], vlg) + b_ref[g]
                da_ref[rows, cols] = (du_scale[rows, cols] * gate).astype(BF16)
                dgate = dgate_all[rows, cols]
                dbs_ref[g] += dgate
                dgate_b = dgate.astype(BF16)
                dw_ref[g] += _dot(dgate_b, vlg, NT)
                dvl_ref[rows, cols] = _dot(wt_ref[g], dgate_b)
        dvl = dvl_ref[...]
        dlnb_ref[...] += jnp.sum(dvl, axis=0, keepdims=True)
        dlng_ref[...] += jnp.sum(dvl * vn, axis=0, keepdims=True)
        dvn = dvl * lng
        dvg = rstd * (dvn - _mean(dvn) - vn * _mean(dvn * vn))
        da_ref[:, W:] = (dvg * _gelu_grad(av)).astype(BF16)

        @pl.when(step == steps - 1)
        def _():
            t_idx = lax.broadcasted_iota(jnp.int32, (CHUNK, CHUNK), 0)
            s_idx = lax.broadcasted_iota(jnp.int32, (CHUNK, CHUNK), 1)
            for g in range(G):
                dw_ref[g] = jnp.where(s_idx <= t_idx, dw_ref[g], 0.0)
                dbs_ref[g] = jnp.broadcast_to(jnp.sum(dbs_ref[g], axis=-1, keepdims=True), (CHUNK, CHUNK))

    vec = pl.BlockSpec((1, W), lambda i: (0, 0))
    grp = pl.BlockSpec((G, CHUNK, CHUNK), lambda i: (0, 0, 0))
    return pl.pallas_call(
        body, name="gate_bwd", grid=(steps,),
        in_specs=[pl.BlockSpec((tr, W2), lambda i: (i, 0)), pl.BlockSpec((tr, W), lambda i: (i, 0)),
                  vec, vec, grp, grp, grp],
        out_specs=[pl.BlockSpec((tr, W2), lambda i: (i, 0)), vec, vec, grp, grp],
        out_shape=[jax.ShapeDtypeStruct((T, W2), BF16), jax.ShapeDtypeStruct((1, W), F32),
                   jax.ShapeDtypeStruct((1, W), F32), jax.ShapeDtypeStruct((G, CHUNK, CHUNK), F32),
                   jax.ShapeDtypeStruct((G, CHUNK, CHUNK), F32)],
        scratch_shapes=[pltpu.VMEM((tr, W), F32)],
        compiler_params=_params(dimension_semantics=("arbitrary",)),
    )(a, dz, ln_g, ln_b, w_tril, w_tril_t, b_rows)


def _bucket_map(dilation):
    rel = BLK + np.arange(BLK)[:, None] - np.arange(2 * BLK)[None, :]
    dist = np.clip(rel, 0, BLK) * dilation
    nf = np.maximum(dist, 1).astype(np.float32)
    large = MAX_EXACT + (np.log(nf / np.float32(MAX_EXACT)) / np.float32(math.log(REL_MAX_DISTANCE / MAX_EXACT))
                         * np.float32(N_BUCKETS - MAX_EXACT)).astype(np.int32)
    large = np.minimum(large, N_BUCKETS - 1)
    return np.where(dist < MAX_EXACT, dist, large).astype(np.int32)


def _bucket_maps():
    return jnp.asarray(np.stack([_bucket_map(d) for d in DILATIONS]))


def _bias_build(rel_bias, buckets):
    NG = len(DILATIONS)

    def body(table_ref, bucket_ref, out_ref):
        for g in range(NG):
            bk = bucket_ref[g]
            for h in range(ATT_HEADS):
                out_ref[0, g, h] = jnp.zeros((BLK, 2 * BLK), F32)
            for b in range(N_BUCKETS):
                hit = bk == b
                for h in range(ATT_HEADS):
                    out_ref[0, g, h] = jnp.where(hit, table_ref[b, g * ATT_HEADS + h], out_ref[0, g, h])
            for h in range(ATT_HEADS):
                for first in range(2):
                    out_ref[first, g, h] = jnp.where(_window_mask(first), out_ref[0, g, h], NEG_INF)

    return pl.pallas_call(
        body, name="bias_build",
        in_specs=[pl.BlockSpec(memory_space=pltpu.SMEM), pl.BlockSpec(memory_space=pltpu.VMEM)],
        out_specs=pl.BlockSpec(memory_space=pltpu.VMEM),
        out_shape=jax.ShapeDtypeStruct((2, NG, ATT_HEADS, BLK, 2 * BLK), F32),
        compiler_params=_params(),
    )(rel_bias, buckets)


def _bias_scatter(dbias, buckets):
    NG = len(DILATIONS)

    def body(dbias_ref, bucket_ref, out_ref):
        for g in range(NG):
            bk = bucket_ref[g]
            for b in range(N_BUCKETS):
                hit = bk == b
                for h in range(ATT_HEADS):
                    out_ref[b, g * ATT_HEADS + h] = jnp.sum(jnp.where(hit, dbias_ref[g, h], 0.0))

    return pl.pallas_call(
        body, name="bias_scatter",
        in_specs=[pl.BlockSpec(memory_space=pltpu.VMEM), pl.BlockSpec(memory_space=pltpu.VMEM)],
        out_specs=pl.BlockSpec(memory_space=pltpu.SMEM),
        out_shape=jax.ShapeDtypeStruct((N_BUCKETS, NG * ATT_HEADS), F32),
        compiler_params=_params(),
    )(dbias, buckets)


def _window_mask(first):
    qi = lax.broadcasted_iota(jnp.int32, (BLK, 2 * BLK), 0)
    kj = lax.broadcasted_iota(jnp.int32, (BLK, 2 * BLK), 1)
    rel = BLK + qi - kj
    return (rel >= 0) & (rel <= BLK) & (kj >= BLK * first)


def _head_lanes(hh):
    lane = lax.broadcasted_iota(jnp.int32, (1, PAIR), 1)
    return (lane >= hh * HEAD_DIM) & (lane < (hh + 1) * HEAD_DIM)


ATT_STEP_BLOCKS = 8


def _attn_steps(stride):
    per_step = math.gcd(stride, ATT_STEP_BLOCKS)
    return per_step, stride // per_step


def _attn_fwd(name, g, qkv, qc, kc, vc, bias, stride):
    T = qkv.shape[0]
    per_step, lag = _attn_steps(stride)
    chained = stride == 1
    if chained:
        per_step, lag = min(ATT_STEP_BLOCKS, T // BLK), 1
    rows = per_step * BLK
    scale = HEAD_DIM ** -0.5

    def body(q_ref, kp_ref, kc_ref, vp_ref, vc_ref, bias_ref, out_ref, *chain):
        step = pl.program_id(0)
        low = _head_lanes(0)
        if chained:
            for cat, before, now in zip(chain, (kp_ref, vp_ref), (kc_ref, vc_ref)):
                cat[:BLK, :] = before[...]
                cat[BLK:, :] = now[...]

        def block(j, carry):
            at = pl.ds(pl.multiple_of(j * BLK, BLK), BLK)
            if chained:
                first = ((step == 0) & (j == 0)).astype(jnp.int32)
                after = pl.ds(pl.multiple_of((j + 1) * BLK, BLK), BLK)
                keys = lambda cols: jnp.concatenate([chain[0][at, cols], chain[0][after, cols]], axis=0)
                values = lambda cols: jnp.concatenate([chain[1][at, cols], chain[1][after, cols]], axis=0)
            else:
                first = (step < lag).astype(jnp.int32)
                keys = lambda cols: jnp.concatenate([kp_ref[at, cols], kc_ref[at, cols]], axis=0)
                values = lambda cols: jnp.concatenate([vp_ref[at, cols], vc_ref[at, cols]], axis=0)
            for hp in range(ATT_HEADS // 2):
                cols = slice(hp * PAIR, (hp + 1) * PAIR)
                qp = q_ref[at, cols]
                kk = keys(cols)
                vv = values(cols)
                o_h, lse_h = [], []
                for hh in range(2):
                    qm = jnp.where(_head_lanes(hh), qp, jnp.zeros_like(qp))
                    s = _dot(qm, kk, NT) * scale
                    logits = s + bias_ref[first, 2 * hp + hh]
                    m = jnp.max(logits, axis=-1, keepdims=True)
                    p = jnp.exp(logits - m)
                    den = jnp.sum(p, axis=-1, keepdims=True)
                    o_h.append(_dot(p.astype(BF16), vv) / den)
                    lse_h.append(m + jnp.log(den))
                out_ref[at, cols] = jnp.where(low, o_h[0], o_h[1])
                out_ref[at, slice(ATT_WIDTH + hp * PAIR, ATT_WIDTH + (hp + 1) * PAIR)] = (
                    jnp.where(low, lse_h[0], lse_h[1]))
            return carry

        lax.fori_loop(0, per_step, block, 0)

    def cur(c):
        return pl.BlockSpec((rows, ATT_WIDTH), lambda s: (s, c))

    def prev(c):
        if chained:
            return pl.BlockSpec((BLK, ATT_WIDTH), lambda s: (jnp.maximum(s * per_step - 1, 0), c))
        return pl.BlockSpec((rows, ATT_WIDTH), lambda s: (jnp.maximum(s - lag, 0), c))

    return pl.pallas_call(
        body, name=name, grid=(T // rows,),
        in_specs=[cur(qc), prev(kc), cur(kc), prev(vc), cur(vc),
                  pl.BlockSpec((2, None, ATT_HEADS, BLK, 2 * BLK), lambda s: (0, g, 0, 0, 0))],
        out_specs=pl.BlockSpec((rows, 2 * ATT_WIDTH), lambda s: (s, 0)),
        out_shape=jax.ShapeDtypeStruct((T, 2 * ATT_WIDTH), F32),
        scratch_shapes=[pltpu.VMEM((rows + BLK, ATT_WIDTH), BF16)] * 2 if chained else [],
        compiler_params=_params(dimension_semantics=("parallel",)),
    )(qkv, qkv, qkv, qkv, qkv, bias)


def _permute_f32(p, x):
    hi = x.astype(BF16)
    rest = x - hi.astype(F32)
    mid = rest.astype(BF16)
    low = (rest - mid.astype(F32)).astype(BF16)
    return _dot(p, hi) + _dot(p, mid) + _dot(p, low)


def _attn_merge(parts):
    T = parts[0].shape[0]
    rows = min(T, REORDER_ROWS)
    n = len(parts)
    width = 2 * PAIR
    ncol = ATT_WIDTH // width

    def body(*refs):
        p_refs, o_refs, l_refs = refs[:n], refs[n:2 * n], refs[2 * n:3 * n]
        o_ref, lse_ref = refs[3 * n], refs[3 * n + 1]

        def positions(ref, g, start):
            d = DILATIONS[g]
            if d == 1:
                return ref[start:start + REORDER_TILE, :]
            span, per = BLK * d, REORDER_TILE // d
            base, t = start // span * span, start % span // REORDER_TILE
            chunks = [ref[base + r * BLK + t * per:base + r * BLK + (t + 1) * per, :] for r in range(d)]
            return _permute_f32(p_refs[g][...], jnp.concatenate(chunks, axis=0))

        for start in range(0, rows, REORDER_TILE):
            ls = [positions(l_refs[g], g, start) for g in range(n)]
            m = functools.reduce(jnp.maximum, ls)
            es = [jnp.exp(l - m) for l in ls]
            tot = functools.reduce(lambda x, y: x + y, es)
            acc = functools.reduce(lambda x, y: x + y, [e * positions(o_refs[g], g, start) for g, e in enumerate(es)])
            o_ref[start:start + REORDER_TILE, :] = (acc / tot).astype(BF16)
            lse_ref[start:start + REORDER_TILE, :] = m + jnp.log(tot)

    matrix = pl.BlockSpec((REORDER_TILE, REORDER_TILE), lambda w, c: (0, 0))
    col = pl.BlockSpec((rows, width), lambda w, c: (w, c))
    col_lse = pl.BlockSpec((rows, width), lambda w, c: (w, ncol + c))
    return pl.pallas_call(
        body, name="attn_merge", grid=(T // rows, ncol),
        in_specs=[matrix] * n + [col] * n + [col_lse] * n, out_specs=[col, col],
        out_shape=[jax.ShapeDtypeStruct((T, ATT_WIDTH), BF16), jax.ShapeDtypeStruct((T, ATT_WIDTH), F32)],
        compiler_params=_params(dimension_semantics=("parallel", "parallel")),
    )(*[_reorder_matrix(max(d, 2), True) for d in DILATIONS], *parts, *parts)


def _attn_bwd(name, g, qkv, qc, kc, vc, do, o, lse, bias, stride, comm=None):
    T = qkv.shape[0]
    per_step, lag = _attn_steps(stride)
    rows = per_step * BLK
    steps = T // rows
    scale = HEAD_DIM ** -0.5
    n_cin = len(comm.inputs) if comm else 0
    n_cout = len(comm.out_shapes) if comm else 0
    assert comm is None or comm.mid is None

    def body(*refs):
        q_ref, kp_ref, kc_ref, vp_ref, vc_ref, do_ref, o_ref, lse_ref, bias_ref = refs[:9]
        comm_in = refs[9:9 + n_cin]
        dq_ref, dkv_ref, db_ref = refs[9 + n_cin:12 + n_cin]
        comm_out = refs[12 + n_cin:12 + n_cin + n_cout]
        carry_k, carry_v = refs[12 + n_cin + n_cout:14 + n_cin + n_cout]
        comm_sems = refs[14 + n_cin + n_cout:]
        step = pl.program_id(0)

        if comm is not None:
            @pl.when(step == 0)
            def _():
                comm.start(comm_in, comm_out, comm_sems)

            @pl.when(step == steps + lag - 1)
            def _():
                comm.end(comm_in, comm_out, comm_sems)

        slot0 = (step % lag) * per_step

        @pl.when(step == 0)
        def _():
            db_ref[...] = jnp.zeros_like(db_ref)
            carry_k[...] = jnp.zeros_like(carry_k)
            carry_v[...] = jnp.zeros_like(carry_v)

        @pl.when(step >= steps)
        def _():
            def flush(j, carry):
                at = pl.ds(pl.multiple_of(j * BLK, BLK), BLK)
                dkv_ref[at, :ATT_WIDTH] = carry_k[slot0 + j].astype(BF16)
                dkv_ref[at, ATT_WIDTH:] = carry_v[slot0 + j].astype(BF16)
                return carry

            lax.fori_loop(0, per_step, flush, 0)

        @pl.when(step < steps)
        def _():
            first = (step < lag).astype(jnp.int32)

            def block(j, carry):
                at = pl.ds(pl.multiple_of(j * BLK, BLK), BLK)
                ck_ref = carry_k.at[slot0 + j]
                cv_ref = carry_v.at[slot0 + j]
                for hp in range(ATT_HEADS // 2):
                    cols = slice(hp * PAIR, (hp + 1) * PAIR)
                    qp = q_ref[at, cols]
                    kk = jnp.concatenate([kp_ref[at, cols], kc_ref[at, cols]], axis=0)
                    vv = jnp.concatenate([vp_ref[at, cols], vc_ref[at, cols]], axis=0)
                    dop = do_ref[at, cols]
                    lsep = lse_ref[at, cols]
                    prod = dop.astype(F32) * o_ref[at, cols].astype(F32)
                    dq = jnp.zeros((BLK, PAIR), F32)
                    dk = jnp.zeros((2 * BLK, PAIR), F32)
                    dv = jnp.zeros((2 * BLK, PAIR), F32)
                    for hh in range(2):
                        lanes = _head_lanes(hh)
                        qm = jnp.where(lanes, qp, jnp.zeros_like(qp))
                        dom = jnp.where(lanes, dop, jnp.zeros_like(dop))
                        km = jnp.where(lanes, kk, jnp.zeros_like(kk))
                        delta = jnp.sum(jnp.where(lanes, prod, 0.0), axis=-1, keepdims=True)
                        lse_h = jnp.max(jnp.where(lanes, lsep, NEG_INF), axis=-1, keepdims=True)
                        s = _dot(qm, kk, NT) * scale
                        logits = s + bias_ref[first, 2 * hp + hh]
                        p = jnp.exp(logits - lse_h)
                        dv += _dot(p.astype(BF16), dom, TN)
                        ds = p * (_dot(dom, vv, NT) - delta)
                        db_ref[2 * hp + hh] += ds
                        dss = (ds * scale).astype(BF16)
                        dq += _dot(dss, km)
                        dk += _dot(dss, qm, TN)
                    dq_ref[at, cols] = dq.astype(BF16)
                    dkv_ref[at, cols] = (ck_ref[:, cols] + dk[:BLK]).astype(BF16)
                    dkv_ref[at, slice(ATT_WIDTH + hp * PAIR, ATT_WIDTH + (hp + 1) * PAIR)] = (
                        cv_ref[:, cols] + dv[:BLK]).astype(BF16)
                    ck_ref[:, cols] = dk[BLK:]
                    cv_ref[:, cols] = dv[BLK:]
                return carry

            lax.fori_loop(0, per_step, block, 0)

    last = steps - 1

    def cur(c):
        return pl.BlockSpec((rows, ATT_WIDTH), lambda s: (jnp.minimum(s, last), c))

    def prev(c):
        return pl.BlockSpec((rows, ATT_WIDTH), lambda s: (jnp.clip(s - lag, 0, last), c))

    dbias_shape = (ATT_HEADS, BLK, 2 * BLK)
    res = pl.pallas_call(
        body, name=name, grid=(steps + lag,),
        in_specs=[cur(qc), prev(kc), cur(kc), prev(vc), cur(vc), cur(0), cur(0), cur(0),
                  pl.BlockSpec((2, None, ATT_HEADS, BLK, 2 * BLK), lambda s: (0, g, 0, 0, 0))] + [ANY] * n_cin,
        out_specs=[cur(0), pl.BlockSpec((rows, 2 * ATT_WIDTH), lambda s: (jnp.clip(s - lag, 0, last), 0)),
                   pl.BlockSpec(dbias_shape, lambda s: (0, 0, 0))] + [ANY] * n_cout,
        out_shape=[jax.ShapeDtypeStruct((T, ATT_WIDTH), BF16), jax.ShapeDtypeStruct((T, 2 * ATT_WIDTH), BF16),
                   jax.ShapeDtypeStruct(dbias_shape, F32)] + (comm.out_shapes if comm else []),
        scratch_shapes=[pltpu.VMEM((stride, BLK, ATT_WIDTH), F32), pltpu.VMEM((stride, BLK, ATT_WIDTH), F32)]
        + (comm.scratch if comm else []),
        compiler_params=_params(dimension_semantics=("arbitrary",)),
    )(qkv, qkv, qkv, qkv, qkv, do, o, lse, bias, *(comm.inputs if comm else []))
    return res[0], res[1], res[2], list(res[3:])


REORDER_TILE = 256
REORDER_ROWS = 2048


def _reorder_matrix(d, inverse):
    per = REORDER_TILE // d
    p = np.zeros((REORDER_TILE, REORDER_TILE), np.float32)
    for src in range(REORDER_TILE):
        i, r = divmod(src, d)
        p[r * per + i, src] = 1.0
    return jnp.asarray(p.T if inverse else p, dtype=BF16)


def _reorder_rows(name, src, d, inverse, *, src_col=0, col_stride=1, ncols=1, dst=None, dst_col=0, dst_stride=1,
                  dst_blocks=None):
    T = src.shape[0]
    dtype = src.dtype
    span = BLK * d
    rows = max(span, min(T, REORDER_ROWS))
    per = REORDER_TILE // d
    tiles = span // REORDER_TILE
    dst_blocks = ncols if dst_blocks is None else dst_blocks

    def apply(p, x):
        return _dot(p, x).astype(BF16) if dtype == BF16 else _permute_f32(p, x)

    def body(*refs):
        p_ref, x_ref, o_ref = refs[0], refs[1], refs[-1]
        if d == 1:
            o_ref[...] = x_ref[...]
            return
        for s in range(rows // span):
            for t in range(tiles):
                base = s * span
                tile_rows = slice(base + t * REORDER_TILE, base + (t + 1) * REORDER_TILE)
                chunk = lambda r: slice(base + r * BLK + t * per, base + r * BLK + (t + 1) * per)
                if inverse:
                    gathered = jnp.concatenate([x_ref[chunk(r), :] for r in range(d)], axis=0)
                    o_ref[tile_rows, :] = apply(p_ref[...], gathered)
                else:
                    y = apply(p_ref[...], x_ref[tile_rows, :])
                    for r in range(d):
                        o_ref[chunk(r), :] = y[r * per:(r + 1) * per]

    in_specs = [pl.BlockSpec((REORDER_TILE, REORDER_TILE), lambda w, k: (0, 0)),
                pl.BlockSpec((rows, ATT_WIDTH), lambda w, k: (w, src_col + col_stride * k))]
    operands = [_reorder_matrix(max(d, 2), inverse), src]
    aliases = {}
    if dst is not None:
        in_specs.append(ANY)
        operands.append(dst)
        aliases = {2: 0}
    return pl.pallas_call(
        body, name=name, grid=(T // rows, ncols), in_specs=in_specs,
        out_specs=pl.BlockSpec((rows, ATT_WIDTH), lambda w, k: (w, dst_col + dst_stride * k)),
        out_shape=jax.ShapeDtypeStruct((T, dst_blocks * ATT_WIDTH), dtype),
        input_output_aliases=aliases,
        compiler_params=_params(dimension_semantics=("parallel", "parallel")),
    )(*operands)


def _group_qkv(qkv, g, d):
    NG = len(DILATIONS)
    if d == 1:
        return qkv, (g, NG + g, 2 * NG + g)
    return _reorder_rows(f"qkv_to_residues{g}", qkv, d, False, src_col=g, col_stride=NG, ncols=3), (0, 1, 2)


def _attention_fwd(qkv, bias):
    T = qkv.shape[0]
    parts = []
    for g, d in enumerate(DILATIONS):
        src, (qc, kc, vc) = _group_qkv(qkv, g, d)
        parts.append(_attn_fwd(f"attn_fwd_{g}", g, src, qc, kc, vc, bias, d))
    return _attn_merge(parts)


def _attention_bwd(qkv, do, o, lse, bias, comms):
    NG = len(DILATIONS)
    dqkv, dbs, carried = None, [], []
    for g, d in enumerate(DILATIONS):
        src, (qc, kc, vc) = _group_qkv(qkv, g, d)
        do_g, o_g, lse_g = do, o, lse
        if d > 1:
            do_g = _reorder_rows(f"do_to_residues{g}", do, d, False)
            o_g = _reorder_rows(f"o_to_residues{g}", o, d, False)
            lse_g = _reorder_rows(f"lse_to_residues{g}", lse, d, False)
        dq, dkv, db, sent = _attn_bwd(f"attn_bwd_{g}", g, src, qc, kc, vc, do_g, o_g, lse_g, bias, d, comm=comms[g])
        dqkv = _reorder_rows(f"dq_to_positions{g}", dq, d, True, dst=dqkv, dst_col=g, dst_blocks=3 * NG)
        dqkv = _reorder_rows(f"dkv_to_positions{g}", dkv, d, True, ncols=2, dst=dqkv, dst_col=NG + g, dst_stride=NG,
                             dst_blocks=3 * NG)
        dbs.append(db)
        carried.append(sent)
    return dqkv, jnp.stack(dbs), carried


def _other_chips(x, y):
    return [(1 - x, y), (x, 1 - y), (1 - x, 1 - y)]


def _shard_region(ref, shape, by_cols, chip, rows=None):
    R, C = shape
    start, size = (0, R) if rows is None else rows
    if by_cols:
        return ref.at[pl.ds(start, size), pl.ds(chip * C, C)]
    return ref.at[pl.ds(chip * R + start, size), :]


def _gather_weights(entries):
    n = len(entries)
    shapes = [e[0].shape[1:] for e in entries]

    def places(ins, outs, sems):
        send_sems, recv_sems, local_sems = sems
        x, y, c = lax.axis_index("x"), lax.axis_index("y"), lax.axis_index("c")

        def landing(f, px, py, pc):
            R = shapes[f][0]
            return _shard_region(outs[f], shapes[f], entries[f][2], 2 * px + py, rows=(pc * (R // 2), R // 2))

        def copy(f, k, block, to, src=None):
            dst = landing(f, *block)
            return pltpu.make_async_remote_copy(
                src_ref=dst if src is None else src, dst_ref=dst,
                send_sem=send_sems.at[6 * f + k], recv_sem=recv_sems.at[6 * f + k],
                device_id=to, device_id_type=MESH)

        def mine(f):
            dst = _shard_region(outs[f], shapes[f], entries[f][2], 2 * x + y)
            return pltpu.make_async_copy(ins[f].at[entries[f][1]], dst, local_sems.at[f])

        def first(f, j):
            R = shapes[f][0]
            src = ins[f].at[entries[f][1], pl.ds(c * (R // 2), R // 2), :]
            return copy(f, j, (x, y, c), (*_other_chips(x, y)[j], c), src=src)

        return x, y, c, copy, mine, first

    def start(ins, outs, sems):
        _, _, _, _, mine, first = places(ins, outs, sems)
        for f in range(n):
            mine(f).start()
        for j in range(3):
            for f in range(n):
                first(f, j).start()

    def mid(ins, outs, sems):
        x, y, c, copy, _, _ = places(ins, outs, sems)
        for j, chip in enumerate(_other_chips(x, y)):
            for f in range(n):
                copy(f, j, (*chip, c), (x, y, c)).wait_recv()
                copy(f, 3 + j, (*chip, c), (x, y, 1 - c)).start()

    def end(ins, outs, sems):
        x, y, c, copy, mine, first = places(ins, outs, sems)
        for j, chip in enumerate(_other_chips(x, y)):
            for f in range(n):
                copy(f, 3 + j, (*chip, 1 - c), (x, y, c)).wait_recv()
        for j, chip in enumerate(_other_chips(x, y)):
            for f in range(n):
                first(f, j).wait_send()
                copy(f, 3 + j, (*chip, c), (x, y, 1 - c)).wait_send()
        for f in range(n):
            mine(f).wait()

    def whole(f):
        R, C = shapes[f]
        return (R, N_CHIPS * C) if entries[f][2] else (N_CHIPS * R, C)

    return _Comm(
        [e[0] for e in entries], [jax.ShapeDtypeStruct(whole(f), BF16) for f in range(n)],
        [pltpu.SemaphoreType.DMA((6 * n,)), pltpu.SemaphoreType.DMA((6 * n,)), pltpu.SemaphoreType.DMA((n,))],
        start, end, mid)


def _scatter_grads(entries):
    n = len(entries)

    def copies(ins, outs, sems):
        send_sems, recv_sems, local_sems = sems
        x, y, c = lax.axis_index("x"), lax.axis_index("y"), lax.axis_index("c")
        me = 2 * x + y

        def piece(f, chip):
            return _shard_region(ins[f], entries[f][1], entries[f][2], chip)

        mine = [pltpu.make_async_copy(piece(f, me), outs[f].at[me], local_sems.at[f]) for f in range(n)]
        sends = [pltpu.make_async_remote_copy(
            src_ref=piece(f, 2 * px + py), dst_ref=outs[f].at[me],
            send_sem=send_sems.at[3 * f + j], recv_sem=recv_sems.at[3 * f + j],
            device_id=(px, py, c), device_id_type=MESH)
            for j, (px, py) in enumerate(_other_chips(x, y)) for f in range(n)]
        return mine, sends

    def start(ins, outs, sems):
        mine, sends = copies(ins, outs, sems)
        for cp in mine + sends:
            cp.start()

    def end(ins, outs, sems):
        mine, sends = copies(ins, outs, sems)
        for cp in sends + mine:
            cp.wait()

    return _Comm(
        [e[0] for e in entries], [jax.ShapeDtypeStruct((N_CHIPS,) + tuple(e[1]), BF16) for e in entries],
        [pltpu.SemaphoreType.DMA((3 * n,)), pltpu.SemaphoreType.DMA((3 * n,)), pltpu.SemaphoreType.DMA((n,))],
        start, end)


def _exchange_sibling(parts):
    n = len(parts)

    def copies(ins, outs, sems):
        send_sems, recv_sems = sems
        sibling = (lax.axis_index("x"), lax.axis_index("y"), 1 - lax.axis_index("c"))
        return [pltpu.make_async_remote_copy(src_ref=ins[i], dst_ref=outs[i], send_sem=send_sems.at[i],
                                             recv_sem=recv_sems.at[i], device_id=sibling, device_id_type=MESH)
                for i in range(n)]

    def start(ins, outs, sems):
        for cp in copies(ins, outs, sems):
            cp.start()

    def end(ins, outs, sems):
        for cp in copies(ins, outs, sems):
            cp.wait()

    return _Comm(parts, [jax.ShapeDtypeStruct(s.shape, s.dtype) for s in parts],
                 [pltpu.SemaphoreType.DMA((n,)), pltpu.SemaphoreType.DMA((n,))], start, end)


def _allgather_small(block):
    m_per, ncol = block.shape

    def places(ins, outs, sems):
        send_sems, recv_sems, local_sem = sems
        x, y, c = lax.axis_index("x"), lax.axis_index("y"), lax.axis_index("c")

        def rows(px, py, pc):
            return outs[0].at[4 * px + 2 * py + pc]

        def copy(k, block_of, to, src=None):
            return pltpu.make_async_remote_copy(
                src_ref=rows(*block_of) if src is None else src, dst_ref=rows(*block_of),
                send_sem=send_sems.at[k], recv_sem=recv_sems.at[k], device_id=to, device_id_type=MESH)

        mine = pltpu.make_async_copy(ins[0], rows(x, y, c), local_sem.at[0])
        first = [copy(0, (x, y, c), (x, y, 1 - c), src=ins[0])]
        first += [copy(1 + j, (x, y, c), (*chip, c), src=ins[0]) for j, chip in enumerate(_other_chips(x, y))]
        passed = [copy(4 + j, (*chip, c), (x, y, 1 - c)) for j, chip in enumerate(_other_chips(x, y))]
        return x, y, c, copy, mine, first, passed

    def start(ins, outs, sems):
        _, _, _, _, mine, first, _ = places(ins, outs, sems)
        for cp in [mine] + first:
            cp.start()

    def mid(ins, outs, sems):
        x, y, c, copy, _, _, passed = places(ins, outs, sems)
        for j, chip in enumerate(_other_chips(x, y)):
            copy(1 + j, (*chip, c), (x, y, c)).wait_recv()
            passed[j].start()

    def end(ins, outs, sems):
        x, y, c, copy, mine, first, passed = places(ins, outs, sems)
        copy(0, (x, y, 1 - c), (x, y, c)).wait_recv()
        for j, chip in enumerate(_other_chips(x, y)):
            copy(4 + j, (*chip, 1 - c), (x, y, c)).wait_recv()
        for cp in first + passed:
            cp.wait_send()
        mine.wait()

    return _Comm([block], [jax.ShapeDtypeStruct((N_DEV, m_per, ncol), block.dtype)],
                 [pltpu.SemaphoreType.DMA((7,)), pltpu.SemaphoreType.DMA((7,)), pltpu.SemaphoreType.DMA((1,))],
                 start, end, mid)


def _join_comms(*progs):
    def split(parts, counts):
        out, pos = [], 0
        for n in counts:
            out.append(parts[pos:pos + n])
            pos += n
        return out

    def phase(which):
        def run(ins, outs, sems):
            args = zip(split(ins, [len(p.inputs) for p in progs]), split(outs, [len(p.out_shapes) for p in progs]),
                       split(sems, [len(p.scratch) for p in progs]))
            for p, (i, o, s) in zip(progs, args):
                fn = getattr(p, which)
                if fn is not None:
                    fn(i, o, s)
        return run

    return _Comm([a for p in progs for a in p.inputs], [s for p in progs for s in p.out_shapes],
                 [s for p in progs for s in p.scratch], phase("start"), phase("end"), phase("mid"))


def _adamw(w, g, m, v):
    m = ADAM_B1 * m + (1.0 - ADAM_B1) * g
    v = ADAM_B2 * v + (1.0 - ADAM_B2) * jnp.square(g)
    m_hat = m / (1.0 - ADAM_B1 ** ADAM_STEP)
    v_hat = v / (1.0 - ADAM_B2 ** ADAM_STEP)
    delta = -ADAM_LR * (m_hat / (jnp.sqrt(v_hat) + ADAM_EPS) + ADAM_WD * w)
    return delta, m, v


def _flat_tile(rows):
    return min(rows, 512)


def _sum_pieces(name, layers):
    L = len(layers)
    P, R, C = layers[0].shape
    tr = _flat_tile(R)

    def body(*refs):
        out_ref = refs[L]
        for l in range(L):
            @pl.when(pl.program_id(0) == l)
            def _(p_ref=refs[l]):
                acc = p_ref[0].astype(F32)
                for j in range(1, P):
                    acc = acc + p_ref[j].astype(F32)
                out_ref[...] = acc

    return pl.pallas_call(
        body, name=name, grid=(L, R // tr),
        in_specs=[pl.BlockSpec((P, tr, C), lambda l, i: (0, i, 0)) for _ in range(L)],
        out_specs=pl.BlockSpec((None, tr, C), lambda l, i: (l, i, 0)),
        out_shape=jax.ShapeDtypeStruct((L, R, C), F32),
        compiler_params=_params(dimension_semantics=("parallel", "parallel")),
    )(*layers)


def _adam_pair(name, w, m, v, part_a, part_b):
    L, R, C = w.shape
    tr = _flat_tile(R)

    def body(w_ref, m_ref, v_ref, a_ref, b_ref, g_ref, d_ref, nm_ref, nv_ref):
        g = a_ref[...] + b_ref[...]
        g_ref[...] = g
        d_ref[...], nm_ref[...], nv_ref[...] = _adamw(w_ref[...], g, m_ref[...], v_ref[...])

    row = pl.BlockSpec((None, tr, C), lambda l, i: (l, i, 0))
    return pl.pallas_call(
        body, name=name, grid=(L, R // tr),
        in_specs=[row] * 5, out_specs=[row] * 4,
        out_shape=[jax.ShapeDtypeStruct((L, R, C), F32)] * 4,
        compiler_params=_params(dimension_semantics=("parallel", "parallel")),
    )(w, m, v, part_a, part_b)


def _adam_small(w, m, v, gathered):
    R, C = w.shape

    def body(w_ref, m_ref, v_ref, p_ref, g_ref, d_ref, nm_ref, nv_ref):
        g = p_ref[0]
        for j in range(1, N_DEV):
            g = g + p_ref[j]
        g_ref[...] = g
        d_ref[...], nm_ref[...], nv_ref[...] = _adamw(w_ref[...], g, m_ref[...], v_ref[...])

    return pl.pallas_call(
        body, name="adam_small",
        out_shape=[jax.ShapeDtypeStruct((R, C), F32)] * 4,
        compiler_params=_params(),
    )(w, m, v, gathered)


SMALL = ("mix_norm_g", "mlp_norm_g", "final_norm_g", "a_ln_g", "a_ln_b", "a_w_s", "a_b_s", "rel_bias")


def _pack_small(arrays, width):
    rows = []
    for a in arrays:
        flat = a.reshape(-1)
        pad = (-flat.shape[0]) % width
        rows.append(jnp.pad(flat, (0, pad)).reshape(-1, width))
    block = jnp.concatenate(rows, axis=0)
    return jnp.pad(block, ((0, (-block.shape[0]) % 8), (0, 0)))


def _unpack_small(block, shapes, width):
    out, row = [], 0
    for shape in shapes:
        size = int(np.prod(shape))
        nrows = -(-size // width)
        out.append(block[row:row + nrows].reshape(-1)[:size].reshape(shape))
        row += nrows
    return out


def kernel(x, mix_norm_g, mlp_norm_g, final_norm_g, a_w_in, a_ln_g, a_ln_b, a_w_s, a_b_s, a_w_out, b_w_qkv, b_w_out, rel_bias, w_up, w_down, loss_target, m_mix_norm_g, m_mlp_norm_g, m_final_norm_g, m_a_w_in, m_a_ln_g, m_a_ln_b, m_a_w_s, m_a_b_s, m_a_w_out, m_b_w_qkv, m_b_w_out, m_rel_bias, m_w_up, m_w_down, v_mix_norm_g, v_mlp_norm_g, v_final_norm_g, v_a_w_in, v_a_ln_g, v_a_ln_b, v_a_w_s, v_a_b_s, v_a_w_out, v_b_w_qkv, v_b_w_out, v_rel_bias, v_w_up, v_w_down):
    T, D = x.shape[1], x.shape[2]
    h0 = x.reshape(T, D)
    target = loss_target.reshape(T, D)
    G = a_w_s.shape[1]

    w_big = [a_w_in, a_w_out, b_w_qkv, b_w_out, w_up, w_down]
    m_big = [m_a_w_in, m_a_w_out, m_b_w_qkv, m_b_w_out, m_w_up, m_w_down]
    v_big = [v_a_w_in, v_a_w_out, v_b_w_qkv, v_b_w_out, v_w_up, v_w_down]
    by_cols = [True, False, True, True, True, False]
    s_in, s_out, s_qkv, s_bo, s_up, s_dn = [w.astype(BF16) for w in w_big]
    W_in, W_out = _run_comm("gather_a", _gather_weights([(s_in, 0, True), (s_out, 0, False)]))

    tril = jnp.tril(jnp.ones((CHUNK, CHUNK), dtype=bool))
    w_tril = jnp.where(tril[None], a_w_s[0], 0.0).astype(BF16)
    w_tril_t = jnp.swapaxes(w_tril, 1, 2)
    b_rows = jnp.broadcast_to(a_b_s[0][:, :, None], (G, CHUNK, CHUNK))
    buckets = _bucket_maps()
    bias = _bias_build(rel_bias, buckets)

    QKV = s_qkv.shape[2] * N_CHIPS
    TM = 1024
    TK_WGRAD = 4096

    def matmul(name, a, b, mode, out, tm=TM, tn=1024, **kw):
        outs = out if isinstance(out, list) else [out]
        return _mm(name, a, b, mode, tm=tm, tn=tn, tk=a.shape[1], outs=outs, **kw)

    def norm_bwd(layer_gain, h, dres, copies=2):
        return dict(epi=_epi_rms_bwd(copies), extras=(h, dres), vecs=(layer_gain,), col_sums=1)

    def wgrad(name, a, b, tn=1024, tk=TK_WGRAD, comm=None):
        return _mm(name, a, b, "tn", tm=1024, tn=tn, tk=tk, outs=[BF16], comm=comm)

    def scatter(*which):
        return _scatter_grads([(g, w_big[i].shape[1:], by_cols[i]) for g, i in which])

    (a_pre, y0), (W_up0,) = matmul("a_in", h0, W_in, "nn", BF16, norm_gain=mix_norm_g[0:1],
                                   comm=_gather_weights([(s_up, 0, True)]))
    z = _gate_fwd(a_pre, a_ln_g, a_ln_b, w_tril, b_rows)
    h1 = matmul("a_out", z, W_out, "nn", F32, epi=_epi_residual, extras=(h0,))
    (q1, y1), (W_dn0,) = matmul("mlp_up0", h1, W_up0, "nn", BF16, epi=_epi_relu2, norm_gain=mlp_norm_g[0:1],
                                comm=_gather_weights([(s_dn, 0, False)]))
    h2, (W_qkv, W_bo) = matmul("mlp_down0", q1, W_dn0, "nn", F32, tm=TM // 2, epi=_epi_residual, extras=(h1,),
                               comm=_gather_weights([(s_qkv, 0, True), (s_bo, 0, True)]))
    (qkv, y2), (W_up1,) = matmul("b_qkv", h2, W_qkv, "nn", BF16, tn=QKV // 4, norm_gain=mix_norm_g[1:2],
                                 comm=_gather_weights([(s_up, 1, True)]))
    o, lse = _attention_fwd(qkv, bias)
    h3 = matmul("b_out", o, W_bo, "nn", F32, epi=_epi_residual, extras=(h2,))
    (q3, y3), (W_dn1,) = matmul("mlp_up1", h3, W_up1, "nn", BF16, epi=_epi_relu2, norm_gain=mlp_norm_g[1:2],
                                comm=_gather_weights([(s_dn, 1, False)]))
    dh4, dh4_b, d_final_g, loss_row = matmul("mlp_down1", q3, W_dn1, "nn", [F32, BF16], tm=TM // 2, epi=_epi_loss_head,
                                             extras=(h3, target), vecs=(final_norm_g.reshape(1, D),), col_sums=2)

    dp3 = matmul("mlp_down_bwd1", dh4_b, W_dn1, "nt", BF16, epi=_epi_relu2_grad, extras=(q3,))
    g_dn1 = wgrad("mlp_down_wgrad1", q3, dh4_b)
    g_up1 = wgrad("mlp_up_wgrad1", y3, dp3)
    dh3, dh3_b, dg_mlp1 = matmul("mlp_up_bwd1", dp3, W_up1, "nt", [F32, BF16], tm=TM // 2,
                                 **norm_bwd(mlp_norm_g[1:2], h3, dh4))
    do = matmul("b_out_bwd", dh3_b, W_bo, "nt", BF16)
    g_bo = wgrad("b_out_wgrad", o, dh3_b)
    dqkv, dbias, ((r_dn1,), (r_up1,), (r_bo,)) = _attention_bwd(
        qkv, do, o, lse, bias, [scatter((g_dn1, 5)), scatter((g_up1, 4)), scatter((g_bo, 3))])
    d_rel_bias = _bias_scatter(dbias, buckets)
    dh2, dh2_b, dg_mix1 = matmul("b_qkv_bwd", dqkv, W_qkv, "nt", [F32, BF16], tm=TM // 2,
                                 **norm_bwd(mix_norm_g[1:2], h2, dh3))
    g_qkv = wgrad("b_qkv_wgrad", y2, dqkv, tn=QKV // 3, tk=TK_WGRAD // 2)
    dp1, (r_qkv,) = matmul("mlp_down_bwd0", dh2_b, W_dn0, "nt", BF16, epi=_epi_relu2_grad, extras=(q1,),
                           comm=scatter((g_qkv, 2)))
    g_up0 = wgrad("mlp_up_wgrad0", y1, dp1)
    g_dn0, (r_up0,) = wgrad("mlp_down_wgrad0", q1, dh2_b, comm=scatter((g_up0, 4)))
    (dh1, dh1_b, dg_mlp0), (r_dn0,) = matmul("mlp_up_bwd0", dp1, W_up0, "nt", [F32, BF16], tm=TM // 2,
                                             comm=scatter((g_dn0, 5)), **norm_bwd(mlp_norm_g[0:1], h1, dh2))
    dz = matmul("a_out_bwd", dh1_b, W_out, "nt", F32)
    g_out = wgrad("a_out_wgrad", z, dh1_b)
    da, d_ln_g, d_ln_b, d_w_s, d_b_s = _gate_bwd(a_pre, dz, a_ln_g, a_ln_b, w_tril, w_tril_t, b_rows)
    received = [None, None, [r_qkv], [r_bo], [r_up0, r_up1], [r_dn0, r_dn1]]
    plane = [None, None] + [_sum_pieces(f"sum_pieces{i}", received[i]) for i in range(2, len(w_big))]
    g_in, carried = wgrad("a_in_wgrad", y0, da, comm=_join_comms(scatter((g_out, 1)), _exchange_sibling(plane[2:])))
    r_out, other = carried[0], [None, None] + list(carried[1:])
    grad_x, dg_mix0 = matmul("a_in_bwd", da, W_in, "nt", F32, **norm_bwd(mix_norm_g[0:1], h0, dh1, copies=1))

    unused = jnp.zeros((1, 1), F32)
    small_w = [mix_norm_g, mlp_norm_g, final_norm_g, a_ln_g, a_ln_b, a_w_s, a_b_s, rel_bias, unused]
    small_m = [m_mix_norm_g, m_mlp_norm_g, m_final_norm_g, m_a_ln_g, m_a_ln_b, m_a_w_s, m_a_b_s, m_rel_bias, unused]
    small_v = [v_mix_norm_g, v_mlp_norm_g, v_final_norm_g, v_a_ln_g, v_a_ln_b, v_a_w_s, v_a_b_s, v_rel_bias, unused]
    small_g = [jnp.concatenate([dg_mix0, dg_mix1]), jnp.concatenate([dg_mlp0, dg_mlp1]), d_final_g,
               d_ln_g, d_ln_b, d_w_s[None], d_b_s[None, :, :, 0], d_rel_bias, loss_row[:, :1]]
    width = max(D, 128)
    plane[1] = _sum_pieces("sum_pieces1", [r_out])
    r_in, other[1], gathered_small = _run_comm("tail_comm", _join_comms(
        scatter((g_in, 0)), _exchange_sibling([plane[1]]), _allgather_small(_pack_small(small_g, width))))
    plane[0] = _sum_pieces("sum_pieces0", [r_in])
    (other[0],) = _run_comm("exchange_a_in", _exchange_sibling([plane[0]]))
    big_out = [_adam_pair(f"adam{i}", w_big[i], m_big[i], v_big[i], plane[i], other[i]) for i in range(len(w_big))]

    def unbig(kind):
        return dict(zip(["a_w_in", "a_w_out", "b_w_qkv", "b_w_out", "w_up", "w_down"], [b[kind] for b in big_out]))

    small_out = _adam_small(_pack_small(small_w, width), _pack_small(small_m, width), _pack_small(small_v, width),
                            gathered_small)
    shapes = [w.shape for w in small_w]
    loss = _unpack_small(small_out[0], shapes, width)[-1][0, 0]

    names = ["mix_norm_g", "mlp_norm_g", "final_norm_g", "a_w_in", "a_ln_g", "a_ln_b", "a_w_s", "a_b_s", "a_w_out",
             "b_w_qkv", "b_w_out", "rel_bias", "w_up", "w_down"]
    results = [loss, grad_x.reshape(x.shape)]
    for kind in range(4):
        table = dict(zip(SMALL, _unpack_small(small_out[kind], shapes, width)))
        table.update(unbig(kind))
        results += [table[n] for n in names]
    return tuple(results)
```

```python
import functools
import math

import numpy as np
import jax
import jax.numpy as jnp
from jax import lax
from jax.experimental import pallas as pl
from jax.experimental.pallas import tpu as pltpu

F32 = jnp.float32
BF16 = jnp.bfloat16
MESH = pl.DeviceIdType.MESH
ANY = pl.BlockSpec(memory_space=pl.ANY)

N_CHIPS = 4
N_DEV = 8
VMEM_LIMIT_BYTES = 56 * 1024 * 1024

EPS = 1e-6
NEG_INF = -1e30
CHUNK = 128
GROUP_DIM = 128
HEAD_DIM = 64
ATT_HEADS = 8
ATT_WIDTH = ATT_HEADS * HEAD_DIM
PAIR = 2 * HEAD_DIM
BLK = 128
DILATIONS = (1, 4, 16)
N_BUCKETS = 32
MAX_EXACT = N_BUCKETS // 2
REL_MAX_DISTANCE = 2048

ADAM_LR = 0.001
ADAM_B1 = 0.9
ADAM_B2 = 0.999
ADAM_EPS = 1e-08
ADAM_WD = 0.01
ADAM_STEP = 10

NN = (((1,), (0,)), ((), ()))
NT = (((1,), (1,)), ((), ()))
TN = (((0,), (0,)), ((), ()))


def _params(**kw):
    return pltpu.CompilerParams(vmem_limit_bytes=VMEM_LIMIT_BYTES, **kw)


def _dot(a, b, dims=NN):
    return lax.dot_general(a, b, dims, preferred_element_type=F32)


def _gelu(x):
    return 0.5 * x * (1.0 + lax.erf(x * math.sqrt(0.5)))


def _gelu_grad(x):
    return 0.5 * (1.0 + lax.erf(x * math.sqrt(0.5))) + x * jnp.exp(-0.5 * x * x) * (1.0 / math.sqrt(2.0 * math.pi))


def _mean(x):
    return jnp.mean(x, axis=-1, keepdims=True)


class _Comm:
    def __init__(self, inputs, out_shapes, scratch, start, end, mid=None):
        self.inputs, self.out_shapes, self.scratch = list(inputs), list(out_shapes), list(scratch)
        self.start, self.mid, self.end = start, mid, end


def _run_comm(name, comm):
    n_in, n_out = len(comm.inputs), len(comm.out_shapes)

    def body(*refs):
        parts = refs[:n_in], refs[n_in:n_in + n_out], refs[n_in + n_out:]
        comm.start(*parts)
        if comm.mid is not None:
            comm.mid(*parts)
        comm.end(*parts)

    return pl.pallas_call(
        body, name=name, in_specs=[ANY] * n_in, out_specs=[ANY] * n_out, out_shape=comm.out_shapes,
        scratch_shapes=comm.scratch, compiler_params=_params(),
    )(*comm.inputs)


def _mm(name, a, b, mode, *, tm, tn, tk, outs, epi=None, extras=(), vecs=(), col_sums=0, norm_gain=None, comm=None):
    if mode == "tn":
        K, M = a.shape
    else:
        M, K = a.shape
    N = b.shape[0] if mode == "nt" else b.shape[1]
    tm, tn, tk = min(tm, M), min(tn, N), min(tk, K)
    assert M % tm == 0 and N % tn == 0 and K % tk == 0, (name, M, N, K, tm, tn, tk)
    nk = K // tk
    grid = (M // tm, N // tn, nk)

    if mode == "tn":
        a_spec = pl.BlockSpec((tk, tm), lambda i, j, k: (k, i))
    else:
        a_spec = pl.BlockSpec((tm, tk), lambda i, j, k: (i, k))
    if mode == "nt":
        b_spec = pl.BlockSpec((tn, tk), lambda i, j, k: (j, k))
    else:
        b_spec = pl.BlockSpec((tk, tn), lambda i, j, k: (k, j))
    tile = pl.BlockSpec((tm, tn), lambda i, j, k: (i, j))
    vec = pl.BlockSpec((1, tn), lambda i, j, k: (0, j))
    normed = norm_gain is not None
    assert not normed or (mode == "nn" and nk == 1 and tm % grid[1] == 0)
    assert col_sums == 0 or grid[1] == 1
    out_shapes = [jax.ShapeDtypeStruct((M, N), dtype) for dtype in outs]
    out_specs = [tile for _ in outs]
    extra_specs = [tile for _ in extras] + [vec for _ in vecs]
    if normed:
        part_rows = tm // grid[1]
        last_part = M // part_rows - 1
        a_spec = pl.BlockSpec((tm, K), lambda i, j, k: (0, 0))
        out_shapes.append(jax.ShapeDtypeStruct((M, K), BF16))
        out_specs.append(pl.BlockSpec((part_rows, K), lambda i, j, k: (i * grid[1] + j, 0)))
        extra_specs.append(pl.BlockSpec((1, K), lambda i, j, k: (0, 0)))
        extra_specs.append(pl.BlockSpec((part_rows, K),
                                        lambda i, j, k: (jnp.minimum((i + 1) * grid[1] + j, last_part), 0)))
    out_shapes += [jax.ShapeDtypeStruct((1, N), F32)] * col_sums
    out_specs += [vec] * col_sums
    n_extra, n_out = len(extra_specs), len(out_shapes)
    n_tiles = len(outs)
    n_cin = len(comm.inputs) if comm else 0
    n_cout = len(comm.out_shapes) if comm else 0
    dims = {"nn": NN, "nt": NT, "tn": TN}[mode]
    steps = grid[0] * grid[1] * grid[2]

    def body(*refs):
        a_ref, b_ref = refs[0], refs[1]
        pos = 2
        extra_refs = refs[pos:pos + n_extra]
        pos += n_extra
        comm_in = refs[pos:pos + n_cin]
        pos += n_cin
        out_refs = refs[pos:pos + n_out]
        pos += n_out
        comm_out = refs[pos:pos + n_cout]
        pos += n_cout
        acc_ref = refs[pos] if nk > 1 else None
        pos += nk > 1
        y_refs = refs[pos:pos + 2 * normed]
        comm_sems = refs[pos + 2 * normed:]
        k = pl.program_id(2)
        step = (pl.program_id(0) * grid[1] + pl.program_id(1)) * nk + k

        if comm is not None:
            @pl.when(step == 0)
            def _():
                comm.start(comm_in, comm_out, comm_sems)

        def finish(acc):
            epi_args = [e[...] for e in extra_refs[:n_extra - 2 * normed]]
            res = epi(acc, *epi_args) if epi is not None else (acc,) * n_tiles
            for o, r in zip(out_refs[:n_tiles], res[:n_tiles]):
                o[...] = r.astype(o.dtype)
            if col_sums:
                sums = out_refs[n_out - col_sums:]

                @pl.when(pl.program_id(0) == 0)
                def _():
                    for o in sums:
                        o[...] = jnp.zeros_like(o)

                for o, r in zip(sums, res[n_tiles:]):
                    o[...] += r

        if normed:
            gain_ref, ahead_ref = extra_refs[-2], extra_refs[-1]

            def norm(hv):
                return (hv * lax.rsqrt(_mean(hv * hv) + EPS) * gain_ref[...]).astype(BF16)

            @pl.when(step == 0)
            def _():
                y_refs[0][...] = norm(a_ref[...])

            part_at = pl.ds(pl.multiple_of(pl.program_id(1) * part_rows, part_rows), part_rows)
            for parity in range(2):
                @pl.when(pl.program_id(0) % 2 == parity)
                def _(y_now=y_refs[parity], y_next=y_refs[1 - parity]):
                    finish(_dot(y_now[...], b_ref[...].astype(BF16), dims))
                    out_refs[n_tiles][...] = y_now[part_at, :]
                    y_next[part_at, :] = norm(ahead_ref[...])
        elif nk == 1:
            finish(_dot(a_ref[...].astype(BF16), b_ref[...].astype(BF16), dims))
        else:
            part = _dot(a_ref[...].astype(BF16), b_ref[...].astype(BF16), dims)

            @pl.when(k == 0)
            def _():
                acc_ref[...] = part

            @pl.when(k > 0)
            def _():
                acc_ref[...] += part

            @pl.when(k == nk - 1)
            def _():
                finish(acc_ref[...])

        if comm is not None:
            if comm.mid is not None:
                @pl.when(step == (3 * steps) // 4)
                def _():
                    comm.mid(comm_in, comm_out, comm_sems)

            @pl.when(step == steps - 1)
            def _():
                comm.end(comm_in, comm_out, comm_sems)

    sequential = comm is not None or normed or col_sums > 0
    order = ("arbitrary",) * 3 if sequential else ("parallel", "parallel", "arbitrary")
    scratch = [pltpu.VMEM((tm, tn), F32)] if nk > 1 else []
    if normed:
        scratch += [pltpu.VMEM((tm, K), BF16)] * 2
    res = pl.pallas_call(
        body, name=name, grid=grid,
        in_specs=[a_spec, b_spec] + extra_specs + [ANY] * n_cin,
        out_specs=out_specs + [ANY] * n_cout,
        out_shape=out_shapes + (comm.out_shapes if comm else []),
        scratch_shapes=scratch + (comm.scratch if comm else []),
        compiler_params=_params(dimension_semantics=order),
    )(a, b, *extras, *vecs, *([norm_gain, a] if normed else []), *(comm.inputs if comm else []))
    mm_out = res[0] if n_out == 1 else list(res[:n_out])
    return (mm_out, list(res[n_out:])) if comm else mm_out


def _epi_residual(acc, res):
    return (res + acc,)


def _epi_relu2(acc):
    return (jnp.square(jnp.maximum(acc, 0.0)),)


def _epi_rms_bwd(copies):
    def epi(acc, h, dres, g):
        r = lax.rsqrt(_mean(h * h) + EPS)
        hn = h * r
        dyg = acc * g
        dh = dres + r * (dyg - hn * _mean(dyg * hn))
        return (dh,) * copies + (jnp.sum(acc * hn, axis=0, keepdims=True),)
    return epi


def _epi_loss_head(acc, res, target, g):
    h = res + acc
    r = lax.rsqrt(_mean(h * h) + EPS)
    hn = h * r
    diff = hn * g - target
    loss = 0.5 * jnp.sum(_mean(diff * diff))
    dy = diff * (1.0 / h.shape[-1])
    dyg = dy * g
    dh = r * (dyg - hn * _mean(dyg * hn))
    return dh, dh, jnp.sum(dy * hn, axis=0, keepdims=True), jnp.full((1, h.shape[-1]), loss, F32)


def _epi_relu2_grad(acc, q):
    qf = q.astype(F32)
    return (acc * jnp.where(qf > 0.0, (2.0 * qf) * lax.rsqrt(qf), 0.0),)


def _row_tile(T):
    return min(T, 512)


def _gate_tile(T):
    return min(T, 256)


def _gate_fwd(a, ln_g, ln_b, w_tril, b_rows):
    T, W2 = a.shape
    W = W2 // 2
    G = W // GROUP_DIM
    tr = _gate_tile(T)

    def body(a_ref, lng_ref, lnb_ref, w_ref, b_ref, z_ref):
        u = _gelu(a_ref[:, :W].astype(F32))
        vg = _gelu(a_ref[:, W:].astype(F32))
        xc = vg - _mean(vg)
        vn = xc * lax.rsqrt(_mean(xc * xc) + EPS)
        vl = (vn * lng_ref[...] + lnb_ref[...]).astype(BF16)
        for n in range(tr // CHUNK):
            rows = slice(n * CHUNK, (n + 1) * CHUNK)
            for g in range(G):
                cols = slice(g * GROUP_DIM, (g + 1) * GROUP_DIM)
                gate = _dot(w_ref[g], vl[rows, cols]) + b_ref[g]
                z_ref[rows, cols] = (u[rows, cols] * gate).astype(BF16)

    vec = pl.BlockSpec((1, W), lambda i: (0, 0))
    grp = pl.BlockSpec((G, CHUNK, CHUNK), lambda i: (0, 0, 0))
    return pl.pallas_call(
        body, name="gate_fwd", grid=(T // tr,),
        in_specs=[pl.BlockSpec((tr, W2), lambda i: (i, 0)), vec, vec, grp, grp],
        out_specs=pl.BlockSpec((tr, W), lambda i: (i, 0)),
        out_shape=jax.ShapeDtypeStruct((T, W), BF16),
        compiler_params=_params(dimension_semantics=("parallel",)),
    )(a, ln_g, ln_b, w_tril, b_rows)


def _gate_bwd(a, dz, ln_g, ln_b, w_tril, w_tril_t, b_rows):
    T, W2 = a.shape
    W = W2 // 2
    G = W // GROUP_DIM
    tr = _gate_tile(T)
    steps = T // tr

    def body(a_ref, dz_ref, lng_ref, lnb_ref, w_ref, wt_ref, b_ref, da_ref, dlng_ref, dlnb_ref, dw_ref, dbs_ref, dvl_ref):
        step = pl.program_id(0)

        @pl.when(step == 0)
        def _():
            dlng_ref[...] = jnp.zeros_like(dlng_ref)
            dlnb_ref[...] = jnp.zeros_like(dlnb_ref)
            dw_ref[...] = jnp.zeros_like(dw_ref)
            dbs_ref[...] = jnp.zeros_like(dbs_ref)

        au = a_ref[:, :W].astype(F32)
        av = a_ref[:, W:].astype(F32)
        u = _gelu(au)
        vg = _gelu(av)
        xc = vg - _mean(vg)
        rstd = lax.rsqrt(_mean(xc * xc) + EPS)
        vn = xc * rstd
        lng = lng_ref[...]
        vl = (vn * lng + lnb_ref[...]).astype(BF16)
        du_scale = dz_ref[...] * _gelu_grad(au)
        dgate_all = dz_ref[...] * u
        for n in range(tr // CHUNK):
            rows = slice(n * CHUNK, (n + 1) * CHUNK)
            for g in range(G):
                cols = slice(g * GROUP_DIM, (g + 1) * GROUP_DIM)
                vlg = vl[rows, cols]
                gate = _dot(w_ref[g], vlg) + b_ref[g]
                da_ref[rows, cols] = (du_scale[rows, cols] * gate).astype(BF16)
                dgate = dgate_all[rows, cols]
                dbs_ref[g] += dgate
                dgate_b = dgate.astype(BF16)
                dw_ref[g] += _dot(dgate_b, vlg, NT)
                dvl_ref[rows, cols] = _dot(wt_ref[g], dgate_b)
        dvl = dvl_ref[...]
        dlnb_ref[...] += jnp.sum(dvl, axis=0, keepdims=True)
        dlng_ref[...] += jnp.sum(dvl * vn, axis=0, keepdims=True)
        dvn = dvl * lng
        dvg = rstd * (dvn - _mean(dvn) - vn * _mean(dvn * vn))
        da_ref[:, W:] = (dvg * _gelu_grad(av)).astype(BF16)

        @pl.when(step == steps - 1)
        def _():
            t_idx = lax.broadcasted_iota(jnp.int32, (CHUNK, CHUNK), 0)
            s_idx = lax.broadcasted_iota(jnp.int32, (CHUNK, CHUNK), 1)
            for g in range(G):
                dw_ref[g] = jnp.where(s_idx <= t_idx, dw_ref[g], 0.0)
                dbs_ref[g] = jnp.broadcast_to(jnp.sum(dbs_ref[g], axis=-1, keepdims=True), (CHUNK, CHUNK))

    vec = pl.BlockSpec((1, W), lambda i: (0, 0))
    grp = pl.BlockSpec((G, CHUNK, CHUNK), lambda i: (0, 0, 0))
    return pl.pallas_call(
        body, name="gate_bwd", grid=(steps,),
        in_specs=[pl.BlockSpec((tr, W2), lambda i: (i, 0)), pl.BlockSpec((tr, W), lambda i: (i, 0)),
                  vec, vec, grp, grp, grp],
        out_specs=[pl.BlockSpec((tr, W2), lambda i: (i, 0)), vec, vec, grp, grp],
        out_shape=[jax.ShapeDtypeStruct((T, W2), BF16), jax.ShapeDtypeStruct((1, W), F32),
                   jax.ShapeDtypeStruct((1, W), F32), jax.ShapeDtypeStruct((G, CHUNK, CHUNK), F32),
                   jax.ShapeDtypeStruct((G, CHUNK, CHUNK), F32)],
        scratch_shapes=[pltpu.VMEM((tr, W), F32)],
        compiler_params=_params(dimension_semantics=("arbitrary",)),
    )(a, dz, ln_g, ln_b, w_tril, w_tril_t, b_rows)


def _bucket_map(dilation):
    rel = BLK + np.arange(BLK)[:, None] - np.arange(2 * BLK)[None, :]
    dist = np.clip(rel, 0, BLK) * dilation
    nf = np.maximum(dist, 1).astype(np.float32)
    large = MAX_EXACT + (np.log(nf / np.float32(MAX_EXACT)) / np.float32(math.log(REL_MAX_DISTANCE / MAX_EXACT))
                         * np.float32(N_BUCKETS - MAX_EXACT)).astype(np.int32)
    large = np.minimum(large, N_BUCKETS - 1)
    return np.where(dist < MAX_EXACT, dist, large).astype(np.int32)


def _bucket_maps():
    return jnp.asarray(np.stack([_bucket_map(d) for d in DILATIONS]))


def _bias_build(rel_bias, buckets):
    NG = len(DILATIONS)

    def body(table_ref, bucket_ref, out_ref):
        for g in range(NG):
            bk = bucket_ref[g]
            for h in range(ATT_HEADS):
                out_ref[0, g, h] = jnp.zeros((BLK, 2 * BLK), F32)
            for b in range(N_BUCKETS):
                hit = bk == b
                for h in range(ATT_HEADS):
                    out_ref[0, g, h] = jnp.where(hit, table_ref[b, g * ATT_HEADS + h], out_ref[0, g, h])
            for h in range(ATT_HEADS):
                for first in range(2):
                    out_ref[first, g, h] = jnp.where(_window_mask(first), out_ref[0, g, h], NEG_INF)

    return pl.pallas_call(
        body, name="bias_build",
        in_specs=[pl.BlockSpec(memory_space=pltpu.SMEM), pl.BlockSpec(memory_space=pltpu.VMEM)],
        out_specs=pl.BlockSpec(memory_space=pltpu.VMEM),
        out_shape=jax.ShapeDtypeStruct((2, NG, ATT_HEADS, BLK, 2 * BLK), F32),
        compiler_params=_params(),
    )(rel_bias, buckets)


def _bias_scatter(dbias, buckets):
    NG = len(DILATIONS)

    def body(dbias_ref, bucket_ref, out_ref):
        for g in range(NG):
            bk = bucket_ref[g]
            for b in range(N_BUCKETS):
                hit = bk == b
                for h in range(ATT_HEADS):
                    out_ref[b, g * ATT_HEADS + h] = jnp.sum(jnp.where(hit, dbias_ref[g, h], 0.0))

    return pl.pallas_call(
        body, name="bias_scatter",
        in_specs=[pl.BlockSpec(memory_space=pltpu.VMEM), pl.BlockSpec(memory_space=pltpu.VMEM)],
        out_specs=pl.BlockSpec(memory_space=pltpu.SMEM),
        out_shape=jax.ShapeDtypeStruct((N_BUCKETS, NG * ATT_HEADS), F32),
        compiler_params=_params(),
    )(dbias, buckets)


def _window_mask(first):
    qi = lax.broadcasted_iota(jnp.int32, (BLK, 2 * BLK), 0)
    kj = lax.broadcasted_iota(jnp.int32, (BLK, 2 * BLK), 1)
    rel = BLK + qi - kj
    return (rel >= 0) & (rel <= BLK) & (kj >= BLK * first)


def _head_lanes(hh):
    lane = lax.broadcasted_iota(jnp.int32, (1, PAIR), 1)
    return (lane >= hh * HEAD_DIM) & (lane < (hh + 1) * HEAD_DIM)


ATT_STEP_BLOCKS = 8


def _attn_steps(stride):
    per_step = math.gcd(stride, ATT_STEP_BLOCKS)
    return per_step, stride // per_step


def _attn_fwd(name, g, qkv, qc, kc, vc, bias, stride):
    T = qkv.shape[0]
    per_step, lag = _attn_steps(stride)
    chained = stride == 1
    if chained:
        per_step, lag = min(ATT_STEP_BLOCKS, T // BLK), 1
    rows = per_step * BLK
    scale = HEAD_DIM ** -0.5

    def body(q_ref, kp_ref, kc_ref, vp_ref, vc_ref, bias_ref, out_ref, *chain):
        step = pl.program_id(0)
        low = _head_lanes(0)
        if chained:
            for cat, before, now in zip(chain, (kp_ref, vp_ref), (kc_ref, vc_ref)):
                cat[:BLK, :] = before[...]
                cat[BLK:, :] = now[...]

        def block(j, carry):
            at = pl.ds(pl.multiple_of(j * BLK, BLK), BLK)
            if chained:
                first = ((step == 0) & (j == 0)).astype(jnp.int32)
                after = pl.ds(pl.multiple_of((j + 1) * BLK, BLK), BLK)
                keys = lambda cols: jnp.concatenate([chain[0][at, cols], chain[0][after, cols]], axis=0)
                values = lambda cols: jnp.concatenate([chain[1][at, cols], chain[1][after, cols]], axis=0)
            else:
                first = (step < lag).astype(jnp.int32)
                keys = lambda cols: jnp.concatenate([kp_ref[at, cols], kc_ref[at, cols]], axis=0)
                values = lambda cols: jnp.concatenate([vp_ref[at, cols], vc_ref[at, cols]], axis=0)
            for hp in range(ATT_HEADS // 2):
                cols = slice(hp * PAIR, (hp + 1) * PAIR)
                qp = q_ref[at, cols]
                kk = keys(cols)
                vv = values(cols)
                o_h, lse_h = [], []
                for hh in range(2):
                    qm = jnp.where(_head_lanes(hh), qp, jnp.zeros_like(qp))
                    s = _dot(qm, kk, NT) * scale
                    logits = s + bias_ref[first, 2 * hp + hh]
                    m = jnp.max(logits, axis=-1, keepdims=True)
                    p = jnp.exp(logits - m)
                    den = jnp.sum(p, axis=-1, keepdims=True)
                    o_h.append(_dot(p.astype(BF16), vv) / den)
                    lse_h.append(m + jnp.log(den))
                out_ref[at, cols] = jnp.where(low, o_h[0], o_h[1])
                out_ref[at, slice(ATT_WIDTH + hp * PAIR, ATT_WIDTH + (hp + 1) * PAIR)] = (
                    jnp.where(low, lse_h[0], lse_h[1]))
            return carry

        lax.fori_loop(0, per_step, block, 0)

    def cur(c):
        return pl.BlockSpec((rows, ATT_WIDTH), lambda s: (s, c))

    def prev(c):
        if chained:
            return pl.BlockSpec((BLK, ATT_WIDTH), lambda s: (jnp.maximum(s * per_step - 1, 0), c))
        return pl.BlockSpec((rows, ATT_WIDTH), lambda s: (jnp.maximum(s - lag, 0), c))

    return pl.pallas_call(
        body, name=name, grid=(T // rows,),
        in_specs=[cur(qc), prev(kc), cur(kc), prev(vc), cur(vc),
                  pl.BlockSpec((2, None, ATT_HEADS, BLK, 2 * BLK), lambda s: (0, g, 0, 0, 0))],
        out_specs=pl.BlockSpec((rows, 2 * ATT_WIDTH), lambda s: (s, 0)),
        out_shape=jax.ShapeDtypeStruct((T, 2 * ATT_WIDTH), F32),
        scratch_shapes=[pltpu.VMEM((rows + BLK, ATT_WIDTH), BF16)] * 2 if chained else [],
        compiler_params=_params(dimension_semantics=("parallel",)),
    )(qkv, qkv, qkv, qkv, qkv, bias)


def _permute_f32(p, x):
    hi = x.astype(BF16)
    rest = x - hi.astype(F32)
    mid = rest.astype(BF16)
    low = (rest - mid.astype(F32)).astype(BF16)
    return _dot(p, hi) + _dot(p, mid) + _dot(p, low)


def _attn_merge(parts):
    T = parts[0].shape[0]
    rows = min(T, REORDER_ROWS)
    n = len(parts)
    width = 2 * PAIR
    ncol = ATT_WIDTH // width

    def body(*refs):
        p_refs, o_refs, l_refs = refs[:n], refs[n:2 * n], refs[2 * n:3 * n]
        o_ref, lse_ref = refs[3 * n], refs[3 * n + 1]

        def positions(ref, g, start):
            d = DILATIONS[g]
            if d == 1:
                return ref[start:start + REORDER_TILE, :]
            span, per = BLK * d, REORDER_TILE // d
            base, t = start // span * span, start % span // REORDER_TILE
            chunks = [ref[base + r * BLK + t * per:base + r * BLK + (t + 1) * per, :] for r in range(d)]
            return _permute_f32(p_refs[g][...], jnp.concatenate(chunks, axis=0))

        for start in range(0, rows, REORDER_TILE):
            ls = [positions(l_refs[g], g, start) for g in range(n)]
            m = functools.reduce(jnp.maximum, ls)
            es = [jnp.exp(l - m) for l in ls]
            tot = functools.reduce(lambda x, y: x + y, es)
            acc = functools.reduce(lambda x, y: x + y, [e * positions(o_refs[g], g, start) for g, e in enumerate(es)])
            o_ref[start:start + REORDER_TILE, :] = (acc / tot).astype(BF16)
            lse_ref[start:start + REORDER_TILE, :] = m + jnp.log(tot)

    matrix = pl.BlockSpec((REORDER_TILE, REORDER_TILE), lambda w, c: (0, 0))
    col = pl.BlockSpec((rows, width), lambda w, c: (w, c))
    col_lse = pl.BlockSpec((rows, width), lambda w, c: (w, ncol + c))
    return pl.pallas_call(
        body, name="attn_merge", grid=(T // rows, ncol),
        in_specs=[matrix] * n + [col] * n + [col_lse] * n, out_specs=[col, col],
        out_shape=[jax.ShapeDtypeStruct((T, ATT_WIDTH), BF16), jax.ShapeDtypeStruct((T, ATT_WIDTH), F32)],
        compiler_params=_params(dimension_semantics=("parallel", "parallel")),
    )(*[_reorder_matrix(max(d, 2), True) for d in DILATIONS], *parts, *parts)


def _attn_bwd(name, g, qkv, qc, kc, vc, do, o, lse, bias, stride, comm=None):
    T = qkv.shape[0]
    per_step, lag = _attn_steps(stride)
    rows = per_step * BLK
    steps = T // rows
    scale = HEAD_DIM ** -0.5
    n_cin = len(comm.inputs) if comm else 0
    n_cout = len(comm.out_shapes) if comm else 0
    assert comm is None or comm.mid is None

    def body(*refs):
        q_ref, kp_ref, kc_ref, vp_ref, vc_ref, do_ref, o_ref, lse_ref, bias_ref = refs[:9]
        comm_in = refs[9:9 + n_cin]
        dq_ref, dkv_ref, db_ref = refs[9 + n_cin:12 + n_cin]
        comm_out = refs[12 + n_cin:12 + n_cin + n_cout]
        carry_k, carry_v = refs[12 + n_cin + n_cout:14 + n_cin + n_cout]
        comm_sems = refs[14 + n_cin + n_cout:]
        step = pl.program_id(0)

        if comm is not None:
            @pl.when(step == 0)
            def _():
                comm.start(comm_in, comm_out, comm_sems)

            @pl.when(step == steps + lag - 1)
            def _():
                comm.end(comm_in, comm_out, comm_sems)

        slot0 = (step % lag) * per_step

        @pl.when(step == 0)
        def _():
            db_ref[...] = jnp.zeros_like(db_ref)
            carry_k[...] = jnp.zeros_like(carry_k)
            carry_v[...] = jnp.zeros_like(carry_v)

        @pl.when(step >= steps)
        def _():
            def flush(j, carry):
                at = pl.ds(pl.multiple_of(j * BLK, BLK), BLK)
                dkv_ref[at, :ATT_WIDTH] = carry_k[slot0 + j].astype(BF16)
                dkv_ref[at, ATT_WIDTH:] = carry_v[slot0 + j].astype(BF16)
                return carry

            lax.fori_loop(0, per_step, flush, 0)

        @pl.when(step < steps)
        def _():
            first = (step < lag).astype(jnp.int32)

            def block(j, carry):
                at = pl.ds(pl.multiple_of(j * BLK, BLK), BLK)
                ck_ref = carry_k.at[slot0 + j]
                cv_ref = carry_v.at[slot0 + j]
                for hp in range(ATT_HEADS // 2):
                    cols = slice(hp * PAIR, (hp + 1) * PAIR)
                    qp = q_ref[at, cols]
                    kk = jnp.concatenate([kp_ref[at, cols], kc_ref[at, cols]], axis=0)
                    vv = jnp.concatenate([vp_ref[at, cols], vc_ref[at, cols]], axis=0)
                    dop = do_ref[at, cols]
                    lsep = lse_ref[at, cols]
                    prod = dop.astype(F32) * o_ref[at, cols].astype(F32)
                    dq = jnp.zeros((BLK, PAIR), F32)
                    dk = jnp.zeros((2 * BLK, PAIR), F32)
                    dv = jnp.zeros((2 * BLK, PAIR), F32)
                    for hh in range(2):
                        lanes = _head_lanes(hh)
                        qm = jnp.where(lanes, qp, jnp.zeros_like(qp))
                        dom = jnp.where(lanes, dop, jnp.zeros_like(dop))
                        km = jnp.where(lanes, kk, jnp.zeros_like(kk))
                        delta = jnp.sum(jnp.where(lanes, prod, 0.0), axis=-1, keepdims=True)
                        lse_h = jnp.max(jnp.where(lanes, lsep, NEG_INF), axis=-1, keepdims=True)
                        s = _dot(qm, kk, NT) * scale
                        logits = s + bias_ref[first, 2 * hp + hh]
                        p = jnp.exp(logits - lse_h)
                        dv += _dot(p.astype(BF16), dom, TN)
                        ds = p * (_dot(dom, vv, NT) - delta)
                        db_ref[2 * hp + hh] += ds
                        dss = (ds * scale).astype(BF16)
                        dq += _dot(dss, km)
                        dk += _dot(dss, qm, TN)
                    dq_ref[at, cols] = dq.astype(BF16)
                    dkv_ref[at, cols] = (ck_ref[:, cols] + dk[:BLK]).astype(BF16)
                    dkv_ref[at, slice(ATT_WIDTH + hp * PAIR, ATT_WIDTH + (hp + 1) * PAIR)] = (
                        cv_ref[:, cols] + dv[:BLK]).astype(BF16)
                    ck_ref[:, cols] = dk[BLK:]
                    cv_ref[:, cols] = dv[BLK:]
                return carry

            lax.fori_loop(0, per_step, block, 0)

    last = steps - 1

    def cur(c):
        return pl.BlockSpec((rows, ATT_WIDTH), lambda s: (jnp.minimum(s, last), c))

    def prev(c):
        return pl.BlockSpec((rows, ATT_WIDTH), lambda s: (jnp.clip(s - lag, 0, last), c))

    dbias_shape = (ATT_HEADS, BLK, 2 * BLK)
    res = pl.pallas_call(
        body, name=name, grid=(steps + lag,),
        in_specs=[cur(qc), prev(kc), cur(kc), prev(vc), cur(vc), cur(0), cur(0), cur(0),
                  pl.BlockSpec((2, None, ATT_HEADS, BLK, 2 * BLK), lambda s: (0, g, 0, 0, 0))] + [ANY] * n_cin,
        out_specs=[cur(0), pl.BlockSpec((rows, 2 * ATT_WIDTH), lambda s: (jnp.clip(s - lag, 0, last), 0)),
                   pl.BlockSpec(dbias_shape, lambda s: (0, 0, 0))] + [ANY] * n_cout,
        out_shape=[jax.ShapeDtypeStruct((T, ATT_WIDTH), BF16), jax.ShapeDtypeStruct((T, 2 * ATT_WIDTH), BF16),
                   jax.ShapeDtypeStruct(dbias_shape, F32)] + (comm.out_shapes if comm else []),
        scratch_shapes=[pltpu.VMEM((stride, BLK, ATT_WIDTH), F32), pltpu.VMEM((stride, BLK, ATT_WIDTH), F32)]
        + (comm.scratch if comm else []),
        compiler_params=_params(dimension_semantics=("arbitrary",)),
    )(qkv, qkv, qkv, qkv, qkv, do, o, lse, bias, *(comm.inputs if comm else []))
    return res[0], res[1], res[2], list(res[3:])


REORDER_TILE = 256
REORDER_ROWS = 2048


def _reorder_matrix(d, inverse):
    per = REORDER_TILE // d
    p = np.zeros((REORDER_TILE, REORDER_TILE), np.float32)
    for src in range(REORDER_TILE):
        i, r = divmod(src, d)
        p[r * per + i, src] = 1.0
    return jnp.asarray(p.T if inverse else p, dtype=BF16)


def _reorder_rows(name, src, d, inverse, *, src_col=0, col_stride=1, ncols=1, dst=None, dst_col=0, dst_stride=1,
                  dst_blocks=None):
    T = src.shape[0]
    dtype = src.dtype
    span = BLK * d
    rows = max(span, min(T, REORDER_ROWS))
    per = REORDER_TILE // d
    tiles = span // REORDER_TILE
    dst_blocks = ncols if dst_blocks is None else dst_blocks

    def apply(p, x):
        return _dot(p, x).astype(BF16) if dtype == BF16 else _permute_f32(p, x)

    def body(*refs):
        p_ref, x_ref, o_ref = refs[0], refs[1], refs[-1]
        if d == 1:
            o_ref[...] = x_ref[...]
            return
        for s in range(rows // span):
            for t in range(tiles):
                base = s * span
                tile_rows = slice(base + t * REORDER_TILE, base + (t + 1) * REORDER_TILE)
                chunk = lambda r: slice(base + r * BLK + t * per, base + r * BLK + (t + 1) * per)
                if inverse:
                    gathered = jnp.concatenate([x_ref[chunk(r), :] for r in range(d)], axis=0)
                    o_ref[tile_rows, :] = apply(p_ref[...], gathered)
                else:
                    y = apply(p_ref[...], x_ref[tile_rows, :])
                    for r in range(d):
                        o_ref[chunk(r), :] = y[r * per:(r + 1) * per]

    in_specs = [pl.BlockSpec((REORDER_TILE, REORDER_TILE), lambda w, k: (0, 0)),
                pl.BlockSpec((rows, ATT_WIDTH), lambda w, k: (w, src_col + col_stride * k))]
    operands = [_reorder_matrix(max(d, 2), inverse), src]
    aliases = {}
    if dst is not None:
        in_specs.append(ANY)
        operands.append(dst)
        aliases = {2: 0}
    return pl.pallas_call(
        body, name=name, grid=(T // rows, ncols), in_specs=in_specs,
        out_specs=pl.BlockSpec((rows, ATT_WIDTH), lambda w, k: (w, dst_col + dst_stride * k)),
        out_shape=jax.ShapeDtypeStruct((T, dst_blocks * ATT_WIDTH), dtype),
        input_output_aliases=aliases,
        compiler_params=_params(dimension_semantics=("parallel", "parallel")),
    )(*operands)


def _group_qkv(qkv, g, d):
    NG = len(DILATIONS)
    if d == 1:
        return qkv, (g, NG + g, 2 * NG + g)
    return _reorder_rows(f"qkv_to_residues{g}", qkv, d, False, src_col=g, col_stride=NG, ncols=3), (0, 1, 2)


def _attention_fwd(qkv, bias):
    T = qkv.shape[0]
    parts = []
    for g, d in enumerate(DILATIONS):
        src, (qc, kc, vc) = _group_qkv(qkv, g, d)
        parts.append(_attn_fwd(f"attn_fwd_{g}", g, src, qc, kc, vc, bias, d))
    return _attn_merge(parts)


def _attention_bwd(qkv, do, o, lse, bias, comms):
    NG = len(DILATIONS)
    dqkv, dbs, carried = None, [], []
    for g, d in enumerate(DILATIONS):
        src, (qc, kc, vc) = _group_qkv(qkv, g, d)
        do_g, o_g, lse_g = do, o, lse
        if d > 1:
            do_g = _reorder_rows(f"do_to_residues{g}", do, d, False)
            o_g = _reorder_rows(f"o_to_residues{g}", o, d, False)
            lse_g = _reorder_rows(f"lse_to_residues{g}", lse, d, False)
        dq, dkv, db, sent = _attn_bwd(f"attn_bwd_{g}", g, src, qc, kc, vc, do_g, o_g, lse_g, bias, d, comm=comms[g])
        dqkv = _reorder_rows(f"dq_to_positions{g}", dq, d, True, dst=dqkv, dst_col=g, dst_blocks=3 * NG)
        dqkv = _reorder_rows(f"dkv_to_positions{g}", dkv, d, True, ncols=2, dst=dqkv, dst_col=NG + g, dst_stride=NG,
                             dst_blocks=3 * NG)
        dbs.append(db)
        carried.append(sent)
    return dqkv, jnp.stack(dbs), carried


def _other_chips(x, y):
    return [(1 - x, y), (x, 1 - y), (1 - x, 1 - y)]


def _shard_region(ref, shape, by_cols, chip, rows=None):
    R, C = shape
    start, size = (0, R) if rows is None else rows
    if by_cols:
        return ref.at[pl.ds(start, size), pl.ds(chip * C, C)]
    return ref.at[pl.ds(chip * R + start, size), :]


def _gather_weights(entries):
    n = len(entries)
    shapes = [e[0].shape[1:] for e in entries]

    def places(ins, outs, sems):
        send_sems, recv_sems, local_sems = sems
        x, y, c = lax.axis_index("x"), lax.axis_index("y"), lax.axis_index("c")

        def landing(f, px, py, pc):
            R = shapes[f][0]
            return _shard_region(outs[f], shapes[f], entries[f][2], 2 * px + py, rows=(pc * (R // 2), R // 2))

        def copy(f, k, block, to, src=None):
            dst = landing(f, *block)
            return pltpu.make_async_remote_copy(
                src_ref=dst if src is None else src, dst_ref=dst,
                send_sem=send_sems.at[6 * f + k], recv_sem=recv_sems.at[6 * f + k],
                device_id=to, device_id_type=MESH)

        def mine(f):
            dst = _shard_region(outs[f], shapes[f], entries[f][2], 2 * x + y)
            return pltpu.make_async_copy(ins[f].at[entries[f][1]], dst, local_sems.at[f])

        def first(f, j):
            R = shapes[f][0]
            src = ins[f].at[entries[f][1], pl.ds(c * (R // 2), R // 2), :]
            return copy(f, j, (x, y, c), (*_other_chips(x, y)[j], c), src=src)

        return x, y, c, copy, mine, first

    def start(ins, outs, sems):
        _, _, _, _, mine, first = places(ins, outs, sems)
        for f in range(n):
            mine(f).start()
        for j in range(3):
            for f in range(n):
                first(f, j).start()

    def mid(ins, outs, sems):
        x, y, c, copy, _, _ = places(ins, outs, sems)
        for j, chip in enumerate(_other_chips(x, y)):
            for f in range(n):
                copy(f, j, (*chip, c), (x, y, c)).wait_recv()
                copy(f, 3 + j, (*chip, c), (x, y, 1 - c)).start()

    def end(ins, outs, sems):
        x, y, c, copy, mine, first = places(ins, outs, sems)
        for j, chip in enumerate(_other_chips(x, y)):
            for f in range(n):
                copy(f, 3 + j, (*chip, 1 - c), (x, y, c)).wait_recv()
        for j, chip in enumerate(_other_chips(x, y)):
            for f in range(n):
                first(f, j).wait_send()
                copy(f, 3 + j, (*chip, c), (x, y, 1 - c)).wait_send()
        for f in range(n):
            mine(f).wait()

    def whole(f):
        R, C = shapes[f]
        return (R, N_CHIPS * C) if entries[f][2] else (N_CHIPS * R, C)

    return _Comm(
        [e[0] for e in entries], [jax.ShapeDtypeStruct(whole(f), BF16) for f in range(n)],
        [pltpu.SemaphoreType.DMA((6 * n,)), pltpu.SemaphoreType.DMA((6 * n,)), pltpu.SemaphoreType.DMA((n,))],
        start, end, mid)


def _scatter_grads(entries):
    n = len(entries)

    def copies(ins, outs, sems):
        send_sems, recv_sems, local_sems = sems
        x, y, c = lax.axis_index("x"), lax.axis_index("y"), lax.axis_index("c")
        me = 2 * x + y

        def piece(f, chip):
            return _shard_region(ins[f], entries[f][1], entries[f][2], chip)

        mine = [pltpu.make_async_copy(piece(f, me), outs[f].at[me], local_sems.at[f]) for f in range(n)]
        sends = [pltpu.make_async_remote_copy(
            src_ref=piece(f, 2 * px + py), dst_ref=outs[f].at[me],
            send_sem=send_sems.at[3 * f + j], recv_sem=recv_sems.at[3 * f + j],
            device_id=(px, py, c), device_id_type=MESH)
            for j, (px, py) in enumerate(_other_chips(x, y)) for f in range(n)]
        return mine, sends

    def start(ins, outs, sems):
        mine, sends = copies(ins, outs, sems)
        for cp in mine + sends:
            cp.start()

    def end(ins, outs, sems):
        mine, sends = copies(ins, outs, sems)
        for cp in sends + mine:
            cp.wait()

    return _Comm(
        [e[0] for e in entries], [jax.ShapeDtypeStruct((N_CHIPS,) + tuple(e[1]), BF16) for e in entries],
        [pltpu.SemaphoreType.DMA((3 * n,)), pltpu.SemaphoreType.DMA((3 * n,)), pltpu.SemaphoreType.DMA((n,))],
        start, end)


def _exchange_sibling(parts):
    n = len(parts)

    def copies(ins, outs, sems):
        send_sems, recv_sems = sems
        sibling = (lax.axis_index("x"), lax.axis_index("y"), 1 - lax.axis_index("c"))
        return [pltpu.make_async_remote_copy(src_ref=ins[i], dst_ref=outs[i], send_sem=send_sems.at[i],
                                             recv_sem=recv_sems.at[i], device_id=sibling, device_id_type=MESH)
                for i in range(n)]

    def start(ins, outs, sems):
        for cp in copies(ins, outs, sems):
            cp.start()

    def end(ins, outs, sems):
        for cp in copies(ins, outs, sems):
            cp.wait()

    return _Comm(parts, [jax.ShapeDtypeStruct(s.shape, s.dtype) for s in parts],
                 [pltpu.SemaphoreType.DMA((n,)), pltpu.SemaphoreType.DMA((n,))], start, end)


def _allgather_small(block):
    m_per, ncol = block.shape

    def places(ins, outs, sems):
        send_sems, recv_sems, local_sem = sems
        x, y, c = lax.axis_index("x"), lax.axis_index("y"), lax.axis_index("c")

        def rows(px, py, pc):
            return outs[0].at[4 * px + 2 * py + pc]

        def copy(k, block_of, to, src=None):
            return pltpu.make_async_remote_copy(
                src_ref=rows(*block_of) if src is None else src, dst_ref=rows(*block_of),
                send_sem=send_sems.at[k], recv_sem=recv_sems.at[k], device_id=to, device_id_type=MESH)

        mine = pltpu.make_async_copy(ins[0], rows(x, y, c), local_sem.at[0])
        first = [copy(0, (x, y, c), (x, y, 1 - c), src=ins[0])]
        first += [copy(1 + j, (x, y, c), (*chip, c), src=ins[0]) for j, chip in enumerate(_other_chips(x, y))]
        passed = [copy(4 + j, (*chip, c), (x, y, 1 - c)) for j, chip in enumerate(_other_chips(x, y))]
        return x, y, c, copy, mine, first, passed

    def start(ins, outs, sems):
        _, _, _, _, mine, first, _ = places(ins, outs, sems)
        for cp in [mine] + first:
            cp.start()

    def mid(ins, outs, sems):
        x, y, c, copy, _, _, passed = places(ins, outs, sems)
        for j, chip in enumerate(_other_chips(x, y)):
            copy(1 + j, (*chip, c), (x, y, c)).wait_recv()
            passed[j].start()

    def end(ins, outs, sems):
        x, y, c, copy, mine, first, passed = places(ins, outs, sems)
        copy(0, (x, y, 1 - c), (x, y, c)).wait_recv()
        for j, chip in enumerate(_other_chips(x, y)):
            copy(4 + j, (*chip, 1 - c), (x, y, c)).wait_recv()
        for cp in first + passed:
            cp.wait_send()
        mine.wait()

    return _Comm([block], [jax.ShapeDtypeStruct((N_DEV, m_per, ncol), block.dtype)],
                 [pltpu.SemaphoreType.DMA((7,)), pltpu.SemaphoreType.DMA((7,)), pltpu.SemaphoreType.DMA((1,))],
                 start, end, mid)


def _join_comms(*progs):
    def split(parts, counts):
        out, pos = [], 0
        for n in counts:
            out.append(parts[pos:pos + n])
            pos += n
        return out

    def phase(which):
        def run(ins, outs, sems):
            args = zip(split(ins, [len(p.inputs) for p in progs]), split(outs, [len(p.out_shapes) for p in progs]),
                       split(sems, [len(p.scratch) for p in progs]))
            for p, (i, o, s) in zip(progs, args):
                fn = getattr(p, which)
                if fn is not None:
                    fn(i, o, s)
        return run

    return _Comm([a for p in progs for a in p.inputs], [s for p in progs for s in p.out_shapes],
                 [s for p in progs for s in p.scratch], phase("start"), phase("end"), phase("mid"))


def _adamw(w, g, m, v):
    m = ADAM_B1 * m + (1.0 - ADAM_B1) * g
    v = ADAM_B2 * v + (1.0 - ADAM_B2) * jnp.square(g)
    m_hat = m / (1.0 - ADAM_B1 ** ADAM_STEP)
    v_hat = v / (1.0 - ADAM_B2 ** ADAM_STEP)
    delta = -ADAM_LR * (m_hat / (jnp.sqrt(v_hat) + ADAM_EPS) + ADAM_WD * w)
    return delta, m, v


def _flat_tile(rows):
    return min(rows, 512)


def _sum_pieces(name, layers):
    L = len(layers)
    P, R, C = layers[0].shape
    tr = _flat_tile(R)

    def body(*refs):
        out_ref = refs[L]
        for l in range(L):
            @pl.when(pl.program_id(0) == l)
            def _(p_ref=refs[l]):
                acc = p_ref[0].astype(F32)
                for j in range(1, P):
                    acc = acc + p_ref[j].astype(F32)
                out_ref[...] = acc

    return pl.pallas_call(
        body, name=name, grid=(L, R // tr),
        in_specs=[pl.BlockSpec((P, tr, C), lambda l, i: (0, i, 0)) for _ in range(L)],
        out_specs=pl.BlockSpec((None, tr, C), lambda l, i: (l, i, 0)),
        out_shape=jax.ShapeDtypeStruct((L, R, C), F32),
        compiler_params=_params(dimension_semantics=("parallel", "parallel")),
    )(*layers)


def _adam_pair(name, w, m, v, part_a, part_b):
    L, R, C = w.shape
    tr = _flat_tile(R)

    def body(w_ref, m_ref, v_ref, a_ref, b_ref, g_ref, d_ref, nm_ref, nv_ref):
        g = a_ref[...] + b_ref[...]
        g_ref[...] = g
        d_ref[...], nm_ref[...], nv_ref[...] = _adamw(w_ref[...], g, m_ref[...], v_ref[...])

    row = pl.BlockSpec((None, tr, C), lambda l, i: (l, i, 0))
    return pl.pallas_call(
        body, name=name, grid=(L, R // tr),
        in_specs=[row] * 5, out_specs=[row] * 4,
        out_shape=[jax.ShapeDtypeStruct((L, R, C), F32)] * 4,
        compiler_params=_params(dimension_semantics=("parallel", "parallel")),
    )(w, m, v, part_a, part_b)


def _adam_small(w, m, v, gathered):
    R, C = w.shape

    def body(w_ref, m_ref, v_ref, p_ref, g_ref, d_ref, nm_ref, nv_ref):
        g = p_ref[0]
        for j in range(1, N_DEV):
            g = g + p_ref[j]
        g_ref[...] = g
        d_ref[...], nm_ref[...], nv_ref[...] = _adamw(w_ref[...], g, m_ref[...], v_ref[...])

    return pl.pallas_call(
        body, name="adam_small",
        out_shape=[jax.ShapeDtypeStruct((R, C), F32)] * 4,
        compiler_params=_params(),
    )(w, m, v, gathered)


SMALL = ("mix_norm_g", "mlp_norm_g", "final_norm_g", "a_ln_g", "a_ln_b", "a_w_s", "a_b_s", "rel_bias")


def _pack_small(arrays, width):
    rows = []
    for a in arrays:
        flat = a.reshape(-1)
        pad = (-flat.shape[0]) % width
        rows.append(jnp.pad(flat, (0, pad)).reshape(-1, width))
    block = jnp.concatenate(rows, axis=0)
    return jnp.pad(block, ((0, (-block.shape[0]) % 8), (0, 0)))


def _unpack_small(block, shapes, width):
    out, row = [], 0
    for shape in shapes:
        size = int(np.prod(shape))
        nrows = -(-size // width)
        out.append(block[row:row + nrows].reshape(-1)[:size].reshape(shape))
        row += nrows
    return out


def kernel(x, mix_norm_g, mlp_norm_g, final_norm_g, a_w_in, a_ln_g, a_ln_b, a_w_s, a_b_s, a_w_out, b_w_qkv, b_w_out, rel_bias, w_up, w_down, loss_target, m_mix_norm_g, m_mlp_norm_g, m_final_norm_g, m_a_w_in, m_a_ln_g, m_a_ln_b, m_a_w_s, m_a_b_s, m_a_w_out, m_b_w_qkv, m_b_w_out, m_rel_bias, m_w_up, m_w_down, v_mix_norm_g, v_mlp_norm_g, v_final_norm_g, v_a_w_in, v_a_ln_g, v_a_ln_b, v_a_w_s, v_a_b_s, v_a_w_out, v_b_w_qkv, v_b_w_out, v_rel_bias, v_w_up, v_w_down):
    T, D = x.shape[1], x.shape[2]
    h0 = x.reshape(T, D)
    target = loss_target.reshape(T, D)
    G = a_w_s.shape[1]

    w_big = [a_w_in, a_w_out, b_w_qkv, b_w_out, w_up, w_down]
    m_big = [m_a_w_in, m_a_w_out, m_b_w_qkv, m_b_w_out, m_w_up, m_w_down]
    v_big = [v_a_w_in, v_a_w_out, v_b_w_qkv, v_b_w_out, v_w_up, v_w_down]
    by_cols = [True, False, True, True, True, False]
    s_in, s_out, s_qkv, s_bo, s_up, s_dn = [w.astype(BF16) for w in w_big]
    W_in, W_out = _run_comm("gather_a", _gather_weights([(s_in, 0, True), (s_out, 0, False)]))

    tril = jnp.tril(jnp.ones((CHUNK, CHUNK), dtype=bool))
    w_tril = jnp.where(tril[None], a_w_s[0], 0.0).astype(BF16)
    w_tril_t = jnp.swapaxes(w_tril, 1, 2)
    b_rows = jnp.broadcast_to(a_b_s[0][:, :, None], (G, CHUNK, CHUNK))
    buckets = _bucket_maps()
    bias = _bias_build(rel_bias, buckets)

    QKV = s_qkv.shape[2] * N_CHIPS
    TM = 1024
    TK_WGRAD = 4096

    def matmul(name, a, b, mode, out, tm=TM, tn=1024, **kw):
        outs = out if isinstance(out, list) else [out]
        return _mm(name, a, b, mode, tm=tm, tn=tn, tk=a.shape[1], outs=outs, **kw)

    def norm_bwd(layer_gain, h, dres, copies=2):
        return dict(epi=_epi_rms_bwd(copies), extras=(h, dres), vecs=(layer_gain,), col_sums=1)

    def wgrad(name, a, b, tn=1024, tk=TK_WGRAD, comm=None):
        return _mm(name, a, b, "tn", tm=1024, tn=tn, tk=tk, outs=[BF16], comm=comm)

    def scatter(*which):
        return _scatter_grads([(g, w_big[i].shape[1:], by_cols[i]) for g, i in which])

    (a_pre, y0), (W_up0,) = matmul("a_in", h0, W_in, "nn", BF16, norm_gain=mix_norm_g[0:1],
                                   comm=_gather_weights([(s_up, 0, True)]))
    z = _gate_fwd(a_pre, a_ln_g, a_ln_b, w_tril, b_rows)
    h1 = matmul("a_out", z, W_out, "nn", F32, epi=_epi_residual, extras=(h0,))
    (q1, y1), (W_dn0,) = matmul("mlp_up0", h1, W_up0, "nn", BF16, epi=_epi_relu2, norm_gain=mlp_norm_g[0:1],
                                comm=_gather_weights([(s_dn, 0, False)]))
    h2, (W_qkv, W_bo) = matmul("mlp_down0", q1, W_dn0, "nn", F32, tm=TM // 2, epi=_epi_residual, extras=(h1,),
                               comm=_gather_weights([(s_qkv, 0, True), (s_bo, 0, True)]))
    (qkv, y2), (W_up1,) = matmul("b_qkv", h2, W_qkv, "nn", BF16, tn=QKV // 4, norm_gain=mix_norm_g[1:2],
                                 comm=_gather_weights([(s_up, 1, True)]))
    o, lse = _attention_fwd(qkv, bias)
    h3 = matmul("b_out", o, W_bo, "nn", F32, epi=_epi_residual, extras=(h2,))
    (q3, y3), (W_dn1,) = matmul("mlp_up1", h3, W_up1, "nn", BF16, epi=_epi_relu2, norm_gain=mlp_norm_g[1:2],
                                comm=_gather_weights([(s_dn, 1, False)]))
    dh4, dh4_b, d_final_g, loss_row = matmul("mlp_down1", q3, W_dn1, "nn", [F32, BF16], tm=TM // 2, epi=_epi_loss_head,
                                             extras=(h3, target), vecs=(final_norm_g.reshape(1, D),), col_sums=2)

    dp3 = matmul("mlp_down_bwd1", dh4_b, W_dn1, "nt", BF16, epi=_epi_relu2_grad, extras=(q3,))
    g_dn1 = wgrad("mlp_down_wgrad1", q3, dh4_b)
    g_up1 = wgrad("mlp_up_wgrad1", y3, dp3)
    dh3, dh3_b, dg_mlp1 = matmul("mlp_up_bwd1", dp3, W_up1, "nt", [F32, BF16], tm=TM // 2,
                                 **norm_bwd(mlp_norm_g[1:2], h3, dh4))
    do = matmul("b_out_bwd", dh3_b, W_bo, "nt", BF16)
    g_bo = wgrad("b_out_wgrad", o, dh3_b)
    dqkv, dbias, ((r_dn1,), (r_up1,), (r_bo,)) = _attention_bwd(
        qkv, do, o, lse, bias, [scatter((g_dn1, 5)), scatter((g_up1, 4)), scatter((g_bo, 3))])
    d_rel_bias = _bias_scatter(dbias, buckets)
    dh2, dh2_b, dg_mix1 = matmul("b_qkv_bwd", dqkv, W_qkv, "nt", [F32, BF16], tm=TM // 2,
                                 **norm_bwd(mix_norm_g[1:2], h2, dh3))
    g_qkv = wgrad("b_qkv_wgrad", y2, dqkv, tn=QKV // 3, tk=TK_WGRAD // 2)
    dp1, (r_qkv,) = matmul("mlp_down_bwd0", dh2_b, W_dn0, "nt", BF16, epi=_epi_relu2_grad, extras=(q1,),
                           comm=scatter((g_qkv, 2)))
    g_up0 = wgrad("mlp_up_wgrad0", y1, dp1)
    g_dn0, (r_up0,) = wgrad("mlp_down_wgrad0", q1, dh2_b, comm=scatter((g_up0, 4)))
    (dh1, dh1_b, dg_mlp0), (r_dn0,) = matmul("mlp_up_bwd0", dp1, W_up0, "nt", [F32, BF16], tm=TM // 2,
                                             comm=scatter((g_dn0, 5)), **norm_bwd(mlp_norm_g[0:1], h1, dh2))
    dz = matmul("a_out_bwd", dh1_b, W_out, "nt", F32)
    g_out = wgrad("a_out_wgrad", z, dh1_b)
    da, d_ln_g, d_ln_b, d_w_s, d_b_s = _gate_bwd(a_pre, dz, a_ln_g, a_ln_b, w_tril, w_tril_t, b_rows)
    received = [None, None, [r_qkv], [r_bo], [r_up0, r_up1], [r_dn0, r_dn1]]
    plane = [None, None] + [_sum_pieces(f"sum_pieces{i}", received[i]) for i in range(2, len(w_big))]
    g_in, carried = wgrad("a_in_wgrad", y0, da, comm=_join_comms(scatter((g_out, 1)), _exchange_sibling(plane[2:])))
    r_out, other = carried[0], [None, None] + list(carried[1:])
    grad_x, dg_mix0 = matmul("a_in_bwd", da, W_in, "nt", F32, **norm_bwd(mix_norm_g[0:1], h0, dh1, copies=1))

    unused = jnp.zeros((1, 1), F32)
    small_w = [mix_norm_g, mlp_norm_g, final_norm_g, a_ln_g, a_ln_b, a_w_s, a_b_s, rel_bias, unused]
    small_m = [m_mix_norm_g, m_mlp_norm_g, m_final_norm_g, m_a_ln_g, m_a_ln_b, m_a_w_s, m_a_b_s, m_rel_bias, unused]
    small_v = [v_mix_norm_g, v_mlp_norm_g, v_final_norm_g, v_a_ln_g, v_a_ln_b, v_a_w_s, v_a_b_s, v_rel_bias, unused]
    small_g = [jnp.concatenate([dg_mix0, dg_mix1]), jnp.concatenate([dg_mlp0, dg_mlp1]), d_final_g,
               d_ln_g, d_ln_b, d_w_s[None], d_b_s[None, :, :, 0], d_rel_bias, loss_row[:, :1]]
    width = max(D, 128)
    plane[1] = _sum_pieces("sum_pieces1", [r_out])
    gathered_small, r_in, other[1] = _run_comm("tail_comm", _join_comms(
        _allgather_small(_pack_small(small_g, width)), scatter((g_in, 0)), _exchange_sibling([plane[1]])))
    plane[0] = _sum_pieces("sum_pieces0", [r_in])
    (other[0],) = _run_comm("exchange_a_in", _exchange_sibling([plane[0]]))
    big_out = [_adam_pair(f"adam{i}", w_big[i], m_big[i], v_big[i], plane[i], other[i]) for i in range(len(w_big))]

    def unbig(kind):
        return dict(zip(["a_w_in", "a_w_out", "b_w_qkv", "b_w_out", "w_up", "w_down"], [b[kind] for b in big_out]))

    small_out = _adam_small(_pack_small(small_w, width), _pack_small(small_m, width), _pack_small(small_v, width),
                            gathered_small)
    shapes = [w.shape for w in small_w]
    loss = _unpack_small(small_out[0], shapes, width)[-1][0, 0]

    names = ["mix_norm_g", "mlp_norm_g", "final_norm_g", "a_w_in", "a_ln_g", "a_ln_b", "a_w_s", "a_b_s", "a_w_out",
             "b_w_qkv", "b_w_out", "rel_bias", "w_up", "w_down"]
    results = [loss, grad_x.reshape(x.shape)]
    for kind in range(4):
        table = dict(zip(SMALL, _unpack_small(small_out[kind], shapes, width)))
        table.update(unbig(kind))
        results += [table[n] for n in names]
    return tuple(results)
```
